```python
import math
import jax, jax.numpy as jnp
from jax import lax
import numpy as np


D_MODEL = 1024
BATCH = 8
SEQ = 4096
DEPTH = 1

D_MIX = 2 * D_MODEL
GMLP_WIDTH = D_MIX // 2
GMLP_HEAD_DIM = 128
GMLP_HEADS = GMLP_WIDTH // GMLP_HEAD_DIM
GMLP_CHUNK = 128
SSD_WIDTH = D_MIX - GMLP_WIDTH
SSD_HEAD_DIM = 64
SSD_HEADS = SSD_WIDTH // SSD_HEAD_DIM
SSD_GROUPS = 2
SSD_STATE = 128
SSD_CHUNK = 128
CONV_WIDTH = 5
CONV_CH = SSD_WIDTH + 2 * SSD_GROUPS * SSD_STATE
DT_MIN = 1e-3
DT_MAX = 1e-1
IN_COLS = 2 * GMLP_WIDTH + SSD_WIDTH + CONV_CH + 2 * SSD_HEADS
IN_SPLITS = [2 * GMLP_WIDTH, 2 * GMLP_WIDTH + SSD_WIDTH, 2 * GMLP_WIDTH + SSD_WIDTH + CONV_CH]
N_EGROUPS = 4
EXPERTS_PER_GROUP = 8
N_EXPERTS = N_EGROUPS * EXPERTS_PER_GROUP
TOP_K_IN_GROUP = 2
D_EXPERT = D_MODEL // 4
N_MOD = 6
EPS = 1e-6

kernel_name = 'hymba_gmlp_ssd_hiermoe_adaln_encoder'


def rmsnorm(x, g):
    xf = x.astype(jnp.float32)
    y = xf * lax.rsqrt(jnp.mean(xf * xf, axis=-1, keepdims=True) + EPS)
    return y.astype(x.dtype) * g


def layernorm(x, g, b):
    xf = x.astype(jnp.float32)
    mu = jnp.mean(xf, axis=-1, keepdims=True)
    var = jnp.mean(jnp.square(xf - mu), axis=-1, keepdims=True)
    y = (xf - mu) * lax.rsqrt(var + EPS)
    return y.astype(x.dtype) * g + b


def gmlp_mixer(zg, ln_g, ln_b, w_s, b_s, out_g):
    bn, s, _ = zg.shape
    zg = jax.nn.gelu(zg, approximate=False)
    u, v = jnp.split(zg, 2, axis=-1)
    v = layernorm(v, ln_g, ln_b)
    nc = s // GMLP_CHUNK
    v = v.reshape(bn, nc, GMLP_CHUNK, GMLP_HEADS, GMLP_HEAD_DIM)
    mixed = jnp.einsum('hij,bnjhd->bnihd', w_s, v) + b_s.T[:, :, None]
    out = u * mixed.reshape(bn, s, GMLP_WIDTH)
    return rmsnorm(out, out_g)


def ssd_chunked(x, dt, a, bm, cm):
    b, s, h, p = x.shape
    g, n = bm.shape[2], bm.shape[3]
    r = h // g
    nc = s // SSD_CHUNK
    xr = (x * dt[..., None]).reshape(b, nc, SSD_CHUNK, g, r, p)
    da = (dt * a).reshape(b, nc, SSD_CHUNK, g, r)
    bc = bm.reshape(b, nc, SSD_CHUNK, g, n)
    cc = cm.reshape(b, nc, SSD_CHUNK, g, n)
    a_cs = jnp.cumsum(jnp.moveaxis(da, 2, -1), axis=-1)
    lower = jnp.tril(jnp.ones((SSD_CHUNK, SSD_CHUNK), dtype=bool))
    seg = a_cs[..., :, None] - a_cs[..., None, :]
    decay_in = jnp.exp(jnp.where(lower, seg, -jnp.inf))
    scores = jnp.einsum('bclgn,bcsgn->bcgls', cc, bc)
    y_diag = jnp.einsum('bcgls,bcgrls,bcsgrp->bclgrp', scores, decay_in, xr)
    decay_to_end = jnp.exp(a_cs[..., -1:] - a_cs)
    states = jnp.einsum('bclgn,bcgrl,bclgrp->bcgrpn', bc, decay_to_end, xr)
    chunk_decay = jnp.exp(a_cs[..., -1])

    def step(carry, inp):
        st, dec = inp
        return carry * dec[..., None, None] + st, carry

    init = jnp.zeros_like(states[:, 0])
    _, prev = lax.scan(step, init, (jnp.moveaxis(states, 1, 0), jnp.moveaxis(chunk_decay, 1, 0)))
    prev = jnp.moveaxis(prev, 0, 1)
    y_off = jnp.einsum('bclgn,bcgrpn,bcgrl->bclgrp', cc, prev, jnp.exp(a_cs))
    return (y_diag + y_off).reshape(b, s, h, p)


def ssd_mixer(z, xbc, dt_raw, conv_w, conv_b, a_log_f, a_log_b, dt_bias_f, dt_bias_b, d_skip, norm_g):
    bn, s, _ = xbc.shape
    f32 = jnp.float32
    xbc = lax.conv_general_dilated(xbc, conv_w[:, None, :], window_strides=(1,),
                                   padding=[(CONV_WIDTH // 2, CONV_WIDTH // 2)],
                                   dimension_numbers=('NWC', 'WIO', 'NWC'),
                                   feature_group_count=CONV_CH)
    xbc = jax.nn.silu(xbc + conv_b).astype(f32)
    xs, bm, cm = jnp.split(xbc, [SSD_WIDTH, SSD_WIDTH + SSD_GROUPS * SSD_STATE], axis=-1)
    xs = xs.reshape(bn, s, SSD_HEADS, SSD_HEAD_DIM)
    bm = bm.reshape(bn, s, SSD_GROUPS, SSD_STATE)
    cm = cm.reshape(bn, s, SSD_GROUPS, SSD_STATE)
    dt_f, dt_b = jnp.split(dt_raw.astype(f32), 2, axis=-1)
    dt_f = jax.nn.softplus(dt_f + dt_bias_f.astype(f32))
    dt_b = jax.nn.softplus(dt_b + dt_bias_b.astype(f32))
    a_f = -jnp.exp(a_log_f.astype(f32))
    a_b = -jnp.exp(a_log_b.astype(f32))
    y_f = ssd_chunked(xs, dt_f, a_f, bm, cm)
    y_b = jnp.flip(ssd_chunked(jnp.flip(xs, 1), jnp.flip(dt_b, 1), a_b,
                               jnp.flip(bm, 1), jnp.flip(cm, 1)), 1)
    y = y_f + y_b + d_skip.astype(f32)[:, None] * xs
    y = y.reshape(bn, s, SSD_WIDTH) * jax.nn.silu(z.astype(f32))
    y = y.reshape(bn, s, SSD_GROUPS, SSD_WIDTH // SSD_GROUPS)
    y = y * lax.rsqrt(jnp.mean(y * y, axis=-1, keepdims=True) + EPS)
    return y.reshape(bn, s, SSD_WIDTH).astype(z.dtype) * norm_g


def hier_moe(h, w_rg, b_rg, w_re, b_re, w1, w3, w2):
    bn, s, d = h.shape
    t = h.reshape(-1, d)
    f32 = jnp.float32
    g_logits = (t @ w_rg).astype(f32) + b_rg.astype(f32)
    g_prob = jax.nn.softmax(g_logits, axis=-1)
    g_idx = jnp.argmax(g_logits, axis=-1)
    p_g = jnp.max(g_prob, axis=-1, keepdims=True)
    e_logits = jnp.einsum('td,gde->tge', t, w_re).astype(f32) + b_re.astype(f32)
    e_sel = jnp.einsum('tge,tg->te', e_logits, jax.nn.one_hot(g_idx, N_EGROUPS, dtype=f32))
    top_v, top_i = lax.top_k(e_sel, TOP_K_IN_GROUP)
    w_k = jax.nn.softmax(top_v, axis=-1) * p_g
    eid = g_idx[:, None] * EXPERTS_PER_GROUP + top_i
    comb = jnp.sum(jax.nn.one_hot(eid, N_EXPERTS, dtype=f32) * w_k[..., None], axis=1)
    comb = comb.astype(t.dtype)
    out = jnp.zeros_like(t)
    for e in range(N_EXPERTS):
        a = jax.nn.silu(t @ w1[e]) * (t @ w3[e])
        out = out + comb[:, e:e + 1] * (a @ w2[e])
    return out.reshape(bn, s, d)


def setup_inputs(seed: int = 0) -> dict:
    key = jax.random.key(seed)
    ks = jax.random.split(key, 32)
    nrm = jax.random.normal
    L = DEPTH

    def gain(k, shape):
        return 1.0 + 0.01 * nrm(k, shape, jnp.float32)

    def small(k, shape):
        return 0.01 * nrm(k, shape, jnp.float32)

    dt = jnp.exp(jax.random.uniform(ks[14], (L, SSD_HEADS)) * (math.log(DT_MAX) - math.log(DT_MIN)) + math.log(DT_MIN))
    dt_bias_f = dt + jnp.log(-jnp.expm1(-dt))
    dt2 = jnp.exp(jax.random.uniform(ks[15], (L, SSD_HEADS)) * (math.log(DT_MAX) - math.log(DT_MIN)) + math.log(DT_MIN))
    dt_bias_b = dt2 + jnp.log(-jnp.expm1(-dt2))
    return {
        'x': nrm(ks[0], (BATCH, SEQ, D_MODEL), jnp.float32),
        'c': nrm(ks[1], (BATCH, D_MODEL), jnp.float32),
        'w_ada': nrm(ks[2], (L, D_MODEL, N_MOD * D_MODEL), jnp.float32) * D_MODEL ** -0.5,
        'b_ada': small(ks[3], (L, N_MOD * D_MODEL)),
        'norm1_g': gain(ks[4], (L, D_MODEL)),
        'w_in': nrm(ks[5], (L, D_MODEL, IN_COLS), jnp.float32) * D_MODEL ** -0.5,
        'b_in': small(ks[6], (L, IN_COLS)),
        'gmlp_ln_g': gain(ks[7], (L, GMLP_WIDTH)),
        'gmlp_ln_b': small(ks[8], (L, GMLP_WIDTH)),
        'gmlp_w_s': nrm(ks[9], (L, GMLP_HEADS, GMLP_CHUNK, GMLP_CHUNK), jnp.float32) * GMLP_CHUNK ** -0.5,
        'gmlp_b_s': gain(ks[10], (L, GMLP_HEADS, GMLP_CHUNK)),
        'gmlp_out_g': gain(ks[11], (L, GMLP_WIDTH)),
        'conv_w': nrm(ks[12], (L, CONV_WIDTH, CONV_CH), jnp.float32) * CONV_WIDTH ** -0.5,
        'conv_b': small(ks[13], (L, CONV_CH)),
        'a_log_f': jnp.log(jax.random.uniform(ks[16], (L, SSD_HEADS), jnp.float32, 1.0, 16.0)),
        'a_log_b': jnp.log(jax.random.uniform(ks[17], (L, SSD_HEADS), jnp.float32, 1.0, 16.0)),
        'dt_bias_f': dt_bias_f,
        'dt_bias_b': dt_bias_b,
        'd_skip': gain(ks[18], (L, SSD_HEADS)),
        'ssd_norm_g': gain(ks[19], (L, SSD_WIDTH)),
        'w_out': nrm(ks[20], (L, D_MIX, D_MODEL), jnp.float32) * D_MIX ** -0.5,
        'norm2_g': gain(ks[21], (L, D_MODEL)),
        'w_router_g': nrm(ks[22], (L, D_MODEL, N_EGROUPS), jnp.float32) * D_MODEL ** -0.5,
        'b_router_g': small(ks[23], (L, N_EGROUPS)),
        'w_router_e': nrm(ks[24], (L, N_EGROUPS, D_MODEL, EXPERTS_PER_GROUP), jnp.float32) * D_MODEL ** -0.5,
        'b_router_e': small(ks[25], (L, N_EGROUPS, EXPERTS_PER_GROUP)),
        'w1': nrm(ks[26], (L, N_EXPERTS, D_MODEL, D_EXPERT), jnp.float32) * D_MODEL ** -0.5,
        'w3': nrm(ks[27], (L, N_EXPERTS, D_MODEL, D_EXPERT), jnp.float32) * D_MODEL ** -0.5,
        'w2': nrm(ks[28], (L, N_EXPERTS, D_EXPERT, D_MODEL), jnp.float32) * D_EXPERT ** -0.5,
        'final_g': gain(ks[29], (D_MODEL,)),
    }


def reference(x, c, w_ada, b_ada, norm1_g, w_in, b_in, gmlp_ln_g, gmlp_ln_b, gmlp_w_s, gmlp_b_s,
              gmlp_out_g, conv_w, conv_b, a_log_f, a_log_b, dt_bias_f, dt_bias_b, d_skip, ssd_norm_g,
              w_out, norm2_g, w_router_g, b_router_g, w_router_e, b_router_e, w1, w3, w2, final_g):
    c_act = jax.nn.silu(c)
    for l in range(DEPTH):
        mod = (c_act @ w_ada[l] + b_ada[l])[:, None, :]
        shift1, scale1, gate1, shift2, scale2, gate2 = jnp.split(mod, N_MOD, axis=-1)
        h = rmsnorm(x, norm1_g[l]) * (1.0 + scale1) + shift1
        proj = h @ w_in[l] + b_in[l]
        zg, z, xbc, dt_raw = jnp.split(proj, IN_SPLITS, axis=-1)
        y_a = gmlp_mixer(zg, gmlp_ln_g[l], gmlp_ln_b[l], gmlp_w_s[l], gmlp_b_s[l], gmlp_out_g[l])
        y_b = ssd_mixer(z, xbc, dt_raw, conv_w[l], conv_b[l], a_log_f[l], a_log_b[l],
                        dt_bias_f[l], dt_bias_b[l], d_skip[l], ssd_norm_g[l])
        mix = jnp.concatenate([y_a, y_b], axis=-1)
        x = x + gate1 * (mix @ w_out[l])
        h = rmsnorm(x, norm2_g[l]) * (1.0 + scale2) + shift2
        x = x + gate2 * hier_moe(h, w_router_g[l], b_router_g[l], w_router_e[l], b_router_e[l],
                                 w1[l], w3[l], w2[l])
    return rmsnorm(x, final_g)
```

```python
import functools
import math

import jax
import jax.numpy as jnp
from jax import lax
from jax.experimental import pallas as pl
from jax.experimental.pallas import tpu as pltpu

F32 = jnp.float32
BF16 = jnp.bfloat16
HIGHEST = lax.Precision.HIGHEST

D_MODEL = 1024
N_MOD = 6
GMLP_WIDTH = 1024
GMLP_HEADS = 8
CHUNK = 128
SSD_WIDTH = 1024
SSD_HEADS = 16
SSD_HEAD_DIM = 64
SSD_GROUPS = 2
SSD_STATE = 128
GROUP_WIDTH = SSD_WIDTH // SSD_GROUPS
CONV_WIDTH = 5
CONV_CH = SSD_WIDTH + 2 * SSD_GROUPS * SSD_STATE
N_EGROUPS = 4
EXPERTS_PER_GROUP = 8
N_EXPERTS = 32
D_EXPERT = 256
EPS = 1e-6

LANES = 128
SUBLANES = 8
COL_U, COL_V, COL_Z, COL_XBC, COL_DT = 0, 1024, 2048, 3072, 4608
IN_COLS = 4640
IN_COLS_PAD = COL_DT + LANES
TM_PROJ = 512
TM_MOE = 1024
VMEM_LIMIT = 56 * 1024 * 1024


def _silu(v):
    return v * jax.nn.sigmoid(v)


def _gelu(v):
    return 0.5 * v * (1.0 + lax.erf(v * math.sqrt(0.5)))


def _softplus(v):
    return jnp.maximum(v, 0.0) + jnp.log1p(jnp.exp(-jnp.abs(v)))


def _rms(v):
    return v * lax.rsqrt(jnp.mean(v * v, axis=-1, keepdims=True) + EPS)


def _mod_body(c_ref, w_ref, b_ref, o_ref):
    ca = _silu(c_ref[...])
    o_ref[...] = jnp.dot(ca, w_ref[...], precision=HIGHEST, preferred_element_type=F32) + b_ref[...]


def _modulation(c, w_ada, b_ada):
    bn = c.shape[0]
    return pl.pallas_call(
        _mod_body,
        grid=(N_MOD,),
        in_specs=[
            pl.BlockSpec((bn, D_MODEL), lambda j: (0, 0)),
            pl.BlockSpec((D_MODEL, D_MODEL), lambda j: (0, j)),
            pl.BlockSpec((1, D_MODEL), lambda j: (0, j)),
        ],
        out_specs=pl.BlockSpec((bn, D_MODEL), lambda j: (0, j)),
        out_shape=jax.ShapeDtypeStruct((bn, N_MOD * D_MODEL), F32),
        name="adaln_mod",
    )(c, w_ada, b_ada.reshape(1, -1))


def _inproj_body(x_ref, shift_ref, scale_ref, g_ref, w_ref, b_ref, lng_ref, lnb_ref, ws_ref, bs_ref,
                 og_ref, ya_ref, z_ref, xbc_ref, dt_ref, mix_scr):
    h = _rms(x_ref[...]) * g_ref[...]
    h = h * (1.0 + scale_ref[...]) + shift_ref[...]
    hb = h.astype(BF16)

    def proj(lo, hi):
        return jnp.dot(hb, w_ref[:, lo:hi], preferred_element_type=F32) + b_ref[:, lo:hi]

    z_ref[...] = proj(COL_Z, COL_XBC).astype(BF16)
    xbc_ref[...] = proj(COL_XBC, COL_DT)
    dt_ref[...] = proj(COL_DT, IN_COLS_PAD)

    v = _gelu(proj(COL_V, COL_Z))
    mu = jnp.mean(v, axis=-1, keepdims=True)
    vc = v - mu
    var = jnp.mean(vc * vc, axis=-1, keepdims=True)
    vn = (vc * lax.rsqrt(var + EPS) * lng_ref[...] + lnb_ref[...]).astype(BF16)
    n_chunks = TM_PROJ // CHUNK
    for hd in range(GMLP_HEADS):
        cols = slice(hd * LANES, (hd + 1) * LANES)
        rhs = jnp.concatenate([vn[c * CHUNK:(c + 1) * CHUNK, cols] for c in range(n_chunks)], axis=1)
        res = jnp.dot(ws_ref[hd], rhs, preferred_element_type=F32)
        for c in range(n_chunks):
            mix_scr[c * CHUNK:(c + 1) * CHUNK, cols] = res[:, c * LANES:(c + 1) * LANES]
    u = _gelu(proj(COL_U, COL_V))
    out = u * (mix_scr[...] + bs_ref[...])
    ya_ref[...] = (_rms(out) * og_ref[...]).astype(BF16)


def _inproj_gmlp(x2, mod3, norm1_g, w_in_p, b_in_p, ln_g, ln_b, w_s, bs_tile, out_g, seq):
    t = x2.shape[0]
    tiles_per_seq = seq // TM_PROJ
    row = lambda i: (i, 0)
    const2 = lambda i: (0, 0)
    return pl.pallas_call(
        _inproj_body,
        grid=(t // TM_PROJ,),
        in_specs=[
            pl.BlockSpec((TM_PROJ, D_MODEL), row),
            pl.BlockSpec((None, 1, D_MODEL), lambda i: (i // tiles_per_seq, 0, 0)),
            pl.BlockSpec((None, 1, D_MODEL), lambda i: (i // tiles_per_seq, 0, 1)),
            pl.BlockSpec((1, D_MODEL), const2),
            pl.BlockSpec((D_MODEL, IN_COLS_PAD), const2),
            pl.BlockSpec((1, IN_COLS_PAD), const2),
            pl.BlockSpec((1, GMLP_WIDTH), const2),
            pl.BlockSpec((1, GMLP_WIDTH), const2),
            pl.BlockSpec((GMLP_HEADS, CHUNK, CHUNK), lambda i: (0, 0, 0)),
            pl.BlockSpec((TM_PROJ, GMLP_WIDTH), const2),
            pl.BlockSpec((1, GMLP_WIDTH), const2),
        ],
        out_specs=[
            pl.BlockSpec((TM_PROJ, GMLP_WIDTH), row),
            pl.BlockSpec((TM_PROJ, SSD_WIDTH), row),
            pl.BlockSpec((TM_PROJ, CONV_CH), row),
            pl.BlockSpec((TM_PROJ, LANES), row),
        ],
        out_shape=[
            jax.ShapeDtypeStruct((t, GMLP_WIDTH), BF16),
            jax.ShapeDtypeStruct((t, SSD_WIDTH), BF16),
            jax.ShapeDtypeStruct((t, CONV_CH), F32),
            jax.ShapeDtypeStruct((t, LANES), F32),
        ],
        scratch_shapes=[pltpu.VMEM((TM_PROJ, GMLP_WIDTH), F32)],
        compiler_params=pltpu.CompilerParams(
            dimension_semantics=("arbitrary",), vmem_limit_bytes=VMEM_LIMIT),
        name="inproj_gmlp",
    )(x2, mod3, mod3, norm1_g, w_in_p, b_in_p, ln_g, ln_b, w_s, bs_tile, out_g)


def _ssd_chunk(act, dtv, a_row, expand, st_scr, rev):
    off = SSD_HEADS if rev else 0
    row = lax.broadcasted_iota(jnp.int32, (CHUNK, CHUNK), 0)
    col = lax.broadcasted_iota(jnp.int32, (CHUNK, CHUNK), 1)
    lower = row >= col
    upper = row <= col
    keep = upper if rev else lower
    keep_t = lower if rev else upper
    da = dtv * a_row
    cs = jnp.dot(keep.astype(F32), da, precision=HIGHEST, preferred_element_type=F32)
    cs_t = jnp.dot(da.T, keep_t.astype(F32), precision=HIGHEST, preferred_element_type=F32)
    dt_t = dtv.T
    tot = cs[0:1, :] if rev else cs[CHUNK - 1:CHUNK, :]

    xs = act[:, :SSD_WIDTH]
    lane = lax.broadcasted_iota(jnp.int32, (CHUNK, LANES), 1)
    first_half = lane < SSD_HEAD_DIM
    zero = jnp.zeros((), BF16)

    stack = jnp.concatenate(
        [jnp.exp(cs), dtv * jnp.exp(tot - cs), jnp.broadcast_to(jnp.exp(tot), (SUBLANES, LANES))], axis=0)
    stack_x = jnp.dot(stack.astype(BF16), expand, preferred_element_type=F32)
    into_x = stack_x[:CHUNK]
    w_x = stack_x[CHUNK:2 * CHUNK]
    cd_x = stack_x[2 * CHUNK:2 * CHUNK + 1]
    xw = (xs.astype(F32) * w_x).astype(BF16)

    pieces = []
    for g in range(SSD_GROUPS):
        bg = act[:, SSD_WIDTH + g * SSD_STATE:SSD_WIDTH + (g + 1) * SSD_STATE]
        cg = act[:, SSD_WIDTH + SSD_GROUPS * SSD_STATE + g * SSD_STATE:
                 SSD_WIDTH + SSD_GROUPS * SSD_STATE + (g + 1) * SSD_STATE]
        scores = lax.dot_general(cg, bg, (((1,), (1,)), ((), ())), preferred_element_type=F32)
        heads_per_group = SSD_HEADS // SSD_GROUPS
        for pair in range(heads_per_group // 2):
            h0 = g * heads_per_group + 2 * pair
            xs_pair = xs[:, h0 * SSD_HEAD_DIM:(h0 + 2) * SSD_HEAD_DIM]
            y_pair = None
            for k in range(2):
                hh = off + h0 + k
                seg = cs[:, hh:hh + 1] - cs_t[hh:hh + 1, :]
                dec = jnp.exp(jnp.where(keep, seg, -jnp.inf))
                m = (scores * dec * dt_t[hh:hh + 1, :]).astype(BF16)
                rhs = jnp.where(first_half if k == 0 else jnp.logical_not(first_half), xs_pair, zero)
                part = jnp.dot(m, rhs, preferred_element_type=F32)
                y_pair = part if y_pair is None else y_pair + part
            pieces.append(y_pair)
    y_diag = jnp.concatenate(pieces, axis=1)

    y_off = []
    for g in range(SSD_GROUPS):
        gcols = slice(g * GROUP_WIDTH, (g + 1) * GROUP_WIDTH)
        bg = act[:, SSD_WIDTH + g * SSD_STATE:SSD_WIDTH + (g + 1) * SSD_STATE]
        cg = act[:, SSD_WIDTH + SSD_GROUPS * SSD_STATE + g * SSD_STATE:
                 SSD_WIDTH + SSD_GROUPS * SSD_STATE + (g + 1) * SSD_STATE]
        prev = st_scr[:, gcols]
        y_off.append(jnp.dot(cg, prev.astype(BF16), preferred_element_type=F32))
        bg_t = bg.astype(F32).T.astype(BF16)
        new = jnp.dot(bg_t, xw[:, gcols], preferred_element_type=F32)
        st_scr[:, gcols] = prev * cd_x[:, gcols] + new
    return y_diag + jnp.concatenate(y_off, axis=1) * into_x


def _ssd_body(xbc_ref, xprev_ref, xnext_ref, dt_ref, z_ref, ya_ref, x_ref, gate_ref, cw_ref, cb_ref,
              dtb_ref, alog_ref, dsk_ref, ng_ref, wout_ref, exp_ref, o_ref,
              act_scr, dts_scr, yf_scr, st_scr, *, n_chunks):
    d = pl.program_id(1)
    c = pl.program_id(2)
    lane1 = lax.broadcasted_iota(jnp.int32, (1, LANES), 1)
    a_row = jnp.where(lane1 < 2 * SSD_HEADS, -jnp.exp(alog_ref[...]), 0.0)

    @pl.when(c == 0)
    def _():
        st_scr[...] = jnp.zeros_like(st_scr)

    @pl.when(d == 0)
    def _forward():
        row0 = pl.multiple_of(c * CHUNK, CHUNK)
        prev = jnp.where(c > 0, xprev_ref[...], 0.0)
        nxt = jnp.where(c < n_chunks - 1, xnext_ref[...], 0.0)
        ext = jnp.concatenate([prev, xbc_ref[...], nxt], axis=0)
        acc = cb_ref[...]
        for k in range(CONV_WIDTH):
            lo = SUBLANES - CONV_WIDTH // 2 + k
            acc = acc + cw_ref[k:k + 1, :] * ext[lo:lo + CHUNK, :]
        act = _silu(acc).astype(BF16)
        dtv = _softplus(dt_ref[...] + dtb_ref[...])
        act_scr[pl.ds(row0, CHUNK), :] = act
        dts_scr[pl.ds(row0, CHUNK), :] = dtv
        y = _ssd_chunk(act, dtv, a_row, exp_ref[0], st_scr, rev=False)
        yf_scr[pl.ds(row0, CHUNK), :] = y

    @pl.when(d == 1)
    def _backward():
        row0 = pl.multiple_of((n_chunks - 1 - c) * CHUNK, CHUNK)
        act = act_scr[pl.ds(row0, CHUNK), :]
        dtv = dts_scr[pl.ds(row0, CHUNK), :]
        yb = _ssd_chunk(act, dtv, a_row, exp_ref[1], st_scr, rev=True)
        xs = act[:, :SSD_WIDTH].astype(F32)
        y = yf_scr[pl.ds(row0, CHUNK), :] + yb + dsk_ref[...] * xs
        y = y * _silu(z_ref[...].astype(F32))
        y = jnp.concatenate(
            [_rms(y[:, g * GROUP_WIDTH:(g + 1) * GROUP_WIDTH]) for g in range(SSD_GROUPS)], axis=1)
        y = y * ng_ref[...]
        mix = jnp.concatenate([ya_ref[...], y.astype(BF16)], axis=1)
        o = jnp.dot(mix, wout_ref[...], preferred_element_type=F32)
        o_ref[...] = x_ref[...] + gate_ref[...] * o


def _ssd_outproj(xbc, dt, z, ya, x2, mod3, conv_w8, conv_b, dtb_row, alog_row, dsk_row, norm_g, w_out_b,
                 expand, bn, seq):
    t = x2.shape[0]
    nc = seq // CHUNK
    blocks8 = CHUNK // SUBLANES
    last8 = t // SUBLANES - 1

    def fwd_chunk(b, d, c):
        return b * nc + c + d * (nc - 1 - c)

    def bwd_chunk(b, d, c):
        return b * nc + nc - 1 - d * c

    const2 = lambda b, d, c: (0, 0)
    return pl.pallas_call(
        functools.partial(_ssd_body, n_chunks=nc),
        grid=(bn, 2, nc),
        in_specs=[
            pl.BlockSpec((CHUNK, CONV_CH), lambda b, d, c: (fwd_chunk(b, d, c), 0)),
            pl.BlockSpec((SUBLANES, CONV_CH),
                         lambda b, d, c: (jnp.maximum(fwd_chunk(b, d, c) * blocks8 - 1, 0), 0)),
            pl.BlockSpec((SUBLANES, CONV_CH),
                         lambda b, d, c: (jnp.minimum((fwd_chunk(b, d, c) + 1) * blocks8, last8), 0)),
            pl.BlockSpec((CHUNK, LANES), lambda b, d, c: (fwd_chunk(b, d, c), 0)),
            pl.BlockSpec((CHUNK, SSD_WIDTH), lambda b, d, c: (bwd_chunk(b, d, c), 0)),
            pl.BlockSpec((CHUNK, GMLP_WIDTH), lambda b, d, c: (bwd_chunk(b, d, c), 0)),
            pl.BlockSpec((CHUNK, D_MODEL), lambda b, d, c: (bwd_chunk(b, d, c), 0)),
            pl.BlockSpec((None, 1, D_MODEL), lambda b, d, c: (b, 0, 2)),
            pl.BlockSpec((SUBLANES, CONV_CH), const2),
            pl.BlockSpec((1, CONV_CH), const2),
            pl.BlockSpec((1, LANES), const2),
            pl.BlockSpec((1, LANES), const2),
            pl.BlockSpec((1, SSD_WIDTH), const2),
            pl.BlockSpec((1, SSD_WIDTH), const2),
            pl.BlockSpec((GMLP_WIDTH + SSD_WIDTH, D_MODEL), const2),
            pl.BlockSpec((2, LANES, SSD_WIDTH), lambda b, d, c: (0, 0, 0)),
        ],
        out_specs=pl.BlockSpec((CHUNK, D_MODEL), lambda b, d, c: (bwd_chunk(b, d, c), 0)),
        out_shape=jax.ShapeDtypeStruct((t, D_MODEL), F32),
        scratch_shapes=[
            pltpu.VMEM((seq, CONV_CH), BF16),
            pltpu.VMEM((seq, LANES), F32),
            pltpu.VMEM((seq, SSD_WIDTH), F32),
            pltpu.VMEM((SSD_STATE, SSD_WIDTH), F32),
        ],
        compiler_params=pltpu.CompilerParams(
            dimension_semantics=("arbitrary", "arbitrary", "arbitrary"), vmem_limit_bytes=VMEM_LIMIT),
        name="ssd_outproj",
    )(xbc, xbc, xbc, dt, z, ya, x2, mod3, conv_w8, conv_b, dtb_row, alog_row, dsk_row, norm_g, w_out_b,
      expand)


def _route(logits):
    lane = lax.broadcasted_iota(jnp.int32, logits.shape, 1)
    big = jnp.int32(LANES)
    neg = -jnp.inf
    gmask = (lane >= N_EXPERTS) & (lane < N_EXPERTS + N_EGROUPS)
    gl = jnp.where(gmask, logits, neg)
    gmax = jnp.max(gl, axis=-1, keepdims=True)
    gidx = jnp.min(jnp.where(gl == gmax, lane, big), axis=-1, keepdims=True) - N_EXPERTS
    p_g = 1.0 / jnp.sum(jnp.where(gmask, jnp.exp(gl - gmax), 0.0), axis=-1, keepdims=True)
    lo = gidx * EXPERTS_PER_GROUP
    emask = (lane >= lo) & (lane < lo + EXPERTS_PER_GROUP)
    el = jnp.where(emask, logits, neg)
    v1 = jnp.max(el, axis=-1, keepdims=True)
    i1 = jnp.min(jnp.where(el == v1, lane, big), axis=-1, keepdims=True)
    el2 = jnp.where(lane == i1, neg, el)
    v2 = jnp.max(el2, axis=-1, keepdims=True)
    i2 = jnp.min(jnp.where(el2 == v2, lane, big), axis=-1, keepdims=True)
    e2 = jnp.exp(v2 - v1)
    den = 1.0 + e2
    w1 = p_g / den
    w2 = p_g * e2 / den
    return jnp.where(lane == i1, w1, 0.0) + jnp.where(lane == i2, w2, 0.0)


def _moe_body(x_ref, shift_ref, scale_ref, gate_ref, g2_ref, wr_ref, br_ref, w1_ref, w3_ref, w2_ref,
              fg_ref, o_ref, h_scr, comb_scr, acc_scr, *, final_norm):
    e = pl.program_id(1)

    @pl.when(e == 0)
    def _():
        h = _rms(x_ref[...]) * g2_ref[...]
        h = h * (1.0 + scale_ref[...]) + shift_ref[...]
        h_scr[...] = h.astype(BF16)
        logits = jnp.dot(h, wr_ref[...], precision=HIGHEST, preferred_element_type=F32) + br_ref[...]
        comb_scr[...] = _route(logits)
        acc_scr[...] = jnp.zeros_like(acc_scr)

    hb = h_scr[...]
    a = _silu(jnp.dot(hb, w1_ref[...], preferred_element_type=F32))
    a = a * jnp.dot(hb, w3_ref[...], preferred_element_type=F32)
    lane = lax.broadcasted_iota(jnp.int32, comb_scr.shape, 1)
    ce = jnp.sum(jnp.where(lane == e, comb_scr[...], 0.0), axis=-1, keepdims=True)
    acc_scr[...] += jnp.dot((a * ce).astype(BF16), w2_ref[...], preferred_element_type=F32)

    @pl.when(e == N_EXPERTS - 1)
    def _():
        x2 = x_ref[...] + gate_ref[...] * acc_scr[...]
        o_ref[...] = _rms(x2) * fg_ref[...] if final_norm else x2


def _moe_final(x1, mod3, norm2_g, w_router, b_router, w1b, w3b, w2b, final_g, seq, final_norm):
    t = x1.shape[0]
    tiles_per_seq = seq // TM_MOE
    row = lambda i, e: (i, 0)
    const2 = lambda i, e: (0, 0)
    return pl.pallas_call(
        functools.partial(_moe_body, final_norm=final_norm),
        grid=(t // TM_MOE, N_EXPERTS),
        in_specs=[
            pl.BlockSpec((TM_MOE, D_MODEL), row),
            pl.BlockSpec((None, 1, D_MODEL), lambda i, e: (i // tiles_per_seq, 0, 3)),
            pl.BlockSpec((None, 1, D_MODEL), lambda i, e: (i // tiles_per_seq, 0, 4)),
            pl.BlockSpec((None, 1, D_MODEL), lambda i, e: (i // tiles_per_seq, 0, 5)),
            pl.BlockSpec((1, D_MODEL), const2),
            pl.BlockSpec((D_MODEL, LANES), const2),
            pl.BlockSpec((1, LANES), const2),
            pl.BlockSpec((None, D_MODEL, D_EXPERT), lambda i, e: (e, 0, 0)),
            pl.BlockSpec((None, D_MODEL, D_EXPERT), lambda i, e: (e, 0, 0)),
            pl.BlockSpec((None, D_EXPERT, D_MODEL), lambda i, e: (e, 0, 0)),
            pl.BlockSpec((1, D_MODEL), const2),
        ],
        out_specs=pl.BlockSpec((TM_MOE, D_MODEL), row),
        out_shape=jax.ShapeDtypeStruct((t, D_MODEL), F32),
        scratch_shapes=[
            pltpu.VMEM((TM_MOE, D_MODEL), BF16),
            pltpu.VMEM((TM_MOE, LANES), F32),
            pltpu.VMEM((TM_MOE, D_MODEL), F32),
        ],
        compiler_params=pltpu.CompilerParams(
            dimension_semantics=("arbitrary", "arbitrary"), vmem_limit_bytes=VMEM_LIMIT),
        name="moe_final",
    )(x1, mod3, mod3, mod3, norm2_g, w_router, b_router, w1b, w3b, w2b, final_g)


def _pad_cols(a, width):
    return jnp.pad(a, ((0, 0), (0, width - a.shape[1])))


def kernel(x, c, w_ada, b_ada, norm1_g, w_in, b_in, gmlp_ln_g, gmlp_ln_b, gmlp_w_s, gmlp_b_s, gmlp_out_g, conv_w, conv_b, a_log_f, a_log_b, dt_bias_f, dt_bias_b, d_skip, ssd_norm_g, w_out, norm2_g, w_router_g, b_router_g, w_router_e, b_router_e, w1, w3, w2, final_g):
    bn, seq, _ = x.shape
    depth = w_ada.shape[0]
    x2 = x.reshape(bn * seq, D_MODEL)

    head_rows = jnp.arange(LANES)[:, None]
    head_cols = jnp.arange(SSD_WIDTH)[None, :] // SSD_HEAD_DIM
    expand = jnp.stack([head_rows == head_cols, head_rows == head_cols + SSD_HEADS]).astype(BF16)

    for l in range(depth):
        mod3 = _modulation(c, w_ada[l], b_ada[l]).reshape(bn, 1, N_MOD * D_MODEL)

        w_in_p = _pad_cols(w_in[l], IN_COLS_PAD).astype(BF16)
        b_in_p = _pad_cols(b_in[l][None, :], IN_COLS_PAD)
        bs = jnp.repeat(gmlp_b_s[l].T, LANES, axis=1)
        bs_tile = jnp.tile(bs, (TM_PROJ // CHUNK, 1))
        ya, z, xbc, dt = _inproj_gmlp(
            x2, mod3, norm1_g[l][None, :], w_in_p, b_in_p, gmlp_ln_g[l][None, :], gmlp_ln_b[l][None, :],
            gmlp_w_s[l].astype(BF16), bs_tile, gmlp_out_g[l][None, :], seq)

        conv_w8 = jnp.pad(conv_w[l], ((0, SUBLANES - CONV_WIDTH), (0, 0)))
        dtb_row = _pad_cols(jnp.concatenate([dt_bias_f[l], dt_bias_b[l]])[None, :], LANES)
        alog_row = _pad_cols(jnp.concatenate([a_log_f[l], a_log_b[l]])[None, :], LANES)
        dsk_row = jnp.repeat(d_skip[l], SSD_HEAD_DIM)[None, :]
        x2 = _ssd_outproj(xbc, dt, z, ya, x2, mod3, conv_w8, conv_b[l][None, :], dtb_row, alog_row, dsk_row,
                          ssd_norm_g[l][None, :], w_out[l].astype(BF16), expand, bn, seq)

        w_re = jnp.transpose(w_router_e[l], (1, 0, 2)).reshape(D_MODEL, N_EXPERTS)
        w_router = _pad_cols(jnp.concatenate([w_re, w_router_g[l]], axis=1), LANES)
        b_router = _pad_cols(jnp.concatenate([b_router_e[l].reshape(-1), b_router_g[l]])[None, :], LANES)
        x2 = _moe_final(x2, mod3, norm2_g[l][None, :], w_router, b_router, w1[l].astype(BF16),
                        w3[l].astype(BF16), w2[l].astype(BF16), final_g[None, :], seq,
                        final_norm=(l == depth - 1))
    return x2.reshape(bn, seq, D_MODEL)
```

```python
import functools
import math

import jax
import jax.numpy as jnp
from jax import lax
from jax.experimental import pallas as pl
from jax.experimental.pallas import tpu as pltpu

F32 = jnp.float32
BF16 = jnp.bfloat16
HIGHEST = lax.Precision.HIGHEST

D_MODEL = 1024
N_MOD = 6
GMLP_WIDTH = 1024
GMLP_HEADS = 8
CHUNK = 128
SSD_WIDTH = 1024
SSD_HEADS = 16
SSD_HEAD_DIM = 64
SSD_GROUPS = 2
SSD_STATE = 128
GROUP_WIDTH = SSD_WIDTH // SSD_GROUPS
CONV_WIDTH = 5
CONV_CH = SSD_WIDTH + 2 * SSD_GROUPS * SSD_STATE
N_EGROUPS = 4
EXPERTS_PER_GROUP = 8
N_EXPERTS = 32
D_EXPERT = 256
EPS = 1e-6

LANES = 128
SUBLANES = 8
COL_U, COL_V, COL_Z, COL_XBC, COL_DT = 0, 1024, 2048, 3072, 4608
IN_COLS = 4640
IN_COLS_PAD = COL_DT + LANES
TM_PROJ = 512
TM_MOE = 512
TR_EXPERT = 256
N_ASSIGN = 2
SEG_ALIGN = SUBLANES
SEG_SIZES = tuple(TM_MOE >> s for s in range(TM_MOE.bit_length()) if TM_MOE >> s >= SEG_ALIGN)
FILL_SIZES = tuple(s for s in SEG_SIZES if s < TR_EXPERT)
R_LOC = N_ASSIGN * TM_MOE + N_EXPERTS * SEG_ALIGN
VMEM_LIMIT = 56 * 1024 * 1024


def _silu(v):
    return v * jax.nn.sigmoid(v)


def _gelu(v):
    return 0.5 * v * (1.0 + lax.erf(v * math.sqrt(0.5)))


def _softplus(v):
    return jnp.maximum(v, 0.0) + jnp.log1p(jnp.exp(-jnp.abs(v)))


def _rms(v):
    return v * lax.rsqrt(jnp.mean(v * v, axis=-1, keepdims=True) + EPS)


def _mod_body(c_ref, w_ref, b_ref, o_ref):
    ca = _silu(c_ref[...])
    o_ref[...] = jnp.dot(ca, w_ref[...], precision=HIGHEST, preferred_element_type=F32) + b_ref[...]


def _modulation(c, w_ada, b_ada):
    bn = c.shape[0]
    return pl.pallas_call(
        _mod_body,
        grid=(N_MOD,),
        in_specs=[
            pl.BlockSpec((bn, D_MODEL), lambda j: (0, 0)),
            pl.BlockSpec((D_MODEL, D_MODEL), lambda j: (0, j)),
            pl.BlockSpec((1, D_MODEL), lambda j: (0, j)),
        ],
        out_specs=pl.BlockSpec((bn, D_MODEL), lambda j: (0, j)),
        out_shape=jax.ShapeDtypeStruct((bn, N_MOD * D_MODEL), F32),
        name="adaln_mod",
    )(c, w_ada, b_ada.reshape(1, -1))


def _inproj_body(x_ref, shift_ref, scale_ref, g_ref, w_ref, b_ref, lng_ref, lnb_ref, ws_ref, bs_ref,
                 og_ref, ya_ref, z_ref, xbc_ref, dt_ref, mix_scr):
    h = _rms(x_ref[...]) * g_ref[...]
    h = h * (1.0 + scale_ref[...]) + shift_ref[...]
    hb = h.astype(BF16)

    def proj(lo, hi):
        return jnp.dot(hb, w_ref[:, lo:hi], preferred_element_type=F32) + b_ref[:, lo:hi]

    z_ref[...] = proj(COL_Z, COL_XBC).astype(BF16)
    xbc_ref[...] = proj(COL_XBC, COL_DT)
    dt_ref[...] = proj(COL_DT, IN_COLS_PAD)

    v = _gelu(proj(COL_V, COL_Z))
    mu = jnp.mean(v, axis=-1, keepdims=True)
    vc = v - mu
    var = jnp.mean(vc * vc, axis=-1, keepdims=True)
    vn = (vc * lax.rsqrt(var + EPS) * lng_ref[...] + lnb_ref[...]).astype(BF16)
    n_chunks = TM_PROJ // CHUNK
    for hd in range(GMLP_HEADS):
        cols = slice(hd * LANES, (hd + 1) * LANES)
        rhs = jnp.concatenate([vn[c * CHUNK:(c + 1) * CHUNK, cols] for c in range(n_chunks)], axis=1)
        res = jnp.dot(ws_ref[hd], rhs, preferred_element_type=F32)
        for c in range(n_chunks):
            mix_scr[c * CHUNK:(c + 1) * CHUNK, cols] = res[:, c * LANES:(c + 1) * LANES]
    u = _gelu(proj(COL_U, COL_V))
    out = u * (mix_scr[...] + bs_ref[...])
    ya_ref[...] = (_rms(out) * og_ref[...]).astype(BF16)


def _inproj_gmlp(x2, mod3, norm1_g, w_in_p, b_in_p, ln_g, ln_b, w_s, bs_tile, out_g, seq):
    t = x2.shape[0]
    tiles_per_seq = seq // TM_PROJ
    row = lambda i: (i, 0)
    const2 = lambda i: (0, 0)
    return pl.pallas_call(
        _inproj_body,
        grid=(t // TM_PROJ,),
        in_specs=[
            pl.BlockSpec((TM_PROJ, D_MODEL), row),
            pl.BlockSpec((None, 1, D_MODEL), lambda i: (i // tiles_per_seq, 0, 0)),
            pl.BlockSpec((None, 1, D_MODEL), lambda i: (i // tiles_per_seq, 0, 1)),
            pl.BlockSpec((1, D_MODEL), const2),
            pl.BlockSpec((D_MODEL, IN_COLS_PAD), const2),
            pl.BlockSpec((1, IN_COLS_PAD), const2),
            pl.BlockSpec((1, GMLP_WIDTH), const2),
            pl.BlockSpec((1, GMLP_WIDTH), const2),
            pl.BlockSpec((GMLP_HEADS, CHUNK, CHUNK), lambda i: (0, 0, 0)),
            pl.BlockSpec((TM_PROJ, GMLP_WIDTH), const2),
            pl.BlockSpec((1, GMLP_WIDTH), const2),
        ],
        out_specs=[
            pl.BlockSpec((TM_PROJ, GMLP_WIDTH), row),
            pl.BlockSpec((TM_PROJ, SSD_WIDTH), row),
            pl.BlockSpec((TM_PROJ, CONV_CH), row),
            pl.BlockSpec((TM_PROJ, LANES), row),
        ],
        out_shape=[
            jax.ShapeDtypeStruct((t, GMLP_WIDTH), BF16),
            jax.ShapeDtypeStruct((t, SSD_WIDTH), BF16),
            jax.ShapeDtypeStruct((t, CONV_CH), F32),
            jax.ShapeDtypeStruct((t, LANES), F32),
        ],
        scratch_shapes=[pltpu.VMEM((TM_PROJ, GMLP_WIDTH), F32)],
        compiler_params=pltpu.CompilerParams(
            dimension_semantics=("arbitrary",), vmem_limit_bytes=VMEM_LIMIT),
        name="inproj_gmlp",
    )(x2, mod3, mod3, norm1_g, w_in_p, b_in_p, ln_g, ln_b, w_s, bs_tile, out_g)


def _ssd_chunk(act, dtv, a_row, expand, st_scr, rev):
    off = SSD_HEADS if rev else 0
    row = lax.broadcasted_iota(jnp.int32, (CHUNK, CHUNK), 0)
    col = lax.broadcasted_iota(jnp.int32, (CHUNK, CHUNK), 1)
    lower = row >= col
    upper = row <= col
    keep = upper if rev else lower
    keep_t = lower if rev else upper
    da = dtv * a_row
    cs = jnp.dot(keep.astype(F32), da, precision=HIGHEST, preferred_element_type=F32)
    cs_t = jnp.dot(da.T, keep_t.astype(F32), precision=HIGHEST, preferred_element_type=F32)
    dt_t = dtv.T
    tot = cs[0:1, :] if rev else cs[CHUNK - 1:CHUNK, :]

    xs = act[:, :SSD_WIDTH]
    lane = lax.broadcasted_iota(jnp.int32, (CHUNK, LANES), 1)
    first_half = lane < SSD_HEAD_DIM
    zero = jnp.zeros((), BF16)

    stack = jnp.concatenate(
        [jnp.exp(cs), dtv * jnp.exp(tot - cs), jnp.broadcast_to(jnp.exp(tot), (SUBLANES, LANES))], axis=0)
    stack_x = jnp.dot(stack.astype(BF16), expand, preferred_element_type=F32)
    into_x = stack_x[:CHUNK]
    w_x = stack_x[CHUNK:2 * CHUNK]
    cd_x = stack_x[2 * CHUNK:2 * CHUNK + 1]
    xw = (xs.astype(F32) * w_x).astype(BF16)

    pieces = []
    for g in range(SSD_GROUPS):
        bg = act[:, SSD_WIDTH + g * SSD_STATE:SSD_WIDTH + (g + 1) * SSD_STATE]
        cg = act[:, SSD_WIDTH + SSD_GROUPS * SSD_STATE + g * SSD_STATE:
                 SSD_WIDTH + SSD_GROUPS * SSD_STATE + (g + 1) * SSD_STATE]
        scores = lax.dot_general(cg, bg, (((1,), (1,)), ((), ())), preferred_element_type=F32)
        heads_per_group = SSD_HEADS // SSD_GROUPS
        for pair in range(heads_per_group // 2):
            h0 = g * heads_per_group + 2 * pair
            xs_pair = xs[:, h0 * SSD_HEAD_DIM:(h0 + 2) * SSD_HEAD_DIM]
            y_pair = None
            for k in range(2):
                hh = off + h0 + k
                seg = cs[:, hh:hh + 1] - cs_t[hh:hh + 1, :]
                dec = jnp.exp(jnp.where(keep, seg, -jnp.inf))
                m = (scores * dec * dt_t[hh:hh + 1, :]).astype(BF16)
                rhs = jnp.where(first_half if k == 0 else jnp.logical_not(first_half), xs_pair, zero)
                part = jnp.dot(m, rhs, preferred_element_type=F32)
                y_pair = part if y_pair is None else y_pair + part
            pieces.append(y_pair)
    y_diag = jnp.concatenate(pieces, axis=1)

    y_off = []
    for g in range(SSD_GROUPS):
        gcols = slice(g * GROUP_WIDTH, (g + 1) * GROUP_WIDTH)
        bg = act[:, SSD_WIDTH + g * SSD_STATE:SSD_WIDTH + (g + 1) * SSD_STATE]
        cg = act[:, SSD_WIDTH + SSD_GROUPS * SSD_STATE + g * SSD_STATE:
                 SSD_WIDTH + SSD_GROUPS * SSD_STATE + (g + 1) * SSD_STATE]
        prev = st_scr[:, gcols]
        y_off.append(jnp.dot(cg, prev.astype(BF16), preferred_element_type=F32))
        bg_t = bg.astype(F32).T.astype(BF16)
        new = jnp.dot(bg_t, xw[:, gcols], preferred_element_type=F32)
        st_scr[:, gcols] = prev * cd_x[:, gcols] + new
    return y_diag + jnp.concatenate(y_off, axis=1) * into_x


def _ssd_body(xbc_ref, xprev_ref, xnext_ref, dt_ref, z_ref, ya_ref, x_ref, gate_ref, cw_ref, cb_ref,
              dtb_ref, alog_ref, dsk_ref, ng_ref, wout_ref, exp_ref, o_ref,
              act_scr, dts_scr, yf_scr, st_scr, *, n_chunks):
    d = pl.program_id(1)
    c = pl.program_id(2)
    lane1 = lax.broadcasted_iota(jnp.int32, (1, LANES), 1)
    a_row = jnp.where(lane1 < 2 * SSD_HEADS, -jnp.exp(alog_ref[...]), 0.0)

    @pl.when(c == 0)
    def _():
        st_scr[...] = jnp.zeros_like(st_scr)

    @pl.when(d == 0)
    def _forward():
        row0 = pl.multiple_of(c * CHUNK, CHUNK)
        prev = jnp.where(c > 0, xprev_ref[...], 0.0)
        nxt = jnp.where(c < n_chunks - 1, xnext_ref[...], 0.0)
        ext = jnp.concatenate([prev, xbc_ref[...], nxt], axis=0)
        acc = cb_ref[...]
        for k in range(CONV_WIDTH):
            lo = SUBLANES - CONV_WIDTH // 2 + k
            acc = acc + cw_ref[k:k + 1, :] * ext[lo:lo + CHUNK, :]
        act = _silu(acc).astype(BF16)
        dtv = _softplus(dt_ref[...] + dtb_ref[...])
        act_scr[pl.ds(row0, CHUNK), :] = act
        dts_scr[pl.ds(row0, CHUNK), :] = dtv
        y = _ssd_chunk(act, dtv, a_row, exp_ref[0], st_scr, rev=False)
        yf_scr[pl.ds(row0, CHUNK), :] = y

    @pl.when(d == 1)
    def _backward():
        row0 = pl.multiple_of((n_chunks - 1 - c) * CHUNK, CHUNK)
        act = act_scr[pl.ds(row0, CHUNK), :]
        dtv = dts_scr[pl.ds(row0, CHUNK), :]
        yb = _ssd_chunk(act, dtv, a_row, exp_ref[1], st_scr, rev=True)
        xs = act[:, :SSD_WIDTH].astype(F32)
        y = yf_scr[pl.ds(row0, CHUNK), :] + yb + dsk_ref[...] * xs
        y = y * _silu(z_ref[...].astype(F32))
        y = jnp.concatenate(
            [_rms(y[:, g * GROUP_WIDTH:(g + 1) * GROUP_WIDTH]) for g in range(SSD_GROUPS)], axis=1)
        y = y * ng_ref[...]
        mix = jnp.concatenate([ya_ref[...], y.astype(BF16)], axis=1)
        o = jnp.dot(mix, wout_ref[...], preferred_element_type=F32)
        o_ref[...] = x_ref[...] + gate_ref[...] * o


def _ssd_outproj(xbc, dt, z, ya, x2, mod3, conv_w8, conv_b, dtb_row, alog_row, dsk_row, norm_g, w_out_b,
                 expand, bn, seq):
    t = x2.shape[0]
    nc = seq // CHUNK
    blocks8 = CHUNK // SUBLANES
    last8 = t // SUBLANES - 1

    def fwd_chunk(b, d, c):
        return b * nc + c + d * (nc - 1 - c)

    def bwd_chunk(b, d, c):
        return b * nc + nc - 1 - d * c

    const2 = lambda b, d, c: (0, 0)
    return pl.pallas_call(
        functools.partial(_ssd_body, n_chunks=nc),
        grid=(bn, 2, nc),
        in_specs=[
            pl.BlockSpec((CHUNK, CONV_CH), lambda b, d, c: (fwd_chunk(b, d, c), 0)),
            pl.BlockSpec((SUBLANES, CONV_CH),
                         lambda b, d, c: (jnp.maximum(fwd_chunk(b, d, c) * blocks8 - 1, 0), 0)),
            pl.BlockSpec((SUBLANES, CONV_CH),
                         lambda b, d, c: (jnp.minimum((fwd_chunk(b, d, c) + 1) * blocks8, last8), 0)),
            pl.BlockSpec((CHUNK, LANES), lambda b, d, c: (fwd_chunk(b, d, c), 0)),
            pl.BlockSpec((CHUNK, SSD_WIDTH), lambda b, d, c: (bwd_chunk(b, d, c), 0)),
            pl.BlockSpec((CHUNK, GMLP_WIDTH), lambda b, d, c: (bwd_chunk(b, d, c), 0)),
            pl.BlockSpec((CHUNK, D_MODEL), lambda b, d, c: (bwd_chunk(b, d, c), 0)),
            pl.BlockSpec((None, 1, D_MODEL), lambda b, d, c: (b, 0, 2)),
            pl.BlockSpec((SUBLANES, CONV_CH), const2),
            pl.BlockSpec((1, CONV_CH), const2),
            pl.BlockSpec((1, LANES), const2),
            pl.BlockSpec((1, LANES), const2),
            pl.BlockSpec((1, SSD_WIDTH), const2),
            pl.BlockSpec((1, SSD_WIDTH), const2),
            pl.BlockSpec((GMLP_WIDTH + SSD_WIDTH, D_MODEL), const2),
            pl.BlockSpec((2, LANES, SSD_WIDTH), lambda b, d, c: (0, 0, 0)),
        ],
        out_specs=pl.BlockSpec((CHUNK, D_MODEL), lambda b, d, c: (bwd_chunk(b, d, c), 0)),
        out_shape=jax.ShapeDtypeStruct((t, D_MODEL), F32),
        scratch_shapes=[
            pltpu.VMEM((seq, CONV_CH), BF16),
            pltpu.VMEM((seq, LANES), F32),
            pltpu.VMEM((seq, SSD_WIDTH), F32),
            pltpu.VMEM((SSD_STATE, SSD_WIDTH), F32),
        ],
        compiler_params=pltpu.CompilerParams(
            dimension_semantics=("arbitrary", "arbitrary", "arbitrary"), vmem_limit_bytes=VMEM_LIMIT),
        name="ssd_outproj",
    )(xbc, xbc, xbc, dt, z, ya, x2, mod3, conv_w8, conv_b, dtb_row, alog_row, dsk_row, norm_g, w_out_b,
      expand)


def _route(logits):
    lane = lax.broadcasted_iota(jnp.int32, logits.shape, 1)
    big = jnp.int32(LANES)
    neg = -jnp.inf
    gmask = (lane >= N_EXPERTS) & (lane < N_EXPERTS + N_EGROUPS)
    gl = jnp.where(gmask, logits, neg)
    gmax = jnp.max(gl, axis=-1, keepdims=True)
    gidx = jnp.min(jnp.where(gl == gmax, lane, big), axis=-1, keepdims=True) - N_EXPERTS
    p_g = 1.0 / jnp.sum(jnp.where(gmask, jnp.exp(gl - gmax), 0.0), axis=-1, keepdims=True)
    lo = gidx * EXPERTS_PER_GROUP
    emask = (lane >= lo) & (lane < lo + EXPERTS_PER_GROUP)
    el = jnp.where(emask, logits, neg)
    v1 = jnp.max(el, axis=-1, keepdims=True)
    i1 = jnp.min(jnp.where(el == v1, lane, big), axis=-1, keepdims=True)
    el2 = jnp.where(lane == i1, neg, el)
    v2 = jnp.max(el2, axis=-1, keepdims=True)
    i2 = jnp.min(jnp.where(el2 == v2, lane, big), axis=-1, keepdims=True)
    e2 = jnp.exp(v2 - v1)
    den = 1.0 + e2
    w1 = p_g / den
    w2 = p_g * e2 / den
    return i1, i2, w1, w2


def _router_body(x_ref, shift_ref, scale_ref, g2_ref, wr_ref, br_ref, h_ref, rt_ref, cnt_ref):
    h = _rms(x_ref[...]) * g2_ref[...]
    h = h * (1.0 + scale_ref[...]) + shift_ref[...]
    h_ref[...] = h.astype(BF16)
    logits = jnp.dot(h, wr_ref[...], precision=HIGHEST, preferred_element_type=F32) + br_ref[...]
    i1, i2, w1, w2 = _route(logits)
    lane = lax.broadcasted_iota(jnp.int32, (TM_MOE, LANES), 1)
    rt_ref[...] = jnp.where(lane == 0, i1.astype(F32), jnp.where(lane == 1, i2.astype(F32),
                            jnp.where(lane == 2, w1, jnp.where(lane == 3, w2, 0.0))))
    mask = jnp.where(lane == i1, 1.0, jnp.where(lane == i2, 1.0, 0.0))
    cnt_ref[...] = jnp.broadcast_to(jnp.sum(mask, axis=0, keepdims=True), (SUBLANES, LANES))


def _router(x1, mod3, norm2_g, w_router, b_router, seq):
    t = x1.shape[0]
    tiles_per_seq = seq // TM_MOE
    n_tiles = t // TM_MOE
    row = lambda i: (i, 0)
    const2 = lambda i: (0, 0)
    return pl.pallas_call(
        _router_body,
        grid=(n_tiles,),
        in_specs=[
            pl.BlockSpec((TM_MOE, D_MODEL), row),
            pl.BlockSpec((None, 1, D_MODEL), lambda i: (i // tiles_per_seq, 0, 3)),
            pl.BlockSpec((None, 1, D_MODEL), lambda i: (i // tiles_per_seq, 0, 4)),
            pl.BlockSpec((1, D_MODEL), const2),
            pl.BlockSpec((D_MODEL, LANES), const2),
            pl.BlockSpec((1, LANES), const2),
        ],
        out_specs=[
            pl.BlockSpec((TM_MOE, D_MODEL), row),
            pl.BlockSpec((TM_MOE, LANES), row),
            pl.BlockSpec((SUBLANES, LANES), row),
        ],
        out_shape=[
            jax.ShapeDtypeStruct((t, D_MODEL), BF16),
            jax.ShapeDtypeStruct((t, LANES), F32),
            jax.ShapeDtypeStruct((n_tiles * SUBLANES, LANES), F32),
        ],
        compiler_params=pltpu.CompilerParams(
            dimension_semantics=("arbitrary",), vmem_limit_bytes=VMEM_LIMIT),
        name="moe_router",
    )(x1, mod3, mod3, norm2_g, w_router, b_router)


def _copy_rows(n, vmem_buf, v0, hbm_ref, g0, sem, sizes, *, to_hbm, wait):
    for size in sizes:
        done = (n // (2 * size)) * (2 * size)

        @pl.when((n & size) != 0)
        def _():
            v_start = 0 if v0 is None else pl.multiple_of(v0 + done, SEG_ALIGN)
            v = vmem_buf.at[pl.ds(v_start, size), :]
            g = hbm_ref.at[pl.ds(pl.multiple_of(g0 + done, SEG_ALIGN), size), :]
            cp = pltpu.make_async_copy(v, g, sem) if to_hbm else pltpu.make_async_copy(g, v, sem)
            if wait:
                cp.wait()
            else:
                cp.start()


def _segment_copies(seg_ref, loc_ref, goff_ref, tile, vmem_buf, hbm_ref, sem, *, to_hbm, wait):
    base = tile * N_EXPERTS

    def per_expert(e, carry):
        _copy_rows(seg_ref[base + e], vmem_buf, loc_ref[base + e], hbm_ref, goff_ref[base + e], sem,
                   SEG_SIZES, to_hbm=to_hbm, wait=wait)
        return carry

    lax.fori_loop(0, N_EXPERTS, per_expert, 0)


def _dispatch_body(seg_ref, loc_ref, goff_ref, fill_ref, h_ref, rt_ref, locrow_ref, pos_ref, xs_hbm, buf, sem,
                   *, n_tiles, n_row_tiles):
    i = pl.program_id(0)
    slot = i % 2
    copies = functools.partial(_segment_copies, seg_ref, loc_ref, goff_ref, hbm_ref=xs_hbm, to_hbm=True)

    @pl.when(i >= 2)
    def _():
        copies(i - 2, buf.at[slot], sem=sem.at[slot], wait=True)

    rt = rt_ref[...]
    lane = lax.broadcasted_iota(jnp.int32, (TM_MOE, LANES), 1)
    lanef = lane.astype(F32)
    e1 = rt[:, 0:1]
    e2 = rt[:, 1:2]
    mask = jnp.where(lanef == e1, 1.0, jnp.where(lanef == e2, 1.0, 0.0))
    r = lax.broadcasted_iota(jnp.int32, (TM_MOE, TM_MOE), 0)
    cc = lax.broadcasted_iota(jnp.int32, (TM_MOE, TM_MOE), 1)
    strict = jnp.where(r > cc, 1.0, 0.0).astype(BF16)
    rank = jnp.dot(strict, mask.astype(BF16), preferred_element_type=F32)
    posall = rank + locrow_ref[...]
    pos1 = jnp.sum(jnp.where(lanef == e1, posall, 0.0), axis=-1, keepdims=True)
    pos2 = jnp.sum(jnp.where(lanef == e2, posall, 0.0), axis=-1, keepdims=True)
    pp = jnp.where(lane == 0, pos1, jnp.where(lane == 1, pos2, rt))
    pos_ref[...] = pp

    rowp = lax.broadcasted_iota(jnp.int32, (R_LOC, LANES), 0).astype(F32)
    blocks = []
    for k in range(TM_MOE // LANES):
        ppt = pp[k * LANES:(k + 1) * LANES, :].T
        blocks.append(jnp.where(rowp == ppt[0:1, :], 1.0, jnp.where(rowp == ppt[1:2, :], 1.0, 0.0)).astype(BF16))
    perm = jnp.concatenate(blocks, axis=1)
    buf[slot] = jnp.dot(perm, h_ref[...], preferred_element_type=F32)
    copies(i, buf.at[slot], sem=sem.at[slot], wait=False)

    @pl.when(i == n_tiles - 1)
    def _():
        if n_tiles > 1:
            copies(i - 1, buf.at[1 - slot], sem=sem.at[1 - slot], wait=True)
        copies(i, buf.at[slot], sem=sem.at[slot], wait=True)
        zsrc = buf.at[1 - slot]
        zsem = sem.at[1 - slot]
        zsrc[0:TR_EXPERT, :] = jnp.zeros((TR_EXPERT, D_MODEL), F32)
        for wait in (False, True):
            def per_expert(e, carry):
                _copy_rows(fill_ref[N_EXPERTS + e], zsrc, None, xs_hbm, fill_ref[e], zsem, FILL_SIZES,
                           to_hbm=True, wait=wait)
                return carry

            def per_row_tile(k, carry):
                g = xs_hbm.at[pl.ds(pl.multiple_of(k * TR_EXPERT, TR_EXPERT), TR_EXPERT), :]
                cp = pltpu.make_async_copy(zsrc.at[pl.ds(0, TR_EXPERT), :], g, zsem)
                if wait:
                    cp.wait()
                else:
                    cp.start()
                return carry

            lax.fori_loop(0, N_EXPERTS, per_expert, 0)
            lax.fori_loop(fill_ref[2 * N_EXPERTS], n_row_tiles, per_row_tile, 0)


def _dispatch(h2, rt, seg, loc, goff, fill, locrow, n_row_tiles):
    t = h2.shape[0]
    n_tiles = t // TM_MOE
    row = lambda i, *_: (i, 0)
    grid_spec = pltpu.PrefetchScalarGridSpec(
        num_scalar_prefetch=4,
        grid=(n_tiles,),
        in_specs=[
            pl.BlockSpec((TM_MOE, D_MODEL), row),
            pl.BlockSpec((TM_MOE, LANES), row),
            pl.BlockSpec((None, 1, LANES), lambda i, *_: (i, 0, 0)),
        ],
        out_specs=[
            pl.BlockSpec((TM_MOE, LANES), row),
            pl.BlockSpec(memory_space=pl.ANY),
        ],
        scratch_shapes=[
            pltpu.VMEM((2, R_LOC, D_MODEL), F32),
            pltpu.SemaphoreType.DMA((2,)),
        ],
    )
    return pl.pallas_call(
        functools.partial(_dispatch_body, n_tiles=n_tiles, n_row_tiles=n_row_tiles),
        grid_spec=grid_spec,
        out_shape=[
            jax.ShapeDtypeStruct((t, LANES), F32),
            jax.ShapeDtypeStruct((n_row_tiles * TR_EXPERT, D_MODEL), F32),
        ],
        compiler_params=pltpu.CompilerParams(
            dimension_semantics=("arbitrary",), vmem_limit_bytes=VMEM_LIMIT),
        name="moe_dispatch",
    )(seg, loc, goff, fill, h2, rt, locrow)


def _expert_body(te_ref, tv_ref, na_ref, xs_ref, w1_ref, w3_ref, w2_ref, y_ref):
    r = pl.program_id(0)

    @pl.when(r < na_ref[0])
    def _():
        rows = lax.broadcasted_iota(jnp.int32, (TR_EXPERT, D_MODEL), 0)
        x = jnp.where(rows < tv_ref[r], xs_ref[...], 0.0).astype(BF16)
        a = _silu(jnp.dot(x, w1_ref[...], preferred_element_type=F32))
        a = a * jnp.dot(x, w3_ref[...], preferred_element_type=F32)
        y_ref[...] = jnp.dot(a.astype(BF16), w2_ref[...], preferred_element_type=F32)

    @pl.when(r >= na_ref[0])
    def _():
        y_ref[...] = jnp.zeros_like(y_ref)


def _experts(xs, tile_expert, tile_valid, n_active, w1b, w3b, w2b):
    n_rows = xs.shape[0]
    rows = lambda r, te, tv, na: (r, 0)
    grid_spec = pltpu.PrefetchScalarGridSpec(
        num_scalar_prefetch=3,
        grid=(n_rows // TR_EXPERT,),
        in_specs=[
            pl.BlockSpec((TR_EXPERT, D_MODEL), lambda r, te, tv, na: (jnp.minimum(r, na[0] - 1), 0)),
            pl.BlockSpec((None, D_MODEL, D_EXPERT), lambda r, te, tv, na: (te[r], 0, 0)),
            pl.BlockSpec((None, D_MODEL, D_EXPERT), lambda r, te, tv, na: (te[r], 0, 0)),
            pl.BlockSpec((None, D_EXPERT, D_MODEL), lambda r, te, tv, na: (te[r], 0, 0)),
        ],
        out_specs=pl.BlockSpec((TR_EXPERT, D_MODEL), rows),
    )
    return pl.pallas_call(
        _expert_body,
        grid_spec=grid_spec,
        out_shape=jax.ShapeDtypeStruct((n_rows, D_MODEL), F32),
        compiler_params=pltpu.CompilerParams(
            dimension_semantics=("arbitrary",), vmem_limit_bytes=VMEM_LIMIT),
        name="moe_experts",
    )(tile_expert, tile_valid, n_active, xs, w1b, w3b, w2b)


def _combine_body(seg_ref, loc_ref, goff_ref, y_hbm, pos_ref, x_ref, gate_ref, fg_ref, o_ref, buf, sem, *,
                  n_tiles, final_norm):
    i = pl.program_id(0)
    slot = i % 2
    copies = functools.partial(_segment_copies, seg_ref, loc_ref, goff_ref, hbm_ref=y_hbm, to_hbm=False)

    @pl.when(i == 0)
    def _():
        buf[...] = jnp.zeros_like(buf)
        copies(0, buf.at[0], sem=sem.at[0], wait=False)

    @pl.when(i + 1 < n_tiles)
    def _():
        copies(i + 1, buf.at[1 - slot], sem=sem.at[1 - slot], wait=False)

    pp = pos_ref[...]
    colp = lax.broadcasted_iota(jnp.int32, (TM_MOE, R_LOC), 1).astype(F32)
    wc = jnp.where(colp == pp[:, 0:1], pp[:, 2:3], jnp.where(colp == pp[:, 1:2], pp[:, 3:4], 0.0)).astype(BF16)
    copies(i, buf.at[slot], sem=sem.at[slot], wait=True)
    moe = jnp.dot(wc, buf[slot].astype(BF16), preferred_element_type=F32)
    x2 = x_ref[...] + gate_ref[...] * moe
    o_ref[...] = _rms(x2) * fg_ref[...] if final_norm else x2


def _combine(y, pos, x1, mod3, final_g, seg, loc, goff, seq, final_norm):
    t = x1.shape[0]
    n_tiles = t // TM_MOE
    tiles_per_seq = seq // TM_MOE
    row = lambda i, *_: (i, 0)
    grid_spec = pltpu.PrefetchScalarGridSpec(
        num_scalar_prefetch=3,
        grid=(n_tiles,),
        in_specs=[
            pl.BlockSpec(memory_space=pl.ANY),
            pl.BlockSpec((TM_MOE, LANES), row),
            pl.BlockSpec((TM_MOE, D_MODEL), row),
            pl.BlockSpec((None, 1, D_MODEL), lambda i, *_: (i // tiles_per_seq, 0, 5)),
            pl.BlockSpec((1, D_MODEL), lambda i, *_: (0, 0)),
        ],
        out_specs=pl.BlockSpec((TM_MOE, D_MODEL), row),
        scratch_shapes=[
            pltpu.VMEM((2, R_LOC, D_MODEL), F32),
            pltpu.SemaphoreType.DMA((2,)),
        ],
    )
    return pl.pallas_call(
        functools.partial(_combine_body, n_tiles=n_tiles, final_norm=final_norm),
        grid_spec=grid_spec,
        out_shape=jax.ShapeDtypeStruct((t, D_MODEL), F32),
        compiler_params=pltpu.CompilerParams(
            dimension_semantics=("arbitrary",), vmem_limit_bytes=VMEM_LIMIT),
        name="moe_combine",
    )(seg, loc, goff, y, pos, x1, mod3, final_g)


def _moe_plan(counts, n_tiles, n_row_tiles):
    cnt = counts.reshape(n_tiles, SUBLANES, LANES)[:, 0, :N_EXPERTS].astype(jnp.int32)
    seg = (cnt + SEG_ALIGN - 1) // SEG_ALIGN * SEG_ALIGN
    loc = jnp.cumsum(seg, axis=1) - seg
    tot = jnp.sum(seg, axis=0)
    region = (tot + TR_EXPERT - 1) // TR_EXPERT * TR_EXPERT
    gend = jnp.cumsum(region)
    gstart = gend - region
    goff = gstart[None, :] + jnp.cumsum(seg, axis=0) - seg
    n_active = gend[-1] // TR_EXPERT
    tile_row0 = jnp.arange(n_row_tiles, dtype=jnp.int32) * TR_EXPERT
    last_row0 = (n_active - 1) * TR_EXPERT
    te = jnp.searchsorted(gend, jnp.minimum(tile_row0, last_row0), side="right").astype(jnp.int32)
    te = jnp.minimum(te, N_EXPERTS - 1)
    tv = jnp.clip(gstart[te] + tot[te] - tile_row0, 0, TR_EXPERT).astype(jnp.int32)
    locrow = _pad_cols(loc.astype(F32), LANES).reshape(n_tiles, 1, LANES)
    n_active = n_active.reshape(1).astype(jnp.int32)
    fill = jnp.concatenate([gstart + tot, region - tot, n_active]).astype(jnp.int32)
    return (seg.reshape(-1), loc.reshape(-1).astype(jnp.int32), goff.reshape(-1).astype(jnp.int32), fill,
            locrow, te, tv, n_active)


def _moe_final(x1, mod3, norm2_g, w_router, b_router, w1b, w3b, w2b, final_g, seq, final_norm):
    t = x1.shape[0]
    n_tiles = t // TM_MOE
    max_rows = N_ASSIGN * t + (SEG_ALIGN - 1) * N_EXPERTS * n_tiles + N_EXPERTS * (TR_EXPERT - SEG_ALIGN)
    n_row_tiles = -(-max_rows // TR_EXPERT)
    h2, rt, counts = _router(x1, mod3, norm2_g, w_router, b_router, seq)
    seg, loc, goff, fill, locrow, te, tv, n_active = _moe_plan(counts, n_tiles, n_row_tiles)
    pos, xs = _dispatch(h2, rt, seg, loc, goff, fill, locrow, n_row_tiles)
    y = _experts(xs, te, tv, n_active, w1b, w3b, w2b)
    return _combine(y, pos, x1, mod3, final_g, seg, loc, goff, seq, final_norm)


def _pad_cols(a, width):
    return jnp.pad(a, ((0, 0), (0, width - a.shape[1])))


def kernel(x, c, w_ada, b_ada, norm1_g, w_in, b_in, gmlp_ln_g, gmlp_ln_b, gmlp_w_s, gmlp_b_s, gmlp_out_g, conv_w, conv_b, a_log_f, a_log_b, dt_bias_f, dt_bias_b, d_skip, ssd_norm_g, w_out, norm2_g, w_router_g, b_router_g, w_router_e, b_router_e, w1, w3, w2, final_g):
    bn, seq, _ = x.shape
    depth = w_ada.shape[0]
    x2 = x.reshape(bn * seq, D_MODEL)

    head_rows = jnp.arange(LANES)[:, None]
    head_cols = jnp.arange(SSD_WIDTH)[None, :] // SSD_HEAD_DIM
    expand = jnp.stack([head_rows == head_cols, head_rows == head_cols + SSD_HEADS]).astype(BF16)

    for l in range(depth):
        mod3 = _modulation(c, w_ada[l], b_ada[l]).reshape(bn, 1, N_MOD * D_MODEL)

        w_in_p = _pad_cols(w_in[l], IN_COLS_PAD).astype(BF16)
        b_in_p = _pad_cols(b_in[l][None, :], IN_COLS_PAD)
        bs = jnp.repeat(gmlp_b_s[l].T, LANES, axis=1)
        bs_tile = jnp.tile(bs, (TM_PROJ // CHUNK, 1))
        ya, z, xbc, dt = _inproj_gmlp(
            x2, mod3, norm1_g[l][None, :], w_in_p, b_in_p, gmlp_ln_g[l][None, :], gmlp_ln_b[l][None, :],
            gmlp_w_s[l].astype(BF16), bs_tile, gmlp_out_g[l][None, :], seq)

        conv_w8 = jnp.pad(conv_w[l], ((0, SUBLANES - CONV_WIDTH), (0, 0)))
        dtb_row = _pad_cols(jnp.concatenate([dt_bias_f[l], dt_bias_b[l]])[None, :], LANES)
        alog_row = _pad_cols(jnp.concatenate([a_log_f[l], a_log_b[l]])[None, :], LANES)
        dsk_row = jnp.repeat(d_skip[l], SSD_HEAD_DIM)[None, :]
        x2 = _ssd_outproj(xbc, dt, z, ya, x2, mod3, conv_w8, conv_b[l][None, :], dtb_row, alog_row, dsk_row,
                          ssd_norm_g[l][None, :], w_out[l].astype(BF16), expand, bn, seq)

        w_re = jnp.transpose(w_router_e[l], (1, 0, 2)).reshape(D_MODEL, N_EXPERTS)
        w_router = _pad_cols(jnp.concatenate([w_re, w_router_g[l]], axis=1), LANES)
        b_router = _pad_cols(jnp.concatenate([b_router_e[l].reshape(-1), b_router_g[l]])[None, :], LANES)
        x2 = _moe_final(x2, mod3, norm2_g[l][None, :], w_router, b_router, w1[l].astype(BF16),
                        w3[l].astype(BF16), w2[l].astype(BF16), final_g[None, :], seq,
                        final_norm=(l == depth - 1))
    return x2.reshape(bn, seq, D_MODEL)
```

```python
import functools
import math

import jax
import jax.numpy as jnp
from jax import lax
from jax.experimental import pallas as pl
from jax.experimental.pallas import tpu as pltpu

F32 = jnp.float32
BF16 = jnp.bfloat16
HIGHEST = lax.Precision.HIGHEST

D_MODEL = 1024
N_MOD = 6
GMLP_WIDTH = 1024
GMLP_HEADS = 8
CHUNK = 128
SSD_WIDTH = 1024
SSD_HEADS = 16
SSD_HEAD_DIM = 64
SSD_GROUPS = 2
SSD_STATE = 128
GROUP_WIDTH = SSD_WIDTH // SSD_GROUPS
CONV_WIDTH = 5
CONV_CH = SSD_WIDTH + 2 * SSD_GROUPS * SSD_STATE
N_EGROUPS = 4
EXPERTS_PER_GROUP = 8
N_EXPERTS = 32
D_EXPERT = 256
EPS = 1e-6

LANES = 128
SUBLANES = 8
COL_U, COL_V, COL_Z, COL_XBC, COL_DT = 0, 1024, 2048, 3072, 4608
IN_COLS = 4640
IN_COLS_PAD = COL_DT + LANES
TM_PROJ = 512
TM_MOE = 512
TR_EXPERT = 256
N_ASSIGN = 2
SEG_ALIGN = 2 * SUBLANES
SEG_SIZES = tuple(TM_MOE >> s for s in range(TM_MOE.bit_length()) if TM_MOE >> s >= SEG_ALIGN)
FILL_SIZES = tuple(s for s in SEG_SIZES if s < TR_EXPERT)
R_LOC = N_ASSIGN * TM_MOE + N_EXPERTS * SEG_ALIGN
VMEM_LIMIT = 56 * 1024 * 1024


def _silu(v):
    return v * jax.nn.sigmoid(v)


def _gelu(v):
    return 0.5 * v * (1.0 + lax.erf(v * math.sqrt(0.5)))


def _softplus(v):
    return jnp.maximum(v, 0.0) + jnp.log1p(jnp.exp(-jnp.abs(v)))


def _rms(v):
    return v * lax.rsqrt(jnp.mean(v * v, axis=-1, keepdims=True) + EPS)


def _mod_body(c_ref, w_ref, b_ref, o_ref):
    ca = _silu(c_ref[...])
    o_ref[...] = jnp.dot(ca, w_ref[...], precision=HIGHEST, preferred_element_type=F32) + b_ref[...]


def _modulation(c, w_ada, b_ada):
    bn = c.shape[0]
    return pl.pallas_call(
        _mod_body,
        grid=(N_MOD,),
        in_specs=[
            pl.BlockSpec((bn, D_MODEL), lambda j: (0, 0)),
            pl.BlockSpec((D_MODEL, D_MODEL), lambda j: (0, j)),
            pl.BlockSpec((1, D_MODEL), lambda j: (0, j)),
        ],
        out_specs=pl.BlockSpec((bn, D_MODEL), lambda j: (0, j)),
        out_shape=jax.ShapeDtypeStruct((bn, N_MOD * D_MODEL), F32),
        name="adaln_mod",
    )(c, w_ada, b_ada.reshape(1, -1))


def _inproj_body(x_ref, shift_ref, scale_ref, g_ref, w_ref, b_ref, lng_ref, lnb_ref, ws_ref, bs_ref,
                 og_ref, ya_ref, z_ref, xbc_ref, dt_ref, mix_scr):
    h = _rms(x_ref[...]) * g_ref[...]
    h = h * (1.0 + scale_ref[...]) + shift_ref[...]
    hb = h.astype(BF16)

    def proj(lo, hi):
        return jnp.dot(hb, w_ref[:, lo:hi], preferred_element_type=F32) + b_ref[:, lo:hi]

    z_ref[...] = proj(COL_Z, COL_XBC).astype(BF16)
    xbc_ref[...] = proj(COL_XBC, COL_DT)
    dt_ref[...] = proj(COL_DT, IN_COLS_PAD)

    v = _gelu(proj(COL_V, COL_Z))
    mu = jnp.mean(v, axis=-1, keepdims=True)
    vc = v - mu
    var = jnp.mean(vc * vc, axis=-1, keepdims=True)
    vn = (vc * lax.rsqrt(var + EPS) * lng_ref[...] + lnb_ref[...]).astype(BF16)
    n_chunks = TM_PROJ // CHUNK
    for hd in range(GMLP_HEADS):
        cols = slice(hd * LANES, (hd + 1) * LANES)
        rhs = jnp.concatenate([vn[c * CHUNK:(c + 1) * CHUNK, cols] for c in range(n_chunks)], axis=1)
        res = jnp.dot(ws_ref[hd], rhs, preferred_element_type=F32)
        for c in range(n_chunks):
            mix_scr[c * CHUNK:(c + 1) * CHUNK, cols] = res[:, c * LANES:(c + 1) * LANES]
    u = _gelu(proj(COL_U, COL_V))
    out = u * (mix_scr[...] + bs_ref[...])
    ya_ref[...] = (_rms(out) * og_ref[...]).astype(BF16)


def _inproj_gmlp(x2, mod3, norm1_g, w_in_p, b_in_p, ln_g, ln_b, w_s, bs_tile, out_g, seq):
    t = x2.shape[0]
    tiles_per_seq = seq // TM_PROJ
    row = lambda i: (i, 0)
    const2 = lambda i: (0, 0)
    return pl.pallas_call(
        _inproj_body,
        grid=(t // TM_PROJ,),
        in_specs=[
            pl.BlockSpec((TM_PROJ, D_MODEL), row),
            pl.BlockSpec((None, 1, D_MODEL), lambda i: (i // tiles_per_seq, 0, 0)),
            pl.BlockSpec((None, 1, D_MODEL), lambda i: (i // tiles_per_seq, 0, 1)),
            pl.BlockSpec((1, D_MODEL), const2),
            pl.BlockSpec((D_MODEL, IN_COLS_PAD), const2),
            pl.BlockSpec((1, IN_COLS_PAD), const2),
            pl.BlockSpec((1, GMLP_WIDTH), const2),
            pl.BlockSpec((1, GMLP_WIDTH), const2),
            pl.BlockSpec((GMLP_HEADS, CHUNK, CHUNK), lambda i: (0, 0, 0)),
            pl.BlockSpec((TM_PROJ, GMLP_WIDTH), const2),
            pl.BlockSpec((1, GMLP_WIDTH), const2),
        ],
        out_specs=[
            pl.BlockSpec((TM_PROJ, GMLP_WIDTH), row),
            pl.BlockSpec((TM_PROJ, SSD_WIDTH), row),
            pl.BlockSpec((TM_PROJ, CONV_CH), row),
            pl.BlockSpec((TM_PROJ, LANES), row),
        ],
        out_shape=[
            jax.ShapeDtypeStruct((t, GMLP_WIDTH), BF16),
            jax.ShapeDtypeStruct((t, SSD_WIDTH), BF16),
            jax.ShapeDtypeStruct((t, CONV_CH), F32),
            jax.ShapeDtypeStruct((t, LANES), F32),
        ],
        scratch_shapes=[pltpu.VMEM((TM_PROJ, GMLP_WIDTH), F32)],
        compiler_params=pltpu.CompilerParams(
            dimension_semantics=("arbitrary",), vmem_limit_bytes=VMEM_LIMIT),
        name="inproj_gmlp",
    )(x2, mod3, mod3, norm1_g, w_in_p, b_in_p, ln_g, ln_b, w_s, bs_tile, out_g)


def _ssd_chunk(act, dtv, a_row, expand, st_scr, rev):
    off = SSD_HEADS if rev else 0
    row = lax.broadcasted_iota(jnp.int32, (CHUNK, CHUNK), 0)
    col = lax.broadcasted_iota(jnp.int32, (CHUNK, CHUNK), 1)
    lower = row >= col
    upper = row <= col
    keep = upper if rev else lower
    keep_t = lower if rev else upper
    da = dtv * a_row
    cs = jnp.dot(keep.astype(F32), da, precision=HIGHEST, preferred_element_type=F32)
    cs_t = jnp.dot(da.T, keep_t.astype(F32), precision=HIGHEST, preferred_element_type=F32)
    dt_t = dtv.T
    tot = cs[0:1, :] if rev else cs[CHUNK - 1:CHUNK, :]

    xs = act[:, :SSD_WIDTH]
    lane = lax.broadcasted_iota(jnp.int32, (CHUNK, LANES), 1)
    first_half = lane < SSD_HEAD_DIM
    zero = jnp.zeros((), BF16)

    stack = jnp.concatenate(
        [jnp.exp(cs), dtv * jnp.exp(tot - cs), jnp.broadcast_to(jnp.exp(tot), (SUBLANES, LANES))], axis=0)
    stack_x = jnp.dot(stack.astype(BF16), expand, preferred_element_type=F32)
    into_x = stack_x[:CHUNK]
    w_x = stack_x[CHUNK:2 * CHUNK]
    cd_x = stack_x[2 * CHUNK:2 * CHUNK + 1]
    xw = (xs.astype(F32) * w_x).astype(BF16)

    pieces = []
    for g in range(SSD_GROUPS):
        bg = act[:, SSD_WIDTH + g * SSD_STATE:SSD_WIDTH + (g + 1) * SSD_STATE]
        cg = act[:, SSD_WIDTH + SSD_GROUPS * SSD_STATE + g * SSD_STATE:
                 SSD_WIDTH + SSD_GROUPS * SSD_STATE + (g + 1) * SSD_STATE]
        scores = lax.dot_general(cg, bg, (((1,), (1,)), ((), ())), preferred_element_type=F32)
        heads_per_group = SSD_HEADS // SSD_GROUPS
        for pair in range(heads_per_group // 2):
            h0 = g * heads_per_group + 2 * pair
            xs_pair = xs[:, h0 * SSD_HEAD_DIM:(h0 + 2) * SSD_HEAD_DIM]
            y_pair = None
            for k in range(2):
                hh = off + h0 + k
                seg = cs[:, hh:hh + 1] - cs_t[hh:hh + 1, :]
                dec = jnp.exp(jnp.where(keep, seg, -jnp.inf))
                m = (scores * dec * dt_t[hh:hh + 1, :]).astype(BF16)
                rhs = jnp.where(first_half if k == 0 else jnp.logical_not(first_half), xs_pair, zero)
                part = jnp.dot(m, rhs, preferred_element_type=F32)
                y_pair = part if y_pair is None else y_pair + part
            pieces.append(y_pair)
    y_diag = jnp.concatenate(pieces, axis=1)

    y_off = []
    for g in range(SSD_GROUPS):
        gcols = slice(g * GROUP_WIDTH, (g + 1) * GROUP_WIDTH)
        bg = act[:, SSD_WIDTH + g * SSD_STATE:SSD_WIDTH + (g + 1) * SSD_STATE]
        cg = act[:, SSD_WIDTH + SSD_GROUPS * SSD_STATE + g * SSD_STATE:
                 SSD_WIDTH + SSD_GROUPS * SSD_STATE + (g + 1) * SSD_STATE]
        prev = st_scr[:, gcols]
        y_off.append(jnp.dot(cg, prev.astype(BF16), preferred_element_type=F32))
        bg_t = bg.astype(F32).T.astype(BF16)
        new = jnp.dot(bg_t, xw[:, gcols], preferred_element_type=F32)
        st_scr[:, gcols] = prev * cd_x[:, gcols] + new
    return y_diag + jnp.concatenate(y_off, axis=1) * into_x


def _ssd_body(xbc_ref, xprev_ref, xnext_ref, dt_ref, z_ref, ya_ref, x_ref, gate_ref, cw_ref, cb_ref,
              dtb_ref, alog_ref, dsk_ref, ng_ref, wout_ref, exp_ref, o_ref,
              act_scr, dts_scr, yf_scr, st_scr, ext_scr, *, n_chunks):
    d = pl.program_id(1)
    c = pl.program_id(2)
    lane1 = lax.broadcasted_iota(jnp.int32, (1, LANES), 1)
    a_row = jnp.where(lane1 < 2 * SSD_HEADS, -jnp.exp(alog_ref[...]), 0.0)

    @pl.when(c == 0)
    def _():
        st_scr[...] = jnp.zeros_like(st_scr)

    @pl.when(d == 0)
    def _forward():
        row0 = pl.multiple_of(c * CHUNK, CHUNK)
        ext_scr[0:SUBLANES, :] = jnp.where(c > 0, xprev_ref[...], 0.0)
        ext_scr[SUBLANES:SUBLANES + CHUNK, :] = xbc_ref[...]
        ext_scr[SUBLANES + CHUNK:, :] = jnp.where(c < n_chunks - 1, xnext_ref[...], 0.0)
        acc = cb_ref[...]
        for k in range(CONV_WIDTH):
            lo = SUBLANES - CONV_WIDTH // 2 + k
            acc = acc + cw_ref[k:k + 1, :] * ext_scr[lo:lo + CHUNK, :]
        act = _silu(acc).astype(BF16)
        dtv = _softplus(dt_ref[...] + dtb_ref[...])
        act_scr[pl.ds(row0, CHUNK), :] = act
        dts_scr[pl.ds(row0, CHUNK), :] = dtv
        y = _ssd_chunk(act, dtv, a_row, exp_ref[0], st_scr, rev=False)
        yf_scr[pl.ds(row0, CHUNK), :] = y

    @pl.when(d == 1)
    def _backward():
        row0 = pl.multiple_of((n_chunks - 1 - c) * CHUNK, CHUNK)
        act = act_scr[pl.ds(row0, CHUNK), :]
        dtv = dts_scr[pl.ds(row0, CHUNK), :]
        yb = _ssd_chunk(act, dtv, a_row, exp_ref[1], st_scr, rev=True)
        xs = act[:, :SSD_WIDTH].astype(F32)
        y = yf_scr[pl.ds(row0, CHUNK), :] + yb + dsk_ref[...] * xs
        y = y * _silu(z_ref[...].astype(F32))
        y = jnp.concatenate(
            [_rms(y[:, g * GROUP_WIDTH:(g + 1) * GROUP_WIDTH]) for g in range(SSD_GROUPS)], axis=1)
        y = y * ng_ref[...]
        mix = jnp.concatenate([ya_ref[...], y.astype(BF16)], axis=1)
        o = jnp.dot(mix, wout_ref[...], preferred_element_type=F32)
        o_ref[...] = x_ref[...] + gate_ref[...] * o


def _ssd_outproj(xbc, dt, z, ya, x2, mod3, conv_w8, conv_b, dtb_row, alog_row, dsk_row, norm_g, w_out_b,
                 expand, bn, seq):
    t = x2.shape[0]
    nc = seq // CHUNK
    blocks8 = CHUNK // SUBLANES
    last8 = t // SUBLANES - 1

    def fwd_chunk(b, d, c):
        return b * nc + c + d * (nc - 1 - c)

    def bwd_chunk(b, d, c):
        return b * nc + nc - 1 - d * c

    const2 = lambda b, d, c: (0, 0)
    return pl.pallas_call(
        functools.partial(_ssd_body, n_chunks=nc),
        grid=(bn, 2, nc),
        in_specs=[
            pl.BlockSpec((CHUNK, CONV_CH), lambda b, d, c: (fwd_chunk(b, d, c), 0)),
            pl.BlockSpec((SUBLANES, CONV_CH),
                         lambda b, d, c: (jnp.maximum(fwd_chunk(b, d, c) * blocks8 - 1, 0), 0)),
            pl.BlockSpec((SUBLANES, CONV_CH),
                         lambda b, d, c: (jnp.minimum((fwd_chunk(b, d, c) + 1) * blocks8, last8), 0)),
            pl.BlockSpec((CHUNK, LANES), lambda b, d, c: (fwd_chunk(b, d, c), 0)),
            pl.BlockSpec((CHUNK, SSD_WIDTH), lambda b, d, c: (bwd_chunk(b, d, c), 0)),
            pl.BlockSpec((CHUNK, GMLP_WIDTH), lambda b, d, c: (bwd_chunk(b, d, c), 0)),
            pl.BlockSpec((CHUNK, D_MODEL), lambda b, d, c: (bwd_chunk(b, d, c), 0)),
            pl.BlockSpec((None, 1, D_MODEL), lambda b, d, c: (b, 0, 2)),
            pl.BlockSpec((SUBLANES, CONV_CH), const2),
            pl.BlockSpec((1, CONV_CH), const2),
            pl.BlockSpec((1, LANES), const2),
            pl.BlockSpec((1, LANES), const2),
            pl.BlockSpec((1, SSD_WIDTH), const2),
            pl.BlockSpec((1, SSD_WIDTH), const2),
            pl.BlockSpec((GMLP_WIDTH + SSD_WIDTH, D_MODEL), const2),
            pl.BlockSpec((2, LANES, SSD_WIDTH), lambda b, d, c: (0, 0, 0)),
        ],
        out_specs=pl.BlockSpec((CHUNK, D_MODEL), lambda b, d, c: (bwd_chunk(b, d, c), 0)),
        out_shape=jax.ShapeDtypeStruct((t, D_MODEL), F32),
        scratch_shapes=[
            pltpu.VMEM((seq, CONV_CH), BF16),
            pltpu.VMEM((seq, LANES), F32),
            pltpu.VMEM((seq, SSD_WIDTH), F32),
            pltpu.VMEM((SSD_STATE, SSD_WIDTH), F32),
            pltpu.VMEM((CHUNK + 2 * SUBLANES, CONV_CH), F32),
        ],
        compiler_params=pltpu.CompilerParams(
            dimension_semantics=("arbitrary", "arbitrary", "arbitrary"), vmem_limit_bytes=VMEM_LIMIT),
        name="ssd_outproj",
    )(xbc, xbc, xbc, dt, z, ya, x2, mod3, conv_w8, conv_b, dtb_row, alog_row, dsk_row, norm_g, w_out_b,
      expand)


def _route(logits):
    lane = lax.broadcasted_iota(jnp.int32, logits.shape, 1)
    big = jnp.int32(LANES)
    neg = -jnp.inf
    gmask = (lane >= N_EXPERTS) & (lane < N_EXPERTS + N_EGROUPS)
    gl = jnp.where(gmask, logits, neg)
    gmax = jnp.max(gl, axis=-1, keepdims=True)
    gidx = jnp.min(jnp.where(gl == gmax, lane, big), axis=-1, keepdims=True) - N_EXPERTS
    p_g = 1.0 / jnp.sum(jnp.where(gmask, jnp.exp(gl - gmax), 0.0), axis=-1, keepdims=True)
    lo = gidx * EXPERTS_PER_GROUP
    emask = (lane >= lo) & (lane < lo + EXPERTS_PER_GROUP)
    el = jnp.where(emask, logits, neg)
    v1 = jnp.max(el, axis=-1, keepdims=True)
    i1 = jnp.min(jnp.where(el == v1, lane, big), axis=-1, keepdims=True)
    el2 = jnp.where(lane == i1, neg, el)
    v2 = jnp.max(el2, axis=-1, keepdims=True)
    i2 = jnp.min(jnp.where(el2 == v2, lane, big), axis=-1, keepdims=True)
    e2 = jnp.exp(v2 - v1)
    den = 1.0 + e2
    w1 = p_g / den
    w2 = p_g * e2 / den
    return i1, i2, w1, w2


def _router_body(x_ref, shift_ref, scale_ref, g2_ref, wr_ref, br_ref, h_ref, rt_ref, cnt_ref):
    h = _rms(x_ref[...]) * g2_ref[...]
    h = h * (1.0 + scale_ref[...]) + shift_ref[...]
    hb = h.astype(BF16)
    h_ref[...] = hb
    h_lo = (h - hb.astype(F32)).astype(BF16)
    logits = (jnp.dot(hb, wr_ref[0], preferred_element_type=F32)
              + jnp.dot(h_lo, wr_ref[0], preferred_element_type=F32)
              + jnp.dot(hb, wr_ref[1], preferred_element_type=F32)) + br_ref[...]
    i1, i2, w1, w2 = _route(logits)
    lane = lax.broadcasted_iota(jnp.int32, (TM_MOE, LANES), 1)
    rt_ref[...] = jnp.where(lane == 0, i1.astype(F32), jnp.where(lane == 1, i2.astype(F32),
                            jnp.where(lane == 2, w1, jnp.where(lane == 3, w2, 0.0))))
    mask = jnp.where(lane == i1, 1.0, jnp.where(lane == i2, 1.0, 0.0))
    cnt_ref[...] = jnp.broadcast_to(jnp.sum(mask, axis=0, keepdims=True), (SUBLANES, LANES))


def _router(x1, mod3, norm2_g, w_router, b_router, seq):
    t = x1.shape[0]
    tiles_per_seq = seq // TM_MOE
    n_tiles = t // TM_MOE
    row = lambda i: (i, 0)
    const2 = lambda i: (0, 0)
    return pl.pallas_call(
        _router_body,
        grid=(n_tiles,),
        in_specs=[
            pl.BlockSpec((TM_MOE, D_MODEL), row),
            pl.BlockSpec((None, 1, D_MODEL), lambda i: (i // tiles_per_seq, 0, 3)),
            pl.BlockSpec((None, 1, D_MODEL), lambda i: (i // tiles_per_seq, 0, 4)),
            pl.BlockSpec((1, D_MODEL), const2),
            pl.BlockSpec((2, D_MODEL, LANES), lambda i: (0, 0, 0)),
            pl.BlockSpec((1, LANES), const2),
        ],
        out_specs=[
            pl.BlockSpec((TM_MOE, D_MODEL), row),
            pl.BlockSpec((TM_MOE, LANES), row),
            pl.BlockSpec((SUBLANES, LANES), row),
        ],
        out_shape=[
            jax.ShapeDtypeStruct((t, D_MODEL), BF16),
            jax.ShapeDtypeStruct((t, LANES), F32),
            jax.ShapeDtypeStruct((n_tiles * SUBLANES, LANES), F32),
        ],
        compiler_params=pltpu.CompilerParams(
            dimension_semantics=("arbitrary",), vmem_limit_bytes=VMEM_LIMIT),
        name="moe_router",
    )(x1, mod3, mod3, norm2_g, w_router, b_router)


def _copy_rows(n, vmem_buf, v0, hbm_ref, g0, sem, sizes, *, to_hbm, wait):
    for size in sizes:
        done = (n // (2 * size)) * (2 * size)

        @pl.when((n & size) != 0)
        def _():
            v_start = 0 if v0 is None else pl.multiple_of(v0 + done, SEG_ALIGN)
            v = vmem_buf.at[pl.ds(v_start, size), :]
            g = hbm_ref.at[pl.ds(pl.multiple_of(g0 + done, SEG_ALIGN), size), :]
            cp = pltpu.make_async_copy(v, g, sem) if to_hbm else pltpu.make_async_copy(g, v, sem)
            if wait:
                cp.wait()
            else:
                cp.start()


def _segment_copies(seg_ref, loc_ref, goff_ref, tile, vmem_buf, hbm_ref, sem, *, to_hbm, wait):
    base = tile * N_EXPERTS

    def per_expert(e, carry):
        _copy_rows(seg_ref[base + e], vmem_buf, loc_ref[base + e], hbm_ref, goff_ref[base + e], sem,
                   SEG_SIZES, to_hbm=to_hbm, wait=wait)
        return carry

    lax.fori_loop(0, N_EXPERTS, per_expert, 0)


def _dispatch_body(seg_ref, loc_ref, goff_ref, fill_ref, h_ref, rt_ref, locrow_ref, pos_ref, xs_hbm, buf, sem,
                   *, n_tiles, n_row_tiles):
    i = pl.program_id(0)
    slot = i % 2
    copies = functools.partial(_segment_copies, seg_ref, loc_ref, goff_ref, hbm_ref=xs_hbm, to_hbm=True)

    @pl.when(i >= 2)
    def _():
        copies(i - 2, buf.at[slot], sem=sem.at[slot], wait=True)

    rt = rt_ref[...]
    lane = lax.broadcasted_iota(jnp.int32, (TM_MOE, LANES), 1)
    lanef = lane.astype(F32)
    e1 = rt[:, 0:1]
    e2 = rt[:, 1:2]
    mask = jnp.where(lanef == e1, 1.0, jnp.where(lanef == e2, 1.0, 0.0))
    r = lax.broadcasted_iota(jnp.int32, (TM_MOE, TM_MOE), 0)
    cc = lax.broadcasted_iota(jnp.int32, (TM_MOE, TM_MOE), 1)
    strict = jnp.where(r > cc, 1.0, 0.0).astype(BF16)
    rank = jnp.dot(strict, mask.astype(BF16), preferred_element_type=F32)
    posall = rank + locrow_ref[...]
    pos1 = jnp.sum(jnp.where(lanef == e1, posall, 0.0), axis=-1, keepdims=True)
    pos2 = jnp.sum(jnp.where(lanef == e2, posall, 0.0), axis=-1, keepdims=True)
    pp = jnp.where(lane == 0, pos1, jnp.where(lane == 1, pos2, rt))
    pos_ref[...] = pp

    rowp = lax.broadcasted_iota(jnp.int32, (R_LOC, LANES), 0).astype(F32)
    blocks = []
    for k in range(TM_MOE // LANES):
        ppt = pp[k * LANES:(k + 1) * LANES, :].T
        blocks.append(jnp.where(rowp == ppt[0:1, :], 1.0, jnp.where(rowp == ppt[1:2, :], 1.0, 0.0)).astype(BF16))
    perm = jnp.concatenate(blocks, axis=1)
    buf[slot] = jnp.dot(perm, h_ref[...], preferred_element_type=F32).astype(BF16)
    copies(i, buf.at[slot], sem=sem.at[slot], wait=False)

    @pl.when(i == n_tiles - 1)
    def _():
        if n_tiles > 1:
            copies(i - 1, buf.at[1 - slot], sem=sem.at[1 - slot], wait=True)
        copies(i, buf.at[slot], sem=sem.at[slot], wait=True)
        zsrc = buf.at[1 - slot]
        zsem = sem.at[1 - slot]
        zsrc[0:TR_EXPERT, :] = jnp.zeros((TR_EXPERT, D_MODEL), BF16)
        for wait in (False, True):
            def per_expert(e, carry):
                _copy_rows(fill_ref[N_EXPERTS + e], zsrc, None, xs_hbm, fill_ref[e], zsem, FILL_SIZES,
                           to_hbm=True, wait=wait)
                return carry

            def per_row_tile(k, carry):
                g = xs_hbm.at[pl.ds(pl.multiple_of(k * TR_EXPERT, TR_EXPERT), TR_EXPERT), :]
                cp = pltpu.make_async_copy(zsrc.at[pl.ds(0, TR_EXPERT), :], g, zsem)
                if wait:
                    cp.wait()
                else:
                    cp.start()
                return carry

            lax.fori_loop(0, N_EXPERTS, per_expert, 0)
            lax.fori_loop(fill_ref[2 * N_EXPERTS], n_row_tiles, per_row_tile, 0)


def _dispatch(h2, rt, seg, loc, goff, fill, locrow, n_row_tiles):
    t = h2.shape[0]
    n_tiles = t // TM_MOE
    row = lambda i, *_: (i, 0)
    grid_spec = pltpu.PrefetchScalarGridSpec(
        num_scalar_prefetch=4,
        grid=(n_tiles,),
        in_specs=[
            pl.BlockSpec((TM_MOE, D_MODEL), row),
            pl.BlockSpec((TM_MOE, LANES), row),
            pl.BlockSpec((None, 1, LANES), lambda i, *_: (i, 0, 0)),
        ],
        out_specs=[
            pl.BlockSpec((TM_MOE, LANES), row),
            pl.BlockSpec(memory_space=pl.ANY),
        ],
        scratch_shapes=[
            pltpu.VMEM((2, R_LOC, D_MODEL), BF16),
            pltpu.SemaphoreType.DMA((2,)),
        ],
    )
    return pl.pallas_call(
        functools.partial(_dispatch_body, n_tiles=n_tiles, n_row_tiles=n_row_tiles),
        grid_spec=grid_spec,
        out_shape=[
            jax.ShapeDtypeStruct((t, LANES), F32),
            jax.ShapeDtypeStruct((n_row_tiles * TR_EXPERT, D_MODEL), BF16),
        ],
        compiler_params=pltpu.CompilerParams(
            dimension_semantics=("arbitrary",), vmem_limit_bytes=VMEM_LIMIT),
        name="moe_dispatch",
    )(seg, loc, goff, fill, h2, rt, locrow)


def _expert_body(te_ref, na_ref, xs_ref, w1_ref, w3_ref, w2_ref, y_ref):
    r = pl.program_id(0)

    @pl.when(r < na_ref[0])
    def _():
        x = xs_ref[...]
        a = _silu(jnp.dot(x, w1_ref[...], preferred_element_type=F32))
        a = a * jnp.dot(x, w3_ref[...], preferred_element_type=F32)
        y_ref[...] = jnp.dot(a.astype(BF16), w2_ref[...], preferred_element_type=F32).astype(BF16)

    @pl.when(r >= na_ref[0])
    def _():
        y_ref[...] = jnp.zeros_like(y_ref)


def _experts(xs, tile_expert, n_active, w1b, w3b, w2b):
    n_rows = xs.shape[0]
    grid_spec = pltpu.PrefetchScalarGridSpec(
        num_scalar_prefetch=2,
        grid=(n_rows // TR_EXPERT,),
        in_specs=[
            pl.BlockSpec((TR_EXPERT, D_MODEL), lambda r, te, na: (jnp.minimum(r, na[0] - 1), 0)),
            pl.BlockSpec((None, D_MODEL, D_EXPERT), lambda r, te, na: (te[r], 0, 0)),
            pl.BlockSpec((None, D_MODEL, D_EXPERT), lambda r, te, na: (te[r], 0, 0)),
            pl.BlockSpec((None, D_EXPERT, D_MODEL), lambda r, te, na: (te[r], 0, 0)),
        ],
        out_specs=pl.BlockSpec((TR_EXPERT, D_MODEL), lambda r, te, na: (r, 0)),
    )
    return pl.pallas_call(
        _expert_body,
        grid_spec=grid_spec,
        out_shape=jax.ShapeDtypeStruct((n_rows, D_MODEL), BF16),
        compiler_params=pltpu.CompilerParams(
            dimension_semantics=("arbitrary",), vmem_limit_bytes=VMEM_LIMIT),
        name="moe_experts",
    )(tile_expert, n_active, xs, w1b, w3b, w2b)


def _combine_body(seg_ref, loc_ref, goff_ref, y_hbm, pos_ref, x_ref, gate_ref, fg_ref, o_ref, buf, sem, *,
                  n_tiles, final_norm):
    i = pl.program_id(0)
    slot = i % 2
    copies = functools.partial(_segment_copies, seg_ref, loc_ref, goff_ref, hbm_ref=y_hbm, to_hbm=False)

    @pl.when(i == 0)
    def _():
        buf[...] = jnp.zeros_like(buf)
        copies(0, buf.at[0], sem=sem.at[0], wait=False)

    @pl.when(i + 1 < n_tiles)
    def _():
        copies(i + 1, buf.at[1 - slot], sem=sem.at[1 - slot], wait=False)

    pp = pos_ref[...]
    colp = lax.broadcasted_iota(jnp.int32, (TM_MOE, R_LOC), 1).astype(F32)
    wc = jnp.where(colp == pp[:, 0:1], pp[:, 2:3], jnp.where(colp == pp[:, 1:2], pp[:, 3:4], 0.0)).astype(BF16)
    copies(i, buf.at[slot], sem=sem.at[slot], wait=True)
    moe = jnp.dot(wc, buf[slot], preferred_element_type=F32)
    x2 = x_ref[...] + gate_ref[...] * moe
    o_ref[...] = _rms(x2) * fg_ref[...] if final_norm else x2


def _combine(y, pos, x1, mod3, final_g, seg, loc, goff, seq, final_norm):
    t = x1.shape[0]
    n_tiles = t // TM_MOE
    tiles_per_seq = seq // TM_MOE
    row = lambda i, *_: (i, 0)
    grid_spec = pltpu.PrefetchScalarGridSpec(
        num_scalar_prefetch=3,
        grid=(n_tiles,),
        in_specs=[
            pl.BlockSpec(memory_space=pl.ANY),
            pl.BlockSpec((TM_MOE, LANES), row),
            pl.BlockSpec((TM_MOE, D_MODEL), row),
            pl.BlockSpec((None, 1, D_MODEL), lambda i, *_: (i // tiles_per_seq, 0, 5)),
            pl.BlockSpec((1, D_MODEL), lambda i, *_: (0, 0)),
        ],
        out_specs=pl.BlockSpec((TM_MOE, D_MODEL), row),
        scratch_shapes=[
            pltpu.VMEM((2, R_LOC, D_MODEL), BF16),
            pltpu.SemaphoreType.DMA((2,)),
        ],
    )
    return pl.pallas_call(
        functools.partial(_combine_body, n_tiles=n_tiles, final_norm=final_norm),
        grid_spec=grid_spec,
        out_shape=jax.ShapeDtypeStruct((t, D_MODEL), F32),
        compiler_params=pltpu.CompilerParams(
            dimension_semantics=("arbitrary",), vmem_limit_bytes=VMEM_LIMIT),
        name="moe_combine",
    )(seg, loc, goff, y, pos, x1, mod3, final_g)


def _moe_plan(counts, n_tiles, n_row_tiles):
    cnt = counts.reshape(n_tiles, SUBLANES, LANES)[:, 0, :N_EXPERTS].astype(jnp.int32)
    seg = (cnt + SEG_ALIGN - 1) // SEG_ALIGN * SEG_ALIGN
    before_e = jnp.arange(N_EXPERTS)[:, None] < jnp.arange(N_EXPERTS)[None, :]
    before_t = jnp.arange(n_tiles)[:, None] > jnp.arange(n_tiles)[None, :]
    loc = jnp.sum(jnp.where(before_e[None], seg[:, :, None], 0), axis=1)
    tot = jnp.sum(seg, axis=0)
    region = (tot + TR_EXPERT - 1) // TR_EXPERT * TR_EXPERT
    gstart = jnp.sum(jnp.where(before_e, region[:, None], 0), axis=0)
    gend = gstart + region
    goff = gstart[None, :] + jnp.sum(jnp.where(before_t[:, :, None], seg[None], 0), axis=1)
    n_active = gend[-1] // TR_EXPERT
    tile_row0 = jnp.arange(n_row_tiles, dtype=jnp.int32) * TR_EXPERT
    last_row0 = (n_active - 1) * TR_EXPERT
    te = jnp.sum(gend[None, :] <= jnp.minimum(tile_row0, last_row0)[:, None], axis=1).astype(jnp.int32)
    te = jnp.minimum(te, N_EXPERTS - 1)
    locrow = _pad_cols(loc.astype(F32), LANES).reshape(n_tiles, 1, LANES)
    n_active = n_active.reshape(1).astype(jnp.int32)
    fill = jnp.concatenate([gstart + tot, region - tot, n_active]).astype(jnp.int32)
    return (seg.reshape(-1), loc.reshape(-1).astype(jnp.int32), goff.reshape(-1).astype(jnp.int32), fill,
            locrow, te, n_active)


def _moe_final(x1, mod3, norm2_g, w_router, b_router, w1b, w3b, w2b, final_g, seq, final_norm):
    t = x1.shape[0]
    n_tiles = t // TM_MOE
    max_rows = N_ASSIGN * t + (SEG_ALIGN - 1) * N_EXPERTS * n_tiles + N_EXPERTS * (TR_EXPERT - SEG_ALIGN)
    n_row_tiles = -(-max_rows // TR_EXPERT)
    h2, rt, counts = _router(x1, mod3, norm2_g, w_router, b_router, seq)
    seg, loc, goff, fill, locrow, te, n_active = _moe_plan(counts, n_tiles, n_row_tiles)
    pos, xs = _dispatch(h2, rt, seg, loc, goff, fill, locrow, n_row_tiles)
    y = _experts(xs, te, n_active, w1b, w3b, w2b)
    return _combine(y, pos, x1, mod3, final_g, seg, loc, goff, seq, final_norm)


def _pad_cols(a, width):
    return jnp.pad(a, ((0, 0), (0, width - a.shape[1])))


def kernel(x, c, w_ada, b_ada, norm1_g, w_in, b_in, gmlp_ln_g, gmlp_ln_b, gmlp_w_s, gmlp_b_s, gmlp_out_g, conv_w, conv_b, a_log_f, a_log_b, dt_bias_f, dt_bias_b, d_skip, ssd_norm_g, w_out, norm2_g, w_router_g, b_router_g, w_router_e, b_router_e, w1, w3, w2, final_g):
    bn, seq, _ = x.shape
    depth = w_ada.shape[0]
    x2 = x.reshape(bn * seq, D_MODEL)

    head_rows = jnp.arange(LANES)[:, None]
    head_cols = jnp.arange(SSD_WIDTH)[None, :] // SSD_HEAD_DIM
    expand = jnp.stack([head_rows == head_cols, head_rows == head_cols + SSD_HEADS]).astype(BF16)

    for l in range(depth):
        mod3 = _modulation(c, w_ada[l], b_ada[l]).reshape(bn, 1, N_MOD * D_MODEL)

        w_in_p = _pad_cols(w_in[l], IN_COLS_PAD).astype(BF16)
        b_in_p = _pad_cols(b_in[l][None, :], IN_COLS_PAD)
        bs = jnp.repeat(gmlp_b_s[l].T, LANES, axis=1)
        bs_tile = jnp.tile(bs, (TM_PROJ // CHUNK, 1))
        ya, z, xbc, dt = _inproj_gmlp(
            x2, mod3, norm1_g[l][None, :], w_in_p, b_in_p, gmlp_ln_g[l][None, :], gmlp_ln_b[l][None, :],
            gmlp_w_s[l].astype(BF16), bs_tile, gmlp_out_g[l][None, :], seq)

        conv_w8 = jnp.pad(conv_w[l], ((0, SUBLANES - CONV_WIDTH), (0, 0)))
        dtb_row = _pad_cols(jnp.concatenate([dt_bias_f[l], dt_bias_b[l]])[None, :], LANES)
        alog_row = _pad_cols(jnp.concatenate([a_log_f[l], a_log_b[l]])[None, :], LANES)
        dsk_row = jnp.repeat(d_skip[l], SSD_HEAD_DIM)[None, :]
        x2 = _ssd_outproj(xbc, dt, z, ya, x2, mod3, conv_w8, conv_b[l][None, :], dtb_row, alog_row, dsk_row,
                          ssd_norm_g[l][None, :], w_out[l].astype(BF16), expand, bn, seq)

        w_re = jnp.transpose(w_router_e[l], (1, 0, 2)).reshape(D_MODEL, N_EXPERTS)
        w_router = _pad_cols(jnp.concatenate([w_re, w_router_g[l]], axis=1), LANES)
        w_router_hi = w_router.astype(BF16)
        w_router = jnp.stack([w_router_hi, (w_router - w_router_hi.astype(F32)).astype(BF16)])
        b_router = _pad_cols(jnp.concatenate([b_router_e[l].reshape(-1), b_router_g[l]])[None, :], LANES)
        x2 = _moe_final(x2, mod3, norm2_g[l][None, :], w_router, b_router, w1[l].astype(BF16),
                        w3[l].astype(BF16), w2[l].astype(BF16), final_g[None, :], seq,
                        final_norm=(l == depth - 1))
    return x2.reshape(bn, seq, D_MODEL)
```

```python
import functools
import math

import jax
import jax.numpy as jnp
from jax import lax
from jax.experimental import pallas as pl
from jax.experimental.pallas import tpu as pltpu

F32 = jnp.float32
BF16 = jnp.bfloat16
HIGHEST = lax.Precision.HIGHEST

D_MODEL = 1024
N_MOD = 6
GMLP_WIDTH = 1024
GMLP_HEADS = 8
CHUNK = 128
SSD_WIDTH = 1024
SSD_HEADS = 16
SSD_HEAD_DIM = 64
SSD_GROUPS = 2
SSD_STATE = 128
GROUP_WIDTH = SSD_WIDTH // SSD_GROUPS
CONV_WIDTH = 5
CONV_CH = SSD_WIDTH + 2 * SSD_GROUPS * SSD_STATE
N_EGROUPS = 4
EXPERTS_PER_GROUP = 8
N_EXPERTS = 32
D_EXPERT = 256
EPS = 1e-6

LANES = 128
SUBLANES = 8
COL_U, COL_V, COL_Z, COL_XBC, COL_DT = 0, 1024, 2048, 3072, 4608
IN_COLS = 4640
IN_COLS_PAD = COL_DT + LANES
TM_PROJ = 512
TM_MOE = 512
TR_EXPERT = 256
EXPERT_TILES_PER_STEP = 2
N_ASSIGN = 2
SEG_ALIGN = 2 * SUBLANES
R_LOC = N_ASSIGN * TM_MOE + N_EXPERTS * SEG_ALIGN
PIECE_ROWS = 128
REM_SIZES = tuple(PIECE_ROWS >> s for s in range(1, PIECE_ROWS.bit_length()) if PIECE_ROWS >> s >= SEG_ALIGN)
TOTAL_SIZES = tuple(1 << s for s in range(R_LOC.bit_length() - 1, -1, -1) if 1 << s >= SEG_ALIGN)
VMEM_LIMIT = 56 * 1024 * 1024


def _silu(v):
    return v * jax.nn.sigmoid(v)


def _gelu(v):
    return 0.5 * v * (1.0 + lax.erf(v * math.sqrt(0.5)))


def _softplus(v):
    return jnp.maximum(v, 0.0) + jnp.log1p(jnp.exp(-jnp.abs(v)))


def _rms(v):
    return v * lax.rsqrt(jnp.mean(v * v, axis=-1, keepdims=True) + EPS)


def _mod_body(c_ref, w_ref, b_ref, o_ref):
    ca = _silu(c_ref[...])
    o_ref[...] = jnp.dot(ca, w_ref[...], precision=HIGHEST, preferred_element_type=F32) + b_ref[...]


def _modulation(c, w_ada, b_ada):
    bn = c.shape[0]
    return pl.pallas_call(
        _mod_body,
        grid=(N_MOD,),
        in_specs=[
            pl.BlockSpec((bn, D_MODEL), lambda j: (0, 0)),
            pl.BlockSpec((D_MODEL, D_MODEL), lambda j: (0, j)),
            pl.BlockSpec((1, D_MODEL), lambda j: (0, j)),
        ],
        out_specs=pl.BlockSpec((bn, D_MODEL), lambda j: (0, j)),
        out_shape=jax.ShapeDtypeStruct((bn, N_MOD * D_MODEL), F32),
        name="adaln_mod",
    )(c, w_ada, b_ada.reshape(1, -1))


def _inproj_body(x_ref, shift_ref, scale_ref, g_ref, w_ref, b_ref, lng_ref, lnb_ref, ws_ref, bs_ref,
                 og_ref, ya_ref, z_ref, xbc_ref, dt_ref, mix_scr):
    h = _rms(x_ref[...]) * g_ref[...]
    h = h * (1.0 + scale_ref[...]) + shift_ref[...]
    hb = h.astype(BF16)

    def proj(lo, hi):
        return jnp.dot(hb, w_ref[:, lo:hi], preferred_element_type=F32) + b_ref[:, lo:hi]

    z_ref[...] = proj(COL_Z, COL_XBC).astype(BF16)
    xbc_ref[...] = proj(COL_XBC, COL_DT)
    dt_ref[...] = proj(COL_DT, IN_COLS_PAD)

    v = _gelu(proj(COL_V, COL_Z))
    mu = jnp.mean(v, axis=-1, keepdims=True)
    vc = v - mu
    var = jnp.mean(vc * vc, axis=-1, keepdims=True)
    vn = (vc * lax.rsqrt(var + EPS) * lng_ref[...] + lnb_ref[...]).astype(BF16)
    n_chunks = TM_PROJ // CHUNK
    for hd in range(GMLP_HEADS):
        cols = slice(hd * LANES, (hd + 1) * LANES)
        rhs = jnp.concatenate([vn[c * CHUNK:(c + 1) * CHUNK, cols] for c in range(n_chunks)], axis=1)
        res = jnp.dot(ws_ref[hd], rhs, preferred_element_type=F32)
        for c in range(n_chunks):
            mix_scr[c * CHUNK:(c + 1) * CHUNK, cols] = res[:, c * LANES:(c + 1) * LANES]
    u = _gelu(proj(COL_U, COL_V))
    out = u * (mix_scr[...] + bs_ref[...])
    ya_ref[...] = (_rms(out) * og_ref[...]).astype(BF16)


def _inproj_gmlp(x2, mod3, norm1_g, w_in_p, b_in_p, ln_g, ln_b, w_s, bs_tile, out_g, seq):
    t = x2.shape[0]
    tiles_per_seq = seq // TM_PROJ
    row = lambda i: (i, 0)
    const2 = lambda i: (0, 0)
    return pl.pallas_call(
        _inproj_body,
        grid=(t // TM_PROJ,),
        in_specs=[
            pl.BlockSpec((TM_PROJ, D_MODEL), row),
            pl.BlockSpec((None, 1, D_MODEL), lambda i: (i // tiles_per_seq, 0, 0)),
            pl.BlockSpec((None, 1, D_MODEL), lambda i: (i // tiles_per_seq, 0, 1)),
            pl.BlockSpec((1, D_MODEL), const2),
            pl.BlockSpec((D_MODEL, IN_COLS_PAD), const2),
            pl.BlockSpec((1, IN_COLS_PAD), const2),
            pl.BlockSpec((1, GMLP_WIDTH), const2),
            pl.BlockSpec((1, GMLP_WIDTH), const2),
            pl.BlockSpec((GMLP_HEADS, CHUNK, CHUNK), lambda i: (0, 0, 0)),
            pl.BlockSpec((TM_PROJ, GMLP_WIDTH), const2),
            pl.BlockSpec((1, GMLP_WIDTH), const2),
        ],
        out_specs=[
            pl.BlockSpec((TM_PROJ, GMLP_WIDTH), row),
            pl.BlockSpec((TM_PROJ, SSD_WIDTH), row),
            pl.BlockSpec((TM_PROJ, CONV_CH), row),
            pl.BlockSpec((TM_PROJ, LANES), row),
        ],
        out_shape=[
            jax.ShapeDtypeStruct((t, GMLP_WIDTH), BF16),
            jax.ShapeDtypeStruct((t, SSD_WIDTH), BF16),
            jax.ShapeDtypeStruct((t, CONV_CH), F32),
            jax.ShapeDtypeStruct((t, LANES), F32),
        ],
        scratch_shapes=[pltpu.VMEM((TM_PROJ, GMLP_WIDTH), F32)],
        compiler_params=pltpu.CompilerParams(
            dimension_semantics=("arbitrary",), vmem_limit_bytes=VMEM_LIMIT),
        name="inproj_gmlp",
    )(x2, mod3, mod3, norm1_g, w_in_p, b_in_p, ln_g, ln_b, w_s, bs_tile, out_g)


def _ssd_chunk(act, dtv, a_row, expand, st_scr, rev):
    off = SSD_HEADS if rev else 0
    row = lax.broadcasted_iota(jnp.int32, (CHUNK, CHUNK), 0)
    col = lax.broadcasted_iota(jnp.int32, (CHUNK, CHUNK), 1)
    lower = row >= col
    upper = row <= col
    keep = upper if rev else lower
    da = dtv * a_row
    cs = jnp.dot(keep.astype(F32), da, precision=HIGHEST, preferred_element_type=F32)
    cs_t = cs.T
    dt_t = dtv.T
    tot = cs[0:1, :] if rev else cs[CHUNK - 1:CHUNK, :]

    xs = act[:, :SSD_WIDTH]
    lane = lax.broadcasted_iota(jnp.int32, (CHUNK, LANES), 1)
    first_half = lane < SSD_HEAD_DIM
    zero = jnp.zeros((), BF16)

    stack = jnp.concatenate(
        [jnp.exp(cs), dtv * jnp.exp(tot - cs), jnp.broadcast_to(jnp.exp(tot), (SUBLANES, LANES))], axis=0)
    stack_x = jnp.dot(stack.astype(BF16), expand, preferred_element_type=F32)
    into_x = stack_x[:CHUNK]
    w_x = stack_x[CHUNK:2 * CHUNK]
    cd_x = stack_x[2 * CHUNK:2 * CHUNK + 1]
    xw = xs * w_x.astype(BF16)

    pieces = []
    for g in range(SSD_GROUPS):
        bg = act[:, SSD_WIDTH + g * SSD_STATE:SSD_WIDTH + (g + 1) * SSD_STATE]
        cg = act[:, SSD_WIDTH + SSD_GROUPS * SSD_STATE + g * SSD_STATE:
                 SSD_WIDTH + SSD_GROUPS * SSD_STATE + (g + 1) * SSD_STATE]
        scores = lax.dot_general(cg, bg, (((1,), (1,)), ((), ())), preferred_element_type=F32)
        heads_per_group = SSD_HEADS // SSD_GROUPS
        for pair in range(heads_per_group // 2):
            h0 = g * heads_per_group + 2 * pair
            xs_pair = xs[:, h0 * SSD_HEAD_DIM:(h0 + 2) * SSD_HEAD_DIM]
            y_pair = None
            for k in range(2):
                hh = off + h0 + k
                seg = cs[:, hh:hh + 1] - cs_t[hh:hh + 1, :]
                dec = jnp.exp(jnp.where(keep, seg, -jnp.inf))
                m = (scores * dec * dt_t[hh:hh + 1, :]).astype(BF16)
                rhs = jnp.where(first_half if k == 0 else jnp.logical_not(first_half), xs_pair, zero)
                part = jnp.dot(m, rhs, preferred_element_type=F32)
                y_pair = part if y_pair is None else y_pair + part
            pieces.append(y_pair)
    y_diag = jnp.concatenate(pieces, axis=1)

    y_off = []
    for g in range(SSD_GROUPS):
        gcols = slice(g * GROUP_WIDTH, (g + 1) * GROUP_WIDTH)
        bg = act[:, SSD_WIDTH + g * SSD_STATE:SSD_WIDTH + (g + 1) * SSD_STATE]
        cg = act[:, SSD_WIDTH + SSD_GROUPS * SSD_STATE + g * SSD_STATE:
                 SSD_WIDTH + SSD_GROUPS * SSD_STATE + (g + 1) * SSD_STATE]
        prev = st_scr[:, gcols]
        y_off.append(jnp.dot(cg, prev.astype(BF16), preferred_element_type=F32))
        bg_t = bg.astype(F32).T.astype(BF16)
        new = jnp.dot(bg_t, xw[:, gcols], preferred_element_type=F32)
        st_scr[:, gcols] = prev * cd_x[:, gcols] + new
    return y_diag + jnp.concatenate(y_off, axis=1) * into_x


def _ssd_body(xbc_ref, xprev_ref, xnext_ref, dt_ref, z_ref, ya_ref, x_ref, gate_ref, cw_ref, cb_ref,
              dtb_ref, alog_ref, dsk_ref, ng_ref, wout_ref, exp_ref, o_ref,
              act_scr, dts_scr, yf_scr, st_scr, ext_scr, *, n_chunks):
    d = pl.program_id(1)
    c = pl.program_id(2)
    lane1 = lax.broadcasted_iota(jnp.int32, (1, LANES), 1)
    a_row = jnp.where(lane1 < 2 * SSD_HEADS, -jnp.exp(alog_ref[...]), 0.0)

    @pl.when(c == 0)
    def _():
        st_scr[...] = jnp.zeros_like(st_scr)

    @pl.when(d == 0)
    def _forward():
        row0 = pl.multiple_of(c * CHUNK, CHUNK)
        ext_scr[0:SUBLANES, :] = jnp.where(c > 0, xprev_ref[...], 0.0)
        ext_scr[SUBLANES:SUBLANES + CHUNK, :] = xbc_ref[...]
        ext_scr[SUBLANES + CHUNK:, :] = jnp.where(c < n_chunks - 1, xnext_ref[...], 0.0)
        acc = cb_ref[...]
        for k in range(CONV_WIDTH):
            lo = SUBLANES - CONV_WIDTH // 2 + k
            acc = acc + cw_ref[k:k + 1, :] * ext_scr[lo:lo + CHUNK, :]
        act = _silu(acc).astype(BF16)
        dtv = _softplus(dt_ref[...] + dtb_ref[...])
        act_scr[pl.ds(row0, CHUNK), :] = act
        dts_scr[pl.ds(row0, CHUNK), :] = dtv
        y = _ssd_chunk(act, dtv, a_row, exp_ref[0], st_scr, rev=False)
        yf_scr[pl.ds(row0, CHUNK), :] = y

    @pl.when(d == 1)
    def _backward():
        row0 = pl.multiple_of((n_chunks - 1 - c) * CHUNK, CHUNK)
        act = act_scr[pl.ds(row0, CHUNK), :]
        dtv = dts_scr[pl.ds(row0, CHUNK), :]
        yb = _ssd_chunk(act, dtv, a_row, exp_ref[1], st_scr, rev=True)
        xs = act[:, :SSD_WIDTH].astype(F32)
        y = yf_scr[pl.ds(row0, CHUNK), :] + yb + dsk_ref[...] * xs
        y = y * _silu(z_ref[...].astype(F32))
        y = jnp.concatenate(
            [_rms(y[:, g * GROUP_WIDTH:(g + 1) * GROUP_WIDTH]) for g in range(SSD_GROUPS)], axis=1)
        y = y * ng_ref[...]
        mix = jnp.concatenate([ya_ref[...], y.astype(BF16)], axis=1)
        o = jnp.dot(mix, wout_ref[...], preferred_element_type=F32)
        o_ref[...] = x_ref[...] + gate_ref[...] * o


def _ssd_outproj(xbc, dt, z, ya, x2, mod3, conv_w8, conv_b, dtb_row, alog_row, dsk_row, norm_g, w_out_b,
                 expand, bn, seq):
    t = x2.shape[0]
    nc = seq // CHUNK
    blocks8 = CHUNK // SUBLANES
    last8 = t // SUBLANES - 1

    def fwd_chunk(b, d, c):
        return b * nc + c + d * (nc - 1 - c)

    def bwd_chunk(b, d, c):
        return b * nc + nc - 1 - d * c

    const2 = lambda b, d, c: (0, 0)
    return pl.pallas_call(
        functools.partial(_ssd_body, n_chunks=nc),
        grid=(bn, 2, nc),
        in_specs=[
            pl.BlockSpec((CHUNK, CONV_CH), lambda b, d, c: (fwd_chunk(b, d, c), 0)),
            pl.BlockSpec((SUBLANES, CONV_CH),
                         lambda b, d, c: (jnp.maximum(fwd_chunk(b, d, c) * blocks8 - 1, 0), 0)),
            pl.BlockSpec((SUBLANES, CONV_CH),
                         lambda b, d, c: (jnp.minimum((fwd_chunk(b, d, c) + 1) * blocks8, last8), 0)),
            pl.BlockSpec((CHUNK, LANES), lambda b, d, c: (fwd_chunk(b, d, c), 0)),
            pl.BlockSpec((CHUNK, SSD_WIDTH), lambda b, d, c: (bwd_chunk(b, d, c), 0)),
            pl.BlockSpec((CHUNK, GMLP_WIDTH), lambda b, d, c: (bwd_chunk(b, d, c), 0)),
            pl.BlockSpec((CHUNK, D_MODEL), lambda b, d, c: (bwd_chunk(b, d, c), 0)),
            pl.BlockSpec((None, 1, D_MODEL), lambda b, d, c: (b, 0, 2)),
            pl.BlockSpec((SUBLANES, CONV_CH), const2),
            pl.BlockSpec((1, CONV_CH), const2),
            pl.BlockSpec((1, LANES), const2),
            pl.BlockSpec((1, LANES), const2),
            pl.BlockSpec((1, SSD_WIDTH), const2),
            pl.BlockSpec((1, SSD_WIDTH), const2),
            pl.BlockSpec((GMLP_WIDTH + SSD_WIDTH, D_MODEL), const2),
            pl.BlockSpec((2, LANES, SSD_WIDTH), lambda b, d, c: (0, 0, 0)),
        ],
        out_specs=pl.BlockSpec((CHUNK, D_MODEL), lambda b, d, c: (bwd_chunk(b, d, c), 0)),
        out_shape=jax.ShapeDtypeStruct((t, D_MODEL), F32),
        scratch_shapes=[
            pltpu.VMEM((seq, CONV_CH), BF16),
            pltpu.VMEM((seq, LANES), F32),
            pltpu.VMEM((seq, SSD_WIDTH), F32),
            pltpu.VMEM((SSD_STATE, SSD_WIDTH), F32),
            pltpu.VMEM((CHUNK + 2 * SUBLANES, CONV_CH), F32),
        ],
        compiler_params=pltpu.CompilerParams(
            dimension_semantics=("arbitrary", "arbitrary", "arbitrary"), vmem_limit_bytes=VMEM_LIMIT),
        name="ssd_outproj",
    )(xbc, xbc, xbc, dt, z, ya, x2, mod3, conv_w8, conv_b, dtb_row, alog_row, dsk_row, norm_g, w_out_b,
      expand)


def _route(logits):
    lane = lax.broadcasted_iota(jnp.int32, logits.shape, 1)
    big = jnp.int32(LANES)
    neg = -jnp.inf
    gmask = (lane >= N_EXPERTS) & (lane < N_EXPERTS + N_EGROUPS)
    gl = jnp.where(gmask, logits, neg)
    gmax = jnp.max(gl, axis=-1, keepdims=True)
    gidx = jnp.min(jnp.where(gl == gmax, lane, big), axis=-1, keepdims=True) - N_EXPERTS
    p_g = 1.0 / jnp.sum(jnp.where(gmask, jnp.exp(gl - gmax), 0.0), axis=-1, keepdims=True)
    lo = gidx * EXPERTS_PER_GROUP
    emask = (lane >= lo) & (lane < lo + EXPERTS_PER_GROUP)
    el = jnp.where(emask, logits, neg)
    v1 = jnp.max(el, axis=-1, keepdims=True)
    i1 = jnp.min(jnp.where(el == v1, lane, big), axis=-1, keepdims=True)
    el2 = jnp.where(lane == i1, neg, el)
    v2 = jnp.max(el2, axis=-1, keepdims=True)
    i2 = jnp.min(jnp.where(el2 == v2, lane, big), axis=-1, keepdims=True)
    e2 = jnp.exp(v2 - v1)
    den = 1.0 + e2
    w1 = p_g / den
    w2 = p_g * e2 / den
    return i1, i2, w1, w2


def _router_body(x_ref, shift_ref, scale_ref, g2_ref, wr_ref, br_ref, h_ref, rt_ref, cnt_ref):
    h = _rms(x_ref[...]) * g2_ref[...]
    h = h * (1.0 + scale_ref[...]) + shift_ref[...]
    hb = h.astype(BF16)
    h_ref[...] = hb
    h_lo = (h - hb.astype(F32)).astype(BF16)
    logits = (jnp.dot(hb, wr_ref[0], preferred_element_type=F32)
              + jnp.dot(h_lo, wr_ref[0], preferred_element_type=F32)
              + jnp.dot(hb, wr_ref[1], preferred_element_type=F32)) + br_ref[...]
    i1, i2, w1, w2 = _route(logits)
    lane = lax.broadcasted_iota(jnp.int32, (TM_MOE, LANES), 1)
    rt_ref[...] = jnp.where(lane == 0, i1.astype(F32), jnp.where(lane == 1, i2.astype(F32),
                            jnp.where(lane == 2, w1, jnp.where(lane == 3, w2, 0.0))))
    mask = jnp.where(lane == i1, 1.0, jnp.where(lane == i2, 1.0, 0.0))
    cnt_ref[...] = jnp.broadcast_to(jnp.sum(mask, axis=0, keepdims=True), (SUBLANES, LANES))


def _router(x1, mod3, norm2_g, w_router, b_router, seq):
    t = x1.shape[0]
    tiles_per_seq = seq // TM_MOE
    n_tiles = t // TM_MOE
    row = lambda i: (i, 0)
    const2 = lambda i: (0, 0)
    return pl.pallas_call(
        _router_body,
        grid=(n_tiles,),
        in_specs=[
            pl.BlockSpec((TM_MOE, D_MODEL), row),
            pl.BlockSpec((None, 1, D_MODEL), lambda i: (i // tiles_per_seq, 0, 3)),
            pl.BlockSpec((None, 1, D_MODEL), lambda i: (i // tiles_per_seq, 0, 4)),
            pl.BlockSpec((1, D_MODEL), const2),
            pl.BlockSpec((2, D_MODEL, LANES), lambda i: (0, 0, 0)),
            pl.BlockSpec((1, LANES), const2),
        ],
        out_specs=[
            pl.BlockSpec((TM_MOE, D_MODEL), row),
            pl.BlockSpec((TM_MOE, LANES), row),
            pl.BlockSpec((SUBLANES, LANES), row),
        ],
        out_shape=[
            jax.ShapeDtypeStruct((t, D_MODEL), BF16),
            jax.ShapeDtypeStruct((t, LANES), F32),
            jax.ShapeDtypeStruct((n_tiles * SUBLANES, LANES), F32),
        ],
        compiler_params=pltpu.CompilerParams(
            dimension_semantics=("arbitrary",), vmem_limit_bytes=VMEM_LIMIT),
        name="moe_router",
    )(x1, mod3, mod3, norm2_g, w_router, b_router)


def _row_copy(vmem_buf, v_start, hbm_ref, g_start, size, sem, to_hbm):
    v = vmem_buf.at[pl.ds(v_start, size), :]
    g = hbm_ref.at[pl.ds(g_start, size), :]
    return pltpu.make_async_copy(v, g, sem) if to_hbm else pltpu.make_async_copy(g, v, sem)


def _copy_rows(n, vmem_buf, v0, hbm_ref, g0, sem, *, to_hbm, wait):
    def run(v_off, g_off, size):
        v_start = 0 if v0 is None else pl.multiple_of(v0 + v_off, SEG_ALIGN)
        cp = _row_copy(vmem_buf, v_start, hbm_ref, pl.multiple_of(g0 + g_off, SEG_ALIGN), size, sem, to_hbm)
        if wait:
            cp.wait()
        else:
            cp.start()

    n_big = n // PIECE_ROWS

    def big_piece(k, carry):
        run(k * PIECE_ROWS, k * PIECE_ROWS, PIECE_ROWS)
        return carry

    lax.fori_loop(0, n_big, big_piece, 0)
    base = n_big * PIECE_ROWS
    rem = n - base
    for size in REM_SIZES:
        done = base + (rem // (2 * size)) * (2 * size)

        @pl.when((rem & size) != 0)
        def _():
            run(done, done, size)


def _segment_copies(seg_ref, loc_ref, goff_ref, tile, vmem_buf, hbm_ref, sem, *, to_hbm):
    base = tile * N_EXPERTS

    def per_expert(e, carry):
        _copy_rows(seg_ref[base + e], vmem_buf, loc_ref[base + e], hbm_ref, goff_ref[base + e], sem,
                   to_hbm=to_hbm, wait=False)
        return carry

    lax.fori_loop(0, N_EXPERTS, per_expert, 0)


def _segment_wait(seg_ref, loc_ref, tile, vmem_buf, hbm_ref, sem, *, to_hbm):
    last = tile * N_EXPERTS + N_EXPERTS - 1
    total = loc_ref[last] + seg_ref[last]
    for size in TOTAL_SIZES:
        @pl.when((total & size) != 0)
        def _():
            _row_copy(vmem_buf, 0, hbm_ref, 0, size, sem, to_hbm).wait()


def _dispatch_body(seg_ref, loc_ref, goff_ref, fill_ref, h_ref, rt_ref, locrow_ref, pos_ref, xs_hbm, buf, sem,
                   *, n_tiles, n_row_tiles):
    i = pl.program_id(0)
    slot = i % 2
    start = functools.partial(_segment_copies, seg_ref, loc_ref, goff_ref, hbm_ref=xs_hbm, to_hbm=True)
    wait = functools.partial(_segment_wait, seg_ref, loc_ref, hbm_ref=xs_hbm, to_hbm=True)

    @pl.when(i >= 2)
    def _():
        wait(i - 2, buf.at[slot], sem=sem.at[slot])

    rt = rt_ref[...]
    lane = lax.broadcasted_iota(jnp.int32, (TM_MOE, LANES), 1)
    lanef = lane.astype(F32)
    e1 = rt[:, 0:1]
    e2 = rt[:, 1:2]
    mask = jnp.where(lanef == e1, 1.0, jnp.where(lanef == e2, 1.0, 0.0))
    r = lax.broadcasted_iota(jnp.int32, (TM_MOE, TM_MOE), 0)
    cc = lax.broadcasted_iota(jnp.int32, (TM_MOE, TM_MOE), 1)
    strict = jnp.where(r > cc, 1.0, 0.0).astype(BF16)
    rank = jnp.dot(strict, mask.astype(BF16), preferred_element_type=F32)
    posall = rank + locrow_ref[...]
    pos1 = jnp.sum(jnp.where(lanef == e1, posall, 0.0), axis=-1, keepdims=True)
    pos2 = jnp.sum(jnp.where(lanef == e2, posall, 0.0), axis=-1, keepdims=True)
    pp = jnp.where(lane == 0, pos1, jnp.where(lane == 1, pos2, rt))
    pos_ref[...] = pp

    rowp = lax.broadcasted_iota(jnp.int32, (R_LOC, LANES), 0).astype(F32)
    blocks = []
    for k in range(TM_MOE // LANES):
        ppt = pp[k * LANES:(k + 1) * LANES, :].T
        blocks.append(jnp.where(rowp == ppt[0:1, :], 1.0, jnp.where(rowp == ppt[1:2, :], 1.0, 0.0)).astype(BF16))
    perm = jnp.concatenate(blocks, axis=1)
    buf[slot] = jnp.dot(perm, h_ref[...], preferred_element_type=F32).astype(BF16)
    start(i, buf.at[slot], sem=sem.at[slot])

    @pl.when(i == n_tiles - 1)
    def _():
        if n_tiles > 1:
            wait(i - 1, buf.at[1 - slot], sem=sem.at[1 - slot])
        wait(i, buf.at[slot], sem=sem.at[slot])
        zsrc = buf.at[1 - slot]
        zsem = sem.at[1 - slot]
        zsrc[0:TR_EXPERT, :] = jnp.zeros((TR_EXPERT, D_MODEL), BF16)
        for waiting in (False, True):
            def per_expert(e, carry):
                _copy_rows(fill_ref[N_EXPERTS + e], zsrc, None, xs_hbm, fill_ref[e], zsem,
                           to_hbm=True, wait=waiting)
                return carry

            def per_row_tile(k, carry):
                cp = _row_copy(zsrc, 0, xs_hbm, pl.multiple_of(k * TR_EXPERT, TR_EXPERT), TR_EXPERT, zsem, True)
                if waiting:
                    cp.wait()
                else:
                    cp.start()
                return carry

            lax.fori_loop(0, N_EXPERTS, per_expert, 0)
            lax.fori_loop(fill_ref[2 * N_EXPERTS], n_row_tiles, per_row_tile, 0)


def _dispatch(h2, rt, seg, loc, goff, fill, locrow, n_row_tiles):
    t = h2.shape[0]
    n_tiles = t // TM_MOE
    row = lambda i, *_: (i, 0)
    grid_spec = pltpu.PrefetchScalarGridSpec(
        num_scalar_prefetch=4,
        grid=(n_tiles,),
        in_specs=[
            pl.BlockSpec((TM_MOE, D_MODEL), row),
            pl.BlockSpec((TM_MOE, LANES), row),
            pl.BlockSpec((None, 1, LANES), lambda i, *_: (i, 0, 0)),
        ],
        out_specs=[
            pl.BlockSpec((TM_MOE, LANES), row),
            pl.BlockSpec(memory_space=pl.ANY),
        ],
        scratch_shapes=[
            pltpu.VMEM((2, R_LOC, D_MODEL), BF16),
            pltpu.SemaphoreType.DMA((2,)),
        ],
    )
    return pl.pallas_call(
        functools.partial(_dispatch_body, n_tiles=n_tiles, n_row_tiles=n_row_tiles),
        grid_spec=grid_spec,
        out_shape=[
            jax.ShapeDtypeStruct((t, LANES), F32),
            jax.ShapeDtypeStruct((n_row_tiles * TR_EXPERT, D_MODEL), BF16),
        ],
        compiler_params=pltpu.CompilerParams(
            dimension_semantics=("arbitrary",), vmem_limit_bytes=VMEM_LIMIT),
        name="moe_dispatch",
    )(seg, loc, goff, fill, h2, rt, locrow)


def _expert_body(te_ref, na_ref, xs_ref, w1a_ref, w3a_ref, w2a_ref, w1b_ref, w3b_ref, w2b_ref, y_ref):
    r = pl.program_id(0)
    n_active = na_ref[0]
    weights = ((w1a_ref, w3a_ref, w2a_ref), (w1b_ref, w3b_ref, w2b_ref))

    def swiglu(half):
        w1_ref, w3_ref, w2_ref = weights[half]
        rows = slice(half * TR_EXPERT, (half + 1) * TR_EXPERT)
        x = xs_ref[rows, :]
        a = _silu(jnp.dot(x, w1_ref[...], preferred_element_type=F32))
        a = a * jnp.dot(x, w3_ref[...], preferred_element_type=F32)
        y_ref[rows, :] = jnp.dot(a.astype(BF16), w2_ref[...], preferred_element_type=F32).astype(BF16)

    first = EXPERT_TILES_PER_STEP * r

    @pl.when(first + 1 < n_active)
    def _():
        swiglu(0)
        swiglu(1)

    @pl.when(first + 1 == n_active)
    def _():
        swiglu(0)
        y_ref[TR_EXPERT:, :] = jnp.zeros((TR_EXPERT, D_MODEL), BF16)

    @pl.when(first >= n_active)
    def _():
        y_ref[...] = jnp.zeros_like(y_ref)


def _experts(xs, tile_expert, n_active, w1b, w3b, w2b):
    n_rows = xs.shape[0]
    step_rows = EXPERT_TILES_PER_STEP * TR_EXPERT
    x_block = lambda r, te, na: (jnp.minimum(r, (na[0] - 1) // EXPERT_TILES_PER_STEP), 0)
    w_specs = []
    for half in range(EXPERT_TILES_PER_STEP):
        w_block = lambda r, te, na, half=half: (te[EXPERT_TILES_PER_STEP * r + half], 0, 0)
        w_specs += [pl.BlockSpec((None, D_MODEL, D_EXPERT), w_block),
                    pl.BlockSpec((None, D_MODEL, D_EXPERT), w_block),
                    pl.BlockSpec((None, D_EXPERT, D_MODEL), w_block)]
    grid_spec = pltpu.PrefetchScalarGridSpec(
        num_scalar_prefetch=2,
        grid=(n_rows // step_rows,),
        in_specs=[pl.BlockSpec((step_rows, D_MODEL), x_block)] + w_specs,
        out_specs=pl.BlockSpec((step_rows, D_MODEL), lambda r, te, na: (r, 0)),
    )
    return pl.pallas_call(
        _expert_body,
        grid_spec=grid_spec,
        out_shape=jax.ShapeDtypeStruct((n_rows, D_MODEL), BF16),
        compiler_params=pltpu.CompilerParams(
            dimension_semantics=("arbitrary",), vmem_limit_bytes=VMEM_LIMIT),
        name="moe_experts",
    )(tile_expert, n_active, xs, w1b, w3b, w2b, w1b, w3b, w2b)


def _combine_body(seg_ref, loc_ref, goff_ref, y_hbm, pos_ref, x_ref, gate_ref, fg_ref, o_ref, buf, sem, *,
                  n_tiles, final_norm):
    i = pl.program_id(0)
    slot = i % 2
    start = functools.partial(_segment_copies, seg_ref, loc_ref, goff_ref, hbm_ref=y_hbm, to_hbm=False)

    @pl.when(i == 0)
    def _():
        buf[...] = jnp.zeros_like(buf)
        start(0, buf.at[0], sem=sem.at[0])

    @pl.when(i + 1 < n_tiles)
    def _():
        start(i + 1, buf.at[1 - slot], sem=sem.at[1 - slot])

    pp = pos_ref[...]
    colp = lax.broadcasted_iota(jnp.int32, (TM_MOE, R_LOC), 1).astype(F32)
    wc = jnp.where(colp == pp[:, 0:1], pp[:, 2:3], jnp.where(colp == pp[:, 1:2], pp[:, 3:4], 0.0)).astype(BF16)
    _segment_wait(seg_ref, loc_ref, i, buf.at[slot], y_hbm, sem.at[slot], to_hbm=False)
    moe = jnp.dot(wc, buf[slot], preferred_element_type=F32)
    x2 = x_ref[...] + gate_ref[...] * moe
    o_ref[...] = _rms(x2) * fg_ref[...] if final_norm else x2


def _combine(y, pos, x1, mod3, final_g, seg, loc, goff, seq, final_norm):
    t = x1.shape[0]
    n_tiles = t // TM_MOE
    tiles_per_seq = seq // TM_MOE
    row = lambda i, *_: (i, 0)
    grid_spec = pltpu.PrefetchScalarGridSpec(
        num_scalar_prefetch=3,
        grid=(n_tiles,),
        in_specs=[
            pl.BlockSpec(memory_space=pl.ANY),
            pl.BlockSpec((TM_MOE, LANES), row),
            pl.BlockSpec((TM_MOE, D_MODEL), row),
            pl.BlockSpec((None, 1, D_MODEL), lambda i, *_: (i // tiles_per_seq, 0, 5)),
            pl.BlockSpec((1, D_MODEL), lambda i, *_: (0, 0)),
        ],
        out_specs=pl.BlockSpec((TM_MOE, D_MODEL), row),
        scratch_shapes=[
            pltpu.VMEM((2, R_LOC, D_MODEL), BF16),
            pltpu.SemaphoreType.DMA((2,)),
        ],
    )
    return pl.pallas_call(
        functools.partial(_combine_body, n_tiles=n_tiles, final_norm=final_norm),
        grid_spec=grid_spec,
        out_shape=jax.ShapeDtypeStruct((t, D_MODEL), F32),
        compiler_params=pltpu.CompilerParams(
            dimension_semantics=("arbitrary",), vmem_limit_bytes=VMEM_LIMIT),
        name="moe_combine",
    )(seg, loc, goff, y, pos, x1, mod3, final_g)


def _moe_plan(counts, n_tiles, n_row_tiles):
    cnt = counts.reshape(n_tiles, SUBLANES, LANES)[:, 0, :N_EXPERTS].astype(jnp.int32)
    seg = (cnt + SEG_ALIGN - 1) // SEG_ALIGN * SEG_ALIGN
    before_e = jnp.arange(N_EXPERTS)[:, None] < jnp.arange(N_EXPERTS)[None, :]
    before_t = jnp.arange(n_tiles)[:, None] > jnp.arange(n_tiles)[None, :]
    loc = jnp.sum(jnp.where(before_e[None], seg[:, :, None], 0), axis=1)
    tot = jnp.sum(seg, axis=0)
    region = (tot + TR_EXPERT - 1) // TR_EXPERT * TR_EXPERT
    gstart = jnp.sum(jnp.where(before_e, region[:, None], 0), axis=0)
    gend = gstart + region
    goff = gstart[None, :] + jnp.sum(jnp.where(before_t[:, :, None], seg[None], 0), axis=1)
    n_active = gend[-1] // TR_EXPERT
    tile_row0 = jnp.arange(n_row_tiles, dtype=jnp.int32) * TR_EXPERT
    last_row0 = (n_active - 1) * TR_EXPERT
    te = jnp.sum(gend[None, :] <= jnp.minimum(tile_row0, last_row0)[:, None], axis=1).astype(jnp.int32)
    te = jnp.minimum(te, N_EXPERTS - 1)
    locrow = _pad_cols(loc.astype(F32), LANES).reshape(n_tiles, 1, LANES)
    n_active = n_active.reshape(1).astype(jnp.int32)
    fill = jnp.concatenate([gstart + tot, region - tot, n_active]).astype(jnp.int32)
    return (seg.reshape(-1), loc.reshape(-1).astype(jnp.int32), goff.reshape(-1).astype(jnp.int32), fill,
            locrow, te, n_active)


def _moe_final(x1, mod3, norm2_g, w_router, b_router, w1b, w3b, w2b, final_g, seq, final_norm):
    t = x1.shape[0]
    n_tiles = t // TM_MOE
    max_rows = N_ASSIGN * t + (SEG_ALIGN - 1) * N_EXPERTS * n_tiles + N_EXPERTS * (TR_EXPERT - SEG_ALIGN)
    step_rows = EXPERT_TILES_PER_STEP * TR_EXPERT
    n_row_tiles = -(-max_rows // step_rows) * EXPERT_TILES_PER_STEP
    h2, rt, counts = _router(x1, mod3, norm2_g, w_router, b_router, seq)
    seg, loc, goff, fill, locrow, te, n_active = _moe_plan(counts, n_tiles, n_row_tiles)
    pos, xs = _dispatch(h2, rt, seg, loc, goff, fill, locrow, n_row_tiles)
    y = _experts(xs, te, n_active, w1b, w3b, w2b)
    return _combine(y, pos, x1, mod3, final_g, seg, loc, goff, seq, final_norm)


def _pad_cols(a, width):
    return jnp.pad(a, ((0, 0), (0, width - a.shape[1])))


def kernel(x, c, w_ada, b_ada, norm1_g, w_in, b_in, gmlp_ln_g, gmlp_ln_b, gmlp_w_s, gmlp_b_s, gmlp_out_g, conv_w, conv_b, a_log_f, a_log_b, dt_bias_f, dt_bias_b, d_skip, ssd_norm_g, w_out, norm2_g, w_router_g, b_router_g, w_router_e, b_router_e, w1, w3, w2, final_g):
    bn, seq, _ = x.shape
    depth = w_ada.shape[0]
    x2 = x.reshape(bn * seq, D_MODEL)

    head_rows = jnp.arange(LANES)[:, None]
    head_cols = jnp.arange(SSD_WIDTH)[None, :] // SSD_HEAD_DIM
    expand = jnp.stack([head_rows == head_cols, head_rows == head_cols + SSD_HEADS]).astype(BF16)

    for l in range(depth):
        mod3 = _modulation(c, w_ada[l], b_ada[l]).reshape(bn, 1, N_MOD * D_MODEL)

        w_in_p = _pad_cols(w_in[l], IN_COLS_PAD).astype(BF16)
        b_in_p = _pad_cols(b_in[l][None, :], IN_COLS_PAD)
        bs = jnp.repeat(gmlp_b_s[l].T, LANES, axis=1)
        bs_tile = jnp.tile(bs, (TM_PROJ // CHUNK, 1))
        ya, z, xbc, dt = _inproj_gmlp(
            x2, mod3, norm1_g[l][None, :], w_in_p, b_in_p, gmlp_ln_g[l][None, :], gmlp_ln_b[l][None, :],
            gmlp_w_s[l].astype(BF16), bs_tile, gmlp_out_g[l][None, :], seq)

        conv_w8 = jnp.pad(conv_w[l], ((0, SUBLANES - CONV_WIDTH), (0, 0)))
        dtb_row = _pad_cols(jnp.concatenate([dt_bias_f[l], dt_bias_b[l]])[None, :], LANES)
        alog_row = _pad_cols(jnp.concatenate([a_log_f[l], a_log_b[l]])[None, :], LANES)
        dsk_row = jnp.repeat(d_skip[l], SSD_HEAD_DIM)[None, :]
        x2 = _ssd_outproj(xbc, dt, z, ya, x2, mod3, conv_w8, conv_b[l][None, :], dtb_row, alog_row, dsk_row,
                          ssd_norm_g[l][None, :], w_out[l].astype(BF16), expand, bn, seq)

        w_re = jnp.transpose(w_router_e[l], (1, 0, 2)).reshape(D_MODEL, N_EXPERTS)
        w_router = _pad_cols(jnp.concatenate([w_re, w_router_g[l]], axis=1), LANES)
        w_router_hi = w_router.astype(BF16)
        w_router = jnp.stack([w_router_hi, (w_router - w_router_hi.astype(F32)).astype(BF16)])
        b_router = _pad_cols(jnp.concatenate([b_router_e[l].reshape(-1), b_router_g[l]])[None, :], LANES)
        x2 = _moe_final(x2, mod3, norm2_g[l][None, :], w_router, b_router, w1[l].astype(BF16),
                        w3[l].astype(BF16), w2[l].astype(BF16), final_g[None, :], seq,
                        final_norm=(l == depth - 1))
    return x2.reshape(bn, seq, D_MODEL)
```

```python
import functools
import math

import jax
import jax.numpy as jnp
from jax import lax
from jax.experimental import pallas as pl
from jax.experimental.pallas import tpu as pltpu

F32 = jnp.float32
BF16 = jnp.bfloat16
HIGHEST = lax.Precision.HIGHEST

D_MODEL = 1024
N_MOD = 6
GMLP_WIDTH = 1024
GMLP_HEADS = 8
CHUNK = 128
SSD_STEP_ROWS = 2 * CHUNK
SSD_WIDTH = 1024
SSD_HEADS = 16
SSD_HEAD_DIM = 64
SSD_GROUPS = 2
SSD_STATE = 128
GROUP_WIDTH = SSD_WIDTH // SSD_GROUPS
CONV_WIDTH = 5
CONV_CH = SSD_WIDTH + 2 * SSD_GROUPS * SSD_STATE
N_EGROUPS = 4
EXPERTS_PER_GROUP = 8
N_EXPERTS = 32
D_EXPERT = 256
EPS = 1e-6

LANES = 128
SUBLANES = 8
COL_U, COL_V, COL_Z, COL_XBC, COL_DT = 0, 1024, 2048, 3072, 4608
IN_COLS = 4640
IN_COLS_PAD = COL_DT + LANES
TM_PROJ = 512
PROJ_ROWS = 256
TM_MOE = 512
TR_EXPERT = 256
EXPERT_TILES_PER_STEP = 2
N_ASSIGN = 2
SEG_ALIGN = 2 * SUBLANES
R_LOC = N_ASSIGN * TM_MOE + N_EXPERTS * SEG_ALIGN
PIECE_ROWS = 128
REM_SIZES = tuple(PIECE_ROWS >> s for s in range(1, PIECE_ROWS.bit_length()) if PIECE_ROWS >> s >= SEG_ALIGN)
TOTAL_SIZES = tuple(1 << s for s in range(R_LOC.bit_length() - 1, -1, -1) if 1 << s >= SEG_ALIGN)
VMEM_LIMIT = 56 * 1024 * 1024


def _silu(v):
    return v * jax.nn.sigmoid(v)


def _gelu(v):
    return 0.5 * v * (1.0 + lax.erf(v * math.sqrt(0.5)))


def _softplus(v):
    return jnp.maximum(v, 0.0) + jnp.log1p(jnp.exp(-jnp.abs(v)))


def _rms(v):
    return v * lax.rsqrt(jnp.mean(v * v, axis=-1, keepdims=True) + EPS)


def _mod_body(c_ref, w_ref, b_ref, o_ref):
    ca = _silu(c_ref[...])
    o_ref[...] = jnp.dot(ca, w_ref[...], precision=HIGHEST, preferred_element_type=F32) + b_ref[...]


def _modulation(c, w_ada, b_ada):
    bn = c.shape[0]
    return pl.pallas_call(
        _mod_body,
        grid=(N_MOD,),
        in_specs=[
            pl.BlockSpec((bn, D_MODEL), lambda j: (0, 0)),
            pl.BlockSpec((D_MODEL, D_MODEL), lambda j: (0, j)),
            pl.BlockSpec((1, D_MODEL), lambda j: (0, j)),
        ],
        out_specs=pl.BlockSpec((bn, D_MODEL), lambda j: (0, j)),
        out_shape=jax.ShapeDtypeStruct((bn, N_MOD * D_MODEL), F32),
        name="adaln_mod",
    )(c, w_ada, b_ada.reshape(1, -1))


def _inproj_body(x_ref, shift_ref, scale_ref, g_ref, w_ref, b_ref, lng_ref, lnb_ref, ws_ref, bs_ref,
                 og_ref, ya_ref, z_ref, xbc_ref, dt_ref, mix_scr):
    for part in range(TM_PROJ // PROJ_ROWS):
        rows = slice(part * PROJ_ROWS, (part + 1) * PROJ_ROWS)
        h = _rms(x_ref[rows, :]) * g_ref[...]
        h = h * (1.0 + scale_ref[...]) + shift_ref[...]
        hb = h.astype(BF16)

        def proj(lo, hi, hb=hb):
            return jnp.dot(hb, w_ref[:, lo:hi], preferred_element_type=F32) + b_ref[:, lo:hi]

        z_ref[rows, :] = proj(COL_Z, COL_XBC).astype(BF16)
        xbc_ref[rows, :] = proj(COL_XBC, COL_DT)
        dt_ref[rows, :] = proj(COL_DT, IN_COLS_PAD)

        v = _gelu(proj(COL_V, COL_Z))
        mu = jnp.mean(v, axis=-1, keepdims=True)
        vc = v - mu
        var = jnp.mean(vc * vc, axis=-1, keepdims=True)
        vn = (vc * lax.rsqrt(var + EPS) * lng_ref[...] + lnb_ref[...]).astype(BF16)
        n_chunks = PROJ_ROWS // CHUNK
        for hd in range(GMLP_HEADS):
            cols = slice(hd * LANES, (hd + 1) * LANES)
            rhs = jnp.concatenate([vn[c * CHUNK:(c + 1) * CHUNK, cols] for c in range(n_chunks)], axis=1)
            res = jnp.dot(ws_ref[hd], rhs, preferred_element_type=F32)
            for c in range(n_chunks):
                mix_scr[part * PROJ_ROWS + c * CHUNK:part * PROJ_ROWS + (c + 1) * CHUNK, cols] = (
                    res[:, c * LANES:(c + 1) * LANES])
        u = _gelu(proj(COL_U, COL_V))
        out = u * (mix_scr[rows, :] + bs_ref[rows, :])
        ya_ref[rows, :] = (_rms(out) * og_ref[...]).astype(BF16)


def _inproj_gmlp(x2, mod3, norm1_g, w_in_p, b_in_p, ln_g, ln_b, w_s, bs_tile, out_g, seq):
    t = x2.shape[0]
    tiles_per_seq = seq // TM_PROJ
    row = lambda i: (i, 0)
    const2 = lambda i: (0, 0)
    return pl.pallas_call(
        _inproj_body,
        grid=(t // TM_PROJ,),
        in_specs=[
            pl.BlockSpec((TM_PROJ, D_MODEL), row),
            pl.BlockSpec((None, 1, D_MODEL), lambda i: (i // tiles_per_seq, 0, 0)),
            pl.BlockSpec((None, 1, D_MODEL), lambda i: (i // tiles_per_seq, 0, 1)),
            pl.BlockSpec((1, D_MODEL), const2),
            pl.BlockSpec((D_MODEL, IN_COLS_PAD), const2),
            pl.BlockSpec((1, IN_COLS_PAD), const2),
            pl.BlockSpec((1, GMLP_WIDTH), const2),
            pl.BlockSpec((1, GMLP_WIDTH), const2),
            pl.BlockSpec((GMLP_HEADS, CHUNK, CHUNK), lambda i: (0, 0, 0)),
            pl.BlockSpec((TM_PROJ, GMLP_WIDTH), const2),
            pl.BlockSpec((1, GMLP_WIDTH), const2),
        ],
        out_specs=[
            pl.BlockSpec((TM_PROJ, GMLP_WIDTH), row),
            pl.BlockSpec((TM_PROJ, SSD_WIDTH), row),
            pl.BlockSpec((TM_PROJ, CONV_CH), row),
            pl.BlockSpec((TM_PROJ, LANES), row),
        ],
        out_shape=[
            jax.ShapeDtypeStruct((t, GMLP_WIDTH), BF16),
            jax.ShapeDtypeStruct((t, SSD_WIDTH), BF16),
            jax.ShapeDtypeStruct((t, CONV_CH), F32),
            jax.ShapeDtypeStruct((t, LANES), F32),
        ],
        scratch_shapes=[pltpu.VMEM((TM_PROJ, GMLP_WIDTH), F32)],
        compiler_params=pltpu.CompilerParams(
            dimension_semantics=("arbitrary",), vmem_limit_bytes=VMEM_LIMIT),
        name="inproj_gmlp",
    )(x2, mod3, mod3, norm1_g, w_in_p, b_in_p, ln_g, ln_b, w_s, bs_tile, out_g)


def _ssd_chunk(act, dtv, a_row, expand, state, rev):
    off = SSD_HEADS if rev else 0
    row = lax.broadcasted_iota(jnp.int32, (CHUNK, CHUNK), 0)
    col = lax.broadcasted_iota(jnp.int32, (CHUNK, CHUNK), 1)
    lower = row >= col
    upper = row <= col
    keep = upper if rev else lower
    da = dtv * a_row
    cs = jnp.dot(keep.astype(F32), da, precision=HIGHEST, preferred_element_type=F32)
    cs_t = cs.T
    dt_t = dtv.T
    tot = cs[0:1, :] if rev else cs[CHUNK - 1:CHUNK, :]

    xs = act[:, :SSD_WIDTH]
    lane = lax.broadcasted_iota(jnp.int32, (CHUNK, LANES), 1)
    first_half = lane < SSD_HEAD_DIM
    zero = jnp.zeros((), BF16)

    stack = jnp.concatenate(
        [jnp.exp(cs), dtv * jnp.exp(tot - cs), jnp.broadcast_to(jnp.exp(tot), (SUBLANES, LANES))], axis=0)
    stack_x = jnp.dot(stack.astype(BF16), expand, preferred_element_type=F32)
    into_x = stack_x[:CHUNK]
    w_x = stack_x[CHUNK:2 * CHUNK]
    cd_x = stack_x[2 * CHUNK:2 * CHUNK + 1]
    xw = xs * w_x.astype(BF16)

    pieces = []
    for g in range(SSD_GROUPS):
        bg = act[:, SSD_WIDTH + g * SSD_STATE:SSD_WIDTH + (g + 1) * SSD_STATE]
        cg = act[:, SSD_WIDTH + SSD_GROUPS * SSD_STATE + g * SSD_STATE:
                 SSD_WIDTH + SSD_GROUPS * SSD_STATE + (g + 1) * SSD_STATE]
        scores = lax.dot_general(cg, bg, (((1,), (1,)), ((), ())), preferred_element_type=F32)
        heads_per_group = SSD_HEADS // SSD_GROUPS
        for pair in range(heads_per_group // 2):
            h0 = g * heads_per_group + 2 * pair
            xs_pair = xs[:, h0 * SSD_HEAD_DIM:(h0 + 2) * SSD_HEAD_DIM]
            y_pair = None
            for k in range(2):
                hh = off + h0 + k
                seg = cs[:, hh:hh + 1] - cs_t[hh:hh + 1, :]
                dec = jnp.exp(jnp.where(keep, seg, -jnp.inf))
                m = (scores * dec * dt_t[hh:hh + 1, :]).astype(BF16)
                rhs = jnp.where(first_half if k == 0 else jnp.logical_not(first_half), xs_pair, zero)
                part = jnp.dot(m, rhs, preferred_element_type=F32)
                y_pair = part if y_pair is None else y_pair + part
            pieces.append(y_pair)
    y_diag = jnp.concatenate(pieces, axis=1)

    y_off = []
    new_state = []
    for g in range(SSD_GROUPS):
        gcols = slice(g * GROUP_WIDTH, (g + 1) * GROUP_WIDTH)
        bg = act[:, SSD_WIDTH + g * SSD_STATE:SSD_WIDTH + (g + 1) * SSD_STATE]
        cg = act[:, SSD_WIDTH + SSD_GROUPS * SSD_STATE + g * SSD_STATE:
                 SSD_WIDTH + SSD_GROUPS * SSD_STATE + (g + 1) * SSD_STATE]
        prev = state[:, gcols]
        y_off.append(jnp.dot(cg, prev.astype(BF16), preferred_element_type=F32))
        bg_t = bg.astype(F32).T.astype(BF16)
        new = jnp.dot(bg_t, xw[:, gcols], preferred_element_type=F32)
        new_state.append(prev * cd_x[:, gcols] + new)
    y = y_diag + jnp.concatenate(y_off, axis=1) * into_x
    return y, jnp.concatenate(new_state, axis=1)


def _ssd_body(xbc_ref, xprev_ref, xnext_ref, dt_ref, z_ref, ya_ref, x_ref, gate_ref, cw_ref, cb_ref,
              dtb_ref, alog_ref, dsk_ref, ng_ref, wout_ref, exp_ref, o_ref,
              act_scr, dts_scr, yf_scr, st_scr, ext_scr, *, n_steps):
    d = pl.program_id(1)
    c = pl.program_id(2)
    lane1 = lax.broadcasted_iota(jnp.int32, (1, LANES), 1)
    a_row = jnp.where(lane1 < 2 * SSD_HEADS, -jnp.exp(alog_ref[...]), 0.0)
    chunk_rows = [slice(j * CHUNK, (j + 1) * CHUNK) for j in range(SSD_STEP_ROWS // CHUNK)]

    @pl.when(c == 0)
    def _():
        st_scr[...] = jnp.zeros_like(st_scr)

    def scan(act, dtv, expand, rev):
        state = st_scr[...]
        ys = [None] * len(chunk_rows)
        for j in (reversed(range(len(chunk_rows))) if rev else range(len(chunk_rows))):
            ys[j], state = _ssd_chunk(act[chunk_rows[j], :], dtv[chunk_rows[j], :], a_row, expand, state, rev)
        st_scr[...] = state
        return jnp.concatenate(ys, axis=0)

    @pl.when(d == 0)
    def _forward():
        row0 = pl.multiple_of(c * SSD_STEP_ROWS, SSD_STEP_ROWS)
        ext_scr[0:SUBLANES, :] = jnp.where(c > 0, xprev_ref[...], 0.0)
        ext_scr[SUBLANES:SUBLANES + SSD_STEP_ROWS, :] = xbc_ref[...]
        ext_scr[SUBLANES + SSD_STEP_ROWS:, :] = jnp.where(c < n_steps - 1, xnext_ref[...], 0.0)
        acc = cb_ref[...]
        for k in range(CONV_WIDTH):
            lo = SUBLANES - CONV_WIDTH // 2 + k
            acc = acc + cw_ref[k:k + 1, :] * ext_scr[lo:lo + SSD_STEP_ROWS, :]
        act = _silu(acc).astype(BF16)
        dtv = _softplus(dt_ref[...] + dtb_ref[...])
        act_scr[pl.ds(row0, SSD_STEP_ROWS), :] = act
        dts_scr[pl.ds(row0, SSD_STEP_ROWS), :] = dtv
        yf_scr[pl.ds(row0, SSD_STEP_ROWS), :] = scan(act, dtv, exp_ref[0], rev=False).astype(BF16)

    @pl.when(d == 1)
    def _backward():
        row0 = pl.multiple_of((n_steps - 1 - c) * SSD_STEP_ROWS, SSD_STEP_ROWS)
        act = act_scr[pl.ds(row0, SSD_STEP_ROWS), :]
        dtv = dts_scr[pl.ds(row0, SSD_STEP_ROWS), :]
        yb = scan(act, dtv, exp_ref[1], rev=True)
        xs = act[:, :SSD_WIDTH].astype(F32)
        y = yf_scr[pl.ds(row0, SSD_STEP_ROWS), :].astype(F32) + yb + dsk_ref[...] * xs
        y = y * _silu(z_ref[...].astype(F32))
        y = jnp.concatenate(
            [_rms(y[:, g * GROUP_WIDTH:(g + 1) * GROUP_WIDTH]) for g in range(SSD_GROUPS)], axis=1)
        y = y * ng_ref[...]
        mix = jnp.concatenate([ya_ref[...], y.astype(BF16)], axis=1)
        o = jnp.dot(mix, wout_ref[...], preferred_element_type=F32)
        o_ref[...] = x_ref[...] + gate_ref[...] * o


def _ssd_outproj(xbc, dt, z, ya, x2, mod3, conv_w8, conv_b, dtb_row, alog_row, dsk_row, norm_g, w_out_b,
                 expand, bn, seq):
    t = x2.shape[0]
    nc = seq // SSD_STEP_ROWS
    blocks8 = SSD_STEP_ROWS // SUBLANES
    last8 = t // SUBLANES - 1

    def fwd_chunk(b, d, c):
        return b * nc + c + d * (nc - 1 - c)

    def bwd_chunk(b, d, c):
        return b * nc + nc - 1 - d * c

    const2 = lambda b, d, c: (0, 0)
    return pl.pallas_call(
        functools.partial(_ssd_body, n_steps=nc),
        grid=(bn, 2, nc),
        in_specs=[
            pl.BlockSpec((SSD_STEP_ROWS, CONV_CH), lambda b, d, c: (fwd_chunk(b, d, c), 0)),
            pl.BlockSpec((SUBLANES, CONV_CH),
                         lambda b, d, c: (jnp.maximum(fwd_chunk(b, d, c) * blocks8 - 1, 0), 0)),
            pl.BlockSpec((SUBLANES, CONV_CH),
                         lambda b, d, c: (jnp.minimum((fwd_chunk(b, d, c) + 1) * blocks8, last8), 0)),
            pl.BlockSpec((SSD_STEP_ROWS, LANES), lambda b, d, c: (fwd_chunk(b, d, c), 0)),
            pl.BlockSpec((SSD_STEP_ROWS, SSD_WIDTH), lambda b, d, c: (bwd_chunk(b, d, c), 0)),
            pl.BlockSpec((SSD_STEP_ROWS, GMLP_WIDTH), lambda b, d, c: (bwd_chunk(b, d, c), 0)),
            pl.BlockSpec((SSD_STEP_ROWS, D_MODEL), lambda b, d, c: (bwd_chunk(b, d, c), 0)),
            pl.BlockSpec((None, 1, D_MODEL), lambda b, d, c: (b, 0, 2)),
            pl.BlockSpec((SUBLANES, CONV_CH), const2),
            pl.BlockSpec((1, CONV_CH), const2),
            pl.BlockSpec((1, LANES), const2),
            pl.BlockSpec((1, LANES), const2),
            pl.BlockSpec((1, SSD_WIDTH), const2),
            pl.BlockSpec((1, SSD_WIDTH), const2),
            pl.BlockSpec((GMLP_WIDTH + SSD_WIDTH, D_MODEL), const2),
            pl.BlockSpec((2, LANES, SSD_WIDTH), lambda b, d, c: (0, 0, 0)),
        ],
        out_specs=pl.BlockSpec((SSD_STEP_ROWS, D_MODEL), lambda b, d, c: (bwd_chunk(b, d, c), 0)),
        out_shape=jax.ShapeDtypeStruct((t, D_MODEL), F32),
        scratch_shapes=[
            pltpu.VMEM((seq, CONV_CH), BF16),
            pltpu.VMEM((seq, LANES), F32),
            pltpu.VMEM((seq, SSD_WIDTH), BF16),
            pltpu.VMEM((SSD_STATE, SSD_WIDTH), F32),
            pltpu.VMEM((SSD_STEP_ROWS + 2 * SUBLANES, CONV_CH), F32),
        ],
        compiler_params=pltpu.CompilerParams(
            dimension_semantics=("arbitrary", "arbitrary", "arbitrary"), vmem_limit_bytes=VMEM_LIMIT),
        name="ssd_outproj",
    )(xbc, xbc, xbc, dt, z, ya, x2, mod3, conv_w8, conv_b, dtb_row, alog_row, dsk_row, norm_g, w_out_b,
      expand)


def _route(logits):
    lane = lax.broadcasted_iota(jnp.int32, logits.shape, 1)
    big = jnp.int32(LANES)
    neg = -jnp.inf
    gmask = (lane >= N_EXPERTS) & (lane < N_EXPERTS + N_EGROUPS)
    gl = jnp.where(gmask, logits, neg)
    gmax = jnp.max(gl, axis=-1, keepdims=True)
    gidx = jnp.min(jnp.where(gl == gmax, lane, big), axis=-1, keepdims=True) - N_EXPERTS
    p_g = 1.0 / jnp.sum(jnp.where(gmask, jnp.exp(gl - gmax), 0.0), axis=-1, keepdims=True)
    lo = gidx * EXPERTS_PER_GROUP
    emask = (lane >= lo) & (lane < lo + EXPERTS_PER_GROUP)
    el = jnp.where(emask, logits, neg)
    v1 = jnp.max(el, axis=-1, keepdims=True)
    i1 = jnp.min(jnp.where(el == v1, lane, big), axis=-1, keepdims=True)
    el2 = jnp.where(lane == i1, neg, el)
    v2 = jnp.max(el2, axis=-1, keepdims=True)
    i2 = jnp.min(jnp.where(el2 == v2, lane, big), axis=-1, keepdims=True)
    e2 = jnp.exp(v2 - v1)
    den = 1.0 + e2
    w1 = p_g / den
    w2 = p_g * e2 / den
    return i1, i2, w1, w2


def _router_body(x_ref, shift_ref, scale_ref, g2_ref, wr_ref, br_ref, h_ref, rt_ref, cnt_ref):
    h = _rms(x_ref[...]) * g2_ref[...]
    h = h * (1.0 + scale_ref[...]) + shift_ref[...]
    hb = h.astype(BF16)
    h_ref[...] = hb
    h_lo = (h - hb.astype(F32)).astype(BF16)
    logits = (jnp.dot(hb, wr_ref[0], preferred_element_type=F32)
              + jnp.dot(h_lo, wr_ref[0], preferred_element_type=F32)
              + jnp.dot(hb, wr_ref[1], preferred_element_type=F32)) + br_ref[...]
    i1, i2, w1, w2 = _route(logits)
    lane = lax.broadcasted_iota(jnp.int32, (TM_MOE, LANES), 1)
    rt_ref[...] = jnp.where(lane == 0, i1.astype(F32), jnp.where(lane == 1, i2.astype(F32),
                            jnp.where(lane == 2, w1, jnp.where(lane == 3, w2, 0.0))))
    mask = jnp.where(lane == i1, 1.0, jnp.where(lane == i2, 1.0, 0.0))
    cnt_ref[...] = jnp.broadcast_to(jnp.sum(mask, axis=0, keepdims=True), (SUBLANES, LANES))


def _router(x1, mod3, norm2_g, w_router, b_router, seq):
    t = x1.shape[0]
    tiles_per_seq = seq // TM_MOE
    n_tiles = t // TM_MOE
    row = lambda i: (i, 0)
    const2 = lambda i: (0, 0)
    return pl.pallas_call(
        _router_body,
        grid=(n_tiles,),
        in_specs=[
            pl.BlockSpec((TM_MOE, D_MODEL), row),
            pl.BlockSpec((None, 1, D_MODEL), lambda i: (i // tiles_per_seq, 0, 3)),
            pl.BlockSpec((None, 1, D_MODEL), lambda i: (i // tiles_per_seq, 0, 4)),
            pl.BlockSpec((1, D_MODEL), const2),
            pl.BlockSpec((2, D_MODEL, LANES), lambda i: (0, 0, 0)),
            pl.BlockSpec((1, LANES), const2),
        ],
        out_specs=[
            pl.BlockSpec((TM_MOE, D_MODEL), row),
            pl.BlockSpec((TM_MOE, LANES), row),
            pl.BlockSpec((SUBLANES, LANES), row),
        ],
        out_shape=[
            jax.ShapeDtypeStruct((t, D_MODEL), BF16),
            jax.ShapeDtypeStruct((t, LANES), F32),
            jax.ShapeDtypeStruct((n_tiles * SUBLANES, LANES), F32),
        ],
        compiler_params=pltpu.CompilerParams(
            dimension_semantics=("arbitrary",), vmem_limit_bytes=VMEM_LIMIT),
        name="moe_router",
    )(x1, mod3, mod3, norm2_g, w_router, b_router)


def _row_copy(vmem_buf, v_start, hbm_ref, g_start, size, sem, to_hbm):
    v = vmem_buf.at[pl.ds(v_start, size), :]
    g = hbm_ref.at[pl.ds(g_start, size), :]
    return pltpu.make_async_copy(v, g, sem) if to_hbm else pltpu.make_async_copy(g, v, sem)


def _copy_rows(n, vmem_buf, v0, hbm_ref, g0, sem, *, to_hbm, wait):
    def run(v_off, g_off, size):
        v_start = 0 if v0 is None else pl.multiple_of(v0 + v_off, SEG_ALIGN)
        cp = _row_copy(vmem_buf, v_start, hbm_ref, pl.multiple_of(g0 + g_off, SEG_ALIGN), size, sem, to_hbm)
        if wait:
            cp.wait()
        else:
            cp.start()

    n_big = n // PIECE_ROWS

    def big_piece(k, carry):
        run(k * PIECE_ROWS, k * PIECE_ROWS, PIECE_ROWS)
        return carry

    lax.fori_loop(0, n_big, big_piece, 0)
    base = n_big * PIECE_ROWS
    rem = n - base
    for size in REM_SIZES:
        done = base + (rem // (2 * size)) * (2 * size)

        @pl.when((rem & size) != 0)
        def _():
            run(done, done, size)


def _segment_copies(seg_ref, loc_ref, goff_ref, tile, vmem_buf, hbm_ref, sem, *, to_hbm):
    base = tile * N_EXPERTS

    def per_expert(e, carry):
        _copy_rows(seg_ref[base + e], vmem_buf, loc_ref[base + e], hbm_ref, goff_ref[base + e], sem,
                   to_hbm=to_hbm, wait=False)
        return carry

    lax.fori_loop(0, N_EXPERTS, per_expert, 0)


def _segment_wait(seg_ref, loc_ref, tile, vmem_buf, hbm_ref, sem, *, to_hbm):
    last = tile * N_EXPERTS + N_EXPERTS - 1
    total = loc_ref[last] + seg_ref[last]
    for size in TOTAL_SIZES:
        @pl.when((total & size) != 0)
        def _():
            _row_copy(vmem_buf, 0, hbm_ref, 0, size, sem, to_hbm).wait()


def _dispatch_body(seg_ref, loc_ref, goff_ref, fill_ref, h_ref, rt_ref, locrow_ref, pos_ref, xs_hbm, buf, sem,
                   *, n_tiles, n_row_tiles):
    i = pl.program_id(0)
    slot = i % 2
    start = functools.partial(_segment_copies, seg_ref, loc_ref, goff_ref, hbm_ref=xs_hbm, to_hbm=True)
    wait = functools.partial(_segment_wait, seg_ref, loc_ref, hbm_ref=xs_hbm, to_hbm=True)

    @pl.when(i >= 2)
    def _():
        wait(i - 2, buf.at[slot], sem=sem.at[slot])

    rt = rt_ref[...]
    lane = lax.broadcasted_iota(jnp.int32, (TM_MOE, LANES), 1)
    lanef = lane.astype(F32)
    e1 = rt[:, 0:1]
    e2 = rt[:, 1:2]
    mask = jnp.where(lanef == e1, 1.0, jnp.where(lanef == e2, 1.0, 0.0))
    r = lax.broadcasted_iota(jnp.int32, (TM_MOE, TM_MOE), 0)
    cc = lax.broadcasted_iota(jnp.int32, (TM_MOE, TM_MOE), 1)
    strict = jnp.where(r > cc, 1.0, 0.0).astype(BF16)
    rank = jnp.dot(strict, mask.astype(BF16), preferred_element_type=F32)
    posall = rank + locrow_ref[...]
    pos1 = jnp.sum(jnp.where(lanef == e1, posall, 0.0), axis=-1, keepdims=True)
    pos2 = jnp.sum(jnp.where(lanef == e2, posall, 0.0), axis=-1, keepdims=True)
    pp = jnp.where(lane == 0, pos1, jnp.where(lane == 1, pos2, rt))
    pos_ref[...] = pp

    rowp = lax.broadcasted_iota(jnp.int32, (R_LOC, LANES), 0).astype(F32)
    blocks = []
    for k in range(TM_MOE // LANES):
        ppt = pp[k * LANES:(k + 1) * LANES, :].T
        blocks.append(jnp.where(rowp == ppt[0:1, :], 1.0, jnp.where(rowp == ppt[1:2, :], 1.0, 0.0)).astype(BF16))
    perm = jnp.concatenate(blocks, axis=1)
    buf[slot] = jnp.dot(perm, h_ref[...], preferred_element_type=F32).astype(BF16)
    start(i, buf.at[slot], sem=sem.at[slot])

    @pl.when(i == n_tiles - 1)
    def _():
        if n_tiles > 1:
            wait(i - 1, buf.at[1 - slot], sem=sem.at[1 - slot])
        wait(i, buf.at[slot], sem=sem.at[slot])
        zsrc = buf.at[1 - slot]
        zsem = sem.at[1 - slot]
        zsrc[0:TR_EXPERT, :] = jnp.zeros((TR_EXPERT, D_MODEL), BF16)
        for waiting in (False, True):
            def per_expert(e, carry):
                _copy_rows(fill_ref[N_EXPERTS + e], zsrc, None, xs_hbm, fill_ref[e], zsem,
                           to_hbm=True, wait=waiting)
                return carry

            def per_row_tile(k, carry):
                cp = _row_copy(zsrc, 0, xs_hbm, pl.multiple_of(k * TR_EXPERT, TR_EXPERT), TR_EXPERT, zsem, True)
                if waiting:
                    cp.wait()
                else:
                    cp.start()
                return carry

            lax.fori_loop(0, N_EXPERTS, per_expert, 0)
            lax.fori_loop(fill_ref[2 * N_EXPERTS], n_row_tiles, per_row_tile, 0)


def _dispatch(h2, rt, seg, loc, goff, fill, locrow, n_row_tiles):
    t = h2.shape[0]
    n_tiles = t // TM_MOE
    row = lambda i, *_: (i, 0)
    grid_spec = pltpu.PrefetchScalarGridSpec(
        num_scalar_prefetch=4,
        grid=(n_tiles,),
        in_specs=[
            pl.BlockSpec((TM_MOE, D_MODEL), row),
            pl.BlockSpec((TM_MOE, LANES), row),
            pl.BlockSpec((None, 1, LANES), lambda i, *_: (i, 0, 0)),
        ],
        out_specs=[
            pl.BlockSpec((TM_MOE, LANES), row),
            pl.BlockSpec(memory_space=pl.ANY),
        ],
        scratch_shapes=[
            pltpu.VMEM((2, R_LOC, D_MODEL), BF16),
            pltpu.SemaphoreType.DMA((2,)),
        ],
    )
    return pl.pallas_call(
        functools.partial(_dispatch_body, n_tiles=n_tiles, n_row_tiles=n_row_tiles),
        grid_spec=grid_spec,
        out_shape=[
            jax.ShapeDtypeStruct((t, LANES), F32),
            jax.ShapeDtypeStruct((n_row_tiles * TR_EXPERT, D_MODEL), BF16),
        ],
        compiler_params=pltpu.CompilerParams(
            dimension_semantics=("arbitrary",), vmem_limit_bytes=VMEM_LIMIT),
        name="moe_dispatch",
    )(seg, loc, goff, fill, h2, rt, locrow)


def _expert_body(te_ref, na_ref, xs_ref, w1a_ref, w3a_ref, w2a_ref, w1b_ref, w3b_ref, w2b_ref, y_ref):
    r = pl.program_id(0)
    n_active = na_ref[0]
    weights = ((w1a_ref, w3a_ref, w2a_ref), (w1b_ref, w3b_ref, w2b_ref))

    def swiglu(half):
        w1_ref, w3_ref, w2_ref = weights[half]
        rows = slice(half * TR_EXPERT, (half + 1) * TR_EXPERT)
        x = xs_ref[rows, :]
        a = _silu(jnp.dot(x, w1_ref[...], preferred_element_type=F32))
        a = a * jnp.dot(x, w3_ref[...], preferred_element_type=F32)
        y_ref[rows, :] = jnp.dot(a.astype(BF16), w2_ref[...], preferred_element_type=F32).astype(BF16)

    first = EXPERT_TILES_PER_STEP * r

    @pl.when(first + 1 < n_active)
    def _():
        swiglu(0)
        swiglu(1)

    @pl.when(first + 1 == n_active)
    def _():
        swiglu(0)
        y_ref[TR_EXPERT:, :] = jnp.zeros((TR_EXPERT, D_MODEL), BF16)

    @pl.when(first >= n_active)
    def _():
        y_ref[...] = jnp.zeros_like(y_ref)


def _experts(xs, tile_expert, n_active, w1b, w3b, w2b):
    n_rows = xs.shape[0]
    step_rows = EXPERT_TILES_PER_STEP * TR_EXPERT
    x_block = lambda r, te, na: (jnp.minimum(r, (na[0] - 1) // EXPERT_TILES_PER_STEP), 0)
    w_specs = []
    for half in range(EXPERT_TILES_PER_STEP):
        w_block = lambda r, te, na, half=half: (te[EXPERT_TILES_PER_STEP * r + half], 0, 0)
        w_specs += [pl.BlockSpec((None, D_MODEL, D_EXPERT), w_block),
                    pl.BlockSpec((None, D_MODEL, D_EXPERT), w_block),
                    pl.BlockSpec((None, D_EXPERT, D_MODEL), w_block)]
    grid_spec = pltpu.PrefetchScalarGridSpec(
        num_scalar_prefetch=2,
        grid=(n_rows // step_rows,),
        in_specs=[pl.BlockSpec((step_rows, D_MODEL), x_block)] + w_specs,
        out_specs=pl.BlockSpec((step_rows, D_MODEL), lambda r, te, na: (r, 0)),
    )
    return pl.pallas_call(
        _expert_body,
        grid_spec=grid_spec,
        out_shape=jax.ShapeDtypeStruct((n_rows, D_MODEL), BF16),
        compiler_params=pltpu.CompilerParams(
            dimension_semantics=("arbitrary",), vmem_limit_bytes=VMEM_LIMIT),
        name="moe_experts",
    )(tile_expert, n_active, xs, w1b, w3b, w2b, w1b, w3b, w2b)


def _combine_body(seg_ref, loc_ref, goff_ref, y_hbm, pos_ref, x_ref, gate_ref, fg_ref, o_ref, buf, sem, *,
                  n_tiles, final_norm):
    i = pl.program_id(0)
    slot = i % 2
    start = functools.partial(_segment_copies, seg_ref, loc_ref, goff_ref, hbm_ref=y_hbm, to_hbm=False)

    @pl.when(i == 0)
    def _():
        buf[...] = jnp.zeros_like(buf)
        start(0, buf.at[0], sem=sem.at[0])

    @pl.when(i + 1 < n_tiles)
    def _():
        start(i + 1, buf.at[1 - slot], sem=sem.at[1 - slot])

    pp = pos_ref[...]
    colp = lax.broadcasted_iota(jnp.int32, (TM_MOE, R_LOC), 1).astype(F32)
    wc = jnp.where(colp == pp[:, 0:1], pp[:, 2:3], jnp.where(colp == pp[:, 1:2], pp[:, 3:4], 0.0)).astype(BF16)
    _segment_wait(seg_ref, loc_ref, i, buf.at[slot], y_hbm, sem.at[slot], to_hbm=False)
    moe = jnp.dot(wc, buf[slot], preferred_element_type=F32)
    x2 = x_ref[...] + gate_ref[...] * moe
    o_ref[...] = _rms(x2) * fg_ref[...] if final_norm else x2


def _combine(y, pos, x1, mod3, final_g, seg, loc, goff, seq, final_norm):
    t = x1.shape[0]
    n_tiles = t // TM_MOE
    tiles_per_seq = seq // TM_MOE
    row = lambda i, *_: (i, 0)
    grid_spec = pltpu.PrefetchScalarGridSpec(
        num_scalar_prefetch=3,
        grid=(n_tiles,),
        in_specs=[
            pl.BlockSpec(memory_space=pl.ANY),
            pl.BlockSpec((TM_MOE, LANES), row),
            pl.BlockSpec((TM_MOE, D_MODEL), row),
            pl.BlockSpec((None, 1, D_MODEL), lambda i, *_: (i // tiles_per_seq, 0, 5)),
            pl.BlockSpec((1, D_MODEL), lambda i, *_: (0, 0)),
        ],
        out_specs=pl.BlockSpec((TM_MOE, D_MODEL), row),
        scratch_shapes=[
            pltpu.VMEM((2, R_LOC, D_MODEL), BF16),
            pltpu.SemaphoreType.DMA((2,)),
        ],
    )
    return pl.pallas_call(
        functools.partial(_combine_body, n_tiles=n_tiles, final_norm=final_norm),
        grid_spec=grid_spec,
        out_shape=jax.ShapeDtypeStruct((t, D_MODEL), F32),
        compiler_params=pltpu.CompilerParams(
            dimension_semantics=("arbitrary",), vmem_limit_bytes=VMEM_LIMIT),
        name="moe_combine",
    )(seg, loc, goff, y, pos, x1, mod3, final_g)


def _moe_plan(counts, n_tiles, n_row_tiles):
    cnt = counts.reshape(n_tiles, SUBLANES, LANES)[:, 0, :N_EXPERTS].astype(jnp.int32)
    seg = (cnt + SEG_ALIGN - 1) // SEG_ALIGN * SEG_ALIGN
    before_e = jnp.arange(N_EXPERTS)[:, None] < jnp.arange(N_EXPERTS)[None, :]
    before_t = jnp.arange(n_tiles)[:, None] > jnp.arange(n_tiles)[None, :]
    loc = jnp.sum(jnp.where(before_e[None], seg[:, :, None], 0), axis=1)
    tot = jnp.sum(seg, axis=0)
    region = (tot + TR_EXPERT - 1) // TR_EXPERT * TR_EXPERT
    gstart = jnp.sum(jnp.where(before_e, region[:, None], 0), axis=0)
    gend = gstart + region
    goff = gstart[None, :] + jnp.sum(jnp.where(before_t[:, :, None], seg[None], 0), axis=1)
    n_active = gend[-1] // TR_EXPERT
    tile_row0 = jnp.arange(n_row_tiles, dtype=jnp.int32) * TR_EXPERT
    last_row0 = (n_active - 1) * TR_EXPERT
    te = jnp.sum(gend[None, :] <= jnp.minimum(tile_row0, last_row0)[:, None], axis=1).astype(jnp.int32)
    te = jnp.minimum(te, N_EXPERTS - 1)
    locrow = _pad_cols(loc.astype(F32), LANES).reshape(n_tiles, 1, LANES)
    n_active = n_active.reshape(1).astype(jnp.int32)
    fill = jnp.concatenate([gstart + tot, region - tot, n_active]).astype(jnp.int32)
    return (seg.reshape(-1), loc.reshape(-1).astype(jnp.int32), goff.reshape(-1).astype(jnp.int32), fill,
            locrow, te, n_active)


def _moe_final(x1, mod3, norm2_g, w_router, b_router, w1b, w3b, w2b, final_g, seq, final_norm):
    t = x1.shape[0]
    n_tiles = t // TM_MOE
    max_rows = N_ASSIGN * t + (SEG_ALIGN - 1) * N_EXPERTS * n_tiles + N_EXPERTS * (TR_EXPERT - SEG_ALIGN)
    step_rows = EXPERT_TILES_PER_STEP * TR_EXPERT
    n_row_tiles = -(-max_rows // step_rows) * EXPERT_TILES_PER_STEP
    h2, rt, counts = _router(x1, mod3, norm2_g, w_router, b_router, seq)
    seg, loc, goff, fill, locrow, te, n_active = _moe_plan(counts, n_tiles, n_row_tiles)
    pos, xs = _dispatch(h2, rt, seg, loc, goff, fill, locrow, n_row_tiles)
    y = _experts(xs, te, n_active, w1b, w3b, w2b)
    return _combine(y, pos, x1, mod3, final_g, seg, loc, goff, seq, final_norm)


def _pad_cols(a, width):
    return jnp.pad(a, ((0, 0), (0, width - a.shape[1])))


def kernel(x, c, w_ada, b_ada, norm1_g, w_in, b_in, gmlp_ln_g, gmlp_ln_b, gmlp_w_s, gmlp_b_s, gmlp_out_g, conv_w, conv_b, a_log_f, a_log_b, dt_bias_f, dt_bias_b, d_skip, ssd_norm_g, w_out, norm2_g, w_router_g, b_router_g, w_router_e, b_router_e, w1, w3, w2, final_g):
    bn, seq, _ = x.shape
    depth = w_ada.shape[0]
    x2 = x.reshape(bn * seq, D_MODEL)

    head_rows = jnp.arange(LANES)[:, None]
    head_cols = jnp.arange(SSD_WIDTH)[None, :] // SSD_HEAD_DIM
    expand = jnp.stack([head_rows == head_cols, head_rows == head_cols + SSD_HEADS]).astype(BF16)

    for l in range(depth):
        mod3 = _modulation(c, w_ada[l], b_ada[l]).reshape(bn, 1, N_MOD * D_MODEL)

        w_in_p = _pad_cols(w_in[l], IN_COLS_PAD).astype(BF16)
        b_in_p = _pad_cols(b_in[l][None, :], IN_COLS_PAD)
        bs = jnp.repeat(gmlp_b_s[l].T, LANES, axis=1)
        bs_tile = jnp.tile(bs, (TM_PROJ // CHUNK, 1))
        ya, z, xbc, dt = _inproj_gmlp(
            x2, mod3, norm1_g[l][None, :], w_in_p, b_in_p, gmlp_ln_g[l][None, :], gmlp_ln_b[l][None, :],
            gmlp_w_s[l].astype(BF16), bs_tile, gmlp_out_g[l][None, :], seq)

        conv_w8 = jnp.pad(conv_w[l], ((0, SUBLANES - CONV_WIDTH), (0, 0)))
        dtb_row = _pad_cols(jnp.concatenate([dt_bias_f[l], dt_bias_b[l]])[None, :], LANES)
        alog_row = _pad_cols(jnp.concatenate([a_log_f[l], a_log_b[l]])[None, :], LANES)
        dsk_row = jnp.repeat(d_skip[l], SSD_HEAD_DIM)[None, :]
        x2 = _ssd_outproj(xbc, dt, z, ya, x2, mod3, conv_w8, conv_b[l][None, :], dtb_row, alog_row, dsk_row,
                          ssd_norm_g[l][None, :], w_out[l].astype(BF16), expand, bn, seq)

        w_re = jnp.transpose(w_router_e[l], (1, 0, 2)).reshape(D_MODEL, N_EXPERTS)
        w_router = _pad_cols(jnp.concatenate([w_re, w_router_g[l]], axis=1), LANES)
        w_router_hi = w_router.astype(BF16)
        w_router = jnp.stack([w_router_hi, (w_router - w_router_hi.astype(F32)).astype(BF16)])
        b_router = _pad_cols(jnp.concatenate([b_router_e[l].reshape(-1), b_router_g[l]])[None, :], LANES)
        x2 = _moe_final(x2, mod3, norm2_g[l][None, :], w_router, b_router, w1[l].astype(BF16),
                        w3[l].astype(BF16), w2[l].astype(BF16), final_g[None, :], seq,
                        final_norm=(l == depth - 1))
    return x2.reshape(bn, seq, D_MODEL)
```

```python
import functools
import math

import jax
import jax.numpy as jnp
from jax import lax
from jax.experimental import pallas as pl
from jax.experimental.pallas import tpu as pltpu

F32 = jnp.float32
BF16 = jnp.bfloat16
HIGHEST = lax.Precision.HIGHEST

D_MODEL = 1024
N_MOD = 6
GMLP_WIDTH = 1024
GMLP_HEADS = 8
CHUNK = 128
SSD_STEP_ROWS = 4 * CHUNK
SSD_WIDTH = 1024
SSD_HEADS = 16
SSD_HEAD_DIM = 64
SSD_GROUPS = 2
SSD_STATE = 128
GROUP_WIDTH = SSD_WIDTH // SSD_GROUPS
CONV_WIDTH = 5
CONV_CH = SSD_WIDTH + 2 * SSD_GROUPS * SSD_STATE
N_EGROUPS = 4
EXPERTS_PER_GROUP = 8
N_EXPERTS = 32
D_EXPERT = 256
EPS = 1e-6

LANES = 128
SUBLANES = 8
COL_U, COL_V, COL_Z, COL_XBC, COL_DT = 0, 1024, 2048, 3072, 4608
IN_COLS = 4640
IN_COLS_PAD = COL_DT + LANES
TM_PROJ = 512
PROJ_ROWS = 256
TM_MOE = 512
TR_EXPERT = 256
EXPERT_TILES_PER_STEP = 2
N_ASSIGN = 2
SEG_ALIGN = 2 * SUBLANES
R_LOC = N_ASSIGN * TM_MOE + N_EXPERTS * SEG_ALIGN
PIECE_ROWS = 128
REM_SIZES = tuple(PIECE_ROWS >> s for s in range(1, PIECE_ROWS.bit_length()) if PIECE_ROWS >> s >= SEG_ALIGN)
TOTAL_SIZES = tuple(1 << s for s in range(R_LOC.bit_length() - 1, -1, -1) if 1 << s >= SEG_ALIGN)
VMEM_LIMIT = 56 * 1024 * 1024


def _silu(v):
    return v * jax.nn.sigmoid(v)


def _gelu(v):
    return 0.5 * v * (1.0 + lax.erf(v * math.sqrt(0.5)))


def _softplus(v):
    return jnp.maximum(v, 0.0) + jnp.log1p(jnp.exp(-jnp.abs(v)))


def _rms(v):
    return v * lax.rsqrt(jnp.mean(v * v, axis=-1, keepdims=True) + EPS)


def _mod_body(c_ref, w_ref, b_ref, o_ref):
    ca = _silu(c_ref[...])
    o_ref[...] = jnp.dot(ca, w_ref[...], precision=HIGHEST, preferred_element_type=F32) + b_ref[...]


def _modulation(c, w_ada, b_ada, layer):
    bn = c.shape[0]
    return pl.pallas_call(
        _mod_body,
        grid=(N_MOD,),
        in_specs=[
            pl.BlockSpec((bn, D_MODEL), lambda j: (0, 0)),
            pl.BlockSpec((None, D_MODEL, D_MODEL), lambda j: (layer, 0, j)),
            pl.BlockSpec((1, D_MODEL), lambda j: (0, j)),
        ],
        out_specs=pl.BlockSpec((bn, D_MODEL), lambda j: (0, j)),
        out_shape=jax.ShapeDtypeStruct((bn, N_MOD * D_MODEL), F32),
        name="adaln_mod",
    )(c, w_ada, b_ada.reshape(1, -1))


def _inproj_body(x_ref, shift_ref, scale_ref, g_ref, w_ref, b_ref, lng_ref, lnb_ref, ws_ref, bs_ref,
                 og_ref, ya_ref, z_ref, xbc_ref, dt_ref, mix_scr):
    for part in range(TM_PROJ // PROJ_ROWS):
        rows = slice(part * PROJ_ROWS, (part + 1) * PROJ_ROWS)
        h = _rms(x_ref[rows, :]) * g_ref[...]
        h = h * (1.0 + scale_ref[...]) + shift_ref[...]
        hb = h.astype(BF16)

        def proj(lo, hi, hb=hb):
            return jnp.dot(hb, w_ref[:, lo:hi], preferred_element_type=F32) + b_ref[:, lo:hi]

        z_ref[rows, :] = proj(COL_Z, COL_XBC).astype(BF16)
        xbc_ref[rows, :] = proj(COL_XBC, COL_DT)
        dt_ref[rows, :] = proj(COL_DT, IN_COLS_PAD)

        v = _gelu(proj(COL_V, COL_Z))
        mu = jnp.mean(v, axis=-1, keepdims=True)
        vc = v - mu
        var = jnp.mean(vc * vc, axis=-1, keepdims=True)
        vn = (vc * lax.rsqrt(var + EPS) * lng_ref[...] + lnb_ref[...]).astype(BF16)
        n_chunks = PROJ_ROWS // CHUNK
        for hd in range(GMLP_HEADS):
            cols = slice(hd * LANES, (hd + 1) * LANES)
            rhs = jnp.concatenate([vn[c * CHUNK:(c + 1) * CHUNK, cols] for c in range(n_chunks)], axis=1)
            res = jnp.dot(ws_ref[hd], rhs, preferred_element_type=F32)
            for c in range(n_chunks):
                mix_scr[part * PROJ_ROWS + c * CHUNK:part * PROJ_ROWS + (c + 1) * CHUNK, cols] = (
                    res[:, c * LANES:(c + 1) * LANES])
        u = _gelu(proj(COL_U, COL_V))
        out = u * (mix_scr[rows, :] + bs_ref[rows, :])
        ya_ref[rows, :] = (_rms(out) * og_ref[...]).astype(BF16)


def _inproj_gmlp(x2, mod3, norm1_g, w_in_p, b_in_p, ln_g, ln_b, w_s, bs_tile, out_g, seq):
    t = x2.shape[0]
    tiles_per_seq = seq // TM_PROJ
    row = lambda i: (i, 0)
    const2 = lambda i: (0, 0)
    return pl.pallas_call(
        _inproj_body,
        grid=(t // TM_PROJ,),
        in_specs=[
            pl.BlockSpec((TM_PROJ, D_MODEL), row),
            pl.BlockSpec((None, 1, D_MODEL), lambda i: (i // tiles_per_seq, 0, 0)),
            pl.BlockSpec((None, 1, D_MODEL), lambda i: (i // tiles_per_seq, 0, 1)),
            pl.BlockSpec((1, D_MODEL), const2),
            pl.BlockSpec((D_MODEL, IN_COLS_PAD), const2),
            pl.BlockSpec((1, IN_COLS_PAD), const2),
            pl.BlockSpec((1, GMLP_WIDTH), const2),
            pl.BlockSpec((1, GMLP_WIDTH), const2),
            pl.BlockSpec((GMLP_HEADS, CHUNK, CHUNK), lambda i: (0, 0, 0)),
            pl.BlockSpec((TM_PROJ, GMLP_WIDTH), const2),
            pl.BlockSpec((1, GMLP_WIDTH), const2),
        ],
        out_specs=[
            pl.BlockSpec((TM_PROJ, GMLP_WIDTH), row),
            pl.BlockSpec((TM_PROJ, SSD_WIDTH), row),
            pl.BlockSpec((TM_PROJ, CONV_CH), row),
            pl.BlockSpec((TM_PROJ, LANES), row),
        ],
        out_shape=[
            jax.ShapeDtypeStruct((t, GMLP_WIDTH), BF16),
            jax.ShapeDtypeStruct((t, SSD_WIDTH), BF16),
            jax.ShapeDtypeStruct((t, CONV_CH), F32),
            jax.ShapeDtypeStruct((t, LANES), F32),
        ],
        scratch_shapes=[pltpu.VMEM((TM_PROJ, GMLP_WIDTH), F32)],
        compiler_params=pltpu.CompilerParams(
            dimension_semantics=("arbitrary",), vmem_limit_bytes=VMEM_LIMIT),
        name="inproj_gmlp",
    )(x2, mod3, mod3, norm1_g, w_in_p, b_in_p, ln_g, ln_b, w_s, bs_tile, out_g)


def _ssd_chunk(act, dtv, a_row, expand, state, rev):
    off = SSD_HEADS if rev else 0
    row = lax.broadcasted_iota(jnp.int32, (CHUNK, CHUNK), 0)
    col = lax.broadcasted_iota(jnp.int32, (CHUNK, CHUNK), 1)
    lower = row >= col
    upper = row <= col
    keep = upper if rev else lower
    da = dtv * a_row
    cs = jnp.dot(keep.astype(F32), da, precision=HIGHEST, preferred_element_type=F32)
    cs_t = cs.T
    dt_t = dtv.T
    tot = cs[0:1, :] if rev else cs[CHUNK - 1:CHUNK, :]

    xs = act[:, :SSD_WIDTH]
    lane = lax.broadcasted_iota(jnp.int32, (CHUNK, LANES), 1)
    first_half = lane < SSD_HEAD_DIM
    zero = jnp.zeros((), BF16)

    stack = jnp.concatenate(
        [jnp.exp(cs), dtv * jnp.exp(tot - cs), jnp.broadcast_to(jnp.exp(tot), (SUBLANES, LANES))], axis=0)
    stack_x = jnp.dot(stack.astype(BF16), expand, preferred_element_type=F32)
    into_x = stack_x[:CHUNK]
    w_x = stack_x[CHUNK:2 * CHUNK]
    cd_x = stack_x[2 * CHUNK:2 * CHUNK + 1]
    xw = xs * w_x.astype(BF16)

    pieces = []
    for g in range(SSD_GROUPS):
        bg = act[:, SSD_WIDTH + g * SSD_STATE:SSD_WIDTH + (g + 1) * SSD_STATE]
        cg = act[:, SSD_WIDTH + SSD_GROUPS * SSD_STATE + g * SSD_STATE:
                 SSD_WIDTH + SSD_GROUPS * SSD_STATE + (g + 1) * SSD_STATE]
        scores = lax.dot_general(cg, bg, (((1,), (1,)), ((), ())), preferred_element_type=F32)
        heads_per_group = SSD_HEADS // SSD_GROUPS
        for pair in range(heads_per_group // 2):
            h0 = g * heads_per_group + 2 * pair
            xs_pair = xs[:, h0 * SSD_HEAD_DIM:(h0 + 2) * SSD_HEAD_DIM]
            y_pair = None
            for k in range(2):
                hh = off + h0 + k
                seg = cs[:, hh:hh + 1] - cs_t[hh:hh + 1, :]
                dec = jnp.exp(jnp.where(keep, seg, -jnp.inf))
                m = (scores * dec * dt_t[hh:hh + 1, :]).astype(BF16)
                rhs = jnp.where(first_half if k == 0 else jnp.logical_not(first_half), xs_pair, zero)
                part = jnp.dot(m, rhs, preferred_element_type=F32)
                y_pair = part if y_pair is None else y_pair + part
            pieces.append(y_pair)
    y_diag = jnp.concatenate(pieces, axis=1)

    y_off = []
    new_state = []
    for g in range(SSD_GROUPS):
        gcols = slice(g * GROUP_WIDTH, (g + 1) * GROUP_WIDTH)
        bg = act[:, SSD_WIDTH + g * SSD_STATE:SSD_WIDTH + (g + 1) * SSD_STATE]
        cg = act[:, SSD_WIDTH + SSD_GROUPS * SSD_STATE + g * SSD_STATE:
                 SSD_WIDTH + SSD_GROUPS * SSD_STATE + (g + 1) * SSD_STATE]
        prev = state[:, gcols]
        y_off.append(jnp.dot(cg, prev.astype(BF16), preferred_element_type=F32))
        bg_t = bg.astype(F32).T.astype(BF16)
        new = jnp.dot(bg_t, xw[:, gcols], preferred_element_type=F32)
        new_state.append(prev * cd_x[:, gcols] + new)
    y = y_diag + jnp.concatenate(y_off, axis=1) * into_x
    return y, jnp.concatenate(new_state, axis=1)


def _ssd_body(xbc_ref, xprev_ref, xnext_ref, dt_ref, z_ref, ya_ref, x_ref, gate_ref, cw_ref, cb_ref,
              dtb_ref, alog_ref, dsk_ref, ng_ref, wout_ref, exp_ref, o_ref,
              act_scr, dts_scr, yf_scr, st_scr, ext_scr, *, n_steps):
    d = pl.program_id(1)
    c = pl.program_id(2)
    lane1 = lax.broadcasted_iota(jnp.int32, (1, LANES), 1)
    a_row = jnp.where(lane1 < 2 * SSD_HEADS, -jnp.exp(alog_ref[...]), 0.0)
    chunk_rows = [slice(j * CHUNK, (j + 1) * CHUNK) for j in range(SSD_STEP_ROWS // CHUNK)]

    @pl.when(c == 0)
    def _():
        st_scr[...] = jnp.zeros_like(st_scr)

    def scan(act, dtv, expand, rev):
        state = st_scr[...]
        ys = [None] * len(chunk_rows)
        for j in (reversed(range(len(chunk_rows))) if rev else range(len(chunk_rows))):
            ys[j], state = _ssd_chunk(act[chunk_rows[j], :], dtv[chunk_rows[j], :], a_row, expand, state, rev)
        st_scr[...] = state
        return jnp.concatenate(ys, axis=0)

    @pl.when(d == 0)
    def _forward():
        row0 = pl.multiple_of(c * SSD_STEP_ROWS, SSD_STEP_ROWS)
        ext_scr[0:SUBLANES, :] = jnp.where(c > 0, xprev_ref[...], 0.0)
        ext_scr[SUBLANES:SUBLANES + SSD_STEP_ROWS, :] = xbc_ref[...]
        ext_scr[SUBLANES + SSD_STEP_ROWS:, :] = jnp.where(c < n_steps - 1, xnext_ref[...], 0.0)
        acc = cb_ref[...]
        for k in range(CONV_WIDTH):
            lo = SUBLANES - CONV_WIDTH // 2 + k
            acc = acc + cw_ref[k:k + 1, :] * ext_scr[lo:lo + SSD_STEP_ROWS, :]
        act = _silu(acc).astype(BF16)
        dtv = _softplus(dt_ref[...] + dtb_ref[...])
        act_scr[pl.ds(row0, SSD_STEP_ROWS), :] = act
        dts_scr[pl.ds(row0, SSD_STEP_ROWS), :] = dtv
        yf_scr[pl.ds(row0, SSD_STEP_ROWS), :] = scan(act, dtv, exp_ref[0], rev=False).astype(BF16)

    @pl.when(d == 1)
    def _backward():
        row0 = pl.multiple_of((n_steps - 1 - c) * SSD_STEP_ROWS, SSD_STEP_ROWS)
        act = act_scr[pl.ds(row0, SSD_STEP_ROWS), :]
        dtv = dts_scr[pl.ds(row0, SSD_STEP_ROWS), :]
        yb = scan(act, dtv, exp_ref[1], rev=True)
        xs = act[:, :SSD_WIDTH].astype(F32)
        y = yf_scr[pl.ds(row0, SSD_STEP_ROWS), :].astype(F32) + yb + dsk_ref[...] * xs
        y = y * _silu(z_ref[...].astype(F32))
        y = jnp.concatenate(
            [_rms(y[:, g * GROUP_WIDTH:(g + 1) * GROUP_WIDTH]) for g in range(SSD_GROUPS)], axis=1)
        y = y * ng_ref[...]
        mix = jnp.concatenate([ya_ref[...], y.astype(BF16)], axis=1)
        o = jnp.dot(mix, wout_ref[...], preferred_element_type=F32)
        o_ref[...] = x_ref[...] + gate_ref[...] * o


def _ssd_outproj(xbc, dt, z, ya, x2, mod3, conv_w8, conv_b, dtb_row, alog_row, dsk_row, norm_g, w_out_b,
                 expand, bn, seq):
    t = x2.shape[0]
    nc = seq // SSD_STEP_ROWS
    blocks8 = SSD_STEP_ROWS // SUBLANES
    last8 = t // SUBLANES - 1

    def fwd_chunk(b, d, c):
        return b * nc + c + d * (nc - 1 - c)

    def bwd_chunk(b, d, c):
        return b * nc + nc - 1 - d * c

    const2 = lambda b, d, c: (0, 0)
    return pl.pallas_call(
        functools.partial(_ssd_body, n_steps=nc),
        grid=(bn, 2, nc),
        in_specs=[
            pl.BlockSpec((SSD_STEP_ROWS, CONV_CH), lambda b, d, c: (fwd_chunk(b, d, c), 0)),
            pl.BlockSpec((SUBLANES, CONV_CH),
                         lambda b, d, c: (jnp.maximum(fwd_chunk(b, d, c) * blocks8 - 1, 0), 0)),
            pl.BlockSpec((SUBLANES, CONV_CH),
                         lambda b, d, c: (jnp.minimum((fwd_chunk(b, d, c) + 1) * blocks8, last8), 0)),
            pl.BlockSpec((SSD_STEP_ROWS, LANES), lambda b, d, c: (fwd_chunk(b, d, c), 0)),
            pl.BlockSpec((SSD_STEP_ROWS, SSD_WIDTH), lambda b, d, c: (bwd_chunk(b, d, c), 0)),
            pl.BlockSpec((SSD_STEP_ROWS, GMLP_WIDTH), lambda b, d, c: (bwd_chunk(b, d, c), 0)),
            pl.BlockSpec((SSD_STEP_ROWS, D_MODEL), lambda b, d, c: (bwd_chunk(b, d, c), 0)),
            pl.BlockSpec((None, 1, D_MODEL), lambda b, d, c: (b, 0, 2)),
            pl.BlockSpec((SUBLANES, CONV_CH), const2),
            pl.BlockSpec((1, CONV_CH), const2),
            pl.BlockSpec((1, LANES), const2),
            pl.BlockSpec((1, LANES), const2),
            pl.BlockSpec((1, SSD_WIDTH), const2),
            pl.BlockSpec((1, SSD_WIDTH), const2),
            pl.BlockSpec((GMLP_WIDTH + SSD_WIDTH, D_MODEL), const2, pipeline_mode=pl.Buffered(1)),
            pl.BlockSpec((2, LANES, SSD_WIDTH), lambda b, d, c: (0, 0, 0), pipeline_mode=pl.Buffered(1)),
        ],
        out_specs=pl.BlockSpec((SSD_STEP_ROWS, D_MODEL), lambda b, d, c: (bwd_chunk(b, d, c), 0)),
        out_shape=jax.ShapeDtypeStruct((t, D_MODEL), F32),
        scratch_shapes=[
            pltpu.VMEM((seq, CONV_CH), BF16),
            pltpu.VMEM((seq, LANES), F32),
            pltpu.VMEM((seq, SSD_WIDTH), BF16),
            pltpu.VMEM((SSD_STATE, SSD_WIDTH), F32),
            pltpu.VMEM((SSD_STEP_ROWS + 2 * SUBLANES, CONV_CH), F32),
        ],
        compiler_params=pltpu.CompilerParams(
            dimension_semantics=("arbitrary", "arbitrary", "arbitrary"), vmem_limit_bytes=VMEM_LIMIT),
        name="ssd_outproj",
    )(xbc, xbc, xbc, dt, z, ya, x2, mod3, conv_w8, conv_b, dtb_row, alog_row, dsk_row, norm_g, w_out_b,
      expand)


def _route(logits):
    lane = lax.broadcasted_iota(jnp.int32, logits.shape, 1)
    big = jnp.int32(LANES)
    neg = -jnp.inf
    gmask = (lane >= N_EXPERTS) & (lane < N_EXPERTS + N_EGROUPS)
    gl = jnp.where(gmask, logits, neg)
    gmax = jnp.max(gl, axis=-1, keepdims=True)
    gidx = jnp.min(jnp.where(gl == gmax, lane, big), axis=-1, keepdims=True) - N_EXPERTS
    p_g = 1.0 / jnp.sum(jnp.where(gmask, jnp.exp(gl - gmax), 0.0), axis=-1, keepdims=True)
    lo = gidx * EXPERTS_PER_GROUP
    emask = (lane >= lo) & (lane < lo + EXPERTS_PER_GROUP)
    el = jnp.where(emask, logits, neg)
    v1 = jnp.max(el, axis=-1, keepdims=True)
    i1 = jnp.min(jnp.where(el == v1, lane, big), axis=-1, keepdims=True)
    el2 = jnp.where(lane == i1, neg, el)
    v2 = jnp.max(el2, axis=-1, keepdims=True)
    i2 = jnp.min(jnp.where(el2 == v2, lane, big), axis=-1, keepdims=True)
    e2 = jnp.exp(v2 - v1)
    den = 1.0 + e2
    w1 = p_g / den
    w2 = p_g * e2 / den
    return i1, i2, w1, w2


def _router_body(x_ref, shift_ref, scale_ref, g2_ref, wr_ref, br_ref, h_ref, rt_ref, cnt_ref):
    h = _rms(x_ref[...]) * g2_ref[...]
    h = h * (1.0 + scale_ref[...]) + shift_ref[...]
    hb = h.astype(BF16)
    h_ref[...] = hb
    h_lo = (h - hb.astype(F32)).astype(BF16)
    logits = (jnp.dot(hb, wr_ref[0], preferred_element_type=F32)
              + jnp.dot(h_lo, wr_ref[0], preferred_element_type=F32)
              + jnp.dot(hb, wr_ref[1], preferred_element_type=F32)) + br_ref[...]
    i1, i2, w1, w2 = _route(logits)
    lane = lax.broadcasted_iota(jnp.int32, (TM_MOE, LANES), 1)
    rt_ref[...] = jnp.where(lane == 0, i1.astype(F32), jnp.where(lane == 1, i2.astype(F32),
                            jnp.where(lane == 2, w1, jnp.where(lane == 3, w2, 0.0))))
    mask = jnp.where(lane == i1, 1.0, jnp.where(lane == i2, 1.0, 0.0))
    cnt_ref[...] = jnp.broadcast_to(jnp.sum(mask, axis=0, keepdims=True), (SUBLANES, LANES))


def _router(x1, mod3, norm2_g, w_router, b_router, seq):
    t = x1.shape[0]
    tiles_per_seq = seq // TM_MOE
    n_tiles = t // TM_MOE
    row = lambda i: (i, 0)
    const2 = lambda i: (0, 0)
    return pl.pallas_call(
        _router_body,
        grid=(n_tiles,),
        in_specs=[
            pl.BlockSpec((TM_MOE, D_MODEL), row),
            pl.BlockSpec((None, 1, D_MODEL), lambda i: (i // tiles_per_seq, 0, 3)),
            pl.BlockSpec((None, 1, D_MODEL), lambda i: (i // tiles_per_seq, 0, 4)),
            pl.BlockSpec((1, D_MODEL), const2),
            pl.BlockSpec((2, D_MODEL, LANES), lambda i: (0, 0, 0)),
            pl.BlockSpec((1, LANES), const2),
        ],
        out_specs=[
            pl.BlockSpec((TM_MOE, D_MODEL), row),
            pl.BlockSpec((TM_MOE, LANES), row),
            pl.BlockSpec((SUBLANES, LANES), row),
        ],
        out_shape=[
            jax.ShapeDtypeStruct((t, D_MODEL), BF16),
            jax.ShapeDtypeStruct((t, LANES), F32),
            jax.ShapeDtypeStruct((n_tiles * SUBLANES, LANES), F32),
        ],
        compiler_params=pltpu.CompilerParams(
            dimension_semantics=("arbitrary",), vmem_limit_bytes=VMEM_LIMIT),
        name="moe_router",
    )(x1, mod3, mod3, norm2_g, w_router, b_router)


def _row_copy(vmem_buf, v_start, hbm_ref, g_start, size, sem, to_hbm):
    v = vmem_buf.at[pl.ds(v_start, size), :]
    g = hbm_ref.at[pl.ds(g_start, size), :]
    return pltpu.make_async_copy(v, g, sem) if to_hbm else pltpu.make_async_copy(g, v, sem)


def _copy_rows(n, vmem_buf, v0, hbm_ref, g0, sem, *, to_hbm, wait):
    def run(v_off, g_off, size):
        v_start = 0 if v0 is None else pl.multiple_of(v0 + v_off, SEG_ALIGN)
        cp = _row_copy(vmem_buf, v_start, hbm_ref, pl.multiple_of(g0 + g_off, SEG_ALIGN), size, sem, to_hbm)
        if wait:
            cp.wait()
        else:
            cp.start()

    n_big = n // PIECE_ROWS

    def big_piece(k, carry):
        run(k * PIECE_ROWS, k * PIECE_ROWS, PIECE_ROWS)
        return carry

    lax.fori_loop(0, n_big, big_piece, 0)
    base = n_big * PIECE_ROWS
    rem = n - base
    for size in REM_SIZES:
        done = base + (rem // (2 * size)) * (2 * size)

        @pl.when((rem & size) != 0)
        def _():
            run(done, done, size)


def _segment_copies(seg_ref, loc_ref, goff_ref, tile, vmem_buf, hbm_ref, sem, *, to_hbm):
    base = tile * N_EXPERTS

    def per_expert(e, carry):
        _copy_rows(seg_ref[base + e], vmem_buf, loc_ref[base + e], hbm_ref, goff_ref[base + e], sem,
                   to_hbm=to_hbm, wait=False)
        return carry

    lax.fori_loop(0, N_EXPERTS, per_expert, 0)


def _tile_rows(seg_ref, loc_ref, tile):
    last = tile * N_EXPERTS + N_EXPERTS - 1
    return loc_ref[last] + seg_ref[last]


def _segment_wait(seg_ref, loc_ref, tile, vmem_buf, hbm_ref, sem, *, to_hbm):
    total = _tile_rows(seg_ref, loc_ref, tile)
    for size in TOTAL_SIZES:
        @pl.when((total & size) != 0)
        def _():
            _row_copy(vmem_buf, 0, hbm_ref, 0, size, sem, to_hbm).wait()


def _dispatch_body(seg_ref, loc_ref, goff_ref, fill_ref, h_ref, rt_ref, locrow_ref, pos_ref, xs_hbm, buf, sem,
                   *, n_tiles, n_row_tiles):
    i = pl.program_id(0)
    slot = i % 2
    start = functools.partial(_segment_copies, seg_ref, loc_ref, goff_ref, hbm_ref=xs_hbm, to_hbm=True)
    wait = functools.partial(_segment_wait, seg_ref, loc_ref, hbm_ref=xs_hbm, to_hbm=True)

    @pl.when(i >= 2)
    def _():
        wait(i - 2, buf.at[slot], sem=sem.at[slot])

    rt = rt_ref[...]
    lane = lax.broadcasted_iota(jnp.int32, (TM_MOE, LANES), 1)
    lanef = lane.astype(F32)
    e1 = rt[:, 0:1]
    e2 = rt[:, 1:2]
    mask = jnp.where(lanef == e1, 1.0, jnp.where(lanef == e2, 1.0, 0.0))
    r = lax.broadcasted_iota(jnp.int32, (TM_MOE, TM_MOE), 0)
    cc = lax.broadcasted_iota(jnp.int32, (TM_MOE, TM_MOE), 1)
    strict = jnp.where(r > cc, 1.0, 0.0).astype(BF16)
    rank = jnp.dot(strict, mask.astype(BF16), preferred_element_type=F32)
    posall = rank + locrow_ref[...]
    pos1 = jnp.sum(jnp.where(lanef == e1, posall, 0.0), axis=-1, keepdims=True)
    pos2 = jnp.sum(jnp.where(lanef == e2, posall, 0.0), axis=-1, keepdims=True)
    pp = jnp.where(lane == 0, pos1, jnp.where(lane == 1, pos2, rt))
    pos_ref[...] = pp

    slots_t = [pp[k * LANES:(k + 1) * LANES, :].T for k in range(TM_MOE // LANES)]

    rowp = lax.broadcasted_iota(jnp.int32, (R_LOC, LANES), 0).astype(F32)
    perm = jnp.concatenate(
        [jnp.where(rowp == st[0:1, :], 1.0, jnp.where(rowp == st[1:2, :], 1.0, 0.0)).astype(BF16)
         for st in slots_t], axis=1)
    buf[slot] = jnp.dot(perm, h_ref[...], preferred_element_type=F32).astype(BF16)
    start(i, buf.at[slot], sem=sem.at[slot])

    @pl.when(i == n_tiles - 1)
    def _():
        if n_tiles > 1:
            wait(i - 1, buf.at[1 - slot], sem=sem.at[1 - slot])
        wait(i, buf.at[slot], sem=sem.at[slot])
        zsrc = buf.at[1 - slot]
        zsem = sem.at[1 - slot]
        zsrc[0:TR_EXPERT, :] = jnp.zeros((TR_EXPERT, D_MODEL), BF16)
        for waiting in (False, True):
            def per_expert(e, carry):
                _copy_rows(fill_ref[N_EXPERTS + e], zsrc, None, xs_hbm, fill_ref[e], zsem,
                           to_hbm=True, wait=waiting)
                return carry

            def per_row_tile(k, carry):
                cp = _row_copy(zsrc, 0, xs_hbm, pl.multiple_of(k * TR_EXPERT, TR_EXPERT), TR_EXPERT, zsem, True)
                if waiting:
                    cp.wait()
                else:
                    cp.start()
                return carry

            lax.fori_loop(0, N_EXPERTS, per_expert, 0)
            lax.fori_loop(fill_ref[2 * N_EXPERTS], n_row_tiles, per_row_tile, 0)


def _dispatch(h2, rt, seg, loc, goff, fill, locrow, n_row_tiles):
    t = h2.shape[0]
    n_tiles = t // TM_MOE
    row = lambda i, *_: (i, 0)
    grid_spec = pltpu.PrefetchScalarGridSpec(
        num_scalar_prefetch=4,
        grid=(n_tiles,),
        in_specs=[
            pl.BlockSpec((TM_MOE, D_MODEL), row),
            pl.BlockSpec((TM_MOE, LANES), row),
            pl.BlockSpec((None, 1, LANES), lambda i, *_: (i, 0, 0)),
        ],
        out_specs=[
            pl.BlockSpec((TM_MOE, LANES), row),
            pl.BlockSpec(memory_space=pl.ANY),
        ],
        scratch_shapes=[
            pltpu.VMEM((2, R_LOC, D_MODEL), BF16),
            pltpu.SemaphoreType.DMA((2,)),
        ],
    )
    return pl.pallas_call(
        functools.partial(_dispatch_body, n_tiles=n_tiles, n_row_tiles=n_row_tiles),
        grid_spec=grid_spec,
        out_shape=[
            jax.ShapeDtypeStruct((t, LANES), F32),
            jax.ShapeDtypeStruct((n_row_tiles * TR_EXPERT, D_MODEL), BF16),
        ],
        compiler_params=pltpu.CompilerParams(
            dimension_semantics=("arbitrary",), vmem_limit_bytes=VMEM_LIMIT),
        name="moe_dispatch",
    )(seg, loc, goff, fill, h2, rt, locrow)


def _expert_body(te_ref, tw_ref, na_ref, xs_ref, w1a_ref, w3a_ref, w2a_ref, w1b_ref, w3b_ref, w2b_ref, y_ref):
    del tw_ref
    r = pl.program_id(0)
    n_active = na_ref[0]
    weights = ((w1a_ref, w3a_ref, w2a_ref), (w1b_ref, w3b_ref, w2b_ref))

    def swiglu(half, rows=None):
        w1_ref, w3_ref, w2_ref = weights[half]
        rows = slice(half * TR_EXPERT, (half + 1) * TR_EXPERT) if rows is None else rows
        x = xs_ref[rows, :]
        a = _silu(jnp.dot(x, w1_ref[...], preferred_element_type=F32))
        a = a * jnp.dot(x, w3_ref[...], preferred_element_type=F32)
        y_ref[rows, :] = jnp.dot(a.astype(BF16), w2_ref[...], preferred_element_type=F32).astype(BF16)

    first = EXPERT_TILES_PER_STEP * r
    both = first + 1 < n_active
    same = te_ref[first] == te_ref[first + 1]

    @pl.when(both & same)
    def _():
        swiglu(0, slice(0, EXPERT_TILES_PER_STEP * TR_EXPERT))

    @pl.when(both & jnp.logical_not(same))
    def _():
        swiglu(0)
        swiglu(1)

    @pl.when(first + 1 == n_active)
    def _():
        swiglu(0)
        y_ref[TR_EXPERT:, :] = jnp.zeros((TR_EXPERT, D_MODEL), BF16)

    @pl.when(first >= n_active)
    def _():
        y_ref[...] = jnp.zeros_like(y_ref)


def _experts(xs, tile_expert, tile_weights, n_active, w1b, w3b, w2b):
    n_rows = xs.shape[0]
    step_rows = EXPERT_TILES_PER_STEP * TR_EXPERT
    x_block = lambda r, te, tw, na: (jnp.minimum(r, (na[0] - 1) // EXPERT_TILES_PER_STEP), 0)
    w_specs = []
    for half in range(EXPERT_TILES_PER_STEP):
        w_block = lambda r, te, tw, na, half=half: (tw[EXPERT_TILES_PER_STEP * r + half], 0, 0)
        w_specs += [pl.BlockSpec((None, D_MODEL, D_EXPERT), w_block),
                    pl.BlockSpec((None, D_MODEL, D_EXPERT), w_block),
                    pl.BlockSpec((None, D_EXPERT, D_MODEL), w_block)]
    grid_spec = pltpu.PrefetchScalarGridSpec(
        num_scalar_prefetch=3,
        grid=(n_rows // step_rows,),
        in_specs=[pl.BlockSpec((step_rows, D_MODEL), x_block)] + w_specs,
        out_specs=pl.BlockSpec((step_rows, D_MODEL), lambda r, te, tw, na: (r, 0)),
    )
    return pl.pallas_call(
        _expert_body,
        grid_spec=grid_spec,
        out_shape=jax.ShapeDtypeStruct((n_rows, D_MODEL), BF16),
        compiler_params=pltpu.CompilerParams(
            dimension_semantics=("arbitrary",), vmem_limit_bytes=VMEM_LIMIT),
        name="moe_experts",
    )(tile_expert, tile_weights, n_active, xs, w1b, w3b, w2b, w1b, w3b, w2b)


def _combine_body(seg_ref, loc_ref, goff_ref, y_hbm, pos_ref, x_ref, gate_ref, fg_ref, o_ref, buf, sem, *,
                  n_tiles, final_norm):
    i = pl.program_id(0)
    slot = i % 2
    start = functools.partial(_segment_copies, seg_ref, loc_ref, goff_ref, hbm_ref=y_hbm, to_hbm=False)

    @pl.when(i == 0)
    def _():
        buf[...] = jnp.zeros_like(buf)
        start(0, buf.at[0], sem=sem.at[0])

    @pl.when(i + 1 < n_tiles)
    def _():
        start(i + 1, buf.at[1 - slot], sem=sem.at[1 - slot])

    pp = pos_ref[...]
    colp = lax.broadcasted_iota(jnp.int32, (TM_MOE, R_LOC), 1).astype(F32)
    wc = jnp.where(colp == pp[:, 0:1], pp[:, 2:3], jnp.where(colp == pp[:, 1:2], pp[:, 3:4], 0.0)).astype(BF16)
    _segment_wait(seg_ref, loc_ref, i, buf.at[slot], y_hbm, sem.at[slot], to_hbm=False)
    moe = jnp.dot(wc, buf[slot], preferred_element_type=F32)
    x2 = x_ref[...] + gate_ref[...] * moe
    o_ref[...] = _rms(x2) * fg_ref[...] if final_norm else x2


def _combine(y, pos, x1, mod3, final_g, seg, loc, goff, seq, final_norm):
    t = x1.shape[0]
    n_tiles = t // TM_MOE
    tiles_per_seq = seq // TM_MOE
    row = lambda i, *_: (i, 0)
    grid_spec = pltpu.PrefetchScalarGridSpec(
        num_scalar_prefetch=3,
        grid=(n_tiles,),
        in_specs=[
            pl.BlockSpec(memory_space=pl.ANY),
            pl.BlockSpec((TM_MOE, LANES), row),
            pl.BlockSpec((TM_MOE, D_MODEL), row),
            pl.BlockSpec((None, 1, D_MODEL), lambda i, *_: (i // tiles_per_seq, 0, 5)),
            pl.BlockSpec((1, D_MODEL), lambda i, *_: (0, 0)),
        ],
        out_specs=pl.BlockSpec((TM_MOE, D_MODEL), row),
        scratch_shapes=[
            pltpu.VMEM((2, R_LOC, D_MODEL), BF16),
            pltpu.SemaphoreType.DMA((2,)),
        ],
    )
    return pl.pallas_call(
        functools.partial(_combine_body, n_tiles=n_tiles, final_norm=final_norm),
        grid_spec=grid_spec,
        out_shape=jax.ShapeDtypeStruct((t, D_MODEL), F32),
        compiler_params=pltpu.CompilerParams(
            dimension_semantics=("arbitrary",), vmem_limit_bytes=VMEM_LIMIT),
        name="moe_combine",
    )(seg, loc, goff, y, pos, x1, mod3, final_g)


def _moe_plan(counts, n_tiles, n_row_tiles):
    cnt = counts.reshape(n_tiles, SUBLANES, LANES)[:, 0, :N_EXPERTS].astype(jnp.int32)
    seg = (cnt + SEG_ALIGN - 1) // SEG_ALIGN * SEG_ALIGN
    before_e = jnp.arange(N_EXPERTS)[:, None] < jnp.arange(N_EXPERTS)[None, :]
    before_t = jnp.arange(n_tiles)[:, None] > jnp.arange(n_tiles)[None, :]
    loc = jnp.sum(jnp.where(before_e[None], seg[:, :, None], 0), axis=1)
    tot = jnp.sum(seg, axis=0)
    region = (tot + TR_EXPERT - 1) // TR_EXPERT * TR_EXPERT
    gstart = jnp.sum(jnp.where(before_e, region[:, None], 0), axis=0)
    gend = gstart + region
    goff = gstart[None, :] + jnp.sum(jnp.where(before_t[:, :, None], seg[None], 0), axis=1)
    n_active = gend[-1] // TR_EXPERT
    tile_row0 = jnp.arange(n_row_tiles, dtype=jnp.int32) * TR_EXPERT
    last_row0 = (n_active - 1) * TR_EXPERT
    te = jnp.sum(gend[None, :] <= jnp.minimum(tile_row0, last_row0)[:, None], axis=1).astype(jnp.int32)
    te = jnp.minimum(te, N_EXPERTS - 1)
    te2 = te.reshape(-1, EXPERT_TILES_PER_STEP)
    cand = jnp.where(te2[:, 1] != te2[:, 0], te2[:, 1], 0)
    steps = jnp.arange(te2.shape[0])
    held = jnp.max(jnp.where(steps[None, :] <= steps[:, None], cand[None, :], 0), axis=1)
    tw = jnp.stack([te2[:, 0], held], axis=1).reshape(-1).astype(jnp.int32)
    locrow = _pad_cols(loc.astype(F32), LANES).reshape(n_tiles, 1, LANES)
    n_active = n_active.reshape(1).astype(jnp.int32)
    fill = jnp.concatenate([gstart + tot, region - tot, n_active]).astype(jnp.int32)
    return (seg.reshape(-1), loc.reshape(-1).astype(jnp.int32), goff.reshape(-1).astype(jnp.int32), fill,
            locrow, te, tw, n_active)


def _moe_final(x1, mod3, norm2_g, w_router, b_router, w1b, w3b, w2b, final_g, seq, final_norm):
    t = x1.shape[0]
    n_tiles = t // TM_MOE
    max_rows = N_ASSIGN * t + (SEG_ALIGN - 1) * N_EXPERTS * n_tiles + N_EXPERTS * (TR_EXPERT - SEG_ALIGN)
    step_rows = EXPERT_TILES_PER_STEP * TR_EXPERT
    n_row_tiles = -(-max_rows // step_rows) * EXPERT_TILES_PER_STEP
    h2, rt, counts = _router(x1, mod3, norm2_g, w_router, b_router, seq)
    seg, loc, goff, fill, locrow, te, tw, n_active = _moe_plan(counts, n_tiles, n_row_tiles)
    pos, xs = _dispatch(h2, rt, seg, loc, goff, fill, locrow, n_row_tiles)
    y = _experts(xs, te, tw, n_active, w1b, w3b, w2b)
    return _combine(y, pos, x1, mod3, final_g, seg, loc, goff, seq, final_norm)


def _pad_cols(a, width):
    return jnp.pad(a, ((0, 0), (0, width - a.shape[1])))


def kernel(x, c, w_ada, b_ada, norm1_g, w_in, b_in, gmlp_ln_g, gmlp_ln_b, gmlp_w_s, gmlp_b_s, gmlp_out_g, conv_w, conv_b, a_log_f, a_log_b, dt_bias_f, dt_bias_b, d_skip, ssd_norm_g, w_out, norm2_g, w_router_g, b_router_g, w_router_e, b_router_e, w1, w3, w2, final_g):
    bn, seq, _ = x.shape
    depth = w_ada.shape[0]
    x2 = x.reshape(bn * seq, D_MODEL)

    head_rows = jnp.arange(LANES)[:, None]
    head_cols = jnp.arange(SSD_WIDTH)[None, :] // SSD_HEAD_DIM
    expand = jnp.stack([head_rows == head_cols, head_rows == head_cols + SSD_HEADS]).astype(BF16)

    for l in range(depth):
        mod3 = _modulation(c, w_ada, b_ada[l], l).reshape(bn, 1, N_MOD * D_MODEL)

        w_in_p = _pad_cols(w_in[l], IN_COLS_PAD).astype(BF16)
        b_in_p = _pad_cols(b_in[l][None, :], IN_COLS_PAD)
        bs = jnp.repeat(gmlp_b_s[l].T, LANES, axis=1)
        bs_tile = jnp.tile(bs, (TM_PROJ // CHUNK, 1))
        ya, z, xbc, dt = _inproj_gmlp(
            x2, mod3, norm1_g[l][None, :], w_in_p, b_in_p, gmlp_ln_g[l][None, :], gmlp_ln_b[l][None, :],
            gmlp_w_s[l].astype(BF16), bs_tile, gmlp_out_g[l][None, :], seq)

        conv_w8 = jnp.pad(conv_w[l], ((0, SUBLANES - CONV_WIDTH), (0, 0)))
        dtb_row = _pad_cols(jnp.concatenate([dt_bias_f[l], dt_bias_b[l]])[None, :], LANES)
        alog_row = _pad_cols(jnp.concatenate([a_log_f[l], a_log_b[l]])[None, :], LANES)
        dsk_row = jnp.repeat(d_skip[l], SSD_HEAD_DIM)[None, :]
        x2 = _ssd_outproj(xbc, dt, z, ya, x2, mod3, conv_w8, conv_b[l][None, :], dtb_row, alog_row, dsk_row,
                          ssd_norm_g[l][None, :], w_out[l].astype(BF16), expand, bn, seq)

        w_re = jnp.transpose(w_router_e[l], (1, 0, 2)).reshape(D_MODEL, N_EXPERTS)
        w_router = _pad_cols(jnp.concatenate([w_re, w_router_g[l]], axis=1), LANES)
        w_router_hi = w_router.astype(BF16)
        w_router = jnp.stack([w_router_hi, (w_router - w_router_hi.astype(F32)).astype(BF16)])
        b_router = _pad_cols(jnp.concatenate([b_router_e[l].reshape(-1), b_router_g[l]])[None, :], LANES)
        x2 = _moe_final(x2, mod3, norm2_g[l][None, :], w_router, b_router, w1[l].astype(BF16),
                        w3[l].astype(BF16), w2[l].astype(BF16), final_g[None, :], seq,
                        final_norm=(l == depth - 1))
    return x2.reshape(bn, seq, D_MODEL)
```

```python
import functools
import math

import jax
import jax.numpy as jnp
from jax import lax
from jax.experimental import pallas as pl
from jax.experimental.pallas import tpu as pltpu

F32 = jnp.float32
BF16 = jnp.bfloat16
HIGHEST = lax.Precision.HIGHEST

D_MODEL = 1024
N_MOD = 6
GMLP_WIDTH = 1024
GMLP_HEADS = 8
CHUNK = 128
SSD_STEP_ROWS = 4 * CHUNK
SSD_WIDTH = 1024
SSD_HEADS = 16
SSD_HEAD_DIM = 64
SSD_GROUPS = 2
SSD_STATE = 128
GROUP_WIDTH = SSD_WIDTH // SSD_GROUPS
CONV_WIDTH = 5
CONV_CH = SSD_WIDTH + 2 * SSD_GROUPS * SSD_STATE
N_EGROUPS = 4
EXPERTS_PER_GROUP = 8
N_EXPERTS = 32
D_EXPERT = 256
EPS = 1e-6

LANES = 128
SUBLANES = 8
COL_U, COL_V, COL_Z, COL_XBC, COL_DT = 0, 1024, 2048, 3072, 4608
IN_COLS = 4640
IN_COLS_PAD = COL_DT + LANES
TM_PROJ = 1024
PROJ_ROWS = 512
COMBINE_ROWS = 256
ROUTER_ROWS = 256
TM_MOE = 512
TR_EXPERT = 256
EXPERT_TILES_PER_STEP = 2
N_ASSIGN = 2
SEG_ALIGN = 2 * SUBLANES
R_LOC = N_ASSIGN * TM_MOE + N_EXPERTS * SEG_ALIGN
PIECE_ROWS = 128
REM_SIZES = tuple(PIECE_ROWS >> s for s in range(1, PIECE_ROWS.bit_length()) if PIECE_ROWS >> s >= SEG_ALIGN)
TOTAL_SIZES = tuple(1 << s for s in range(R_LOC.bit_length() - 1, -1, -1) if 1 << s >= SEG_ALIGN)
VMEM_LIMIT = 56 * 1024 * 1024


def _silu(v):
    return v * jax.nn.sigmoid(v)


def _gelu(v):
    return 0.5 * v * (1.0 + lax.erf(v * math.sqrt(0.5)))


def _softplus(v):
    return jnp.maximum(v, 0.0) + jnp.log1p(jnp.exp(-jnp.abs(v)))


def _rms(v):
    return v * lax.rsqrt(jnp.mean(v * v, axis=-1, keepdims=True) + EPS)


def _mod_body(c_ref, w_ref, b_ref, o_ref):
    ca = _silu(c_ref[...])
    o_ref[...] = jnp.dot(ca, w_ref[...], precision=HIGHEST, preferred_element_type=F32) + b_ref[...]


def _modulation(c, w_ada, b_ada, layer):
    bn = c.shape[0]
    return pl.pallas_call(
        _mod_body,
        grid=(N_MOD,),
        in_specs=[
            pl.BlockSpec((bn, D_MODEL), lambda j: (0, 0)),
            pl.BlockSpec((None, D_MODEL, D_MODEL), lambda j: (layer, 0, j)),
            pl.BlockSpec((1, D_MODEL), lambda j: (0, j)),
        ],
        out_specs=pl.BlockSpec((bn, D_MODEL), lambda j: (0, j)),
        out_shape=jax.ShapeDtypeStruct((bn, N_MOD * D_MODEL), F32),
        name="adaln_mod",
    )(c, w_ada, b_ada.reshape(1, -1))


def _inproj_body(x_ref, shift_ref, scale_ref, g_ref, w_ref, b_ref, lng_ref, lnb_ref, ws_ref, bs_ref,
                 og_ref, ya_ref, z_ref, xbc_ref, dt_ref, mix_scr):
    for part in range(TM_PROJ // PROJ_ROWS):
        rows = slice(part * PROJ_ROWS, (part + 1) * PROJ_ROWS)
        h = _rms(x_ref[rows, :]) * g_ref[...]
        h = h * (1.0 + scale_ref[...]) + shift_ref[...]
        hb = h.astype(BF16)

        def proj(lo, hi, hb=hb):
            return jnp.dot(hb, w_ref[:, lo:hi], preferred_element_type=F32) + b_ref[:, lo:hi]

        z_ref[rows, :] = proj(COL_Z, COL_XBC).astype(BF16)
        xbc_ref[rows, :] = proj(COL_XBC, COL_DT)
        dt_ref[rows, :] = proj(COL_DT, IN_COLS_PAD)

        v = _gelu(proj(COL_V, COL_Z))
        mu = jnp.mean(v, axis=-1, keepdims=True)
        vc = v - mu
        var = jnp.mean(vc * vc, axis=-1, keepdims=True)
        vn = (vc * lax.rsqrt(var + EPS) * lng_ref[...] + lnb_ref[...]).astype(BF16)
        n_chunks = PROJ_ROWS // CHUNK
        for hd in range(GMLP_HEADS):
            cols = slice(hd * LANES, (hd + 1) * LANES)
            rhs = jnp.concatenate([vn[c * CHUNK:(c + 1) * CHUNK, cols] for c in range(n_chunks)], axis=1)
            res = jnp.dot(ws_ref[hd], rhs, preferred_element_type=F32)
            for c in range(n_chunks):
                mix_scr[part * PROJ_ROWS + c * CHUNK:part * PROJ_ROWS + (c + 1) * CHUNK, cols] = (
                    res[:, c * LANES:(c + 1) * LANES])
        u = _gelu(proj(COL_U, COL_V))
        out = u * (mix_scr[rows, :] + bs_ref[rows, :])
        ya_ref[rows, :] = (_rms(out) * og_ref[...]).astype(BF16)


def _inproj_gmlp(x2, mod3, norm1_g, w_in_p, b_in_p, ln_g, ln_b, w_s, bs_tile, out_g, seq):
    t = x2.shape[0]
    tiles_per_seq = seq // TM_PROJ
    row = lambda i: (i, 0)
    const2 = lambda i: (0, 0)
    return pl.pallas_call(
        _inproj_body,
        grid=(t // TM_PROJ,),
        in_specs=[
            pl.BlockSpec((TM_PROJ, D_MODEL), row),
            pl.BlockSpec((None, 1, D_MODEL), lambda i: (i // tiles_per_seq, 0, 0)),
            pl.BlockSpec((None, 1, D_MODEL), lambda i: (i // tiles_per_seq, 0, 1)),
            pl.BlockSpec((1, D_MODEL), const2),
            pl.BlockSpec((D_MODEL, IN_COLS_PAD), const2, pipeline_mode=pl.Buffered(1)),
            pl.BlockSpec((1, IN_COLS_PAD), const2),
            pl.BlockSpec((1, GMLP_WIDTH), const2),
            pl.BlockSpec((1, GMLP_WIDTH), const2),
            pl.BlockSpec((GMLP_HEADS, CHUNK, CHUNK), lambda i: (0, 0, 0)),
            pl.BlockSpec((TM_PROJ, GMLP_WIDTH), const2, pipeline_mode=pl.Buffered(1)),
            pl.BlockSpec((1, GMLP_WIDTH), const2),
        ],
        out_specs=[
            pl.BlockSpec((TM_PROJ, GMLP_WIDTH), row),
            pl.BlockSpec((TM_PROJ, SSD_WIDTH), row),
            pl.BlockSpec((TM_PROJ, CONV_CH), row),
            pl.BlockSpec((TM_PROJ, LANES), row),
        ],
        out_shape=[
            jax.ShapeDtypeStruct((t, GMLP_WIDTH), BF16),
            jax.ShapeDtypeStruct((t, SSD_WIDTH), BF16),
            jax.ShapeDtypeStruct((t, CONV_CH), F32),
            jax.ShapeDtypeStruct((t, LANES), F32),
        ],
        scratch_shapes=[pltpu.VMEM((TM_PROJ, GMLP_WIDTH), F32)],
        compiler_params=pltpu.CompilerParams(
            dimension_semantics=("arbitrary",), vmem_limit_bytes=VMEM_LIMIT),
        name="inproj_gmlp",
    )(x2, mod3, mod3, norm1_g, w_in_p, b_in_p, ln_g, ln_b, w_s, bs_tile, out_g)


def _ssd_chunk(act, dtv, a_row, expand, state, rev):
    off = SSD_HEADS if rev else 0
    row = lax.broadcasted_iota(jnp.int32, (CHUNK, CHUNK), 0)
    col = lax.broadcasted_iota(jnp.int32, (CHUNK, CHUNK), 1)
    lower = row >= col
    upper = row <= col
    keep = upper if rev else lower
    da = dtv * a_row
    cs = jnp.dot(keep.astype(F32), da, precision=HIGHEST, preferred_element_type=F32)
    cs_t = cs.T
    dt_t = dtv.T
    tot = cs[0:1, :] if rev else cs[CHUNK - 1:CHUNK, :]

    xs = act[:, :SSD_WIDTH]
    lane = lax.broadcasted_iota(jnp.int32, (CHUNK, LANES), 1)
    first_half = lane < SSD_HEAD_DIM
    zero = jnp.zeros((), BF16)

    stack = jnp.concatenate(
        [jnp.exp(cs), dtv * jnp.exp(tot - cs), jnp.broadcast_to(jnp.exp(tot), (SUBLANES, LANES))], axis=0)
    stack_x = jnp.dot(stack.astype(BF16), expand, preferred_element_type=F32)
    into_x = stack_x[:CHUNK]
    w_x = stack_x[CHUNK:2 * CHUNK]
    cd_x = stack_x[2 * CHUNK:2 * CHUNK + 1]
    xw = xs * w_x.astype(BF16)

    pieces = []
    for g in range(SSD_GROUPS):
        bg = act[:, SSD_WIDTH + g * SSD_STATE:SSD_WIDTH + (g + 1) * SSD_STATE]
        cg = act[:, SSD_WIDTH + SSD_GROUPS * SSD_STATE + g * SSD_STATE:
                 SSD_WIDTH + SSD_GROUPS * SSD_STATE + (g + 1) * SSD_STATE]
        scores = lax.dot_general(cg, bg, (((1,), (1,)), ((), ())), preferred_element_type=F32)
        heads_per_group = SSD_HEADS // SSD_GROUPS
        for pair in range(heads_per_group // 2):
            h0 = g * heads_per_group + 2 * pair
            xs_pair = xs[:, h0 * SSD_HEAD_DIM:(h0 + 2) * SSD_HEAD_DIM]
            y_pair = None
            for k in range(2):
                hh = off + h0 + k
                seg = cs[:, hh:hh + 1] - cs_t[hh:hh + 1, :]
                dec = jnp.exp(jnp.where(keep, seg, -jnp.inf))
                m = (scores * dec * dt_t[hh:hh + 1, :]).astype(BF16)
                rhs = jnp.where(first_half if k == 0 else jnp.logical_not(first_half), xs_pair, zero)
                part = jnp.dot(m, rhs, preferred_element_type=F32)
                y_pair = part if y_pair is None else y_pair + part
            pieces.append(y_pair)
    y_diag = jnp.concatenate(pieces, axis=1)

    y_off = []
    new_state = []
    for g in range(SSD_GROUPS):
        gcols = slice(g * GROUP_WIDTH, (g + 1) * GROUP_WIDTH)
        bg = act[:, SSD_WIDTH + g * SSD_STATE:SSD_WIDTH + (g + 1) * SSD_STATE]
        cg = act[:, SSD_WIDTH + SSD_GROUPS * SSD_STATE + g * SSD_STATE:
                 SSD_WIDTH + SSD_GROUPS * SSD_STATE + (g + 1) * SSD_STATE]
        prev = state[:, gcols]
        y_off.append(jnp.dot(cg, prev.astype(BF16), preferred_element_type=F32))
        bg_t = bg.astype(F32).T.astype(BF16)
        new = jnp.dot(bg_t, xw[:, gcols], preferred_element_type=F32)
        new_state.append(prev * cd_x[:, gcols] + new)
    y = y_diag + jnp.concatenate(y_off, axis=1) * into_x
    return y, jnp.concatenate(new_state, axis=1)


def _ssd_body(xbc_ref, xprev_ref, xnext_ref, dt_ref, z_ref, ya_ref, x_ref, gate_ref, cw_ref, cb_ref,
              dtb_ref, alog_ref, dsk_ref, ng_ref, wout_ref, exp_ref, o_ref,
              act_scr, dts_scr, yf_scr, st_scr, ext_scr, *, n_steps):
    d = pl.program_id(1)
    c = pl.program_id(2)
    lane1 = lax.broadcasted_iota(jnp.int32, (1, LANES), 1)
    a_row = jnp.where(lane1 < 2 * SSD_HEADS, -jnp.exp(alog_ref[...]), 0.0)
    chunk_rows = [slice(j * CHUNK, (j + 1) * CHUNK) for j in range(SSD_STEP_ROWS // CHUNK)]

    @pl.when(c == 0)
    def _():
        st_scr[...] = jnp.zeros_like(st_scr)

    def scan(act, dtv, expand, rev):
        state = st_scr[...]
        ys = [None] * len(chunk_rows)
        for j in (reversed(range(len(chunk_rows))) if rev else range(len(chunk_rows))):
            ys[j], state = _ssd_chunk(act[chunk_rows[j], :], dtv[chunk_rows[j], :], a_row, expand, state, rev)
        st_scr[...] = state
        return jnp.concatenate(ys, axis=0)

    @pl.when(d == 0)
    def _forward():
        row0 = pl.multiple_of(c * SSD_STEP_ROWS, SSD_STEP_ROWS)
        ext_scr[0:SUBLANES, :] = jnp.where(c > 0, xprev_ref[...], 0.0)
        ext_scr[SUBLANES:SUBLANES + SSD_STEP_ROWS, :] = xbc_ref[...]
        ext_scr[SUBLANES + SSD_STEP_ROWS:, :] = jnp.where(c < n_steps - 1, xnext_ref[...], 0.0)
        acc = cb_ref[...]
        for k in range(CONV_WIDTH):
            lo = SUBLANES - CONV_WIDTH // 2 + k
            acc = acc + cw_ref[k:k + 1, :] * ext_scr[lo:lo + SSD_STEP_ROWS, :]
        act = _silu(acc).astype(BF16)
        dtv = _softplus(dt_ref[...] + dtb_ref[...])
        act_scr[pl.ds(row0, SSD_STEP_ROWS), :] = act
        dts_scr[pl.ds(row0, SSD_STEP_ROWS), :] = dtv
        yf_scr[pl.ds(row0, SSD_STEP_ROWS), :] = scan(act, dtv, exp_ref[0], rev=False).astype(BF16)

    @pl.when(d == 1)
    def _backward():
        row0 = pl.multiple_of((n_steps - 1 - c) * SSD_STEP_ROWS, SSD_STEP_ROWS)
        act = act_scr[pl.ds(row0, SSD_STEP_ROWS), :]
        dtv = dts_scr[pl.ds(row0, SSD_STEP_ROWS), :]
        yb = scan(act, dtv, exp_ref[1], rev=True)
        xs = act[:, :SSD_WIDTH].astype(F32)
        y = yf_scr[pl.ds(row0, SSD_STEP_ROWS), :].astype(F32) + yb + dsk_ref[...] * xs
        y = y * _silu(z_ref[...].astype(F32))
        y = jnp.concatenate(
            [_rms(y[:, g * GROUP_WIDTH:(g + 1) * GROUP_WIDTH]) for g in range(SSD_GROUPS)], axis=1)
        y = y * ng_ref[...]
        mix = jnp.concatenate([ya_ref[...], y.astype(BF16)], axis=1)
        o = jnp.dot(mix, wout_ref[...], preferred_element_type=F32)
        o_ref[...] = x_ref[...] + gate_ref[...] * o


def _ssd_outproj(xbc, dt, z, ya, x2, mod3, conv_w8, conv_b, dtb_row, alog_row, dsk_row, norm_g, w_out_b,
                 expand, bn, seq):
    t = x2.shape[0]
    nc = seq // SSD_STEP_ROWS
    blocks8 = SSD_STEP_ROWS // SUBLANES
    last8 = t // SUBLANES - 1

    def fwd_chunk(b, d, c):
        return b * nc + c + d * (nc - 1 - c)

    def bwd_chunk(b, d, c):
        return b * nc + nc - 1 - d * c

    const2 = lambda b, d, c: (0, 0)
    return pl.pallas_call(
        functools.partial(_ssd_body, n_steps=nc),
        grid=(bn, 2, nc),
        in_specs=[
            pl.BlockSpec((SSD_STEP_ROWS, CONV_CH), lambda b, d, c: (fwd_chunk(b, d, c), 0)),
            pl.BlockSpec((SUBLANES, CONV_CH),
                         lambda b, d, c: (jnp.maximum(fwd_chunk(b, d, c) * blocks8 - 1, 0), 0)),
            pl.BlockSpec((SUBLANES, CONV_CH),
                         lambda b, d, c: (jnp.minimum((fwd_chunk(b, d, c) + 1) * blocks8, last8), 0)),
            pl.BlockSpec((SSD_STEP_ROWS, LANES), lambda b, d, c: (fwd_chunk(b, d, c), 0)),
            pl.BlockSpec((SSD_STEP_ROWS, SSD_WIDTH), lambda b, d, c: (bwd_chunk(b, d, c), 0)),
            pl.BlockSpec((SSD_STEP_ROWS, GMLP_WIDTH), lambda b, d, c: (bwd_chunk(b, d, c), 0)),
            pl.BlockSpec((SSD_STEP_ROWS, D_MODEL), lambda b, d, c: (bwd_chunk(b, d, c), 0)),
            pl.BlockSpec((None, 1, D_MODEL), lambda b, d, c: (b, 0, 2)),
            pl.BlockSpec((SUBLANES, CONV_CH), const2),
            pl.BlockSpec((1, CONV_CH), const2),
            pl.BlockSpec((1, LANES), const2),
            pl.BlockSpec((1, LANES), const2),
            pl.BlockSpec((1, SSD_WIDTH), const2),
            pl.BlockSpec((1, SSD_WIDTH), const2),
            pl.BlockSpec((GMLP_WIDTH + SSD_WIDTH, D_MODEL), const2, pipeline_mode=pl.Buffered(1)),
            pl.BlockSpec((2, LANES, SSD_WIDTH), lambda b, d, c: (0, 0, 0), pipeline_mode=pl.Buffered(1)),
        ],
        out_specs=pl.BlockSpec((SSD_STEP_ROWS, D_MODEL), lambda b, d, c: (bwd_chunk(b, d, c), 0)),
        out_shape=jax.ShapeDtypeStruct((t, D_MODEL), F32),
        scratch_shapes=[
            pltpu.VMEM((seq, CONV_CH), BF16),
            pltpu.VMEM((seq, LANES), F32),
            pltpu.VMEM((seq, SSD_WIDTH), BF16),
            pltpu.VMEM((SSD_STATE, SSD_WIDTH), F32),
            pltpu.VMEM((SSD_STEP_ROWS + 2 * SUBLANES, CONV_CH), F32),
        ],
        compiler_params=pltpu.CompilerParams(
            dimension_semantics=("arbitrary", "arbitrary", "arbitrary"), vmem_limit_bytes=VMEM_LIMIT),
        name="ssd_outproj",
    )(xbc, xbc, xbc, dt, z, ya, x2, mod3, conv_w8, conv_b, dtb_row, alog_row, dsk_row, norm_g, w_out_b,
      expand)


def _route(logits):
    lane = lax.broadcasted_iota(jnp.int32, logits.shape, 1)
    big = jnp.int32(LANES)
    neg = -jnp.inf
    gmask = (lane >= N_EXPERTS) & (lane < N_EXPERTS + N_EGROUPS)
    gl = jnp.where(gmask, logits, neg)
    gmax = jnp.max(gl, axis=-1, keepdims=True)
    gidx = jnp.min(jnp.where(gl == gmax, lane, big), axis=-1, keepdims=True) - N_EXPERTS
    p_g = 1.0 / jnp.sum(jnp.where(gmask, jnp.exp(gl - gmax), 0.0), axis=-1, keepdims=True)
    lo = gidx * EXPERTS_PER_GROUP
    emask = (lane >= lo) & (lane < lo + EXPERTS_PER_GROUP)
    el = jnp.where(emask, logits, neg)
    v1 = jnp.max(el, axis=-1, keepdims=True)
    i1 = jnp.min(jnp.where(el == v1, lane, big), axis=-1, keepdims=True)
    el2 = jnp.where(lane == i1, neg, el)
    v2 = jnp.max(el2, axis=-1, keepdims=True)
    i2 = jnp.min(jnp.where(el2 == v2, lane, big), axis=-1, keepdims=True)
    e2 = jnp.exp(v2 - v1)
    den = 1.0 + e2
    w1 = p_g / den
    w2 = p_g * e2 / den
    return i1, i2, w1, w2


def _router_body(x_ref, shift_ref, scale_ref, g2_ref, wr_ref, br_ref, h_ref, rt_ref, cnt_ref):
    counts = jnp.zeros((1, LANES), F32)
    for part in range(TM_MOE // ROUTER_ROWS):
        rows = slice(part * ROUTER_ROWS, (part + 1) * ROUTER_ROWS)
        h = _rms(x_ref[rows, :]) * g2_ref[...]
        h = h * (1.0 + scale_ref[...]) + shift_ref[...]
        hb = h.astype(BF16)
        h_ref[rows, :] = hb
        h_lo = (h - hb.astype(F32)).astype(BF16)
        both = jnp.dot(hb, wr_ref[...], preferred_element_type=F32)
        logits = (both[:, :LANES] + both[:, LANES:]
                  + jnp.dot(h_lo, wr_ref[:, :LANES], preferred_element_type=F32)) + br_ref[...]
        i1, i2, w1, w2 = _route(logits)
        lane = lax.broadcasted_iota(jnp.int32, (ROUTER_ROWS, LANES), 1)
        rt_ref[rows, :] = jnp.where(lane == 0, i1.astype(F32), jnp.where(lane == 1, i2.astype(F32),
                                    jnp.where(lane == 2, w1, jnp.where(lane == 3, w2, 0.0))))
        mask = jnp.where(lane == i1, 1.0, jnp.where(lane == i2, 1.0, 0.0))
        counts = counts + jnp.sum(mask, axis=0, keepdims=True)
    cnt_ref[...] = jnp.broadcast_to(counts, (SUBLANES, LANES))


def _router(x1, mod3, norm2_g, w_router, b_router, seq):
    t = x1.shape[0]
    tiles_per_seq = seq // TM_MOE
    n_tiles = t // TM_MOE
    row = lambda i: (i, 0)
    const2 = lambda i: (0, 0)
    return pl.pallas_call(
        _router_body,
        grid=(n_tiles,),
        in_specs=[
            pl.BlockSpec((TM_MOE, D_MODEL), row),
            pl.BlockSpec((None, 1, D_MODEL), lambda i: (i // tiles_per_seq, 0, 3)),
            pl.BlockSpec((None, 1, D_MODEL), lambda i: (i // tiles_per_seq, 0, 4)),
            pl.BlockSpec((1, D_MODEL), const2),
            pl.BlockSpec((D_MODEL, 2 * LANES), const2),
            pl.BlockSpec((1, LANES), const2),
        ],
        out_specs=[
            pl.BlockSpec((TM_MOE, D_MODEL), row),
            pl.BlockSpec((TM_MOE, LANES), row),
            pl.BlockSpec((SUBLANES, LANES), row),
        ],
        out_shape=[
            jax.ShapeDtypeStruct((t, D_MODEL), BF16),
            jax.ShapeDtypeStruct((t, LANES), F32),
            jax.ShapeDtypeStruct((n_tiles * SUBLANES, LANES), F32),
        ],
        compiler_params=pltpu.CompilerParams(
            dimension_semantics=("arbitrary",), vmem_limit_bytes=VMEM_LIMIT),
        name="moe_router",
    )(x1, mod3, mod3, norm2_g, w_router, b_router)


def _row_copy(vmem_buf, v_start, hbm_ref, g_start, size, sem, to_hbm):
    v = vmem_buf.at[pl.ds(v_start, size), :]
    g = hbm_ref.at[pl.ds(g_start, size), :]
    return pltpu.make_async_copy(v, g, sem) if to_hbm else pltpu.make_async_copy(g, v, sem)


def _copy_rows(n, vmem_buf, v0, hbm_ref, g0, sem, *, to_hbm, wait):
    def run(v_off, g_off, size):
        v_start = 0 if v0 is None else pl.multiple_of(v0 + v_off, SEG_ALIGN)
        cp = _row_copy(vmem_buf, v_start, hbm_ref, pl.multiple_of(g0 + g_off, SEG_ALIGN), size, sem, to_hbm)
        if wait:
            cp.wait()
        else:
            cp.start()

    n_big = n // PIECE_ROWS

    def big_piece(k, carry):
        run(k * PIECE_ROWS, k * PIECE_ROWS, PIECE_ROWS)
        return carry

    lax.fori_loop(0, n_big, big_piece, 0)
    base = n_big * PIECE_ROWS
    rem = n - base
    for size in REM_SIZES:
        done = base + (rem // (2 * size)) * (2 * size)

        @pl.when((rem & size) != 0)
        def _():
            run(done, done, size)


def _segment_copies(seg_ref, loc_ref, goff_ref, tile, vmem_buf, hbm_ref, sem, *, to_hbm):
    base = tile * N_EXPERTS

    def per_expert(e, carry):
        _copy_rows(seg_ref[base + e], vmem_buf, loc_ref[base + e], hbm_ref, goff_ref[base + e], sem,
                   to_hbm=to_hbm, wait=False)
        return carry

    lax.fori_loop(0, N_EXPERTS, per_expert, 0)


def _tile_rows(seg_ref, loc_ref, tile):
    last = tile * N_EXPERTS + N_EXPERTS - 1
    return loc_ref[last] + seg_ref[last]


def _segment_wait(seg_ref, loc_ref, tile, vmem_buf, hbm_ref, sem, *, to_hbm):
    total = _tile_rows(seg_ref, loc_ref, tile)
    for size in TOTAL_SIZES:
        @pl.when((total & size) != 0)
        def _():
            _row_copy(vmem_buf, 0, hbm_ref, 0, size, sem, to_hbm).wait()


def _dispatch_body(seg_ref, loc_ref, goff_ref, fill_ref, h_ref, rt_ref, locrow_ref, pos_ref, xs_hbm, buf, sem,
                   *, n_tiles, n_row_tiles):
    i = pl.program_id(0)
    slot = i % 2
    start = functools.partial(_segment_copies, seg_ref, loc_ref, goff_ref, hbm_ref=xs_hbm, to_hbm=True)
    wait = functools.partial(_segment_wait, seg_ref, loc_ref, hbm_ref=xs_hbm, to_hbm=True)

    @pl.when(i >= 2)
    def _():
        wait(i - 2, buf.at[slot], sem=sem.at[slot])

    rt = rt_ref[...]
    lane = lax.broadcasted_iota(jnp.int32, (TM_MOE, LANES), 1)
    lanef = lane.astype(F32)
    e1 = rt[:, 0:1]
    e2 = rt[:, 1:2]
    mask = jnp.where(lanef == e1, 1.0, jnp.where(lanef == e2, 1.0, 0.0))
    r = lax.broadcasted_iota(jnp.int32, (TM_MOE, TM_MOE), 0)
    cc = lax.broadcasted_iota(jnp.int32, (TM_MOE, TM_MOE), 1)
    strict = jnp.where(r > cc, 1.0, 0.0).astype(BF16)
    rank = jnp.dot(strict, mask.astype(BF16), preferred_element_type=F32)
    posall = rank + locrow_ref[...]
    pos1 = jnp.sum(jnp.where(lanef == e1, posall, 0.0), axis=-1, keepdims=True)
    pos2 = jnp.sum(jnp.where(lanef == e2, posall, 0.0), axis=-1, keepdims=True)
    pp = jnp.where(lane == 0, pos1, jnp.where(lane == 1, pos2, rt))
    pos_ref[...] = pp

    slots_t = [pp[k * LANES:(k + 1) * LANES, :].T for k in range(TM_MOE // LANES)]

    rowp = lax.broadcasted_iota(jnp.int32, (R_LOC, LANES), 0).astype(F32)
    perm = jnp.concatenate(
        [jnp.where(rowp == st[0:1, :], 1.0, jnp.where(rowp == st[1:2, :], 1.0, 0.0)).astype(BF16)
         for st in slots_t], axis=1)
    buf[slot] = jnp.dot(perm, h_ref[...], preferred_element_type=F32).astype(BF16)
    start(i, buf.at[slot], sem=sem.at[slot])

    @pl.when(i == n_tiles - 1)
    def _():
        if n_tiles > 1:
            wait(i - 1, buf.at[1 - slot], sem=sem.at[1 - slot])
        wait(i, buf.at[slot], sem=sem.at[slot])
        zsrc = buf.at[1 - slot]
        zsem = sem.at[1 - slot]
        zsrc[0:TR_EXPERT, :] = jnp.zeros((TR_EXPERT, D_MODEL), BF16)
        for waiting in (False, True):
            def per_expert(e, carry):
                _copy_rows(fill_ref[N_EXPERTS + e], zsrc, None, xs_hbm, fill_ref[e], zsem,
                           to_hbm=True, wait=waiting)
                return carry

            def per_row_tile(k, carry):
                cp = _row_copy(zsrc, 0, xs_hbm, pl.multiple_of(k * TR_EXPERT, TR_EXPERT), TR_EXPERT, zsem, True)
                if waiting:
                    cp.wait()
                else:
                    cp.start()
                return carry

            lax.fori_loop(0, N_EXPERTS, per_expert, 0)
            lax.fori_loop(fill_ref[2 * N_EXPERTS], n_row_tiles, per_row_tile, 0)


def _dispatch(h2, rt, seg, loc, goff, fill, locrow, n_row_tiles):
    t = h2.shape[0]
    n_tiles = t // TM_MOE
    row = lambda i, *_: (i, 0)
    grid_spec = pltpu.PrefetchScalarGridSpec(
        num_scalar_prefetch=4,
        grid=(n_tiles,),
        in_specs=[
            pl.BlockSpec((TM_MOE, D_MODEL), row),
            pl.BlockSpec((TM_MOE, LANES), row),
            pl.BlockSpec((None, 1, LANES), lambda i, *_: (i, 0, 0)),
        ],
        out_specs=[
            pl.BlockSpec((TM_MOE, LANES), row),
            pl.BlockSpec(memory_space=pl.ANY),
        ],
        scratch_shapes=[
            pltpu.VMEM((2, R_LOC, D_MODEL), BF16),
            pltpu.SemaphoreType.DMA((2,)),
        ],
    )
    return pl.pallas_call(
        functools.partial(_dispatch_body, n_tiles=n_tiles, n_row_tiles=n_row_tiles),
        grid_spec=grid_spec,
        out_shape=[
            jax.ShapeDtypeStruct((t, LANES), F32),
            jax.ShapeDtypeStruct((n_row_tiles * TR_EXPERT, D_MODEL), BF16),
        ],
        compiler_params=pltpu.CompilerParams(
            dimension_semantics=("arbitrary",), vmem_limit_bytes=VMEM_LIMIT),
        name="moe_dispatch",
    )(seg, loc, goff, fill, h2, rt, locrow)


def _expert_body(te_ref, tw_ref, na_ref, xs_ref, w1a_ref, w3a_ref, w2a_ref, w1b_ref, w3b_ref, w2b_ref, y_ref):
    del tw_ref
    r = pl.program_id(0)
    n_active = na_ref[0]
    weights = ((w1a_ref, w3a_ref, w2a_ref), (w1b_ref, w3b_ref, w2b_ref))

    def swiglu(half, rows=None):
        w1_ref, w3_ref, w2_ref = weights[half]
        rows = slice(half * TR_EXPERT, (half + 1) * TR_EXPERT) if rows is None else rows
        x = xs_ref[rows, :]
        a = _silu(jnp.dot(x, w1_ref[...], preferred_element_type=F32))
        a = a * jnp.dot(x, w3_ref[...], preferred_element_type=F32)
        y_ref[rows, :] = jnp.dot(a.astype(BF16), w2_ref[...], preferred_element_type=F32).astype(BF16)

    first = EXPERT_TILES_PER_STEP * r
    both = first + 1 < n_active
    same = te_ref[first] == te_ref[first + 1]

    @pl.when(both & same)
    def _():
        swiglu(0, slice(0, EXPERT_TILES_PER_STEP * TR_EXPERT))

    @pl.when(both & jnp.logical_not(same))
    def _():
        swiglu(0)
        swiglu(1)

    @pl.when(first + 1 == n_active)
    def _():
        swiglu(0)
        y_ref[TR_EXPERT:, :] = jnp.zeros((TR_EXPERT, D_MODEL), BF16)

    @pl.when(first >= n_active)
    def _():
        y_ref[...] = jnp.zeros_like(y_ref)


def _experts(xs, tile_expert, tile_weights, n_active, w1b, w3b, w2b):
    n_rows = xs.shape[0]
    step_rows = EXPERT_TILES_PER_STEP * TR_EXPERT
    x_block = lambda r, te, tw, na: (jnp.minimum(r, (na[0] - 1) // EXPERT_TILES_PER_STEP), 0)
    w_specs = []
    for half in range(EXPERT_TILES_PER_STEP):
        w_block = lambda r, te, tw, na, half=half: (tw[EXPERT_TILES_PER_STEP * r + half], 0, 0)
        w_specs += [pl.BlockSpec((None, D_MODEL, D_EXPERT), w_block),
                    pl.BlockSpec((None, D_MODEL, D_EXPERT), w_block),
                    pl.BlockSpec((None, D_EXPERT, D_MODEL), w_block)]
    grid_spec = pltpu.PrefetchScalarGridSpec(
        num_scalar_prefetch=3,
        grid=(n_rows // step_rows,),
        in_specs=[pl.BlockSpec((step_rows, D_MODEL), x_block)] + w_specs,
        out_specs=pl.BlockSpec((step_rows, D_MODEL), lambda r, te, tw, na: (r, 0)),
    )
    return pl.pallas_call(
        _expert_body,
        grid_spec=grid_spec,
        out_shape=jax.ShapeDtypeStruct((n_rows, D_MODEL), BF16),
        compiler_params=pltpu.CompilerParams(
            dimension_semantics=("arbitrary",), vmem_limit_bytes=VMEM_LIMIT),
        name="moe_experts",
    )(tile_expert, tile_weights, n_active, xs, w1b, w3b, w2b, w1b, w3b, w2b)


def _combine_body(seg_ref, loc_ref, goff_ref, y_hbm, pos_ref, x_ref, gate_ref, fg_ref, o_ref, buf, sem, *,
                  n_tiles, final_norm):
    i = pl.program_id(0)
    slot = i % 2
    start = functools.partial(_segment_copies, seg_ref, loc_ref, goff_ref, hbm_ref=y_hbm, to_hbm=False)

    @pl.when(i == 0)
    def _():
        buf[...] = jnp.zeros_like(buf)
        start(0, buf.at[0], sem=sem.at[0])

    @pl.when(i + 1 < n_tiles)
    def _():
        start(i + 1, buf.at[1 - slot], sem=sem.at[1 - slot])

    _segment_wait(seg_ref, loc_ref, i, buf.at[slot], y_hbm, sem.at[slot], to_hbm=False)
    for part in range(TM_MOE // COMBINE_ROWS):
        rows = slice(part * COMBINE_ROWS, (part + 1) * COMBINE_ROWS)
        pp = pos_ref[rows, :]
        colp = lax.broadcasted_iota(jnp.int32, (COMBINE_ROWS, R_LOC), 1).astype(F32)
        wc = jnp.where(colp == pp[:, 0:1], pp[:, 2:3], jnp.where(colp == pp[:, 1:2], pp[:, 3:4], 0.0))
        moe = jnp.dot(wc.astype(BF16), buf[slot], preferred_element_type=F32)
        x2 = x_ref[rows, :] + gate_ref[...] * moe
        o_ref[rows, :] = _rms(x2) * fg_ref[...] if final_norm else x2


def _combine(y, pos, x1, mod3, final_g, seg, loc, goff, seq, final_norm):
    t = x1.shape[0]
    n_tiles = t // TM_MOE
    tiles_per_seq = seq // TM_MOE
    row = lambda i, *_: (i, 0)
    grid_spec = pltpu.PrefetchScalarGridSpec(
        num_scalar_prefetch=3,
        grid=(n_tiles,),
        in_specs=[
            pl.BlockSpec(memory_space=pl.ANY),
            pl.BlockSpec((TM_MOE, LANES), row),
            pl.BlockSpec((TM_MOE, D_MODEL), row),
            pl.BlockSpec((None, 1, D_MODEL), lambda i, *_: (i // tiles_per_seq, 0, 5)),
            pl.BlockSpec((1, D_MODEL), lambda i, *_: (0, 0)),
        ],
        out_specs=pl.BlockSpec((TM_MOE, D_MODEL), row),
        scratch_shapes=[
            pltpu.VMEM((2, R_LOC, D_MODEL), BF16),
            pltpu.SemaphoreType.DMA((2,)),
        ],
    )
    return pl.pallas_call(
        functools.partial(_combine_body, n_tiles=n_tiles, final_norm=final_norm),
        grid_spec=grid_spec,
        out_shape=jax.ShapeDtypeStruct((t, D_MODEL), F32),
        compiler_params=pltpu.CompilerParams(
            dimension_semantics=("arbitrary",), vmem_limit_bytes=VMEM_LIMIT),
        name="moe_combine",
    )(seg, loc, goff, y, pos, x1, mod3, final_g)


def _moe_plan(counts, n_tiles, n_row_tiles):
    cnt = counts.reshape(n_tiles, SUBLANES, LANES)[:, 0, :N_EXPERTS].astype(jnp.int32)
    seg = (cnt + SEG_ALIGN - 1) // SEG_ALIGN * SEG_ALIGN
    before_e = jnp.arange(N_EXPERTS)[:, None] < jnp.arange(N_EXPERTS)[None, :]
    before_t = jnp.arange(n_tiles)[:, None] > jnp.arange(n_tiles)[None, :]
    loc = jnp.sum(jnp.where(before_e[None], seg[:, :, None], 0), axis=1)
    tot = jnp.sum(seg, axis=0)
    region = (tot + TR_EXPERT - 1) // TR_EXPERT * TR_EXPERT
    gstart = jnp.sum(jnp.where(before_e, region[:, None], 0), axis=0)
    gend = gstart + region
    goff = gstart[None, :] + jnp.sum(jnp.where(before_t[:, :, None], seg[None], 0), axis=1)
    n_active = gend[-1] // TR_EXPERT
    tile_row0 = jnp.arange(n_row_tiles, dtype=jnp.int32) * TR_EXPERT
    last_row0 = (n_active - 1) * TR_EXPERT
    te = jnp.sum(gend[None, :] <= jnp.minimum(tile_row0, last_row0)[:, None], axis=1).astype(jnp.int32)
    te = jnp.minimum(te, N_EXPERTS - 1)
    te2 = te.reshape(-1, EXPERT_TILES_PER_STEP)
    cand = jnp.where(te2[:, 1] != te2[:, 0], te2[:, 1], 0)
    steps = jnp.arange(te2.shape[0])
    held = jnp.max(jnp.where(steps[None, :] <= steps[:, None], cand[None, :], 0), axis=1)
    tw = jnp.stack([te2[:, 0], held], axis=1).reshape(-1).astype(jnp.int32)
    locrow = _pad_cols(loc.astype(F32), LANES).reshape(n_tiles, 1, LANES)
    n_active = n_active.reshape(1).astype(jnp.int32)
    fill = jnp.concatenate([gstart + tot, region - tot, n_active]).astype(jnp.int32)
    return (seg.reshape(-1), loc.reshape(-1).astype(jnp.int32), goff.reshape(-1).astype(jnp.int32), fill,
            locrow, te, tw, n_active)


def _moe_final(x1, mod3, norm2_g, w_router, b_router, w1b, w3b, w2b, final_g, seq, final_norm):
    t = x1.shape[0]
    n_tiles = t // TM_MOE
    max_rows = N_ASSIGN * t + (SEG_ALIGN - 1) * N_EXPERTS * n_tiles + N_EXPERTS * (TR_EXPERT - SEG_ALIGN)
    step_rows = EXPERT_TILES_PER_STEP * TR_EXPERT
    n_row_tiles = -(-max_rows // step_rows) * EXPERT_TILES_PER_STEP
    h2, rt, counts = _router(x1, mod3, norm2_g, w_router, b_router, seq)
    seg, loc, goff, fill, locrow, te, tw, n_active = _moe_plan(counts, n_tiles, n_row_tiles)
    pos, xs = _dispatch(h2, rt, seg, loc, goff, fill, locrow, n_row_tiles)
    y = _experts(xs, te, tw, n_active, w1b, w3b, w2b)
    return _combine(y, pos, x1, mod3, final_g, seg, loc, goff, seq, final_norm)


def _pad_cols(a, width):
    return jnp.pad(a, ((0, 0), (0, width - a.shape[1])))


def kernel(x, c, w_ada, b_ada, norm1_g, w_in, b_in, gmlp_ln_g, gmlp_ln_b, gmlp_w_s, gmlp_b_s, gmlp_out_g, conv_w, conv_b, a_log_f, a_log_b, dt_bias_f, dt_bias_b, d_skip, ssd_norm_g, w_out, norm2_g, w_router_g, b_router_g, w_router_e, b_router_e, w1, w3, w2, final_g):
    bn, seq, _ = x.shape
    depth = w_ada.shape[0]
    x2 = x.reshape(bn * seq, D_MODEL)

    head_rows = jnp.arange(LANES)[:, None]
    head_cols = jnp.arange(SSD_WIDTH)[None, :] // SSD_HEAD_DIM
    expand = jnp.stack([head_rows == head_cols, head_rows == head_cols + SSD_HEADS]).astype(BF16)

    for l in range(depth):
        mod3 = _modulation(c, w_ada, b_ada[l], l).reshape(bn, 1, N_MOD * D_MODEL)

        w_in_p = _pad_cols(w_in[l], IN_COLS_PAD).astype(BF16)
        b_in_p = _pad_cols(b_in[l][None, :], IN_COLS_PAD)
        bs = jnp.repeat(gmlp_b_s[l].T, LANES, axis=1)
        bs_tile = jnp.tile(bs, (TM_PROJ // CHUNK, 1))
        ya, z, xbc, dt = _inproj_gmlp(
            x2, mod3, norm1_g[l][None, :], w_in_p, b_in_p, gmlp_ln_g[l][None, :], gmlp_ln_b[l][None, :],
            gmlp_w_s[l].astype(BF16), bs_tile, gmlp_out_g[l][None, :], seq)

        conv_w8 = jnp.pad(conv_w[l], ((0, SUBLANES - CONV_WIDTH), (0, 0)))
        dtb_row = _pad_cols(jnp.concatenate([dt_bias_f[l], dt_bias_b[l]])[None, :], LANES)
        alog_row = _pad_cols(jnp.concatenate([a_log_f[l], a_log_b[l]])[None, :], LANES)
        dsk_row = jnp.repeat(d_skip[l], SSD_HEAD_DIM)[None, :]
        x2 = _ssd_outproj(xbc, dt, z, ya, x2, mod3, conv_w8, conv_b[l][None, :], dtb_row, alog_row, dsk_row,
                          ssd_norm_g[l][None, :], w_out[l].astype(BF16), expand, bn, seq)

        w_re = jnp.transpose(w_router_e[l], (1, 0, 2)).reshape(D_MODEL, N_EXPERTS)
        w_router = _pad_cols(jnp.concatenate([w_re, w_router_g[l]], axis=1), LANES)
        w_router_hi = w_router.astype(BF16)
        w_router = jnp.concatenate([w_router_hi, (w_router - w_router_hi.astype(F32)).astype(BF16)], axis=1)
        b_router = _pad_cols(jnp.concatenate([b_router_e[l].reshape(-1), b_router_g[l]])[None, :], LANES)
        x2 = _moe_final(x2, mod3, norm2_g[l][None, :], w_router, b_router, w1[l].astype(BF16),
                        w3[l].astype(BF16), w2[l].astype(BF16), final_g[None, :], seq,
                        final_norm=(l == depth - 1))
    return x2.reshape(bn, seq, D_MODEL)
```

```python
import functools
import math

import jax
import jax.numpy as jnp
from jax import lax
from jax.experimental import pallas as pl
from jax.experimental.pallas import tpu as pltpu

F32 = jnp.float32
BF16 = jnp.bfloat16
HIGHEST = lax.Precision.HIGHEST

D_MODEL = 1024
N_MOD = 6
GMLP_WIDTH = 1024
GMLP_HEADS = 8
CHUNK = 128
SSD_STEP_ROWS = 4 * CHUNK
SSD_WIDTH = 1024
SSD_HEADS = 16
SSD_HEAD_DIM = 64
SSD_GROUPS = 2
SSD_STATE = 128
GROUP_WIDTH = SSD_WIDTH // SSD_GROUPS
CONV_WIDTH = 5
CONV_CH = SSD_WIDTH + 2 * SSD_GROUPS * SSD_STATE
N_EGROUPS = 4
EXPERTS_PER_GROUP = 8
N_EXPERTS = 32
D_EXPERT = 256
EPS = 1e-6

LANES = 128
SUBLANES = 8
COL_U, COL_V, COL_Z, COL_XBC, COL_DT = 0, 1024, 2048, 3072, 4608
IN_COLS = 4640
IN_COLS_PAD = COL_DT + LANES
TM_PROJ = 1024
PROJ_ROWS = 512
COMBINE_ROWS = 256
ROUTER_ROWS = 256
TM_MOE = 512
TR_EXPERT = 256
EXPERT_TILES_PER_STEP = 2
N_ASSIGN = 2
SEG_ALIGN = 2 * SUBLANES
R_LOC = N_ASSIGN * TM_MOE + N_EXPERTS * SEG_ALIGN
SEG_PIECE = 64
WAIT_ROWS = 1024
PIECE_ROWS = 128
REM_SIZES = tuple(PIECE_ROWS >> s for s in range(1, PIECE_ROWS.bit_length()) if PIECE_ROWS >> s >= SEG_ALIGN)
TOTAL_SIZES = tuple(1 << s for s in range(R_LOC.bit_length() - 1, -1, -1) if 1 << s >= SEG_ALIGN)
VMEM_LIMIT = 56 * 1024 * 1024


def _silu(v):
    return v * jax.nn.sigmoid(v)


def _gelu(v):
    return 0.5 * v * (1.0 + lax.erf(v * math.sqrt(0.5)))


def _softplus(v):
    return jnp.maximum(v, 0.0) + jnp.log1p(jnp.exp(-jnp.abs(v)))


def _rms(v):
    return v * lax.rsqrt(jnp.mean(v * v, axis=-1, keepdims=True) + EPS)


def _mod_body(c_ref, w_ref, b_ref, o_ref):
    ca = _silu(c_ref[...])
    o_ref[...] = jnp.dot(ca, w_ref[...], precision=HIGHEST, preferred_element_type=F32) + b_ref[...]


def _modulation(c, w_ada, b_ada, layer):
    bn = c.shape[0]
    return pl.pallas_call(
        _mod_body,
        grid=(N_MOD,),
        in_specs=[
            pl.BlockSpec((bn, D_MODEL), lambda j: (0, 0)),
            pl.BlockSpec((None, D_MODEL, D_MODEL), lambda j: (layer, 0, j)),
            pl.BlockSpec((1, D_MODEL), lambda j: (0, j)),
        ],
        out_specs=pl.BlockSpec((bn, D_MODEL), lambda j: (0, j)),
        out_shape=jax.ShapeDtypeStruct((bn, N_MOD * D_MODEL), F32),
        name="adaln_mod",
    )(c, w_ada, b_ada.reshape(1, -1))


def _inproj_body(x_ref, shift_ref, scale_ref, g_ref, w_ref, b_ref, lng_ref, lnb_ref, ws_ref, bs_ref,
                 og_ref, ya_ref, z_ref, xbc_ref, dt_ref, mix_scr):
    for part in range(TM_PROJ // PROJ_ROWS):
        rows = slice(part * PROJ_ROWS, (part + 1) * PROJ_ROWS)
        h = _rms(x_ref[rows, :]) * g_ref[...]
        h = h * (1.0 + scale_ref[...]) + shift_ref[...]
        hb = h.astype(BF16)

        def proj(lo, hi, hb=hb):
            return jnp.dot(hb, w_ref[:, lo:hi], preferred_element_type=F32) + b_ref[:, lo:hi]

        z_ref[rows, :] = proj(COL_Z, COL_XBC).astype(BF16)
        xbc_ref[rows, :] = proj(COL_XBC, COL_DT)
        dt_ref[rows, :] = proj(COL_DT, IN_COLS_PAD)

        v = _gelu(proj(COL_V, COL_Z))
        mu = jnp.mean(v, axis=-1, keepdims=True)
        vc = v - mu
        var = jnp.mean(vc * vc, axis=-1, keepdims=True)
        vn = (vc * lax.rsqrt(var + EPS) * lng_ref[...] + lnb_ref[...]).astype(BF16)
        n_chunks = PROJ_ROWS // CHUNK
        for hd in range(GMLP_HEADS):
            cols = slice(hd * LANES, (hd + 1) * LANES)
            rhs = jnp.concatenate([vn[c * CHUNK:(c + 1) * CHUNK, cols] for c in range(n_chunks)], axis=1)
            res = jnp.dot(ws_ref[hd], rhs, preferred_element_type=F32)
            for c in range(n_chunks):
                mix_scr[part * PROJ_ROWS + c * CHUNK:part * PROJ_ROWS + (c + 1) * CHUNK, cols] = (
                    res[:, c * LANES:(c + 1) * LANES])
        u = _gelu(proj(COL_U, COL_V))
        out = u * (mix_scr[rows, :] + bs_ref[rows, :])
        ya_ref[rows, :] = (_rms(out) * og_ref[...]).astype(BF16)


def _inproj_gmlp(x2, mod3, norm1_g, w_in_p, b_in_p, ln_g, ln_b, w_s, bs_tile, out_g, seq):
    t = x2.shape[0]
    tiles_per_seq = seq // TM_PROJ
    row = lambda i: (i, 0)
    const2 = lambda i: (0, 0)
    return pl.pallas_call(
        _inproj_body,
        grid=(t // TM_PROJ,),
        in_specs=[
            pl.BlockSpec((TM_PROJ, D_MODEL), row),
            pl.BlockSpec((None, 1, D_MODEL), lambda i: (i // tiles_per_seq, 0, 0)),
            pl.BlockSpec((None, 1, D_MODEL), lambda i: (i // tiles_per_seq, 0, 1)),
            pl.BlockSpec((1, D_MODEL), const2),
            pl.BlockSpec((D_MODEL, IN_COLS_PAD), const2, pipeline_mode=pl.Buffered(1)),
            pl.BlockSpec((1, IN_COLS_PAD), const2),
            pl.BlockSpec((1, GMLP_WIDTH), const2),
            pl.BlockSpec((1, GMLP_WIDTH), const2),
            pl.BlockSpec((GMLP_HEADS, CHUNK, CHUNK), lambda i: (0, 0, 0)),
            pl.BlockSpec((TM_PROJ, GMLP_WIDTH), const2, pipeline_mode=pl.Buffered(1)),
            pl.BlockSpec((1, GMLP_WIDTH), const2),
        ],
        out_specs=[
            pl.BlockSpec((TM_PROJ, GMLP_WIDTH), row),
            pl.BlockSpec((TM_PROJ, SSD_WIDTH), row),
            pl.BlockSpec((TM_PROJ, CONV_CH), row),
            pl.BlockSpec((TM_PROJ, LANES), row),
        ],
        out_shape=[
            jax.ShapeDtypeStruct((t, GMLP_WIDTH), BF16),
            jax.ShapeDtypeStruct((t, SSD_WIDTH), BF16),
            jax.ShapeDtypeStruct((t, CONV_CH), F32),
            jax.ShapeDtypeStruct((t, LANES), F32),
        ],
        scratch_shapes=[pltpu.VMEM((TM_PROJ, GMLP_WIDTH), F32)],
        compiler_params=pltpu.CompilerParams(
            dimension_semantics=("arbitrary",), vmem_limit_bytes=VMEM_LIMIT),
        name="inproj_gmlp",
    )(x2, mod3, mod3, norm1_g, w_in_p, b_in_p, ln_g, ln_b, w_s, bs_tile, out_g)


def _ssd_chunk(act, dtv, a_row, expand, state, rev):
    off = SSD_HEADS if rev else 0
    row = lax.broadcasted_iota(jnp.int32, (CHUNK, CHUNK), 0)
    col = lax.broadcasted_iota(jnp.int32, (CHUNK, CHUNK), 1)
    lower = row >= col
    upper = row <= col
    keep = upper if rev else lower
    da = dtv * a_row
    cs = jnp.dot(keep.astype(F32), da, precision=HIGHEST, preferred_element_type=F32)
    cs_t = cs.T
    dt_t = dtv.T
    tot = cs[0:1, :] if rev else cs[CHUNK - 1:CHUNK, :]

    xs = act[:, :SSD_WIDTH]
    lane = lax.broadcasted_iota(jnp.int32, (CHUNK, LANES), 1)
    first_half = lane < SSD_HEAD_DIM
    zero = jnp.zeros((), BF16)

    stack = jnp.concatenate(
        [jnp.exp(cs), dtv * jnp.exp(tot - cs), jnp.broadcast_to(jnp.exp(tot), (SUBLANES, LANES))], axis=0)
    stack_x = jnp.dot(stack.astype(BF16), expand, preferred_element_type=F32)
    into_x = stack_x[:CHUNK]
    w_x = stack_x[CHUNK:2 * CHUNK]
    cd_x = stack_x[2 * CHUNK:2 * CHUNK + 1]
    xw = xs * w_x.astype(BF16)

    pieces = []
    for g in range(SSD_GROUPS):
        bg = act[:, SSD_WIDTH + g * SSD_STATE:SSD_WIDTH + (g + 1) * SSD_STATE]
        cg = act[:, SSD_WIDTH + SSD_GROUPS * SSD_STATE + g * SSD_STATE:
                 SSD_WIDTH + SSD_GROUPS * SSD_STATE + (g + 1) * SSD_STATE]
        scores = lax.dot_general(cg, bg, (((1,), (1,)), ((), ())), preferred_element_type=F32)
        heads_per_group = SSD_HEADS // SSD_GROUPS
        for pair in range(heads_per_group // 2):
            h0 = g * heads_per_group + 2 * pair
            xs_pair = xs[:, h0 * SSD_HEAD_DIM:(h0 + 2) * SSD_HEAD_DIM]
            y_pair = None
            for k in range(2):
                hh = off + h0 + k
                seg = cs[:, hh:hh + 1] - cs_t[hh:hh + 1, :]
                dec = jnp.exp(jnp.where(keep, seg, -jnp.inf))
                m = (scores * dec * dt_t[hh:hh + 1, :]).astype(BF16)
                rhs = jnp.where(first_half if k == 0 else jnp.logical_not(first_half), xs_pair, zero)
                part = jnp.dot(m, rhs, preferred_element_type=F32)
                y_pair = part if y_pair is None else y_pair + part
            pieces.append(y_pair)
    y_diag = jnp.concatenate(pieces, axis=1)

    y_off = []
    new_state = []
    for g in range(SSD_GROUPS):
        gcols = slice(g * GROUP_WIDTH, (g + 1) * GROUP_WIDTH)
        bg = act[:, SSD_WIDTH + g * SSD_STATE:SSD_WIDTH + (g + 1) * SSD_STATE]
        cg = act[:, SSD_WIDTH + SSD_GROUPS * SSD_STATE + g * SSD_STATE:
                 SSD_WIDTH + SSD_GROUPS * SSD_STATE + (g + 1) * SSD_STATE]
        prev = state[:, gcols]
        y_off.append(jnp.dot(cg, prev.astype(BF16), preferred_element_type=F32))
        bg_t = bg.astype(F32).T.astype(BF16)
        new = jnp.dot(bg_t, xw[:, gcols], preferred_element_type=F32)
        new_state.append(prev * cd_x[:, gcols] + new)
    y = y_diag + jnp.concatenate(y_off, axis=1) * into_x
    return y, jnp.concatenate(new_state, axis=1)


def _ssd_body(xbc_ref, xprev_ref, xnext_ref, dt_ref, z_ref, ya_ref, x_ref, gate_ref, cw_ref, cb_ref,
              dtb_ref, alog_ref, dsk_ref, ng_ref, wout_ref, exp_ref, o_ref,
              act_scr, dts_scr, yf_scr, st_scr, ext_scr, *, n_steps):
    d = pl.program_id(1)
    c = pl.program_id(2)
    lane1 = lax.broadcasted_iota(jnp.int32, (1, LANES), 1)
    a_row = jnp.where(lane1 < 2 * SSD_HEADS, -jnp.exp(alog_ref[...]), 0.0)
    chunk_rows = [slice(j * CHUNK, (j + 1) * CHUNK) for j in range(SSD_STEP_ROWS // CHUNK)]

    @pl.when(c == 0)
    def _():
        st_scr[...] = jnp.zeros_like(st_scr)

    def scan(act, dtv, expand, rev):
        state = st_scr[...]
        ys = [None] * len(chunk_rows)
        for j in (reversed(range(len(chunk_rows))) if rev else range(len(chunk_rows))):
            ys[j], state = _ssd_chunk(act[chunk_rows[j], :], dtv[chunk_rows[j], :], a_row, expand, state, rev)
        st_scr[...] = state
        return jnp.concatenate(ys, axis=0)

    @pl.when(d == 0)
    def _forward():
        row0 = pl.multiple_of(c * SSD_STEP_ROWS, SSD_STEP_ROWS)
        ext_scr[0:SUBLANES, :] = jnp.where(c > 0, xprev_ref[...], 0.0)
        ext_scr[SUBLANES:SUBLANES + SSD_STEP_ROWS, :] = xbc_ref[...]
        ext_scr[SUBLANES + SSD_STEP_ROWS:, :] = jnp.where(c < n_steps - 1, xnext_ref[...], 0.0)
        acc = cb_ref[...]
        for k in range(CONV_WIDTH):
            lo = SUBLANES - CONV_WIDTH // 2 + k
            acc = acc + cw_ref[k:k + 1, :] * ext_scr[lo:lo + SSD_STEP_ROWS, :]
        act = _silu(acc).astype(BF16)
        dtv = _softplus(dt_ref[...] + dtb_ref[...])
        act_scr[pl.ds(row0, SSD_STEP_ROWS), :] = act
        dts_scr[pl.ds(row0, SSD_STEP_ROWS), :] = dtv
        yf_scr[pl.ds(row0, SSD_STEP_ROWS), :] = scan(act, dtv, exp_ref[0], rev=False).astype(BF16)

    @pl.when(d == 1)
    def _backward():
        row0 = pl.multiple_of((n_steps - 1 - c) * SSD_STEP_ROWS, SSD_STEP_ROWS)
        act = act_scr[pl.ds(row0, SSD_STEP_ROWS), :]
        dtv = dts_scr[pl.ds(row0, SSD_STEP_ROWS), :]
        yb = scan(act, dtv, exp_ref[1], rev=True)
        xs = act[:, :SSD_WIDTH].astype(F32)
        y = yf_scr[pl.ds(row0, SSD_STEP_ROWS), :].astype(F32) + yb + dsk_ref[...] * xs
        y = y * _silu(z_ref[...].astype(F32))
        y = jnp.concatenate(
            [_rms(y[:, g * GROUP_WIDTH:(g + 1) * GROUP_WIDTH]) for g in range(SSD_GROUPS)], axis=1)
        y = y * ng_ref[...]
        mix = jnp.concatenate([ya_ref[...], y.astype(BF16)], axis=1)
        o = jnp.dot(mix, wout_ref[...], preferred_element_type=F32)
        o_ref[...] = x_ref[...] + gate_ref[...] * o


def _ssd_outproj(xbc, dt, z, ya, x2, mod3, conv_w8, conv_b, dtb_row, alog_row, dsk_row, norm_g, w_out_b,
                 expand, bn, seq):
    t = x2.shape[0]
    nc = seq // SSD_STEP_ROWS
    blocks8 = SSD_STEP_ROWS // SUBLANES
    last8 = t // SUBLANES - 1

    def fwd_chunk(b, d, c):
        return b * nc + c + d * (nc - 1 - c)

    def bwd_chunk(b, d, c):
        return b * nc + nc - 1 - d * c

    const2 = lambda b, d, c: (0, 0)
    return pl.pallas_call(
        functools.partial(_ssd_body, n_steps=nc),
        grid=(bn, 2, nc),
        in_specs=[
            pl.BlockSpec((SSD_STEP_ROWS, CONV_CH), lambda b, d, c: (fwd_chunk(b, d, c), 0)),
            pl.BlockSpec((SUBLANES, CONV_CH),
                         lambda b, d, c: (jnp.maximum(fwd_chunk(b, d, c) * blocks8 - 1, 0), 0)),
            pl.BlockSpec((SUBLANES, CONV_CH),
                         lambda b, d, c: (jnp.minimum((fwd_chunk(b, d, c) + 1) * blocks8, last8), 0)),
            pl.BlockSpec((SSD_STEP_ROWS, LANES), lambda b, d, c: (fwd_chunk(b, d, c), 0)),
            pl.BlockSpec((SSD_STEP_ROWS, SSD_WIDTH), lambda b, d, c: (bwd_chunk(b, d, c), 0)),
            pl.BlockSpec((SSD_STEP_ROWS, GMLP_WIDTH), lambda b, d, c: (bwd_chunk(b, d, c), 0)),
            pl.BlockSpec((SSD_STEP_ROWS, D_MODEL), lambda b, d, c: (bwd_chunk(b, d, c), 0)),
            pl.BlockSpec((None, 1, D_MODEL), lambda b, d, c: (b, 0, 2)),
            pl.BlockSpec((SUBLANES, CONV_CH), const2),
            pl.BlockSpec((1, CONV_CH), const2),
            pl.BlockSpec((1, LANES), const2),
            pl.BlockSpec((1, LANES), const2),
            pl.BlockSpec((1, SSD_WIDTH), const2),
            pl.BlockSpec((1, SSD_WIDTH), const2),
            pl.BlockSpec((GMLP_WIDTH + SSD_WIDTH, D_MODEL), const2, pipeline_mode=pl.Buffered(1)),
            pl.BlockSpec((2, LANES, SSD_WIDTH), lambda b, d, c: (0, 0, 0), pipeline_mode=pl.Buffered(1)),
        ],
        out_specs=pl.BlockSpec((SSD_STEP_ROWS, D_MODEL), lambda b, d, c: (bwd_chunk(b, d, c), 0)),
        out_shape=jax.ShapeDtypeStruct((t, D_MODEL), F32),
        scratch_shapes=[
            pltpu.VMEM((seq, CONV_CH), BF16),
            pltpu.VMEM((seq, LANES), F32),
            pltpu.VMEM((seq, SSD_WIDTH), BF16),
            pltpu.VMEM((SSD_STATE, SSD_WIDTH), F32),
            pltpu.VMEM((SSD_STEP_ROWS + 2 * SUBLANES, CONV_CH), F32),
        ],
        compiler_params=pltpu.CompilerParams(
            dimension_semantics=("arbitrary", "arbitrary", "arbitrary"), vmem_limit_bytes=VMEM_LIMIT),
        name="ssd_outproj",
    )(xbc, xbc, xbc, dt, z, ya, x2, mod3, conv_w8, conv_b, dtb_row, alog_row, dsk_row, norm_g, w_out_b,
      expand)


def _route(logits):
    lane = lax.broadcasted_iota(jnp.int32, logits.shape, 1)
    big = jnp.int32(LANES)
    neg = -jnp.inf
    gmask = (lane >= N_EXPERTS) & (lane < N_EXPERTS + N_EGROUPS)
    gl = jnp.where(gmask, logits, neg)
    gmax = jnp.max(gl, axis=-1, keepdims=True)
    gidx = jnp.min(jnp.where(gl == gmax, lane, big), axis=-1, keepdims=True) - N_EXPERTS
    p_g = 1.0 / jnp.sum(jnp.where(gmask, jnp.exp(gl - gmax), 0.0), axis=-1, keepdims=True)
    lo = gidx * EXPERTS_PER_GROUP
    emask = (lane >= lo) & (lane < lo + EXPERTS_PER_GROUP)
    el = jnp.where(emask, logits, neg)
    v1 = jnp.max(el, axis=-1, keepdims=True)
    i1 = jnp.min(jnp.where(el == v1, lane, big), axis=-1, keepdims=True)
    el2 = jnp.where(lane == i1, neg, el)
    v2 = jnp.max(el2, axis=-1, keepdims=True)
    i2 = jnp.min(jnp.where(el2 == v2, lane, big), axis=-1, keepdims=True)
    e2 = jnp.exp(v2 - v1)
    den = 1.0 + e2
    w1 = p_g / den
    w2 = p_g * e2 / den
    return i1, i2, w1, w2


def _router_body(x_ref, shift_ref, scale_ref, g2_ref, wr_ref, br_ref, h_ref, rt_ref, cnt_ref):
    counts = jnp.zeros((1, LANES), F32)
    for part in range(TM_MOE // ROUTER_ROWS):
        rows = slice(part * ROUTER_ROWS, (part + 1) * ROUTER_ROWS)
        h = _rms(x_ref[rows, :]) * g2_ref[...]
        h = h * (1.0 + scale_ref[...]) + shift_ref[...]
        hb = h.astype(BF16)
        h_ref[rows, :] = hb
        h_lo = (h - hb.astype(F32)).astype(BF16)
        both = jnp.dot(hb, wr_ref[...], preferred_element_type=F32)
        logits = (both[:, :LANES] + both[:, LANES:]
                  + jnp.dot(h_lo, wr_ref[:, :LANES], preferred_element_type=F32)) + br_ref[...]
        i1, i2, w1, w2 = _route(logits)
        lane = lax.broadcasted_iota(jnp.int32, (ROUTER_ROWS, LANES), 1)
        rt_ref[rows, :] = jnp.where(lane == 0, i1.astype(F32), jnp.where(lane == 1, i2.astype(F32),
                                    jnp.where(lane == 2, w1, jnp.where(lane == 3, w2, 0.0))))
        mask = jnp.where(lane == i1, 1.0, jnp.where(lane == i2, 1.0, 0.0))
        counts = counts + jnp.sum(mask, axis=0, keepdims=True)
    cnt_ref[...] = jnp.broadcast_to(counts, (SUBLANES, LANES))


def _router(x1, mod3, norm2_g, w_router, b_router, seq):
    t = x1.shape[0]
    tiles_per_seq = seq // TM_MOE
    n_tiles = t // TM_MOE
    row = lambda i: (i, 0)
    const2 = lambda i: (0, 0)
    return pl.pallas_call(
        _router_body,
        grid=(n_tiles,),
        in_specs=[
            pl.BlockSpec((TM_MOE, D_MODEL), row),
            pl.BlockSpec((None, 1, D_MODEL), lambda i: (i // tiles_per_seq, 0, 3)),
            pl.BlockSpec((None, 1, D_MODEL), lambda i: (i // tiles_per_seq, 0, 4)),
            pl.BlockSpec((1, D_MODEL), const2),
            pl.BlockSpec((D_MODEL, 2 * LANES), const2),
            pl.BlockSpec((1, LANES), const2),
        ],
        out_specs=[
            pl.BlockSpec((TM_MOE, D_MODEL), row),
            pl.BlockSpec((TM_MOE, LANES), row),
            pl.BlockSpec((SUBLANES, LANES), row),
        ],
        out_shape=[
            jax.ShapeDtypeStruct((t, D_MODEL), BF16),
            jax.ShapeDtypeStruct((t, LANES), F32),
            jax.ShapeDtypeStruct((n_tiles * SUBLANES, LANES), F32),
        ],
        compiler_params=pltpu.CompilerParams(
            dimension_semantics=("arbitrary",), vmem_limit_bytes=VMEM_LIMIT),
        name="moe_router",
    )(x1, mod3, mod3, norm2_g, w_router, b_router)


def _row_copy(vmem_buf, v_start, hbm_ref, g_start, size, sem, to_hbm):
    v = vmem_buf.at[pl.ds(v_start, size), :]
    g = hbm_ref.at[pl.ds(g_start, size), :]
    return pltpu.make_async_copy(v, g, sem) if to_hbm else pltpu.make_async_copy(g, v, sem)


def _copy_rows(n, vmem_buf, v0, hbm_ref, g0, sem, *, to_hbm, wait):
    def run(v_off, g_off, size):
        v_start = 0 if v0 is None else pl.multiple_of(v0 + v_off, SEG_ALIGN)
        cp = _row_copy(vmem_buf, v_start, hbm_ref, pl.multiple_of(g0 + g_off, SEG_ALIGN), size, sem, to_hbm)
        if wait:
            cp.wait()
        else:
            cp.start()

    n_big = n // PIECE_ROWS

    def big_piece(k, carry):
        run(k * PIECE_ROWS, k * PIECE_ROWS, PIECE_ROWS)
        return carry

    lax.fori_loop(0, n_big, big_piece, 0)
    base = n_big * PIECE_ROWS
    rem = n - base
    for size in REM_SIZES:
        done = base + (rem // (2 * size)) * (2 * size)

        @pl.when((rem & size) != 0)
        def _():
            run(done, done, size)


def _segment_copies(seg_ref, loc_ref, goff_ref, tile, vmem_buf, hbm_ref, sem, *, to_hbm):
    base = tile * N_EXPERTS

    def per_expert(e, carry):
        _copy_rows(seg_ref[base + e], vmem_buf, loc_ref[base + e], hbm_ref, goff_ref[base + e], sem,
                   to_hbm=to_hbm, wait=False)
        return carry

    lax.fori_loop(0, N_EXPERTS, per_expert, 0)


def _tile_rows(seg_ref, loc_ref, tile):
    last = tile * N_EXPERTS + N_EXPERTS - 1
    return loc_ref[last] + seg_ref[last]


def _segment_wait(seg_ref, loc_ref, tile, vmem_buf, hbm_ref, sem, *, to_hbm):
    total = _tile_rows(seg_ref, loc_ref, tile)
    for size in TOTAL_SIZES:
        @pl.when((total & size) != 0)
        def _():
            _row_copy(vmem_buf, 0, hbm_ref, 0, size, sem, to_hbm).wait()


def _dispatch_body(seg_ref, loc_ref, goff_ref, fill_ref, h_ref, rt_ref, locrow_ref, pos_ref, xs_hbm, buf, sem,
                   *, n_tiles, n_row_tiles):
    i = pl.program_id(0)
    slot = i % 2

    def copies(tile, vmem_buf, dma_sem, waiting):
        base = tile * N_EXPERTS
        if waiting:
            for _ in range(N_EXPERTS * SEG_PIECE // WAIT_ROWS):
                _row_copy(vmem_buf, 0, xs_hbm, 0, WAIT_ROWS, dma_sem, True).wait()
        else:
            for e in range(N_EXPERTS):
                _row_copy(vmem_buf, pl.multiple_of(loc_ref[base + e], SEG_ALIGN), xs_hbm,
                          pl.multiple_of(goff_ref[base + e], SEG_ALIGN), SEG_PIECE, dma_sem, True).start()

        @pl.when(fill_ref[2 * N_EXPERTS + 1 + tile] > 0)
        def _():
            def per_expert(e, carry):
                def piece(k, inner):
                    off = k * SEG_PIECE
                    cp = _row_copy(vmem_buf, pl.multiple_of(loc_ref[base + e] + off, SEG_ALIGN), xs_hbm,
                                   pl.multiple_of(goff_ref[base + e] + off, SEG_ALIGN), SEG_PIECE, dma_sem, True)
                    if waiting:
                        cp.wait()
                    else:
                        cp.start()
                    return inner

                lax.fori_loop(1, (seg_ref[base + e] + SEG_PIECE - 1) // SEG_PIECE, piece, 0)
                return carry

            lax.fori_loop(0, N_EXPERTS, per_expert, 0)

    @pl.when(i == 0)
    def _():
        buf[:, R_LOC:, :] = jnp.zeros((2, SEG_PIECE, D_MODEL), BF16)

    rt = rt_ref[...]
    lane = lax.broadcasted_iota(jnp.int32, (TM_MOE, LANES), 1)
    lanef = lane.astype(F32)
    e1 = rt[:, 0:1]
    e2 = rt[:, 1:2]
    mask = jnp.where(lanef == e1, 1.0, jnp.where(lanef == e2, 1.0, 0.0))
    r = lax.broadcasted_iota(jnp.int32, (TM_MOE, TM_MOE), 0)
    cc = lax.broadcasted_iota(jnp.int32, (TM_MOE, TM_MOE), 1)
    strict = jnp.where(r > cc, 1.0, 0.0).astype(BF16)
    rank = jnp.dot(strict, mask.astype(BF16), preferred_element_type=F32)
    posall = rank + locrow_ref[...]
    pos1 = jnp.sum(jnp.where(lanef == e1, posall, 0.0), axis=-1, keepdims=True)
    pos2 = jnp.sum(jnp.where(lanef == e2, posall, 0.0), axis=-1, keepdims=True)
    pp = jnp.where(lane == 0, pos1, jnp.where(lane == 1, pos2, rt))
    pos_ref[...] = pp

    slots_t = [pp[k * LANES:(k + 1) * LANES, :].T for k in range(TM_MOE // LANES)]

    rowp = lax.broadcasted_iota(jnp.int32, (R_LOC, LANES), 0).astype(F32)
    perm = jnp.concatenate(
        [jnp.where(rowp == st[0:1, :], 1.0, jnp.where(rowp == st[1:2, :], 1.0, 0.0)).astype(BF16)
         for st in slots_t], axis=1)
    buf[slot, 0:R_LOC, :] = jnp.dot(perm, h_ref[...], preferred_element_type=F32).astype(BF16)

    @pl.when(i >= 1)
    def _():
        copies(i - 1, buf.at[1 - slot], sem.at[1 - slot], waiting=True)

    copies(i, buf.at[slot], sem.at[slot], waiting=False)

    @pl.when(i == n_tiles - 1)
    def _():
        copies(i, buf.at[slot], sem.at[slot], waiting=True)
        zsrc = buf.at[1 - slot]
        zsem = sem.at[1 - slot]
        zsrc[0:TR_EXPERT, :] = jnp.zeros((TR_EXPERT, D_MODEL), BF16)
        for waiting in (False, True):
            def per_expert(e, carry):
                _copy_rows(fill_ref[N_EXPERTS + e], zsrc, None, xs_hbm, fill_ref[e], zsem,
                           to_hbm=True, wait=waiting)
                return carry

            def per_row_tile(k, carry):
                cp = _row_copy(zsrc, 0, xs_hbm, pl.multiple_of(k * TR_EXPERT, TR_EXPERT), TR_EXPERT, zsem, True)
                if waiting:
                    cp.wait()
                else:
                    cp.start()
                return carry

            lax.fori_loop(0, N_EXPERTS, per_expert, 0)
            lax.fori_loop(fill_ref[2 * N_EXPERTS], n_row_tiles, per_row_tile, 0)


def _dispatch(h2, rt, seg, loc, goff, fill, locrow, n_row_tiles):
    t = h2.shape[0]
    n_tiles = t // TM_MOE
    row = lambda i, *_: (i, 0)
    grid_spec = pltpu.PrefetchScalarGridSpec(
        num_scalar_prefetch=4,
        grid=(n_tiles,),
        in_specs=[
            pl.BlockSpec((TM_MOE, D_MODEL), row),
            pl.BlockSpec((TM_MOE, LANES), row),
            pl.BlockSpec((None, 1, LANES), lambda i, *_: (i, 0, 0)),
        ],
        out_specs=[
            pl.BlockSpec((TM_MOE, LANES), row),
            pl.BlockSpec(memory_space=pl.ANY),
        ],
        scratch_shapes=[
            pltpu.VMEM((2, R_LOC + SEG_PIECE, D_MODEL), BF16),
            pltpu.SemaphoreType.DMA((2,)),
        ],
    )
    return pl.pallas_call(
        functools.partial(_dispatch_body, n_tiles=n_tiles, n_row_tiles=n_row_tiles),
        grid_spec=grid_spec,
        out_shape=[
            jax.ShapeDtypeStruct((t, LANES), F32),
            jax.ShapeDtypeStruct((n_row_tiles * TR_EXPERT, D_MODEL), BF16),
        ],
        compiler_params=pltpu.CompilerParams(
            dimension_semantics=("arbitrary",), vmem_limit_bytes=VMEM_LIMIT),
        name="moe_dispatch",
    )(seg, loc, goff, fill, h2, rt, locrow)


def _expert_body(te_ref, tw_ref, na_ref, xs_ref, w1a_ref, w3a_ref, w2a_ref, w1b_ref, w3b_ref, w2b_ref, y_ref):
    del tw_ref
    r = pl.program_id(0)
    n_active = na_ref[0]
    weights = ((w1a_ref, w3a_ref, w2a_ref), (w1b_ref, w3b_ref, w2b_ref))

    def swiglu(half, rows=None):
        w1_ref, w3_ref, w2_ref = weights[half]
        rows = slice(half * TR_EXPERT, (half + 1) * TR_EXPERT) if rows is None else rows
        x = xs_ref[rows, :]
        a = _silu(jnp.dot(x, w1_ref[...], preferred_element_type=F32))
        a = a * jnp.dot(x, w3_ref[...], preferred_element_type=F32)
        y_ref[rows, :] = jnp.dot(a.astype(BF16), w2_ref[...], preferred_element_type=F32).astype(BF16)

    first = EXPERT_TILES_PER_STEP * r
    both = first + 1 < n_active
    same = te_ref[first] == te_ref[first + 1]

    @pl.when(both & same)
    def _():
        swiglu(0, slice(0, EXPERT_TILES_PER_STEP * TR_EXPERT))

    @pl.when(both & jnp.logical_not(same))
    def _():
        swiglu(0)
        swiglu(1)

    @pl.when(first + 1 == n_active)
    def _():
        swiglu(0)
        y_ref[TR_EXPERT:, :] = jnp.zeros((TR_EXPERT, D_MODEL), BF16)

    @pl.when(first >= n_active)
    def _():
        y_ref[...] = jnp.zeros_like(y_ref)


def _experts(xs, tile_expert, tile_weights, n_active, w1b, w3b, w2b):
    n_rows = xs.shape[0]
    step_rows = EXPERT_TILES_PER_STEP * TR_EXPERT
    x_block = lambda r, te, tw, na: (jnp.minimum(r, (na[0] - 1) // EXPERT_TILES_PER_STEP), 0)
    w_specs = []
    for half in range(EXPERT_TILES_PER_STEP):
        w_block = lambda r, te, tw, na, half=half: (tw[EXPERT_TILES_PER_STEP * r + half], 0, 0)
        w_specs += [pl.BlockSpec((None, D_MODEL, D_EXPERT), w_block),
                    pl.BlockSpec((None, D_MODEL, D_EXPERT), w_block),
                    pl.BlockSpec((None, D_EXPERT, D_MODEL), w_block)]
    grid_spec = pltpu.PrefetchScalarGridSpec(
        num_scalar_prefetch=3,
        grid=(n_rows // step_rows,),
        in_specs=[pl.BlockSpec((step_rows, D_MODEL), x_block)] + w_specs,
        out_specs=pl.BlockSpec((step_rows, D_MODEL), lambda r, te, tw, na: (r, 0)),
    )
    return pl.pallas_call(
        _expert_body,
        grid_spec=grid_spec,
        out_shape=jax.ShapeDtypeStruct((n_rows, D_MODEL), BF16),
        compiler_params=pltpu.CompilerParams(
            dimension_semantics=("arbitrary",), vmem_limit_bytes=VMEM_LIMIT),
        name="moe_experts",
    )(tile_expert, tile_weights, n_active, xs, w1b, w3b, w2b, w1b, w3b, w2b)


def _combine_body(seg_ref, loc_ref, goff_ref, y_hbm, pos_ref, x_ref, gate_ref, fg_ref, o_ref, buf, sem, *,
                  n_tiles, final_norm):
    i = pl.program_id(0)
    slot = i % 2
    start = functools.partial(_segment_copies, seg_ref, loc_ref, goff_ref, hbm_ref=y_hbm, to_hbm=False)

    @pl.when(i == 0)
    def _():
        buf[...] = jnp.zeros_like(buf)
        start(0, buf.at[0], sem=sem.at[0])

    @pl.when(i + 1 < n_tiles)
    def _():
        start(i + 1, buf.at[1 - slot], sem=sem.at[1 - slot])

    _segment_wait(seg_ref, loc_ref, i, buf.at[slot], y_hbm, sem.at[slot], to_hbm=False)
    for part in range(TM_MOE // COMBINE_ROWS):
        rows = slice(part * COMBINE_ROWS, (part + 1) * COMBINE_ROWS)
        pp = pos_ref[rows, :]
        colp = lax.broadcasted_iota(jnp.int32, (COMBINE_ROWS, R_LOC), 1).astype(F32)
        wc = jnp.where(colp == pp[:, 0:1], pp[:, 2:3], jnp.where(colp == pp[:, 1:2], pp[:, 3:4], 0.0))
        moe = jnp.dot(wc.astype(BF16), buf[slot], preferred_element_type=F32)
        x2 = x_ref[rows, :] + gate_ref[...] * moe
        o_ref[rows, :] = _rms(x2) * fg_ref[...] if final_norm else x2


def _combine(y, pos, x1, mod3, final_g, seg, loc, goff, seq, final_norm):
    t = x1.shape[0]
    n_tiles = t // TM_MOE
    tiles_per_seq = seq // TM_MOE
    row = lambda i, *_: (i, 0)
    grid_spec = pltpu.PrefetchScalarGridSpec(
        num_scalar_prefetch=3,
        grid=(n_tiles,),
        in_specs=[
            pl.BlockSpec(memory_space=pl.ANY),
            pl.BlockSpec((TM_MOE, LANES), row),
            pl.BlockSpec((TM_MOE, D_MODEL), row),
            pl.BlockSpec((None, 1, D_MODEL), lambda i, *_: (i // tiles_per_seq, 0, 5)),
            pl.BlockSpec((1, D_MODEL), lambda i, *_: (0, 0)),
        ],
        out_specs=pl.BlockSpec((TM_MOE, D_MODEL), row),
        scratch_shapes=[
            pltpu.VMEM((2, R_LOC, D_MODEL), BF16),
            pltpu.SemaphoreType.DMA((2,)),
        ],
    )
    return pl.pallas_call(
        functools.partial(_combine_body, n_tiles=n_tiles, final_norm=final_norm),
        grid_spec=grid_spec,
        out_shape=jax.ShapeDtypeStruct((t, D_MODEL), F32),
        compiler_params=pltpu.CompilerParams(
            dimension_semantics=("arbitrary",), vmem_limit_bytes=VMEM_LIMIT),
        name="moe_combine",
    )(seg, loc, goff, y, pos, x1, mod3, final_g)


def _moe_plan(counts, n_tiles, n_row_tiles):
    cnt = counts.reshape(n_tiles, SUBLANES, LANES)[:, 0, :N_EXPERTS].astype(jnp.int32)
    seg = (cnt + SEG_ALIGN - 1) // SEG_ALIGN * SEG_ALIGN
    before_e = jnp.arange(N_EXPERTS)[:, None] < jnp.arange(N_EXPERTS)[None, :]
    before_t = jnp.arange(n_tiles)[:, None] > jnp.arange(n_tiles)[None, :]
    loc = jnp.sum(jnp.where(before_e[None], seg[:, :, None], 0), axis=1)
    tot = jnp.sum(seg, axis=0)
    region = (tot + SEG_PIECE + TR_EXPERT - 1) // TR_EXPERT * TR_EXPERT
    gstart = jnp.sum(jnp.where(before_e, region[:, None], 0), axis=0)
    gend = gstart + region
    goff = gstart[None, :] + jnp.sum(jnp.where(before_t[:, :, None], seg[None], 0), axis=1)
    n_active = gend[-1] // TR_EXPERT
    tile_row0 = jnp.arange(n_row_tiles, dtype=jnp.int32) * TR_EXPERT
    last_row0 = (n_active - 1) * TR_EXPERT
    te = jnp.sum(gend[None, :] <= jnp.minimum(tile_row0, last_row0)[:, None], axis=1).astype(jnp.int32)
    te = jnp.minimum(te, N_EXPERTS - 1)
    te2 = te.reshape(-1, EXPERT_TILES_PER_STEP)
    cand = jnp.where(te2[:, 1] != te2[:, 0], te2[:, 1], 0)
    steps = jnp.arange(te2.shape[0])
    held = jnp.max(jnp.where(steps[None, :] <= steps[:, None], cand[None, :], 0), axis=1)
    tw = jnp.stack([te2[:, 0], held], axis=1).reshape(-1).astype(jnp.int32)
    locrow = _pad_cols(loc.astype(F32), LANES).reshape(n_tiles, 1, LANES)
    n_active = n_active.reshape(1).astype(jnp.int32)
    extra_pieces = jnp.sum(jnp.maximum((seg + SEG_PIECE - 1) // SEG_PIECE - 1, 0), axis=1)
    fill = jnp.concatenate([gstart + tot, region - tot, n_active, extra_pieces]).astype(jnp.int32)
    return (seg.reshape(-1), loc.reshape(-1).astype(jnp.int32), goff.reshape(-1).astype(jnp.int32), fill,
            locrow, te, tw, n_active)


def _moe_final(x1, mod3, norm2_g, w_router, b_router, w1b, w3b, w2b, final_g, seq, final_norm):
    t = x1.shape[0]
    n_tiles = t // TM_MOE
    max_rows = (N_ASSIGN * t + (SEG_ALIGN - 1) * N_EXPERTS * n_tiles
                + N_EXPERTS * (SEG_PIECE + TR_EXPERT - SEG_ALIGN))
    step_rows = EXPERT_TILES_PER_STEP * TR_EXPERT
    n_row_tiles = -(-max_rows // step_rows) * EXPERT_TILES_PER_STEP
    h2, rt, counts = _router(x1, mod3, norm2_g, w_router, b_router, seq)
    seg, loc, goff, fill, locrow, te, tw, n_active = _moe_plan(counts, n_tiles, n_row_tiles)
    pos, xs = _dispatch(h2, rt, seg, loc, goff, fill, locrow, n_row_tiles)
    y = _experts(xs, te, tw, n_active, w1b, w3b, w2b)
    return _combine(y, pos, x1, mod3, final_g, seg, loc, goff, seq, final_norm)


def _pad_cols(a, width):
    return jnp.pad(a, ((0, 0), (0, width - a.shape[1])))


def kernel(x, c, w_ada, b_ada, norm1_g, w_in, b_in, gmlp_ln_g, gmlp_ln_b, gmlp_w_s, gmlp_b_s, gmlp_out_g, conv_w, conv_b, a_log_f, a_log_b, dt_bias_f, dt_bias_b, d_skip, ssd_norm_g, w_out, norm2_g, w_router_g, b_router_g, w_router_e, b_router_e, w1, w3, w2, final_g):
    bn, seq, _ = x.shape
    depth = w_ada.shape[0]
    x2 = x.reshape(bn * seq, D_MODEL)

    head_rows = jnp.arange(LANES)[:, None]
    head_cols = jnp.arange(SSD_WIDTH)[None, :] // SSD_HEAD_DIM
    expand = jnp.stack([head_rows == head_cols, head_rows == head_cols + SSD_HEADS]).astype(BF16)

    for l in range(depth):
        mod3 = _modulation(c, w_ada, b_ada[l], l).reshape(bn, 1, N_MOD * D_MODEL)

        w_in_p = _pad_cols(w_in[l], IN_COLS_PAD).astype(BF16)
        b_in_p = _pad_cols(b_in[l][None, :], IN_COLS_PAD)
        bs = jnp.repeat(gmlp_b_s[l].T, LANES, axis=1)
        bs_tile = jnp.tile(bs, (TM_PROJ // CHUNK, 1))
        ya, z, xbc, dt = _inproj_gmlp(
            x2, mod3, norm1_g[l][None, :], w_in_p, b_in_p, gmlp_ln_g[l][None, :], gmlp_ln_b[l][None, :],
            gmlp_w_s[l].astype(BF16), bs_tile, gmlp_out_g[l][None, :], seq)

        conv_w8 = jnp.pad(conv_w[l], ((0, SUBLANES - CONV_WIDTH), (0, 0)))
        dtb_row = _pad_cols(jnp.concatenate([dt_bias_f[l], dt_bias_b[l]])[None, :], LANES)
        alog_row = _pad_cols(jnp.concatenate([a_log_f[l], a_log_b[l]])[None, :], LANES)
        dsk_row = jnp.repeat(d_skip[l], SSD_HEAD_DIM)[None, :]
        x2 = _ssd_outproj(xbc, dt, z, ya, x2, mod3, conv_w8, conv_b[l][None, :], dtb_row, alog_row, dsk_row,
                          ssd_norm_g[l][None, :], w_out[l].astype(BF16), expand, bn, seq)

        w_re = jnp.transpose(w_router_e[l], (1, 0, 2)).reshape(D_MODEL, N_EXPERTS)
        w_router = _pad_cols(jnp.concatenate([w_re, w_router_g[l]], axis=1), LANES)
        w_router_hi = w_router.astype(BF16)
        w_router = jnp.concatenate([w_router_hi, (w_router - w_router_hi.astype(F32)).astype(BF16)], axis=1)
        b_router = _pad_cols(jnp.concatenate([b_router_e[l].reshape(-1), b_router_g[l]])[None, :], LANES)
        x2 = _moe_final(x2, mod3, norm2_g[l][None, :], w_router, b_router, w1[l].astype(BF16),
                        w3[l].astype(BF16), w2[l].astype(BF16), final_g[None, :], seq,
                        final_norm=(l == depth - 1))
    return x2.reshape(bn, seq, D_MODEL)
```

```python
import functools
import math

import jax
import jax.numpy as jnp
from jax import lax
from jax.experimental import pallas as pl
from jax.experimental.pallas import tpu as pltpu

F32 = jnp.float32
BF16 = jnp.bfloat16
HIGHEST = lax.Precision.HIGHEST

D_MODEL = 1024
N_MOD = 6
GMLP_WIDTH = 1024
GMLP_HEADS = 8
CHUNK = 128
SSD_STEP_ROWS = 4 * CHUNK
SSD_WIDTH = 1024
SSD_HEADS = 16
SSD_HEAD_DIM = 64
SSD_GROUPS = 2
SSD_STATE = 128
GROUP_WIDTH = SSD_WIDTH // SSD_GROUPS
CONV_WIDTH = 5
CONV_CH = SSD_WIDTH + 2 * SSD_GROUPS * SSD_STATE
N_EGROUPS = 4
EXPERTS_PER_GROUP = 8
N_EXPERTS = 32
D_EXPERT = 256
EPS = 1e-6

LANES = 128
SUBLANES = 8
COL_U, COL_V, COL_Z, COL_XBC, COL_DT = 0, 1024, 2048, 3072, 4608
IN_COLS = 4640
IN_COLS_PAD = COL_DT + LANES
TM_PROJ = 1024
PROJ_ROWS = 512
COMBINE_ROWS = 256
ROUTER_ROWS = 256
TM_MOE = 512
TR_EXPERT = 512
EXPERT_TILES_PER_STEP = 2
N_ASSIGN = 2
SEG_ALIGN = 2 * SUBLANES
R_LOC = N_ASSIGN * TM_MOE + N_EXPERTS * SEG_ALIGN
PIECE_ROWS = 128
REM_SIZES = tuple(PIECE_ROWS >> s for s in range(1, PIECE_ROWS.bit_length()) if PIECE_ROWS >> s >= SEG_ALIGN)
TOTAL_SIZES = tuple(1 << s for s in range(R_LOC.bit_length() - 1, -1, -1) if 1 << s >= SEG_ALIGN)
VMEM_LIMIT = 56 * 1024 * 1024


def _silu(v):
    return v * jax.nn.sigmoid(v)


def _gelu(v):
    return 0.5 * v * (1.0 + lax.erf(v * math.sqrt(0.5)))


def _softplus(v):
    return jnp.maximum(v, 0.0) + jnp.log1p(jnp.exp(-jnp.abs(v)))


def _rms(v):
    return v * lax.rsqrt(jnp.mean(v * v, axis=-1, keepdims=True) + EPS)


def _mod_body(c_ref, w_ref, b_ref, o_ref):
    ca = _silu(c_ref[...])
    o_ref[...] = jnp.dot(ca, w_ref[...], precision=HIGHEST, preferred_element_type=F32) + b_ref[...]


def _modulation(c, w_ada, b_ada, layer):
    bn = c.shape[0]
    return pl.pallas_call(
        _mod_body,
        grid=(N_MOD,),
        in_specs=[
            pl.BlockSpec((bn, D_MODEL), lambda j: (0, 0)),
            pl.BlockSpec((None, D_MODEL, D_MODEL), lambda j: (layer, 0, j)),
            pl.BlockSpec((1, D_MODEL), lambda j: (0, j)),
        ],
        out_specs=pl.BlockSpec((bn, D_MODEL), lambda j: (0, j)),
        out_shape=jax.ShapeDtypeStruct((bn, N_MOD * D_MODEL), F32),
        name="adaln_mod",
    )(c, w_ada, b_ada.reshape(1, -1))


def _inproj_body(x_ref, shift_ref, scale_ref, g_ref, w_ref, b_ref, lng_ref, lnb_ref, ws_ref, bs_ref,
                 og_ref, ya_ref, z_ref, xbc_ref, dt_ref, mix_scr):
    for part in range(TM_PROJ // PROJ_ROWS):
        rows = slice(part * PROJ_ROWS, (part + 1) * PROJ_ROWS)
        h = _rms(x_ref[rows, :]) * g_ref[...]
        h = h * (1.0 + scale_ref[...]) + shift_ref[...]
        hb = h.astype(BF16)

        def proj(lo, hi, hb=hb):
            return jnp.dot(hb, w_ref[:, lo:hi], preferred_element_type=F32) + b_ref[:, lo:hi]

        z_ref[rows, :] = proj(COL_Z, COL_XBC).astype(BF16)
        xbc_ref[rows, :] = proj(COL_XBC, COL_DT)
        dt_ref[rows, :] = proj(COL_DT, IN_COLS_PAD)

        v = _gelu(proj(COL_V, COL_Z))
        mu = jnp.mean(v, axis=-1, keepdims=True)
        vc = v - mu
        var = jnp.mean(vc * vc, axis=-1, keepdims=True)
        vn = (vc * lax.rsqrt(var + EPS) * lng_ref[...] + lnb_ref[...]).astype(BF16)
        n_chunks = PROJ_ROWS // CHUNK
        for hd in range(GMLP_HEADS):
            cols = slice(hd * LANES, (hd + 1) * LANES)
            rhs = jnp.concatenate([vn[c * CHUNK:(c + 1) * CHUNK, cols] for c in range(n_chunks)], axis=1)
            res = jnp.dot(ws_ref[hd], rhs, preferred_element_type=F32)
            for c in range(n_chunks):
                mix_scr[part * PROJ_ROWS + c * CHUNK:part * PROJ_ROWS + (c + 1) * CHUNK, cols] = (
                    res[:, c * LANES:(c + 1) * LANES])
        u = _gelu(proj(COL_U, COL_V))
        out = u * (mix_scr[rows, :] + bs_ref[rows, :])
        ya_ref[rows, :] = (_rms(out) * og_ref[...]).astype(BF16)


def _inproj_gmlp(x2, mod3, norm1_g, w_in_p, b_in_p, ln_g, ln_b, w_s, bs_tile, out_g, seq):
    t = x2.shape[0]
    tiles_per_seq = seq // TM_PROJ
    row = lambda i: (i, 0)
    const2 = lambda i: (0, 0)
    return pl.pallas_call(
        _inproj_body,
        grid=(t // TM_PROJ,),
        in_specs=[
            pl.BlockSpec((TM_PROJ, D_MODEL), row),
            pl.BlockSpec((None, 1, D_MODEL), lambda i: (i // tiles_per_seq, 0, 0)),
            pl.BlockSpec((None, 1, D_MODEL), lambda i: (i // tiles_per_seq, 0, 1)),
            pl.BlockSpec((1, D_MODEL), const2),
            pl.BlockSpec((D_MODEL, IN_COLS_PAD), const2, pipeline_mode=pl.Buffered(1)),
            pl.BlockSpec((1, IN_COLS_PAD), const2),
            pl.BlockSpec((1, GMLP_WIDTH), const2),
            pl.BlockSpec((1, GMLP_WIDTH), const2),
            pl.BlockSpec((GMLP_HEADS, CHUNK, CHUNK), lambda i: (0, 0, 0)),
            pl.BlockSpec((TM_PROJ, GMLP_WIDTH), const2, pipeline_mode=pl.Buffered(1)),
            pl.BlockSpec((1, GMLP_WIDTH), const2),
        ],
        out_specs=[
            pl.BlockSpec((TM_PROJ, GMLP_WIDTH), row),
            pl.BlockSpec((TM_PROJ, SSD_WIDTH), row),
            pl.BlockSpec((TM_PROJ, CONV_CH), row),
            pl.BlockSpec((TM_PROJ, LANES), row),
        ],
        out_shape=[
            jax.ShapeDtypeStruct((t, GMLP_WIDTH), BF16),
            jax.ShapeDtypeStruct((t, SSD_WIDTH), BF16),
            jax.ShapeDtypeStruct((t, CONV_CH), F32),
            jax.ShapeDtypeStruct((t, LANES), F32),
        ],
        scratch_shapes=[pltpu.VMEM((TM_PROJ, GMLP_WIDTH), F32)],
        compiler_params=pltpu.CompilerParams(
            dimension_semantics=("arbitrary",), vmem_limit_bytes=VMEM_LIMIT),
        name="inproj_gmlp",
    )(x2, mod3, mod3, norm1_g, w_in_p, b_in_p, ln_g, ln_b, w_s, bs_tile, out_g)


def _ssd_chunk(act, dtv, a_row, expand, state, rev):
    off = SSD_HEADS if rev else 0
    row = lax.broadcasted_iota(jnp.int32, (CHUNK, CHUNK), 0)
    col = lax.broadcasted_iota(jnp.int32, (CHUNK, CHUNK), 1)
    lower = row >= col
    upper = row <= col
    keep = upper if rev else lower
    da = dtv * a_row
    cs = jnp.dot(keep.astype(F32), da, precision=HIGHEST, preferred_element_type=F32)
    cs_t = cs.T
    dt_t = dtv.T
    tot = cs[0:1, :] if rev else cs[CHUNK - 1:CHUNK, :]

    xs = act[:, :SSD_WIDTH]
    lane = lax.broadcasted_iota(jnp.int32, (CHUNK, LANES), 1)
    first_half = lane < SSD_HEAD_DIM
    zero = jnp.zeros((), BF16)

    stack = jnp.concatenate(
        [jnp.exp(cs), dtv * jnp.exp(tot - cs), jnp.broadcast_to(jnp.exp(tot), (SUBLANES, LANES))], axis=0)
    stack_x = jnp.dot(stack.astype(BF16), expand, preferred_element_type=F32)
    into_x = stack_x[:CHUNK]
    w_x = stack_x[CHUNK:2 * CHUNK]
    cd_x = stack_x[2 * CHUNK:2 * CHUNK + 1]
    xw = xs * w_x.astype(BF16)

    pieces = []
    for g in range(SSD_GROUPS):
        bg = act[:, SSD_WIDTH + g * SSD_STATE:SSD_WIDTH + (g + 1) * SSD_STATE]
        cg = act[:, SSD_WIDTH + SSD_GROUPS * SSD_STATE + g * SSD_STATE:
                 SSD_WIDTH + SSD_GROUPS * SSD_STATE + (g + 1) * SSD_STATE]
        scores = lax.dot_general(cg, bg, (((1,), (1,)), ((), ())), preferred_element_type=F32)
        heads_per_group = SSD_HEADS // SSD_GROUPS
        for pair in range(heads_per_group // 2):
            h0 = g * heads_per_group + 2 * pair
            xs_pair = xs[:, h0 * SSD_HEAD_DIM:(h0 + 2) * SSD_HEAD_DIM]
            y_pair = None
            for k in range(2):
                hh = off + h0 + k
                seg = cs[:, hh:hh + 1] - cs_t[hh:hh + 1, :]
                dec = jnp.exp(jnp.where(keep, seg, -jnp.inf))
                m = (scores * dec * dt_t[hh:hh + 1, :]).astype(BF16)
                rhs = jnp.where(first_half if k == 0 else jnp.logical_not(first_half), xs_pair, zero)
                part = jnp.dot(m, rhs, preferred_element_type=F32)
                y_pair = part if y_pair is None else y_pair + part
            pieces.append(y_pair)
    y_diag = jnp.concatenate(pieces, axis=1)

    y_off = []
    new_state = []
    for g in range(SSD_GROUPS):
        gcols = slice(g * GROUP_WIDTH, (g + 1) * GROUP_WIDTH)
        bg = act[:, SSD_WIDTH + g * SSD_STATE:SSD_WIDTH + (g + 1) * SSD_STATE]
        cg = act[:, SSD_WIDTH + SSD_GROUPS * SSD_STATE + g * SSD_STATE:
                 SSD_WIDTH + SSD_GROUPS * SSD_STATE + (g + 1) * SSD_STATE]
        prev = state[:, gcols]
        y_off.append(jnp.dot(cg, prev.astype(BF16), preferred_element_type=F32))
        bg_t = bg.astype(F32).T.astype(BF16)
        new = jnp.dot(bg_t, xw[:, gcols], preferred_element_type=F32)
        new_state.append(prev * cd_x[:, gcols] + new)
    y = y_diag + jnp.concatenate(y_off, axis=1) * into_x
    return y, jnp.concatenate(new_state, axis=1)


def _ssd_body(xbc_ref, xprev_ref, xnext_ref, dt_ref, z_ref, ya_ref, x_ref, gate_ref, cw_ref, cb_ref,
              dtb_ref, alog_ref, dsk_ref, ng_ref, wout_ref, exp_ref, o_ref,
              act_scr, dts_scr, yf_scr, st_scr, ext_scr, *, n_steps):
    d = pl.program_id(1)
    c = pl.program_id(2)
    lane1 = lax.broadcasted_iota(jnp.int32, (1, LANES), 1)
    a_row = jnp.where(lane1 < 2 * SSD_HEADS, -jnp.exp(alog_ref[...]), 0.0)
    chunk_rows = [slice(j * CHUNK, (j + 1) * CHUNK) for j in range(SSD_STEP_ROWS // CHUNK)]

    @pl.when(c == 0)
    def _():
        st_scr[...] = jnp.zeros_like(st_scr)

    def scan(act, dtv, expand, rev):
        state = st_scr[...]
        ys = [None] * len(chunk_rows)
        for j in (reversed(range(len(chunk_rows))) if rev else range(len(chunk_rows))):
            ys[j], state = _ssd_chunk(act[chunk_rows[j], :], dtv[chunk_rows[j], :], a_row, expand, state, rev)
        st_scr[...] = state
        return jnp.concatenate(ys, axis=0)

    @pl.when(d == 0)
    def _forward():
        row0 = pl.multiple_of(c * SSD_STEP_ROWS, SSD_STEP_ROWS)
        ext_scr[0:SUBLANES, :] = jnp.where(c > 0, xprev_ref[...], 0.0)
        ext_scr[SUBLANES:SUBLANES + SSD_STEP_ROWS, :] = xbc_ref[...]
        ext_scr[SUBLANES + SSD_STEP_ROWS:, :] = jnp.where(c < n_steps - 1, xnext_ref[...], 0.0)
        acc = cb_ref[...]
        for k in range(CONV_WIDTH):
            lo = SUBLANES - CONV_WIDTH // 2 + k
            acc = acc + cw_ref[k:k + 1, :] * ext_scr[lo:lo + SSD_STEP_ROWS, :]
        act = _silu(acc).astype(BF16)
        dtv = _softplus(dt_ref[...] + dtb_ref[...])
        act_scr[pl.ds(row0, SSD_STEP_ROWS), :] = act
        dts_scr[pl.ds(row0, SSD_STEP_ROWS), :] = dtv
        yf_scr[pl.ds(row0, SSD_STEP_ROWS), :] = scan(act, dtv, exp_ref[0], rev=False).astype(BF16)

    @pl.when(d == 1)
    def _backward():
        row0 = pl.multiple_of((n_steps - 1 - c) * SSD_STEP_ROWS, SSD_STEP_ROWS)
        act = act_scr[pl.ds(row0, SSD_STEP_ROWS), :]
        dtv = dts_scr[pl.ds(row0, SSD_STEP_ROWS), :]
        yb = scan(act, dtv, exp_ref[1], rev=True)
        xs = act[:, :SSD_WIDTH].astype(F32)
        y = yf_scr[pl.ds(row0, SSD_STEP_ROWS), :].astype(F32) + yb + dsk_ref[...] * xs
        y = y * _silu(z_ref[...].astype(F32))
        y = jnp.concatenate(
            [_rms(y[:, g * GROUP_WIDTH:(g + 1) * GROUP_WIDTH]) for g in range(SSD_GROUPS)], axis=1)
        y = y * ng_ref[...]
        mix = jnp.concatenate([ya_ref[...], y.astype(BF16)], axis=1)
        o = jnp.dot(mix, wout_ref[...], preferred_element_type=F32)
        o_ref[...] = x_ref[...] + gate_ref[...] * o


def _ssd_outproj(xbc, dt, z, ya, x2, mod3, conv_w8, conv_b, dtb_row, alog_row, dsk_row, norm_g, w_out_b,
                 expand, bn, seq):
    t = x2.shape[0]
    nc = seq // SSD_STEP_ROWS
    blocks8 = SSD_STEP_ROWS // SUBLANES
    last8 = t // SUBLANES - 1

    def fwd_chunk(b, d, c):
        return b * nc + c + d * (nc - 1 - c)

    def bwd_chunk(b, d, c):
        return b * nc + nc - 1 - d * c

    const2 = lambda b, d, c: (0, 0)
    return pl.pallas_call(
        functools.partial(_ssd_body, n_steps=nc),
        grid=(bn, 2, nc),
        in_specs=[
            pl.BlockSpec((SSD_STEP_ROWS, CONV_CH), lambda b, d, c: (fwd_chunk(b, d, c), 0)),
            pl.BlockSpec((SUBLANES, CONV_CH),
                         lambda b, d, c: (jnp.maximum(fwd_chunk(b, d, c) * blocks8 - 1, 0), 0)),
            pl.BlockSpec((SUBLANES, CONV_CH),
                         lambda b, d, c: (jnp.minimum((fwd_chunk(b, d, c) + 1) * blocks8, last8), 0)),
            pl.BlockSpec((SSD_STEP_ROWS, LANES), lambda b, d, c: (fwd_chunk(b, d, c), 0)),
            pl.BlockSpec((SSD_STEP_ROWS, SSD_WIDTH), lambda b, d, c: (bwd_chunk(b, d, c), 0)),
            pl.BlockSpec((SSD_STEP_ROWS, GMLP_WIDTH), lambda b, d, c: (bwd_chunk(b, d, c), 0)),
            pl.BlockSpec((SSD_STEP_ROWS, D_MODEL), lambda b, d, c: (bwd_chunk(b, d, c), 0)),
            pl.BlockSpec((None, 1, D_MODEL), lambda b, d, c: (b, 0, 2)),
            pl.BlockSpec((SUBLANES, CONV_CH), const2),
            pl.BlockSpec((1, CONV_CH), const2),
            pl.BlockSpec((1, LANES), const2),
            pl.BlockSpec((1, LANES), const2),
            pl.BlockSpec((1, SSD_WIDTH), const2),
            pl.BlockSpec((1, SSD_WIDTH), const2),
            pl.BlockSpec((GMLP_WIDTH + SSD_WIDTH, D_MODEL), const2, pipeline_mode=pl.Buffered(1)),
            pl.BlockSpec((2, LANES, SSD_WIDTH), lambda b, d, c: (0, 0, 0), pipeline_mode=pl.Buffered(1)),
        ],
        out_specs=pl.BlockSpec((SSD_STEP_ROWS, D_MODEL), lambda b, d, c: (bwd_chunk(b, d, c), 0)),
        out_shape=jax.ShapeDtypeStruct((t, D_MODEL), F32),
        scratch_shapes=[
            pltpu.VMEM((seq, CONV_CH), BF16),
            pltpu.VMEM((seq, LANES), F32),
            pltpu.VMEM((seq, SSD_WIDTH), BF16),
            pltpu.VMEM((SSD_STATE, SSD_WIDTH), F32),
            pltpu.VMEM((SSD_STEP_ROWS + 2 * SUBLANES, CONV_CH), F32),
        ],
        compiler_params=pltpu.CompilerParams(
            dimension_semantics=("arbitrary", "arbitrary", "arbitrary"), vmem_limit_bytes=VMEM_LIMIT),
        name="ssd_outproj",
    )(xbc, xbc, xbc, dt, z, ya, x2, mod3, conv_w8, conv_b, dtb_row, alog_row, dsk_row, norm_g, w_out_b,
      expand)


def _route(logits):
    lane = lax.broadcasted_iota(jnp.int32, logits.shape, 1)
    big = jnp.int32(LANES)
    neg = -jnp.inf
    gmask = (lane >= N_EXPERTS) & (lane < N_EXPERTS + N_EGROUPS)
    gl = jnp.where(gmask, logits, neg)
    gmax = jnp.max(gl, axis=-1, keepdims=True)
    gidx = jnp.min(jnp.where(gl == gmax, lane, big), axis=-1, keepdims=True) - N_EXPERTS
    p_g = 1.0 / jnp.sum(jnp.where(gmask, jnp.exp(gl - gmax), 0.0), axis=-1, keepdims=True)
    lo = gidx * EXPERTS_PER_GROUP
    emask = (lane >= lo) & (lane < lo + EXPERTS_PER_GROUP)
    el = jnp.where(emask, logits, neg)
    v1 = jnp.max(el, axis=-1, keepdims=True)
    i1 = jnp.min(jnp.where(el == v1, lane, big), axis=-1, keepdims=True)
    el2 = jnp.where(lane == i1, neg, el)
    v2 = jnp.max(el2, axis=-1, keepdims=True)
    i2 = jnp.min(jnp.where(el2 == v2, lane, big), axis=-1, keepdims=True)
    e2 = jnp.exp(v2 - v1)
    den = 1.0 + e2
    w1 = p_g / den
    w2 = p_g * e2 / den
    return i1, i2, w1, w2


def _router_body(x_ref, shift_ref, scale_ref, g2_ref, wr_ref, br_ref, h_ref, rt_ref, cnt_ref):
    counts = jnp.zeros((1, LANES), F32)
    for part in range(TM_MOE // ROUTER_ROWS):
        rows = slice(part * ROUTER_ROWS, (part + 1) * ROUTER_ROWS)
        h = _rms(x_ref[rows, :]) * g2_ref[...]
        h = h * (1.0 + scale_ref[...]) + shift_ref[...]
        hb = h.astype(BF16)
        h_ref[rows, :] = hb
        h_lo = (h - hb.astype(F32)).astype(BF16)
        both = jnp.dot(hb, wr_ref[...], preferred_element_type=F32)
        logits = (both[:, :LANES] + both[:, LANES:]
                  + jnp.dot(h_lo, wr_ref[:, :LANES], preferred_element_type=F32)) + br_ref[...]
        i1, i2, w1, w2 = _route(logits)
        lane = lax.broadcasted_iota(jnp.int32, (ROUTER_ROWS, LANES), 1)
        rt_ref[rows, :] = jnp.where(lane == 0, i1.astype(F32), jnp.where(lane == 1, i2.astype(F32),
                                    jnp.where(lane == 2, w1, jnp.where(lane == 3, w2, 0.0))))
        mask = jnp.where(lane == i1, 1.0, jnp.where(lane == i2, 1.0, 0.0))
        counts = counts + jnp.sum(mask, axis=0, keepdims=True)
    cnt_ref[...] = jnp.broadcast_to(counts, (SUBLANES, LANES))


def _router(x1, mod3, norm2_g, w_router, b_router, seq):
    t = x1.shape[0]
    tiles_per_seq = seq // TM_MOE
    n_tiles = t // TM_MOE
    row = lambda i: (i, 0)
    const2 = lambda i: (0, 0)
    return pl.pallas_call(
        _router_body,
        grid=(n_tiles,),
        in_specs=[
            pl.BlockSpec((TM_MOE, D_MODEL), row),
            pl.BlockSpec((None, 1, D_MODEL), lambda i: (i // tiles_per_seq, 0, 3)),
            pl.BlockSpec((None, 1, D_MODEL), lambda i: (i // tiles_per_seq, 0, 4)),
            pl.BlockSpec((1, D_MODEL), const2),
            pl.BlockSpec((D_MODEL, 2 * LANES), const2),
            pl.BlockSpec((1, LANES), const2),
        ],
        out_specs=[
            pl.BlockSpec((TM_MOE, D_MODEL), row),
            pl.BlockSpec((TM_MOE, LANES), row),
            pl.BlockSpec((SUBLANES, LANES), row),
        ],
        out_shape=[
            jax.ShapeDtypeStruct((t, D_MODEL), BF16),
            jax.ShapeDtypeStruct((t, LANES), F32),
            jax.ShapeDtypeStruct((n_tiles * SUBLANES, LANES), F32),
        ],
        compiler_params=pltpu.CompilerParams(
            dimension_semantics=("arbitrary",), vmem_limit_bytes=VMEM_LIMIT),
        name="moe_router",
    )(x1, mod3, mod3, norm2_g, w_router, b_router)


def _row_copy(vmem_buf, v_start, hbm_ref, g_start, size, sem, to_hbm):
    v = vmem_buf.at[pl.ds(v_start, size), :]
    g = hbm_ref.at[pl.ds(g_start, size), :]
    return pltpu.make_async_copy(v, g, sem) if to_hbm else pltpu.make_async_copy(g, v, sem)


def _copy_rows(n, vmem_buf, v0, hbm_ref, g0, sem, *, to_hbm, wait):
    def run(v_off, g_off, size):
        v_start = 0 if v0 is None else pl.multiple_of(v0 + v_off, SEG_ALIGN)
        cp = _row_copy(vmem_buf, v_start, hbm_ref, pl.multiple_of(g0 + g_off, SEG_ALIGN), size, sem, to_hbm)
        if wait:
            cp.wait()
        else:
            cp.start()

    n_big = n // PIECE_ROWS

    def big_piece(k, carry):
        run(k * PIECE_ROWS, k * PIECE_ROWS, PIECE_ROWS)
        return carry

    lax.fori_loop(0, n_big, big_piece, 0)
    base = n_big * PIECE_ROWS
    rem = n - base
    for size in REM_SIZES:
        done = base + (rem // (2 * size)) * (2 * size)

        @pl.when((rem & size) != 0)
        def _():
            run(done, done, size)


def _segment_copies(seg_ref, loc_ref, goff_ref, tile, vmem_buf, hbm_ref, sem, *, to_hbm):
    base = tile * N_EXPERTS

    def per_expert(e, carry):
        _copy_rows(seg_ref[base + e], vmem_buf, loc_ref[base + e], hbm_ref, goff_ref[base + e], sem,
                   to_hbm=to_hbm, wait=False)
        return carry

    lax.fori_loop(0, N_EXPERTS, per_expert, 0)


def _tile_rows(seg_ref, loc_ref, tile):
    last = tile * N_EXPERTS + N_EXPERTS - 1
    return loc_ref[last] + seg_ref[last]


def _segment_wait(seg_ref, loc_ref, tile, vmem_buf, hbm_ref, sem, *, to_hbm):
    total = _tile_rows(seg_ref, loc_ref, tile)
    for size in TOTAL_SIZES:
        @pl.when((total & size) != 0)
        def _():
            _row_copy(vmem_buf, 0, hbm_ref, 0, size, sem, to_hbm).wait()


def _dispatch_body(seg_ref, loc_ref, goff_ref, fill_ref, h_ref, rt_ref, locrow_ref, pos_ref, xs_hbm, buf, sem,
                   *, n_tiles, n_row_tiles):
    i = pl.program_id(0)
    slot = i % 2

    start = functools.partial(_segment_copies, seg_ref, loc_ref, goff_ref, hbm_ref=xs_hbm, to_hbm=True)
    wait = functools.partial(_segment_wait, seg_ref, loc_ref, hbm_ref=xs_hbm, to_hbm=True)

    @pl.when(i >= 2)
    def _():
        wait(i - 2, buf.at[slot], sem=sem.at[slot])

    rt = rt_ref[...]
    lane = lax.broadcasted_iota(jnp.int32, (TM_MOE, LANES), 1)
    lanef = lane.astype(F32)
    e1 = rt[:, 0:1]
    e2 = rt[:, 1:2]
    mask = jnp.where(lanef == e1, 1.0, jnp.where(lanef == e2, 1.0, 0.0))
    r = lax.broadcasted_iota(jnp.int32, (TM_MOE, TM_MOE), 0)
    cc = lax.broadcasted_iota(jnp.int32, (TM_MOE, TM_MOE), 1)
    strict = jnp.where(r > cc, 1.0, 0.0).astype(BF16)
    rank = jnp.dot(strict, mask.astype(BF16), preferred_element_type=F32)
    posall = rank + locrow_ref[...]
    pos1 = jnp.sum(jnp.where(lanef == e1, posall, 0.0), axis=-1, keepdims=True)
    pos2 = jnp.sum(jnp.where(lanef == e2, posall, 0.0), axis=-1, keepdims=True)
    pp = jnp.where(lane == 0, pos1, jnp.where(lane == 1, pos2, rt))
    pos_ref[...] = pp

    slots_t = [pp[k * LANES:(k + 1) * LANES, :].T for k in range(TM_MOE // LANES)]

    rowp = lax.broadcasted_iota(jnp.int32, (R_LOC, LANES), 0).astype(F32)
    perm = jnp.concatenate(
        [jnp.where(rowp == st[0:1, :], 1.0, jnp.where(rowp == st[1:2, :], 1.0, 0.0)).astype(BF16)
         for st in slots_t], axis=1)
    buf[slot] = jnp.dot(perm, h_ref[...], preferred_element_type=F32).astype(BF16)
    start(i, buf.at[slot], sem=sem.at[slot])

    @pl.when(i == n_tiles - 1)
    def _():
        if n_tiles > 1:
            wait(i - 1, buf.at[1 - slot], sem=sem.at[1 - slot])
        wait(i, buf.at[slot], sem=sem.at[slot])
        zsrc = buf.at[1 - slot]
        zsem = sem.at[1 - slot]
        zsrc[0:TR_EXPERT, :] = jnp.zeros((TR_EXPERT, D_MODEL), BF16)
        for waiting in (False, True):
            def per_expert(e, carry):
                _copy_rows(fill_ref[N_EXPERTS + e], zsrc, None, xs_hbm, fill_ref[e], zsem,
                           to_hbm=True, wait=waiting)
                return carry

            def per_row_tile(k, carry):
                cp = _row_copy(zsrc, 0, xs_hbm, pl.multiple_of(k * TR_EXPERT, TR_EXPERT), TR_EXPERT, zsem, True)
                if waiting:
                    cp.wait()
                else:
                    cp.start()
                return carry

            lax.fori_loop(0, N_EXPERTS, per_expert, 0)
            lax.fori_loop(fill_ref[2 * N_EXPERTS], n_row_tiles, per_row_tile, 0)


def _dispatch(h2, rt, seg, loc, goff, fill, locrow, n_row_tiles):
    t = h2.shape[0]
    n_tiles = t // TM_MOE
    row = lambda i, *_: (i, 0)
    grid_spec = pltpu.PrefetchScalarGridSpec(
        num_scalar_prefetch=4,
        grid=(n_tiles,),
        in_specs=[
            pl.BlockSpec((TM_MOE, D_MODEL), row),
            pl.BlockSpec((TM_MOE, LANES), row),
            pl.BlockSpec((None, 1, LANES), lambda i, *_: (i, 0, 0)),
        ],
        out_specs=[
            pl.BlockSpec((TM_MOE, LANES), row),
            pl.BlockSpec(memory_space=pl.ANY),
        ],
        scratch_shapes=[
            pltpu.VMEM((2, R_LOC, D_MODEL), BF16),
            pltpu.SemaphoreType.DMA((2,)),
        ],
    )
    return pl.pallas_call(
        functools.partial(_dispatch_body, n_tiles=n_tiles, n_row_tiles=n_row_tiles),
        grid_spec=grid_spec,
        out_shape=[
            jax.ShapeDtypeStruct((t, LANES), F32),
            jax.ShapeDtypeStruct((n_row_tiles * TR_EXPERT, D_MODEL), BF16),
        ],
        compiler_params=pltpu.CompilerParams(
            dimension_semantics=("arbitrary",), vmem_limit_bytes=VMEM_LIMIT),
        name="moe_dispatch",
    )(seg, loc, goff, fill, h2, rt, locrow)


def _expert_body(te_ref, tw_ref, na_ref, xs_ref, w1a_ref, w3a_ref, w2a_ref, w1b_ref, w3b_ref, w2b_ref, y_ref):
    del tw_ref
    r = pl.program_id(0)
    n_active = na_ref[0]
    weights = ((w1a_ref, w3a_ref, w2a_ref), (w1b_ref, w3b_ref, w2b_ref))

    def swiglu(half, rows=None):
        w1_ref, w3_ref, w2_ref = weights[half]
        rows = slice(half * TR_EXPERT, (half + 1) * TR_EXPERT) if rows is None else rows
        x = xs_ref[rows, :]
        a = _silu(jnp.dot(x, w1_ref[...], preferred_element_type=F32))
        a = a * jnp.dot(x, w3_ref[...], preferred_element_type=F32)
        y_ref[rows, :] = jnp.dot(a.astype(BF16), w2_ref[...], preferred_element_type=F32).astype(BF16)

    first = EXPERT_TILES_PER_STEP * r
    both = first + 1 < n_active
    same = te_ref[first] == te_ref[first + 1]

    @pl.when(both & same)
    def _():
        swiglu(0, slice(0, EXPERT_TILES_PER_STEP * TR_EXPERT))

    @pl.when(both & jnp.logical_not(same))
    def _():
        swiglu(0)
        swiglu(1)

    @pl.when(first + 1 == n_active)
    def _():
        swiglu(0)
        y_ref[TR_EXPERT:, :] = jnp.zeros((TR_EXPERT, D_MODEL), BF16)

    @pl.when(first >= n_active)
    def _():
        y_ref[...] = jnp.zeros_like(y_ref)


def _experts(xs, tile_expert, tile_weights, n_active, w1b, w3b, w2b):
    n_rows = xs.shape[0]
    step_rows = EXPERT_TILES_PER_STEP * TR_EXPERT
    x_block = lambda r, te, tw, na: (jnp.minimum(r, (na[0] - 1) // EXPERT_TILES_PER_STEP), 0)
    w_specs = []
    for half in range(EXPERT_TILES_PER_STEP):
        w_block = lambda r, te, tw, na, half=half: (tw[EXPERT_TILES_PER_STEP * r + half], 0, 0)
        w_specs += [pl.BlockSpec((None, D_MODEL, D_EXPERT), w_block),
                    pl.BlockSpec((None, D_MODEL, D_EXPERT), w_block),
                    pl.BlockSpec((None, D_EXPERT, D_MODEL), w_block)]
    grid_spec = pltpu.PrefetchScalarGridSpec(
        num_scalar_prefetch=3,
        grid=(n_rows // step_rows,),
        in_specs=[pl.BlockSpec((step_rows, D_MODEL), x_block)] + w_specs,
        out_specs=pl.BlockSpec((step_rows, D_MODEL), lambda r, te, tw, na: (r, 0)),
    )
    return pl.pallas_call(
        _expert_body,
        grid_spec=grid_spec,
        out_shape=jax.ShapeDtypeStruct((n_rows, D_MODEL), BF16),
        compiler_params=pltpu.CompilerParams(
            dimension_semantics=("arbitrary",), vmem_limit_bytes=VMEM_LIMIT),
        name="moe_experts",
    )(tile_expert, tile_weights, n_active, xs, w1b, w3b, w2b, w1b, w3b, w2b)


def _combine_body(seg_ref, loc_ref, goff_ref, y_hbm, pos_ref, x_ref, gate_ref, fg_ref, o_ref, buf, sem, *,
                  n_tiles, final_norm):
    i = pl.program_id(0)
    slot = i % 2
    start = functools.partial(_segment_copies, seg_ref, loc_ref, goff_ref, hbm_ref=y_hbm, to_hbm=False)

    @pl.when(i == 0)
    def _():
        buf[...] = jnp.zeros_like(buf)
        start(0, buf.at[0], sem=sem.at[0])

    @pl.when(i + 1 < n_tiles)
    def _():
        start(i + 1, buf.at[1 - slot], sem=sem.at[1 - slot])

    _segment_wait(seg_ref, loc_ref, i, buf.at[slot], y_hbm, sem.at[slot], to_hbm=False)
    for part in range(TM_MOE // COMBINE_ROWS):
        rows = slice(part * COMBINE_ROWS, (part + 1) * COMBINE_ROWS)
        pp = pos_ref[rows, :]
        colp = lax.broadcasted_iota(jnp.int32, (COMBINE_ROWS, R_LOC), 1).astype(F32)
        wc = jnp.where(colp == pp[:, 0:1], pp[:, 2:3], jnp.where(colp == pp[:, 1:2], pp[:, 3:4], 0.0))
        moe = jnp.dot(wc.astype(BF16), buf[slot], preferred_element_type=F32)
        x2 = x_ref[rows, :] + gate_ref[...] * moe
        o_ref[rows, :] = _rms(x2) * fg_ref[...] if final_norm else x2


def _combine(y, pos, x1, mod3, final_g, seg, loc, goff, seq, final_norm):
    t = x1.shape[0]
    n_tiles = t // TM_MOE
    tiles_per_seq = seq // TM_MOE
    row = lambda i, *_: (i, 0)
    grid_spec = pltpu.PrefetchScalarGridSpec(
        num_scalar_prefetch=3,
        grid=(n_tiles,),
        in_specs=[
            pl.BlockSpec(memory_space=pl.ANY),
            pl.BlockSpec((TM_MOE, LANES), row),
            pl.BlockSpec((TM_MOE, D_MODEL), row),
            pl.BlockSpec((None, 1, D_MODEL), lambda i, *_: (i // tiles_per_seq, 0, 5)),
            pl.BlockSpec((1, D_MODEL), lambda i, *_: (0, 0)),
        ],
        out_specs=pl.BlockSpec((TM_MOE, D_MODEL), row),
        scratch_shapes=[
            pltpu.VMEM((2, R_LOC, D_MODEL), BF16),
            pltpu.SemaphoreType.DMA((2,)),
        ],
    )
    return pl.pallas_call(
        functools.partial(_combine_body, n_tiles=n_tiles, final_norm=final_norm),
        grid_spec=grid_spec,
        out_shape=jax.ShapeDtypeStruct((t, D_MODEL), F32),
        compiler_params=pltpu.CompilerParams(
            dimension_semantics=("arbitrary",), vmem_limit_bytes=VMEM_LIMIT),
        name="moe_combine",
    )(seg, loc, goff, y, pos, x1, mod3, final_g)


def _moe_plan(counts, n_tiles, n_row_tiles):
    cnt = counts.reshape(n_tiles, SUBLANES, LANES)[:, 0, :N_EXPERTS].astype(jnp.int32)
    seg = (cnt + SEG_ALIGN - 1) // SEG_ALIGN * SEG_ALIGN
    before_e = jnp.arange(N_EXPERTS)[:, None] < jnp.arange(N_EXPERTS)[None, :]
    before_t = jnp.arange(n_tiles)[:, None] > jnp.arange(n_tiles)[None, :]
    loc = jnp.sum(jnp.where(before_e[None], seg[:, :, None], 0), axis=1)
    tot = jnp.sum(seg, axis=0)
    region = (tot + TR_EXPERT - 1) // TR_EXPERT * TR_EXPERT
    gstart = jnp.sum(jnp.where(before_e, region[:, None], 0), axis=0)
    gend = gstart + region
    goff = gstart[None, :] + jnp.sum(jnp.where(before_t[:, :, None], seg[None], 0), axis=1)
    n_active = gend[-1] // TR_EXPERT
    tile_row0 = jnp.arange(n_row_tiles, dtype=jnp.int32) * TR_EXPERT
    last_row0 = (n_active - 1) * TR_EXPERT
    te = jnp.sum(gend[None, :] <= jnp.minimum(tile_row0, last_row0)[:, None], axis=1).astype(jnp.int32)
    te = jnp.minimum(te, N_EXPERTS - 1)
    te2 = te.reshape(-1, EXPERT_TILES_PER_STEP)
    cand = jnp.where(te2[:, 1] != te2[:, 0], te2[:, 1], 0)
    steps = jnp.arange(te2.shape[0])
    held = jnp.max(jnp.where(steps[None, :] <= steps[:, None], cand[None, :], 0), axis=1)
    tw = jnp.stack([te2[:, 0], held], axis=1).reshape(-1).astype(jnp.int32)
    locrow = _pad_cols(loc.astype(F32), LANES).reshape(n_tiles, 1, LANES)
    n_active = n_active.reshape(1).astype(jnp.int32)
    fill = jnp.concatenate([gstart + tot, region - tot, n_active]).astype(jnp.int32)
    return (seg.reshape(-1), loc.reshape(-1).astype(jnp.int32), goff.reshape(-1).astype(jnp.int32), fill,
            locrow, te, tw, n_active)


def _moe_final(x1, mod3, norm2_g, w_router, b_router, w1b, w3b, w2b, final_g, seq, final_norm):
    t = x1.shape[0]
    n_tiles = t // TM_MOE
    max_rows = N_ASSIGN * t + (SEG_ALIGN - 1) * N_EXPERTS * n_tiles + N_EXPERTS * (TR_EXPERT - SEG_ALIGN)
    step_rows = EXPERT_TILES_PER_STEP * TR_EXPERT
    n_row_tiles = -(-max_rows // step_rows) * EXPERT_TILES_PER_STEP
    h2, rt, counts = _router(x1, mod3, norm2_g, w_router, b_router, seq)
    seg, loc, goff, fill, locrow, te, tw, n_active = _moe_plan(counts, n_tiles, n_row_tiles)
    pos, xs = _dispatch(h2, rt, seg, loc, goff, fill, locrow, n_row_tiles)
    y = _experts(xs, te, tw, n_active, w1b, w3b, w2b)
    return _combine(y, pos, x1, mod3, final_g, seg, loc, goff, seq, final_norm)


def _pad_cols(a, width):
    return jnp.pad(a, ((0, 0), (0, width - a.shape[1])))


def kernel(x, c, w_ada, b_ada, norm1_g, w_in, b_in, gmlp_ln_g, gmlp_ln_b, gmlp_w_s, gmlp_b_s, gmlp_out_g, conv_w, conv_b, a_log_f, a_log_b, dt_bias_f, dt_bias_b, d_skip, ssd_norm_g, w_out, norm2_g, w_router_g, b_router_g, w_router_e, b_router_e, w1, w3, w2, final_g):
    bn, seq, _ = x.shape
    depth = w_ada.shape[0]
    x2 = x.reshape(bn * seq, D_MODEL)

    head_rows = jnp.arange(LANES)[:, None]
    head_cols = jnp.arange(SSD_WIDTH)[None, :] // SSD_HEAD_DIM
    expand = jnp.stack([head_rows == head_cols, head_rows == head_cols + SSD_HEADS]).astype(BF16)

    for l in range(depth):
        mod3 = _modulation(c, w_ada, b_ada[l], l).reshape(bn, 1, N_MOD * D_MODEL)

        w_in_p = _pad_cols(w_in[l].astype(BF16), IN_COLS_PAD)
        b_in_p = _pad_cols(b_in[l][None, :], IN_COLS_PAD)
        bs = jnp.repeat(gmlp_b_s[l].T, LANES, axis=1)
        bs_tile = jnp.tile(bs, (TM_PROJ // CHUNK, 1))
        ya, z, xbc, dt = _inproj_gmlp(
            x2, mod3, norm1_g[l][None, :], w_in_p, b_in_p, gmlp_ln_g[l][None, :], gmlp_ln_b[l][None, :],
            gmlp_w_s[l].astype(BF16), bs_tile, gmlp_out_g[l][None, :], seq)

        conv_w8 = jnp.pad(conv_w[l], ((0, SUBLANES - CONV_WIDTH), (0, 0)))
        dtb_row = _pad_cols(jnp.concatenate([dt_bias_f[l], dt_bias_b[l]])[None, :], LANES)
        alog_row = _pad_cols(jnp.concatenate([a_log_f[l], a_log_b[l]])[None, :], LANES)
        dsk_row = jnp.repeat(d_skip[l], SSD_HEAD_DIM)[None, :]
        x2 = _ssd_outproj(xbc, dt, z, ya, x2, mod3, conv_w8, conv_b[l][None, :], dtb_row, alog_row, dsk_row,
                          ssd_norm_g[l][None, :], w_out[l].astype(BF16), expand, bn, seq)

        w_re = jnp.transpose(w_router_e[l], (1, 0, 2)).reshape(D_MODEL, N_EXPERTS)
        w_router = _pad_cols(jnp.concatenate([w_re, w_router_g[l]], axis=1), LANES)
        w_router_hi = w_router.astype(BF16)
        w_router = jnp.concatenate([w_router_hi, (w_router - w_router_hi.astype(F32)).astype(BF16)], axis=1)
        b_router = _pad_cols(jnp.concatenate([b_router_e[l].reshape(-1), b_router_g[l]])[None, :], LANES)
        x2 = _moe_final(x2, mod3, norm2_g[l][None, :], w_router, b_router, w1[l].astype(BF16),
                        w3[l].astype(BF16), w2[l].astype(BF16), final_g[None, :], seq,
                        final_norm=(l == depth - 1))
    return x2.reshape(bn, seq, D_MODEL)
```

```python
import functools
import math

import jax
import jax.numpy as jnp
from jax import lax
from jax.experimental import pallas as pl
from jax.experimental.pallas import tpu as pltpu

F32 = jnp.float32
BF16 = jnp.bfloat16
HIGHEST = lax.Precision.HIGHEST

D_MODEL = 1024
N_MOD = 6
GMLP_WIDTH = 1024
GMLP_HEADS = 8
CHUNK = 128
SSD_STEP_ROWS = 4 * CHUNK
SSD_WIDTH = 1024
SSD_HEADS = 16
SSD_HEAD_DIM = 64
SSD_GROUPS = 2
SSD_STATE = 128
GROUP_WIDTH = SSD_WIDTH // SSD_GROUPS
CONV_WIDTH = 5
CONV_CH = SSD_WIDTH + 2 * SSD_GROUPS * SSD_STATE
N_EGROUPS = 4
EXPERTS_PER_GROUP = 8
N_EXPERTS = 32
D_EXPERT = 256
EPS = 1e-6

LANES = 128
SUBLANES = 8
COL_U, COL_V, COL_Z, COL_XBC, COL_DT = 0, 1024, 2048, 3072, 4608
IN_COLS = 4640
IN_COLS_PAD = COL_DT + LANES
TM_PROJ = 1024
PROJ_ROWS = 512
COMBINE_ROWS = 256
ROUTER_ROWS = 256
TM_MOE = 512
TR_EXPERT = 512
EXPERT_TILES_PER_STEP = 2
N_ASSIGN = 2
SEG_ALIGN = 2 * SUBLANES
R_LOC = N_ASSIGN * TM_MOE + N_EXPERTS * SEG_ALIGN
PIECE_ROWS = 128
REM_SIZES = tuple(PIECE_ROWS >> s for s in range(1, PIECE_ROWS.bit_length()) if PIECE_ROWS >> s >= SEG_ALIGN)
TOTAL_SIZES = tuple(1 << s for s in range(R_LOC.bit_length() - 1, -1, -1) if 1 << s >= SEG_ALIGN)
VMEM_LIMIT = 56 * 1024 * 1024


def _silu(v):
    return v * jax.nn.sigmoid(v)


def _gelu(v):
    return 0.5 * v * (1.0 + lax.erf(v * math.sqrt(0.5)))


def _softplus(v):
    return jnp.maximum(v, 0.0) + jnp.log1p(jnp.exp(-jnp.abs(v)))


def _rms(v):
    return v * lax.rsqrt(jnp.mean(v * v, axis=-1, keepdims=True) + EPS)


def _mod_body(c_ref, w_ref, b_ref, o_ref):
    ca = _silu(c_ref[...])
    o_ref[...] = jnp.dot(ca, w_ref[...], precision=HIGHEST, preferred_element_type=F32) + b_ref[...]


def _modulation(c, w_ada, b_ada, layer):
    bn = c.shape[0]
    return pl.pallas_call(
        _mod_body,
        grid=(N_MOD,),
        in_specs=[
            pl.BlockSpec((bn, D_MODEL), lambda j: (0, 0)),
            pl.BlockSpec((None, D_MODEL, D_MODEL), lambda j: (layer, 0, j)),
            pl.BlockSpec((1, D_MODEL), lambda j: (0, j)),
        ],
        out_specs=pl.BlockSpec((bn, D_MODEL), lambda j: (0, j)),
        out_shape=jax.ShapeDtypeStruct((bn, N_MOD * D_MODEL), F32),
        name="adaln_mod",
    )(c, w_ada, b_ada.reshape(1, -1))


def _cast_pad_body(w_ref, o_ref):
    o_ref[:, :IN_COLS] = w_ref[...].astype(BF16)
    o_ref[:, IN_COLS:] = jnp.zeros((o_ref.shape[0], IN_COLS_PAD - IN_COLS), BF16)


def _cast_pad_in_weights(w_in, layer):
    rows = D_MODEL // 8
    return pl.pallas_call(
        _cast_pad_body,
        grid=(D_MODEL // rows,),
        in_specs=[pl.BlockSpec((None, rows, IN_COLS), lambda i: (layer, i, 0))],
        out_specs=pl.BlockSpec((rows, IN_COLS_PAD), lambda i: (i, 0)),
        out_shape=jax.ShapeDtypeStruct((D_MODEL, IN_COLS_PAD), BF16),
        name="cast_in_weights",
    )(w_in)


def _inproj_body(x_ref, shift_ref, scale_ref, g_ref, w_ref, b_ref, lng_ref, lnb_ref, ws_ref, bs_ref,
                 og_ref, ya_ref, z_ref, xbc_ref, dt_ref, mix_scr):
    for part in range(TM_PROJ // PROJ_ROWS):
        rows = slice(part * PROJ_ROWS, (part + 1) * PROJ_ROWS)
        h = _rms(x_ref[rows, :]) * g_ref[...]
        h = h * (1.0 + scale_ref[...]) + shift_ref[...]
        hb = h.astype(BF16)

        def proj(lo, hi, hb=hb):
            return jnp.dot(hb, w_ref[:, lo:hi], preferred_element_type=F32) + b_ref[:, lo:hi]

        z_ref[rows, :] = proj(COL_Z, COL_XBC).astype(BF16)
        xbc_ref[rows, :] = proj(COL_XBC, COL_DT)
        dt_ref[rows, :] = proj(COL_DT, IN_COLS_PAD)

        v = _gelu(proj(COL_V, COL_Z))
        mu = jnp.mean(v, axis=-1, keepdims=True)
        vc = v - mu
        var = jnp.mean(vc * vc, axis=-1, keepdims=True)
        vn = (vc * lax.rsqrt(var + EPS) * lng_ref[...] + lnb_ref[...]).astype(BF16)
        n_chunks = PROJ_ROWS // CHUNK
        for hd in range(GMLP_HEADS):
            cols = slice(hd * LANES, (hd + 1) * LANES)
            rhs = jnp.concatenate([vn[c * CHUNK:(c + 1) * CHUNK, cols] for c in range(n_chunks)], axis=1)
            res = jnp.dot(ws_ref[hd], rhs, preferred_element_type=F32)
            for c in range(n_chunks):
                mix_scr[part * PROJ_ROWS + c * CHUNK:part * PROJ_ROWS + (c + 1) * CHUNK, cols] = (
                    res[:, c * LANES:(c + 1) * LANES])
        u = _gelu(proj(COL_U, COL_V))
        out = u * (mix_scr[rows, :] + bs_ref[rows, :])
        ya_ref[rows, :] = (_rms(out) * og_ref[...]).astype(BF16)


def _inproj_gmlp(x2, mod3, norm1_g, w_in_p, b_in_p, ln_g, ln_b, w_s, bs_tile, out_g, seq):
    t = x2.shape[0]
    tiles_per_seq = seq // TM_PROJ
    row = lambda i: (i, 0)
    const2 = lambda i: (0, 0)
    return pl.pallas_call(
        _inproj_body,
        grid=(t // TM_PROJ,),
        in_specs=[
            pl.BlockSpec((TM_PROJ, D_MODEL), row),
            pl.BlockSpec((None, 1, D_MODEL), lambda i: (i // tiles_per_seq, 0, 0)),
            pl.BlockSpec((None, 1, D_MODEL), lambda i: (i // tiles_per_seq, 0, 1)),
            pl.BlockSpec((1, D_MODEL), const2),
            pl.BlockSpec((D_MODEL, IN_COLS_PAD), const2, pipeline_mode=pl.Buffered(1)),
            pl.BlockSpec((1, IN_COLS_PAD), const2),
            pl.BlockSpec((1, GMLP_WIDTH), const2),
            pl.BlockSpec((1, GMLP_WIDTH), const2),
            pl.BlockSpec((GMLP_HEADS, CHUNK, CHUNK), lambda i: (0, 0, 0)),
            pl.BlockSpec((TM_PROJ, GMLP_WIDTH), const2, pipeline_mode=pl.Buffered(1)),
            pl.BlockSpec((1, GMLP_WIDTH), const2),
        ],
        out_specs=[
            pl.BlockSpec((TM_PROJ, GMLP_WIDTH), row),
            pl.BlockSpec((TM_PROJ, SSD_WIDTH), row),
            pl.BlockSpec((TM_PROJ, CONV_CH), row),
            pl.BlockSpec((TM_PROJ, LANES), row),
        ],
        out_shape=[
            jax.ShapeDtypeStruct((t, GMLP_WIDTH), BF16),
            jax.ShapeDtypeStruct((t, SSD_WIDTH), BF16),
            jax.ShapeDtypeStruct((t, CONV_CH), F32),
            jax.ShapeDtypeStruct((t, LANES), F32),
        ],
        scratch_shapes=[pltpu.VMEM((TM_PROJ, GMLP_WIDTH), F32)],
        compiler_params=pltpu.CompilerParams(
            dimension_semantics=("arbitrary",), vmem_limit_bytes=VMEM_LIMIT),
        name="inproj_gmlp",
    )(x2, mod3, mod3, norm1_g, w_in_p, b_in_p, ln_g, ln_b, w_s, bs_tile, out_g)


def _ssd_chunk(act, dtv, a_row, expand, state, rev):
    off = SSD_HEADS if rev else 0
    row = lax.broadcasted_iota(jnp.int32, (CHUNK, CHUNK), 0)
    col = lax.broadcasted_iota(jnp.int32, (CHUNK, CHUNK), 1)
    lower = row >= col
    upper = row <= col
    keep = upper if rev else lower
    da = dtv * a_row
    cs = jnp.dot(keep.astype(F32), da, precision=HIGHEST, preferred_element_type=F32)
    cs_t = cs.T
    dt_t = dtv.T
    tot = cs[0:1, :] if rev else cs[CHUNK - 1:CHUNK, :]

    xs = act[:, :SSD_WIDTH]
    lane = lax.broadcasted_iota(jnp.int32, (CHUNK, LANES), 1)
    first_half = lane < SSD_HEAD_DIM
    zero = jnp.zeros((), BF16)

    stack = jnp.concatenate(
        [jnp.exp(cs), dtv * jnp.exp(tot - cs), jnp.broadcast_to(jnp.exp(tot), (SUBLANES, LANES))], axis=0)
    stack_x = jnp.dot(stack.astype(BF16), expand, preferred_element_type=F32)
    into_x = stack_x[:CHUNK]
    w_x = stack_x[CHUNK:2 * CHUNK]
    cd_x = stack_x[2 * CHUNK:2 * CHUNK + 1]
    xw = xs * w_x.astype(BF16)

    pieces = []
    for g in range(SSD_GROUPS):
        bg = act[:, SSD_WIDTH + g * SSD_STATE:SSD_WIDTH + (g + 1) * SSD_STATE]
        cg = act[:, SSD_WIDTH + SSD_GROUPS * SSD_STATE + g * SSD_STATE:
                 SSD_WIDTH + SSD_GROUPS * SSD_STATE + (g + 1) * SSD_STATE]
        scores = lax.dot_general(cg, bg, (((1,), (1,)), ((), ())), preferred_element_type=F32)
        heads_per_group = SSD_HEADS // SSD_GROUPS
        for pair in range(heads_per_group // 2):
            h0 = g * heads_per_group + 2 * pair
            xs_pair = xs[:, h0 * SSD_HEAD_DIM:(h0 + 2) * SSD_HEAD_DIM]
            y_pair = None
            for k in range(2):
                hh = off + h0 + k
                seg = cs[:, hh:hh + 1] - cs_t[hh:hh + 1, :]
                dec = jnp.exp(jnp.where(keep, seg, -jnp.inf))
                m = (scores * dec * dt_t[hh:hh + 1, :]).astype(BF16)
                rhs = jnp.where(first_half if k == 0 else jnp.logical_not(first_half), xs_pair, zero)
                part = jnp.dot(m, rhs, preferred_element_type=F32)
                y_pair = part if y_pair is None else y_pair + part
            pieces.append(y_pair)
    y_diag = jnp.concatenate(pieces, axis=1)

    y_off = []
    new_state = []
    for g in range(SSD_GROUPS):
        gcols = slice(g * GROUP_WIDTH, (g + 1) * GROUP_WIDTH)
        bg = act[:, SSD_WIDTH + g * SSD_STATE:SSD_WIDTH + (g + 1) * SSD_STATE]
        cg = act[:, SSD_WIDTH + SSD_GROUPS * SSD_STATE + g * SSD_STATE:
                 SSD_WIDTH + SSD_GROUPS * SSD_STATE + (g + 1) * SSD_STATE]
        prev = state[:, gcols]
        y_off.append(jnp.dot(cg, prev.astype(BF16), preferred_element_type=F32))
        bg_t = bg.astype(F32).T.astype(BF16)
        new = jnp.dot(bg_t, xw[:, gcols], preferred_element_type=F32)
        new_state.append(prev * cd_x[:, gcols] + new)
    y = y_diag + jnp.concatenate(y_off, axis=1) * into_x
    return y, jnp.concatenate(new_state, axis=1)


def _ssd_body(xbc_ref, xprev_ref, xnext_ref, dt_ref, z_ref, ya_ref, x_ref, gate_ref, cw_ref, cb_ref,
              dtb_ref, alog_ref, dsk_ref, ng_ref, wout_ref, exp_ref, o_ref,
              act_scr, dts_scr, yf_scr, st_scr, ext_scr, *, n_steps):
    d = pl.program_id(1)
    c = pl.program_id(2)
    lane1 = lax.broadcasted_iota(jnp.int32, (1, LANES), 1)
    a_row = jnp.where(lane1 < 2 * SSD_HEADS, -jnp.exp(alog_ref[...]), 0.0)
    chunk_rows = [slice(j * CHUNK, (j + 1) * CHUNK) for j in range(SSD_STEP_ROWS // CHUNK)]

    @pl.when(c == 0)
    def _():
        st_scr[...] = jnp.zeros_like(st_scr)

    def scan(act, dtv, expand, rev):
        state = st_scr[...]
        ys = [None] * len(chunk_rows)
        for j in (reversed(range(len(chunk_rows))) if rev else range(len(chunk_rows))):
            ys[j], state = _ssd_chunk(act[chunk_rows[j], :], dtv[chunk_rows[j], :], a_row, expand, state, rev)
        st_scr[...] = state
        return jnp.concatenate(ys, axis=0)

    @pl.when(d == 0)
    def _forward():
        row0 = pl.multiple_of(c * SSD_STEP_ROWS, SSD_STEP_ROWS)
        ext_scr[0:SUBLANES, :] = jnp.where(c > 0, xprev_ref[...], 0.0)
        ext_scr[SUBLANES:SUBLANES + SSD_STEP_ROWS, :] = xbc_ref[...]
        ext_scr[SUBLANES + SSD_STEP_ROWS:, :] = jnp.where(c < n_steps - 1, xnext_ref[...], 0.0)
        acc = cb_ref[...]
        for k in range(CONV_WIDTH):
            lo = SUBLANES - CONV_WIDTH // 2 + k
            acc = acc + cw_ref[k:k + 1, :] * ext_scr[lo:lo + SSD_STEP_ROWS, :]
        act = _silu(acc).astype(BF16)
        dtv = _softplus(dt_ref[...] + dtb_ref[...])
        act_scr[pl.ds(row0, SSD_STEP_ROWS), :] = act
        dts_scr[pl.ds(row0, SSD_STEP_ROWS), :] = dtv
        yf_scr[pl.ds(row0, SSD_STEP_ROWS), :] = scan(act, dtv, exp_ref[0], rev=False).astype(BF16)

    @pl.when(d == 1)
    def _backward():
        row0 = pl.multiple_of((n_steps - 1 - c) * SSD_STEP_ROWS, SSD_STEP_ROWS)
        act = act_scr[pl.ds(row0, SSD_STEP_ROWS), :]
        dtv = dts_scr[pl.ds(row0, SSD_STEP_ROWS), :]
        yb = scan(act, dtv, exp_ref[1], rev=True)
        xs = act[:, :SSD_WIDTH].astype(F32)
        y = yf_scr[pl.ds(row0, SSD_STEP_ROWS), :].astype(F32) + yb + dsk_ref[...] * xs
        y = y * _silu(z_ref[...].astype(F32))
        y = jnp.concatenate(
            [_rms(y[:, g * GROUP_WIDTH:(g + 1) * GROUP_WIDTH]) for g in range(SSD_GROUPS)], axis=1)
        y = y * ng_ref[...]
        mix = jnp.concatenate([ya_ref[...], y.astype(BF16)], axis=1)
        o = jnp.dot(mix, wout_ref[...], preferred_element_type=F32)
        o_ref[...] = x_ref[...] + gate_ref[...] * o


def _ssd_outproj(xbc, dt, z, ya, x2, mod3, conv_w8, conv_b, dtb_row, alog_row, dsk_row, norm_g, w_out_b,
                 expand, bn, seq):
    t = x2.shape[0]
    nc = seq // SSD_STEP_ROWS
    blocks8 = SSD_STEP_ROWS // SUBLANES
    last8 = t // SUBLANES - 1

    def fwd_chunk(b, d, c):
        return b * nc + c + d * (nc - 1 - c)

    def bwd_chunk(b, d, c):
        return b * nc + nc - 1 - d * c

    const2 = lambda b, d, c: (0, 0)
    return pl.pallas_call(
        functools.partial(_ssd_body, n_steps=nc),
        grid=(bn, 2, nc),
        in_specs=[
            pl.BlockSpec((SSD_STEP_ROWS, CONV_CH), lambda b, d, c: (fwd_chunk(b, d, c), 0)),
            pl.BlockSpec((SUBLANES, CONV_CH),
                         lambda b, d, c: (jnp.maximum(fwd_chunk(b, d, c) * blocks8 - 1, 0), 0)),
            pl.BlockSpec((SUBLANES, CONV_CH),
                         lambda b, d, c: (jnp.minimum((fwd_chunk(b, d, c) + 1) * blocks8, last8), 0)),
            pl.BlockSpec((SSD_STEP_ROWS, LANES), lambda b, d, c: (fwd_chunk(b, d, c), 0)),
            pl.BlockSpec((SSD_STEP_ROWS, SSD_WIDTH), lambda b, d, c: (bwd_chunk(b, d, c), 0)),
            pl.BlockSpec((SSD_STEP_ROWS, GMLP_WIDTH), lambda b, d, c: (bwd_chunk(b, d, c), 0)),
            pl.BlockSpec((SSD_STEP_ROWS, D_MODEL), lambda b, d, c: (bwd_chunk(b, d, c), 0)),
            pl.BlockSpec((None, 1, D_MODEL), lambda b, d, c: (b, 0, 2)),
            pl.BlockSpec((SUBLANES, CONV_CH), const2),
            pl.BlockSpec((1, CONV_CH), const2),
            pl.BlockSpec((1, LANES), const2),
            pl.BlockSpec((1, LANES), const2),
            pl.BlockSpec((1, SSD_WIDTH), const2),
            pl.BlockSpec((1, SSD_WIDTH), const2),
            pl.BlockSpec((GMLP_WIDTH + SSD_WIDTH, D_MODEL), const2, pipeline_mode=pl.Buffered(1)),
            pl.BlockSpec((2, LANES, SSD_WIDTH), lambda b, d, c: (0, 0, 0), pipeline_mode=pl.Buffered(1)),
        ],
        out_specs=pl.BlockSpec((SSD_STEP_ROWS, D_MODEL), lambda b, d, c: (bwd_chunk(b, d, c), 0)),
        out_shape=jax.ShapeDtypeStruct((t, D_MODEL), F32),
        scratch_shapes=[
            pltpu.VMEM((seq, CONV_CH), BF16),
            pltpu.VMEM((seq, LANES), F32),
            pltpu.VMEM((seq, SSD_WIDTH), BF16),
            pltpu.VMEM((SSD_STATE, SSD_WIDTH), F32),
            pltpu.VMEM((SSD_STEP_ROWS + 2 * SUBLANES, CONV_CH), F32),
        ],
        compiler_params=pltpu.CompilerParams(
            dimension_semantics=("arbitrary", "arbitrary", "arbitrary"), vmem_limit_bytes=VMEM_LIMIT),
        name="ssd_outproj",
    )(xbc, xbc, xbc, dt, z, ya, x2, mod3, conv_w8, conv_b, dtb_row, alog_row, dsk_row, norm_g, w_out_b,
      expand)


def _route(logits):
    lane = lax.broadcasted_iota(jnp.int32, logits.shape, 1)
    big = jnp.int32(LANES)
    neg = -jnp.inf
    gmask = (lane >= N_EXPERTS) & (lane < N_EXPERTS + N_EGROUPS)
    gl = jnp.where(gmask, logits, neg)
    gmax = jnp.max(gl, axis=-1, keepdims=True)
    gidx = jnp.min(jnp.where(gl == gmax, lane, big), axis=-1, keepdims=True) - N_EXPERTS
    p_g = 1.0 / jnp.sum(jnp.where(gmask, jnp.exp(gl - gmax), 0.0), axis=-1, keepdims=True)
    lo = gidx * EXPERTS_PER_GROUP
    emask = (lane >= lo) & (lane < lo + EXPERTS_PER_GROUP)
    el = jnp.where(emask, logits, neg)
    v1 = jnp.max(el, axis=-1, keepdims=True)
    i1 = jnp.min(jnp.where(el == v1, lane, big), axis=-1, keepdims=True)
    el2 = jnp.where(lane == i1, neg, el)
    v2 = jnp.max(el2, axis=-1, keepdims=True)
    i2 = jnp.min(jnp.where(el2 == v2, lane, big), axis=-1, keepdims=True)
    e2 = jnp.exp(v2 - v1)
    den = 1.0 + e2
    w1 = p_g / den
    w2 = p_g * e2 / den
    return i1, i2, w1, w2


def _router_body(x_ref, shift_ref, scale_ref, g2_ref, wr_ref, br_ref, w1_ref, w3_ref, w2_ref,
                 h_ref, rt_ref, cnt_ref, w1b_ref, w3b_ref, w2b_ref):
    w1b_ref[...] = w1_ref[...].astype(BF16)
    w3b_ref[...] = w3_ref[...].astype(BF16)
    w2b_ref[...] = w2_ref[...].astype(BF16)
    counts = jnp.zeros((1, LANES), F32)
    for part in range(TM_MOE // ROUTER_ROWS):
        rows = slice(part * ROUTER_ROWS, (part + 1) * ROUTER_ROWS)
        h = _rms(x_ref[rows, :]) * g2_ref[...]
        h = h * (1.0 + scale_ref[...]) + shift_ref[...]
        hb = h.astype(BF16)
        h_ref[rows, :] = hb
        h_lo = (h - hb.astype(F32)).astype(BF16)
        both = jnp.dot(hb, wr_ref[...], preferred_element_type=F32)
        logits = (both[:, :LANES] + both[:, LANES:]
                  + jnp.dot(h_lo, wr_ref[:, :LANES], preferred_element_type=F32)) + br_ref[...]
        i1, i2, w1, w2 = _route(logits)
        lane = lax.broadcasted_iota(jnp.int32, (ROUTER_ROWS, LANES), 1)
        rt_ref[rows, :] = jnp.where(lane == 0, i1.astype(F32), jnp.where(lane == 1, i2.astype(F32),
                                    jnp.where(lane == 2, w1, jnp.where(lane == 3, w2, 0.0))))
        mask = jnp.where(lane == i1, 1.0, jnp.where(lane == i2, 1.0, 0.0))
        counts = counts + jnp.sum(mask, axis=0, keepdims=True)
    cnt_ref[...] = jnp.broadcast_to(counts, (SUBLANES, LANES))


def _router(x1, mod3, norm2_g, w_router, b_router, w1, w3, w2, seq):
    t = x1.shape[0]
    tiles_per_seq = seq // TM_MOE
    n_tiles = t // TM_MOE
    up_rows = N_EXPERTS * D_MODEL // n_tiles
    down_rows = N_EXPERTS * D_EXPERT // n_tiles
    assert up_rows * n_tiles == N_EXPERTS * D_MODEL and up_rows % SEG_ALIGN == 0
    assert down_rows * n_tiles == N_EXPERTS * D_EXPERT and down_rows % SEG_ALIGN == 0
    row = lambda i: (i, 0)
    const2 = lambda i: (0, 0)
    outs = pl.pallas_call(
        _router_body,
        grid=(n_tiles,),
        in_specs=[
            pl.BlockSpec((TM_MOE, D_MODEL), row),
            pl.BlockSpec((None, 1, D_MODEL), lambda i: (i // tiles_per_seq, 0, 3)),
            pl.BlockSpec((None, 1, D_MODEL), lambda i: (i // tiles_per_seq, 0, 4)),
            pl.BlockSpec((1, D_MODEL), const2),
            pl.BlockSpec((D_MODEL, 2 * LANES), const2),
            pl.BlockSpec((1, LANES), const2),
            pl.BlockSpec((up_rows, D_EXPERT), row),
            pl.BlockSpec((up_rows, D_EXPERT), row),
            pl.BlockSpec((down_rows, D_MODEL), row),
        ],
        out_specs=[
            pl.BlockSpec((TM_MOE, D_MODEL), row),
            pl.BlockSpec((TM_MOE, LANES), row),
            pl.BlockSpec((SUBLANES, LANES), row),
            pl.BlockSpec((up_rows, D_EXPERT), row),
            pl.BlockSpec((up_rows, D_EXPERT), row),
            pl.BlockSpec((down_rows, D_MODEL), row),
        ],
        out_shape=[
            jax.ShapeDtypeStruct((t, D_MODEL), BF16),
            jax.ShapeDtypeStruct((t, LANES), F32),
            jax.ShapeDtypeStruct((n_tiles * SUBLANES, LANES), F32),
            jax.ShapeDtypeStruct((N_EXPERTS * D_MODEL, D_EXPERT), BF16),
            jax.ShapeDtypeStruct((N_EXPERTS * D_MODEL, D_EXPERT), BF16),
            jax.ShapeDtypeStruct((N_EXPERTS * D_EXPERT, D_MODEL), BF16),
        ],
        compiler_params=pltpu.CompilerParams(
            dimension_semantics=("arbitrary",), vmem_limit_bytes=VMEM_LIMIT),
        name="moe_router",
    )(x1, mod3, mod3, norm2_g, w_router, b_router, w1.reshape(-1, D_EXPERT), w3.reshape(-1, D_EXPERT),
      w2.reshape(-1, D_MODEL))
    h2, rt, counts, w1b, w3b, w2b = outs
    return (h2, rt, counts, w1b.reshape(N_EXPERTS, D_MODEL, D_EXPERT), w3b.reshape(N_EXPERTS, D_MODEL, D_EXPERT),
            w2b.reshape(N_EXPERTS, D_EXPERT, D_MODEL))


def _row_copy(vmem_buf, v_start, hbm_ref, g_start, size, sem, to_hbm):
    v = vmem_buf.at[pl.ds(v_start, size), :]
    g = hbm_ref.at[pl.ds(g_start, size), :]
    return pltpu.make_async_copy(v, g, sem) if to_hbm else pltpu.make_async_copy(g, v, sem)


def _copy_rows(n, vmem_buf, v0, hbm_ref, g0, sem, *, to_hbm, wait):
    def run(v_off, g_off, size):
        v_start = 0 if v0 is None else pl.multiple_of(v0 + v_off, SEG_ALIGN)
        cp = _row_copy(vmem_buf, v_start, hbm_ref, pl.multiple_of(g0 + g_off, SEG_ALIGN), size, sem, to_hbm)
        if wait:
            cp.wait()
        else:
            cp.start()

    n_big = n // PIECE_ROWS

    def big_piece(k, carry):
        run(k * PIECE_ROWS, k * PIECE_ROWS, PIECE_ROWS)
        return carry

    lax.fori_loop(0, n_big, big_piece, 0)
    base = n_big * PIECE_ROWS
    rem = n - base
    for size in REM_SIZES:
        done = base + (rem // (2 * size)) * (2 * size)

        @pl.when((rem & size) != 0)
        def _():
            run(done, done, size)


def _segment_copies(seg_ref, loc_ref, goff_ref, tile, vmem_buf, hbm_ref, sem, *, to_hbm):
    base = tile * N_EXPERTS

    def per_expert(e, carry):
        _copy_rows(seg_ref[base + e], vmem_buf, loc_ref[base + e], hbm_ref, goff_ref[base + e], sem,
                   to_hbm=to_hbm, wait=False)
        return carry

    lax.fori_loop(0, N_EXPERTS, per_expert, 0)


def _tile_rows(seg_ref, loc_ref, tile):
    last = tile * N_EXPERTS + N_EXPERTS - 1
    return loc_ref[last] + seg_ref[last]


def _segment_wait(seg_ref, loc_ref, tile, vmem_buf, hbm_ref, sem, *, to_hbm):
    total = _tile_rows(seg_ref, loc_ref, tile)
    for size in TOTAL_SIZES:
        @pl.when((total & size) != 0)
        def _():
            _row_copy(vmem_buf, 0, hbm_ref, 0, size, sem, to_hbm).wait()


def _dispatch_body(seg_ref, loc_ref, goff_ref, fill_ref, h_ref, rt_ref, locrow_ref, pos_ref, xs_hbm, buf, sem,
                   *, n_tiles, n_row_tiles):
    i = pl.program_id(0)
    slot = i % 2

    start = functools.partial(_segment_copies, seg_ref, loc_ref, goff_ref, hbm_ref=xs_hbm, to_hbm=True)
    wait = functools.partial(_segment_wait, seg_ref, loc_ref, hbm_ref=xs_hbm, to_hbm=True)

    @pl.when(i >= 2)
    def _():
        wait(i - 2, buf.at[slot], sem=sem.at[slot])

    rt = rt_ref[...]
    lane = lax.broadcasted_iota(jnp.int32, (TM_MOE, LANES), 1)
    lanef = lane.astype(F32)
    e1 = rt[:, 0:1]
    e2 = rt[:, 1:2]
    mask = jnp.where(lanef == e1, 1.0, jnp.where(lanef == e2, 1.0, 0.0))
    r = lax.broadcasted_iota(jnp.int32, (TM_MOE, TM_MOE), 0)
    cc = lax.broadcasted_iota(jnp.int32, (TM_MOE, TM_MOE), 1)
    strict = jnp.where(r > cc, 1.0, 0.0).astype(BF16)
    rank = jnp.dot(strict, mask.astype(BF16), preferred_element_type=F32)
    posall = rank + locrow_ref[...]
    pos1 = jnp.sum(jnp.where(lanef == e1, posall, 0.0), axis=-1, keepdims=True)
    pos2 = jnp.sum(jnp.where(lanef == e2, posall, 0.0), axis=-1, keepdims=True)
    pp = jnp.where(lane == 0, pos1, jnp.where(lane == 1, pos2, rt))
    pos_ref[...] = pp

    slots_t = [pp[k * LANES:(k + 1) * LANES, :].T for k in range(TM_MOE // LANES)]

    rowp = lax.broadcasted_iota(jnp.int32, (R_LOC, LANES), 0).astype(F32)
    perm = jnp.concatenate(
        [jnp.where(rowp == st[0:1, :], 1.0, jnp.where(rowp == st[1:2, :], 1.0, 0.0)).astype(BF16)
         for st in slots_t], axis=1)
    buf[slot] = jnp.dot(perm, h_ref[...], preferred_element_type=F32).astype(BF16)
    start(i, buf.at[slot], sem=sem.at[slot])

    @pl.when(i == n_tiles - 1)
    def _():
        if n_tiles > 1:
            wait(i - 1, buf.at[1 - slot], sem=sem.at[1 - slot])
        wait(i, buf.at[slot], sem=sem.at[slot])
        zsrc = buf.at[1 - slot]
        zsem = sem.at[1 - slot]
        zsrc[0:TR_EXPERT, :] = jnp.zeros((TR_EXPERT, D_MODEL), BF16)
        for waiting in (False, True):
            def per_expert(e, carry):
                _copy_rows(fill_ref[N_EXPERTS + e], zsrc, None, xs_hbm, fill_ref[e], zsem,
                           to_hbm=True, wait=waiting)
                return carry

            def per_row_tile(k, carry):
                cp = _row_copy(zsrc, 0, xs_hbm, pl.multiple_of(k * TR_EXPERT, TR_EXPERT), TR_EXPERT, zsem, True)
                if waiting:
                    cp.wait()
                else:
                    cp.start()
                return carry

            lax.fori_loop(0, N_EXPERTS, per_expert, 0)
            lax.fori_loop(fill_ref[2 * N_EXPERTS], n_row_tiles, per_row_tile, 0)


def _dispatch(h2, rt, seg, loc, goff, fill, locrow, n_row_tiles):
    t = h2.shape[0]
    n_tiles = t // TM_MOE
    row = lambda i, *_: (i, 0)
    grid_spec = pltpu.PrefetchScalarGridSpec(
        num_scalar_prefetch=4,
        grid=(n_tiles,),
        in_specs=[
            pl.BlockSpec((TM_MOE, D_MODEL), row),
            pl.BlockSpec((TM_MOE, LANES), row),
            pl.BlockSpec((None, 1, LANES), lambda i, *_: (i, 0, 0)),
        ],
        out_specs=[
            pl.BlockSpec((TM_MOE, LANES), row),
            pl.BlockSpec(memory_space=pl.ANY),
        ],
        scratch_shapes=[
            pltpu.VMEM((2, R_LOC, D_MODEL), BF16),
            pltpu.SemaphoreType.DMA((2,)),
        ],
    )
    return pl.pallas_call(
        functools.partial(_dispatch_body, n_tiles=n_tiles, n_row_tiles=n_row_tiles),
        grid_spec=grid_spec,
        out_shape=[
            jax.ShapeDtypeStruct((t, LANES), F32),
            jax.ShapeDtypeStruct((n_row_tiles * TR_EXPERT, D_MODEL), BF16),
        ],
        compiler_params=pltpu.CompilerParams(
            dimension_semantics=("arbitrary",), vmem_limit_bytes=VMEM_LIMIT),
        name="moe_dispatch",
    )(seg, loc, goff, fill, h2, rt, locrow)


def _expert_body(te_ref, tw_ref, na_ref, xs_ref, w1a_ref, w3a_ref, w2a_ref, w1b_ref, w3b_ref, w2b_ref, y_ref):
    del tw_ref
    r = pl.program_id(0)
    n_active = na_ref[0]
    weights = ((w1a_ref, w3a_ref, w2a_ref), (w1b_ref, w3b_ref, w2b_ref))

    def swiglu(half, rows=None):
        w1_ref, w3_ref, w2_ref = weights[half]
        rows = slice(half * TR_EXPERT, (half + 1) * TR_EXPERT) if rows is None else rows
        x = xs_ref[rows, :]
        a = _silu(jnp.dot(x, w1_ref[...], preferred_element_type=F32))
        a = a * jnp.dot(x, w3_ref[...], preferred_element_type=F32)
        y_ref[rows, :] = jnp.dot(a.astype(BF16), w2_ref[...], preferred_element_type=F32).astype(BF16)

    first = EXPERT_TILES_PER_STEP * r
    both = first + 1 < n_active
    same = te_ref[first] == te_ref[first + 1]

    @pl.when(both & same)
    def _():
        swiglu(0, slice(0, EXPERT_TILES_PER_STEP * TR_EXPERT))

    @pl.when(both & jnp.logical_not(same))
    def _():
        swiglu(0)
        swiglu(1)

    @pl.when(first + 1 == n_active)
    def _():
        swiglu(0)
        y_ref[TR_EXPERT:, :] = jnp.zeros((TR_EXPERT, D_MODEL), BF16)

    @pl.when(first >= n_active)
    def _():
        y_ref[...] = jnp.zeros_like(y_ref)


def _experts(xs, tile_expert, tile_weights, n_active, w1b, w3b, w2b):
    n_rows = xs.shape[0]
    step_rows = EXPERT_TILES_PER_STEP * TR_EXPERT
    x_block = lambda r, te, tw, na: (jnp.minimum(r, (na[0] - 1) // EXPERT_TILES_PER_STEP), 0)
    w_specs = []
    for half in range(EXPERT_TILES_PER_STEP):
        w_block = lambda r, te, tw, na, half=half: (tw[EXPERT_TILES_PER_STEP * r + half], 0, 0)
        w_specs += [pl.BlockSpec((None, D_MODEL, D_EXPERT), w_block),
                    pl.BlockSpec((None, D_MODEL, D_EXPERT), w_block),
                    pl.BlockSpec((None, D_EXPERT, D_MODEL), w_block)]
    grid_spec = pltpu.PrefetchScalarGridSpec(
        num_scalar_prefetch=3,
        grid=(n_rows // step_rows,),
        in_specs=[pl.BlockSpec((step_rows, D_MODEL), x_block)] + w_specs,
        out_specs=pl.BlockSpec((step_rows, D_MODEL), lambda r, te, tw, na: (r, 0)),
    )
    return pl.pallas_call(
        _expert_body,
        grid_spec=grid_spec,
        out_shape=jax.ShapeDtypeStruct((n_rows, D_MODEL), BF16),
        compiler_params=pltpu.CompilerParams(
            dimension_semantics=("arbitrary",), vmem_limit_bytes=VMEM_LIMIT),
        name="moe_experts",
    )(tile_expert, tile_weights, n_active, xs, w1b, w3b, w2b, w1b, w3b, w2b)


def _combine_body(seg_ref, loc_ref, goff_ref, y_hbm, pos_ref, x_ref, gate_ref, fg_ref, o_ref, buf, sem, *,
                  n_tiles, final_norm):
    i = pl.program_id(0)
    slot = i % 2
    start = functools.partial(_segment_copies, seg_ref, loc_ref, goff_ref, hbm_ref=y_hbm, to_hbm=False)

    @pl.when(i == 0)
    def _():
        buf[...] = jnp.zeros_like(buf)
        start(0, buf.at[0], sem=sem.at[0])

    @pl.when(i + 1 < n_tiles)
    def _():
        start(i + 1, buf.at[1 - slot], sem=sem.at[1 - slot])

    _segment_wait(seg_ref, loc_ref, i, buf.at[slot], y_hbm, sem.at[slot], to_hbm=False)
    for part in range(TM_MOE // COMBINE_ROWS):
        rows = slice(part * COMBINE_ROWS, (part + 1) * COMBINE_ROWS)
        pp = pos_ref[rows, :]
        colp = lax.broadcasted_iota(jnp.int32, (COMBINE_ROWS, R_LOC), 1).astype(F32)
        wc = jnp.where(colp == pp[:, 0:1], pp[:, 2:3], jnp.where(colp == pp[:, 1:2], pp[:, 3:4], 0.0))
        moe = jnp.dot(wc.astype(BF16), buf[slot], preferred_element_type=F32)
        x2 = x_ref[rows, :] + gate_ref[...] * moe
        o_ref[rows, :] = _rms(x2) * fg_ref[...] if final_norm else x2


def _combine(y, pos, x1, mod3, final_g, seg, loc, goff, seq, final_norm):
    t = x1.shape[0]
    n_tiles = t // TM_MOE
    tiles_per_seq = seq // TM_MOE
    row = lambda i, *_: (i, 0)
    grid_spec = pltpu.PrefetchScalarGridSpec(
        num_scalar_prefetch=3,
        grid=(n_tiles,),
        in_specs=[
            pl.BlockSpec(memory_space=pl.ANY),
            pl.BlockSpec((TM_MOE, LANES), row),
            pl.BlockSpec((TM_MOE, D_MODEL), row),
            pl.BlockSpec((None, 1, D_MODEL), lambda i, *_: (i // tiles_per_seq, 0, 5)),
            pl.BlockSpec((1, D_MODEL), lambda i, *_: (0, 0)),
        ],
        out_specs=pl.BlockSpec((TM_MOE, D_MODEL), row),
        scratch_shapes=[
            pltpu.VMEM((2, R_LOC, D_MODEL), BF16),
            pltpu.SemaphoreType.DMA((2,)),
        ],
    )
    return pl.pallas_call(
        functools.partial(_combine_body, n_tiles=n_tiles, final_norm=final_norm),
        grid_spec=grid_spec,
        out_shape=jax.ShapeDtypeStruct((t, D_MODEL), F32),
        compiler_params=pltpu.CompilerParams(
            dimension_semantics=("arbitrary",), vmem_limit_bytes=VMEM_LIMIT),
        name="moe_combine",
    )(seg, loc, goff, y, pos, x1, mod3, final_g)


def _moe_plan(counts, n_tiles, n_row_tiles):
    cnt = counts.reshape(n_tiles, SUBLANES, LANES)[:, 0, :N_EXPERTS].astype(jnp.int32)
    seg = (cnt + SEG_ALIGN - 1) // SEG_ALIGN * SEG_ALIGN
    before_e = jnp.arange(N_EXPERTS)[:, None] < jnp.arange(N_EXPERTS)[None, :]
    before_t = jnp.arange(n_tiles)[:, None] > jnp.arange(n_tiles)[None, :]
    loc = jnp.sum(jnp.where(before_e[None], seg[:, :, None], 0), axis=1)
    tot = jnp.sum(seg, axis=0)
    region = (tot + TR_EXPERT - 1) // TR_EXPERT * TR_EXPERT
    gstart = jnp.sum(jnp.where(before_e, region[:, None], 0), axis=0)
    gend = gstart + region
    goff = gstart[None, :] + jnp.sum(jnp.where(before_t[:, :, None], seg[None], 0), axis=1)
    n_active = gend[-1] // TR_EXPERT
    tile_row0 = jnp.arange(n_row_tiles, dtype=jnp.int32) * TR_EXPERT
    last_row0 = (n_active - 1) * TR_EXPERT
    te = jnp.sum(gend[None, :] <= jnp.minimum(tile_row0, last_row0)[:, None], axis=1).astype(jnp.int32)
    te = jnp.minimum(te, N_EXPERTS - 1)
    te2 = te.reshape(-1, EXPERT_TILES_PER_STEP)
    cand = jnp.where(te2[:, 1] != te2[:, 0], te2[:, 1], 0)
    steps = jnp.arange(te2.shape[0])
    held = jnp.max(jnp.where(steps[None, :] <= steps[:, None], cand[None, :], 0), axis=1)
    tw = jnp.stack([te2[:, 0], held], axis=1).reshape(-1).astype(jnp.int32)
    locrow = _pad_cols(loc.astype(F32), LANES).reshape(n_tiles, 1, LANES)
    n_active = n_active.reshape(1).astype(jnp.int32)
    fill = jnp.concatenate([gstart + tot, region - tot, n_active]).astype(jnp.int32)
    return (seg.reshape(-1), loc.reshape(-1).astype(jnp.int32), goff.reshape(-1).astype(jnp.int32), fill,
            locrow, te, tw, n_active)


def _moe_final(x1, mod3, norm2_g, w_router, b_router, w1, w3, w2, final_g, seq, final_norm):
    t = x1.shape[0]
    n_tiles = t // TM_MOE
    max_rows = N_ASSIGN * t + (SEG_ALIGN - 1) * N_EXPERTS * n_tiles + N_EXPERTS * (TR_EXPERT - SEG_ALIGN)
    step_rows = EXPERT_TILES_PER_STEP * TR_EXPERT
    n_row_tiles = -(-max_rows // step_rows) * EXPERT_TILES_PER_STEP
    h2, rt, counts, w1b, w3b, w2b = _router(x1, mod3, norm2_g, w_router, b_router, w1, w3, w2, seq)
    seg, loc, goff, fill, locrow, te, tw, n_active = _moe_plan(counts, n_tiles, n_row_tiles)
    pos, xs = _dispatch(h2, rt, seg, loc, goff, fill, locrow, n_row_tiles)
    y = _experts(xs, te, tw, n_active, w1b, w3b, w2b)
    return _combine(y, pos, x1, mod3, final_g, seg, loc, goff, seq, final_norm)


def _pad_cols(a, width):
    return jnp.pad(a, ((0, 0), (0, width - a.shape[1])))


def kernel(x, c, w_ada, b_ada, norm1_g, w_in, b_in, gmlp_ln_g, gmlp_ln_b, gmlp_w_s, gmlp_b_s, gmlp_out_g, conv_w, conv_b, a_log_f, a_log_b, dt_bias_f, dt_bias_b, d_skip, ssd_norm_g, w_out, norm2_g, w_router_g, b_router_g, w_router_e, b_router_e, w1, w3, w2, final_g):
    bn, seq, _ = x.shape
    depth = w_ada.shape[0]
    x2 = x.reshape(bn * seq, D_MODEL)

    head_rows = jnp.arange(LANES)[:, None]
    head_cols = jnp.arange(SSD_WIDTH)[None, :] // SSD_HEAD_DIM
    expand = jnp.stack([head_rows == head_cols, head_rows == head_cols + SSD_HEADS]).astype(BF16)

    for l in range(depth):
        mod3 = _modulation(c, w_ada, b_ada[l], l).reshape(bn, 1, N_MOD * D_MODEL)

        w_in_p = _cast_pad_in_weights(w_in, l)
        b_in_p = _pad_cols(b_in[l][None, :], IN_COLS_PAD)
        bs = jnp.repeat(gmlp_b_s[l].T, LANES, axis=1)
        bs_tile = jnp.tile(bs, (TM_PROJ // CHUNK, 1))
        ya, z, xbc, dt = _inproj_gmlp(
            x2, mod3, norm1_g[l][None, :], w_in_p, b_in_p, gmlp_ln_g[l][None, :], gmlp_ln_b[l][None, :],
            gmlp_w_s[l].astype(BF16), bs_tile, gmlp_out_g[l][None, :], seq)

        conv_w8 = jnp.pad(conv_w[l], ((0, SUBLANES - CONV_WIDTH), (0, 0)))
        dtb_row = _pad_cols(jnp.concatenate([dt_bias_f[l], dt_bias_b[l]])[None, :], LANES)
        alog_row = _pad_cols(jnp.concatenate([a_log_f[l], a_log_b[l]])[None, :], LANES)
        dsk_row = jnp.repeat(d_skip[l], SSD_HEAD_DIM)[None, :]
        x2 = _ssd_outproj(xbc, dt, z, ya, x2, mod3, conv_w8, conv_b[l][None, :], dtb_row, alog_row, dsk_row,
                          ssd_norm_g[l][None, :], w_out[l].astype(BF16), expand, bn, seq)

        w_re = jnp.transpose(w_router_e[l], (1, 0, 2)).reshape(D_MODEL, N_EXPERTS)
        w_router = _pad_cols(jnp.concatenate([w_re, w_router_g[l]], axis=1), LANES)
        w_router_hi = w_router.astype(BF16)
        w_router = jnp.concatenate([w_router_hi, (w_router - w_router_hi.astype(F32)).astype(BF16)], axis=1)
        b_router = _pad_cols(jnp.concatenate([b_router_e[l].reshape(-1), b_router_g[l]])[None, :], LANES)
        x2 = _moe_final(x2, mod3, norm2_g[l][None, :], w_router, b_router, w1[l], w3[l], w2[l],
                        final_g[None, :], seq, final_norm=(l == depth - 1))
    return x2.reshape(bn, seq, D_MODEL)
```

```python
import functools
import math

import jax
import jax.numpy as jnp
from jax import lax
from jax.experimental import pallas as pl
from jax.experimental.pallas import tpu as pltpu

F32 = jnp.float32
BF16 = jnp.bfloat16
HIGHEST = lax.Precision.HIGHEST

D_MODEL = 1024
N_MOD = 6
GMLP_WIDTH = 1024
GMLP_HEADS = 8
CHUNK = 128
SSD_STEP_ROWS = 4 * CHUNK
SSD_WIDTH = 1024
SSD_HEADS = 16
SSD_HEAD_DIM = 64
SSD_GROUPS = 2
SSD_STATE = 128
GROUP_WIDTH = SSD_WIDTH // SSD_GROUPS
CONV_WIDTH = 5
CONV_CH = SSD_WIDTH + 2 * SSD_GROUPS * SSD_STATE
N_EGROUPS = 4
EXPERTS_PER_GROUP = 8
N_EXPERTS = 32
D_EXPERT = 256
EPS = 1e-6

LANES = 128
SUBLANES = 8
COL_U, COL_V, COL_Z, COL_XBC, COL_DT = 0, 1024, 2048, 3072, 4608
IN_COLS = 4640
IN_COLS_PAD = COL_DT + LANES
TM_PROJ = 1024
PROJ_ROWS = 512
COMBINE_ROWS = 128
ROUTER_ROWS = 256
TM_MOE = 512
TR_EXPERT = 512
EXPERT_TILES_PER_STEP = 2
N_ASSIGN = 2
SEG_ALIGN = 2 * SUBLANES
R_LOC = N_ASSIGN * TM_MOE + N_EXPERTS * SEG_ALIGN
PIECE_ROWS = 128
REM_SIZES = tuple(PIECE_ROWS >> s for s in range(1, PIECE_ROWS.bit_length()) if PIECE_ROWS >> s >= SEG_ALIGN)
TOTAL_SIZES = tuple(1 << s for s in range(R_LOC.bit_length() - 1, -1, -1) if 1 << s >= SEG_ALIGN)
VMEM_LIMIT = 56 * 1024 * 1024


def _silu(v):
    return v * jax.nn.sigmoid(v)


def _gelu(v):
    return 0.5 * v * (1.0 + lax.erf(v * math.sqrt(0.5)))


def _softplus(v):
    return jnp.maximum(v, 0.0) + jnp.log1p(jnp.exp(-jnp.abs(v)))


def _rms(v):
    return v * lax.rsqrt(jnp.mean(v * v, axis=-1, keepdims=True) + EPS)


def _mod_body(c_ref, w_ref, b_ref, o_ref):
    ca = _silu(c_ref[...])
    o_ref[...] = jnp.dot(ca, w_ref[...], precision=HIGHEST, preferred_element_type=F32) + b_ref[...]


def _modulation(c, w_ada, b_ada, layer):
    bn = c.shape[0]
    return pl.pallas_call(
        _mod_body,
        grid=(N_MOD,),
        in_specs=[
            pl.BlockSpec((bn, D_MODEL), lambda j: (0, 0)),
            pl.BlockSpec((None, D_MODEL, D_MODEL), lambda j: (layer, 0, j)),
            pl.BlockSpec((1, D_MODEL), lambda j: (0, j)),
        ],
        out_specs=pl.BlockSpec((bn, D_MODEL), lambda j: (0, j)),
        out_shape=jax.ShapeDtypeStruct((bn, N_MOD * D_MODEL), F32),
        name="adaln_mod",
    )(c, w_ada, b_ada.reshape(1, -1))


def _inproj_body(x_ref, shift_ref, scale_ref, g_ref, w_ref, b_ref, lng_ref, lnb_ref, ws_ref, bs_ref,
                 og_ref, ya_ref, z_ref, xbc_ref, dt_ref, mix_scr):
    for part in range(TM_PROJ // PROJ_ROWS):
        rows = slice(part * PROJ_ROWS, (part + 1) * PROJ_ROWS)
        h = _rms(x_ref[rows, :]) * g_ref[...]
        h = h * (1.0 + scale_ref[...]) + shift_ref[...]
        hb = h.astype(BF16)

        def proj(lo, hi, hb=hb):
            return jnp.dot(hb, w_ref[:, lo:hi], preferred_element_type=F32) + b_ref[:, lo:hi]

        z_ref[rows, :] = proj(COL_Z, COL_XBC).astype(BF16)
        xbc_ref[rows, :] = proj(COL_XBC, COL_DT)
        dt_ref[rows, :] = proj(COL_DT, IN_COLS_PAD)

        v = _gelu(proj(COL_V, COL_Z))
        mu = jnp.mean(v, axis=-1, keepdims=True)
        vc = v - mu
        var = jnp.mean(vc * vc, axis=-1, keepdims=True)
        vn = (vc * lax.rsqrt(var + EPS) * lng_ref[...] + lnb_ref[...]).astype(BF16)
        n_chunks = PROJ_ROWS // CHUNK
        for hd in range(GMLP_HEADS):
            cols = slice(hd * LANES, (hd + 1) * LANES)
            rhs = jnp.concatenate([vn[c * CHUNK:(c + 1) * CHUNK, cols] for c in range(n_chunks)], axis=1)
            res = jnp.dot(ws_ref[hd], rhs, preferred_element_type=F32)
            for c in range(n_chunks):
                mix_scr[part * PROJ_ROWS + c * CHUNK:part * PROJ_ROWS + (c + 1) * CHUNK, cols] = (
                    res[:, c * LANES:(c + 1) * LANES])
        u = _gelu(proj(COL_U, COL_V))
        out = u * (mix_scr[rows, :] + bs_ref[rows, :])
        ya_ref[rows, :] = (_rms(out) * og_ref[...]).astype(BF16)


def _inproj_gmlp(x2, mod3, norm1_g, w_in_p, b_in_p, ln_g, ln_b, w_s, bs_tile, out_g, seq):
    t = x2.shape[0]
    tiles_per_seq = seq // TM_PROJ
    row = lambda i: (i, 0)
    const2 = lambda i: (0, 0)
    return pl.pallas_call(
        _inproj_body,
        grid=(t // TM_PROJ,),
        in_specs=[
            pl.BlockSpec((TM_PROJ, D_MODEL), row),
            pl.BlockSpec((None, 1, D_MODEL), lambda i: (i // tiles_per_seq, 0, 0)),
            pl.BlockSpec((None, 1, D_MODEL), lambda i: (i // tiles_per_seq, 0, 1)),
            pl.BlockSpec((1, D_MODEL), const2),
            pl.BlockSpec((D_MODEL, IN_COLS_PAD), const2, pipeline_mode=pl.Buffered(1)),
            pl.BlockSpec((1, IN_COLS_PAD), const2),
            pl.BlockSpec((1, GMLP_WIDTH), const2),
            pl.BlockSpec((1, GMLP_WIDTH), const2),
            pl.BlockSpec((GMLP_HEADS, CHUNK, CHUNK), lambda i: (0, 0, 0)),
            pl.BlockSpec((TM_PROJ, GMLP_WIDTH), const2, pipeline_mode=pl.Buffered(1)),
            pl.BlockSpec((1, GMLP_WIDTH), const2),
        ],
        out_specs=[
            pl.BlockSpec((TM_PROJ, GMLP_WIDTH), row),
            pl.BlockSpec((TM_PROJ, SSD_WIDTH), row),
            pl.BlockSpec((TM_PROJ, CONV_CH), row),
            pl.BlockSpec((TM_PROJ, LANES), row),
        ],
        out_shape=[
            jax.ShapeDtypeStruct((t, GMLP_WIDTH), BF16),
            jax.ShapeDtypeStruct((t, SSD_WIDTH), BF16),
            jax.ShapeDtypeStruct((t, CONV_CH), F32),
            jax.ShapeDtypeStruct((t, LANES), F32),
        ],
        scratch_shapes=[pltpu.VMEM((TM_PROJ, GMLP_WIDTH), F32)],
        compiler_params=pltpu.CompilerParams(
            dimension_semantics=("arbitrary",), vmem_limit_bytes=VMEM_LIMIT),
        name="inproj_gmlp",
    )(x2, mod3, mod3, norm1_g, w_in_p, b_in_p, ln_g, ln_b, w_s, bs_tile, out_g)


def _ssd_chunk(act, dtv, a_row, expand, state, rev):
    off = SSD_HEADS if rev else 0
    row = lax.broadcasted_iota(jnp.int32, (CHUNK, CHUNK), 0)
    col = lax.broadcasted_iota(jnp.int32, (CHUNK, CHUNK), 1)
    lower = row >= col
    upper = row <= col
    keep = upper if rev else lower
    da = dtv * a_row
    cs = jnp.dot(keep.astype(F32), da, precision=HIGHEST, preferred_element_type=F32)
    cs_t = cs.T
    dt_t = dtv.T
    tot = cs[0:1, :] if rev else cs[CHUNK - 1:CHUNK, :]

    xs = act[:, :SSD_WIDTH]
    lane = lax.broadcasted_iota(jnp.int32, (CHUNK, LANES), 1)
    first_half = lane < SSD_HEAD_DIM
    zero = jnp.zeros((), BF16)

    stack = jnp.concatenate(
        [jnp.exp(cs), dtv * jnp.exp(tot - cs), jnp.broadcast_to(jnp.exp(tot), (SUBLANES, LANES))], axis=0)
    stack_x = jnp.dot(stack.astype(BF16), expand, preferred_element_type=F32)
    into_x = stack_x[:CHUNK]
    w_x = stack_x[CHUNK:2 * CHUNK]
    cd_x = stack_x[2 * CHUNK:2 * CHUNK + 1]
    xw = xs * w_x.astype(BF16)

    pieces = []
    for g in range(SSD_GROUPS):
        bg = act[:, SSD_WIDTH + g * SSD_STATE:SSD_WIDTH + (g + 1) * SSD_STATE]
        cg = act[:, SSD_WIDTH + SSD_GROUPS * SSD_STATE + g * SSD_STATE:
                 SSD_WIDTH + SSD_GROUPS * SSD_STATE + (g + 1) * SSD_STATE]
        scores = lax.dot_general(cg, bg, (((1,), (1,)), ((), ())), preferred_element_type=F32)
        heads_per_group = SSD_HEADS // SSD_GROUPS
        for pair in range(heads_per_group // 2):
            h0 = g * heads_per_group + 2 * pair
            xs_pair = xs[:, h0 * SSD_HEAD_DIM:(h0 + 2) * SSD_HEAD_DIM]
            y_pair = None
            for k in range(2):
                hh = off + h0 + k
                seg = cs[:, hh:hh + 1] - cs_t[hh:hh + 1, :]
                dec = jnp.exp(jnp.where(keep, seg, -jnp.inf))
                m = (scores * dec * dt_t[hh:hh + 1, :]).astype(BF16)
                rhs = jnp.where(first_half if k == 0 else jnp.logical_not(first_half), xs_pair, zero)
                part = jnp.dot(m, rhs, preferred_element_type=F32)
                y_pair = part if y_pair is None else y_pair + part
            pieces.append(y_pair)
    y_diag = jnp.concatenate(pieces, axis=1)

    y_off = []
    new_state = []
    for g in range(SSD_GROUPS):
        gcols = slice(g * GROUP_WIDTH, (g + 1) * GROUP_WIDTH)
        bg = act[:, SSD_WIDTH + g * SSD_STATE:SSD_WIDTH + (g + 1) * SSD_STATE]
        cg = act[:, SSD_WIDTH + SSD_GROUPS * SSD_STATE + g * SSD_STATE:
                 SSD_WIDTH + SSD_GROUPS * SSD_STATE + (g + 1) * SSD_STATE]
        prev = state[:, gcols]
        y_off.append(jnp.dot(cg, prev.astype(BF16), preferred_element_type=F32))
        bg_t = bg.astype(F32).T.astype(BF16)
        new = jnp.dot(bg_t, xw[:, gcols], preferred_element_type=F32)
        new_state.append(prev * cd_x[:, gcols] + new)
    y = y_diag + jnp.concatenate(y_off, axis=1) * into_x
    return y, jnp.concatenate(new_state, axis=1)


def _ssd_body(xbc_ref, xprev_ref, xnext_ref, dt_ref, z_ref, ya_ref, x_ref, gate_ref, cw_ref, cb_ref,
              dtb_ref, alog_ref, dsk_ref, ng_ref, wout_ref, exp_ref, o_ref,
              act_scr, dts_scr, yf_scr, st_scr, ext_scr, *, n_steps):
    d = pl.program_id(1)
    c = pl.program_id(2)
    lane1 = lax.broadcasted_iota(jnp.int32, (1, LANES), 1)
    a_row = jnp.where(lane1 < 2 * SSD_HEADS, -jnp.exp(alog_ref[...]), 0.0)
    chunk_rows = [slice(j * CHUNK, (j + 1) * CHUNK) for j in range(SSD_STEP_ROWS // CHUNK)]

    @pl.when(c == 0)
    def _():
        st_scr[...] = jnp.zeros_like(st_scr)

    def scan(act, dtv, expand, rev):
        state = st_scr[...]
        ys = [None] * len(chunk_rows)
        for j in (reversed(range(len(chunk_rows))) if rev else range(len(chunk_rows))):
            ys[j], state = _ssd_chunk(act[chunk_rows[j], :], dtv[chunk_rows[j], :], a_row, expand, state, rev)
        st_scr[...] = state
        return jnp.concatenate(ys, axis=0)

    @pl.when(d == 0)
    def _forward():
        row0 = pl.multiple_of(c * SSD_STEP_ROWS, SSD_STEP_ROWS)
        ext_scr[0:SUBLANES, :] = jnp.where(c > 0, xprev_ref[...], 0.0)
        ext_scr[SUBLANES:SUBLANES + SSD_STEP_ROWS, :] = xbc_ref[...]
        ext_scr[SUBLANES + SSD_STEP_ROWS:, :] = jnp.where(c < n_steps - 1, xnext_ref[...], 0.0)
        acc = cb_ref[...]
        for k in range(CONV_WIDTH):
            lo = SUBLANES - CONV_WIDTH // 2 + k
            acc = acc + cw_ref[k:k + 1, :] * ext_scr[lo:lo + SSD_STEP_ROWS, :]
        act = _silu(acc).astype(BF16)
        dtv = _softplus(dt_ref[...] + dtb_ref[...])
        act_scr[pl.ds(row0, SSD_STEP_ROWS), :] = act
        dts_scr[pl.ds(row0, SSD_STEP_ROWS), :] = dtv
        yf_scr[pl.ds(row0, SSD_STEP_ROWS), :] = scan(act, dtv, exp_ref[0], rev=False).astype(BF16)

    @pl.when(d == 1)
    def _backward():
        row0 = pl.multiple_of((n_steps - 1 - c) * SSD_STEP_ROWS, SSD_STEP_ROWS)
        act = act_scr[pl.ds(row0, SSD_STEP_ROWS), :]
        dtv = dts_scr[pl.ds(row0, SSD_STEP_ROWS), :]
        yb = scan(act, dtv, exp_ref[1], rev=True)
        xs = act[:, :SSD_WIDTH].astype(F32)
        y = yf_scr[pl.ds(row0, SSD_STEP_ROWS), :].astype(F32) + yb + dsk_ref[...] * xs
        y = y * _silu(z_ref[...].astype(F32))
        y = jnp.concatenate(
            [_rms(y[:, g * GROUP_WIDTH:(g + 1) * GROUP_WIDTH]) for g in range(SSD_GROUPS)], axis=1)
        y = y * ng_ref[...]
        mix = jnp.concatenate([ya_ref[...], y.astype(BF16)], axis=1)
        o = jnp.dot(mix, wout_ref[...], preferred_element_type=F32)
        o_ref[...] = x_ref[...] + gate_ref[...] * o


def _ssd_outproj(xbc, dt, z, ya, x2, mod3, conv_w8, conv_b, dtb_row, alog_row, dsk_row, norm_g, w_out_b,
                 expand, bn, seq):
    t = x2.shape[0]
    nc = seq // SSD_STEP_ROWS
    blocks8 = SSD_STEP_ROWS // SUBLANES
    last8 = t // SUBLANES - 1

    def fwd_chunk(b, d, c):
        return b * nc + c + d * (nc - 1 - c)

    def bwd_chunk(b, d, c):
        return b * nc + nc - 1 - d * c

    const2 = lambda b, d, c: (0, 0)
    return pl.pallas_call(
        functools.partial(_ssd_body, n_steps=nc),
        grid=(bn, 2, nc),
        in_specs=[
            pl.BlockSpec((SSD_STEP_ROWS, CONV_CH), lambda b, d, c: (fwd_chunk(b, d, c), 0)),
            pl.BlockSpec((SUBLANES, CONV_CH),
                         lambda b, d, c: (jnp.maximum(fwd_chunk(b, d, c) * blocks8 - 1, 0), 0)),
            pl.BlockSpec((SUBLANES, CONV_CH),
                         lambda b, d, c: (jnp.minimum((fwd_chunk(b, d, c) + 1) * blocks8, last8), 0)),
            pl.BlockSpec((SSD_STEP_ROWS, LANES), lambda b, d, c: (fwd_chunk(b, d, c), 0)),
            pl.BlockSpec((SSD_STEP_ROWS, SSD_WIDTH), lambda b, d, c: (bwd_chunk(b, d, c), 0)),
            pl.BlockSpec((SSD_STEP_ROWS, GMLP_WIDTH), lambda b, d, c: (bwd_chunk(b, d, c), 0)),
            pl.BlockSpec((SSD_STEP_ROWS, D_MODEL), lambda b, d, c: (bwd_chunk(b, d, c), 0)),
            pl.BlockSpec((None, 1, D_MODEL), lambda b, d, c: (b, 0, 2)),
            pl.BlockSpec((SUBLANES, CONV_CH), const2),
            pl.BlockSpec((1, CONV_CH), const2),
            pl.BlockSpec((1, LANES), const2),
            pl.BlockSpec((1, LANES), const2),
            pl.BlockSpec((1, SSD_WIDTH), const2),
            pl.BlockSpec((1, SSD_WIDTH), const2),
            pl.BlockSpec((GMLP_WIDTH + SSD_WIDTH, D_MODEL), const2, pipeline_mode=pl.Buffered(1)),
            pl.BlockSpec((2, LANES, SSD_WIDTH), lambda b, d, c: (0, 0, 0), pipeline_mode=pl.Buffered(1)),
        ],
        out_specs=pl.BlockSpec((SSD_STEP_ROWS, D_MODEL), lambda b, d, c: (bwd_chunk(b, d, c), 0)),
        out_shape=jax.ShapeDtypeStruct((t, D_MODEL), F32),
        scratch_shapes=[
            pltpu.VMEM((seq, CONV_CH), BF16),
            pltpu.VMEM((seq, LANES), F32),
            pltpu.VMEM((seq, SSD_WIDTH), BF16),
            pltpu.VMEM((SSD_STATE, SSD_WIDTH), F32),
            pltpu.VMEM((SSD_STEP_ROWS + 2 * SUBLANES, CONV_CH), F32),
        ],
        compiler_params=pltpu.CompilerParams(
            dimension_semantics=("arbitrary", "arbitrary", "arbitrary"), vmem_limit_bytes=VMEM_LIMIT),
        name="ssd_outproj",
    )(xbc, xbc, xbc, dt, z, ya, x2, mod3, conv_w8, conv_b, dtb_row, alog_row, dsk_row, norm_g, w_out_b,
      expand)


def _route(logits):
    lane = lax.broadcasted_iota(jnp.int32, logits.shape, 1)
    big = jnp.int32(LANES)
    neg = -jnp.inf
    gmask = (lane >= N_EXPERTS) & (lane < N_EXPERTS + N_EGROUPS)
    gl = jnp.where(gmask, logits, neg)
    gmax = jnp.max(gl, axis=-1, keepdims=True)
    gidx = jnp.min(jnp.where(gl == gmax, lane, big), axis=-1, keepdims=True) - N_EXPERTS
    p_g = 1.0 / jnp.sum(jnp.where(gmask, jnp.exp(gl - gmax), 0.0), axis=-1, keepdims=True)
    lo = gidx * EXPERTS_PER_GROUP
    emask = (lane >= lo) & (lane < lo + EXPERTS_PER_GROUP)
    el = jnp.where(emask, logits, neg)
    v1 = jnp.max(el, axis=-1, keepdims=True)
    i1 = jnp.min(jnp.where(el == v1, lane, big), axis=-1, keepdims=True)
    el2 = jnp.where(lane == i1, neg, el)
    v2 = jnp.max(el2, axis=-1, keepdims=True)
    i2 = jnp.min(jnp.where(el2 == v2, lane, big), axis=-1, keepdims=True)
    e2 = jnp.exp(v2 - v1)
    den = 1.0 + e2
    w1 = p_g / den
    w2 = p_g * e2 / den
    return i1, i2, w1, w2


def _router_body(x_ref, shift_ref, scale_ref, g2_ref, wr_ref, br_ref, w1_ref, w3_ref, w2_ref,
                 h_ref, rt_ref, cnt_ref, w1b_ref, w3b_ref, w2b_ref):
    w1b_ref[...] = w1_ref[...].astype(BF16)
    w3b_ref[...] = w3_ref[...].astype(BF16)
    w2b_ref[...] = w2_ref[...].astype(BF16)
    counts = jnp.zeros((1, LANES), F32)
    for part in range(TM_MOE // ROUTER_ROWS):
        rows = slice(part * ROUTER_ROWS, (part + 1) * ROUTER_ROWS)
        h = _rms(x_ref[rows, :]) * g2_ref[...]
        h = h * (1.0 + scale_ref[...]) + shift_ref[...]
        hb = h.astype(BF16)
        h_ref[rows, :] = hb
        h_lo = (h - hb.astype(F32)).astype(BF16)
        both = jnp.dot(hb, wr_ref[...], preferred_element_type=F32)
        logits = (both[:, :LANES] + both[:, LANES:]
                  + jnp.dot(h_lo, wr_ref[:, :LANES], preferred_element_type=F32)) + br_ref[...]
        i1, i2, w1, w2 = _route(logits)
        lane = lax.broadcasted_iota(jnp.int32, (ROUTER_ROWS, LANES), 1)
        rt_ref[rows, :] = jnp.where(lane == 0, i1.astype(F32), jnp.where(lane == 1, i2.astype(F32),
                                    jnp.where(lane == 2, w1, jnp.where(lane == 3, w2, 0.0))))
        mask = jnp.where(lane == i1, 1.0, jnp.where(lane == i2, 1.0, 0.0))
        counts = counts + jnp.sum(mask, axis=0, keepdims=True)
    cnt_ref[...] = jnp.broadcast_to(counts, (SUBLANES, LANES))


def _router(x1, mod3, norm2_g, w_router, b_router, w1, w3, w2, seq):
    t = x1.shape[0]
    tiles_per_seq = seq // TM_MOE
    n_tiles = t // TM_MOE
    up_rows = N_EXPERTS * D_MODEL // n_tiles
    down_rows = N_EXPERTS * D_EXPERT // n_tiles
    assert up_rows * n_tiles == N_EXPERTS * D_MODEL and up_rows % SEG_ALIGN == 0
    assert down_rows * n_tiles == N_EXPERTS * D_EXPERT and down_rows % SEG_ALIGN == 0
    row = lambda i: (i, 0)
    const2 = lambda i: (0, 0)
    outs = pl.pallas_call(
        _router_body,
        grid=(n_tiles,),
        in_specs=[
            pl.BlockSpec((TM_MOE, D_MODEL), row),
            pl.BlockSpec((None, 1, D_MODEL), lambda i: (i // tiles_per_seq, 0, 3)),
            pl.BlockSpec((None, 1, D_MODEL), lambda i: (i // tiles_per_seq, 0, 4)),
            pl.BlockSpec((1, D_MODEL), const2),
            pl.BlockSpec((D_MODEL, 2 * LANES), const2),
            pl.BlockSpec((1, LANES), const2),
            pl.BlockSpec((up_rows, D_EXPERT), row),
            pl.BlockSpec((up_rows, D_EXPERT), row),
            pl.BlockSpec((down_rows, D_MODEL), row),
        ],
        out_specs=[
            pl.BlockSpec((TM_MOE, D_MODEL), row),
            pl.BlockSpec((TM_MOE, LANES), row),
            pl.BlockSpec((SUBLANES, LANES), row),
            pl.BlockSpec((up_rows, D_EXPERT), row),
            pl.BlockSpec((up_rows, D_EXPERT), row),
            pl.BlockSpec((down_rows, D_MODEL), row),
        ],
        out_shape=[
            jax.ShapeDtypeStruct((t, D_MODEL), BF16),
            jax.ShapeDtypeStruct((t, LANES), F32),
            jax.ShapeDtypeStruct((n_tiles * SUBLANES, LANES), F32),
            jax.ShapeDtypeStruct((N_EXPERTS * D_MODEL, D_EXPERT), BF16),
            jax.ShapeDtypeStruct((N_EXPERTS * D_MODEL, D_EXPERT), BF16),
            jax.ShapeDtypeStruct((N_EXPERTS * D_EXPERT, D_MODEL), BF16),
        ],
        compiler_params=pltpu.CompilerParams(
            dimension_semantics=("arbitrary",), vmem_limit_bytes=VMEM_LIMIT),
        name="moe_router",
    )(x1, mod3, mod3, norm2_g, w_router, b_router, w1.reshape(-1, D_EXPERT), w3.reshape(-1, D_EXPERT),
      w2.reshape(-1, D_MODEL))
    h2, rt, counts, w1b, w3b, w2b = outs
    return (h2, rt, counts, w1b.reshape(N_EXPERTS, D_MODEL, D_EXPERT), w3b.reshape(N_EXPERTS, D_MODEL, D_EXPERT),
            w2b.reshape(N_EXPERTS, D_EXPERT, D_MODEL))


def _row_copy(vmem_buf, v_start, hbm_ref, g_start, size, sem, to_hbm):
    v = vmem_buf.at[pl.ds(v_start, size), :]
    g = hbm_ref.at[pl.ds(g_start, size), :]
    return pltpu.make_async_copy(v, g, sem) if to_hbm else pltpu.make_async_copy(g, v, sem)


def _copy_rows(n, vmem_buf, v0, hbm_ref, g0, sem, *, to_hbm, wait):
    def run(v_off, g_off, size):
        v_start = 0 if v0 is None else pl.multiple_of(v0 + v_off, SEG_ALIGN)
        cp = _row_copy(vmem_buf, v_start, hbm_ref, pl.multiple_of(g0 + g_off, SEG_ALIGN), size, sem, to_hbm)
        if wait:
            cp.wait()
        else:
            cp.start()

    n_big = n // PIECE_ROWS

    def big_piece(k, carry):
        run(k * PIECE_ROWS, k * PIECE_ROWS, PIECE_ROWS)
        return carry

    lax.fori_loop(0, n_big, big_piece, 0)
    base = n_big * PIECE_ROWS
    rem = n - base
    for size in REM_SIZES:
        done = base + (rem // (2 * size)) * (2 * size)

        @pl.when((rem & size) != 0)
        def _():
            run(done, done, size)


def _segment_copies(seg_ref, loc_ref, goff_ref, tile, vmem_buf, hbm_ref, sem, *, to_hbm):
    base = tile * N_EXPERTS

    def per_expert(e, carry):
        _copy_rows(seg_ref[base + e], vmem_buf, loc_ref[base + e], hbm_ref, goff_ref[base + e], sem,
                   to_hbm=to_hbm, wait=False)
        return carry

    lax.fori_loop(0, N_EXPERTS, per_expert, 0)


def _tile_rows(seg_ref, loc_ref, tile):
    last = tile * N_EXPERTS + N_EXPERTS - 1
    return loc_ref[last] + seg_ref[last]


def _segment_wait(seg_ref, loc_ref, tile, vmem_buf, hbm_ref, sem, *, to_hbm):
    total = _tile_rows(seg_ref, loc_ref, tile)
    for size in TOTAL_SIZES:
        @pl.when((total & size) != 0)
        def _():
            _row_copy(vmem_buf, 0, hbm_ref, 0, size, sem, to_hbm).wait()


def _dispatch_body(seg_ref, loc_ref, goff_ref, fill_ref, h_ref, rt_ref, locrow_ref, pos_ref, xs_hbm, buf, sem,
                   *, n_tiles, n_row_tiles):
    i = pl.program_id(0)
    slot = i % 2

    start = functools.partial(_segment_copies, seg_ref, loc_ref, goff_ref, hbm_ref=xs_hbm, to_hbm=True)
    wait = functools.partial(_segment_wait, seg_ref, loc_ref, hbm_ref=xs_hbm, to_hbm=True)

    @pl.when(i >= 2)
    def _():
        wait(i - 2, buf.at[slot], sem=sem.at[slot])

    rt = rt_ref[...]
    lane = lax.broadcasted_iota(jnp.int32, (TM_MOE, LANES), 1)
    lanef = lane.astype(F32)
    e1 = rt[:, 0:1]
    e2 = rt[:, 1:2]
    mask = jnp.where(lanef == e1, 1.0, jnp.where(lanef == e2, 1.0, 0.0))
    r = lax.broadcasted_iota(jnp.int32, (TM_MOE, TM_MOE), 0)
    cc = lax.broadcasted_iota(jnp.int32, (TM_MOE, TM_MOE), 1)
    strict = jnp.where(r > cc, 1.0, 0.0).astype(BF16)
    rank = jnp.dot(strict, mask.astype(BF16), preferred_element_type=F32)
    posall = rank + locrow_ref[...]
    pos1 = jnp.sum(jnp.where(lanef == e1, posall, 0.0), axis=-1, keepdims=True)
    pos2 = jnp.sum(jnp.where(lanef == e2, posall, 0.0), axis=-1, keepdims=True)
    pp = jnp.where(lane == 0, pos1, jnp.where(lane == 1, pos2, rt))
    pos_ref[...] = pp

    slots_t = [pp[k * LANES:(k + 1) * LANES, :].T for k in range(TM_MOE // LANES)]

    rowp = lax.broadcasted_iota(jnp.int32, (R_LOC, LANES), 0).astype(F32)
    perm = jnp.concatenate(
        [jnp.where(rowp == st[0:1, :], 1.0, jnp.where(rowp == st[1:2, :], 1.0, 0.0)).astype(BF16)
         for st in slots_t], axis=1)
    buf[slot] = jnp.dot(perm, h_ref[...], preferred_element_type=F32).astype(BF16)
    start(i, buf.at[slot], sem=sem.at[slot])

    @pl.when(i == n_tiles - 1)
    def _():
        if n_tiles > 1:
            wait(i - 1, buf.at[1 - slot], sem=sem.at[1 - slot])
        wait(i, buf.at[slot], sem=sem.at[slot])
        zsrc = buf.at[1 - slot]
        zsem = sem.at[1 - slot]
        zsrc[0:TR_EXPERT, :] = jnp.zeros((TR_EXPERT, D_MODEL), BF16)
        for waiting in (False, True):
            def per_expert(e, carry):
                _copy_rows(fill_ref[N_EXPERTS + e], zsrc, None, xs_hbm, fill_ref[e], zsem,
                           to_hbm=True, wait=waiting)
                return carry

            def per_row_tile(k, carry):
                cp = _row_copy(zsrc, 0, xs_hbm, pl.multiple_of(k * TR_EXPERT, TR_EXPERT), TR_EXPERT, zsem, True)
                if waiting:
                    cp.wait()
                else:
                    cp.start()
                return carry

            lax.fori_loop(0, N_EXPERTS, per_expert, 0)
            lax.fori_loop(fill_ref[2 * N_EXPERTS], n_row_tiles, per_row_tile, 0)


def _dispatch(h2, rt, seg, loc, goff, fill, locrow, n_row_tiles):
    t = h2.shape[0]
    n_tiles = t // TM_MOE
    row = lambda i, *_: (i, 0)
    grid_spec = pltpu.PrefetchScalarGridSpec(
        num_scalar_prefetch=4,
        grid=(n_tiles,),
        in_specs=[
            pl.BlockSpec((TM_MOE, D_MODEL), row),
            pl.BlockSpec((TM_MOE, LANES), row),
            pl.BlockSpec((None, 1, LANES), lambda i, *_: (i, 0, 0)),
        ],
        out_specs=[
            pl.BlockSpec((TM_MOE, LANES), row),
            pl.BlockSpec(memory_space=pl.ANY),
        ],
        scratch_shapes=[
            pltpu.VMEM((2, R_LOC, D_MODEL), BF16),
            pltpu.SemaphoreType.DMA((2,)),
        ],
    )
    return pl.pallas_call(
        functools.partial(_dispatch_body, n_tiles=n_tiles, n_row_tiles=n_row_tiles),
        grid_spec=grid_spec,
        out_shape=[
            jax.ShapeDtypeStruct((t, LANES), F32),
            jax.ShapeDtypeStruct((n_row_tiles * TR_EXPERT, D_MODEL), BF16),
        ],
        compiler_params=pltpu.CompilerParams(
            dimension_semantics=("arbitrary",), vmem_limit_bytes=VMEM_LIMIT),
        name="moe_dispatch",
    )(seg, loc, goff, fill, h2, rt, locrow)


def _expert_body(te_ref, tw_ref, na_ref, xs_ref, w1a_ref, w3a_ref, w2a_ref, w1b_ref, w3b_ref, w2b_ref, y_ref):
    del tw_ref
    r = pl.program_id(0)
    n_active = na_ref[0]
    weights = ((w1a_ref, w3a_ref, w2a_ref), (w1b_ref, w3b_ref, w2b_ref))

    def swiglu(half, rows=None):
        w1_ref, w3_ref, w2_ref = weights[half]
        rows = slice(half * TR_EXPERT, (half + 1) * TR_EXPERT) if rows is None else rows
        x = xs_ref[rows, :]
        a = _silu(jnp.dot(x, w1_ref[...], preferred_element_type=F32))
        a = a * jnp.dot(x, w3_ref[...], preferred_element_type=F32)
        y_ref[rows, :] = jnp.dot(a.astype(BF16), w2_ref[...], preferred_element_type=F32).astype(BF16)

    first = EXPERT_TILES_PER_STEP * r
    both = first + 1 < n_active
    same = te_ref[first] == te_ref[first + 1]

    @pl.when(both & same)
    def _():
        swiglu(0, slice(0, EXPERT_TILES_PER_STEP * TR_EXPERT))

    @pl.when(both & jnp.logical_not(same))
    def _():
        swiglu(0)
        swiglu(1)

    @pl.when(first + 1 == n_active)
    def _():
        swiglu(0)
        y_ref[TR_EXPERT:, :] = jnp.zeros((TR_EXPERT, D_MODEL), BF16)

    @pl.when(first >= n_active)
    def _():
        y_ref[...] = jnp.zeros_like(y_ref)


def _experts(xs, tile_expert, tile_weights, n_active, w1b, w3b, w2b):
    n_rows = xs.shape[0]
    step_rows = EXPERT_TILES_PER_STEP * TR_EXPERT
    x_block = lambda r, te, tw, na: (jnp.minimum(r, (na[0] - 1) // EXPERT_TILES_PER_STEP), 0)
    w_specs = []
    for half in range(EXPERT_TILES_PER_STEP):
        w_block = lambda r, te, tw, na, half=half: (tw[EXPERT_TILES_PER_STEP * r + half], 0, 0)
        w_specs += [pl.BlockSpec((None, D_MODEL, D_EXPERT), w_block),
                    pl.BlockSpec((None, D_MODEL, D_EXPERT), w_block),
                    pl.BlockSpec((None, D_EXPERT, D_MODEL), w_block)]
    grid_spec = pltpu.PrefetchScalarGridSpec(
        num_scalar_prefetch=3,
        grid=(n_rows // step_rows,),
        in_specs=[pl.BlockSpec((step_rows, D_MODEL), x_block)] + w_specs,
        out_specs=pl.BlockSpec((step_rows, D_MODEL), lambda r, te, tw, na: (r, 0)),
    )
    return pl.pallas_call(
        _expert_body,
        grid_spec=grid_spec,
        out_shape=jax.ShapeDtypeStruct((n_rows, D_MODEL), BF16),
        compiler_params=pltpu.CompilerParams(
            dimension_semantics=("arbitrary",), vmem_limit_bytes=VMEM_LIMIT),
        name="moe_experts",
    )(tile_expert, tile_weights, n_active, xs, w1b, w3b, w2b, w1b, w3b, w2b)


def _combine_body(seg_ref, loc_ref, goff_ref, y_hbm, pos_ref, x_ref, gate_ref, fg_ref, o_ref, buf, sem, *,
                  n_tiles, final_norm):
    i = pl.program_id(0)
    slot = i % 2
    start = functools.partial(_segment_copies, seg_ref, loc_ref, goff_ref, hbm_ref=y_hbm, to_hbm=False)

    @pl.when(i == 0)
    def _():
        buf[...] = jnp.zeros_like(buf)
        start(0, buf.at[0], sem=sem.at[0])

    @pl.when(i + 1 < n_tiles)
    def _():
        start(i + 1, buf.at[1 - slot], sem=sem.at[1 - slot])

    _segment_wait(seg_ref, loc_ref, i, buf.at[slot], y_hbm, sem.at[slot], to_hbm=False)
    for part in range(TM_MOE // COMBINE_ROWS):
        rows = slice(part * COMBINE_ROWS, (part + 1) * COMBINE_ROWS)
        pp = pos_ref[rows, :]
        colp = lax.broadcasted_iota(jnp.int32, (COMBINE_ROWS, R_LOC), 1).astype(F32)
        wc = jnp.where(colp == pp[:, 0:1], pp[:, 2:3], jnp.where(colp == pp[:, 1:2], pp[:, 3:4], 0.0))
        moe = jnp.dot(wc.astype(BF16), buf[slot], preferred_element_type=F32)
        x2 = x_ref[rows, :] + gate_ref[...] * moe
        o_ref[rows, :] = _rms(x2) * fg_ref[...] if final_norm else x2


def _combine(y, pos, x1, mod3, final_g, seg, loc, goff, seq, final_norm):
    t = x1.shape[0]
    n_tiles = t // TM_MOE
    tiles_per_seq = seq // TM_MOE
    row = lambda i, *_: (i, 0)
    grid_spec = pltpu.PrefetchScalarGridSpec(
        num_scalar_prefetch=3,
        grid=(n_tiles,),
        in_specs=[
            pl.BlockSpec(memory_space=pl.ANY),
            pl.BlockSpec((TM_MOE, LANES), row),
            pl.BlockSpec((TM_MOE, D_MODEL), row),
            pl.BlockSpec((None, 1, D_MODEL), lambda i, *_: (i // tiles_per_seq, 0, 5)),
            pl.BlockSpec((1, D_MODEL), lambda i, *_: (0, 0)),
        ],
        out_specs=pl.BlockSpec((TM_MOE, D_MODEL), row),
        scratch_shapes=[
            pltpu.VMEM((2, R_LOC, D_MODEL), BF16),
            pltpu.SemaphoreType.DMA((2,)),
        ],
    )
    return pl.pallas_call(
        functools.partial(_combine_body, n_tiles=n_tiles, final_norm=final_norm),
        grid_spec=grid_spec,
        out_shape=jax.ShapeDtypeStruct((t, D_MODEL), F32),
        compiler_params=pltpu.CompilerParams(
            dimension_semantics=("arbitrary",), vmem_limit_bytes=VMEM_LIMIT),
        name="moe_combine",
    )(seg, loc, goff, y, pos, x1, mod3, final_g)


def _moe_plan(counts, n_tiles, n_row_tiles):
    cnt = counts.reshape(n_tiles, SUBLANES, LANES)[:, 0, :N_EXPERTS].astype(jnp.int32)
    seg = (cnt + SEG_ALIGN - 1) // SEG_ALIGN * SEG_ALIGN
    before_e = jnp.arange(N_EXPERTS)[:, None] < jnp.arange(N_EXPERTS)[None, :]
    before_t = jnp.arange(n_tiles)[:, None] > jnp.arange(n_tiles)[None, :]
    loc = jnp.sum(jnp.where(before_e[None], seg[:, :, None], 0), axis=1)
    tot = jnp.sum(seg, axis=0)
    region = (tot + TR_EXPERT - 1) // TR_EXPERT * TR_EXPERT
    gstart = jnp.sum(jnp.where(before_e, region[:, None], 0), axis=0)
    gend = gstart + region
    goff = gstart[None, :] + jnp.sum(jnp.where(before_t[:, :, None], seg[None], 0), axis=1)
    n_active = gend[-1] // TR_EXPERT
    tile_row0 = jnp.arange(n_row_tiles, dtype=jnp.int32) * TR_EXPERT
    last_row0 = (n_active - 1) * TR_EXPERT
    te = jnp.sum(gend[None, :] <= jnp.minimum(tile_row0, last_row0)[:, None], axis=1).astype(jnp.int32)
    te = jnp.minimum(te, N_EXPERTS - 1)
    te2 = te.reshape(-1, EXPERT_TILES_PER_STEP)
    cand = jnp.where(te2[:, 1] != te2[:, 0], te2[:, 1], 0)
    steps = jnp.arange(te2.shape[0])
    held = jnp.max(jnp.where(steps[None, :] <= steps[:, None], cand[None, :], 0), axis=1)
    tw = jnp.stack([te2[:, 0], held], axis=1).reshape(-1).astype(jnp.int32)
    locrow = _pad_cols(loc.astype(F32), LANES).reshape(n_tiles, 1, LANES)
    n_active = n_active.reshape(1).astype(jnp.int32)
    fill = jnp.concatenate([gstart + tot, region - tot, n_active]).astype(jnp.int32)
    return (seg.reshape(-1), loc.reshape(-1).astype(jnp.int32), goff.reshape(-1).astype(jnp.int32), fill,
            locrow, te, tw, n_active)


def _moe_final(x1, mod3, norm2_g, w_router, b_router, w1, w3, w2, final_g, seq, final_norm):
    t = x1.shape[0]
    n_tiles = t // TM_MOE
    max_rows = N_ASSIGN * t + (SEG_ALIGN - 1) * N_EXPERTS * n_tiles + N_EXPERTS * (TR_EXPERT - SEG_ALIGN)
    step_rows = EXPERT_TILES_PER_STEP * TR_EXPERT
    n_row_tiles = -(-max_rows // step_rows) * EXPERT_TILES_PER_STEP
    h2, rt, counts, w1b, w3b, w2b = _router(x1, mod3, norm2_g, w_router, b_router, w1, w3, w2, seq)
    seg, loc, goff, fill, locrow, te, tw, n_active = _moe_plan(counts, n_tiles, n_row_tiles)
    pos, xs = _dispatch(h2, rt, seg, loc, goff, fill, locrow, n_row_tiles)
    y = _experts(xs, te, tw, n_active, w1b, w3b, w2b)
    return _combine(y, pos, x1, mod3, final_g, seg, loc, goff, seq, final_norm)


def _pad_cols(a, width):
    return jnp.pad(a, ((0, 0), (0, width - a.shape[1])))


def kernel(x, c, w_ada, b_ada, norm1_g, w_in, b_in, gmlp_ln_g, gmlp_ln_b, gmlp_w_s, gmlp_b_s, gmlp_out_g, conv_w, conv_b, a_log_f, a_log_b, dt_bias_f, dt_bias_b, d_skip, ssd_norm_g, w_out, norm2_g, w_router_g, b_router_g, w_router_e, b_router_e, w1, w3, w2, final_g):
    bn, seq, _ = x.shape
    depth = w_ada.shape[0]
    x2 = x.reshape(bn * seq, D_MODEL)

    head_rows = jnp.arange(LANES)[:, None]
    head_cols = jnp.arange(SSD_WIDTH)[None, :] // SSD_HEAD_DIM
    expand = jnp.stack([head_rows == head_cols, head_rows == head_cols + SSD_HEADS]).astype(BF16)

    for l in range(depth):
        mod3 = _modulation(c, w_ada, b_ada[l], l).reshape(bn, 1, N_MOD * D_MODEL)

        w_in_p = _pad_cols(w_in[l].astype(BF16), IN_COLS_PAD)
        b_in_p = _pad_cols(b_in[l][None, :], IN_COLS_PAD)
        bs = jnp.repeat(gmlp_b_s[l].T, LANES, axis=1)
        bs_tile = jnp.tile(bs, (TM_PROJ // CHUNK, 1))
        ya, z, xbc, dt = _inproj_gmlp(
            x2, mod3, norm1_g[l][None, :], w_in_p, b_in_p, gmlp_ln_g[l][None, :], gmlp_ln_b[l][None, :],
            gmlp_w_s[l].astype(BF16), bs_tile, gmlp_out_g[l][None, :], seq)

        conv_w8 = jnp.pad(conv_w[l], ((0, SUBLANES - CONV_WIDTH), (0, 0)))
        dtb_row = _pad_cols(jnp.concatenate([dt_bias_f[l], dt_bias_b[l]])[None, :], LANES)
        alog_row = _pad_cols(jnp.concatenate([a_log_f[l], a_log_b[l]])[None, :], LANES)
        dsk_row = jnp.repeat(d_skip[l], SSD_HEAD_DIM)[None, :]
        x2 = _ssd_outproj(xbc, dt, z, ya, x2, mod3, conv_w8, conv_b[l][None, :], dtb_row, alog_row, dsk_row,
                          ssd_norm_g[l][None, :], w_out[l].astype(BF16), expand, bn, seq)

        w_re = jnp.transpose(w_router_e[l], (1, 0, 2)).reshape(D_MODEL, N_EXPERTS)
        w_router = _pad_cols(jnp.concatenate([w_re, w_router_g[l]], axis=1), LANES)
        w_router_hi = w_router.astype(BF16)
        w_router = jnp.concatenate([w_router_hi, (w_router - w_router_hi.astype(F32)).astype(BF16)], axis=1)
        b_router = _pad_cols(jnp.concatenate([b_router_e[l].reshape(-1), b_router_g[l]])[None, :], LANES)
        x2 = _moe_final(x2, mod3, norm2_g[l][None, :], w_router, b_router, w1[l], w3[l], w2[l],
                        final_g[None, :], seq, final_norm=(l == depth - 1))
    return x2.reshape(bn, seq, D_MODEL)
```

```python
import functools
import math

import jax
import jax.numpy as jnp
from jax import lax
from jax.experimental import pallas as pl
from jax.experimental.pallas import tpu as pltpu

F32 = jnp.float32
BF16 = jnp.bfloat16
HIGHEST = lax.Precision.HIGHEST

D_MODEL = 1024
N_MOD = 6
GMLP_WIDTH = 1024
GMLP_HEADS = 8
CHUNK = 128
SSD_STEP_ROWS = 4 * CHUNK
SSD_WIDTH = 1024
SSD_HEADS = 16
SSD_HEAD_DIM = 64
SSD_GROUPS = 2
SSD_STATE = 128
GROUP_WIDTH = SSD_WIDTH // SSD_GROUPS
CONV_WIDTH = 5
CONV_CH = SSD_WIDTH + 2 * SSD_GROUPS * SSD_STATE
N_EGROUPS = 4
EXPERTS_PER_GROUP = 8
N_EXPERTS = 32
D_EXPERT = 256
EPS = 1e-6

LANES = 128
SUBLANES = 8
COL_U, COL_V, COL_Z, COL_XBC, COL_DT = 0, 1024, 2048, 3072, 4608
IN_COLS = 4640
IN_COLS_PAD = COL_DT + LANES
TM_PROJ = 1024
PROJ_ROWS = 512
COMBINE_ROWS = 256
ROUTER_ROWS = 256
TM_MOE = 512
TR_EXPERT = 512
EXPERT_TILES_PER_STEP = 2
N_ASSIGN = 2
SEG_ALIGN = 2 * SUBLANES
R_LOC = N_ASSIGN * TM_MOE + N_EXPERTS * SEG_ALIGN
PIECE_ROWS = 128
REM_SIZES = tuple(PIECE_ROWS >> s for s in range(1, PIECE_ROWS.bit_length()) if PIECE_ROWS >> s >= SEG_ALIGN)
TOTAL_SIZES = tuple(1 << s for s in range(R_LOC.bit_length() - 1, -1, -1) if 1 << s >= SEG_ALIGN)
VMEM_LIMIT = 56 * 1024 * 1024


def _silu(v):
    return v * jax.nn.sigmoid(v)


def _gelu(v):
    return 0.5 * v * (1.0 + lax.erf(v * math.sqrt(0.5)))


def _softplus(v):
    return jnp.maximum(v, 0.0) + jnp.log1p(jnp.exp(-jnp.abs(v)))


def _rms(v):
    return v * lax.rsqrt(jnp.mean(v * v, axis=-1, keepdims=True) + EPS)


def _mod_body(c_ref, w_ref, b_ref, o_ref):
    ca = _silu(c_ref[...])
    o_ref[...] = jnp.dot(ca, w_ref[...], precision=HIGHEST, preferred_element_type=F32) + b_ref[...]


def _modulation(c, w_ada, b_ada, layer):
    bn = c.shape[0]
    return pl.pallas_call(
        _mod_body,
        grid=(N_MOD,),
        in_specs=[
            pl.BlockSpec((bn, D_MODEL), lambda j: (0, 0)),
            pl.BlockSpec((None, D_MODEL, D_MODEL), lambda j: (layer, 0, j)),
            pl.BlockSpec((1, D_MODEL), lambda j: (0, j)),
        ],
        out_specs=pl.BlockSpec((bn, D_MODEL), lambda j: (0, j)),
        out_shape=jax.ShapeDtypeStruct((bn, N_MOD * D_MODEL), F32),
        name="adaln_mod",
    )(c, w_ada, b_ada.reshape(1, -1))


def _inproj_body(x_ref, shift_ref, scale_ref, g_ref, w_ref, b_ref, lng_ref, lnb_ref, ws_ref, bs_ref,
                 og_ref, ya_ref, z_ref, xbc_ref, dt_ref, mix_scr):
    for part in range(TM_PROJ // PROJ_ROWS):
        rows = slice(part * PROJ_ROWS, (part + 1) * PROJ_ROWS)
        h = _rms(x_ref[rows, :]) * g_ref[...]
        h = h * (1.0 + scale_ref[...]) + shift_ref[...]
        hb = h.astype(BF16)

        def proj(lo, hi, hb=hb):
            return jnp.dot(hb, w_ref[:, lo:hi], preferred_element_type=F32) + b_ref[:, lo:hi]

        z_ref[rows, :] = proj(COL_Z, COL_XBC).astype(BF16)
        xbc_ref[rows, :] = proj(COL_XBC, COL_DT)
        dt_ref[rows, :] = proj(COL_DT, IN_COLS_PAD)

        v = _gelu(proj(COL_V, COL_Z))
        mu = jnp.mean(v, axis=-1, keepdims=True)
        vc = v - mu
        var = jnp.mean(vc * vc, axis=-1, keepdims=True)
        vn = (vc * lax.rsqrt(var + EPS) * lng_ref[...] + lnb_ref[...]).astype(BF16)
        n_chunks = PROJ_ROWS // CHUNK
        for hd in range(GMLP_HEADS):
            cols = slice(hd * LANES, (hd + 1) * LANES)
            rhs = jnp.concatenate([vn[c * CHUNK:(c + 1) * CHUNK, cols] for c in range(n_chunks)], axis=1)
            res = jnp.dot(ws_ref[hd], rhs, preferred_element_type=F32)
            for c in range(n_chunks):
                mix_scr[part * PROJ_ROWS + c * CHUNK:part * PROJ_ROWS + (c + 1) * CHUNK, cols] = (
                    res[:, c * LANES:(c + 1) * LANES])
        u = _gelu(proj(COL_U, COL_V))
        out = u * (mix_scr[rows, :] + bs_ref[rows, :])
        ya_ref[rows, :] = (_rms(out) * og_ref[...]).astype(BF16)


def _inproj_gmlp(x2, mod3, norm1_g, w_in_p, b_in_p, ln_g, ln_b, w_s, bs_tile, out_g, seq):
    t = x2.shape[0]
    tiles_per_seq = seq // TM_PROJ
    row = lambda i: (i, 0)
    const2 = lambda i: (0, 0)
    return pl.pallas_call(
        _inproj_body,
        grid=(t // TM_PROJ,),
        in_specs=[
            pl.BlockSpec((TM_PROJ, D_MODEL), row),
            pl.BlockSpec((None, 1, D_MODEL), lambda i: (i // tiles_per_seq, 0, 0)),
            pl.BlockSpec((None, 1, D_MODEL), lambda i: (i // tiles_per_seq, 0, 1)),
            pl.BlockSpec((1, D_MODEL), const2),
            pl.BlockSpec((D_MODEL, IN_COLS_PAD), const2, pipeline_mode=pl.Buffered(1)),
            pl.BlockSpec((1, IN_COLS_PAD), const2),
            pl.BlockSpec((1, GMLP_WIDTH), const2),
            pl.BlockSpec((1, GMLP_WIDTH), const2),
            pl.BlockSpec((GMLP_HEADS, CHUNK, CHUNK), lambda i: (0, 0, 0)),
            pl.BlockSpec((TM_PROJ, GMLP_WIDTH), const2, pipeline_mode=pl.Buffered(1)),
            pl.BlockSpec((1, GMLP_WIDTH), const2),
        ],
        out_specs=[
            pl.BlockSpec((TM_PROJ, GMLP_WIDTH), row),
            pl.BlockSpec((TM_PROJ, SSD_WIDTH), row),
            pl.BlockSpec((TM_PROJ, CONV_CH), row),
            pl.BlockSpec((TM_PROJ, LANES), row),
        ],
        out_shape=[
            jax.ShapeDtypeStruct((t, GMLP_WIDTH), BF16),
            jax.ShapeDtypeStruct((t, SSD_WIDTH), BF16),
            jax.ShapeDtypeStruct((t, CONV_CH), F32),
            jax.ShapeDtypeStruct((t, LANES), F32),
        ],
        scratch_shapes=[pltpu.VMEM((TM_PROJ, GMLP_WIDTH), F32)],
        compiler_params=pltpu.CompilerParams(
            dimension_semantics=("arbitrary",), vmem_limit_bytes=VMEM_LIMIT),
        name="inproj_gmlp",
    )(x2, mod3, mod3, norm1_g, w_in_p, b_in_p, ln_g, ln_b, w_s, bs_tile, out_g)


def _ssd_chunk(act, dtv, a_row, expand, state, rev):
    off = SSD_HEADS if rev else 0
    row = lax.broadcasted_iota(jnp.int32, (CHUNK, CHUNK), 0)
    col = lax.broadcasted_iota(jnp.int32, (CHUNK, CHUNK), 1)
    lower = row >= col
    upper = row <= col
    keep = upper if rev else lower
    da = dtv * a_row
    cs = jnp.dot(keep.astype(F32), da, precision=HIGHEST, preferred_element_type=F32)
    cs_t = cs.T
    dt_t = dtv.T
    tot = cs[0:1, :] if rev else cs[CHUNK - 1:CHUNK, :]

    xs = act[:, :SSD_WIDTH]
    lane = lax.broadcasted_iota(jnp.int32, (CHUNK, LANES), 1)
    first_half = lane < SSD_HEAD_DIM
    zero = jnp.zeros((), BF16)

    stack = jnp.concatenate(
        [jnp.exp(cs), dtv * jnp.exp(tot - cs), jnp.broadcast_to(jnp.exp(tot), (SUBLANES, LANES))], axis=0)
    stack_x = jnp.dot(stack.astype(BF16), expand, preferred_element_type=F32)
    into_x = stack_x[:CHUNK]
    w_x = stack_x[CHUNK:2 * CHUNK]
    cd_x = stack_x[2 * CHUNK:2 * CHUNK + 1]
    xw = xs * w_x.astype(BF16)

    pieces = []
    for g in range(SSD_GROUPS):
        bg = act[:, SSD_WIDTH + g * SSD_STATE:SSD_WIDTH + (g + 1) * SSD_STATE]
        cg = act[:, SSD_WIDTH + SSD_GROUPS * SSD_STATE + g * SSD_STATE:
                 SSD_WIDTH + SSD_GROUPS * SSD_STATE + (g + 1) * SSD_STATE]
        scores = lax.dot_general(cg, bg, (((1,), (1,)), ((), ())), preferred_element_type=F32)
        heads_per_group = SSD_HEADS // SSD_GROUPS
        for pair in range(heads_per_group // 2):
            h0 = g * heads_per_group + 2 * pair
            xs_pair = xs[:, h0 * SSD_HEAD_DIM:(h0 + 2) * SSD_HEAD_DIM]
            y_pair = None
            for k in range(2):
                hh = off + h0 + k
                seg = cs[:, hh:hh + 1] - cs_t[hh:hh + 1, :]
                dec = jnp.exp(jnp.where(keep, seg, -jnp.inf))
                m = (scores * dec * dt_t[hh:hh + 1, :]).astype(BF16)
                rhs = jnp.where(first_half if k == 0 else jnp.logical_not(first_half), xs_pair, zero)
                part = jnp.dot(m, rhs, preferred_element_type=F32)
                y_pair = part if y_pair is None else y_pair + part
            pieces.append(y_pair)
    y_diag = jnp.concatenate(pieces, axis=1)

    y_off = []
    new_state = []
    for g in range(SSD_GROUPS):
        gcols = slice(g * GROUP_WIDTH, (g + 1) * GROUP_WIDTH)
        bg = act[:, SSD_WIDTH + g * SSD_STATE:SSD_WIDTH + (g + 1) * SSD_STATE]
        cg = act[:, SSD_WIDTH + SSD_GROUPS * SSD_STATE + g * SSD_STATE:
                 SSD_WIDTH + SSD_GROUPS * SSD_STATE + (g + 1) * SSD_STATE]
        prev = state[:, gcols]
        y_off.append(jnp.dot(cg, prev.astype(BF16), preferred_element_type=F32))
        bg_t = bg.astype(F32).T.astype(BF16)
        new = jnp.dot(bg_t, xw[:, gcols], preferred_element_type=F32)
        new_state.append(prev * cd_x[:, gcols] + new)
    y = y_diag + jnp.concatenate(y_off, axis=1) * into_x
    return y, jnp.concatenate(new_state, axis=1)


def _ssd_body(xbc_ref, xprev_ref, xnext_ref, dt_ref, z_ref, ya_ref, x_ref, gate_ref, cw_ref, cb_ref,
              dtb_ref, alog_ref, dsk_ref, ng_ref, wout_ref, exp_ref, o_ref,
              act_scr, dts_scr, yf_scr, st_scr, ext_scr, *, n_steps):
    d = pl.program_id(1)
    c = pl.program_id(2)
    lane1 = lax.broadcasted_iota(jnp.int32, (1, LANES), 1)
    a_row = jnp.where(lane1 < 2 * SSD_HEADS, -jnp.exp(alog_ref[...]), 0.0)
    chunk_rows = [slice(j * CHUNK, (j + 1) * CHUNK) for j in range(SSD_STEP_ROWS // CHUNK)]

    @pl.when(c == 0)
    def _():
        st_scr[...] = jnp.zeros_like(st_scr)

    def scan(act, dtv, expand, rev):
        state = st_scr[...]
        ys = [None] * len(chunk_rows)
        for j in (reversed(range(len(chunk_rows))) if rev else range(len(chunk_rows))):
            ys[j], state = _ssd_chunk(act[chunk_rows[j], :], dtv[chunk_rows[j], :], a_row, expand, state, rev)
        st_scr[...] = state
        return jnp.concatenate(ys, axis=0)

    @pl.when(d == 0)
    def _forward():
        row0 = pl.multiple_of(c * SSD_STEP_ROWS, SSD_STEP_ROWS)
        ext_scr[0:SUBLANES, :] = jnp.where(c > 0, xprev_ref[...], 0.0)
        ext_scr[SUBLANES:SUBLANES + SSD_STEP_ROWS, :] = xbc_ref[...]
        ext_scr[SUBLANES + SSD_STEP_ROWS:, :] = jnp.where(c < n_steps - 1, xnext_ref[...], 0.0)
        acc = cb_ref[...]
        for k in range(CONV_WIDTH):
            lo = SUBLANES - CONV_WIDTH // 2 + k
            acc = acc + cw_ref[k:k + 1, :] * ext_scr[lo:lo + SSD_STEP_ROWS, :]
        act = _silu(acc).astype(BF16)
        dtv = _softplus(dt_ref[...] + dtb_ref[...])
        act_scr[pl.ds(row0, SSD_STEP_ROWS), :] = act
        dts_scr[pl.ds(row0, SSD_STEP_ROWS), :] = dtv
        yf_scr[pl.ds(row0, SSD_STEP_ROWS), :] = scan(act, dtv, exp_ref[0], rev=False).astype(BF16)

    @pl.when(d == 1)
    def _backward():
        row0 = pl.multiple_of((n_steps - 1 - c) * SSD_STEP_ROWS, SSD_STEP_ROWS)
        act = act_scr[pl.ds(row0, SSD_STEP_ROWS), :]
        dtv = dts_scr[pl.ds(row0, SSD_STEP_ROWS), :]
        yb = scan(act, dtv, exp_ref[1], rev=True)
        xs = act[:, :SSD_WIDTH].astype(F32)
        y = yf_scr[pl.ds(row0, SSD_STEP_ROWS), :].astype(F32) + yb + dsk_ref[...] * xs
        y = y * _silu(z_ref[...].astype(F32))
        y = jnp.concatenate(
            [_rms(y[:, g * GROUP_WIDTH:(g + 1) * GROUP_WIDTH]) for g in range(SSD_GROUPS)], axis=1)
        y = y * ng_ref[...]
        mix = jnp.concatenate([ya_ref[...], y.astype(BF16)], axis=1)
        o = jnp.dot(mix, wout_ref[...], preferred_element_type=F32)
        o_ref[...] = x_ref[...] + gate_ref[...] * o


def _ssd_outproj(xbc, dt, z, ya, x2, mod3, conv_w8, conv_b, dtb_row, alog_row, dsk_row, norm_g, w_out_b,
                 expand, bn, seq):
    t = x2.shape[0]
    nc = seq // SSD_STEP_ROWS
    blocks8 = SSD_STEP_ROWS // SUBLANES
    last8 = t // SUBLANES - 1

    def fwd_chunk(b, d, c):
        return b * nc + c + d * (nc - 1 - c)

    def bwd_chunk(b, d, c):
        return b * nc + nc - 1 - d * c

    const2 = lambda b, d, c: (0, 0)
    return pl.pallas_call(
        functools.partial(_ssd_body, n_steps=nc),
        grid=(bn, 2, nc),
        in_specs=[
            pl.BlockSpec((SSD_STEP_ROWS, CONV_CH), lambda b, d, c: (fwd_chunk(b, d, c), 0)),
            pl.BlockSpec((SUBLANES, CONV_CH),
                         lambda b, d, c: (jnp.maximum(fwd_chunk(b, d, c) * blocks8 - 1, 0), 0)),
            pl.BlockSpec((SUBLANES, CONV_CH),
                         lambda b, d, c: (jnp.minimum((fwd_chunk(b, d, c) + 1) * blocks8, last8), 0)),
            pl.BlockSpec((SSD_STEP_ROWS, LANES), lambda b, d, c: (fwd_chunk(b, d, c), 0)),
            pl.BlockSpec((SSD_STEP_ROWS, SSD_WIDTH), lambda b, d, c: (bwd_chunk(b, d, c), 0)),
            pl.BlockSpec((SSD_STEP_ROWS, GMLP_WIDTH), lambda b, d, c: (bwd_chunk(b, d, c), 0)),
            pl.BlockSpec((SSD_STEP_ROWS, D_MODEL), lambda b, d, c: (bwd_chunk(b, d, c), 0)),
            pl.BlockSpec((None, 1, D_MODEL), lambda b, d, c: (b, 0, 2)),
            pl.BlockSpec((SUBLANES, CONV_CH), const2),
            pl.BlockSpec((1, CONV_CH), const2),
            pl.BlockSpec((1, LANES), const2),
            pl.BlockSpec((1, LANES), const2),
            pl.BlockSpec((1, SSD_WIDTH), const2),
            pl.BlockSpec((1, SSD_WIDTH), const2),
            pl.BlockSpec((GMLP_WIDTH + SSD_WIDTH, D_MODEL), const2, pipeline_mode=pl.Buffered(1)),
            pl.BlockSpec((2, LANES, SSD_WIDTH), lambda b, d, c: (0, 0, 0), pipeline_mode=pl.Buffered(1)),
        ],
        out_specs=pl.BlockSpec((SSD_STEP_ROWS, D_MODEL), lambda b, d, c: (bwd_chunk(b, d, c), 0)),
        out_shape=jax.ShapeDtypeStruct((t, D_MODEL), F32),
        scratch_shapes=[
            pltpu.VMEM((seq, CONV_CH), BF16),
            pltpu.VMEM((seq, LANES), F32),
            pltpu.VMEM((seq, SSD_WIDTH), BF16),
            pltpu.VMEM((SSD_STATE, SSD_WIDTH), F32),
            pltpu.VMEM((SSD_STEP_ROWS + 2 * SUBLANES, CONV_CH), F32),
        ],
        compiler_params=pltpu.CompilerParams(
            dimension_semantics=("arbitrary", "arbitrary", "arbitrary"), vmem_limit_bytes=VMEM_LIMIT),
        name="ssd_outproj",
    )(xbc, xbc, xbc, dt, z, ya, x2, mod3, conv_w8, conv_b, dtb_row, alog_row, dsk_row, norm_g, w_out_b,
      expand)


def _route(logits):
    lane = lax.broadcasted_iota(jnp.int32, logits.shape, 1)
    big = jnp.int32(LANES)
    neg = -jnp.inf
    gmask = (lane >= N_EXPERTS) & (lane < N_EXPERTS + N_EGROUPS)
    gl = jnp.where(gmask, logits, neg)
    gmax = jnp.max(gl, axis=-1, keepdims=True)
    gidx = jnp.min(jnp.where(gl == gmax, lane, big), axis=-1, keepdims=True) - N_EXPERTS
    p_g = 1.0 / jnp.sum(jnp.where(gmask, jnp.exp(gl - gmax), 0.0), axis=-1, keepdims=True)
    lo = gidx * EXPERTS_PER_GROUP
    emask = (lane >= lo) & (lane < lo + EXPERTS_PER_GROUP)
    el = jnp.where(emask, logits, neg)
    v1 = jnp.max(el, axis=-1, keepdims=True)
    i1 = jnp.min(jnp.where(el == v1, lane, big), axis=-1, keepdims=True)
    el2 = jnp.where(lane == i1, neg, el)
    v2 = jnp.max(el2, axis=-1, keepdims=True)
    i2 = jnp.min(jnp.where(el2 == v2, lane, big), axis=-1, keepdims=True)
    e2 = jnp.exp(v2 - v1)
    den = 1.0 + e2
    w1 = p_g / den
    w2 = p_g * e2 / den
    return i1, i2, w1, w2


def _router_body(x_ref, shift_ref, scale_ref, g2_ref, wr_ref, br_ref, w1_ref, w3_ref, w2_ref,
                 h_ref, rt_ref, cnt_ref, w1b_ref, w3b_ref, w2b_ref):
    w1b_ref[...] = w1_ref[...].astype(BF16)
    w3b_ref[...] = w3_ref[...].astype(BF16)
    w2b_ref[...] = w2_ref[...].astype(BF16)
    counts = jnp.zeros((1, LANES), F32)
    for part in range(TM_MOE // ROUTER_ROWS):
        rows = slice(part * ROUTER_ROWS, (part + 1) * ROUTER_ROWS)
        h = _rms(x_ref[rows, :]) * g2_ref[...]
        h = h * (1.0 + scale_ref[...]) + shift_ref[...]
        hb = h.astype(BF16)
        h_ref[rows, :] = hb
        h_lo = (h - hb.astype(F32)).astype(BF16)
        both = jnp.dot(hb, wr_ref[...], preferred_element_type=F32)
        logits = (both[:, :LANES] + both[:, LANES:]
                  + jnp.dot(h_lo, wr_ref[:, :LANES], preferred_element_type=F32)) + br_ref[...]
        i1, i2, w1, w2 = _route(logits)
        lane = lax.broadcasted_iota(jnp.int32, (ROUTER_ROWS, LANES), 1)
        rt_ref[rows, :] = jnp.where(lane == 0, i1.astype(F32), jnp.where(lane == 1, i2.astype(F32),
                                    jnp.where(lane == 2, w1, jnp.where(lane == 3, w2, 0.0))))
        mask = jnp.where(lane == i1, 1.0, jnp.where(lane == i2, 1.0, 0.0))
        counts = counts + jnp.sum(mask, axis=0, keepdims=True)
    cnt_ref[...] = jnp.broadcast_to(counts, (SUBLANES, LANES))


def _router(x1, mod3, norm2_g, w_router, b_router, w1, w3, w2, seq):
    t = x1.shape[0]
    tiles_per_seq = seq // TM_MOE
    n_tiles = t // TM_MOE
    up_rows = N_EXPERTS * D_MODEL // n_tiles
    down_rows = N_EXPERTS * D_EXPERT // n_tiles
    assert up_rows * n_tiles == N_EXPERTS * D_MODEL and up_rows % SEG_ALIGN == 0
    assert down_rows * n_tiles == N_EXPERTS * D_EXPERT and down_rows % SEG_ALIGN == 0
    row = lambda i: (i, 0)
    const2 = lambda i: (0, 0)
    outs = pl.pallas_call(
        _router_body,
        grid=(n_tiles,),
        in_specs=[
            pl.BlockSpec((TM_MOE, D_MODEL), row),
            pl.BlockSpec((None, 1, D_MODEL), lambda i: (i // tiles_per_seq, 0, 3)),
            pl.BlockSpec((None, 1, D_MODEL), lambda i: (i // tiles_per_seq, 0, 4)),
            pl.BlockSpec((1, D_MODEL), const2),
            pl.BlockSpec((D_MODEL, 2 * LANES), const2),
            pl.BlockSpec((1, LANES), const2),
            pl.BlockSpec((up_rows, D_EXPERT), row),
            pl.BlockSpec((up_rows, D_EXPERT), row),
            pl.BlockSpec((down_rows, D_MODEL), row),
        ],
        out_specs=[
            pl.BlockSpec((TM_MOE, D_MODEL), row),
            pl.BlockSpec((TM_MOE, LANES), row),
            pl.BlockSpec((SUBLANES, LANES), row),
            pl.BlockSpec((up_rows, D_EXPERT), row),
            pl.BlockSpec((up_rows, D_EXPERT), row),
            pl.BlockSpec((down_rows, D_MODEL), row),
        ],
        out_shape=[
            jax.ShapeDtypeStruct((t, D_MODEL), BF16),
            jax.ShapeDtypeStruct((t, LANES), F32),
            jax.ShapeDtypeStruct((n_tiles * SUBLANES, LANES), F32),
            jax.ShapeDtypeStruct((N_EXPERTS * D_MODEL, D_EXPERT), BF16),
            jax.ShapeDtypeStruct((N_EXPERTS * D_MODEL, D_EXPERT), BF16),
            jax.ShapeDtypeStruct((N_EXPERTS * D_EXPERT, D_MODEL), BF16),
        ],
        compiler_params=pltpu.CompilerParams(
            dimension_semantics=("arbitrary",), vmem_limit_bytes=VMEM_LIMIT),
        name="moe_router",
    )(x1, mod3, mod3, norm2_g, w_router, b_router, w1.reshape(-1, D_EXPERT), w3.reshape(-1, D_EXPERT),
      w2.reshape(-1, D_MODEL))
    h2, rt, counts, w1b, w3b, w2b = outs
    return (h2, rt, counts, w1b.reshape(N_EXPERTS, D_MODEL, D_EXPERT), w3b.reshape(N_EXPERTS, D_MODEL, D_EXPERT),
            w2b.reshape(N_EXPERTS, D_EXPERT, D_MODEL))


def _row_copy(vmem_buf, v_start, hbm_ref, g_start, size, sem, to_hbm):
    v = vmem_buf.at[pl.ds(v_start, size), :]
    g = hbm_ref.at[pl.ds(g_start, size), :]
    return pltpu.make_async_copy(v, g, sem) if to_hbm else pltpu.make_async_copy(g, v, sem)


def _copy_rows(n, vmem_buf, v0, hbm_ref, g0, sem, *, to_hbm, wait):
    def run(v_off, g_off, size):
        v_start = 0 if v0 is None else pl.multiple_of(v0 + v_off, SEG_ALIGN)
        cp = _row_copy(vmem_buf, v_start, hbm_ref, pl.multiple_of(g0 + g_off, SEG_ALIGN), size, sem, to_hbm)
        if wait:
            cp.wait()
        else:
            cp.start()

    n_big = n // PIECE_ROWS

    def big_piece(k, carry):
        run(k * PIECE_ROWS, k * PIECE_ROWS, PIECE_ROWS)
        return carry

    lax.fori_loop(0, n_big, big_piece, 0)
    base = n_big * PIECE_ROWS
    rem = n - base
    for size in REM_SIZES:
        done = base + (rem // (2 * size)) * (2 * size)

        @pl.when((rem & size) != 0)
        def _():
            run(done, done, size)


def _segment_copies(seg_ref, loc_ref, goff_ref, tile, vmem_buf, hbm_ref, sem, *, to_hbm):
    base = tile * N_EXPERTS

    def per_expert(e, carry):
        _copy_rows(seg_ref[base + e], vmem_buf, loc_ref[base + e], hbm_ref, goff_ref[base + e], sem,
                   to_hbm=to_hbm, wait=False)
        return carry

    lax.fori_loop(0, N_EXPERTS, per_expert, 0)


def _tile_rows(seg_ref, loc_ref, tile):
    last = tile * N_EXPERTS + N_EXPERTS - 1
    return loc_ref[last] + seg_ref[last]


def _segment_wait(seg_ref, loc_ref, tile, vmem_buf, hbm_ref, sem, *, to_hbm):
    total = _tile_rows(seg_ref, loc_ref, tile)
    for size in TOTAL_SIZES:
        @pl.when((total & size) != 0)
        def _():
            _row_copy(vmem_buf, 0, hbm_ref, 0, size, sem, to_hbm).wait()


def _dispatch_body(seg_ref, loc_ref, goff_ref, fill_ref, h_ref, rt_ref, locrow_ref, pos_ref, xs_hbm, buf, sem,
                   *, n_tiles, n_row_tiles):
    i = pl.program_id(0)
    slot = i % 2

    start = functools.partial(_segment_copies, seg_ref, loc_ref, goff_ref, hbm_ref=xs_hbm, to_hbm=True)
    wait = functools.partial(_segment_wait, seg_ref, loc_ref, hbm_ref=xs_hbm, to_hbm=True)

    @pl.when(i >= 2)
    def _():
        wait(i - 2, buf.at[slot], sem=sem.at[slot])

    rt = rt_ref[...]
    lane = lax.broadcasted_iota(jnp.int32, (TM_MOE, LANES), 1)
    lanef = lane.astype(F32)
    e1 = rt[:, 0:1]
    e2 = rt[:, 1:2]
    mask = jnp.where(lanef == e1, 1.0, jnp.where(lanef == e2, 1.0, 0.0))
    r = lax.broadcasted_iota(jnp.int32, (TM_MOE, TM_MOE), 0)
    cc = lax.broadcasted_iota(jnp.int32, (TM_MOE, TM_MOE), 1)
    strict = jnp.where(r > cc, 1.0, 0.0).astype(BF16)
    rank = jnp.dot(strict, mask.astype(BF16), preferred_element_type=F32)
    posall = rank + locrow_ref[...]
    pos1 = jnp.sum(jnp.where(lanef == e1, posall, 0.0), axis=-1, keepdims=True)
    pos2 = jnp.sum(jnp.where(lanef == e2, posall, 0.0), axis=-1, keepdims=True)
    pp = jnp.where(lane == 0, pos1, jnp.where(lane == 1, pos2, rt))
    pos_ref[...] = pp

    slots_t = [pp[k * LANES:(k + 1) * LANES, :].T for k in range(TM_MOE // LANES)]

    rowp = lax.broadcasted_iota(jnp.int32, (R_LOC, LANES), 0).astype(F32)
    perm = jnp.concatenate(
        [jnp.where(rowp == st[0:1, :], 1.0, jnp.where(rowp == st[1:2, :], 1.0, 0.0)).astype(BF16)
         for st in slots_t], axis=1)
    buf[slot] = jnp.dot(perm, h_ref[...], preferred_element_type=F32).astype(BF16)
    start(i, buf.at[slot], sem=sem.at[slot])

    @pl.when(i == n_tiles - 1)
    def _():
        if n_tiles > 1:
            wait(i - 1, buf.at[1 - slot], sem=sem.at[1 - slot])
        wait(i, buf.at[slot], sem=sem.at[slot])
        zsrc = buf.at[1 - slot]
        zsem = sem.at[1 - slot]
        zsrc[0:TR_EXPERT, :] = jnp.zeros((TR_EXPERT, D_MODEL), BF16)
        for waiting in (False, True):
            def per_expert(e, carry):
                _copy_rows(fill_ref[N_EXPERTS + e], zsrc, None, xs_hbm, fill_ref[e], zsem,
                           to_hbm=True, wait=waiting)
                return carry

            def per_row_tile(k, carry):
                cp = _row_copy(zsrc, 0, xs_hbm, pl.multiple_of(k * TR_EXPERT, TR_EXPERT), TR_EXPERT, zsem, True)
                if waiting:
                    cp.wait()
                else:
                    cp.start()
                return carry

            lax.fori_loop(0, N_EXPERTS, per_expert, 0)
            lax.fori_loop(fill_ref[2 * N_EXPERTS], n_row_tiles, per_row_tile, 0)


def _dispatch(h2, rt, seg, loc, goff, fill, locrow, n_row_tiles):
    t = h2.shape[0]
    n_tiles = t // TM_MOE
    row = lambda i, *_: (i, 0)
    grid_spec = pltpu.PrefetchScalarGridSpec(
        num_scalar_prefetch=4,
        grid=(n_tiles,),
        in_specs=[
            pl.BlockSpec((TM_MOE, D_MODEL), row),
            pl.BlockSpec((TM_MOE, LANES), row),
            pl.BlockSpec((None, 1, LANES), lambda i, *_: (i, 0, 0)),
        ],
        out_specs=[
            pl.BlockSpec((TM_MOE, LANES), row),
            pl.BlockSpec(memory_space=pl.ANY),
        ],
        scratch_shapes=[
            pltpu.VMEM((2, R_LOC, D_MODEL), BF16),
            pltpu.SemaphoreType.DMA((2,)),
        ],
    )
    return pl.pallas_call(
        functools.partial(_dispatch_body, n_tiles=n_tiles, n_row_tiles=n_row_tiles),
        grid_spec=grid_spec,
        out_shape=[
            jax.ShapeDtypeStruct((t, LANES), F32),
            jax.ShapeDtypeStruct((n_row_tiles * TR_EXPERT, D_MODEL), BF16),
        ],
        compiler_params=pltpu.CompilerParams(
            dimension_semantics=("arbitrary",), vmem_limit_bytes=VMEM_LIMIT),
        name="moe_dispatch",
    )(seg, loc, goff, fill, h2, rt, locrow)


def _expert_body(te_ref, tw_ref, na_ref, xs_ref, w1a_ref, w3a_ref, w2a_ref, w1b_ref, w3b_ref, w2b_ref, y_ref):
    del tw_ref
    r = pl.program_id(0)
    n_active = na_ref[0]
    weights = ((w1a_ref, w3a_ref, w2a_ref), (w1b_ref, w3b_ref, w2b_ref))

    def swiglu(half, rows=None):
        w1_ref, w3_ref, w2_ref = weights[half]
        rows = slice(half * TR_EXPERT, (half + 1) * TR_EXPERT) if rows is None else rows
        x = xs_ref[rows, :]
        a = _silu(jnp.dot(x, w1_ref[...], preferred_element_type=F32))
        a = a * jnp.dot(x, w3_ref[...], preferred_element_type=F32)
        y_ref[rows, :] = jnp.dot(a.astype(BF16), w2_ref[...], preferred_element_type=F32).astype(BF16)

    first = EXPERT_TILES_PER_STEP * r
    both = first + 1 < n_active
    same = te_ref[first] == te_ref[first + 1]

    @pl.when(both & same)
    def _():
        swiglu(0, slice(0, EXPERT_TILES_PER_STEP * TR_EXPERT))

    @pl.when(both & jnp.logical_not(same))
    def _():
        swiglu(0)
        swiglu(1)

    @pl.when(first + 1 == n_active)
    def _():
        swiglu(0)
        y_ref[TR_EXPERT:, :] = jnp.zeros((TR_EXPERT, D_MODEL), BF16)

    @pl.when(first >= n_active)
    def _():
        y_ref[...] = jnp.zeros_like(y_ref)


def _experts(xs, tile_expert, tile_weights, n_active, w1b, w3b, w2b):
    n_rows = xs.shape[0]
    step_rows = EXPERT_TILES_PER_STEP * TR_EXPERT
    x_block = lambda r, te, tw, na: (jnp.minimum(r, (na[0] - 1) // EXPERT_TILES_PER_STEP), 0)
    w_specs = []
    for half in range(EXPERT_TILES_PER_STEP):
        w_block = lambda r, te, tw, na, half=half: (tw[EXPERT_TILES_PER_STEP * r + half], 0, 0)
        w_specs += [pl.BlockSpec((None, D_MODEL, D_EXPERT), w_block),
                    pl.BlockSpec((None, D_MODEL, D_EXPERT), w_block),
                    pl.BlockSpec((None, D_EXPERT, D_MODEL), w_block)]
    grid_spec = pltpu.PrefetchScalarGridSpec(
        num_scalar_prefetch=3,
        grid=(n_rows // step_rows,),
        in_specs=[pl.BlockSpec((step_rows, D_MODEL), x_block)] + w_specs,
        out_specs=pl.BlockSpec((step_rows, D_MODEL), lambda r, te, tw, na: (r, 0)),
    )
    return pl.pallas_call(
        _expert_body,
        grid_spec=grid_spec,
        out_shape=jax.ShapeDtypeStruct((n_rows, D_MODEL), BF16),
        compiler_params=pltpu.CompilerParams(
            dimension_semantics=("arbitrary",), vmem_limit_bytes=VMEM_LIMIT),
        name="moe_experts",
    )(tile_expert, tile_weights, n_active, xs, w1b, w3b, w2b, w1b, w3b, w2b)


def _combine_body(seg_ref, loc_ref, goff_ref, y_hbm, pos_ref, x_ref, gate_ref, fg_ref, o_ref, buf, sem, *,
                  n_tiles, final_norm):
    i = pl.program_id(0)
    slot = i % 2
    start = functools.partial(_segment_copies, seg_ref, loc_ref, goff_ref, hbm_ref=y_hbm, to_hbm=False)

    @pl.when(i == 0)
    def _():
        buf[...] = jnp.zeros_like(buf)
        start(0, buf.at[0], sem=sem.at[0])

    @pl.when(i + 1 < n_tiles)
    def _():
        start(i + 1, buf.at[1 - slot], sem=sem.at[1 - slot])

    _segment_wait(seg_ref, loc_ref, i, buf.at[slot], y_hbm, sem.at[slot], to_hbm=False)
    for part in range(TM_MOE // COMBINE_ROWS):
        rows = slice(part * COMBINE_ROWS, (part + 1) * COMBINE_ROWS)
        pp = pos_ref[rows, :]
        colp = lax.broadcasted_iota(jnp.int32, (COMBINE_ROWS, R_LOC), 1).astype(F32)
        wc = jnp.where(colp == pp[:, 0:1], pp[:, 2:3], jnp.where(colp == pp[:, 1:2], pp[:, 3:4], 0.0))
        moe = jnp.dot(wc.astype(BF16), buf[slot], preferred_element_type=F32)
        x2 = x_ref[rows, :] + gate_ref[...] * moe
        o_ref[rows, :] = _rms(x2) * fg_ref[...] if final_norm else x2


def _combine(y, pos, x1, mod3, final_g, seg, loc, goff, seq, final_norm):
    t = x1.shape[0]
    n_tiles = t // TM_MOE
    tiles_per_seq = seq // TM_MOE
    row = lambda i, *_: (i, 0)
    grid_spec = pltpu.PrefetchScalarGridSpec(
        num_scalar_prefetch=3,
        grid=(n_tiles,),
        in_specs=[
            pl.BlockSpec(memory_space=pl.ANY),
            pl.BlockSpec((TM_MOE, LANES), row),
            pl.BlockSpec((TM_MOE, D_MODEL), row),
            pl.BlockSpec((None, 1, D_MODEL), lambda i, *_: (i // tiles_per_seq, 0, 5)),
            pl.BlockSpec((1, D_MODEL), lambda i, *_: (0, 0)),
        ],
        out_specs=pl.BlockSpec((TM_MOE, D_MODEL), row),
        scratch_shapes=[
            pltpu.VMEM((2, R_LOC, D_MODEL), BF16),
            pltpu.SemaphoreType.DMA((2,)),
        ],
    )
    return pl.pallas_call(
        functools.partial(_combine_body, n_tiles=n_tiles, final_norm=final_norm),
        grid_spec=grid_spec,
        out_shape=jax.ShapeDtypeStruct((t, D_MODEL), F32),
        compiler_params=pltpu.CompilerParams(
            dimension_semantics=("arbitrary",), vmem_limit_bytes=VMEM_LIMIT),
        name="moe_combine",
    )(seg, loc, goff, y, pos, x1, mod3, final_g)


def _moe_plan(counts, n_tiles, n_row_tiles):
    cnt = counts.reshape(n_tiles, SUBLANES, LANES)[:, 0, :N_EXPERTS].astype(jnp.int32)
    seg = (cnt + SEG_ALIGN - 1) // SEG_ALIGN * SEG_ALIGN
    before_e = jnp.arange(N_EXPERTS)[:, None] < jnp.arange(N_EXPERTS)[None, :]
    before_t = jnp.arange(n_tiles)[:, None] > jnp.arange(n_tiles)[None, :]
    loc = jnp.sum(jnp.where(before_e[None], seg[:, :, None], 0), axis=1)
    tot = jnp.sum(seg, axis=0)
    region = (tot + TR_EXPERT - 1) // TR_EXPERT * TR_EXPERT
    gstart = jnp.sum(jnp.where(before_e, region[:, None], 0), axis=0)
    gend = gstart + region
    goff = gstart[None, :] + jnp.sum(jnp.where(before_t[:, :, None], seg[None], 0), axis=1)
    n_active = gend[-1] // TR_EXPERT
    tile_row0 = jnp.arange(n_row_tiles, dtype=jnp.int32) * TR_EXPERT
    last_row0 = (n_active - 1) * TR_EXPERT
    te = jnp.sum(gend[None, :] <= jnp.minimum(tile_row0, last_row0)[:, None], axis=1).astype(jnp.int32)
    te = jnp.minimum(te, N_EXPERTS - 1)
    te2 = te.reshape(-1, EXPERT_TILES_PER_STEP)
    cand = jnp.where(te2[:, 1] != te2[:, 0], te2[:, 1], 0)
    steps = jnp.arange(te2.shape[0])
    held = jnp.max(jnp.where(steps[None, :] <= steps[:, None], cand[None, :], 0), axis=1)
    tw = jnp.stack([te2[:, 0], held], axis=1).reshape(-1).astype(jnp.int32)
    locrow = _pad_cols(loc.astype(F32), LANES).reshape(n_tiles, 1, LANES)
    n_active = n_active.reshape(1).astype(jnp.int32)
    fill = jnp.concatenate([gstart + tot, region - tot, n_active]).astype(jnp.int32)
    return (seg.reshape(-1), loc.reshape(-1).astype(jnp.int32), goff.reshape(-1).astype(jnp.int32), fill,
            locrow, te, tw, n_active)


def _moe_final(x1, mod3, norm2_g, w_router, b_router, w1, w3, w2, final_g, seq, final_norm):
    t = x1.shape[0]
    n_tiles = t // TM_MOE
    max_rows = N_ASSIGN * t + (SEG_ALIGN - 1) * N_EXPERTS * n_tiles + N_EXPERTS * (TR_EXPERT - SEG_ALIGN)
    step_rows = EXPERT_TILES_PER_STEP * TR_EXPERT
    n_row_tiles = -(-max_rows // step_rows) * EXPERT_TILES_PER_STEP
    h2, rt, counts, w1b, w3b, w2b = _router(x1, mod3, norm2_g, w_router, b_router, w1, w3, w2, seq)
    seg, loc, goff, fill, locrow, te, tw, n_active = _moe_plan(counts, n_tiles, n_row_tiles)
    pos, xs = _dispatch(h2, rt, seg, loc, goff, fill, locrow, n_row_tiles)
    y = _experts(xs, te, tw, n_active, w1b, w3b, w2b)
    return _combine(y, pos, x1, mod3, final_g, seg, loc, goff, seq, final_norm)


def _pad_cols(a, width):
    return jnp.pad(a, ((0, 0), (0, width - a.shape[1])))


def kernel(x, c, w_ada, b_ada, norm1_g, w_in, b_in, gmlp_ln_g, gmlp_ln_b, gmlp_w_s, gmlp_b_s, gmlp_out_g, conv_w, conv_b, a_log_f, a_log_b, dt_bias_f, dt_bias_b, d_skip, ssd_norm_g, w_out, norm2_g, w_router_g, b_router_g, w_router_e, b_router_e, w1, w3, w2, final_g):
    bn, seq, _ = x.shape
    depth = w_ada.shape[0]
    x2 = x.reshape(bn * seq, D_MODEL)

    head_rows = jnp.arange(LANES)[:, None]
    head_cols = jnp.arange(SSD_WIDTH)[None, :] // SSD_HEAD_DIM
    expand = jnp.stack([head_rows == head_cols, head_rows == head_cols + SSD_HEADS]).astype(BF16)

    for l in range(depth):
        mod3 = _modulation(c, w_ada, b_ada[l], l).reshape(bn, 1, N_MOD * D_MODEL)

        w_in_p = _pad_cols(w_in[l].astype(BF16), IN_COLS_PAD)
        b_in_p = _pad_cols(b_in[l][None, :], IN_COLS_PAD)
        bs = jnp.repeat(gmlp_b_s[l].T, LANES, axis=1)
        bs_tile = jnp.tile(bs, (TM_PROJ // CHUNK, 1))
        ya, z, xbc, dt = _inproj_gmlp(
            x2, mod3, norm1_g[l][None, :], w_in_p, b_in_p, gmlp_ln_g[l][None, :], gmlp_ln_b[l][None, :],
            gmlp_w_s[l].astype(BF16), bs_tile, gmlp_out_g[l][None, :], seq)

        conv_w8 = jnp.pad(conv_w[l], ((0, SUBLANES - CONV_WIDTH), (0, 0)))
        dtb_row = _pad_cols(jnp.concatenate([dt_bias_f[l], dt_bias_b[l]])[None, :], LANES)
        alog_row = _pad_cols(jnp.concatenate([a_log_f[l], a_log_b[l]])[None, :], LANES)
        dsk_row = jnp.repeat(d_skip[l], SSD_HEAD_DIM)[None, :]
        x2 = _ssd_outproj(xbc, dt, z, ya, x2, mod3, conv_w8, conv_b[l][None, :], dtb_row, alog_row, dsk_row,
                          ssd_norm_g[l][None, :], w_out[l].astype(BF16), expand, bn, seq)

        w_re = jnp.transpose(w_router_e[l], (1, 0, 2)).reshape(D_MODEL, N_EXPERTS)
        w_router = _pad_cols(jnp.concatenate([w_re, w_router_g[l]], axis=1), LANES)
        w_router_hi = w_router.astype(BF16)
        w_router = jnp.concatenate([w_router_hi, (w_router - w_router_hi.astype(F32)).astype(BF16)], axis=1)
        b_router = _pad_cols(jnp.concatenate([b_router_e[l].reshape(-1), b_router_g[l]])[None, :], LANES)
        x2 = _moe_final(x2, mod3, norm2_g[l][None, :], w_router, b_router, w1[l], w3[l], w2[l],
                        final_g[None, :], seq, final_norm=(l == depth - 1))
    return x2.reshape(bn, seq, D_MODEL)
```

```python
import functools
import math

import jax
import jax.numpy as jnp
from jax import lax
from jax.experimental import pallas as pl
from jax.experimental.pallas import tpu as pltpu

F32 = jnp.float32
BF16 = jnp.bfloat16
HIGHEST = lax.Precision.HIGHEST

D_MODEL = 1024
N_MOD = 6
GMLP_WIDTH = 1024
GMLP_HEADS = 8
CHUNK = 128
SSD_STEP_ROWS = 4 * CHUNK
SSD_WIDTH = 1024
SSD_HEADS = 16
SSD_HEAD_DIM = 64
SSD_GROUPS = 2
SSD_STATE = 128
GROUP_WIDTH = SSD_WIDTH // SSD_GROUPS
CONV_WIDTH = 5
CONV_CH = SSD_WIDTH + 2 * SSD_GROUPS * SSD_STATE
N_EGROUPS = 4
EXPERTS_PER_GROUP = 8
N_EXPERTS = 32
D_EXPERT = 256
EPS = 1e-6

LANES = 128
SUBLANES = 8
COL_U, COL_V, COL_Z, COL_XBC, COL_DT = 0, 1024, 2048, 3072, 4608
IN_COLS = 4640
IN_COLS_PAD = COL_DT + LANES
TM_PROJ = 1024
PROJ_ROWS = 512
COMBINE_ROWS = 256
ROUTER_ROWS = 256
TM_MOE = 512
TR_EXPERT = 512
EXPERT_TILES_PER_STEP = 2
N_ASSIGN = 2
SEG_ALIGN = 2 * SUBLANES
R_LOC = N_ASSIGN * TM_MOE + N_EXPERTS * SEG_ALIGN
PIECE_ROWS = 128
REM_SIZES = tuple(PIECE_ROWS >> s for s in range(1, PIECE_ROWS.bit_length()) if PIECE_ROWS >> s >= SEG_ALIGN)
TOTAL_SIZES = tuple(1 << s for s in range(R_LOC.bit_length() - 1, -1, -1) if 1 << s >= SEG_ALIGN)
VMEM_LIMIT = 56 * 1024 * 1024


def _silu(v):
    return v * jax.nn.sigmoid(v)


def _gelu(v):
    return 0.5 * v * (1.0 + lax.erf(v * math.sqrt(0.5)))


def _softplus(v):
    return jnp.maximum(v, 0.0) + jnp.log1p(jnp.exp(-jnp.abs(v)))


def _rms(v):
    return v * lax.rsqrt(jnp.mean(v * v, axis=-1, keepdims=True) + EPS)


def _mod_body(c_ref, w_ref, b_ref, o_ref):
    ca = _silu(c_ref[...])
    o_ref[...] = jnp.dot(ca, w_ref[...], precision=HIGHEST, preferred_element_type=F32) + b_ref[...]


def _modulation(c, w_ada, b_ada, layer):
    bn = c.shape[0]
    return pl.pallas_call(
        _mod_body,
        grid=(N_MOD,),
        in_specs=[
            pl.BlockSpec((bn, D_MODEL), lambda j: (0, 0)),
            pl.BlockSpec((None, D_MODEL, D_MODEL), lambda j: (layer, 0, j)),
            pl.BlockSpec((1, D_MODEL), lambda j: (0, j)),
        ],
        out_specs=pl.BlockSpec((bn, D_MODEL), lambda j: (0, j)),
        out_shape=jax.ShapeDtypeStruct((bn, N_MOD * D_MODEL), F32),
        name="adaln_mod",
    )(c, w_ada, b_ada.reshape(1, -1))


def _inproj_body(x_ref, shift_ref, scale_ref, g_ref, w_ref, b_ref, lng_ref, lnb_ref, ws_ref, bs_ref,
                 og_ref, ya_ref, z_ref, xbc_ref, dt_ref, mix_scr):
    for part in range(TM_PROJ // PROJ_ROWS):
        rows = slice(part * PROJ_ROWS, (part + 1) * PROJ_ROWS)
        h = _rms(x_ref[rows, :]) * g_ref[...]
        h = h * (1.0 + scale_ref[...]) + shift_ref[...]
        hb = h.astype(BF16)

        def proj(lo, hi, hb=hb):
            return jnp.dot(hb, w_ref[:, lo:hi], preferred_element_type=F32) + b_ref[:, lo:hi]

        z_ref[rows, :] = proj(COL_Z, COL_XBC).astype(BF16)
        xbc_ref[rows, :] = proj(COL_XBC, COL_DT)
        dt_ref[rows, :] = proj(COL_DT, IN_COLS_PAD)

        v = _gelu(proj(COL_V, COL_Z))
        mu = jnp.mean(v, axis=-1, keepdims=True)
        vc = v - mu
        var = jnp.mean(vc * vc, axis=-1, keepdims=True)
        vn = (vc * lax.rsqrt(var + EPS) * lng_ref[...] + lnb_ref[...]).astype(BF16)
        n_chunks = PROJ_ROWS // CHUNK
        for hd in range(GMLP_HEADS):
            cols = slice(hd * LANES, (hd + 1) * LANES)
            rhs = jnp.concatenate([vn[c * CHUNK:(c + 1) * CHUNK, cols] for c in range(n_chunks)], axis=1)
            res = jnp.dot(ws_ref[hd], rhs, preferred_element_type=F32)
            for c in range(n_chunks):
                mix_scr[part * PROJ_ROWS + c * CHUNK:part * PROJ_ROWS + (c + 1) * CHUNK, cols] = (
                    res[:, c * LANES:(c + 1) * LANES])
        u = _gelu(proj(COL_U, COL_V))
        out = u * (mix_scr[rows, :] + bs_ref[rows, :])
        ya_ref[rows, :] = (_rms(out) * og_ref[...]).astype(BF16)


def _inproj_gmlp(x2, mod3, norm1_g, w_in_p, b_in_p, ln_g, ln_b, w_s, bs_tile, out_g, seq):
    t = x2.shape[0]
    tiles_per_seq = seq // TM_PROJ
    row = lambda i: (i, 0)
    const2 = lambda i: (0, 0)
    return pl.pallas_call(
        _inproj_body,
        grid=(t // TM_PROJ,),
        in_specs=[
            pl.BlockSpec((TM_PROJ, D_MODEL), row),
            pl.BlockSpec((None, 1, D_MODEL), lambda i: (i // tiles_per_seq, 0, 0)),
            pl.BlockSpec((None, 1, D_MODEL), lambda i: (i // tiles_per_seq, 0, 1)),
            pl.BlockSpec((1, D_MODEL), const2),
            pl.BlockSpec((D_MODEL, IN_COLS_PAD), const2, pipeline_mode=pl.Buffered(1)),
            pl.BlockSpec((1, IN_COLS_PAD), const2),
            pl.BlockSpec((1, GMLP_WIDTH), const2),
            pl.BlockSpec((1, GMLP_WIDTH), const2),
            pl.BlockSpec((GMLP_HEADS, CHUNK, CHUNK), lambda i: (0, 0, 0)),
            pl.BlockSpec((TM_PROJ, GMLP_WIDTH), const2, pipeline_mode=pl.Buffered(1)),
            pl.BlockSpec((1, GMLP_WIDTH), const2),
        ],
        out_specs=[
            pl.BlockSpec((TM_PROJ, GMLP_WIDTH), row),
            pl.BlockSpec((TM_PROJ, SSD_WIDTH), row),
            pl.BlockSpec((TM_PROJ, CONV_CH), row),
            pl.BlockSpec((TM_PROJ, LANES), row),
        ],
        out_shape=[
            jax.ShapeDtypeStruct((t, GMLP_WIDTH), BF16),
            jax.ShapeDtypeStruct((t, SSD_WIDTH), BF16),
            jax.ShapeDtypeStruct((t, CONV_CH), F32),
            jax.ShapeDtypeStruct((t, LANES), F32),
        ],
        scratch_shapes=[pltpu.VMEM((TM_PROJ, GMLP_WIDTH), F32)],
        compiler_params=pltpu.CompilerParams(
            dimension_semantics=("arbitrary",), vmem_limit_bytes=VMEM_LIMIT),
        name="inproj_gmlp",
    )(x2, mod3, mod3, norm1_g, w_in_p, b_in_p, ln_g, ln_b, w_s, bs_tile, out_g)


def _ssd_chunk(act, dtv, a_row, expand, state, rev):
    off = SSD_HEADS if rev else 0
    row = lax.broadcasted_iota(jnp.int32, (CHUNK, CHUNK), 0)
    col = lax.broadcasted_iota(jnp.int32, (CHUNK, CHUNK), 1)
    lower = row >= col
    upper = row <= col
    keep = upper if rev else lower
    da = dtv * a_row
    cs = jnp.dot(keep.astype(F32), da, precision=HIGHEST, preferred_element_type=F32)
    cs_t = cs.T
    dt_t = dtv.T
    tot = cs[0:1, :] if rev else cs[CHUNK - 1:CHUNK, :]

    xs = act[:, :SSD_WIDTH]
    lane = lax.broadcasted_iota(jnp.int32, (CHUNK, LANES), 1)
    first_half = lane < SSD_HEAD_DIM
    zero = jnp.zeros((), BF16)

    stack = jnp.concatenate(
        [jnp.exp(cs), dtv * jnp.exp(tot - cs), jnp.broadcast_to(jnp.exp(tot), (SUBLANES, LANES))], axis=0)
    stack_x = jnp.dot(stack.astype(BF16), expand, preferred_element_type=F32)
    into_x = stack_x[:CHUNK]
    w_x = stack_x[CHUNK:2 * CHUNK]
    cd_x = stack_x[2 * CHUNK:2 * CHUNK + 1]
    xw = xs * w_x.astype(BF16)

    pieces = []
    for g in range(SSD_GROUPS):
        bg = act[:, SSD_WIDTH + g * SSD_STATE:SSD_WIDTH + (g + 1) * SSD_STATE]
        cg = act[:, SSD_WIDTH + SSD_GROUPS * SSD_STATE + g * SSD_STATE:
                 SSD_WIDTH + SSD_GROUPS * SSD_STATE + (g + 1) * SSD_STATE]
        scores = lax.dot_general(cg, bg, (((1,), (1,)), ((), ())), preferred_element_type=F32)
        heads_per_group = SSD_HEADS // SSD_GROUPS
        for pair in range(heads_per_group // 2):
            h0 = g * heads_per_group + 2 * pair
            xs_pair = xs[:, h0 * SSD_HEAD_DIM:(h0 + 2) * SSD_HEAD_DIM]
            y_pair = None
            for k in range(2):
                hh = off + h0 + k
                seg = cs[:, hh:hh + 1] - cs_t[hh:hh + 1, :]
                dec = jnp.exp(jnp.where(keep, seg, -jnp.inf))
                m = (scores * dec * dt_t[hh:hh + 1, :]).astype(BF16)
                rhs = jnp.where(first_half if k == 0 else jnp.logical_not(first_half), xs_pair, zero)
                part = jnp.dot(m, rhs, preferred_element_type=F32)
                y_pair = part if y_pair is None else y_pair + part
            pieces.append(y_pair)
    y_diag = jnp.concatenate(pieces, axis=1)

    y_off = []
    new_state = []
    for g in range(SSD_GROUPS):
        gcols = slice(g * GROUP_WIDTH, (g + 1) * GROUP_WIDTH)
        bg = act[:, SSD_WIDTH + g * SSD_STATE:SSD_WIDTH + (g + 1) * SSD_STATE]
        cg = act[:, SSD_WIDTH + SSD_GROUPS * SSD_STATE + g * SSD_STATE:
                 SSD_WIDTH + SSD_GROUPS * SSD_STATE + (g + 1) * SSD_STATE]
        prev = state[:, gcols]
        y_off.append(jnp.dot(cg, prev.astype(BF16), preferred_element_type=F32))
        bg_t = bg.astype(F32).T.astype(BF16)
        new = jnp.dot(bg_t, xw[:, gcols], preferred_element_type=F32)
        new_state.append(prev * cd_x[:, gcols] + new)
    y = y_diag + jnp.concatenate(y_off, axis=1) * into_x
    return y, jnp.concatenate(new_state, axis=1)


def _ssd_body(xbc_ref, xprev_ref, xnext_ref, dt_ref, z_ref, ya_ref, x_ref, gate_ref, cw_ref, cb_ref,
              dtb_ref, alog_ref, dsk_ref, ng_ref, wout_ref, exp_ref, o_ref,
              act_scr, dts_scr, yf_scr, st_scr, ext_scr, *, n_steps):
    d = pl.program_id(1)
    c = pl.program_id(2)
    lane1 = lax.broadcasted_iota(jnp.int32, (1, LANES), 1)
    a_row = jnp.where(lane1 < 2 * SSD_HEADS, -jnp.exp(alog_ref[...]), 0.0)
    chunk_rows = [slice(j * CHUNK, (j + 1) * CHUNK) for j in range(SSD_STEP_ROWS // CHUNK)]

    @pl.when(c == 0)
    def _():
        st_scr[...] = jnp.zeros_like(st_scr)

    def scan(act, dtv, expand, rev):
        state = st_scr[...]
        ys = [None] * len(chunk_rows)
        for j in (reversed(range(len(chunk_rows))) if rev else range(len(chunk_rows))):
            ys[j], state = _ssd_chunk(act[chunk_rows[j], :], dtv[chunk_rows[j], :], a_row, expand, state, rev)
        st_scr[...] = state
        return jnp.concatenate(ys, axis=0)

    @pl.when(d == 0)
    def _forward():
        row0 = pl.multiple_of(c * SSD_STEP_ROWS, SSD_STEP_ROWS)
        ext_scr[0:SUBLANES, :] = jnp.where(c > 0, xprev_ref[...], 0.0)
        ext_scr[SUBLANES:SUBLANES + SSD_STEP_ROWS, :] = xbc_ref[...]
        ext_scr[SUBLANES + SSD_STEP_ROWS:, :] = jnp.where(c < n_steps - 1, xnext_ref[...], 0.0)
        ext = ext_scr[...]
        n_ext = SSD_STEP_ROWS + 2 * SUBLANES
        body = slice(SUBLANES, SUBLANES + SSD_STEP_ROWS)
        down1 = pltpu.roll(ext, 1, 0)
        down2 = pltpu.roll(down1, 1, 0)
        up1 = pltpu.roll(ext, n_ext - 1, 0)
        up2 = pltpu.roll(up1, n_ext - 1, 0)
        acc = cb_ref[...]
        for k, tap in enumerate((down2, down1, ext, up1, up2)):
            acc = acc + cw_ref[k:k + 1, :] * tap[body, :]
        act = _silu(acc).astype(BF16)
        dtv = _softplus(dt_ref[...] + dtb_ref[...])
        act_scr[pl.ds(row0, SSD_STEP_ROWS), :] = act
        dts_scr[pl.ds(row0, SSD_STEP_ROWS), :] = dtv
        yf_scr[pl.ds(row0, SSD_STEP_ROWS), :] = scan(act, dtv, exp_ref[0], rev=False).astype(BF16)

    @pl.when(d == 1)
    def _backward():
        row0 = pl.multiple_of((n_steps - 1 - c) * SSD_STEP_ROWS, SSD_STEP_ROWS)
        act = act_scr[pl.ds(row0, SSD_STEP_ROWS), :]
        dtv = dts_scr[pl.ds(row0, SSD_STEP_ROWS), :]
        yb = scan(act, dtv, exp_ref[1], rev=True)
        xs = act[:, :SSD_WIDTH].astype(F32)
        y = yf_scr[pl.ds(row0, SSD_STEP_ROWS), :].astype(F32) + yb + dsk_ref[...] * xs
        y = y * _silu(z_ref[...].astype(F32))
        y = jnp.concatenate(
            [_rms(y[:, g * GROUP_WIDTH:(g + 1) * GROUP_WIDTH]) for g in range(SSD_GROUPS)], axis=1)
        y = y * ng_ref[...]
        mix = jnp.concatenate([ya_ref[...], y.astype(BF16)], axis=1)
        o = jnp.dot(mix, wout_ref[...], preferred_element_type=F32)
        o_ref[...] = x_ref[...] + gate_ref[...] * o


def _ssd_outproj(xbc, dt, z, ya, x2, mod3, conv_w8, conv_b, dtb_row, alog_row, dsk_row, norm_g, w_out_b,
                 expand, bn, seq):
    t = x2.shape[0]
    nc = seq // SSD_STEP_ROWS
    blocks8 = SSD_STEP_ROWS // SUBLANES
    last8 = t // SUBLANES - 1

    def fwd_chunk(b, d, c):
        return b * nc + c + d * (nc - 1 - c)

    def bwd_chunk(b, d, c):
        return b * nc + nc - 1 - d * c

    const2 = lambda b, d, c: (0, 0)
    return pl.pallas_call(
        functools.partial(_ssd_body, n_steps=nc),
        grid=(bn, 2, nc),
        in_specs=[
            pl.BlockSpec((SSD_STEP_ROWS, CONV_CH), lambda b, d, c: (fwd_chunk(b, d, c), 0)),
            pl.BlockSpec((SUBLANES, CONV_CH),
                         lambda b, d, c: (jnp.maximum(fwd_chunk(b, d, c) * blocks8 - 1, 0), 0)),
            pl.BlockSpec((SUBLANES, CONV_CH),
                         lambda b, d, c: (jnp.minimum((fwd_chunk(b, d, c) + 1) * blocks8, last8), 0)),
            pl.BlockSpec((SSD_STEP_ROWS, LANES), lambda b, d, c: (fwd_chunk(b, d, c), 0)),
            pl.BlockSpec((SSD_STEP_ROWS, SSD_WIDTH), lambda b, d, c: (bwd_chunk(b, d, c), 0)),
            pl.BlockSpec((SSD_STEP_ROWS, GMLP_WIDTH), lambda b, d, c: (bwd_chunk(b, d, c), 0)),
            pl.BlockSpec((SSD_STEP_ROWS, D_MODEL), lambda b, d, c: (bwd_chunk(b, d, c), 0)),
            pl.BlockSpec((None, 1, D_MODEL), lambda b, d, c: (b, 0, 2)),
            pl.BlockSpec((SUBLANES, CONV_CH), const2),
            pl.BlockSpec((1, CONV_CH), const2),
            pl.BlockSpec((1, LANES), const2),
            pl.BlockSpec((1, LANES), const2),
            pl.BlockSpec((1, SSD_WIDTH), const2),
            pl.BlockSpec((1, SSD_WIDTH), const2),
            pl.BlockSpec((GMLP_WIDTH + SSD_WIDTH, D_MODEL), const2, pipeline_mode=pl.Buffered(1)),
            pl.BlockSpec((2, LANES, SSD_WIDTH), lambda b, d, c: (0, 0, 0), pipeline_mode=pl.Buffered(1)),
        ],
        out_specs=pl.BlockSpec((SSD_STEP_ROWS, D_MODEL), lambda b, d, c: (bwd_chunk(b, d, c), 0)),
        out_shape=jax.ShapeDtypeStruct((t, D_MODEL), F32),
        scratch_shapes=[
            pltpu.VMEM((seq, CONV_CH), BF16),
            pltpu.VMEM((seq, LANES), F32),
            pltpu.VMEM((seq, SSD_WIDTH), BF16),
            pltpu.VMEM((SSD_STATE, SSD_WIDTH), F32),
            pltpu.VMEM((SSD_STEP_ROWS + 2 * SUBLANES, CONV_CH), F32),
        ],
        compiler_params=pltpu.CompilerParams(
            dimension_semantics=("arbitrary", "arbitrary", "arbitrary"), vmem_limit_bytes=VMEM_LIMIT),
        name="ssd_outproj",
    )(xbc, xbc, xbc, dt, z, ya, x2, mod3, conv_w8, conv_b, dtb_row, alog_row, dsk_row, norm_g, w_out_b,
      expand)


def _route(logits):
    lane = lax.broadcasted_iota(jnp.int32, logits.shape, 1)
    big = jnp.int32(LANES)
    neg = -jnp.inf
    gmask = (lane >= N_EXPERTS) & (lane < N_EXPERTS + N_EGROUPS)
    gl = jnp.where(gmask, logits, neg)
    gmax = jnp.max(gl, axis=-1, keepdims=True)
    gidx = jnp.min(jnp.where(gl == gmax, lane, big), axis=-1, keepdims=True) - N_EXPERTS
    p_g = 1.0 / jnp.sum(jnp.where(gmask, jnp.exp(gl - gmax), 0.0), axis=-1, keepdims=True)
    lo = gidx * EXPERTS_PER_GROUP
    emask = (lane >= lo) & (lane < lo + EXPERTS_PER_GROUP)
    el = jnp.where(emask, logits, neg)
    v1 = jnp.max(el, axis=-1, keepdims=True)
    i1 = jnp.min(jnp.where(el == v1, lane, big), axis=-1, keepdims=True)
    el2 = jnp.where(lane == i1, neg, el)
    v2 = jnp.max(el2, axis=-1, keepdims=True)
    i2 = jnp.min(jnp.where(el2 == v2, lane, big), axis=-1, keepdims=True)
    e2 = jnp.exp(v2 - v1)
    den = 1.0 + e2
    w1 = p_g / den
    w2 = p_g * e2 / den
    return i1, i2, w1, w2


def _router_body(x_ref, shift_ref, scale_ref, g2_ref, wr_ref, br_ref, w1_ref, w3_ref, w2_ref,
                 h_ref, rt_ref, cnt_ref, w1b_ref, w3b_ref, w2b_ref):
    w1b_ref[...] = w1_ref[...].astype(BF16)
    w3b_ref[...] = w3_ref[...].astype(BF16)
    w2b_ref[...] = w2_ref[...].astype(BF16)
    counts = jnp.zeros((1, LANES), F32)
    for part in range(TM_MOE // ROUTER_ROWS):
        rows = slice(part * ROUTER_ROWS, (part + 1) * ROUTER_ROWS)
        h = _rms(x_ref[rows, :]) * g2_ref[...]
        h = h * (1.0 + scale_ref[...]) + shift_ref[...]
        hb = h.astype(BF16)
        h_ref[rows, :] = hb
        h_lo = (h - hb.astype(F32)).astype(BF16)
        both = jnp.dot(hb, wr_ref[...], preferred_element_type=F32)
        logits = (both[:, :LANES] + both[:, LANES:]
                  + jnp.dot(h_lo, wr_ref[:, :LANES], preferred_element_type=F32)) + br_ref[...]
        i1, i2, w1, w2 = _route(logits)
        lane = lax.broadcasted_iota(jnp.int32, (ROUTER_ROWS, LANES), 1)
        rt_ref[rows, :] = jnp.where(lane == 0, i1.astype(F32), jnp.where(lane == 1, i2.astype(F32),
                                    jnp.where(lane == 2, w1, jnp.where(lane == 3, w2, 0.0))))
        mask = jnp.where(lane == i1, 1.0, jnp.where(lane == i2, 1.0, 0.0))
        counts = counts + jnp.sum(mask, axis=0, keepdims=True)
    cnt_ref[...] = jnp.broadcast_to(counts, (SUBLANES, LANES))


def _router(x1, mod3, norm2_g, w_router, b_router, w1, w3, w2, seq):
    t = x1.shape[0]
    tiles_per_seq = seq // TM_MOE
    n_tiles = t // TM_MOE
    up_rows = N_EXPERTS * D_MODEL // n_tiles
    down_rows = N_EXPERTS * D_EXPERT // n_tiles
    assert up_rows * n_tiles == N_EXPERTS * D_MODEL and up_rows % SEG_ALIGN == 0
    assert down_rows * n_tiles == N_EXPERTS * D_EXPERT and down_rows % SEG_ALIGN == 0
    row = lambda i: (i, 0)
    const2 = lambda i: (0, 0)
    outs = pl.pallas_call(
        _router_body,
        grid=(n_tiles,),
        in_specs=[
            pl.BlockSpec((TM_MOE, D_MODEL), row),
            pl.BlockSpec((None, 1, D_MODEL), lambda i: (i // tiles_per_seq, 0, 3)),
            pl.BlockSpec((None, 1, D_MODEL), lambda i: (i // tiles_per_seq, 0, 4)),
            pl.BlockSpec((1, D_MODEL), const2),
            pl.BlockSpec((D_MODEL, 2 * LANES), const2),
            pl.BlockSpec((1, LANES), const2),
            pl.BlockSpec((up_rows, D_EXPERT), row),
            pl.BlockSpec((up_rows, D_EXPERT), row),
            pl.BlockSpec((down_rows, D_MODEL), row),
        ],
        out_specs=[
            pl.BlockSpec((TM_MOE, D_MODEL), row),
            pl.BlockSpec((TM_MOE, LANES), row),
            pl.BlockSpec((SUBLANES, LANES), row),
            pl.BlockSpec((up_rows, D_EXPERT), row),
            pl.BlockSpec((up_rows, D_EXPERT), row),
            pl.BlockSpec((down_rows, D_MODEL), row),
        ],
        out_shape=[
            jax.ShapeDtypeStruct((t, D_MODEL), BF16),
            jax.ShapeDtypeStruct((t, LANES), F32),
            jax.ShapeDtypeStruct((n_tiles * SUBLANES, LANES), F32),
            jax.ShapeDtypeStruct((N_EXPERTS * D_MODEL, D_EXPERT), BF16),
            jax.ShapeDtypeStruct((N_EXPERTS * D_MODEL, D_EXPERT), BF16),
            jax.ShapeDtypeStruct((N_EXPERTS * D_EXPERT, D_MODEL), BF16),
        ],
        compiler_params=pltpu.CompilerParams(
            dimension_semantics=("arbitrary",), vmem_limit_bytes=VMEM_LIMIT),
        name="moe_router",
    )(x1, mod3, mod3, norm2_g, w_router, b_router, w1.reshape(-1, D_EXPERT), w3.reshape(-1, D_EXPERT),
      w2.reshape(-1, D_MODEL))
    h2, rt, counts, w1b, w3b, w2b = outs
    return (h2, rt, counts, w1b.reshape(N_EXPERTS, D_MODEL, D_EXPERT), w3b.reshape(N_EXPERTS, D_MODEL, D_EXPERT),
            w2b.reshape(N_EXPERTS, D_EXPERT, D_MODEL))


def _row_copy(vmem_buf, v_start, hbm_ref, g_start, size, sem, to_hbm):
    v = vmem_buf.at[pl.ds(v_start, size), :]
    g = hbm_ref.at[pl.ds(g_start, size), :]
    return pltpu.make_async_copy(v, g, sem) if to_hbm else pltpu.make_async_copy(g, v, sem)


def _copy_rows(n, vmem_buf, v0, hbm_ref, g0, sem, *, to_hbm, wait):
    def run(v_off, g_off, size):
        v_start = 0 if v0 is None else pl.multiple_of(v0 + v_off, SEG_ALIGN)
        cp = _row_copy(vmem_buf, v_start, hbm_ref, pl.multiple_of(g0 + g_off, SEG_ALIGN), size, sem, to_hbm)
        if wait:
            cp.wait()
        else:
            cp.start()

    n_big = n // PIECE_ROWS

    def big_piece(k, carry):
        run(k * PIECE_ROWS, k * PIECE_ROWS, PIECE_ROWS)
        return carry

    lax.fori_loop(0, n_big, big_piece, 0)
    base = n_big * PIECE_ROWS
    rem = n - base
    for size in REM_SIZES:
        done = base + (rem // (2 * size)) * (2 * size)

        @pl.when((rem & size) != 0)
        def _():
            run(done, done, size)


def _segment_copies(seg_ref, loc_ref, goff_ref, tile, vmem_buf, hbm_ref, sem, *, to_hbm):
    base = tile * N_EXPERTS

    def per_expert(e, carry):
        _copy_rows(seg_ref[base + e], vmem_buf, loc_ref[base + e], hbm_ref, goff_ref[base + e], sem,
                   to_hbm=to_hbm, wait=False)
        return carry

    lax.fori_loop(0, N_EXPERTS, per_expert, 0)


def _tile_rows(seg_ref, loc_ref, tile):
    last = tile * N_EXPERTS + N_EXPERTS - 1
    return loc_ref[last] + seg_ref[last]


def _segment_wait(seg_ref, loc_ref, tile, vmem_buf, hbm_ref, sem, *, to_hbm):
    total = _tile_rows(seg_ref, loc_ref, tile)
    for size in TOTAL_SIZES:
        @pl.when((total & size) != 0)
        def _():
            _row_copy(vmem_buf, 0, hbm_ref, 0, size, sem, to_hbm).wait()


def _dispatch_body(seg_ref, loc_ref, goff_ref, fill_ref, h_ref, rt_ref, locrow_ref, pos_ref, xs_hbm, buf, sem,
                   *, n_tiles, n_row_tiles):
    i = pl.program_id(0)
    slot = i % 2

    start = functools.partial(_segment_copies, seg_ref, loc_ref, goff_ref, hbm_ref=xs_hbm, to_hbm=True)
    wait = functools.partial(_segment_wait, seg_ref, loc_ref, hbm_ref=xs_hbm, to_hbm=True)

    @pl.when(i >= 2)
    def _():
        wait(i - 2, buf.at[slot], sem=sem.at[slot])

    rt = rt_ref[...]
    lane = lax.broadcasted_iota(jnp.int32, (TM_MOE, LANES), 1)
    lanef = lane.astype(F32)
    e1 = rt[:, 0:1]
    e2 = rt[:, 1:2]
    mask = jnp.where(lanef == e1, 1.0, jnp.where(lanef == e2, 1.0, 0.0))
    r = lax.broadcasted_iota(jnp.int32, (TM_MOE, TM_MOE), 0)
    cc = lax.broadcasted_iota(jnp.int32, (TM_MOE, TM_MOE), 1)
    strict = jnp.where(r > cc, 1.0, 0.0).astype(BF16)
    rank = jnp.dot(strict, mask.astype(BF16), preferred_element_type=F32)
    posall = rank + locrow_ref[...]
    pos1 = jnp.sum(jnp.where(lanef == e1, posall, 0.0), axis=-1, keepdims=True)
    pos2 = jnp.sum(jnp.where(lanef == e2, posall, 0.0), axis=-1, keepdims=True)
    pp = jnp.where(lane == 0, pos1, jnp.where(lane == 1, pos2, rt))
    pos_ref[...] = pp

    slots_t = [pp[k * LANES:(k + 1) * LANES, :].T for k in range(TM_MOE // LANES)]

    rowp = lax.broadcasted_iota(jnp.int32, (R_LOC, LANES), 0).astype(F32)
    perm = jnp.concatenate(
        [jnp.where(rowp == st[0:1, :], 1.0, jnp.where(rowp == st[1:2, :], 1.0, 0.0)).astype(BF16)
         for st in slots_t], axis=1)
    buf[slot] = jnp.dot(perm, h_ref[...], preferred_element_type=F32).astype(BF16)
    start(i, buf.at[slot], sem=sem.at[slot])

    @pl.when(i == n_tiles - 1)
    def _():
        if n_tiles > 1:
            wait(i - 1, buf.at[1 - slot], sem=sem.at[1 - slot])
        wait(i, buf.at[slot], sem=sem.at[slot])
        zsrc = buf.at[1 - slot]
        zsem = sem.at[1 - slot]
        zsrc[0:TR_EXPERT, :] = jnp.zeros((TR_EXPERT, D_MODEL), BF16)
        for waiting in (False, True):
            def per_expert(e, carry):
                _copy_rows(fill_ref[N_EXPERTS + e], zsrc, None, xs_hbm, fill_ref[e], zsem,
                           to_hbm=True, wait=waiting)
                return carry

            def per_row_tile(k, carry):
                cp = _row_copy(zsrc, 0, xs_hbm, pl.multiple_of(k * TR_EXPERT, TR_EXPERT), TR_EXPERT, zsem, True)
                if waiting:
                    cp.wait()
                else:
                    cp.start()
                return carry

            lax.fori_loop(0, N_EXPERTS, per_expert, 0)
            lax.fori_loop(fill_ref[2 * N_EXPERTS], n_row_tiles, per_row_tile, 0)


def _dispatch(h2, rt, seg, loc, goff, fill, locrow, n_row_tiles):
    t = h2.shape[0]
    n_tiles = t // TM_MOE
    row = lambda i, *_: (i, 0)
    grid_spec = pltpu.PrefetchScalarGridSpec(
        num_scalar_prefetch=4,
        grid=(n_tiles,),
        in_specs=[
            pl.BlockSpec((TM_MOE, D_MODEL), row),
            pl.BlockSpec((TM_MOE, LANES), row),
            pl.BlockSpec((None, 1, LANES), lambda i, *_: (i, 0, 0)),
        ],
        out_specs=[
            pl.BlockSpec((TM_MOE, LANES), row),
            pl.BlockSpec(memory_space=pl.ANY),
        ],
        scratch_shapes=[
            pltpu.VMEM((2, R_LOC, D_MODEL), BF16),
            pltpu.SemaphoreType.DMA((2,)),
        ],
    )
    return pl.pallas_call(
        functools.partial(_dispatch_body, n_tiles=n_tiles, n_row_tiles=n_row_tiles),
        grid_spec=grid_spec,
        out_shape=[
            jax.ShapeDtypeStruct((t, LANES), F32),
            jax.ShapeDtypeStruct((n_row_tiles * TR_EXPERT, D_MODEL), BF16),
        ],
        compiler_params=pltpu.CompilerParams(
            dimension_semantics=("arbitrary",), vmem_limit_bytes=VMEM_LIMIT),
        name="moe_dispatch",
    )(seg, loc, goff, fill, h2, rt, locrow)


def _expert_body(te_ref, tw_ref, na_ref, xs_ref, w1a_ref, w3a_ref, w2a_ref, w1b_ref, w3b_ref, w2b_ref, y_ref):
    del tw_ref
    r = pl.program_id(0)
    n_active = na_ref[0]
    weights = ((w1a_ref, w3a_ref, w2a_ref), (w1b_ref, w3b_ref, w2b_ref))

    def swiglu(half, rows=None):
        w1_ref, w3_ref, w2_ref = weights[half]
        rows = slice(half * TR_EXPERT, (half + 1) * TR_EXPERT) if rows is None else rows
        x = xs_ref[rows, :]
        a = _silu(jnp.dot(x, w1_ref[...], preferred_element_type=F32))
        a = a * jnp.dot(x, w3_ref[...], preferred_element_type=F32)
        y_ref[rows, :] = jnp.dot(a.astype(BF16), w2_ref[...], preferred_element_type=F32).astype(BF16)

    first = EXPERT_TILES_PER_STEP * r
    both = first + 1 < n_active
    same = te_ref[first] == te_ref[first + 1]

    @pl.when(both & same)
    def _():
        swiglu(0, slice(0, EXPERT_TILES_PER_STEP * TR_EXPERT))

    @pl.when(both & jnp.logical_not(same))
    def _():
        swiglu(0)
        swiglu(1)

    @pl.when(first + 1 == n_active)
    def _():
        swiglu(0)
        y_ref[TR_EXPERT:, :] = jnp.zeros((TR_EXPERT, D_MODEL), BF16)

    @pl.when(first >= n_active)
    def _():
        y_ref[...] = jnp.zeros_like(y_ref)


def _experts(xs, tile_expert, tile_weights, n_active, w1b, w3b, w2b):
    n_rows = xs.shape[0]
    step_rows = EXPERT_TILES_PER_STEP * TR_EXPERT
    x_block = lambda r, te, tw, na: (jnp.minimum(r, (na[0] - 1) // EXPERT_TILES_PER_STEP), 0)
    w_specs = []
    for half in range(EXPERT_TILES_PER_STEP):
        w_block = lambda r, te, tw, na, half=half: (tw[EXPERT_TILES_PER_STEP * r + half], 0, 0)
        w_specs += [pl.BlockSpec((None, D_MODEL, D_EXPERT), w_block),
                    pl.BlockSpec((None, D_MODEL, D_EXPERT), w_block),
                    pl.BlockSpec((None, D_EXPERT, D_MODEL), w_block)]
    grid_spec = pltpu.PrefetchScalarGridSpec(
        num_scalar_prefetch=3,
        grid=(n_rows // step_rows,),
        in_specs=[pl.BlockSpec((step_rows, D_MODEL), x_block)] + w_specs,
        out_specs=pl.BlockSpec((step_rows, D_MODEL), lambda r, te, tw, na: (r, 0)),
    )
    return pl.pallas_call(
        _expert_body,
        grid_spec=grid_spec,
        out_shape=jax.ShapeDtypeStruct((n_rows, D_MODEL), BF16),
        compiler_params=pltpu.CompilerParams(
            dimension_semantics=("arbitrary",), vmem_limit_bytes=VMEM_LIMIT),
        name="moe_experts",
    )(tile_expert, tile_weights, n_active, xs, w1b, w3b, w2b, w1b, w3b, w2b)


def _combine_body(seg_ref, loc_ref, goff_ref, y_hbm, pos_ref, x_ref, gate_ref, fg_ref, o_ref, buf, sem, *,
                  n_tiles, final_norm):
    i = pl.program_id(0)
    slot = i % 2
    start = functools.partial(_segment_copies, seg_ref, loc_ref, goff_ref, hbm_ref=y_hbm, to_hbm=False)

    @pl.when(i == 0)
    def _():
        buf[...] = jnp.zeros_like(buf)
        start(0, buf.at[0], sem=sem.at[0])

    @pl.when(i + 1 < n_tiles)
    def _():
        start(i + 1, buf.at[1 - slot], sem=sem.at[1 - slot])

    _segment_wait(seg_ref, loc_ref, i, buf.at[slot], y_hbm, sem.at[slot], to_hbm=False)
    for part in range(TM_MOE // COMBINE_ROWS):
        rows = slice(part * COMBINE_ROWS, (part + 1) * COMBINE_ROWS)
        pp = pos_ref[rows, :]
        colp = lax.broadcasted_iota(jnp.int32, (COMBINE_ROWS, R_LOC), 1).astype(F32)
        wc = jnp.where(colp == pp[:, 0:1], pp[:, 2:3], jnp.where(colp == pp[:, 1:2], pp[:, 3:4], 0.0))
        moe = jnp.dot(wc.astype(BF16), buf[slot], preferred_element_type=F32)
        x2 = x_ref[rows, :] + gate_ref[...] * moe
        o_ref[rows, :] = _rms(x2) * fg_ref[...] if final_norm else x2


def _combine(y, pos, x1, mod3, final_g, seg, loc, goff, seq, final_norm):
    t = x1.shape[0]
    n_tiles = t // TM_MOE
    tiles_per_seq = seq // TM_MOE
    row = lambda i, *_: (i, 0)
    grid_spec = pltpu.PrefetchScalarGridSpec(
        num_scalar_prefetch=3,
        grid=(n_tiles,),
        in_specs=[
            pl.BlockSpec(memory_space=pl.ANY),
            pl.BlockSpec((TM_MOE, LANES), row),
            pl.BlockSpec((TM_MOE, D_MODEL), row),
            pl.BlockSpec((None, 1, D_MODEL), lambda i, *_: (i // tiles_per_seq, 0, 5)),
            pl.BlockSpec((1, D_MODEL), lambda i, *_: (0, 0)),
        ],
        out_specs=pl.BlockSpec((TM_MOE, D_MODEL), row),
        scratch_shapes=[
            pltpu.VMEM((2, R_LOC, D_MODEL), BF16),
            pltpu.SemaphoreType.DMA((2,)),
        ],
    )
    return pl.pallas_call(
        functools.partial(_combine_body, n_tiles=n_tiles, final_norm=final_norm),
        grid_spec=grid_spec,
        out_shape=jax.ShapeDtypeStruct((t, D_MODEL), F32),
        compiler_params=pltpu.CompilerParams(
            dimension_semantics=("arbitrary",), vmem_limit_bytes=VMEM_LIMIT),
        name="moe_combine",
    )(seg, loc, goff, y, pos, x1, mod3, final_g)


def _moe_plan(counts, n_tiles, n_row_tiles):
    cnt = counts.reshape(n_tiles, SUBLANES, LANES)[:, 0, :N_EXPERTS].astype(jnp.int32)
    seg = (cnt + SEG_ALIGN - 1) // SEG_ALIGN * SEG_ALIGN
    before_e = jnp.arange(N_EXPERTS)[:, None] < jnp.arange(N_EXPERTS)[None, :]
    before_t = jnp.arange(n_tiles)[:, None] > jnp.arange(n_tiles)[None, :]
    loc = jnp.sum(jnp.where(before_e[None], seg[:, :, None], 0), axis=1)
    tot = jnp.sum(seg, axis=0)
    region = (tot + TR_EXPERT - 1) // TR_EXPERT * TR_EXPERT
    gstart = jnp.sum(jnp.where(before_e, region[:, None], 0), axis=0)
    gend = gstart + region
    goff = gstart[None, :] + jnp.sum(jnp.where(before_t[:, :, None], seg[None], 0), axis=1)
    n_active = gend[-1] // TR_EXPERT
    tile_row0 = jnp.arange(n_row_tiles, dtype=jnp.int32) * TR_EXPERT
    last_row0 = (n_active - 1) * TR_EXPERT
    te = jnp.sum(gend[None, :] <= jnp.minimum(tile_row0, last_row0)[:, None], axis=1).astype(jnp.int32)
    te = jnp.minimum(te, N_EXPERTS - 1)
    te2 = te.reshape(-1, EXPERT_TILES_PER_STEP)
    cand = jnp.where(te2[:, 1] != te2[:, 0], te2[:, 1], 0)
    steps = jnp.arange(te2.shape[0])
    held = jnp.max(jnp.where(steps[None, :] <= steps[:, None], cand[None, :], 0), axis=1)
    tw = jnp.stack([te2[:, 0], held], axis=1).reshape(-1).astype(jnp.int32)
    locrow = _pad_cols(loc.astype(F32), LANES).reshape(n_tiles, 1, LANES)
    n_active = n_active.reshape(1).astype(jnp.int32)
    fill = jnp.concatenate([gstart + tot, region - tot, n_active]).astype(jnp.int32)
    return (seg.reshape(-1), loc.reshape(-1).astype(jnp.int32), goff.reshape(-1).astype(jnp.int32), fill,
            locrow, te, tw, n_active)


def _moe_final(x1, mod3, norm2_g, w_router, b_router, w1, w3, w2, final_g, seq, final_norm):
    t = x1.shape[0]
    n_tiles = t // TM_MOE
    max_rows = N_ASSIGN * t + (SEG_ALIGN - 1) * N_EXPERTS * n_tiles + N_EXPERTS * (TR_EXPERT - SEG_ALIGN)
    step_rows = EXPERT_TILES_PER_STEP * TR_EXPERT
    n_row_tiles = -(-max_rows // step_rows) * EXPERT_TILES_PER_STEP
    h2, rt, counts, w1b, w3b, w2b = _router(x1, mod3, norm2_g, w_router, b_router, w1, w3, w2, seq)
    seg, loc, goff, fill, locrow, te, tw, n_active = _moe_plan(counts, n_tiles, n_row_tiles)
    pos, xs = _dispatch(h2, rt, seg, loc, goff, fill, locrow, n_row_tiles)
    y = _experts(xs, te, tw, n_active, w1b, w3b, w2b)
    return _combine(y, pos, x1, mod3, final_g, seg, loc, goff, seq, final_norm)


def _pad_cols(a, width):
    return jnp.pad(a, ((0, 0), (0, width - a.shape[1])))


def kernel(x, c, w_ada, b_ada, norm1_g, w_in, b_in, gmlp_ln_g, gmlp_ln_b, gmlp_w_s, gmlp_b_s, gmlp_out_g, conv_w, conv_b, a_log_f, a_log_b, dt_bias_f, dt_bias_b, d_skip, ssd_norm_g, w_out, norm2_g, w_router_g, b_router_g, w_router_e, b_router_e, w1, w3, w2, final_g):
    bn, seq, _ = x.shape
    depth = w_ada.shape[0]
    x2 = x.reshape(bn * seq, D_MODEL)

    head_rows = jnp.arange(LANES)[:, None]
    head_cols = jnp.arange(SSD_WIDTH)[None, :] // SSD_HEAD_DIM
    expand = jnp.stack([head_rows == head_cols, head_rows == head_cols + SSD_HEADS]).astype(BF16)

    for l in range(depth):
        mod3 = _modulation(c, w_ada, b_ada[l], l).reshape(bn, 1, N_MOD * D_MODEL)

        w_in_p = _pad_cols(w_in[l].astype(BF16), IN_COLS_PAD)
        b_in_p = _pad_cols(b_in[l][None, :], IN_COLS_PAD)
        bs = jnp.repeat(gmlp_b_s[l].T, LANES, axis=1)
        bs_tile = jnp.tile(bs, (TM_PROJ // CHUNK, 1))
        ya, z, xbc, dt = _inproj_gmlp(
            x2, mod3, norm1_g[l][None, :], w_in_p, b_in_p, gmlp_ln_g[l][None, :], gmlp_ln_b[l][None, :],
            gmlp_w_s[l].astype(BF16), bs_tile, gmlp_out_g[l][None, :], seq)

        conv_w8 = jnp.pad(conv_w[l], ((0, SUBLANES - CONV_WIDTH), (0, 0)))
        dtb_row = _pad_cols(jnp.concatenate([dt_bias_f[l], dt_bias_b[l]])[None, :], LANES)
        alog_row = _pad_cols(jnp.concatenate([a_log_f[l], a_log_b[l]])[None, :], LANES)
        dsk_row = jnp.repeat(d_skip[l], SSD_HEAD_DIM)[None, :]
        x2 = _ssd_outproj(xbc, dt, z, ya, x2, mod3, conv_w8, conv_b[l][None, :], dtb_row, alog_row, dsk_row,
                          ssd_norm_g[l][None, :], w_out[l].astype(BF16), expand, bn, seq)

        w_re = jnp.transpose(w_router_e[l], (1, 0, 2)).reshape(D_MODEL, N_EXPERTS)
        w_router = _pad_cols(jnp.concatenate([w_re, w_router_g[l]], axis=1), LANES)
        w_router_hi = w_router.astype(BF16)
        w_router = jnp.concatenate([w_router_hi, (w_router - w_router_hi.astype(F32)).astype(BF16)], axis=1)
        b_router = _pad_cols(jnp.concatenate([b_router_e[l].reshape(-1), b_router_g[l]])[None, :], LANES)
        x2 = _moe_final(x2, mod3, norm2_g[l][None, :], w_router, b_router, w1[l], w3[l], w2[l],
                        final_g[None, :], seq, final_norm=(l == depth - 1))
    return x2.reshape(bn, seq, D_MODEL)
```

```python
import functools
import math

import jax
import jax.numpy as jnp
from jax import lax
from jax.experimental import pallas as pl
from jax.experimental.pallas import tpu as pltpu

F32 = jnp.float32
BF16 = jnp.bfloat16
HIGHEST = lax.Precision.HIGHEST

D_MODEL = 1024
N_MOD = 6
GMLP_WIDTH = 1024
GMLP_HEADS = 8
CHUNK = 128
SSD_STEP_ROWS = 4 * CHUNK
SSD_WIDTH = 1024
SSD_HEADS = 16
SSD_HEAD_DIM = 64
SSD_GROUPS = 2
SSD_STATE = 128
GROUP_WIDTH = SSD_WIDTH // SSD_GROUPS
CONV_WIDTH = 5
CONV_CH = SSD_WIDTH + 2 * SSD_GROUPS * SSD_STATE
N_EGROUPS = 4
EXPERTS_PER_GROUP = 8
N_EXPERTS = 32
D_EXPERT = 256
EPS = 1e-6
LOG2_E = 1.4426950408889634

LANES = 128
SUBLANES = 8
COL_U, COL_V, COL_Z, COL_XBC, COL_DT = 0, 1024, 2048, 3072, 4608
IN_COLS = 4640
IN_COLS_PAD = COL_DT + LANES
TM_PROJ = 1024
PROJ_ROWS = 512
COMBINE_ROWS = 256
ROUTER_ROWS = 256
TM_MOE = 512
TR_EXPERT = 512
EXPERT_TILES_PER_STEP = 2
N_ASSIGN = 2
SEG_ALIGN = 2 * SUBLANES
R_LOC = N_ASSIGN * TM_MOE + N_EXPERTS * SEG_ALIGN
PIECE_ROWS = 128
REM_SIZES = tuple(PIECE_ROWS >> s for s in range(1, PIECE_ROWS.bit_length()) if PIECE_ROWS >> s >= SEG_ALIGN)
TOTAL_SIZES = tuple(1 << s for s in range(R_LOC.bit_length() - 1, -1, -1) if 1 << s >= SEG_ALIGN)
VMEM_LIMIT = 56 * 1024 * 1024


def _silu(v):
    return v * jax.nn.sigmoid(v)


def _gelu(v):
    return 0.5 * v * (1.0 + lax.erf(v * math.sqrt(0.5)))


def _softplus(v):
    return jnp.maximum(v, 0.0) + jnp.log1p(jnp.exp(-jnp.abs(v)))


def _rms(v):
    return v * lax.rsqrt(jnp.mean(v * v, axis=-1, keepdims=True) + EPS)


def _mod_body(c_ref, w_ref, b_ref, o_ref):
    ca = _silu(c_ref[...])
    o_ref[...] = jnp.dot(ca, w_ref[...], precision=HIGHEST, preferred_element_type=F32) + b_ref[...]


def _modulation(c, w_ada, b_ada, layer):
    bn = c.shape[0]
    return pl.pallas_call(
        _mod_body,
        grid=(N_MOD,),
        in_specs=[
            pl.BlockSpec((bn, D_MODEL), lambda j: (0, 0)),
            pl.BlockSpec((None, D_MODEL, D_MODEL), lambda j: (layer, 0, j)),
            pl.BlockSpec((1, D_MODEL), lambda j: (0, j)),
        ],
        out_specs=pl.BlockSpec((bn, D_MODEL), lambda j: (0, j)),
        out_shape=jax.ShapeDtypeStruct((bn, N_MOD * D_MODEL), F32),
        name="adaln_mod",
    )(c, w_ada, b_ada.reshape(1, -1))


def _inproj_body(x_ref, shift_ref, scale_ref, g_ref, w_ref, b_ref, lng_ref, lnb_ref, ws_ref, bs_ref,
                 og_ref, ya_ref, z_ref, xbc_ref, dt_ref, mix_scr):
    for part in range(TM_PROJ // PROJ_ROWS):
        rows = slice(part * PROJ_ROWS, (part + 1) * PROJ_ROWS)
        h = _rms(x_ref[rows, :]) * g_ref[...]
        h = h * (1.0 + scale_ref[...]) + shift_ref[...]
        hb = h.astype(BF16)

        def proj(lo, hi, hb=hb):
            return jnp.dot(hb, w_ref[:, lo:hi], preferred_element_type=F32) + b_ref[:, lo:hi]

        z_ref[rows, :] = proj(COL_Z, COL_XBC).astype(BF16)
        xbc_ref[rows, :] = proj(COL_XBC, COL_DT)
        dt_ref[rows, :] = proj(COL_DT, IN_COLS_PAD)

        v = _gelu(proj(COL_V, COL_Z))
        mu = jnp.mean(v, axis=-1, keepdims=True)
        vc = v - mu
        var = jnp.mean(vc * vc, axis=-1, keepdims=True)
        vn = (vc * lax.rsqrt(var + EPS) * lng_ref[...] + lnb_ref[...]).astype(BF16)
        n_chunks = PROJ_ROWS // CHUNK
        for hd in range(GMLP_HEADS):
            cols = slice(hd * LANES, (hd + 1) * LANES)
            rhs = jnp.concatenate([vn[c * CHUNK:(c + 1) * CHUNK, cols] for c in range(n_chunks)], axis=1)
            res = jnp.dot(ws_ref[hd], rhs, preferred_element_type=F32)
            for c in range(n_chunks):
                mix_scr[part * PROJ_ROWS + c * CHUNK:part * PROJ_ROWS + (c + 1) * CHUNK, cols] = (
                    res[:, c * LANES:(c + 1) * LANES])
        u = _gelu(proj(COL_U, COL_V))
        out = u * (mix_scr[rows, :] + bs_ref[rows, :])
        ya_ref[rows, :] = (_rms(out) * og_ref[...]).astype(BF16)


def _inproj_gmlp(x2, mod3, norm1_g, w_in_p, b_in_p, ln_g, ln_b, w_s, bs_tile, out_g, seq):
    t = x2.shape[0]
    tiles_per_seq = seq // TM_PROJ
    row = lambda i: (i, 0)
    const2 = lambda i: (0, 0)
    return pl.pallas_call(
        _inproj_body,
        grid=(t // TM_PROJ,),
        in_specs=[
            pl.BlockSpec((TM_PROJ, D_MODEL), row),
            pl.BlockSpec((None, 1, D_MODEL), lambda i: (i // tiles_per_seq, 0, 0)),
            pl.BlockSpec((None, 1, D_MODEL), lambda i: (i // tiles_per_seq, 0, 1)),
            pl.BlockSpec((1, D_MODEL), const2),
            pl.BlockSpec((D_MODEL, IN_COLS_PAD), const2, pipeline_mode=pl.Buffered(1)),
            pl.BlockSpec((1, IN_COLS_PAD), const2),
            pl.BlockSpec((1, GMLP_WIDTH), const2),
            pl.BlockSpec((1, GMLP_WIDTH), const2),
            pl.BlockSpec((GMLP_HEADS, CHUNK, CHUNK), lambda i: (0, 0, 0)),
            pl.BlockSpec((TM_PROJ, GMLP_WIDTH), const2, pipeline_mode=pl.Buffered(1)),
            pl.BlockSpec((1, GMLP_WIDTH), const2),
        ],
        out_specs=[
            pl.BlockSpec((TM_PROJ, GMLP_WIDTH), row),
            pl.BlockSpec((TM_PROJ, SSD_WIDTH), row),
            pl.BlockSpec((TM_PROJ, CONV_CH), row),
            pl.BlockSpec((TM_PROJ, LANES), row),
        ],
        out_shape=[
            jax.ShapeDtypeStruct((t, GMLP_WIDTH), BF16),
            jax.ShapeDtypeStruct((t, SSD_WIDTH), BF16),
            jax.ShapeDtypeStruct((t, CONV_CH), F32),
            jax.ShapeDtypeStruct((t, LANES), F32),
        ],
        scratch_shapes=[pltpu.VMEM((TM_PROJ, GMLP_WIDTH), F32)],
        compiler_params=pltpu.CompilerParams(
            dimension_semantics=("arbitrary",), vmem_limit_bytes=VMEM_LIMIT),
        name="inproj_gmlp",
    )(x2, mod3, mod3, norm1_g, w_in_p, b_in_p, ln_g, ln_b, w_s, bs_tile, out_g)


def _ssd_chunk(act, dtv, a_row, expand, state, rev):
    off = SSD_HEADS if rev else 0
    row = lax.broadcasted_iota(jnp.int32, (CHUNK, CHUNK), 0)
    col = lax.broadcasted_iota(jnp.int32, (CHUNK, CHUNK), 1)
    lower = row >= col
    upper = row <= col
    keep = upper if rev else lower
    da = dtv * a_row
    cs = jnp.dot(keep.astype(F32), da, precision=HIGHEST, preferred_element_type=F32)
    cs2 = cs * LOG2_E
    col_t = cs2.T - jnp.log2(dtv.T)
    tot = cs[0:1, :] if rev else cs[CHUNK - 1:CHUNK, :]

    xs = act[:, :SSD_WIDTH]
    lane = lax.broadcasted_iota(jnp.int32, (CHUNK, LANES), 1)
    first_half = lane < SSD_HEAD_DIM
    zero = jnp.zeros((), BF16)

    stack = jnp.concatenate(
        [jnp.exp(cs), dtv * jnp.exp(tot - cs), jnp.broadcast_to(jnp.exp(tot), (SUBLANES, LANES))], axis=0)
    stack_x = jnp.dot(stack.astype(BF16), expand, preferred_element_type=F32)
    into_x = stack_x[:CHUNK]
    w_x = stack_x[CHUNK:2 * CHUNK]
    cd_x = stack_x[2 * CHUNK:2 * CHUNK + 1]
    xw = xs * w_x.astype(BF16)

    pieces = []
    for g in range(SSD_GROUPS):
        bg = act[:, SSD_WIDTH + g * SSD_STATE:SSD_WIDTH + (g + 1) * SSD_STATE]
        cg = act[:, SSD_WIDTH + SSD_GROUPS * SSD_STATE + g * SSD_STATE:
                 SSD_WIDTH + SSD_GROUPS * SSD_STATE + (g + 1) * SSD_STATE]
        scores = lax.dot_general(cg, bg, (((1,), (1,)), ((), ())), preferred_element_type=F32)
        heads_per_group = SSD_HEADS // SSD_GROUPS
        for pair in range(heads_per_group // 2):
            h0 = g * heads_per_group + 2 * pair
            xs_pair = xs[:, h0 * SSD_HEAD_DIM:(h0 + 2) * SSD_HEAD_DIM]
            y_pair = None
            for k in range(2):
                hh = off + h0 + k
                seg = cs2[:, hh:hh + 1] - col_t[hh:hh + 1, :]
                dec = jnp.exp2(jnp.where(keep, seg, -jnp.inf))
                m = (scores * dec).astype(BF16)
                rhs = jnp.where(first_half if k == 0 else jnp.logical_not(first_half), xs_pair, zero)
                part = jnp.dot(m, rhs, preferred_element_type=F32)
                y_pair = part if y_pair is None else y_pair + part
            pieces.append(y_pair)
    y_diag = jnp.concatenate(pieces, axis=1)

    y_off = []
    new_state = []
    for g in range(SSD_GROUPS):
        gcols = slice(g * GROUP_WIDTH, (g + 1) * GROUP_WIDTH)
        bg = act[:, SSD_WIDTH + g * SSD_STATE:SSD_WIDTH + (g + 1) * SSD_STATE]
        cg = act[:, SSD_WIDTH + SSD_GROUPS * SSD_STATE + g * SSD_STATE:
                 SSD_WIDTH + SSD_GROUPS * SSD_STATE + (g + 1) * SSD_STATE]
        prev = state[:, gcols]
        y_off.append(jnp.dot(cg, prev.astype(BF16), preferred_element_type=F32))
        bg_t = bg.astype(F32).T.astype(BF16)
        new = jnp.dot(bg_t, xw[:, gcols], preferred_element_type=F32)
        new_state.append(prev * cd_x[:, gcols] + new)
    y = y_diag + jnp.concatenate(y_off, axis=1) * into_x
    return y, jnp.concatenate(new_state, axis=1)


def _ssd_body(xbc_ref, xprev_ref, xnext_ref, dt_ref, z_ref, ya_ref, x_ref, gate_ref, cw_ref, cb_ref,
              dtb_ref, alog_ref, dsk_ref, ng_ref, wout_ref, exp_ref, o_ref,
              act_scr, dts_scr, yf_scr, st_scr, ext_scr, *, n_steps):
    d = pl.program_id(1)
    c = pl.program_id(2)
    lane1 = lax.broadcasted_iota(jnp.int32, (1, LANES), 1)
    a_row = jnp.where(lane1 < 2 * SSD_HEADS, -jnp.exp(alog_ref[...]), 0.0)
    chunk_rows = [slice(j * CHUNK, (j + 1) * CHUNK) for j in range(SSD_STEP_ROWS // CHUNK)]

    @pl.when(c == 0)
    def _():
        st_scr[...] = jnp.zeros_like(st_scr)

    def scan(act, dtv, expand, rev):
        state = st_scr[...]
        ys = [None] * len(chunk_rows)
        for j in (reversed(range(len(chunk_rows))) if rev else range(len(chunk_rows))):
            ys[j], state = _ssd_chunk(act[chunk_rows[j], :], dtv[chunk_rows[j], :], a_row, expand, state, rev)
        st_scr[...] = state
        return jnp.concatenate(ys, axis=0)

    @pl.when(d == 0)
    def _forward():
        row0 = pl.multiple_of(c * SSD_STEP_ROWS, SSD_STEP_ROWS)
        ext_scr[0:SUBLANES, :] = jnp.where(c > 0, xprev_ref[...], 0.0)
        ext_scr[SUBLANES:SUBLANES + SSD_STEP_ROWS, :] = xbc_ref[...]
        ext_scr[SUBLANES + SSD_STEP_ROWS:, :] = jnp.where(c < n_steps - 1, xnext_ref[...], 0.0)
        ext = ext_scr[...]
        n_ext = SSD_STEP_ROWS + 2 * SUBLANES
        body = slice(SUBLANES, SUBLANES + SSD_STEP_ROWS)
        down1 = pltpu.roll(ext, 1, 0)
        down2 = pltpu.roll(down1, 1, 0)
        up1 = pltpu.roll(ext, n_ext - 1, 0)
        up2 = pltpu.roll(up1, n_ext - 1, 0)
        acc = cb_ref[...]
        for k, tap in enumerate((down2, down1, ext, up1, up2)):
            acc = acc + cw_ref[k:k + 1, :] * tap[body, :]
        act = _silu(acc).astype(BF16)
        dtv = _softplus(dt_ref[...] + dtb_ref[...])
        act_scr[pl.ds(row0, SSD_STEP_ROWS), :] = act
        dts_scr[pl.ds(row0, SSD_STEP_ROWS), :] = dtv
        yf_scr[pl.ds(row0, SSD_STEP_ROWS), :] = scan(act, dtv, exp_ref[0], rev=False).astype(BF16)

    @pl.when(d == 1)
    def _backward():
        row0 = pl.multiple_of((n_steps - 1 - c) * SSD_STEP_ROWS, SSD_STEP_ROWS)
        act = act_scr[pl.ds(row0, SSD_STEP_ROWS), :]
        dtv = dts_scr[pl.ds(row0, SSD_STEP_ROWS), :]
        yb = scan(act, dtv, exp_ref[1], rev=True)
        xs = act[:, :SSD_WIDTH].astype(F32)
        y = yf_scr[pl.ds(row0, SSD_STEP_ROWS), :].astype(F32) + yb + dsk_ref[...] * xs
        y = y * _silu(z_ref[...].astype(F32))
        y = jnp.concatenate(
            [_rms(y[:, g * GROUP_WIDTH:(g + 1) * GROUP_WIDTH]) for g in range(SSD_GROUPS)], axis=1)
        y = y * ng_ref[...]
        mix = jnp.concatenate([ya_ref[...], y.astype(BF16)], axis=1)
        o = jnp.dot(mix, wout_ref[...], preferred_element_type=F32)
        o_ref[...] = x_ref[...] + gate_ref[...] * o


def _ssd_outproj(xbc, dt, z, ya, x2, mod3, conv_w8, conv_b, dtb_row, alog_row, dsk_row, norm_g, w_out_b,
                 expand, bn, seq):
    t = x2.shape[0]
    nc = seq // SSD_STEP_ROWS
    blocks8 = SSD_STEP_ROWS // SUBLANES
    last8 = t // SUBLANES - 1

    def fwd_chunk(b, d, c):
        return b * nc + c + d * (nc - 1 - c)

    def bwd_chunk(b, d, c):
        return b * nc + nc - 1 - d * c

    const2 = lambda b, d, c: (0, 0)
    return pl.pallas_call(
        functools.partial(_ssd_body, n_steps=nc),
        grid=(bn, 2, nc),
        in_specs=[
            pl.BlockSpec((SSD_STEP_ROWS, CONV_CH), lambda b, d, c: (fwd_chunk(b, d, c), 0)),
            pl.BlockSpec((SUBLANES, CONV_CH),
                         lambda b, d, c: (jnp.maximum(fwd_chunk(b, d, c) * blocks8 - 1, 0), 0)),
            pl.BlockSpec((SUBLANES, CONV_CH),
                         lambda b, d, c: (jnp.minimum((fwd_chunk(b, d, c) + 1) * blocks8, last8), 0)),
            pl.BlockSpec((SSD_STEP_ROWS, LANES), lambda b, d, c: (fwd_chunk(b, d, c), 0)),
            pl.BlockSpec((SSD_STEP_ROWS, SSD_WIDTH), lambda b, d, c: (bwd_chunk(b, d, c), 0)),
            pl.BlockSpec((SSD_STEP_ROWS, GMLP_WIDTH), lambda b, d, c: (bwd_chunk(b, d, c), 0)),
            pl.BlockSpec((SSD_STEP_ROWS, D_MODEL), lambda b, d, c: (bwd_chunk(b, d, c), 0)),
            pl.BlockSpec((None, 1, D_MODEL), lambda b, d, c: (b, 0, 2)),
            pl.BlockSpec((SUBLANES, CONV_CH), const2),
            pl.BlockSpec((1, CONV_CH), const2),
            pl.BlockSpec((1, LANES), const2),
            pl.BlockSpec((1, LANES), const2),
            pl.BlockSpec((1, SSD_WIDTH), const2),
            pl.BlockSpec((1, SSD_WIDTH), const2),
            pl.BlockSpec((GMLP_WIDTH + SSD_WIDTH, D_MODEL), const2, pipeline_mode=pl.Buffered(1)),
            pl.BlockSpec((2, LANES, SSD_WIDTH), lambda b, d, c: (0, 0, 0), pipeline_mode=pl.Buffered(1)),
        ],
        out_specs=pl.BlockSpec((SSD_STEP_ROWS, D_MODEL), lambda b, d, c: (bwd_chunk(b, d, c), 0)),
        out_shape=jax.ShapeDtypeStruct((t, D_MODEL), F32),
        scratch_shapes=[
            pltpu.VMEM((seq, CONV_CH), BF16),
            pltpu.VMEM((seq, LANES), F32),
            pltpu.VMEM((seq, SSD_WIDTH), BF16),
            pltpu.VMEM((SSD_STATE, SSD_WIDTH), F32),
            pltpu.VMEM((SSD_STEP_ROWS + 2 * SUBLANES, CONV_CH), F32),
        ],
        compiler_params=pltpu.CompilerParams(
            dimension_semantics=("arbitrary", "arbitrary", "arbitrary"), vmem_limit_bytes=VMEM_LIMIT),
        name="ssd_outproj",
    )(xbc, xbc, xbc, dt, z, ya, x2, mod3, conv_w8, conv_b, dtb_row, alog_row, dsk_row, norm_g, w_out_b,
      expand)


def _route(logits):
    lane = lax.broadcasted_iota(jnp.int32, logits.shape, 1)
    big = jnp.int32(LANES)
    neg = -jnp.inf
    gmask = (lane >= N_EXPERTS) & (lane < N_EXPERTS + N_EGROUPS)
    gl = jnp.where(gmask, logits, neg)
    gmax = jnp.max(gl, axis=-1, keepdims=True)
    gidx = jnp.min(jnp.where(gl == gmax, lane, big), axis=-1, keepdims=True) - N_EXPERTS
    p_g = 1.0 / jnp.sum(jnp.where(gmask, jnp.exp(gl - gmax), 0.0), axis=-1, keepdims=True)
    lo = gidx * EXPERTS_PER_GROUP
    emask = (lane >= lo) & (lane < lo + EXPERTS_PER_GROUP)
    el = jnp.where(emask, logits, neg)
    v1 = jnp.max(el, axis=-1, keepdims=True)
    i1 = jnp.min(jnp.where(el == v1, lane, big), axis=-1, keepdims=True)
    el2 = jnp.where(lane == i1, neg, el)
    v2 = jnp.max(el2, axis=-1, keepdims=True)
    i2 = jnp.min(jnp.where(el2 == v2, lane, big), axis=-1, keepdims=True)
    e2 = jnp.exp(v2 - v1)
    den = 1.0 + e2
    w1 = p_g / den
    w2 = p_g * e2 / den
    return i1, i2, w1, w2


def _router_body(x_ref, shift_ref, scale_ref, g2_ref, wr_ref, br_ref, w1_ref, w3_ref, w2_ref,
                 h_ref, rt_ref, cnt_ref, w1b_ref, w3b_ref, w2b_ref):
    w1b_ref[...] = w1_ref[...].astype(BF16)
    w3b_ref[...] = w3_ref[...].astype(BF16)
    w2b_ref[...] = w2_ref[...].astype(BF16)
    counts = jnp.zeros((1, LANES), F32)
    for part in range(TM_MOE // ROUTER_ROWS):
        rows = slice(part * ROUTER_ROWS, (part + 1) * ROUTER_ROWS)
        h = _rms(x_ref[rows, :]) * g2_ref[...]
        h = h * (1.0 + scale_ref[...]) + shift_ref[...]
        hb = h.astype(BF16)
        h_ref[rows, :] = hb
        h_lo = (h - hb.astype(F32)).astype(BF16)
        both = jnp.dot(hb, wr_ref[...], preferred_element_type=F32)
        logits = (both[:, :LANES] + both[:, LANES:]
                  + jnp.dot(h_lo, wr_ref[:, :LANES], preferred_element_type=F32)) + br_ref[...]
        i1, i2, w1, w2 = _route(logits)
        lane = lax.broadcasted_iota(jnp.int32, (ROUTER_ROWS, LANES), 1)
        rt_ref[rows, :] = jnp.where(lane == 0, i1.astype(F32), jnp.where(lane == 1, i2.astype(F32),
                                    jnp.where(lane == 2, w1, jnp.where(lane == 3, w2, 0.0))))
        mask = jnp.where(lane == i1, 1.0, jnp.where(lane == i2, 1.0, 0.0))
        counts = counts + jnp.sum(mask, axis=0, keepdims=True)
    cnt_ref[...] = jnp.broadcast_to(counts, (SUBLANES, LANES))


def _router(x1, mod3, norm2_g, w_router, b_router, w1, w3, w2, seq):
    t = x1.shape[0]
    tiles_per_seq = seq // TM_MOE
    n_tiles = t // TM_MOE
    up_rows = N_EXPERTS * D_MODEL // n_tiles
    down_rows = N_EXPERTS * D_EXPERT // n_tiles
    assert up_rows * n_tiles == N_EXPERTS * D_MODEL and up_rows % SEG_ALIGN == 0
    assert down_rows * n_tiles == N_EXPERTS * D_EXPERT and down_rows % SEG_ALIGN == 0
    row = lambda i: (i, 0)
    const2 = lambda i: (0, 0)
    outs = pl.pallas_call(
        _router_body,
        grid=(n_tiles,),
        in_specs=[
            pl.BlockSpec((TM_MOE, D_MODEL), row),
            pl.BlockSpec((None, 1, D_MODEL), lambda i: (i // tiles_per_seq, 0, 3)),
            pl.BlockSpec((None, 1, D_MODEL), lambda i: (i // tiles_per_seq, 0, 4)),
            pl.BlockSpec((1, D_MODEL), const2),
            pl.BlockSpec((D_MODEL, 2 * LANES), const2),
            pl.BlockSpec((1, LANES), const2),
            pl.BlockSpec((up_rows, D_EXPERT), row),
            pl.BlockSpec((up_rows, D_EXPERT), row),
            pl.BlockSpec((down_rows, D_MODEL), row),
        ],
        out_specs=[
            pl.BlockSpec((TM_MOE, D_MODEL), row),
            pl.BlockSpec((TM_MOE, LANES), row),
            pl.BlockSpec((SUBLANES, LANES), row),
            pl.BlockSpec((up_rows, D_EXPERT), row),
            pl.BlockSpec((up_rows, D_EXPERT), row),
            pl.BlockSpec((down_rows, D_MODEL), row),
        ],
        out_shape=[
            jax.ShapeDtypeStruct((t, D_MODEL), BF16),
            jax.ShapeDtypeStruct((t, LANES), F32),
            jax.ShapeDtypeStruct((n_tiles * SUBLANES, LANES), F32),
            jax.ShapeDtypeStruct((N_EXPERTS * D_MODEL, D_EXPERT), BF16),
            jax.ShapeDtypeStruct((N_EXPERTS * D_MODEL, D_EXPERT), BF16),
            jax.ShapeDtypeStruct((N_EXPERTS * D_EXPERT, D_MODEL), BF16),
        ],
        compiler_params=pltpu.CompilerParams(
            dimension_semantics=("arbitrary",), vmem_limit_bytes=VMEM_LIMIT),
        name="moe_router",
    )(x1, mod3, mod3, norm2_g, w_router, b_router, w1.reshape(-1, D_EXPERT), w3.reshape(-1, D_EXPERT),
      w2.reshape(-1, D_MODEL))
    h2, rt, counts, w1b, w3b, w2b = outs
    return (h2, rt, counts, w1b.reshape(N_EXPERTS, D_MODEL, D_EXPERT), w3b.reshape(N_EXPERTS, D_MODEL, D_EXPERT),
            w2b.reshape(N_EXPERTS, D_EXPERT, D_MODEL))


def _row_copy(vmem_buf, v_start, hbm_ref, g_start, size, sem, to_hbm):
    v = vmem_buf.at[pl.ds(v_start, size), :]
    g = hbm_ref.at[pl.ds(g_start, size), :]
    return pltpu.make_async_copy(v, g, sem) if to_hbm else pltpu.make_async_copy(g, v, sem)


def _copy_rows(n, vmem_buf, v0, hbm_ref, g0, sem, *, to_hbm, wait):
    def run(v_off, g_off, size):
        v_start = 0 if v0 is None else pl.multiple_of(v0 + v_off, SEG_ALIGN)
        cp = _row_copy(vmem_buf, v_start, hbm_ref, pl.multiple_of(g0 + g_off, SEG_ALIGN), size, sem, to_hbm)
        if wait:
            cp.wait()
        else:
            cp.start()

    n_big = n // PIECE_ROWS

    def big_piece(k, carry):
        run(k * PIECE_ROWS, k * PIECE_ROWS, PIECE_ROWS)
        return carry

    lax.fori_loop(0, n_big, big_piece, 0)
    base = n_big * PIECE_ROWS
    rem = n - base
    for size in REM_SIZES:
        done = base + (rem // (2 * size)) * (2 * size)

        @pl.when((rem & size) != 0)
        def _():
            run(done, done, size)


def _segment_copies(seg_ref, loc_ref, goff_ref, tile, vmem_buf, hbm_ref, sem, *, to_hbm):
    base = tile * N_EXPERTS

    def per_expert(e, carry):
        _copy_rows(seg_ref[base + e], vmem_buf, loc_ref[base + e], hbm_ref, goff_ref[base + e], sem,
                   to_hbm=to_hbm, wait=False)
        return carry

    lax.fori_loop(0, N_EXPERTS, per_expert, 0)


def _tile_rows(seg_ref, loc_ref, tile):
    last = tile * N_EXPERTS + N_EXPERTS - 1
    return loc_ref[last] + seg_ref[last]


def _segment_wait(seg_ref, loc_ref, tile, vmem_buf, hbm_ref, sem, *, to_hbm):
    total = _tile_rows(seg_ref, loc_ref, tile)
    for size in TOTAL_SIZES:
        @pl.when((total & size) != 0)
        def _():
            _row_copy(vmem_buf, 0, hbm_ref, 0, size, sem, to_hbm).wait()


def _dispatch_body(seg_ref, loc_ref, goff_ref, fill_ref, h_ref, rt_ref, locrow_ref, pos_ref, xs_hbm, buf, sem,
                   *, n_tiles, n_row_tiles):
    i = pl.program_id(0)
    slot = i % 2

    start = functools.partial(_segment_copies, seg_ref, loc_ref, goff_ref, hbm_ref=xs_hbm, to_hbm=True)
    wait = functools.partial(_segment_wait, seg_ref, loc_ref, hbm_ref=xs_hbm, to_hbm=True)

    @pl.when(i >= 2)
    def _():
        wait(i - 2, buf.at[slot], sem=sem.at[slot])

    rt = rt_ref[...]
    lane = lax.broadcasted_iota(jnp.int32, (TM_MOE, LANES), 1)
    lanef = lane.astype(F32)
    e1 = rt[:, 0:1]
    e2 = rt[:, 1:2]
    mask = jnp.where(lanef == e1, 1.0, jnp.where(lanef == e2, 1.0, 0.0))
    r = lax.broadcasted_iota(jnp.int32, (TM_MOE, TM_MOE), 0)
    cc = lax.broadcasted_iota(jnp.int32, (TM_MOE, TM_MOE), 1)
    strict = jnp.where(r > cc, 1.0, 0.0).astype(BF16)
    rank = jnp.dot(strict, mask.astype(BF16), preferred_element_type=F32)
    posall = rank + locrow_ref[...]
    pos1 = jnp.sum(jnp.where(lanef == e1, posall, 0.0), axis=-1, keepdims=True)
    pos2 = jnp.sum(jnp.where(lanef == e2, posall, 0.0), axis=-1, keepdims=True)
    pp = jnp.where(lane == 0, pos1, jnp.where(lane == 1, pos2, rt))
    pos_ref[...] = pp

    slots_t = [pp[k * LANES:(k + 1) * LANES, :].T for k in range(TM_MOE // LANES)]

    rowp = lax.broadcasted_iota(jnp.int32, (R_LOC, LANES), 0).astype(F32)
    perm = jnp.concatenate(
        [jnp.where(rowp == st[0:1, :], 1.0, jnp.where(rowp == st[1:2, :], 1.0, 0.0)).astype(BF16)
         for st in slots_t], axis=1)
    buf[slot] = jnp.dot(perm, h_ref[...], preferred_element_type=F32).astype(BF16)
    start(i, buf.at[slot], sem=sem.at[slot])

    @pl.when(i == n_tiles - 1)
    def _():
        if n_tiles > 1:
            wait(i - 1, buf.at[1 - slot], sem=sem.at[1 - slot])
        wait(i, buf.at[slot], sem=sem.at[slot])
        zsrc = buf.at[1 - slot]
        zsem = sem.at[1 - slot]
        zsrc[0:TR_EXPERT, :] = jnp.zeros((TR_EXPERT, D_MODEL), BF16)
        for waiting in (False, True):
            def per_expert(e, carry):
                _copy_rows(fill_ref[N_EXPERTS + e], zsrc, None, xs_hbm, fill_ref[e], zsem,
                           to_hbm=True, wait=waiting)
                return carry

            def per_row_tile(k, carry):
                cp = _row_copy(zsrc, 0, xs_hbm, pl.multiple_of(k * TR_EXPERT, TR_EXPERT), TR_EXPERT, zsem, True)
                if waiting:
                    cp.wait()
                else:
                    cp.start()
                return carry

            lax.fori_loop(0, N_EXPERTS, per_expert, 0)
            lax.fori_loop(fill_ref[2 * N_EXPERTS], n_row_tiles, per_row_tile, 0)


def _dispatch(h2, rt, seg, loc, goff, fill, locrow, n_row_tiles):
    t = h2.shape[0]
    n_tiles = t // TM_MOE
    row = lambda i, *_: (i, 0)
    grid_spec = pltpu.PrefetchScalarGridSpec(
        num_scalar_prefetch=4,
        grid=(n_tiles,),
        in_specs=[
            pl.BlockSpec((TM_MOE, D_MODEL), row),
            pl.BlockSpec((TM_MOE, LANES), row),
            pl.BlockSpec((None, 1, LANES), lambda i, *_: (i, 0, 0)),
        ],
        out_specs=[
            pl.BlockSpec((TM_MOE, LANES), row),
            pl.BlockSpec(memory_space=pl.ANY),
        ],
        scratch_shapes=[
            pltpu.VMEM((2, R_LOC, D_MODEL), BF16),
            pltpu.SemaphoreType.DMA((2,)),
        ],
    )
    return pl.pallas_call(
        functools.partial(_dispatch_body, n_tiles=n_tiles, n_row_tiles=n_row_tiles),
        grid_spec=grid_spec,
        out_shape=[
            jax.ShapeDtypeStruct((t, LANES), F32),
            jax.ShapeDtypeStruct((n_row_tiles * TR_EXPERT, D_MODEL), BF16),
        ],
        compiler_params=pltpu.CompilerParams(
            dimension_semantics=("arbitrary",), vmem_limit_bytes=VMEM_LIMIT),
        name="moe_dispatch",
    )(seg, loc, goff, fill, h2, rt, locrow)


def _expert_body(te_ref, tw_ref, na_ref, xs_ref, w1a_ref, w3a_ref, w2a_ref, w1b_ref, w3b_ref, w2b_ref, y_ref):
    del tw_ref
    r = pl.program_id(0)
    n_active = na_ref[0]
    weights = ((w1a_ref, w3a_ref, w2a_ref), (w1b_ref, w3b_ref, w2b_ref))

    def swiglu(half, rows=None):
        w1_ref, w3_ref, w2_ref = weights[half]
        rows = slice(half * TR_EXPERT, (half + 1) * TR_EXPERT) if rows is None else rows
        x = xs_ref[rows, :]
        a = _silu(jnp.dot(x, w1_ref[...], preferred_element_type=F32))
        a = a * jnp.dot(x, w3_ref[...], preferred_element_type=F32)
        y_ref[rows, :] = jnp.dot(a.astype(BF16), w2_ref[...], preferred_element_type=F32).astype(BF16)

    first = EXPERT_TILES_PER_STEP * r
    both = first + 1 < n_active
    same = te_ref[first] == te_ref[first + 1]

    @pl.when(both & same)
    def _():
        swiglu(0, slice(0, EXPERT_TILES_PER_STEP * TR_EXPERT))

    @pl.when(both & jnp.logical_not(same))
    def _():
        swiglu(0)
        swiglu(1)

    @pl.when(first + 1 == n_active)
    def _():
        swiglu(0)
        y_ref[TR_EXPERT:, :] = jnp.zeros((TR_EXPERT, D_MODEL), BF16)

    @pl.when(first >= n_active)
    def _():
        y_ref[...] = jnp.zeros_like(y_ref)


def _experts(xs, tile_expert, tile_weights, n_active, w1b, w3b, w2b):
    n_rows = xs.shape[0]
    step_rows = EXPERT_TILES_PER_STEP * TR_EXPERT
    x_block = lambda r, te, tw, na: (jnp.minimum(r, (na[0] - 1) // EXPERT_TILES_PER_STEP), 0)
    w_specs = []
    for half in range(EXPERT_TILES_PER_STEP):
        w_block = lambda r, te, tw, na, half=half: (tw[EXPERT_TILES_PER_STEP * r + half], 0, 0)
        w_specs += [pl.BlockSpec((None, D_MODEL, D_EXPERT), w_block),
                    pl.BlockSpec((None, D_MODEL, D_EXPERT), w_block),
                    pl.BlockSpec((None, D_EXPERT, D_MODEL), w_block)]
    grid_spec = pltpu.PrefetchScalarGridSpec(
        num_scalar_prefetch=3,
        grid=(n_rows // step_rows,),
        in_specs=[pl.BlockSpec((step_rows, D_MODEL), x_block)] + w_specs,
        out_specs=pl.BlockSpec((step_rows, D_MODEL), lambda r, te, tw, na: (r, 0)),
    )
    return pl.pallas_call(
        _expert_body,
        grid_spec=grid_spec,
        out_shape=jax.ShapeDtypeStruct((n_rows, D_MODEL), BF16),
        compiler_params=pltpu.CompilerParams(
            dimension_semantics=("arbitrary",), vmem_limit_bytes=VMEM_LIMIT),
        name="moe_experts",
    )(tile_expert, tile_weights, n_active, xs, w1b, w3b, w2b, w1b, w3b, w2b)


def _combine_body(seg_ref, loc_ref, goff_ref, y_hbm, pos_ref, x_ref, gate_ref, fg_ref, o_ref, buf, sem, *,
                  n_tiles, final_norm):
    i = pl.program_id(0)
    slot = i % 2
    start = functools.partial(_segment_copies, seg_ref, loc_ref, goff_ref, hbm_ref=y_hbm, to_hbm=False)

    @pl.when(i == 0)
    def _():
        buf[...] = jnp.zeros_like(buf)
        start(0, buf.at[0], sem=sem.at[0])

    @pl.when(i + 1 < n_tiles)
    def _():
        start(i + 1, buf.at[1 - slot], sem=sem.at[1 - slot])

    _segment_wait(seg_ref, loc_ref, i, buf.at[slot], y_hbm, sem.at[slot], to_hbm=False)
    for part in range(TM_MOE // COMBINE_ROWS):
        rows = slice(part * COMBINE_ROWS, (part + 1) * COMBINE_ROWS)
        pp = pos_ref[rows, :]
        colp = lax.broadcasted_iota(jnp.int32, (COMBINE_ROWS, R_LOC), 1).astype(F32)
        wc = jnp.where(colp == pp[:, 0:1], pp[:, 2:3], jnp.where(colp == pp[:, 1:2], pp[:, 3:4], 0.0))
        moe = jnp.dot(wc.astype(BF16), buf[slot], preferred_element_type=F32)
        x2 = x_ref[rows, :] + gate_ref[...] * moe
        o_ref[rows, :] = _rms(x2) * fg_ref[...] if final_norm else x2


def _combine(y, pos, x1, mod3, final_g, seg, loc, goff, seq, final_norm):
    t = x1.shape[0]
    n_tiles = t // TM_MOE
    tiles_per_seq = seq // TM_MOE
    row = lambda i, *_: (i, 0)
    grid_spec = pltpu.PrefetchScalarGridSpec(
        num_scalar_prefetch=3,
        grid=(n_tiles,),
        in_specs=[
            pl.BlockSpec(memory_space=pl.ANY),
            pl.BlockSpec((TM_MOE, LANES), row),
            pl.BlockSpec((TM_MOE, D_MODEL), row),
            pl.BlockSpec((None, 1, D_MODEL), lambda i, *_: (i // tiles_per_seq, 0, 5)),
            pl.BlockSpec((1, D_MODEL), lambda i, *_: (0, 0)),
        ],
        out_specs=pl.BlockSpec((TM_MOE, D_MODEL), row),
        scratch_shapes=[
            pltpu.VMEM((2, R_LOC, D_MODEL), BF16),
            pltpu.SemaphoreType.DMA((2,)),
        ],
    )
    return pl.pallas_call(
        functools.partial(_combine_body, n_tiles=n_tiles, final_norm=final_norm),
        grid_spec=grid_spec,
        out_shape=jax.ShapeDtypeStruct((t, D_MODEL), F32),
        compiler_params=pltpu.CompilerParams(
            dimension_semantics=("arbitrary",), vmem_limit_bytes=VMEM_LIMIT),
        name="moe_combine",
    )(seg, loc, goff, y, pos, x1, mod3, final_g)


def _moe_plan(counts, n_tiles, n_row_tiles):
    cnt = counts.reshape(n_tiles, SUBLANES, LANES)[:, 0, :N_EXPERTS].astype(jnp.int32)
    seg = (cnt + SEG_ALIGN - 1) // SEG_ALIGN * SEG_ALIGN
    before_e = jnp.arange(N_EXPERTS)[:, None] < jnp.arange(N_EXPERTS)[None, :]
    before_t = jnp.arange(n_tiles)[:, None] > jnp.arange(n_tiles)[None, :]
    loc = jnp.sum(jnp.where(before_e[None], seg[:, :, None], 0), axis=1)
    tot = jnp.sum(seg, axis=0)
    region = (tot + TR_EXPERT - 1) // TR_EXPERT * TR_EXPERT
    gstart = jnp.sum(jnp.where(before_e, region[:, None], 0), axis=0)
    gend = gstart + region
    goff = gstart[None, :] + jnp.sum(jnp.where(before_t[:, :, None], seg[None], 0), axis=1)
    n_active = gend[-1] // TR_EXPERT
    tile_row0 = jnp.arange(n_row_tiles, dtype=jnp.int32) * TR_EXPERT
    last_row0 = (n_active - 1) * TR_EXPERT
    te = jnp.sum(gend[None, :] <= jnp.minimum(tile_row0, last_row0)[:, None], axis=1).astype(jnp.int32)
    te = jnp.minimum(te, N_EXPERTS - 1)
    te2 = te.reshape(-1, EXPERT_TILES_PER_STEP)
    cand = jnp.where(te2[:, 1] != te2[:, 0], te2[:, 1], 0)
    steps = jnp.arange(te2.shape[0])
    held = jnp.max(jnp.where(steps[None, :] <= steps[:, None], cand[None, :], 0), axis=1)
    tw = jnp.stack([te2[:, 0], held], axis=1).reshape(-1).astype(jnp.int32)
    locrow = _pad_cols(loc.astype(F32), LANES).reshape(n_tiles, 1, LANES)
    n_active = n_active.reshape(1).astype(jnp.int32)
    fill = jnp.concatenate([gstart + tot, region - tot, n_active]).astype(jnp.int32)
    return (seg.reshape(-1), loc.reshape(-1).astype(jnp.int32), goff.reshape(-1).astype(jnp.int32), fill,
            locrow, te, tw, n_active)


def _moe_final(x1, mod3, norm2_g, w_router, b_router, w1, w3, w2, final_g, seq, final_norm):
    t = x1.shape[0]
    n_tiles = t // TM_MOE
    max_rows = N_ASSIGN * t + (SEG_ALIGN - 1) * N_EXPERTS * n_tiles + N_EXPERTS * (TR_EXPERT - SEG_ALIGN)
    step_rows = EXPERT_TILES_PER_STEP * TR_EXPERT
    n_row_tiles = -(-max_rows // step_rows) * EXPERT_TILES_PER_STEP
    h2, rt, counts, w1b, w3b, w2b = _router(x1, mod3, norm2_g, w_router, b_router, w1, w3, w2, seq)
    seg, loc, goff, fill, locrow, te, tw, n_active = _moe_plan(counts, n_tiles, n_row_tiles)
    pos, xs = _dispatch(h2, rt, seg, loc, goff, fill, locrow, n_row_tiles)
    y = _experts(xs, te, tw, n_active, w1b, w3b, w2b)
    return _combine(y, pos, x1, mod3, final_g, seg, loc, goff, seq, final_norm)


def _pad_cols(a, width):
    return jnp.pad(a, ((0, 0), (0, width - a.shape[1])))


def kernel(x, c, w_ada, b_ada, norm1_g, w_in, b_in, gmlp_ln_g, gmlp_ln_b, gmlp_w_s, gmlp_b_s, gmlp_out_g, conv_w, conv_b, a_log_f, a_log_b, dt_bias_f, dt_bias_b, d_skip, ssd_norm_g, w_out, norm2_g, w_router_g, b_router_g, w_router_e, b_router_e, w1, w3, w2, final_g):
    bn, seq, _ = x.shape
    depth = w_ada.shape[0]
    x2 = x.reshape(bn * seq, D_MODEL)

    head_rows = jnp.arange(LANES)[:, None]
    head_cols = jnp.arange(SSD_WIDTH)[None, :] // SSD_HEAD_DIM
    expand = jnp.stack([head_rows == head_cols, head_rows == head_cols + SSD_HEADS]).astype(BF16)

    for l in range(depth):
        mod3 = _modulation(c, w_ada, b_ada[l], l).reshape(bn, 1, N_MOD * D_MODEL)

        w_in_p = _pad_cols(w_in[l].astype(BF16), IN_COLS_PAD)
        b_in_p = _pad_cols(b_in[l][None, :], IN_COLS_PAD)
        bs = jnp.repeat(gmlp_b_s[l].T, LANES, axis=1)
        bs_tile = jnp.tile(bs, (TM_PROJ // CHUNK, 1))
        ya, z, xbc, dt = _inproj_gmlp(
            x2, mod3, norm1_g[l][None, :], w_in_p, b_in_p, gmlp_ln_g[l][None, :], gmlp_ln_b[l][None, :],
            gmlp_w_s[l].astype(BF16), bs_tile, gmlp_out_g[l][None, :], seq)

        conv_w8 = jnp.pad(conv_w[l], ((0, SUBLANES - CONV_WIDTH), (0, 0)))
        dtb_row = _pad_cols(jnp.concatenate([dt_bias_f[l], dt_bias_b[l]])[None, :], LANES)
        alog_row = _pad_cols(jnp.concatenate([a_log_f[l], a_log_b[l]])[None, :], LANES)
        dsk_row = jnp.repeat(d_skip[l], SSD_HEAD_DIM)[None, :]
        x2 = _ssd_outproj(xbc, dt, z, ya, x2, mod3, conv_w8, conv_b[l][None, :], dtb_row, alog_row, dsk_row,
                          ssd_norm_g[l][None, :], w_out[l].astype(BF16), expand, bn, seq)

        w_re = jnp.transpose(w_router_e[l], (1, 0, 2)).reshape(D_MODEL, N_EXPERTS)
        w_router = _pad_cols(jnp.concatenate([w_re, w_router_g[l]], axis=1), LANES)
        w_router_hi = w_router.astype(BF16)
        w_router = jnp.concatenate([w_router_hi, (w_router - w_router_hi.astype(F32)).astype(BF16)], axis=1)
        b_router = _pad_cols(jnp.concatenate([b_router_e[l].reshape(-1), b_router_g[l]])[None, :], LANES)
        x2 = _moe_final(x2, mod3, norm2_g[l][None, :], w_router, b_router, w1[l], w3[l], w2[l],
                        final_g[None, :], seq, final_norm=(l == depth - 1))
    return x2.reshape(bn, seq, D_MODEL)
```

```python
import functools
import math

import jax
import jax.numpy as jnp
from jax import lax
from jax.experimental import pallas as pl
from jax.experimental.pallas import tpu as pltpu

F32 = jnp.float32
BF16 = jnp.bfloat16
HIGHEST = lax.Precision.HIGHEST

D_MODEL = 1024
N_MOD = 6
GMLP_WIDTH = 1024
GMLP_HEADS = 8
CHUNK = 128
SSD_STEP_ROWS = 4 * CHUNK
SSD_WIDTH = 1024
SSD_HEADS = 16
SSD_HEAD_DIM = 64
SSD_GROUPS = 2
SSD_STATE = 128
GROUP_WIDTH = SSD_WIDTH // SSD_GROUPS
CONV_WIDTH = 5
CONV_CH = SSD_WIDTH + 2 * SSD_GROUPS * SSD_STATE
N_EGROUPS = 4
EXPERTS_PER_GROUP = 8
N_EXPERTS = 32
D_EXPERT = 256
EPS = 1e-6
LOG2_E = 1.4426950408889634

LANES = 128
SUBLANES = 8
COL_U, COL_V, COL_Z, COL_XBC, COL_DT = 0, 1024, 2048, 3072, 4608
IN_COLS = 4640
IN_COLS_PAD = COL_DT + LANES
TM_PROJ = 1024
PROJ_ROWS = 512
COMBINE_ROWS = 256
ROUTER_ROWS = 256
TM_MOE = 512
TR_EXPERT = 512
EXPERT_TILES_PER_STEP = 2
N_ASSIGN = 2
SEG_ALIGN = 2 * SUBLANES
R_LOC = N_ASSIGN * TM_MOE + N_EXPERTS * SEG_ALIGN
PIECE_ROWS = 128
REM_SIZES = tuple(PIECE_ROWS >> s for s in range(1, PIECE_ROWS.bit_length()) if PIECE_ROWS >> s >= SEG_ALIGN)
TOTAL_SIZES = tuple(1 << s for s in range(R_LOC.bit_length() - 1, -1, -1) if 1 << s >= SEG_ALIGN)
VMEM_LIMIT = 56 * 1024 * 1024


def _silu(v):
    return v * jax.nn.sigmoid(v)


def _gelu(v):
    return 0.5 * v * (1.0 + lax.erf(v * math.sqrt(0.5)))


def _softplus(v):
    return jnp.maximum(v, 0.0) + jnp.log1p(jnp.exp(-jnp.abs(v)))


def _rms(v):
    return v * lax.rsqrt(jnp.mean(v * v, axis=-1, keepdims=True) + EPS)


def _mod_body(c_ref, w_ref, b_ref, o_ref):
    ca = _silu(c_ref[...])
    o_ref[...] = jnp.dot(ca, w_ref[...], precision=HIGHEST, preferred_element_type=F32) + b_ref[...]


def _modulation(c, w_ada, b_ada, layer):
    bn = c.shape[0]
    return pl.pallas_call(
        _mod_body,
        grid=(N_MOD,),
        in_specs=[
            pl.BlockSpec((bn, D_MODEL), lambda j: (0, 0)),
            pl.BlockSpec((None, D_MODEL, D_MODEL), lambda j: (layer, 0, j)),
            pl.BlockSpec((1, D_MODEL), lambda j: (0, j)),
        ],
        out_specs=pl.BlockSpec((bn, D_MODEL), lambda j: (0, j)),
        out_shape=jax.ShapeDtypeStruct((bn, N_MOD * D_MODEL), F32),
        name="adaln_mod",
    )(c, w_ada, b_ada.reshape(1, -1))


def _inproj_body(x_ref, shift_ref, scale_ref, g_ref, w_ref, b_ref, lng_ref, lnb_ref, ws_ref, bs_ref,
                 og_ref, ya_ref, z_ref, xbc_ref, dt_ref, mix_scr):
    gain = g_ref[...] * (1.0 + scale_ref[...])
    for part in range(TM_PROJ // PROJ_ROWS):
        rows = slice(part * PROJ_ROWS, (part + 1) * PROJ_ROWS)
        hb = (_rms(x_ref[rows, :]) * gain + shift_ref[...]).astype(BF16)

        def proj(lo, hi, hb=hb):
            return jnp.dot(hb, w_ref[:, lo:hi], preferred_element_type=F32) + b_ref[:, lo:hi]

        z_ref[rows, :] = proj(COL_Z, COL_XBC).astype(BF16)
        xbc_ref[rows, :] = proj(COL_XBC, COL_DT)
        dt_ref[rows, :] = proj(COL_DT, IN_COLS_PAD)

        v = _gelu(proj(COL_V, COL_Z))
        mu = jnp.mean(v, axis=-1, keepdims=True)
        vc = v - mu
        var = jnp.mean(vc * vc, axis=-1, keepdims=True)
        vn = (vc * lax.rsqrt(var + EPS) * lng_ref[...] + lnb_ref[...]).astype(BF16)
        n_chunks = PROJ_ROWS // CHUNK
        for hd in range(GMLP_HEADS):
            cols = slice(hd * LANES, (hd + 1) * LANES)
            rhs = jnp.concatenate([vn[c * CHUNK:(c + 1) * CHUNK, cols] for c in range(n_chunks)], axis=1)
            res = jnp.dot(ws_ref[hd], rhs, preferred_element_type=F32)
            for c in range(n_chunks):
                mix_scr[part * PROJ_ROWS + c * CHUNK:part * PROJ_ROWS + (c + 1) * CHUNK, cols] = (
                    res[:, c * LANES:(c + 1) * LANES])
        u = _gelu(proj(COL_U, COL_V))
        out = u * (mix_scr[rows, :] + bs_ref[rows, :])
        ya_ref[rows, :] = (_rms(out) * og_ref[...]).astype(BF16)


def _inproj_gmlp(x2, mod3, norm1_g, w_in_p, b_in_p, ln_g, ln_b, w_s, bs_tile, out_g, seq):
    t = x2.shape[0]
    tiles_per_seq = seq // TM_PROJ
    row = lambda i: (i, 0)
    const2 = lambda i: (0, 0)
    return pl.pallas_call(
        _inproj_body,
        grid=(t // TM_PROJ,),
        in_specs=[
            pl.BlockSpec((TM_PROJ, D_MODEL), row),
            pl.BlockSpec((None, 1, D_MODEL), lambda i: (i // tiles_per_seq, 0, 0)),
            pl.BlockSpec((None, 1, D_MODEL), lambda i: (i // tiles_per_seq, 0, 1)),
            pl.BlockSpec((1, D_MODEL), const2),
            pl.BlockSpec((D_MODEL, IN_COLS_PAD), const2, pipeline_mode=pl.Buffered(1)),
            pl.BlockSpec((1, IN_COLS_PAD), const2),
            pl.BlockSpec((1, GMLP_WIDTH), const2),
            pl.BlockSpec((1, GMLP_WIDTH), const2),
            pl.BlockSpec((GMLP_HEADS, CHUNK, CHUNK), lambda i: (0, 0, 0)),
            pl.BlockSpec((TM_PROJ, GMLP_WIDTH), const2, pipeline_mode=pl.Buffered(1)),
            pl.BlockSpec((1, GMLP_WIDTH), const2),
        ],
        out_specs=[
            pl.BlockSpec((TM_PROJ, GMLP_WIDTH), row),
            pl.BlockSpec((TM_PROJ, SSD_WIDTH), row),
            pl.BlockSpec((TM_PROJ, CONV_CH), row),
            pl.BlockSpec((TM_PROJ, LANES), row),
        ],
        out_shape=[
            jax.ShapeDtypeStruct((t, GMLP_WIDTH), BF16),
            jax.ShapeDtypeStruct((t, SSD_WIDTH), BF16),
            jax.ShapeDtypeStruct((t, CONV_CH), F32),
            jax.ShapeDtypeStruct((t, LANES), F32),
        ],
        scratch_shapes=[pltpu.VMEM((TM_PROJ, GMLP_WIDTH), F32)],
        compiler_params=pltpu.CompilerParams(
            dimension_semantics=("arbitrary",), vmem_limit_bytes=VMEM_LIMIT),
        name="inproj_gmlp",
    )(x2, mod3, mod3, norm1_g, w_in_p, b_in_p, ln_g, ln_b, w_s, bs_tile, out_g)


def _ssd_chunk(act, dtv, a_row, expand, state, rev):
    off = SSD_HEADS if rev else 0
    row = lax.broadcasted_iota(jnp.int32, (CHUNK, CHUNK), 0)
    col = lax.broadcasted_iota(jnp.int32, (CHUNK, CHUNK), 1)
    lower = row >= col
    upper = row <= col
    keep = upper if rev else lower
    da = dtv * a_row
    cs = jnp.dot(keep.astype(F32), da, precision=HIGHEST, preferred_element_type=F32)
    cs2 = cs * LOG2_E
    col_t = cs2.T - jnp.log2(dtv.T)
    tot = cs[0:1, :] if rev else cs[CHUNK - 1:CHUNK, :]

    xs = act[:, :SSD_WIDTH]
    lane = lax.broadcasted_iota(jnp.int32, (CHUNK, LANES), 1)
    first_half = lane < SSD_HEAD_DIM
    zero = jnp.zeros((), BF16)

    stack = jnp.concatenate(
        [jnp.exp(cs), dtv * jnp.exp(tot - cs), jnp.broadcast_to(jnp.exp(tot), (SUBLANES, LANES))], axis=0)
    stack_x = jnp.dot(stack.astype(BF16), expand, preferred_element_type=F32)
    into_x = stack_x[:CHUNK]
    w_x = stack_x[CHUNK:2 * CHUNK]
    cd_x = stack_x[2 * CHUNK:2 * CHUNK + 1]
    xw = xs * w_x.astype(BF16)

    pieces = []
    for g in range(SSD_GROUPS):
        bg = act[:, SSD_WIDTH + g * SSD_STATE:SSD_WIDTH + (g + 1) * SSD_STATE]
        cg = act[:, SSD_WIDTH + SSD_GROUPS * SSD_STATE + g * SSD_STATE:
                 SSD_WIDTH + SSD_GROUPS * SSD_STATE + (g + 1) * SSD_STATE]
        scores = lax.dot_general(cg, bg, (((1,), (1,)), ((), ())), preferred_element_type=F32)
        heads_per_group = SSD_HEADS // SSD_GROUPS
        for pair in range(heads_per_group // 2):
            h0 = g * heads_per_group + 2 * pair
            xs_pair = xs[:, h0 * SSD_HEAD_DIM:(h0 + 2) * SSD_HEAD_DIM]
            y_pair = None
            for k in range(2):
                hh = off + h0 + k
                seg = cs2[:, hh:hh + 1] - col_t[hh:hh + 1, :]
                dec = jnp.exp2(jnp.where(keep, seg, -jnp.inf))
                m = (scores * dec).astype(BF16)
                rhs = jnp.where(first_half if k == 0 else jnp.logical_not(first_half), xs_pair, zero)
                part = jnp.dot(m, rhs, preferred_element_type=F32)
                y_pair = part if y_pair is None else y_pair + part
            pieces.append(y_pair)
    y_diag = jnp.concatenate(pieces, axis=1)

    y_off = []
    new_state = []
    for g in range(SSD_GROUPS):
        gcols = slice(g * GROUP_WIDTH, (g + 1) * GROUP_WIDTH)
        bg = act[:, SSD_WIDTH + g * SSD_STATE:SSD_WIDTH + (g + 1) * SSD_STATE]
        cg = act[:, SSD_WIDTH + SSD_GROUPS * SSD_STATE + g * SSD_STATE:
                 SSD_WIDTH + SSD_GROUPS * SSD_STATE + (g + 1) * SSD_STATE]
        prev = state[:, gcols]
        y_off.append(jnp.dot(cg, prev.astype(BF16), preferred_element_type=F32))
        bg_t = bg.astype(F32).T.astype(BF16)
        new = jnp.dot(bg_t, xw[:, gcols], preferred_element_type=F32)
        new_state.append(prev * cd_x[:, gcols] + new)
    y = y_diag + jnp.concatenate(y_off, axis=1) * into_x
    return y, jnp.concatenate(new_state, axis=1)


def _ssd_body(xbc_ref, xprev_ref, xnext_ref, dt_ref, z_ref, ya_ref, x_ref, gate_ref, cw_ref, cb_ref,
              dtb_ref, alog_ref, dsk_ref, ng_ref, wout_ref, exp_ref, o_ref,
              act_scr, dts_scr, yf_scr, st_scr, ext_scr, *, n_steps):
    d = pl.program_id(1)
    c = pl.program_id(2)
    lane1 = lax.broadcasted_iota(jnp.int32, (1, LANES), 1)
    a_row = jnp.where(lane1 < 2 * SSD_HEADS, -jnp.exp(alog_ref[...]), 0.0)
    chunk_rows = [slice(j * CHUNK, (j + 1) * CHUNK) for j in range(SSD_STEP_ROWS // CHUNK)]

    @pl.when(c == 0)
    def _():
        st_scr[...] = jnp.zeros_like(st_scr)

    def scan(act, dtv, expand, rev):
        state = st_scr[...]
        ys = [None] * len(chunk_rows)
        for j in (reversed(range(len(chunk_rows))) if rev else range(len(chunk_rows))):
            ys[j], state = _ssd_chunk(act[chunk_rows[j], :], dtv[chunk_rows[j], :], a_row, expand, state, rev)
        st_scr[...] = state
        return jnp.concatenate(ys, axis=0)

    @pl.when(d == 0)
    def _forward():
        row0 = pl.multiple_of(c * SSD_STEP_ROWS, SSD_STEP_ROWS)
        ext_scr[0:SUBLANES, :] = jnp.where(c > 0, xprev_ref[...], 0.0)
        ext_scr[SUBLANES:SUBLANES + SSD_STEP_ROWS, :] = xbc_ref[...]
        ext_scr[SUBLANES + SSD_STEP_ROWS:, :] = jnp.where(c < n_steps - 1, xnext_ref[...], 0.0)
        ext = ext_scr[...]
        n_ext = SSD_STEP_ROWS + 2 * SUBLANES
        body = slice(SUBLANES, SUBLANES + SSD_STEP_ROWS)
        down1 = pltpu.roll(ext, 1, 0)
        down2 = pltpu.roll(down1, 1, 0)
        up1 = pltpu.roll(ext, n_ext - 1, 0)
        up2 = pltpu.roll(up1, n_ext - 1, 0)
        acc = cb_ref[...]
        for k, tap in enumerate((down2, down1, ext, up1, up2)):
            acc = acc + cw_ref[k:k + 1, :] * tap[body, :]
        act = _silu(acc).astype(BF16)
        dtv = _softplus(dt_ref[...] + dtb_ref[...])
        act_scr[pl.ds(row0, SSD_STEP_ROWS), :] = act
        dts_scr[pl.ds(row0, SSD_STEP_ROWS), :] = dtv
        yf_scr[pl.ds(row0, SSD_STEP_ROWS), :] = scan(act, dtv, exp_ref[0], rev=False).astype(BF16)

    @pl.when(d == 1)
    def _backward():
        row0 = pl.multiple_of((n_steps - 1 - c) * SSD_STEP_ROWS, SSD_STEP_ROWS)
        act = act_scr[pl.ds(row0, SSD_STEP_ROWS), :]
        dtv = dts_scr[pl.ds(row0, SSD_STEP_ROWS), :]
        yb = scan(act, dtv, exp_ref[1], rev=True)
        xs = act[:, :SSD_WIDTH].astype(F32)
        y = yf_scr[pl.ds(row0, SSD_STEP_ROWS), :].astype(F32) + yb + dsk_ref[...] * xs
        y = y * _silu(z_ref[...].astype(F32))
        y = jnp.concatenate(
            [_rms(y[:, g * GROUP_WIDTH:(g + 1) * GROUP_WIDTH]) for g in range(SSD_GROUPS)], axis=1)
        y = y * ng_ref[...]
        mix = jnp.concatenate([ya_ref[...], y.astype(BF16)], axis=1)
        o = jnp.dot(mix, wout_ref[...], preferred_element_type=F32)
        o_ref[...] = x_ref[...] + gate_ref[...] * o


def _ssd_outproj(xbc, dt, z, ya, x2, mod3, conv_w8, conv_b, dtb_row, alog_row, dsk_row, norm_g, w_out_b,
                 expand, bn, seq):
    t = x2.shape[0]
    nc = seq // SSD_STEP_ROWS
    blocks8 = SSD_STEP_ROWS // SUBLANES
    last8 = t // SUBLANES - 1

    def fwd_chunk(b, d, c):
        return b * nc + c + d * (nc - 1 - c)

    def bwd_chunk(b, d, c):
        return b * nc + nc - 1 - d * c

    const2 = lambda b, d, c: (0, 0)
    return pl.pallas_call(
        functools.partial(_ssd_body, n_steps=nc),
        grid=(bn, 2, nc),
        in_specs=[
            pl.BlockSpec((SSD_STEP_ROWS, CONV_CH), lambda b, d, c: (fwd_chunk(b, d, c), 0)),
            pl.BlockSpec((SUBLANES, CONV_CH),
                         lambda b, d, c: (jnp.maximum(fwd_chunk(b, d, c) * blocks8 - 1, 0), 0)),
            pl.BlockSpec((SUBLANES, CONV_CH),
                         lambda b, d, c: (jnp.minimum((fwd_chunk(b, d, c) + 1) * blocks8, last8), 0)),
            pl.BlockSpec((SSD_STEP_ROWS, LANES), lambda b, d, c: (fwd_chunk(b, d, c), 0)),
            pl.BlockSpec((SSD_STEP_ROWS, SSD_WIDTH), lambda b, d, c: (bwd_chunk(b, d, c), 0)),
            pl.BlockSpec((SSD_STEP_ROWS, GMLP_WIDTH), lambda b, d, c: (bwd_chunk(b, d, c), 0)),
            pl.BlockSpec((SSD_STEP_ROWS, D_MODEL), lambda b, d, c: (bwd_chunk(b, d, c), 0)),
            pl.BlockSpec((None, 1, D_MODEL), lambda b, d, c: (b, 0, 2)),
            pl.BlockSpec((SUBLANES, CONV_CH), const2),
            pl.BlockSpec((1, CONV_CH), const2),
            pl.BlockSpec((1, LANES), const2),
            pl.BlockSpec((1, LANES), const2),
            pl.BlockSpec((1, SSD_WIDTH), const2),
            pl.BlockSpec((1, SSD_WIDTH), const2),
            pl.BlockSpec((GMLP_WIDTH + SSD_WIDTH, D_MODEL), const2, pipeline_mode=pl.Buffered(1)),
            pl.BlockSpec((2, LANES, SSD_WIDTH), lambda b, d, c: (0, 0, 0), pipeline_mode=pl.Buffered(1)),
        ],
        out_specs=pl.BlockSpec((SSD_STEP_ROWS, D_MODEL), lambda b, d, c: (bwd_chunk(b, d, c), 0)),
        out_shape=jax.ShapeDtypeStruct((t, D_MODEL), F32),
        scratch_shapes=[
            pltpu.VMEM((seq, CONV_CH), BF16),
            pltpu.VMEM((seq, LANES), F32),
            pltpu.VMEM((seq, SSD_WIDTH), BF16),
            pltpu.VMEM((SSD_STATE, SSD_WIDTH), F32),
            pltpu.VMEM((SSD_STEP_ROWS + 2 * SUBLANES, CONV_CH), F32),
        ],
        compiler_params=pltpu.CompilerParams(
            dimension_semantics=("arbitrary", "arbitrary", "arbitrary"), vmem_limit_bytes=VMEM_LIMIT),
        name="ssd_outproj",
    )(xbc, xbc, xbc, dt, z, ya, x2, mod3, conv_w8, conv_b, dtb_row, alog_row, dsk_row, norm_g, w_out_b,
      expand)


def _route(logits):
    lane = lax.broadcasted_iota(jnp.int32, logits.shape, 1)
    big = jnp.int32(LANES)
    neg = -jnp.inf
    gmask = (lane >= N_EXPERTS) & (lane < N_EXPERTS + N_EGROUPS)
    gl = jnp.where(gmask, logits, neg)
    gmax = jnp.max(gl, axis=-1, keepdims=True)
    gidx = jnp.min(jnp.where(gl == gmax, lane, big), axis=-1, keepdims=True) - N_EXPERTS
    p_g = 1.0 / jnp.sum(jnp.where(gmask, jnp.exp(gl - gmax), 0.0), axis=-1, keepdims=True)
    lo = gidx * EXPERTS_PER_GROUP
    emask = (lane >= lo) & (lane < lo + EXPERTS_PER_GROUP)
    el = jnp.where(emask, logits, neg)
    v1 = jnp.max(el, axis=-1, keepdims=True)
    i1 = jnp.min(jnp.where(el == v1, lane, big), axis=-1, keepdims=True)
    el2 = jnp.where(lane == i1, neg, el)
    v2 = jnp.max(el2, axis=-1, keepdims=True)
    i2 = jnp.min(jnp.where(el2 == v2, lane, big), axis=-1, keepdims=True)
    e2 = jnp.exp(v2 - v1)
    den = 1.0 + e2
    w1 = p_g / den
    w2 = p_g * e2 / den
    return i1, i2, w1, w2


def _router_body(x_ref, shift_ref, scale_ref, g2_ref, wr_ref, br_ref, w1_ref, w3_ref, w2_ref,
                 h_ref, rt_ref, cnt_ref, w1b_ref, w3b_ref, w2b_ref):
    w1b_ref[...] = w1_ref[...].astype(BF16)
    w3b_ref[...] = w3_ref[...].astype(BF16)
    w2b_ref[...] = w2_ref[...].astype(BF16)
    counts = jnp.zeros((1, LANES), F32)
    gain = g2_ref[...] * (1.0 + scale_ref[...])
    for part in range(TM_MOE // ROUTER_ROWS):
        rows = slice(part * ROUTER_ROWS, (part + 1) * ROUTER_ROWS)
        h = _rms(x_ref[rows, :]) * gain + shift_ref[...]
        hb = h.astype(BF16)
        h_ref[rows, :] = hb
        h_lo = (h - hb.astype(F32)).astype(BF16)
        both = jnp.dot(hb, wr_ref[...], preferred_element_type=F32)
        logits = (both[:, :LANES] + both[:, LANES:]
                  + jnp.dot(h_lo, wr_ref[:, :LANES], preferred_element_type=F32)) + br_ref[...]
        i1, i2, w1, w2 = _route(logits)
        lane = lax.broadcasted_iota(jnp.int32, (ROUTER_ROWS, LANES), 1)
        rt_ref[rows, :] = jnp.where(lane == 0, i1.astype(F32), jnp.where(lane == 1, i2.astype(F32),
                                    jnp.where(lane == 2, w1, jnp.where(lane == 3, w2, 0.0))))
        mask = jnp.where(lane == i1, 1.0, jnp.where(lane == i2, 1.0, 0.0))
        counts = counts + jnp.sum(mask, axis=0, keepdims=True)
    cnt_ref[...] = jnp.broadcast_to(counts, (SUBLANES, LANES))


def _router(x1, mod3, norm2_g, w_router, b_router, w1, w3, w2, seq):
    t = x1.shape[0]
    tiles_per_seq = seq // TM_MOE
    n_tiles = t // TM_MOE
    up_rows = N_EXPERTS * D_MODEL // n_tiles
    down_rows = N_EXPERTS * D_EXPERT // n_tiles
    assert up_rows * n_tiles == N_EXPERTS * D_MODEL and up_rows % SEG_ALIGN == 0
    assert down_rows * n_tiles == N_EXPERTS * D_EXPERT and down_rows % SEG_ALIGN == 0
    row = lambda i: (i, 0)
    const2 = lambda i: (0, 0)
    outs = pl.pallas_call(
        _router_body,
        grid=(n_tiles,),
        in_specs=[
            pl.BlockSpec((TM_MOE, D_MODEL), row),
            pl.BlockSpec((None, 1, D_MODEL), lambda i: (i // tiles_per_seq, 0, 3)),
            pl.BlockSpec((None, 1, D_MODEL), lambda i: (i // tiles_per_seq, 0, 4)),
            pl.BlockSpec((1, D_MODEL), const2),
            pl.BlockSpec((D_MODEL, 2 * LANES), const2),
            pl.BlockSpec((1, LANES), const2),
            pl.BlockSpec((up_rows, D_EXPERT), row),
            pl.BlockSpec((up_rows, D_EXPERT), row),
            pl.BlockSpec((down_rows, D_MODEL), row),
        ],
        out_specs=[
            pl.BlockSpec((TM_MOE, D_MODEL), row),
            pl.BlockSpec((TM_MOE, LANES), row),
            pl.BlockSpec((SUBLANES, LANES), row),
            pl.BlockSpec((up_rows, D_EXPERT), row),
            pl.BlockSpec((up_rows, D_EXPERT), row),
            pl.BlockSpec((down_rows, D_MODEL), row),
        ],
        out_shape=[
            jax.ShapeDtypeStruct((t, D_MODEL), BF16),
            jax.ShapeDtypeStruct((t, LANES), F32),
            jax.ShapeDtypeStruct((n_tiles * SUBLANES, LANES), F32),
            jax.ShapeDtypeStruct((N_EXPERTS * D_MODEL, D_EXPERT), BF16),
            jax.ShapeDtypeStruct((N_EXPERTS * D_MODEL, D_EXPERT), BF16),
            jax.ShapeDtypeStruct((N_EXPERTS * D_EXPERT, D_MODEL), BF16),
        ],
        compiler_params=pltpu.CompilerParams(
            dimension_semantics=("arbitrary",), vmem_limit_bytes=VMEM_LIMIT),
        name="moe_router",
    )(x1, mod3, mod3, norm2_g, w_router, b_router, w1.reshape(-1, D_EXPERT), w3.reshape(-1, D_EXPERT),
      w2.reshape(-1, D_MODEL))
    h2, rt, counts, w1b, w3b, w2b = outs
    return (h2, rt, counts, w1b.reshape(N_EXPERTS, D_MODEL, D_EXPERT), w3b.reshape(N_EXPERTS, D_MODEL, D_EXPERT),
            w2b.reshape(N_EXPERTS, D_EXPERT, D_MODEL))


def _row_copy(vmem_buf, v_start, hbm_ref, g_start, size, sem, to_hbm):
    v = vmem_buf.at[pl.ds(v_start, size), :]
    g = hbm_ref.at[pl.ds(g_start, size), :]
    return pltpu.make_async_copy(v, g, sem) if to_hbm else pltpu.make_async_copy(g, v, sem)


def _copy_rows(n, vmem_buf, v0, hbm_ref, g0, sem, *, to_hbm, wait):
    def run(v_off, g_off, size):
        v_start = 0 if v0 is None else pl.multiple_of(v0 + v_off, SEG_ALIGN)
        cp = _row_copy(vmem_buf, v_start, hbm_ref, pl.multiple_of(g0 + g_off, SEG_ALIGN), size, sem, to_hbm)
        if wait:
            cp.wait()
        else:
            cp.start()

    n_big = n // PIECE_ROWS

    def big_piece(k, carry):
        run(k * PIECE_ROWS, k * PIECE_ROWS, PIECE_ROWS)
        return carry

    lax.fori_loop(0, n_big, big_piece, 0)
    base = n_big * PIECE_ROWS
    rem = n - base
    for size in REM_SIZES:
        done = base + (rem // (2 * size)) * (2 * size)

        @pl.when((rem & size) != 0)
        def _():
            run(done, done, size)


def _segment_copies(seg_ref, loc_ref, goff_ref, tile, vmem_buf, hbm_ref, sem, *, to_hbm):
    base = tile * N_EXPERTS

    def per_expert(e, carry):
        _copy_rows(seg_ref[base + e], vmem_buf, loc_ref[base + e], hbm_ref, goff_ref[base + e], sem,
                   to_hbm=to_hbm, wait=False)
        return carry

    lax.fori_loop(0, N_EXPERTS, per_expert, 0)


def _tile_rows(seg_ref, loc_ref, tile):
    last = tile * N_EXPERTS + N_EXPERTS - 1
    return loc_ref[last] + seg_ref[last]


def _segment_wait(seg_ref, loc_ref, tile, vmem_buf, hbm_ref, sem, *, to_hbm):
    total = _tile_rows(seg_ref, loc_ref, tile)
    for size in TOTAL_SIZES:
        @pl.when((total & size) != 0)
        def _():
            _row_copy(vmem_buf, 0, hbm_ref, 0, size, sem, to_hbm).wait()


def _dispatch_body(seg_ref, loc_ref, goff_ref, fill_ref, h_ref, rt_ref, locrow_ref, pos_ref, xs_hbm, buf, sem,
                   *, n_tiles, n_row_tiles):
    i = pl.program_id(0)
    slot = i % 2

    start = functools.partial(_segment_copies, seg_ref, loc_ref, goff_ref, hbm_ref=xs_hbm, to_hbm=True)
    wait = functools.partial(_segment_wait, seg_ref, loc_ref, hbm_ref=xs_hbm, to_hbm=True)

    @pl.when(i >= 2)
    def _():
        wait(i - 2, buf.at[slot], sem=sem.at[slot])

    rt = rt_ref[...]
    lane = lax.broadcasted_iota(jnp.int32, (TM_MOE, LANES), 1)
    lanef = lane.astype(F32)
    e1 = rt[:, 0:1]
    e2 = rt[:, 1:2]
    mask = jnp.where(lanef == e1, 1.0, jnp.where(lanef == e2, 1.0, 0.0))
    r = lax.broadcasted_iota(jnp.int32, (TM_MOE, TM_MOE), 0)
    cc = lax.broadcasted_iota(jnp.int32, (TM_MOE, TM_MOE), 1)
    strict = jnp.where(r > cc, 1.0, 0.0).astype(BF16)
    rank = jnp.dot(strict, mask.astype(BF16), preferred_element_type=F32)
    posall = rank + locrow_ref[...]
    pos1 = jnp.sum(jnp.where(lanef == e1, posall, 0.0), axis=-1, keepdims=True)
    pos2 = jnp.sum(jnp.where(lanef == e2, posall, 0.0), axis=-1, keepdims=True)
    pp = jnp.where(lane == 0, pos1, jnp.where(lane == 1, pos2, rt))
    pos_ref[...] = pp

    slots_t = [pp[k * LANES:(k + 1) * LANES, :].T for k in range(TM_MOE // LANES)]

    rowp = lax.broadcasted_iota(jnp.int32, (R_LOC, LANES), 0).astype(F32)
    perm = jnp.concatenate(
        [jnp.where(rowp == st[0:1, :], 1.0, jnp.where(rowp == st[1:2, :], 1.0, 0.0)).astype(BF16)
         for st in slots_t], axis=1)
    buf[slot] = jnp.dot(perm, h_ref[...], preferred_element_type=F32).astype(BF16)
    start(i, buf.at[slot], sem=sem.at[slot])

    @pl.when(i == n_tiles - 1)
    def _():
        if n_tiles > 1:
            wait(i - 1, buf.at[1 - slot], sem=sem.at[1 - slot])
        wait(i, buf.at[slot], sem=sem.at[slot])
        zsrc = buf.at[1 - slot]
        zsem = sem.at[1 - slot]
        zsrc[0:TR_EXPERT, :] = jnp.zeros((TR_EXPERT, D_MODEL), BF16)
        for waiting in (False, True):
            def per_expert(e, carry):
                _copy_rows(fill_ref[N_EXPERTS + e], zsrc, None, xs_hbm, fill_ref[e], zsem,
                           to_hbm=True, wait=waiting)
                return carry

            def per_row_tile(k, carry):
                cp = _row_copy(zsrc, 0, xs_hbm, pl.multiple_of(k * TR_EXPERT, TR_EXPERT), TR_EXPERT, zsem, True)
                if waiting:
                    cp.wait()
                else:
                    cp.start()
                return carry

            lax.fori_loop(0, N_EXPERTS, per_expert, 0)
            lax.fori_loop(fill_ref[2 * N_EXPERTS], n_row_tiles, per_row_tile, 0)


def _dispatch(h2, rt, seg, loc, goff, fill, locrow, n_row_tiles):
    t = h2.shape[0]
    n_tiles = t // TM_MOE
    row = lambda i, *_: (i, 0)
    grid_spec = pltpu.PrefetchScalarGridSpec(
        num_scalar_prefetch=4,
        grid=(n_tiles,),
        in_specs=[
            pl.BlockSpec((TM_MOE, D_MODEL), row),
            pl.BlockSpec((TM_MOE, LANES), row),
            pl.BlockSpec((None, 1, LANES), lambda i, *_: (i, 0, 0)),
        ],
        out_specs=[
            pl.BlockSpec((TM_MOE, LANES), row),
            pl.BlockSpec(memory_space=pl.ANY),
        ],
        scratch_shapes=[
            pltpu.VMEM((2, R_LOC, D_MODEL), BF16),
            pltpu.SemaphoreType.DMA((2,)),
        ],
    )
    return pl.pallas_call(
        functools.partial(_dispatch_body, n_tiles=n_tiles, n_row_tiles=n_row_tiles),
        grid_spec=grid_spec,
        out_shape=[
            jax.ShapeDtypeStruct((t, LANES), F32),
            jax.ShapeDtypeStruct((n_row_tiles * TR_EXPERT, D_MODEL), BF16),
        ],
        compiler_params=pltpu.CompilerParams(
            dimension_semantics=("arbitrary",), vmem_limit_bytes=VMEM_LIMIT),
        name="moe_dispatch",
    )(seg, loc, goff, fill, h2, rt, locrow)


def _expert_body(te_ref, tw_ref, na_ref, xs_ref, w1a_ref, w3a_ref, w2a_ref, w1b_ref, w3b_ref, w2b_ref, y_ref):
    del tw_ref
    r = pl.program_id(0)
    n_active = na_ref[0]
    weights = ((w1a_ref, w3a_ref, w2a_ref), (w1b_ref, w3b_ref, w2b_ref))

    def swiglu(half, rows=None):
        w1_ref, w3_ref, w2_ref = weights[half]
        rows = slice(half * TR_EXPERT, (half + 1) * TR_EXPERT) if rows is None else rows
        x = xs_ref[rows, :]
        a = _silu(jnp.dot(x, w1_ref[...], preferred_element_type=F32))
        a = a * jnp.dot(x, w3_ref[...], preferred_element_type=F32)
        y_ref[rows, :] = jnp.dot(a.astype(BF16), w2_ref[...], preferred_element_type=F32).astype(BF16)

    first = EXPERT_TILES_PER_STEP * r
    both = first + 1 < n_active
    same = te_ref[first] == te_ref[first + 1]

    @pl.when(both & same)
    def _():
        swiglu(0, slice(0, EXPERT_TILES_PER_STEP * TR_EXPERT))

    @pl.when(both & jnp.logical_not(same))
    def _():
        swiglu(0)
        swiglu(1)

    @pl.when(first + 1 == n_active)
    def _():
        swiglu(0)
        y_ref[TR_EXPERT:, :] = jnp.zeros((TR_EXPERT, D_MODEL), BF16)

    @pl.when(first >= n_active)
    def _():
        y_ref[...] = jnp.zeros_like(y_ref)


def _experts(xs, tile_expert, tile_weights, n_active, w1b, w3b, w2b):
    n_rows = xs.shape[0]
    step_rows = EXPERT_TILES_PER_STEP * TR_EXPERT
    x_block = lambda r, te, tw, na: (jnp.minimum(r, (na[0] - 1) // EXPERT_TILES_PER_STEP), 0)
    w_specs = []
    for half in range(EXPERT_TILES_PER_STEP):
        w_block = lambda r, te, tw, na, half=half: (tw[EXPERT_TILES_PER_STEP * r + half], 0, 0)
        w_specs += [pl.BlockSpec((None, D_MODEL, D_EXPERT), w_block),
                    pl.BlockSpec((None, D_MODEL, D_EXPERT), w_block),
                    pl.BlockSpec((None, D_EXPERT, D_MODEL), w_block)]
    grid_spec = pltpu.PrefetchScalarGridSpec(
        num_scalar_prefetch=3,
        grid=(n_rows // step_rows,),
        in_specs=[pl.BlockSpec((step_rows, D_MODEL), x_block)] + w_specs,
        out_specs=pl.BlockSpec((step_rows, D_MODEL), lambda r, te, tw, na: (r, 0)),
    )
    return pl.pallas_call(
        _expert_body,
        grid_spec=grid_spec,
        out_shape=jax.ShapeDtypeStruct((n_rows, D_MODEL), BF16),
        compiler_params=pltpu.CompilerParams(
            dimension_semantics=("arbitrary",), vmem_limit_bytes=VMEM_LIMIT),
        name="moe_experts",
    )(tile_expert, tile_weights, n_active, xs, w1b, w3b, w2b, w1b, w3b, w2b)


def _combine_body(seg_ref, loc_ref, goff_ref, y_hbm, pos_ref, x_ref, gate_ref, fg_ref, o_ref, buf, sem, *,
                  n_tiles, final_norm):
    i = pl.program_id(0)
    slot = i % 2
    start = functools.partial(_segment_copies, seg_ref, loc_ref, goff_ref, hbm_ref=y_hbm, to_hbm=False)

    @pl.when(i == 0)
    def _():
        buf[...] = jnp.zeros_like(buf)
        start(0, buf.at[0], sem=sem.at[0])

    @pl.when(i + 1 < n_tiles)
    def _():
        start(i + 1, buf.at[1 - slot], sem=sem.at[1 - slot])

    _segment_wait(seg_ref, loc_ref, i, buf.at[slot], y_hbm, sem.at[slot], to_hbm=False)
    for part in range(TM_MOE // COMBINE_ROWS):
        rows = slice(part * COMBINE_ROWS, (part + 1) * COMBINE_ROWS)
        pp = pos_ref[rows, :]
        colp = lax.broadcasted_iota(jnp.int32, (COMBINE_ROWS, R_LOC), 1).astype(F32)
        wc = jnp.where(colp == pp[:, 0:1], pp[:, 2:3], jnp.where(colp == pp[:, 1:2], pp[:, 3:4], 0.0))
        moe = jnp.dot(wc.astype(BF16), buf[slot], preferred_element_type=F32)
        x2 = x_ref[rows, :] + gate_ref[...] * moe
        o_ref[rows, :] = _rms(x2) * fg_ref[...] if final_norm else x2


def _combine(y, pos, x1, mod3, final_g, seg, loc, goff, seq, final_norm):
    t = x1.shape[0]
    n_tiles = t // TM_MOE
    tiles_per_seq = seq // TM_MOE
    row = lambda i, *_: (i, 0)
    grid_spec = pltpu.PrefetchScalarGridSpec(
        num_scalar_prefetch=3,
        grid=(n_tiles,),
        in_specs=[
            pl.BlockSpec(memory_space=pl.ANY),
            pl.BlockSpec((TM_MOE, LANES), row),
            pl.BlockSpec((TM_MOE, D_MODEL), row),
            pl.BlockSpec((None, 1, D_MODEL), lambda i, *_: (i // tiles_per_seq, 0, 5)),
            pl.BlockSpec((1, D_MODEL), lambda i, *_: (0, 0)),
        ],
        out_specs=pl.BlockSpec((TM_MOE, D_MODEL), row),
        scratch_shapes=[
            pltpu.VMEM((2, R_LOC, D_MODEL), BF16),
            pltpu.SemaphoreType.DMA((2,)),
        ],
    )
    return pl.pallas_call(
        functools.partial(_combine_body, n_tiles=n_tiles, final_norm=final_norm),
        grid_spec=grid_spec,
        out_shape=jax.ShapeDtypeStruct((t, D_MODEL), F32),
        compiler_params=pltpu.CompilerParams(
            dimension_semantics=("arbitrary",), vmem_limit_bytes=VMEM_LIMIT),
        name="moe_combine",
    )(seg, loc, goff, y, pos, x1, mod3, final_g)


def _moe_plan(counts, n_tiles, n_row_tiles):
    cnt = counts.reshape(n_tiles, SUBLANES, LANES)[:, 0, :N_EXPERTS].astype(jnp.int32)
    seg = (cnt + SEG_ALIGN - 1) // SEG_ALIGN * SEG_ALIGN
    before_e = jnp.arange(N_EXPERTS)[:, None] < jnp.arange(N_EXPERTS)[None, :]
    before_t = jnp.arange(n_tiles)[:, None] > jnp.arange(n_tiles)[None, :]
    loc = jnp.sum(jnp.where(before_e[None], seg[:, :, None], 0), axis=1)
    tot = jnp.sum(seg, axis=0)
    region = (tot + TR_EXPERT - 1) // TR_EXPERT * TR_EXPERT
    gstart = jnp.sum(jnp.where(before_e, region[:, None], 0), axis=0)
    gend = gstart + region
    goff = gstart[None, :] + jnp.sum(jnp.where(before_t[:, :, None], seg[None], 0), axis=1)
    n_active = gend[-1] // TR_EXPERT
    tile_row0 = jnp.arange(n_row_tiles, dtype=jnp.int32) * TR_EXPERT
    last_row0 = (n_active - 1) * TR_EXPERT
    te = jnp.sum(gend[None, :] <= jnp.minimum(tile_row0, last_row0)[:, None], axis=1).astype(jnp.int32)
    te = jnp.minimum(te, N_EXPERTS - 1)
    te2 = te.reshape(-1, EXPERT_TILES_PER_STEP)
    cand = jnp.where(te2[:, 1] != te2[:, 0], te2[:, 1], 0)
    steps = jnp.arange(te2.shape[0])
    held = jnp.max(jnp.where(steps[None, :] <= steps[:, None], cand[None, :], 0), axis=1)
    tw = jnp.stack([te2[:, 0], held], axis=1).reshape(-1).astype(jnp.int32)
    locrow = _pad_cols(loc.astype(F32), LANES).reshape(n_tiles, 1, LANES)
    n_active = n_active.reshape(1).astype(jnp.int32)
    fill = jnp.concatenate([gstart + tot, region - tot, n_active]).astype(jnp.int32)
    return (seg.reshape(-1), loc.reshape(-1).astype(jnp.int32), goff.reshape(-1).astype(jnp.int32), fill,
            locrow, te, tw, n_active)


def _moe_final(x1, mod3, norm2_g, w_router, b_router, w1, w3, w2, final_g, seq, final_norm):
    t = x1.shape[0]
    n_tiles = t // TM_MOE
    max_rows = N_ASSIGN * t + (SEG_ALIGN - 1) * N_EXPERTS * n_tiles + N_EXPERTS * (TR_EXPERT - SEG_ALIGN)
    step_rows = EXPERT_TILES_PER_STEP * TR_EXPERT
    n_row_tiles = -(-max_rows // step_rows) * EXPERT_TILES_PER_STEP
    h2, rt, counts, w1b, w3b, w2b = _router(x1, mod3, norm2_g, w_router, b_router, w1, w3, w2, seq)
    seg, loc, goff, fill, locrow, te, tw, n_active = _moe_plan(counts, n_tiles, n_row_tiles)
    pos, xs = _dispatch(h2, rt, seg, loc, goff, fill, locrow, n_row_tiles)
    y = _experts(xs, te, tw, n_active, w1b, w3b, w2b)
    return _combine(y, pos, x1, mod3, final_g, seg, loc, goff, seq, final_norm)


def _pad_cols(a, width):
    return jnp.pad(a, ((0, 0), (0, width - a.shape[1])))


def kernel(x, c, w_ada, b_ada, norm1_g, w_in, b_in, gmlp_ln_g, gmlp_ln_b, gmlp_w_s, gmlp_b_s, gmlp_out_g, conv_w, conv_b, a_log_f, a_log_b, dt_bias_f, dt_bias_b, d_skip, ssd_norm_g, w_out, norm2_g, w_router_g, b_router_g, w_router_e, b_router_e, w1, w3, w2, final_g):
    bn, seq, _ = x.shape
    depth = w_ada.shape[0]
    x2 = x.reshape(bn * seq, D_MODEL)

    head_rows = jnp.arange(LANES)[:, None]
    head_cols = jnp.arange(SSD_WIDTH)[None, :] // SSD_HEAD_DIM
    expand = jnp.stack([head_rows == head_cols, head_rows == head_cols + SSD_HEADS]).astype(BF16)

    for l in range(depth):
        mod3 = _modulation(c, w_ada, b_ada[l], l).reshape(bn, 1, N_MOD * D_MODEL)

        w_in_p = _pad_cols(w_in[l].astype(BF16), IN_COLS_PAD)
        b_in_p = _pad_cols(b_in[l][None, :], IN_COLS_PAD)
        bs = jnp.repeat(gmlp_b_s[l].T, LANES, axis=1)
        bs_tile = jnp.tile(bs, (TM_PROJ // CHUNK, 1))
        ya, z, xbc, dt = _inproj_gmlp(
            x2, mod3, norm1_g[l][None, :], w_in_p, b_in_p, gmlp_ln_g[l][None, :], gmlp_ln_b[l][None, :],
            gmlp_w_s[l].astype(BF16), bs_tile, gmlp_out_g[l][None, :], seq)

        conv_w8 = jnp.pad(conv_w[l], ((0, SUBLANES - CONV_WIDTH), (0, 0)))
        dtb_row = _pad_cols(jnp.concatenate([dt_bias_f[l], dt_bias_b[l]])[None, :], LANES)
        alog_row = _pad_cols(jnp.concatenate([a_log_f[l], a_log_b[l]])[None, :], LANES)
        dsk_row = jnp.repeat(d_skip[l], SSD_HEAD_DIM)[None, :]
        x2 = _ssd_outproj(xbc, dt, z, ya, x2, mod3, conv_w8, conv_b[l][None, :], dtb_row, alog_row, dsk_row,
                          ssd_norm_g[l][None, :], w_out[l].astype(BF16), expand, bn, seq)

        w_re = jnp.transpose(w_router_e[l], (1, 0, 2)).reshape(D_MODEL, N_EXPERTS)
        w_router = _pad_cols(jnp.concatenate([w_re, w_router_g[l]], axis=1), LANES)
        w_router_hi = w_router.astype(BF16)
        w_router = jnp.concatenate([w_router_hi, (w_router - w_router_hi.astype(F32)).astype(BF16)], axis=1)
        b_router = _pad_cols(jnp.concatenate([b_router_e[l].reshape(-1), b_router_g[l]])[None, :], LANES)
        x2 = _moe_final(x2, mod3, norm2_g[l][None, :], w_router, b_router, w1[l], w3[l], w2[l],
                        final_g[None, :], seq, final_norm=(l == depth - 1))
    return x2.reshape(bn, seq, D_MODEL)
```

```python
import functools
import math

import jax
import jax.numpy as jnp
from jax import lax
from jax.experimental import pallas as pl
from jax.experimental.pallas import tpu as pltpu

F32 = jnp.float32
BF16 = jnp.bfloat16
HIGHEST = lax.Precision.HIGHEST

D_MODEL = 1024
N_MOD = 6
GMLP_WIDTH = 1024
GMLP_HEADS = 8
CHUNK = 128
SSD_STEP_ROWS = 4 * CHUNK
SSD_WIDTH = 1024
SSD_HEADS = 16
SSD_HEAD_DIM = 64
SSD_GROUPS = 2
SSD_STATE = 128
GROUP_WIDTH = SSD_WIDTH // SSD_GROUPS
CONV_WIDTH = 5
CONV_CH = SSD_WIDTH + 2 * SSD_GROUPS * SSD_STATE
N_EGROUPS = 4
EXPERTS_PER_GROUP = 8
N_EXPERTS = 32
D_EXPERT = 256
EPS = 1e-6
LOG2_E = 1.4426950408889634

LANES = 128
SUBLANES = 8
COL_U, COL_V, COL_Z, COL_XBC, COL_DT = 0, 1024, 2048, 3072, 4608
IN_COLS = 4640
IN_COLS_PAD = COL_DT + LANES
TM_PROJ = 1024
PROJ_ROWS = 512
COMBINE_ROWS = 256
ROUTER_ROWS = 256
TM_MOE = 512
TR_EXPERT = 512
EXPERT_TILES_PER_STEP = 2
N_ASSIGN = 2
SEG_ALIGN = 2 * SUBLANES
R_LOC = N_ASSIGN * TM_MOE + N_EXPERTS * SEG_ALIGN
PIECE_ROWS = 128
REM_SIZES = tuple(PIECE_ROWS >> s for s in range(1, PIECE_ROWS.bit_length()) if PIECE_ROWS >> s >= SEG_ALIGN)
TOTAL_SIZES = tuple(1 << s for s in range(R_LOC.bit_length() - 1, -1, -1) if 1 << s >= SEG_ALIGN)
VMEM_LIMIT = 56 * 1024 * 1024


def _silu(v):
    return v * jax.nn.sigmoid(v)


def _gelu(v):
    return 0.5 * v * (1.0 + lax.erf(v * math.sqrt(0.5)))


def _softplus(v):
    return jnp.maximum(v, 0.0) + jnp.log1p(jnp.exp(-jnp.abs(v)))


def _rms(v):
    return v * lax.rsqrt(jnp.mean(v * v, axis=-1, keepdims=True) + EPS)


def _mod_body(c_ref, w_ref, b_ref, o_ref):
    ca = _silu(c_ref[...])
    o_ref[...] = jnp.dot(ca, w_ref[...], precision=HIGHEST, preferred_element_type=F32) + b_ref[...]


def _modulation(c, w_ada, b_ada, layer):
    bn = c.shape[0]
    return pl.pallas_call(
        _mod_body,
        grid=(N_MOD,),
        in_specs=[
            pl.BlockSpec((bn, D_MODEL), lambda j: (0, 0)),
            pl.BlockSpec((None, D_MODEL, D_MODEL), lambda j: (layer, 0, j)),
            pl.BlockSpec((1, D_MODEL), lambda j: (0, j)),
        ],
        out_specs=pl.BlockSpec((bn, D_MODEL), lambda j: (0, j)),
        out_shape=jax.ShapeDtypeStruct((bn, N_MOD * D_MODEL), F32),
        name="adaln_mod",
    )(c, w_ada, b_ada.reshape(1, -1))


def _inproj_body(x_ref, shift_ref, scale_ref, g_ref, w_ref, b_ref, lng_ref, lnb_ref, ws_ref, bs_ref,
                 og_ref, ya_ref, z_ref, xbc_ref, dt_ref, mix_scr):
    gain = g_ref[...] * (1.0 + scale_ref[...])
    for part in range(TM_PROJ // PROJ_ROWS):
        rows = slice(part * PROJ_ROWS, (part + 1) * PROJ_ROWS)
        hb = (_rms(x_ref[rows, :]) * gain + shift_ref[...]).astype(BF16)

        def proj(lo, hi, hb=hb):
            return jnp.dot(hb, w_ref[:, lo:hi], preferred_element_type=F32) + b_ref[:, lo:hi]

        z_ref[rows, :] = proj(COL_Z, COL_XBC).astype(BF16)
        xbc_ref[rows, :] = proj(COL_XBC, COL_DT)
        dt_ref[rows, :] = proj(COL_DT, IN_COLS_PAD)

        v = _gelu(proj(COL_V, COL_Z))
        mu = jnp.mean(v, axis=-1, keepdims=True)
        vc = v - mu
        var = jnp.mean(vc * vc, axis=-1, keepdims=True)
        vn = (vc * lax.rsqrt(var + EPS) * lng_ref[...] + lnb_ref[...]).astype(BF16)
        n_chunks = PROJ_ROWS // CHUNK
        for hd in range(GMLP_HEADS):
            cols = slice(hd * LANES, (hd + 1) * LANES)
            rhs = jnp.concatenate([vn[c * CHUNK:(c + 1) * CHUNK, cols] for c in range(n_chunks)], axis=1)
            res = jnp.dot(ws_ref[hd], rhs, preferred_element_type=F32)
            for c in range(n_chunks):
                mix_scr[part * PROJ_ROWS + c * CHUNK:part * PROJ_ROWS + (c + 1) * CHUNK, cols] = (
                    res[:, c * LANES:(c + 1) * LANES])
        u = _gelu(proj(COL_U, COL_V))
        out = u * (mix_scr[rows, :] + bs_ref[rows, :])
        ya_ref[rows, :] = (_rms(out) * og_ref[...]).astype(BF16)


def _inproj_gmlp(x2, mod3, norm1_g, w_in_p, b_in_p, ln_g, ln_b, w_s, bs_tile, out_g, seq):
    t = x2.shape[0]
    tiles_per_seq = seq // TM_PROJ
    row = lambda i: (i, 0)
    const2 = lambda i: (0, 0)
    return pl.pallas_call(
        _inproj_body,
        grid=(t // TM_PROJ,),
        in_specs=[
            pl.BlockSpec((TM_PROJ, D_MODEL), row),
            pl.BlockSpec((None, 1, D_MODEL), lambda i: (i // tiles_per_seq, 0, 0)),
            pl.BlockSpec((None, 1, D_MODEL), lambda i: (i // tiles_per_seq, 0, 1)),
            pl.BlockSpec((1, D_MODEL), const2),
            pl.BlockSpec((D_MODEL, IN_COLS_PAD), const2, pipeline_mode=pl.Buffered(1)),
            pl.BlockSpec((1, IN_COLS_PAD), const2),
            pl.BlockSpec((1, GMLP_WIDTH), const2),
            pl.BlockSpec((1, GMLP_WIDTH), const2),
            pl.BlockSpec((GMLP_HEADS, CHUNK, CHUNK), lambda i: (0, 0, 0)),
            pl.BlockSpec((TM_PROJ, GMLP_WIDTH), const2, pipeline_mode=pl.Buffered(1)),
            pl.BlockSpec((1, GMLP_WIDTH), const2),
        ],
        out_specs=[
            pl.BlockSpec((TM_PROJ, GMLP_WIDTH), row),
            pl.BlockSpec((TM_PROJ, SSD_WIDTH), row),
            pl.BlockSpec((TM_PROJ, CONV_CH), row),
            pl.BlockSpec((TM_PROJ, LANES), row),
        ],
        out_shape=[
            jax.ShapeDtypeStruct((t, GMLP_WIDTH), BF16),
            jax.ShapeDtypeStruct((t, SSD_WIDTH), BF16),
            jax.ShapeDtypeStruct((t, CONV_CH), F32),
            jax.ShapeDtypeStruct((t, LANES), F32),
        ],
        scratch_shapes=[pltpu.VMEM((TM_PROJ, GMLP_WIDTH), F32)],
        compiler_params=pltpu.CompilerParams(
            dimension_semantics=("arbitrary",), vmem_limit_bytes=VMEM_LIMIT),
        name="inproj_gmlp",
    )(x2, mod3, mod3, norm1_g, w_in_p, b_in_p, ln_g, ln_b, w_s, bs_tile, out_g)


def _ssd_chunk(act, dtv, a_row, expand, state, rev):
    off = SSD_HEADS if rev else 0
    row = lax.broadcasted_iota(jnp.int32, (CHUNK, CHUNK), 0)
    col = lax.broadcasted_iota(jnp.int32, (CHUNK, CHUNK), 1)
    lower = row >= col
    upper = row <= col
    keep = upper if rev else lower
    da = dtv * a_row
    cs = jnp.dot(keep.astype(F32), da, precision=HIGHEST, preferred_element_type=F32)
    cs2 = cs * LOG2_E
    col_t = cs2.T - jnp.log2(dtv.T)
    tot = cs[0:1, :] if rev else cs[CHUNK - 1:CHUNK, :]

    xs = act[:, :SSD_WIDTH]
    lane = lax.broadcasted_iota(jnp.int32, (CHUNK, LANES), 1)
    first_half = lane < SSD_HEAD_DIM
    zero = jnp.zeros((), BF16)

    stack = jnp.concatenate(
        [jnp.exp(cs), dtv * jnp.exp(tot - cs), jnp.broadcast_to(jnp.exp(tot), (SUBLANES, LANES))], axis=0)
    stack_x = jnp.dot(stack.astype(BF16), expand, preferred_element_type=F32)
    into_x = stack_x[:CHUNK]
    w_x = stack_x[CHUNK:2 * CHUNK]
    cd_x = stack_x[2 * CHUNK:2 * CHUNK + 1]
    xw = xs * w_x.astype(BF16)

    pieces = []
    for g in range(SSD_GROUPS):
        bg = act[:, SSD_WIDTH + g * SSD_STATE:SSD_WIDTH + (g + 1) * SSD_STATE]
        cg = act[:, SSD_WIDTH + SSD_GROUPS * SSD_STATE + g * SSD_STATE:
                 SSD_WIDTH + SSD_GROUPS * SSD_STATE + (g + 1) * SSD_STATE]
        scores = lax.dot_general(cg, bg, (((1,), (1,)), ((), ())), preferred_element_type=F32)
        heads_per_group = SSD_HEADS // SSD_GROUPS
        for pair in range(heads_per_group // 2):
            h0 = g * heads_per_group + 2 * pair
            xs_pair = xs[:, h0 * SSD_HEAD_DIM:(h0 + 2) * SSD_HEAD_DIM]
            y_pair = None
            for k in range(2):
                hh = off + h0 + k
                seg = cs2[:, hh:hh + 1] - col_t[hh:hh + 1, :]
                dec = jnp.exp2(jnp.where(keep, seg, -jnp.inf))
                m = (scores * dec).astype(BF16)
                rhs = jnp.where(first_half if k == 0 else jnp.logical_not(first_half), xs_pair, zero)
                part = jnp.dot(m, rhs, preferred_element_type=F32)
                y_pair = part if y_pair is None else y_pair + part
            pieces.append(y_pair)
    y_diag = jnp.concatenate(pieces, axis=1)

    y_off = []
    new_state = []
    for g in range(SSD_GROUPS):
        gcols = slice(g * GROUP_WIDTH, (g + 1) * GROUP_WIDTH)
        bg = act[:, SSD_WIDTH + g * SSD_STATE:SSD_WIDTH + (g + 1) * SSD_STATE]
        cg = act[:, SSD_WIDTH + SSD_GROUPS * SSD_STATE + g * SSD_STATE:
                 SSD_WIDTH + SSD_GROUPS * SSD_STATE + (g + 1) * SSD_STATE]
        prev = state[:, gcols]
        y_off.append(jnp.dot(cg, prev.astype(BF16), preferred_element_type=F32))
        bg_t = bg.astype(F32).T.astype(BF16)
        new = jnp.dot(bg_t, xw[:, gcols], preferred_element_type=F32)
        new_state.append(prev * cd_x[:, gcols] + new)
    y = y_diag + jnp.concatenate(y_off, axis=1) * into_x
    return y, jnp.concatenate(new_state, axis=1)


def _ssd_body(xbc_ref, xprev_ref, xnext_ref, dt_ref, z_ref, ya_ref, x_ref, gate_ref, cw_ref, cb_ref,
              dtb_ref, alog_ref, dsk_ref, ng_ref, wout_ref, exp_ref, w1_ref, w3_ref, w2_ref,
              o_ref, w1b_ref, w3b_ref, w2b_ref,
              act_scr, dts_scr, yf_scr, st_scr, ext_scr, *, n_steps):
    d = pl.program_id(1)
    c = pl.program_id(2)
    w1b_ref[...] = w1_ref[...].astype(BF16)
    w3b_ref[...] = w3_ref[...].astype(BF16)
    w2b_ref[...] = w2_ref[...].astype(BF16)
    lane1 = lax.broadcasted_iota(jnp.int32, (1, LANES), 1)
    a_row = jnp.where(lane1 < 2 * SSD_HEADS, -jnp.exp(alog_ref[...]), 0.0)
    chunk_rows = [slice(j * CHUNK, (j + 1) * CHUNK) for j in range(SSD_STEP_ROWS // CHUNK)]

    @pl.when(c == 0)
    def _():
        st_scr[...] = jnp.zeros_like(st_scr)

    def scan(act, dtv, expand, rev):
        state = st_scr[...]
        ys = [None] * len(chunk_rows)
        for j in (reversed(range(len(chunk_rows))) if rev else range(len(chunk_rows))):
            ys[j], state = _ssd_chunk(act[chunk_rows[j], :], dtv[chunk_rows[j], :], a_row, expand, state, rev)
        st_scr[...] = state
        return jnp.concatenate(ys, axis=0)

    @pl.when(d == 0)
    def _forward():
        row0 = pl.multiple_of(c * SSD_STEP_ROWS, SSD_STEP_ROWS)
        ext_scr[0:SUBLANES, :] = jnp.where(c > 0, xprev_ref[...], 0.0)
        ext_scr[SUBLANES:SUBLANES + SSD_STEP_ROWS, :] = xbc_ref[...]
        ext_scr[SUBLANES + SSD_STEP_ROWS:, :] = jnp.where(c < n_steps - 1, xnext_ref[...], 0.0)
        ext = ext_scr[...]
        n_ext = SSD_STEP_ROWS + 2 * SUBLANES
        body = slice(SUBLANES, SUBLANES + SSD_STEP_ROWS)
        down1 = pltpu.roll(ext, 1, 0)
        down2 = pltpu.roll(down1, 1, 0)
        up1 = pltpu.roll(ext, n_ext - 1, 0)
        up2 = pltpu.roll(up1, n_ext - 1, 0)
        acc = cb_ref[...]
        for k, tap in enumerate((down2, down1, ext, up1, up2)):
            acc = acc + cw_ref[k:k + 1, :] * tap[body, :]
        act = _silu(acc).astype(BF16)
        dtv = _softplus(dt_ref[...] + dtb_ref[...])
        act_scr[pl.ds(row0, SSD_STEP_ROWS), :] = act
        dts_scr[pl.ds(row0, SSD_STEP_ROWS), :] = dtv
        yf_scr[pl.ds(row0, SSD_STEP_ROWS), :] = scan(act, dtv, exp_ref[0], rev=False).astype(BF16)

    @pl.when(d == 1)
    def _backward():
        row0 = pl.multiple_of((n_steps - 1 - c) * SSD_STEP_ROWS, SSD_STEP_ROWS)
        act = act_scr[pl.ds(row0, SSD_STEP_ROWS), :]
        dtv = dts_scr[pl.ds(row0, SSD_STEP_ROWS), :]
        yb = scan(act, dtv, exp_ref[1], rev=True)
        xs = act[:, :SSD_WIDTH].astype(F32)
        y = yf_scr[pl.ds(row0, SSD_STEP_ROWS), :].astype(F32) + yb + dsk_ref[...] * xs
        y = y * _silu(z_ref[...].astype(F32))
        y = jnp.concatenate(
            [_rms(y[:, g * GROUP_WIDTH:(g + 1) * GROUP_WIDTH]) for g in range(SSD_GROUPS)], axis=1)
        y = y * ng_ref[...]
        mix = jnp.concatenate([ya_ref[...], y.astype(BF16)], axis=1)
        o = jnp.dot(mix, wout_ref[...], preferred_element_type=F32)
        o_ref[...] = x_ref[...] + gate_ref[...] * o


def _ssd_outproj(xbc, dt, z, ya, x2, mod3, conv_w8, conv_b, dtb_row, alog_row, dsk_row, norm_g, w_out_b,
                 expand, w1, w3, w2, bn, seq):
    t = x2.shape[0]
    nc = seq // SSD_STEP_ROWS
    blocks8 = SSD_STEP_ROWS // SUBLANES
    last8 = t // SUBLANES - 1

    def fwd_chunk(b, d, c):
        return b * nc + c + d * (nc - 1 - c)

    def bwd_chunk(b, d, c):
        return b * nc + nc - 1 - d * c

    const2 = lambda b, d, c: (0, 0)
    n_grid = bn * 2 * nc
    up_rows = N_EXPERTS * D_MODEL // n_grid
    down_rows = N_EXPERTS * D_EXPERT // n_grid
    assert up_rows * n_grid == N_EXPERTS * D_MODEL and up_rows % SEG_ALIGN == 0
    assert down_rows * n_grid == N_EXPERTS * D_EXPERT and down_rows % SEG_ALIGN == 0
    share = lambda b, d, c: ((b * 2 + d) * nc + c, 0)
    outs = pl.pallas_call(
        functools.partial(_ssd_body, n_steps=nc),
        grid=(bn, 2, nc),
        in_specs=[
            pl.BlockSpec((SSD_STEP_ROWS, CONV_CH), lambda b, d, c: (fwd_chunk(b, d, c), 0)),
            pl.BlockSpec((SUBLANES, CONV_CH),
                         lambda b, d, c: (jnp.maximum(fwd_chunk(b, d, c) * blocks8 - 1, 0), 0)),
            pl.BlockSpec((SUBLANES, CONV_CH),
                         lambda b, d, c: (jnp.minimum((fwd_chunk(b, d, c) + 1) * blocks8, last8), 0)),
            pl.BlockSpec((SSD_STEP_ROWS, LANES), lambda b, d, c: (fwd_chunk(b, d, c), 0)),
            pl.BlockSpec((SSD_STEP_ROWS, SSD_WIDTH), lambda b, d, c: (bwd_chunk(b, d, c), 0)),
            pl.BlockSpec((SSD_STEP_ROWS, GMLP_WIDTH), lambda b, d, c: (bwd_chunk(b, d, c), 0)),
            pl.BlockSpec((SSD_STEP_ROWS, D_MODEL), lambda b, d, c: (bwd_chunk(b, d, c), 0)),
            pl.BlockSpec((None, 1, D_MODEL), lambda b, d, c: (b, 0, 2)),
            pl.BlockSpec((SUBLANES, CONV_CH), const2),
            pl.BlockSpec((1, CONV_CH), const2),
            pl.BlockSpec((1, LANES), const2),
            pl.BlockSpec((1, LANES), const2),
            pl.BlockSpec((1, SSD_WIDTH), const2),
            pl.BlockSpec((1, SSD_WIDTH), const2),
            pl.BlockSpec((GMLP_WIDTH + SSD_WIDTH, D_MODEL), const2, pipeline_mode=pl.Buffered(1)),
            pl.BlockSpec((2, LANES, SSD_WIDTH), lambda b, d, c: (0, 0, 0), pipeline_mode=pl.Buffered(1)),
            pl.BlockSpec((up_rows, D_EXPERT), share),
            pl.BlockSpec((up_rows, D_EXPERT), share),
            pl.BlockSpec((down_rows, D_MODEL), share),
        ],
        out_specs=[
            pl.BlockSpec((SSD_STEP_ROWS, D_MODEL), lambda b, d, c: (bwd_chunk(b, d, c), 0)),
            pl.BlockSpec((up_rows, D_EXPERT), share),
            pl.BlockSpec((up_rows, D_EXPERT), share),
            pl.BlockSpec((down_rows, D_MODEL), share),
        ],
        out_shape=[
            jax.ShapeDtypeStruct((t, D_MODEL), F32),
            jax.ShapeDtypeStruct((N_EXPERTS * D_MODEL, D_EXPERT), BF16),
            jax.ShapeDtypeStruct((N_EXPERTS * D_MODEL, D_EXPERT), BF16),
            jax.ShapeDtypeStruct((N_EXPERTS * D_EXPERT, D_MODEL), BF16),
        ],
        scratch_shapes=[
            pltpu.VMEM((seq, CONV_CH), BF16),
            pltpu.VMEM((seq, LANES), F32),
            pltpu.VMEM((seq, SSD_WIDTH), BF16),
            pltpu.VMEM((SSD_STATE, SSD_WIDTH), F32),
            pltpu.VMEM((SSD_STEP_ROWS + 2 * SUBLANES, CONV_CH), F32),
        ],
        compiler_params=pltpu.CompilerParams(
            dimension_semantics=("arbitrary", "arbitrary", "arbitrary"), vmem_limit_bytes=VMEM_LIMIT),
        name="ssd_outproj",
    )(xbc, xbc, xbc, dt, z, ya, x2, mod3, conv_w8, conv_b, dtb_row, alog_row, dsk_row, norm_g, w_out_b,
      expand, w1.reshape(-1, D_EXPERT), w3.reshape(-1, D_EXPERT), w2.reshape(-1, D_MODEL))
    x1, w1b, w3b, w2b = outs
    return (x1, w1b.reshape(N_EXPERTS, D_MODEL, D_EXPERT), w3b.reshape(N_EXPERTS, D_MODEL, D_EXPERT),
            w2b.reshape(N_EXPERTS, D_EXPERT, D_MODEL))


def _route(logits):
    lane = lax.broadcasted_iota(jnp.int32, logits.shape, 1)
    big = jnp.int32(LANES)
    neg = -jnp.inf
    gmask = (lane >= N_EXPERTS) & (lane < N_EXPERTS + N_EGROUPS)
    gl = jnp.where(gmask, logits, neg)
    gmax = jnp.max(gl, axis=-1, keepdims=True)
    gidx = jnp.min(jnp.where(gl == gmax, lane, big), axis=-1, keepdims=True) - N_EXPERTS
    p_g = 1.0 / jnp.sum(jnp.where(gmask, jnp.exp(gl - gmax), 0.0), axis=-1, keepdims=True)
    lo = gidx * EXPERTS_PER_GROUP
    emask = (lane >= lo) & (lane < lo + EXPERTS_PER_GROUP)
    el = jnp.where(emask, logits, neg)
    v1 = jnp.max(el, axis=-1, keepdims=True)
    i1 = jnp.min(jnp.where(el == v1, lane, big), axis=-1, keepdims=True)
    el2 = jnp.where(lane == i1, neg, el)
    v2 = jnp.max(el2, axis=-1, keepdims=True)
    i2 = jnp.min(jnp.where(el2 == v2, lane, big), axis=-1, keepdims=True)
    e2 = jnp.exp(v2 - v1)
    den = 1.0 + e2
    w1 = p_g / den
    w2 = p_g * e2 / den
    return i1, i2, w1, w2


def _router_body(x_ref, shift_ref, scale_ref, g2_ref, wr_ref, br_ref, h_ref, rt_ref, cnt_ref):
    counts = jnp.zeros((1, LANES), F32)
    gain = g2_ref[...] * (1.0 + scale_ref[...])
    for part in range(TM_MOE // ROUTER_ROWS):
        rows = slice(part * ROUTER_ROWS, (part + 1) * ROUTER_ROWS)
        h = _rms(x_ref[rows, :]) * gain + shift_ref[...]
        hb = h.astype(BF16)
        h_ref[rows, :] = hb
        h_lo = (h - hb.astype(F32)).astype(BF16)
        both = jnp.dot(hb, wr_ref[...], preferred_element_type=F32)
        logits = (both[:, :LANES] + both[:, LANES:]
                  + jnp.dot(h_lo, wr_ref[:, :LANES], preferred_element_type=F32)) + br_ref[...]
        i1, i2, w1, w2 = _route(logits)
        lane = lax.broadcasted_iota(jnp.int32, (ROUTER_ROWS, LANES), 1)
        rt_ref[rows, :] = jnp.where(lane == 0, i1.astype(F32), jnp.where(lane == 1, i2.astype(F32),
                                    jnp.where(lane == 2, w1, jnp.where(lane == 3, w2, 0.0))))
        mask = jnp.where(lane == i1, 1.0, jnp.where(lane == i2, 1.0, 0.0))
        counts = counts + jnp.sum(mask, axis=0, keepdims=True)
    cnt_ref[...] = jnp.broadcast_to(counts, (SUBLANES, LANES))


def _router(x1, mod3, norm2_g, w_router, b_router, seq):
    t = x1.shape[0]
    tiles_per_seq = seq // TM_MOE
    n_tiles = t // TM_MOE
    row = lambda i: (i, 0)
    const2 = lambda i: (0, 0)
    return pl.pallas_call(
        _router_body,
        grid=(n_tiles,),
        in_specs=[
            pl.BlockSpec((TM_MOE, D_MODEL), row),
            pl.BlockSpec((None, 1, D_MODEL), lambda i: (i // tiles_per_seq, 0, 3)),
            pl.BlockSpec((None, 1, D_MODEL), lambda i: (i // tiles_per_seq, 0, 4)),
            pl.BlockSpec((1, D_MODEL), const2),
            pl.BlockSpec((D_MODEL, 2 * LANES), const2),
            pl.BlockSpec((1, LANES), const2),
        ],
        out_specs=[
            pl.BlockSpec((TM_MOE, D_MODEL), row),
            pl.BlockSpec((TM_MOE, LANES), row),
            pl.BlockSpec((SUBLANES, LANES), row),
        ],
        out_shape=[
            jax.ShapeDtypeStruct((t, D_MODEL), BF16),
            jax.ShapeDtypeStruct((t, LANES), F32),
            jax.ShapeDtypeStruct((n_tiles * SUBLANES, LANES), F32),
        ],
        compiler_params=pltpu.CompilerParams(
            dimension_semantics=("arbitrary",), vmem_limit_bytes=VMEM_LIMIT),
        name="moe_router",
    )(x1, mod3, mod3, norm2_g, w_router, b_router)


def _row_copy(vmem_buf, v_start, hbm_ref, g_start, size, sem, to_hbm):
    v = vmem_buf.at[pl.ds(v_start, size), :]
    g = hbm_ref.at[pl.ds(g_start, size), :]
    return pltpu.make_async_copy(v, g, sem) if to_hbm else pltpu.make_async_copy(g, v, sem)


def _copy_rows(n, vmem_buf, v0, hbm_ref, g0, sem, *, to_hbm, wait):
    def run(v_off, g_off, size):
        v_start = 0 if v0 is None else pl.multiple_of(v0 + v_off, SEG_ALIGN)
        cp = _row_copy(vmem_buf, v_start, hbm_ref, pl.multiple_of(g0 + g_off, SEG_ALIGN), size, sem, to_hbm)
        if wait:
            cp.wait()
        else:
            cp.start()

    n_big = n // PIECE_ROWS

    def big_piece(k, carry):
        run(k * PIECE_ROWS, k * PIECE_ROWS, PIECE_ROWS)
        return carry

    lax.fori_loop(0, n_big, big_piece, 0)
    base = n_big * PIECE_ROWS
    rem = n - base
    for size in REM_SIZES:
        done = base + (rem // (2 * size)) * (2 * size)

        @pl.when((rem & size) != 0)
        def _():
            run(done, done, size)


def _segment_copies(seg_ref, loc_ref, goff_ref, tile, vmem_buf, hbm_ref, sem, *, to_hbm):
    base = tile * N_EXPERTS

    def per_expert(e, carry):
        _copy_rows(seg_ref[base + e], vmem_buf, loc_ref[base + e], hbm_ref, goff_ref[base + e], sem,
                   to_hbm=to_hbm, wait=False)
        return carry

    lax.fori_loop(0, N_EXPERTS, per_expert, 0)


def _tile_rows(seg_ref, loc_ref, tile):
    last = tile * N_EXPERTS + N_EXPERTS - 1
    return loc_ref[last] + seg_ref[last]


def _segment_wait(seg_ref, loc_ref, tile, vmem_buf, hbm_ref, sem, *, to_hbm):
    total = _tile_rows(seg_ref, loc_ref, tile)
    for size in TOTAL_SIZES:
        @pl.when((total & size) != 0)
        def _():
            _row_copy(vmem_buf, 0, hbm_ref, 0, size, sem, to_hbm).wait()


def _dispatch_body(seg_ref, loc_ref, goff_ref, fill_ref, h_ref, rt_ref, locrow_ref, pos_ref, xs_hbm, buf, sem,
                   *, n_tiles, n_row_tiles):
    i = pl.program_id(0)
    slot = i % 2

    start = functools.partial(_segment_copies, seg_ref, loc_ref, goff_ref, hbm_ref=xs_hbm, to_hbm=True)
    wait = functools.partial(_segment_wait, seg_ref, loc_ref, hbm_ref=xs_hbm, to_hbm=True)

    @pl.when(i >= 2)
    def _():
        wait(i - 2, buf.at[slot], sem=sem.at[slot])

    rt = rt_ref[...]
    lane = lax.broadcasted_iota(jnp.int32, (TM_MOE, LANES), 1)
    lanef = lane.astype(F32)
    e1 = rt[:, 0:1]
    e2 = rt[:, 1:2]
    mask = jnp.where(lanef == e1, 1.0, jnp.where(lanef == e2, 1.0, 0.0))
    r = lax.broadcasted_iota(jnp.int32, (TM_MOE, TM_MOE), 0)
    cc = lax.broadcasted_iota(jnp.int32, (TM_MOE, TM_MOE), 1)
    strict = jnp.where(r > cc, 1.0, 0.0).astype(BF16)
    rank = jnp.dot(strict, mask.astype(BF16), preferred_element_type=F32)
    posall = rank + locrow_ref[...]
    pos1 = jnp.sum(jnp.where(lanef == e1, posall, 0.0), axis=-1, keepdims=True)
    pos2 = jnp.sum(jnp.where(lanef == e2, posall, 0.0), axis=-1, keepdims=True)
    pp = jnp.where(lane == 0, pos1, jnp.where(lane == 1, pos2, rt))
    pos_ref[...] = pp

    slots_t = [pp[k * LANES:(k + 1) * LANES, :].T for k in range(TM_MOE // LANES)]

    rowp = lax.broadcasted_iota(jnp.int32, (R_LOC, LANES), 0).astype(F32)
    perm = jnp.concatenate(
        [jnp.where(rowp == st[0:1, :], 1.0, jnp.where(rowp == st[1:2, :], 1.0, 0.0)).astype(BF16)
         for st in slots_t], axis=1)
    buf[slot] = jnp.dot(perm, h_ref[...], preferred_element_type=F32).astype(BF16)
    start(i, buf.at[slot], sem=sem.at[slot])

    @pl.when(i == n_tiles - 1)
    def _():
        if n_tiles > 1:
            wait(i - 1, buf.at[1 - slot], sem=sem.at[1 - slot])
        wait(i, buf.at[slot], sem=sem.at[slot])
        zsrc = buf.at[1 - slot]
        zsem = sem.at[1 - slot]
        zsrc[0:TR_EXPERT, :] = jnp.zeros((TR_EXPERT, D_MODEL), BF16)
        for waiting in (False, True):
            def per_expert(e, carry):
                _copy_rows(fill_ref[N_EXPERTS + e], zsrc, None, xs_hbm, fill_ref[e], zsem,
                           to_hbm=True, wait=waiting)
                return carry

            def per_row_tile(k, carry):
                cp = _row_copy(zsrc, 0, xs_hbm, pl.multiple_of(k * TR_EXPERT, TR_EXPERT), TR_EXPERT, zsem, True)
                if waiting:
                    cp.wait()
                else:
                    cp.start()
                return carry

            lax.fori_loop(0, N_EXPERTS, per_expert, 0)
            lax.fori_loop(fill_ref[2 * N_EXPERTS], n_row_tiles, per_row_tile, 0)


def _dispatch(h2, rt, seg, loc, goff, fill, locrow, n_row_tiles):
    t = h2.shape[0]
    n_tiles = t // TM_MOE
    row = lambda i, *_: (i, 0)
    grid_spec = pltpu.PrefetchScalarGridSpec(
        num_scalar_prefetch=4,
        grid=(n_tiles,),
        in_specs=[
            pl.BlockSpec((TM_MOE, D_MODEL), row),
            pl.BlockSpec((TM_MOE, LANES), row),
            pl.BlockSpec((None, 1, LANES), lambda i, *_: (i, 0, 0)),
        ],
        out_specs=[
            pl.BlockSpec((TM_MOE, LANES), row),
            pl.BlockSpec(memory_space=pl.ANY),
        ],
        scratch_shapes=[
            pltpu.VMEM((2, R_LOC, D_MODEL), BF16),
            pltpu.SemaphoreType.DMA((2,)),
        ],
    )
    return pl.pallas_call(
        functools.partial(_dispatch_body, n_tiles=n_tiles, n_row_tiles=n_row_tiles),
        grid_spec=grid_spec,
        out_shape=[
            jax.ShapeDtypeStruct((t, LANES), F32),
            jax.ShapeDtypeStruct((n_row_tiles * TR_EXPERT, D_MODEL), BF16),
        ],
        compiler_params=pltpu.CompilerParams(
            dimension_semantics=("arbitrary",), vmem_limit_bytes=VMEM_LIMIT),
        name="moe_dispatch",
    )(seg, loc, goff, fill, h2, rt, locrow)


def _expert_body(te_ref, tw_ref, na_ref, xs_ref, w1a_ref, w3a_ref, w2a_ref, w1b_ref, w3b_ref, w2b_ref, y_ref):
    del tw_ref
    r = pl.program_id(0)
    n_active = na_ref[0]
    weights = ((w1a_ref, w3a_ref, w2a_ref), (w1b_ref, w3b_ref, w2b_ref))

    def swiglu(half, rows=None):
        w1_ref, w3_ref, w2_ref = weights[half]
        rows = slice(half * TR_EXPERT, (half + 1) * TR_EXPERT) if rows is None else rows
        x = xs_ref[rows, :]
        a = _silu(jnp.dot(x, w1_ref[...], preferred_element_type=F32))
        a = a * jnp.dot(x, w3_ref[...], preferred_element_type=F32)
        y_ref[rows, :] = jnp.dot(a.astype(BF16), w2_ref[...], preferred_element_type=F32).astype(BF16)

    first = EXPERT_TILES_PER_STEP * r
    both = first + 1 < n_active
    same = te_ref[first] == te_ref[first + 1]

    @pl.when(both & same)
    def _():
        swiglu(0, slice(0, EXPERT_TILES_PER_STEP * TR_EXPERT))

    @pl.when(both & jnp.logical_not(same))
    def _():
        swiglu(0)
        swiglu(1)

    @pl.when(first + 1 == n_active)
    def _():
        swiglu(0)
        y_ref[TR_EXPERT:, :] = jnp.zeros((TR_EXPERT, D_MODEL), BF16)

    @pl.when(first >= n_active)
    def _():
        y_ref[...] = jnp.zeros_like(y_ref)


def _experts(xs, tile_expert, tile_weights, n_active, w1b, w3b, w2b):
    n_rows = xs.shape[0]
    step_rows = EXPERT_TILES_PER_STEP * TR_EXPERT
    x_block = lambda r, te, tw, na: (jnp.minimum(r, (na[0] - 1) // EXPERT_TILES_PER_STEP), 0)
    w_specs = []
    for half in range(EXPERT_TILES_PER_STEP):
        w_block = lambda r, te, tw, na, half=half: (tw[EXPERT_TILES_PER_STEP * r + half], 0, 0)
        w_specs += [pl.BlockSpec((None, D_MODEL, D_EXPERT), w_block),
                    pl.BlockSpec((None, D_MODEL, D_EXPERT), w_block),
                    pl.BlockSpec((None, D_EXPERT, D_MODEL), w_block)]
    grid_spec = pltpu.PrefetchScalarGridSpec(
        num_scalar_prefetch=3,
        grid=(n_rows // step_rows,),
        in_specs=[pl.BlockSpec((step_rows, D_MODEL), x_block)] + w_specs,
        out_specs=pl.BlockSpec((step_rows, D_MODEL), lambda r, te, tw, na: (r, 0)),
    )
    return pl.pallas_call(
        _expert_body,
        grid_spec=grid_spec,
        out_shape=jax.ShapeDtypeStruct((n_rows, D_MODEL), BF16),
        compiler_params=pltpu.CompilerParams(
            dimension_semantics=("arbitrary",), vmem_limit_bytes=VMEM_LIMIT),
        name="moe_experts",
    )(tile_expert, tile_weights, n_active, xs, w1b, w3b, w2b, w1b, w3b, w2b)


def _combine_body(seg_ref, loc_ref, goff_ref, y_hbm, pos_ref, x_ref, gate_ref, fg_ref, o_ref, buf, sem, *,
                  n_tiles, final_norm):
    i = pl.program_id(0)
    slot = i % 2
    start = functools.partial(_segment_copies, seg_ref, loc_ref, goff_ref, hbm_ref=y_hbm, to_hbm=False)

    @pl.when(i == 0)
    def _():
        buf[...] = jnp.zeros_like(buf)
        start(0, buf.at[0], sem=sem.at[0])

    @pl.when(i + 1 < n_tiles)
    def _():
        start(i + 1, buf.at[1 - slot], sem=sem.at[1 - slot])

    _segment_wait(seg_ref, loc_ref, i, buf.at[slot], y_hbm, sem.at[slot], to_hbm=False)
    for part in range(TM_MOE // COMBINE_ROWS):
        rows = slice(part * COMBINE_ROWS, (part + 1) * COMBINE_ROWS)
        pp = pos_ref[rows, :]
        colp = lax.broadcasted_iota(jnp.int32, (COMBINE_ROWS, R_LOC), 1).astype(F32)
        wc = jnp.where(colp == pp[:, 0:1], pp[:, 2:3], jnp.where(colp == pp[:, 1:2], pp[:, 3:4], 0.0))
        moe = jnp.dot(wc.astype(BF16), buf[slot], preferred_element_type=F32)
        x2 = x_ref[rows, :] + gate_ref[...] * moe
        o_ref[rows, :] = _rms(x2) * fg_ref[...] if final_norm else x2


def _combine(y, pos, x1, mod3, final_g, seg, loc, goff, seq, final_norm):
    t = x1.shape[0]
    n_tiles = t // TM_MOE
    tiles_per_seq = seq // TM_MOE
    row = lambda i, *_: (i, 0)
    grid_spec = pltpu.PrefetchScalarGridSpec(
        num_scalar_prefetch=3,
        grid=(n_tiles,),
        in_specs=[
            pl.BlockSpec(memory_space=pl.ANY),
            pl.BlockSpec((TM_MOE, LANES), row),
            pl.BlockSpec((TM_MOE, D_MODEL), row),
            pl.BlockSpec((None, 1, D_MODEL), lambda i, *_: (i // tiles_per_seq, 0, 5)),
            pl.BlockSpec((1, D_MODEL), lambda i, *_: (0, 0)),
        ],
        out_specs=pl.BlockSpec((TM_MOE, D_MODEL), row),
        scratch_shapes=[
            pltpu.VMEM((2, R_LOC, D_MODEL), BF16),
            pltpu.SemaphoreType.DMA((2,)),
        ],
    )
    return pl.pallas_call(
        functools.partial(_combine_body, n_tiles=n_tiles, final_norm=final_norm),
        grid_spec=grid_spec,
        out_shape=jax.ShapeDtypeStruct((t, D_MODEL), F32),
        compiler_params=pltpu.CompilerParams(
            dimension_semantics=("arbitrary",), vmem_limit_bytes=VMEM_LIMIT),
        name="moe_combine",
    )(seg, loc, goff, y, pos, x1, mod3, final_g)


def _moe_plan(counts, n_tiles, n_row_tiles):
    cnt = counts.reshape(n_tiles, SUBLANES, LANES)[:, 0, :N_EXPERTS].astype(jnp.int32)
    seg = (cnt + SEG_ALIGN - 1) // SEG_ALIGN * SEG_ALIGN
    before_e = jnp.arange(N_EXPERTS)[:, None] < jnp.arange(N_EXPERTS)[None, :]
    before_t = jnp.arange(n_tiles)[:, None] > jnp.arange(n_tiles)[None, :]
    loc = jnp.sum(jnp.where(before_e[None], seg[:, :, None], 0), axis=1)
    tot = jnp.sum(seg, axis=0)
    region = (tot + TR_EXPERT - 1) // TR_EXPERT * TR_EXPERT
    gstart = jnp.sum(jnp.where(before_e, region[:, None], 0), axis=0)
    gend = gstart + region
    goff = gstart[None, :] + jnp.sum(jnp.where(before_t[:, :, None], seg[None], 0), axis=1)
    n_active = gend[-1] // TR_EXPERT
    tile_row0 = jnp.arange(n_row_tiles, dtype=jnp.int32) * TR_EXPERT
    last_row0 = (n_active - 1) * TR_EXPERT
    te = jnp.sum(gend[None, :] <= jnp.minimum(tile_row0, last_row0)[:, None], axis=1).astype(jnp.int32)
    te = jnp.minimum(te, N_EXPERTS - 1)
    te2 = te.reshape(-1, EXPERT_TILES_PER_STEP)
    cand = jnp.where(te2[:, 1] != te2[:, 0], te2[:, 1], 0)
    steps = jnp.arange(te2.shape[0])
    held = jnp.max(jnp.where(steps[None, :] <= steps[:, None], cand[None, :], 0), axis=1)
    tw = jnp.stack([te2[:, 0], held], axis=1).reshape(-1).astype(jnp.int32)
    locrow = _pad_cols(loc.astype(F32), LANES).reshape(n_tiles, 1, LANES)
    n_active = n_active.reshape(1).astype(jnp.int32)
    fill = jnp.concatenate([gstart + tot, region - tot, n_active]).astype(jnp.int32)
    return (seg.reshape(-1), loc.reshape(-1).astype(jnp.int32), goff.reshape(-1).astype(jnp.int32), fill,
            locrow, te, tw, n_active)


def _moe_final(x1, mod3, norm2_g, w_router, b_router, w1b, w3b, w2b, final_g, seq, final_norm):
    t = x1.shape[0]
    n_tiles = t // TM_MOE
    max_rows = N_ASSIGN * t + (SEG_ALIGN - 1) * N_EXPERTS * n_tiles + N_EXPERTS * (TR_EXPERT - SEG_ALIGN)
    step_rows = EXPERT_TILES_PER_STEP * TR_EXPERT
    n_row_tiles = -(-max_rows // step_rows) * EXPERT_TILES_PER_STEP
    h2, rt, counts = _router(x1, mod3, norm2_g, w_router, b_router, seq)
    seg, loc, goff, fill, locrow, te, tw, n_active = _moe_plan(counts, n_tiles, n_row_tiles)
    pos, xs = _dispatch(h2, rt, seg, loc, goff, fill, locrow, n_row_tiles)
    y = _experts(xs, te, tw, n_active, w1b, w3b, w2b)
    return _combine(y, pos, x1, mod3, final_g, seg, loc, goff, seq, final_norm)


def _pad_cols(a, width):
    return jnp.pad(a, ((0, 0), (0, width - a.shape[1])))


def kernel(x, c, w_ada, b_ada, norm1_g, w_in, b_in, gmlp_ln_g, gmlp_ln_b, gmlp_w_s, gmlp_b_s, gmlp_out_g, conv_w, conv_b, a_log_f, a_log_b, dt_bias_f, dt_bias_b, d_skip, ssd_norm_g, w_out, norm2_g, w_router_g, b_router_g, w_router_e, b_router_e, w1, w3, w2, final_g):
    bn, seq, _ = x.shape
    depth = w_ada.shape[0]
    x2 = x.reshape(bn * seq, D_MODEL)

    head_rows = jnp.arange(LANES)[:, None]
    head_cols = jnp.arange(SSD_WIDTH)[None, :] // SSD_HEAD_DIM
    expand = jnp.stack([head_rows == head_cols, head_rows == head_cols + SSD_HEADS]).astype(BF16)

    for l in range(depth):
        mod3 = _modulation(c, w_ada, b_ada[l], l).reshape(bn, 1, N_MOD * D_MODEL)

        w_in_p = _pad_cols(w_in[l].astype(BF16), IN_COLS_PAD)
        b_in_p = _pad_cols(b_in[l][None, :], IN_COLS_PAD)
        bs = jnp.repeat(gmlp_b_s[l].T, LANES, axis=1)
        bs_tile = jnp.tile(bs, (TM_PROJ // CHUNK, 1))
        ya, z, xbc, dt = _inproj_gmlp(
            x2, mod3, norm1_g[l][None, :], w_in_p, b_in_p, gmlp_ln_g[l][None, :], gmlp_ln_b[l][None, :],
            gmlp_w_s[l].astype(BF16), bs_tile, gmlp_out_g[l][None, :], seq)

        conv_w8 = jnp.pad(conv_w[l], ((0, SUBLANES - CONV_WIDTH), (0, 0)))
        dtb_row = _pad_cols(jnp.concatenate([dt_bias_f[l], dt_bias_b[l]])[None, :], LANES)
        alog_row = _pad_cols(jnp.concatenate([a_log_f[l], a_log_b[l]])[None, :], LANES)
        dsk_row = jnp.repeat(d_skip[l], SSD_HEAD_DIM)[None, :]
        x2, w1b, w3b, w2b = _ssd_outproj(
            xbc, dt, z, ya, x2, mod3, conv_w8, conv_b[l][None, :], dtb_row, alog_row, dsk_row,
            ssd_norm_g[l][None, :], w_out[l].astype(BF16), expand, w1[l], w3[l], w2[l], bn, seq)

        w_re = jnp.transpose(w_router_e[l], (1, 0, 2)).reshape(D_MODEL, N_EXPERTS)
        w_router = _pad_cols(jnp.concatenate([w_re, w_router_g[l]], axis=1), LANES)
        w_router_hi = w_router.astype(BF16)
        w_router = jnp.concatenate([w_router_hi, (w_router - w_router_hi.astype(F32)).astype(BF16)], axis=1)
        b_router = _pad_cols(jnp.concatenate([b_router_e[l].reshape(-1), b_router_g[l]])[None, :], LANES)
        x2 = _moe_final(x2, mod3, norm2_g[l][None, :], w_router, b_router, w1b, w3b, w2b,
                        final_g[None, :], seq, final_norm=(l == depth - 1))
    return x2.reshape(bn, seq, D_MODEL)
```

```python
import functools
import math

import jax
import jax.numpy as jnp
from jax import lax
from jax.experimental import pallas as pl
from jax.experimental.pallas import tpu as pltpu

F32 = jnp.float32
BF16 = jnp.bfloat16
HIGHEST = lax.Precision.HIGHEST

D_MODEL = 1024
N_MOD = 6
GMLP_WIDTH = 1024
GMLP_HEADS = 8
CHUNK = 128
SSD_STEP_ROWS = 4 * CHUNK
SSD_WIDTH = 1024
SSD_HEADS = 16
SSD_HEAD_DIM = 64
SSD_GROUPS = 2
SSD_STATE = 128
GROUP_WIDTH = SSD_WIDTH // SSD_GROUPS
CONV_WIDTH = 5
CONV_CH = SSD_WIDTH + 2 * SSD_GROUPS * SSD_STATE
N_EGROUPS = 4
EXPERTS_PER_GROUP = 8
N_EXPERTS = 32
D_EXPERT = 256
EPS = 1e-6
LOG2_E = 1.4426950408889634

LANES = 128
SUBLANES = 8
COL_U, COL_V, COL_Z, COL_XBC, COL_DT = 0, 1024, 2048, 3072, 4608
IN_COLS = 4640
IN_COLS_PAD = COL_DT + LANES
TM_PROJ = 1024
PROJ_ROWS = 512
COMBINE_ROWS = 256
ROUTER_ROWS = 256
TM_ROUTER = 1024
TM_MOE = 512
TR_EXPERT = 512
EXPERT_TILES_PER_STEP = 2
N_ASSIGN = 2
SEG_ALIGN = 2 * SUBLANES
R_LOC = N_ASSIGN * TM_MOE + N_EXPERTS * SEG_ALIGN
PIECE_ROWS = 128
REM_SIZES = tuple(PIECE_ROWS >> s for s in range(1, PIECE_ROWS.bit_length()) if PIECE_ROWS >> s >= SEG_ALIGN)
TOTAL_SIZES = tuple(1 << s for s in range(R_LOC.bit_length() - 1, -1, -1) if 1 << s >= SEG_ALIGN)
VMEM_LIMIT = 56 * 1024 * 1024


def _silu(v):
    return v * jax.nn.sigmoid(v)


def _gelu(v):
    return 0.5 * v * (1.0 + lax.erf(v * math.sqrt(0.5)))


def _softplus(v):
    return jnp.maximum(v, 0.0) + jnp.log1p(jnp.exp(-jnp.abs(v)))


def _rms(v):
    return v * lax.rsqrt(jnp.mean(v * v, axis=-1, keepdims=True) + EPS)


def _mod_body(c_ref, w_ref, b_ref, o_ref):
    ca = _silu(c_ref[...])
    o_ref[...] = jnp.dot(ca, w_ref[...], precision=HIGHEST, preferred_element_type=F32) + b_ref[...]


def _modulation(c, w_ada, b_ada, layer):
    bn = c.shape[0]
    return pl.pallas_call(
        _mod_body,
        grid=(N_MOD,),
        in_specs=[
            pl.BlockSpec((bn, D_MODEL), lambda j: (0, 0)),
            pl.BlockSpec((None, D_MODEL, D_MODEL), lambda j: (layer, 0, j)),
            pl.BlockSpec((1, D_MODEL), lambda j: (0, j)),
        ],
        out_specs=pl.BlockSpec((bn, D_MODEL), lambda j: (0, j)),
        out_shape=jax.ShapeDtypeStruct((bn, N_MOD * D_MODEL), F32),
        name="adaln_mod",
    )(c, w_ada, b_ada.reshape(1, -1))


def _inproj_body(x_ref, shift_ref, scale_ref, g_ref, w_ref, b_ref, lng_ref, lnb_ref, ws_ref, bs_ref,
                 og_ref, ya_ref, z_ref, xbc_ref, dt_ref, mix_scr):
    gain = g_ref[...] * (1.0 + scale_ref[...])
    for part in range(TM_PROJ // PROJ_ROWS):
        rows = slice(part * PROJ_ROWS, (part + 1) * PROJ_ROWS)
        hb = (_rms(x_ref[rows, :]) * gain + shift_ref[...]).astype(BF16)

        def proj(lo, hi, hb=hb):
            return jnp.dot(hb, w_ref[:, lo:hi], preferred_element_type=F32) + b_ref[:, lo:hi]

        z_ref[rows, :] = proj(COL_Z, COL_XBC).astype(BF16)
        xbc_ref[rows, :] = proj(COL_XBC, COL_DT)
        dt_ref[rows, :] = proj(COL_DT, IN_COLS_PAD)

        v = _gelu(proj(COL_V, COL_Z))
        mu = jnp.mean(v, axis=-1, keepdims=True)
        vc = v - mu
        var = jnp.mean(vc * vc, axis=-1, keepdims=True)
        vn = (vc * lax.rsqrt(var + EPS) * lng_ref[...] + lnb_ref[...]).astype(BF16)
        n_chunks = PROJ_ROWS // CHUNK
        for hd in range(GMLP_HEADS):
            cols = slice(hd * LANES, (hd + 1) * LANES)
            rhs = jnp.concatenate([vn[c * CHUNK:(c + 1) * CHUNK, cols] for c in range(n_chunks)], axis=1)
            res = jnp.dot(ws_ref[hd], rhs, preferred_element_type=F32)
            for c in range(n_chunks):
                mix_scr[part * PROJ_ROWS + c * CHUNK:part * PROJ_ROWS + (c + 1) * CHUNK, cols] = (
                    res[:, c * LANES:(c + 1) * LANES])
        u = _gelu(proj(COL_U, COL_V))
        out = u * (mix_scr[rows, :] + bs_ref[rows, :])
        ya_ref[rows, :] = (_rms(out) * og_ref[...]).astype(BF16)


def _inproj_gmlp(x2, mod3, norm1_g, w_in_p, b_in_p, ln_g, ln_b, w_s, bs_tile, out_g, seq):
    t = x2.shape[0]
    tiles_per_seq = seq // TM_PROJ
    row = lambda i: (i, 0)
    const2 = lambda i: (0, 0)
    return pl.pallas_call(
        _inproj_body,
        grid=(t // TM_PROJ,),
        in_specs=[
            pl.BlockSpec((TM_PROJ, D_MODEL), row),
            pl.BlockSpec((None, 1, D_MODEL), lambda i: (i // tiles_per_seq, 0, 0)),
            pl.BlockSpec((None, 1, D_MODEL), lambda i: (i // tiles_per_seq, 0, 1)),
            pl.BlockSpec((1, D_MODEL), const2),
            pl.BlockSpec((D_MODEL, IN_COLS_PAD), const2, pipeline_mode=pl.Buffered(1)),
            pl.BlockSpec((1, IN_COLS_PAD), const2),
            pl.BlockSpec((1, GMLP_WIDTH), const2),
            pl.BlockSpec((1, GMLP_WIDTH), const2),
            pl.BlockSpec((GMLP_HEADS, CHUNK, CHUNK), lambda i: (0, 0, 0)),
            pl.BlockSpec((TM_PROJ, GMLP_WIDTH), const2, pipeline_mode=pl.Buffered(1)),
            pl.BlockSpec((1, GMLP_WIDTH), const2),
        ],
        out_specs=[
            pl.BlockSpec((TM_PROJ, GMLP_WIDTH), row),
            pl.BlockSpec((TM_PROJ, SSD_WIDTH), row),
            pl.BlockSpec((TM_PROJ, CONV_CH), row),
            pl.BlockSpec((TM_PROJ, LANES), row),
        ],
        out_shape=[
            jax.ShapeDtypeStruct((t, GMLP_WIDTH), BF16),
            jax.ShapeDtypeStruct((t, SSD_WIDTH), BF16),
            jax.ShapeDtypeStruct((t, CONV_CH), F32),
            jax.ShapeDtypeStruct((t, LANES), F32),
        ],
        scratch_shapes=[pltpu.VMEM((TM_PROJ, GMLP_WIDTH), F32)],
        compiler_params=pltpu.CompilerParams(
            dimension_semantics=("arbitrary",), vmem_limit_bytes=VMEM_LIMIT),
        name="inproj_gmlp",
    )(x2, mod3, mod3, norm1_g, w_in_p, b_in_p, ln_g, ln_b, w_s, bs_tile, out_g)


def _ssd_chunk(act, dtv, a_row, expand, state, rev):
    off = SSD_HEADS if rev else 0
    row = lax.broadcasted_iota(jnp.int32, (CHUNK, CHUNK), 0)
    col = lax.broadcasted_iota(jnp.int32, (CHUNK, CHUNK), 1)
    lower = row >= col
    upper = row <= col
    keep = upper if rev else lower
    da = dtv * a_row
    cs = jnp.dot(keep.astype(F32), da, precision=HIGHEST, preferred_element_type=F32)
    cs2 = cs * LOG2_E
    col_t = cs2.T - jnp.log2(dtv.T)
    tot = cs[0:1, :] if rev else cs[CHUNK - 1:CHUNK, :]

    xs = act[:, :SSD_WIDTH]
    lane = lax.broadcasted_iota(jnp.int32, (CHUNK, LANES), 1)
    first_half = lane < SSD_HEAD_DIM
    zero = jnp.zeros((), BF16)

    stack = jnp.concatenate(
        [jnp.exp(cs), dtv * jnp.exp(tot - cs), jnp.broadcast_to(jnp.exp(tot), (SUBLANES, LANES))], axis=0)
    stack_x = jnp.dot(stack.astype(BF16), expand, preferred_element_type=F32)
    into_x = stack_x[:CHUNK]
    w_x = stack_x[CHUNK:2 * CHUNK]
    cd_x = stack_x[2 * CHUNK:2 * CHUNK + 1]
    xw = xs * w_x.astype(BF16)

    pieces = []
    for g in range(SSD_GROUPS):
        bg = act[:, SSD_WIDTH + g * SSD_STATE:SSD_WIDTH + (g + 1) * SSD_STATE]
        cg = act[:, SSD_WIDTH + SSD_GROUPS * SSD_STATE + g * SSD_STATE:
                 SSD_WIDTH + SSD_GROUPS * SSD_STATE + (g + 1) * SSD_STATE]
        scores = lax.dot_general(cg, bg, (((1,), (1,)), ((), ())), preferred_element_type=F32)
        heads_per_group = SSD_HEADS // SSD_GROUPS
        for pair in range(heads_per_group // 2):
            h0 = g * heads_per_group + 2 * pair
            xs_pair = xs[:, h0 * SSD_HEAD_DIM:(h0 + 2) * SSD_HEAD_DIM]
            y_pair = None
            for k in range(2):
                hh = off + h0 + k
                seg = cs2[:, hh:hh + 1] - col_t[hh:hh + 1, :]
                dec = jnp.exp2(jnp.where(keep, seg, -jnp.inf))
                m = (scores * dec).astype(BF16)
                rhs = jnp.where(first_half if k == 0 else jnp.logical_not(first_half), xs_pair, zero)
                part = jnp.dot(m, rhs, preferred_element_type=F32)
                y_pair = part if y_pair is None else y_pair + part
            pieces.append(y_pair)
    y_diag = jnp.concatenate(pieces, axis=1)

    y_off = []
    new_state = []
    for g in range(SSD_GROUPS):
        gcols = slice(g * GROUP_WIDTH, (g + 1) * GROUP_WIDTH)
        bg = act[:, SSD_WIDTH + g * SSD_STATE:SSD_WIDTH + (g + 1) * SSD_STATE]
        cg = act[:, SSD_WIDTH + SSD_GROUPS * SSD_STATE + g * SSD_STATE:
                 SSD_WIDTH + SSD_GROUPS * SSD_STATE + (g + 1) * SSD_STATE]
        prev = state[:, gcols]
        y_off.append(jnp.dot(cg, prev.astype(BF16), preferred_element_type=F32))
        bg_t = bg.astype(F32).T.astype(BF16)
        new = jnp.dot(bg_t, xw[:, gcols], preferred_element_type=F32)
        new_state.append(prev * cd_x[:, gcols] + new)
    y = y_diag + jnp.concatenate(y_off, axis=1) * into_x
    return y, jnp.concatenate(new_state, axis=1)


def _ssd_body(xbc_ref, xprev_ref, xnext_ref, dt_ref, z_ref, ya_ref, x_ref, gate_ref, cw_ref, cb_ref,
              dtb_ref, alog_ref, dsk_ref, ng_ref, wout_ref, exp_ref, w1_ref, w3_ref, w2_ref,
              o_ref, w1b_ref, w3b_ref, w2b_ref,
              act_scr, dts_scr, yf_scr, st_scr, ext_scr, *, n_steps):
    d = pl.program_id(1)
    c = pl.program_id(2)
    w1b_ref[...] = w1_ref[...].astype(BF16)
    w3b_ref[...] = w3_ref[...].astype(BF16)
    w2b_ref[...] = w2_ref[...].astype(BF16)
    lane1 = lax.broadcasted_iota(jnp.int32, (1, LANES), 1)
    a_row = jnp.where(lane1 < 2 * SSD_HEADS, -jnp.exp(alog_ref[...]), 0.0)
    chunk_rows = [slice(j * CHUNK, (j + 1) * CHUNK) for j in range(SSD_STEP_ROWS // CHUNK)]

    @pl.when(c == 0)
    def _():
        st_scr[...] = jnp.zeros_like(st_scr)

    def scan(act, dtv, expand, rev):
        state = st_scr[...]
        ys = [None] * len(chunk_rows)
        for j in (reversed(range(len(chunk_rows))) if rev else range(len(chunk_rows))):
            ys[j], state = _ssd_chunk(act[chunk_rows[j], :], dtv[chunk_rows[j], :], a_row, expand, state, rev)
        st_scr[...] = state
        return jnp.concatenate(ys, axis=0)

    @pl.when(d == 0)
    def _forward():
        row0 = pl.multiple_of(c * SSD_STEP_ROWS, SSD_STEP_ROWS)
        ext_scr[0:SUBLANES, :] = jnp.where(c > 0, xprev_ref[...], 0.0)
        ext_scr[SUBLANES:SUBLANES + SSD_STEP_ROWS, :] = xbc_ref[...]
        ext_scr[SUBLANES + SSD_STEP_ROWS:, :] = jnp.where(c < n_steps - 1, xnext_ref[...], 0.0)
        ext = ext_scr[...]
        n_ext = SSD_STEP_ROWS + 2 * SUBLANES
        body = slice(SUBLANES, SUBLANES + SSD_STEP_ROWS)
        down1 = pltpu.roll(ext, 1, 0)
        down2 = pltpu.roll(down1, 1, 0)
        up1 = pltpu.roll(ext, n_ext - 1, 0)
        up2 = pltpu.roll(up1, n_ext - 1, 0)
        acc = cb_ref[...]
        for k, tap in enumerate((down2, down1, ext, up1, up2)):
            acc = acc + cw_ref[k:k + 1, :] * tap[body, :]
        act = _silu(acc).astype(BF16)
        dtv = _softplus(dt_ref[...] + dtb_ref[...])
        act_scr[pl.ds(row0, SSD_STEP_ROWS), :] = act
        dts_scr[pl.ds(row0, SSD_STEP_ROWS), :] = dtv
        yf_scr[pl.ds(row0, SSD_STEP_ROWS), :] = scan(act, dtv, exp_ref[0], rev=False).astype(BF16)

    @pl.when(d == 1)
    def _backward():
        row0 = pl.multiple_of((n_steps - 1 - c) * SSD_STEP_ROWS, SSD_STEP_ROWS)
        act = act_scr[pl.ds(row0, SSD_STEP_ROWS), :]
        dtv = dts_scr[pl.ds(row0, SSD_STEP_ROWS), :]
        yb = scan(act, dtv, exp_ref[1], rev=True)
        xs = act[:, :SSD_WIDTH].astype(F32)
        y = yf_scr[pl.ds(row0, SSD_STEP_ROWS), :].astype(F32) + yb + dsk_ref[...] * xs
        y = y * _silu(z_ref[...].astype(F32))
        y = jnp.concatenate(
            [_rms(y[:, g * GROUP_WIDTH:(g + 1) * GROUP_WIDTH]) for g in range(SSD_GROUPS)], axis=1)
        y = y * ng_ref[...]
        mix = jnp.concatenate([ya_ref[...], y.astype(BF16)], axis=1)
        o = jnp.dot(mix, wout_ref[...], preferred_element_type=F32)
        o_ref[...] = x_ref[...] + gate_ref[...] * o


def _ssd_outproj(xbc, dt, z, ya, x2, mod3, conv_w8, conv_b, dtb_row, alog_row, dsk_row, norm_g, w_out_b,
                 expand, w1, w3, w2, bn, seq):
    t = x2.shape[0]
    nc = seq // SSD_STEP_ROWS
    blocks8 = SSD_STEP_ROWS // SUBLANES
    last8 = t // SUBLANES - 1

    def fwd_chunk(b, d, c):
        return b * nc + c + d * (nc - 1 - c)

    def bwd_chunk(b, d, c):
        return b * nc + nc - 1 - d * c

    const2 = lambda b, d, c: (0, 0)
    n_grid = bn * 2 * nc
    up_rows = N_EXPERTS * D_MODEL // n_grid
    down_rows = N_EXPERTS * D_EXPERT // n_grid
    assert up_rows * n_grid == N_EXPERTS * D_MODEL and up_rows % SEG_ALIGN == 0
    assert down_rows * n_grid == N_EXPERTS * D_EXPERT and down_rows % SEG_ALIGN == 0
    share = lambda b, d, c: ((b * 2 + d) * nc + c, 0)
    outs = pl.pallas_call(
        functools.partial(_ssd_body, n_steps=nc),
        grid=(bn, 2, nc),
        in_specs=[
            pl.BlockSpec((SSD_STEP_ROWS, CONV_CH), lambda b, d, c: (fwd_chunk(b, d, c), 0)),
            pl.BlockSpec((SUBLANES, CONV_CH),
                         lambda b, d, c: (jnp.maximum(fwd_chunk(b, d, c) * blocks8 - 1, 0), 0)),
            pl.BlockSpec((SUBLANES, CONV_CH),
                         lambda b, d, c: (jnp.minimum((fwd_chunk(b, d, c) + 1) * blocks8, last8), 0)),
            pl.BlockSpec((SSD_STEP_ROWS, LANES), lambda b, d, c: (fwd_chunk(b, d, c), 0)),
            pl.BlockSpec((SSD_STEP_ROWS, SSD_WIDTH), lambda b, d, c: (bwd_chunk(b, d, c), 0)),
            pl.BlockSpec((SSD_STEP_ROWS, GMLP_WIDTH), lambda b, d, c: (bwd_chunk(b, d, c), 0)),
            pl.BlockSpec((SSD_STEP_ROWS, D_MODEL), lambda b, d, c: (bwd_chunk(b, d, c), 0)),
            pl.BlockSpec((None, 1, D_MODEL), lambda b, d, c: (b, 0, 2)),
            pl.BlockSpec((SUBLANES, CONV_CH), const2),
            pl.BlockSpec((1, CONV_CH), const2),
            pl.BlockSpec((1, LANES), const2),
            pl.BlockSpec((1, LANES), const2),
            pl.BlockSpec((1, SSD_WIDTH), const2),
            pl.BlockSpec((1, SSD_WIDTH), const2),
            pl.BlockSpec((GMLP_WIDTH + SSD_WIDTH, D_MODEL), const2, pipeline_mode=pl.Buffered(1)),
            pl.BlockSpec((2, LANES, SSD_WIDTH), lambda b, d, c: (0, 0, 0), pipeline_mode=pl.Buffered(1)),
            pl.BlockSpec((up_rows, D_EXPERT), share),
            pl.BlockSpec((up_rows, D_EXPERT), share),
            pl.BlockSpec((down_rows, D_MODEL), share),
        ],
        out_specs=[
            pl.BlockSpec((SSD_STEP_ROWS, D_MODEL), lambda b, d, c: (bwd_chunk(b, d, c), 0)),
            pl.BlockSpec((up_rows, D_EXPERT), share),
            pl.BlockSpec((up_rows, D_EXPERT), share),
            pl.BlockSpec((down_rows, D_MODEL), share),
        ],
        out_shape=[
            jax.ShapeDtypeStruct((t, D_MODEL), F32),
            jax.ShapeDtypeStruct((N_EXPERTS * D_MODEL, D_EXPERT), BF16),
            jax.ShapeDtypeStruct((N_EXPERTS * D_MODEL, D_EXPERT), BF16),
            jax.ShapeDtypeStruct((N_EXPERTS * D_EXPERT, D_MODEL), BF16),
        ],
        scratch_shapes=[
            pltpu.VMEM((seq, CONV_CH), BF16),
            pltpu.VMEM((seq, LANES), F32),
            pltpu.VMEM((seq, SSD_WIDTH), BF16),
            pltpu.VMEM((SSD_STATE, SSD_WIDTH), F32),
            pltpu.VMEM((SSD_STEP_ROWS + 2 * SUBLANES, CONV_CH), F32),
        ],
        compiler_params=pltpu.CompilerParams(
            dimension_semantics=("arbitrary", "arbitrary", "arbitrary"), vmem_limit_bytes=VMEM_LIMIT),
        name="ssd_outproj",
    )(xbc, xbc, xbc, dt, z, ya, x2, mod3, conv_w8, conv_b, dtb_row, alog_row, dsk_row, norm_g, w_out_b,
      expand, w1.reshape(-1, D_EXPERT), w3.reshape(-1, D_EXPERT), w2.reshape(-1, D_MODEL))
    x1, w1b, w3b, w2b = outs
    return (x1, w1b.reshape(N_EXPERTS, D_MODEL, D_EXPERT), w3b.reshape(N_EXPERTS, D_MODEL, D_EXPERT),
            w2b.reshape(N_EXPERTS, D_EXPERT, D_MODEL))


def _route(logits):
    lane = lax.broadcasted_iota(jnp.int32, logits.shape, 1)
    big = jnp.int32(LANES)
    neg = -jnp.inf
    gmask = (lane >= N_EXPERTS) & (lane < N_EXPERTS + N_EGROUPS)
    gl = jnp.where(gmask, logits, neg)
    gmax = jnp.max(gl, axis=-1, keepdims=True)
    gidx = jnp.min(jnp.where(gl == gmax, lane, big), axis=-1, keepdims=True) - N_EXPERTS
    p_g = 1.0 / jnp.sum(jnp.where(gmask, jnp.exp(gl - gmax), 0.0), axis=-1, keepdims=True)
    lo = gidx * EXPERTS_PER_GROUP
    emask = (lane >= lo) & (lane < lo + EXPERTS_PER_GROUP)
    el = jnp.where(emask, logits, neg)
    v1 = jnp.max(el, axis=-1, keepdims=True)
    i1 = jnp.min(jnp.where(el == v1, lane, big), axis=-1, keepdims=True)
    el2 = jnp.where(lane == i1, neg, el)
    v2 = jnp.max(el2, axis=-1, keepdims=True)
    i2 = jnp.min(jnp.where(el2 == v2, lane, big), axis=-1, keepdims=True)
    e2 = jnp.exp(v2 - v1)
    den = 1.0 + e2
    w1 = p_g / den
    w2 = p_g * e2 / den
    return i1, i2, w1, w2


def _router_body(x_ref, shift_ref, scale_ref, g2_ref, wr_ref, br_ref, h_ref, rt_ref, cnt_ref):
    counts = jnp.zeros((1, LANES), F32)
    gain = g2_ref[...] * (1.0 + scale_ref[...])
    groups_per_count = TM_MOE // ROUTER_ROWS
    for part in range(TM_ROUTER // ROUTER_ROWS):
        rows = slice(part * ROUTER_ROWS, (part + 1) * ROUTER_ROWS)
        h = _rms(x_ref[rows, :]) * gain + shift_ref[...]
        hb = h.astype(BF16)
        h_ref[rows, :] = hb
        h_lo = (h - hb.astype(F32)).astype(BF16)
        both = jnp.dot(hb, wr_ref[...], preferred_element_type=F32)
        logits = (both[:, :LANES] + both[:, LANES:]
                  + jnp.dot(h_lo, wr_ref[:, :LANES], preferred_element_type=F32)) + br_ref[...]
        i1, i2, w1, w2 = _route(logits)
        lane = lax.broadcasted_iota(jnp.int32, (ROUTER_ROWS, LANES), 1)
        rt_ref[rows, :] = jnp.where(lane == 0, i1.astype(F32), jnp.where(lane == 1, i2.astype(F32),
                                    jnp.where(lane == 2, w1, jnp.where(lane == 3, w2, 0.0))))
        mask = jnp.where(lane == i1, 1.0, jnp.where(lane == i2, 1.0, 0.0))
        counts = counts + jnp.sum(mask, axis=0, keepdims=True)
        if (part + 1) % groups_per_count == 0:
            tile = part // groups_per_count
            cnt_ref[tile * SUBLANES:(tile + 1) * SUBLANES, :] = jnp.broadcast_to(counts, (SUBLANES, LANES))
            counts = jnp.zeros((1, LANES), F32)


def _router(x1, mod3, norm2_g, w_router, b_router, seq):
    t = x1.shape[0]
    tiles_per_seq = seq // TM_ROUTER
    n_tiles = t // TM_MOE
    count_rows = TM_ROUTER // TM_MOE * SUBLANES
    row = lambda i: (i, 0)
    const2 = lambda i: (0, 0)
    return pl.pallas_call(
        _router_body,
        grid=(t // TM_ROUTER,),
        in_specs=[
            pl.BlockSpec((TM_ROUTER, D_MODEL), row),
            pl.BlockSpec((None, 1, D_MODEL), lambda i: (i // tiles_per_seq, 0, 3)),
            pl.BlockSpec((None, 1, D_MODEL), lambda i: (i // tiles_per_seq, 0, 4)),
            pl.BlockSpec((1, D_MODEL), const2),
            pl.BlockSpec((D_MODEL, 2 * LANES), const2),
            pl.BlockSpec((1, LANES), const2),
        ],
        out_specs=[
            pl.BlockSpec((TM_ROUTER, D_MODEL), row),
            pl.BlockSpec((TM_ROUTER, LANES), row),
            pl.BlockSpec((count_rows, LANES), row),
        ],
        out_shape=[
            jax.ShapeDtypeStruct((t, D_MODEL), BF16),
            jax.ShapeDtypeStruct((t, LANES), F32),
            jax.ShapeDtypeStruct((n_tiles * SUBLANES, LANES), F32),
        ],
        compiler_params=pltpu.CompilerParams(
            dimension_semantics=("arbitrary",), vmem_limit_bytes=VMEM_LIMIT),
        name="moe_router",
    )(x1, mod3, mod3, norm2_g, w_router, b_router)


def _row_copy(vmem_buf, v_start, hbm_ref, g_start, size, sem, to_hbm):
    v = vmem_buf.at[pl.ds(v_start, size), :]
    g = hbm_ref.at[pl.ds(g_start, size), :]
    return pltpu.make_async_copy(v, g, sem) if to_hbm else pltpu.make_async_copy(g, v, sem)


def _copy_rows(n, vmem_buf, v0, hbm_ref, g0, sem, *, to_hbm, wait):
    def run(v_off, g_off, size):
        v_start = 0 if v0 is None else pl.multiple_of(v0 + v_off, SEG_ALIGN)
        cp = _row_copy(vmem_buf, v_start, hbm_ref, pl.multiple_of(g0 + g_off, SEG_ALIGN), size, sem, to_hbm)
        if wait:
            cp.wait()
        else:
            cp.start()

    n_big = n // PIECE_ROWS

    def big_piece(k, carry):
        run(k * PIECE_ROWS, k * PIECE_ROWS, PIECE_ROWS)
        return carry

    lax.fori_loop(0, n_big, big_piece, 0)
    base = n_big * PIECE_ROWS
    rem = n - base
    for size in REM_SIZES:
        done = base + (rem // (2 * size)) * (2 * size)

        @pl.when((rem & size) != 0)
        def _():
            run(done, done, size)


def _segment_copies(seg_ref, loc_ref, goff_ref, tile, vmem_buf, hbm_ref, sem, *, to_hbm):
    base = tile * N_EXPERTS

    def per_expert(e, carry):
        _copy_rows(seg_ref[base + e], vmem_buf, loc_ref[base + e], hbm_ref, goff_ref[base + e], sem,
                   to_hbm=to_hbm, wait=False)
        return carry

    lax.fori_loop(0, N_EXPERTS, per_expert, 0)


def _tile_rows(seg_ref, loc_ref, tile):
    last = tile * N_EXPERTS + N_EXPERTS - 1
    return loc_ref[last] + seg_ref[last]


def _segment_wait(seg_ref, loc_ref, tile, vmem_buf, hbm_ref, sem, *, to_hbm):
    total = _tile_rows(seg_ref, loc_ref, tile)
    for size in TOTAL_SIZES:
        @pl.when((total & size) != 0)
        def _():
            _row_copy(vmem_buf, 0, hbm_ref, 0, size, sem, to_hbm).wait()


def _dispatch_body(seg_ref, loc_ref, goff_ref, fill_ref, h_ref, rt_ref, locrow_ref, pos_ref, xs_hbm, buf, sem,
                   *, n_tiles, n_row_tiles):
    i = pl.program_id(0)
    slot = i % 2

    start = functools.partial(_segment_copies, seg_ref, loc_ref, goff_ref, hbm_ref=xs_hbm, to_hbm=True)
    wait = functools.partial(_segment_wait, seg_ref, loc_ref, hbm_ref=xs_hbm, to_hbm=True)

    @pl.when(i >= 2)
    def _():
        wait(i - 2, buf.at[slot], sem=sem.at[slot])

    rt = rt_ref[...]
    lane = lax.broadcasted_iota(jnp.int32, (TM_MOE, LANES), 1)
    lanef = lane.astype(F32)
    e1 = rt[:, 0:1]
    e2 = rt[:, 1:2]
    mask = jnp.where(lanef == e1, 1.0, jnp.where(lanef == e2, 1.0, 0.0))
    r = lax.broadcasted_iota(jnp.int32, (TM_MOE, TM_MOE), 0)
    cc = lax.broadcasted_iota(jnp.int32, (TM_MOE, TM_MOE), 1)
    strict = jnp.where(r > cc, 1.0, 0.0).astype(BF16)
    rank = jnp.dot(strict, mask.astype(BF16), preferred_element_type=F32)
    posall = rank + locrow_ref[...]
    pos1 = jnp.sum(jnp.where(lanef == e1, posall, 0.0), axis=-1, keepdims=True)
    pos2 = jnp.sum(jnp.where(lanef == e2, posall, 0.0), axis=-1, keepdims=True)
    pp = jnp.where(lane == 0, pos1, jnp.where(lane == 1, pos2, rt))
    pos_ref[...] = pp

    slots_t = [pp[k * LANES:(k + 1) * LANES, :].T for k in range(TM_MOE // LANES)]

    rowp = lax.broadcasted_iota(jnp.int32, (R_LOC, LANES), 0).astype(F32)
    perm = jnp.concatenate(
        [jnp.where(rowp == st[0:1, :], 1.0, jnp.where(rowp == st[1:2, :], 1.0, 0.0)).astype(BF16)
         for st in slots_t], axis=1)
    buf[slot] = jnp.dot(perm, h_ref[...], preferred_element_type=F32).astype(BF16)
    start(i, buf.at[slot], sem=sem.at[slot])

    @pl.when(i == n_tiles - 1)
    def _():
        if n_tiles > 1:
            wait(i - 1, buf.at[1 - slot], sem=sem.at[1 - slot])
        wait(i, buf.at[slot], sem=sem.at[slot])
        zsrc = buf.at[1 - slot]
        zsem = sem.at[1 - slot]
        zsrc[0:TR_EXPERT, :] = jnp.zeros((TR_EXPERT, D_MODEL), BF16)
        for waiting in (False, True):
            def per_expert(e, carry):
                _copy_rows(fill_ref[N_EXPERTS + e], zsrc, None, xs_hbm, fill_ref[e], zsem,
                           to_hbm=True, wait=waiting)
                return carry

            def per_row_tile(k, carry):
                cp = _row_copy(zsrc, 0, xs_hbm, pl.multiple_of(k * TR_EXPERT, TR_EXPERT), TR_EXPERT, zsem, True)
                if waiting:
                    cp.wait()
                else:
                    cp.start()
                return carry

            lax.fori_loop(0, N_EXPERTS, per_expert, 0)
            lax.fori_loop(fill_ref[2 * N_EXPERTS], n_row_tiles, per_row_tile, 0)


def _dispatch(h2, rt, seg, loc, goff, fill, locrow, n_row_tiles):
    t = h2.shape[0]
    n_tiles = t // TM_MOE
    row = lambda i, *_: (i, 0)
    grid_spec = pltpu.PrefetchScalarGridSpec(
        num_scalar_prefetch=4,
        grid=(n_tiles,),
        in_specs=[
            pl.BlockSpec((TM_MOE, D_MODEL), row),
            pl.BlockSpec((TM_MOE, LANES), row),
            pl.BlockSpec((None, 1, LANES), lambda i, *_: (i, 0, 0)),
        ],
        out_specs=[
            pl.BlockSpec((TM_MOE, LANES), row),
            pl.BlockSpec(memory_space=pl.ANY),
        ],
        scratch_shapes=[
            pltpu.VMEM((2, R_LOC, D_MODEL), BF16),
            pltpu.SemaphoreType.DMA((2,)),
        ],
    )
    return pl.pallas_call(
        functools.partial(_dispatch_body, n_tiles=n_tiles, n_row_tiles=n_row_tiles),
        grid_spec=grid_spec,
        out_shape=[
            jax.ShapeDtypeStruct((t, LANES), F32),
            jax.ShapeDtypeStruct((n_row_tiles * TR_EXPERT, D_MODEL), BF16),
        ],
        compiler_params=pltpu.CompilerParams(
            dimension_semantics=("arbitrary",), vmem_limit_bytes=VMEM_LIMIT),
        name="moe_dispatch",
    )(seg, loc, goff, fill, h2, rt, locrow)


def _expert_body(te_ref, tw_ref, na_ref, xs_ref, w1a_ref, w3a_ref, w2a_ref, w1b_ref, w3b_ref, w2b_ref, y_ref):
    del tw_ref
    r = pl.program_id(0)
    n_active = na_ref[0]
    weights = ((w1a_ref, w3a_ref, w2a_ref), (w1b_ref, w3b_ref, w2b_ref))

    def swiglu(half, rows=None):
        w1_ref, w3_ref, w2_ref = weights[half]
        rows = slice(half * TR_EXPERT, (half + 1) * TR_EXPERT) if rows is None else rows
        x = xs_ref[rows, :]
        a = _silu(jnp.dot(x, w1_ref[...], preferred_element_type=F32))
        a = a * jnp.dot(x, w3_ref[...], preferred_element_type=F32)
        y_ref[rows, :] = jnp.dot(a.astype(BF16), w2_ref[...], preferred_element_type=F32).astype(BF16)

    first = EXPERT_TILES_PER_STEP * r
    both = first + 1 < n_active
    same = te_ref[first] == te_ref[first + 1]

    @pl.when(both & same)
    def _():
        swiglu(0, slice(0, EXPERT_TILES_PER_STEP * TR_EXPERT))

    @pl.when(both & jnp.logical_not(same))
    def _():
        swiglu(0)
        swiglu(1)

    @pl.when(first + 1 == n_active)
    def _():
        swiglu(0)
        y_ref[TR_EXPERT:, :] = jnp.zeros((TR_EXPERT, D_MODEL), BF16)

    @pl.when(first >= n_active)
    def _():
        y_ref[...] = jnp.zeros_like(y_ref)


def _experts(xs, tile_expert, tile_weights, n_active, w1b, w3b, w2b):
    n_rows = xs.shape[0]
    step_rows = EXPERT_TILES_PER_STEP * TR_EXPERT
    x_block = lambda r, te, tw, na: (jnp.minimum(r, (na[0] - 1) // EXPERT_TILES_PER_STEP), 0)
    w_specs = []
    for half in range(EXPERT_TILES_PER_STEP):
        w_block = lambda r, te, tw, na, half=half: (tw[EXPERT_TILES_PER_STEP * r + half], 0, 0)
        w_specs += [pl.BlockSpec((None, D_MODEL, D_EXPERT), w_block),
                    pl.BlockSpec((None, D_MODEL, D_EXPERT), w_block),
                    pl.BlockSpec((None, D_EXPERT, D_MODEL), w_block)]
    grid_spec = pltpu.PrefetchScalarGridSpec(
        num_scalar_prefetch=3,
        grid=(n_rows // step_rows,),
        in_specs=[pl.BlockSpec((step_rows, D_MODEL), x_block)] + w_specs,
        out_specs=pl.BlockSpec((step_rows, D_MODEL), lambda r, te, tw, na: (r, 0)),
    )
    return pl.pallas_call(
        _expert_body,
        grid_spec=grid_spec,
        out_shape=jax.ShapeDtypeStruct((n_rows, D_MODEL), BF16),
        compiler_params=pltpu.CompilerParams(
            dimension_semantics=("arbitrary",), vmem_limit_bytes=VMEM_LIMIT),
        name="moe_experts",
    )(tile_expert, tile_weights, n_active, xs, w1b, w3b, w2b, w1b, w3b, w2b)


def _combine_body(seg_ref, loc_ref, goff_ref, y_hbm, pos_ref, x_ref, gate_ref, fg_ref, o_ref, buf, sem, *,
                  n_tiles, final_norm):
    i = pl.program_id(0)
    slot = i % 2
    start = functools.partial(_segment_copies, seg_ref, loc_ref, goff_ref, hbm_ref=y_hbm, to_hbm=False)

    @pl.when(i == 0)
    def _():
        buf[...] = jnp.zeros_like(buf)
        start(0, buf.at[0], sem=sem.at[0])

    @pl.when(i + 1 < n_tiles)
    def _():
        start(i + 1, buf.at[1 - slot], sem=sem.at[1 - slot])

    _segment_wait(seg_ref, loc_ref, i, buf.at[slot], y_hbm, sem.at[slot], to_hbm=False)
    for part in range(TM_MOE // COMBINE_ROWS):
        rows = slice(part * COMBINE_ROWS, (part + 1) * COMBINE_ROWS)
        pp = pos_ref[rows, :]
        colp = lax.broadcasted_iota(jnp.int32, (COMBINE_ROWS, R_LOC), 1).astype(F32)
        wc = jnp.where(colp == pp[:, 0:1], pp[:, 2:3], jnp.where(colp == pp[:, 1:2], pp[:, 3:4], 0.0))
        moe = jnp.dot(wc.astype(BF16), buf[slot], preferred_element_type=F32)
        x2 = x_ref[rows, :] + gate_ref[...] * moe
        o_ref[rows, :] = _rms(x2) * fg_ref[...] if final_norm else x2


def _combine(y, pos, x1, mod3, final_g, seg, loc, goff, seq, final_norm):
    t = x1.shape[0]
    n_tiles = t // TM_MOE
    tiles_per_seq = seq // TM_MOE
    row = lambda i, *_: (i, 0)
    grid_spec = pltpu.PrefetchScalarGridSpec(
        num_scalar_prefetch=3,
        grid=(n_tiles,),
        in_specs=[
            pl.BlockSpec(memory_space=pl.ANY),
            pl.BlockSpec((TM_MOE, LANES), row),
            pl.BlockSpec((TM_MOE, D_MODEL), row),
            pl.BlockSpec((None, 1, D_MODEL), lambda i, *_: (i // tiles_per_seq, 0, 5)),
            pl.BlockSpec((1, D_MODEL), lambda i, *_: (0, 0)),
        ],
        out_specs=pl.BlockSpec((TM_MOE, D_MODEL), row),
        scratch_shapes=[
            pltpu.VMEM((2, R_LOC, D_MODEL), BF16),
            pltpu.SemaphoreType.DMA((2,)),
        ],
    )
    return pl.pallas_call(
        functools.partial(_combine_body, n_tiles=n_tiles, final_norm=final_norm),
        grid_spec=grid_spec,
        out_shape=jax.ShapeDtypeStruct((t, D_MODEL), F32),
        compiler_params=pltpu.CompilerParams(
            dimension_semantics=("arbitrary",), vmem_limit_bytes=VMEM_LIMIT),
        name="moe_combine",
    )(seg, loc, goff, y, pos, x1, mod3, final_g)


def _moe_plan(counts, n_tiles, n_row_tiles):
    cnt = counts.reshape(n_tiles, SUBLANES, LANES)[:, 0, :N_EXPERTS].astype(jnp.int32)
    seg = (cnt + SEG_ALIGN - 1) // SEG_ALIGN * SEG_ALIGN
    before_e = jnp.arange(N_EXPERTS)[:, None] < jnp.arange(N_EXPERTS)[None, :]
    before_t = jnp.arange(n_tiles)[:, None] > jnp.arange(n_tiles)[None, :]
    loc = jnp.sum(jnp.where(before_e[None], seg[:, :, None], 0), axis=1)
    tot = jnp.sum(seg, axis=0)
    region = (tot + TR_EXPERT - 1) // TR_EXPERT * TR_EXPERT
    gstart = jnp.sum(jnp.where(before_e, region[:, None], 0), axis=0)
    gend = gstart + region
    goff = gstart[None, :] + jnp.sum(jnp.where(before_t[:, :, None], seg[None], 0), axis=1)
    n_active = gend[-1] // TR_EXPERT
    tile_row0 = jnp.arange(n_row_tiles, dtype=jnp.int32) * TR_EXPERT
    last_row0 = (n_active - 1) * TR_EXPERT
    te = jnp.sum(gend[None, :] <= jnp.minimum(tile_row0, last_row0)[:, None], axis=1).astype(jnp.int32)
    te = jnp.minimum(te, N_EXPERTS - 1)
    te2 = te.reshape(-1, EXPERT_TILES_PER_STEP)
    cand = jnp.where(te2[:, 1] != te2[:, 0], te2[:, 1], 0)
    steps = jnp.arange(te2.shape[0])
    held = jnp.max(jnp.where(steps[None, :] <= steps[:, None], cand[None, :], 0), axis=1)
    tw = jnp.stack([te2[:, 0], held], axis=1).reshape(-1).astype(jnp.int32)
    locrow = _pad_cols(loc.astype(F32), LANES).reshape(n_tiles, 1, LANES)
    n_active = n_active.reshape(1).astype(jnp.int32)
    fill = jnp.concatenate([gstart + tot, region - tot, n_active]).astype(jnp.int32)
    return (seg.reshape(-1), loc.reshape(-1).astype(jnp.int32), goff.reshape(-1).astype(jnp.int32), fill,
            locrow, te, tw, n_active)


def _moe_final(x1, mod3, norm2_g, w_router, b_router, w1b, w3b, w2b, final_g, seq, final_norm):
    t = x1.shape[0]
    n_tiles = t // TM_MOE
    max_rows = N_ASSIGN * t + (SEG_ALIGN - 1) * N_EXPERTS * n_tiles + N_EXPERTS * (TR_EXPERT - SEG_ALIGN)
    step_rows = EXPERT_TILES_PER_STEP * TR_EXPERT
    n_row_tiles = -(-max_rows // step_rows) * EXPERT_TILES_PER_STEP
    h2, rt, counts = _router(x1, mod3, norm2_g, w_router, b_router, seq)
    seg, loc, goff, fill, locrow, te, tw, n_active = _moe_plan(counts, n_tiles, n_row_tiles)
    pos, xs = _dispatch(h2, rt, seg, loc, goff, fill, locrow, n_row_tiles)
    y = _experts(xs, te, tw, n_active, w1b, w3b, w2b)
    return _combine(y, pos, x1, mod3, final_g, seg, loc, goff, seq, final_norm)


def _pad_cols(a, width):
    return jnp.pad(a, ((0, 0), (0, width - a.shape[1])))


def kernel(x, c, w_ada, b_ada, norm1_g, w_in, b_in, gmlp_ln_g, gmlp_ln_b, gmlp_w_s, gmlp_b_s, gmlp_out_g, conv_w, conv_b, a_log_f, a_log_b, dt_bias_f, dt_bias_b, d_skip, ssd_norm_g, w_out, norm2_g, w_router_g, b_router_g, w_router_e, b_router_e, w1, w3, w2, final_g):
    bn, seq, _ = x.shape
    depth = w_ada.shape[0]
    x2 = x.reshape(bn * seq, D_MODEL)

    head_rows = jnp.arange(LANES)[:, None]
    head_cols = jnp.arange(SSD_WIDTH)[None, :] // SSD_HEAD_DIM
    expand = jnp.stack([head_rows == head_cols, head_rows == head_cols + SSD_HEADS]).astype(BF16)

    for l in range(depth):
        mod3 = _modulation(c, w_ada, b_ada[l], l).reshape(bn, 1, N_MOD * D_MODEL)

        w_in_p = _pad_cols(w_in[l].astype(BF16), IN_COLS_PAD)
        b_in_p = _pad_cols(b_in[l][None, :], IN_COLS_PAD)
        bs = jnp.repeat(gmlp_b_s[l].T, LANES, axis=1)
        bs_tile = jnp.tile(bs, (TM_PROJ // CHUNK, 1))
        ya, z, xbc, dt = _inproj_gmlp(
            x2, mod3, norm1_g[l][None, :], w_in_p, b_in_p, gmlp_ln_g[l][None, :], gmlp_ln_b[l][None, :],
            gmlp_w_s[l].astype(BF16), bs_tile, gmlp_out_g[l][None, :], seq)

        conv_w8 = jnp.pad(conv_w[l], ((0, SUBLANES - CONV_WIDTH), (0, 0)))
        dtb_row = _pad_cols(jnp.concatenate([dt_bias_f[l], dt_bias_b[l]])[None, :], LANES)
        alog_row = _pad_cols(jnp.concatenate([a_log_f[l], a_log_b[l]])[None, :], LANES)
        dsk_row = jnp.repeat(d_skip[l], SSD_HEAD_DIM)[None, :]
        x2, w1b, w3b, w2b = _ssd_outproj(
            xbc, dt, z, ya, x2, mod3, conv_w8, conv_b[l][None, :], dtb_row, alog_row, dsk_row,
            ssd_norm_g[l][None, :], w_out[l].astype(BF16), expand, w1[l], w3[l], w2[l], bn, seq)

        w_re = jnp.transpose(w_router_e[l], (1, 0, 2)).reshape(D_MODEL, N_EXPERTS)
        w_router = _pad_cols(jnp.concatenate([w_re, w_router_g[l]], axis=1), LANES)
        w_router_hi = w_router.astype(BF16)
        w_router = jnp.concatenate([w_router_hi, (w_router - w_router_hi.astype(F32)).astype(BF16)], axis=1)
        b_router = _pad_cols(jnp.concatenate([b_router_e[l].reshape(-1), b_router_g[l]])[None, :], LANES)
        x2 = _moe_final(x2, mod3, norm2_g[l][None, :], w_router, b_router, w1b, w3b, w2b,
                        final_g[None, :], seq, final_norm=(l == depth - 1))
    return x2.reshape(bn, seq, D_MODEL)
```

```python
import functools
import math

import jax
import jax.numpy as jnp
from jax import lax
from jax.experimental import pallas as pl
from jax.experimental.pallas import tpu as pltpu

F32 = jnp.float32
BF16 = jnp.bfloat16
HIGHEST = lax.Precision.HIGHEST

D_MODEL = 1024
N_MOD = 6
GMLP_WIDTH = 1024
GMLP_HEADS = 8
CHUNK = 128
SSD_STEP_ROWS = 4 * CHUNK
SSD_WIDTH = 1024
SSD_HEADS = 16
SSD_HEAD_DIM = 64
SSD_GROUPS = 2
SSD_STATE = 128
GROUP_WIDTH = SSD_WIDTH // SSD_GROUPS
CONV_WIDTH = 5
CONV_CH = SSD_WIDTH + 2 * SSD_GROUPS * SSD_STATE
N_EGROUPS = 4
EXPERTS_PER_GROUP = 8
N_EXPERTS = 32
D_EXPERT = 256
EPS = 1e-6
LOG2_E = 1.4426950408889634

LANES = 128
SUBLANES = 8
COL_U, COL_V, COL_Z, COL_XBC, COL_DT = 0, 1024, 2048, 3072, 4608
IN_COLS = 4640
IN_COLS_PAD = COL_DT + LANES
TM_PROJ = 1024
PROJ_ROWS = 512
COMBINE_ROWS = 256
ROUTER_ROWS = 256
TM_ROUTER = 1024
TM_MOE = 512
TR_EXPERT = 512
EXPERT_TILES_PER_STEP = 2
N_ASSIGN = 2
SEG_ALIGN = 2 * SUBLANES
R_LOC = N_ASSIGN * TM_MOE + N_EXPERTS * SEG_ALIGN
PIECE_ROWS = 128
REM_SIZES = tuple(PIECE_ROWS >> s for s in range(1, PIECE_ROWS.bit_length()) if PIECE_ROWS >> s >= SEG_ALIGN)
TOTAL_SIZES = tuple(1 << s for s in range(R_LOC.bit_length() - 1, -1, -1) if 1 << s >= SEG_ALIGN)
VMEM_LIMIT = 56 * 1024 * 1024


def _silu(v):
    return v * jax.nn.sigmoid(v)


def _gelu(v):
    return 0.5 * v * (1.0 + lax.erf(v * math.sqrt(0.5)))


def _softplus(v):
    return jnp.maximum(v, 0.0) + jnp.log1p(jnp.exp(-jnp.abs(v)))


def _rms(v):
    return v * lax.rsqrt(jnp.mean(v * v, axis=-1, keepdims=True) + EPS)


def _mod_body(c_ref, w_ref, b_ref, o_ref):
    ca = _silu(c_ref[...])
    o_ref[...] = jnp.dot(ca, w_ref[...], precision=HIGHEST, preferred_element_type=F32) + b_ref[...]


def _modulation(c, w_ada, b_ada, layer):
    bn = c.shape[0]
    return pl.pallas_call(
        _mod_body,
        grid=(N_MOD,),
        in_specs=[
            pl.BlockSpec((bn, D_MODEL), lambda j: (0, 0)),
            pl.BlockSpec((None, D_MODEL, D_MODEL), lambda j: (layer, 0, j)),
            pl.BlockSpec((1, D_MODEL), lambda j: (0, j)),
        ],
        out_specs=pl.BlockSpec((bn, D_MODEL), lambda j: (0, j)),
        out_shape=jax.ShapeDtypeStruct((bn, N_MOD * D_MODEL), F32),
        name="adaln_mod",
    )(c, w_ada, b_ada.reshape(1, -1))


def _inproj_body(x_ref, shift_ref, scale_ref, g_ref, w_ref, b_ref, lng_ref, lnb_ref, ws_ref, bs_ref,
                 og_ref, ya_ref, z_ref, xbc_ref, dt_ref, mix_scr):
    gain = g_ref[...] * (1.0 + scale_ref[...])
    for part in range(TM_PROJ // PROJ_ROWS):
        rows = slice(part * PROJ_ROWS, (part + 1) * PROJ_ROWS)
        hb = (_rms(x_ref[rows, :]) * gain + shift_ref[...]).astype(BF16)

        def proj(lo, hi, hb=hb):
            return jnp.dot(hb, w_ref[:, lo:hi], preferred_element_type=F32) + b_ref[:, lo:hi]

        z_ref[rows, :] = proj(COL_Z, COL_XBC).astype(BF16)
        xbc_ref[rows, :] = proj(COL_XBC, COL_DT)
        dt_ref[rows, :] = proj(COL_DT, IN_COLS_PAD)

        v = _gelu(proj(COL_V, COL_Z))
        mu = jnp.mean(v, axis=-1, keepdims=True)
        vc = v - mu
        var = jnp.mean(vc * vc, axis=-1, keepdims=True)
        vn = (vc * lax.rsqrt(var + EPS) * lng_ref[...] + lnb_ref[...]).astype(BF16)
        n_chunks = PROJ_ROWS // CHUNK
        for hd in range(GMLP_HEADS):
            cols = slice(hd * LANES, (hd + 1) * LANES)
            rhs = jnp.concatenate([vn[c * CHUNK:(c + 1) * CHUNK, cols] for c in range(n_chunks)], axis=1)
            res = jnp.dot(ws_ref[hd], rhs, preferred_element_type=F32)
            for c in range(n_chunks):
                mix_scr[part * PROJ_ROWS + c * CHUNK:part * PROJ_ROWS + (c + 1) * CHUNK, cols] = (
                    res[:, c * LANES:(c + 1) * LANES])
        u = _gelu(proj(COL_U, COL_V))
        out = u * (mix_scr[rows, :] + bs_ref[rows, :])
        ya_ref[rows, :] = (_rms(out) * og_ref[...]).astype(BF16)


def _inproj_gmlp(x2, mod3, norm1_g, w_in_p, b_in_p, ln_g, ln_b, w_s, bs_tile, out_g, seq):
    t = x2.shape[0]
    tiles_per_seq = seq // TM_PROJ
    row = lambda i: (i, 0)
    const2 = lambda i: (0, 0)
    return pl.pallas_call(
        _inproj_body,
        grid=(t // TM_PROJ,),
        in_specs=[
            pl.BlockSpec((TM_PROJ, D_MODEL), row),
            pl.BlockSpec((None, 1, D_MODEL), lambda i: (i // tiles_per_seq, 0, 0)),
            pl.BlockSpec((None, 1, D_MODEL), lambda i: (i // tiles_per_seq, 0, 1)),
            pl.BlockSpec((1, D_MODEL), const2),
            pl.BlockSpec((D_MODEL, IN_COLS_PAD), const2, pipeline_mode=pl.Buffered(1)),
            pl.BlockSpec((1, IN_COLS_PAD), const2),
            pl.BlockSpec((1, GMLP_WIDTH), const2),
            pl.BlockSpec((1, GMLP_WIDTH), const2),
            pl.BlockSpec((GMLP_HEADS, CHUNK, CHUNK), lambda i: (0, 0, 0)),
            pl.BlockSpec((TM_PROJ, GMLP_WIDTH), const2, pipeline_mode=pl.Buffered(1)),
            pl.BlockSpec((1, GMLP_WIDTH), const2),
        ],
        out_specs=[
            pl.BlockSpec((TM_PROJ, GMLP_WIDTH), row),
            pl.BlockSpec((TM_PROJ, SSD_WIDTH), row),
            pl.BlockSpec((TM_PROJ, CONV_CH), row),
            pl.BlockSpec((TM_PROJ, LANES), row),
        ],
        out_shape=[
            jax.ShapeDtypeStruct((t, GMLP_WIDTH), BF16),
            jax.ShapeDtypeStruct((t, SSD_WIDTH), BF16),
            jax.ShapeDtypeStruct((t, CONV_CH), F32),
            jax.ShapeDtypeStruct((t, LANES), F32),
        ],
        scratch_shapes=[pltpu.VMEM((TM_PROJ, GMLP_WIDTH), F32)],
        compiler_params=pltpu.CompilerParams(
            dimension_semantics=("arbitrary",), vmem_limit_bytes=VMEM_LIMIT),
        name="inproj_gmlp",
    )(x2, mod3, mod3, norm1_g, w_in_p, b_in_p, ln_g, ln_b, w_s, bs_tile, out_g)


def _ssd_chunk(act, dtv, a_row, expand, state, rev):
    off = SSD_HEADS if rev else 0
    row = lax.broadcasted_iota(jnp.int32, (CHUNK, CHUNK), 0)
    col = lax.broadcasted_iota(jnp.int32, (CHUNK, CHUNK), 1)
    lower = row >= col
    upper = row <= col
    keep = upper if rev else lower
    da = dtv * a_row
    cs = jnp.dot(keep.astype(F32), da, precision=HIGHEST, preferred_element_type=F32)
    cs2 = cs * LOG2_E
    col_t = cs2.T - jnp.log2(dtv.T)
    tot = cs[0:1, :] if rev else cs[CHUNK - 1:CHUNK, :]

    xs = act[:, :SSD_WIDTH]
    lane = lax.broadcasted_iota(jnp.int32, (CHUNK, LANES), 1)
    first_half = lane < SSD_HEAD_DIM
    zero = jnp.zeros((), BF16)

    stack = jnp.concatenate(
        [jnp.exp(cs), dtv * jnp.exp(tot - cs), jnp.broadcast_to(jnp.exp(tot), (SUBLANES, LANES))], axis=0)
    stack_x = jnp.dot(stack.astype(BF16), expand, preferred_element_type=F32)
    into_x = stack_x[:CHUNK]
    w_x = stack_x[CHUNK:2 * CHUNK]
    cd_x = stack_x[2 * CHUNK:2 * CHUNK + 1]
    xw = xs * w_x.astype(BF16)

    pieces = []
    for g in range(SSD_GROUPS):
        bg = act[:, SSD_WIDTH + g * SSD_STATE:SSD_WIDTH + (g + 1) * SSD_STATE]
        cg = act[:, SSD_WIDTH + SSD_GROUPS * SSD_STATE + g * SSD_STATE:
                 SSD_WIDTH + SSD_GROUPS * SSD_STATE + (g + 1) * SSD_STATE]
        scores = lax.dot_general(cg, bg, (((1,), (1,)), ((), ())), preferred_element_type=F32)
        heads_per_group = SSD_HEADS // SSD_GROUPS
        for pair in range(heads_per_group // 2):
            h0 = g * heads_per_group + 2 * pair
            xs_pair = xs[:, h0 * SSD_HEAD_DIM:(h0 + 2) * SSD_HEAD_DIM]
            y_pair = None
            for k in range(2):
                hh = off + h0 + k
                seg = cs2[:, hh:hh + 1] - col_t[hh:hh + 1, :]
                dec = jnp.exp2(jnp.where(keep, seg, -jnp.inf))
                m = (scores * dec).astype(BF16)
                rhs = jnp.where(first_half if k == 0 else jnp.logical_not(first_half), xs_pair, zero)
                part = jnp.dot(m, rhs, preferred_element_type=F32)
                y_pair = part if y_pair is None else y_pair + part
            pieces.append(y_pair)
    y_diag = jnp.concatenate(pieces, axis=1)

    y_off = []
    new_state = []
    for g in range(SSD_GROUPS):
        gcols = slice(g * GROUP_WIDTH, (g + 1) * GROUP_WIDTH)
        bg = act[:, SSD_WIDTH + g * SSD_STATE:SSD_WIDTH + (g + 1) * SSD_STATE]
        cg = act[:, SSD_WIDTH + SSD_GROUPS * SSD_STATE + g * SSD_STATE:
                 SSD_WIDTH + SSD_GROUPS * SSD_STATE + (g + 1) * SSD_STATE]
        prev = state[:, gcols]
        y_off.append(jnp.dot(cg, prev.astype(BF16), preferred_element_type=F32))
        bg_t = bg.astype(F32).T.astype(BF16)
        new = jnp.dot(bg_t, xw[:, gcols], preferred_element_type=F32)
        new_state.append(prev * cd_x[:, gcols] + new)
    y = y_diag + jnp.concatenate(y_off, axis=1) * into_x
    return y, jnp.concatenate(new_state, axis=1)


def _ssd_body(xbc_ref, xprev_ref, xnext_ref, dt_ref, z_ref, ya_ref, x_ref, gate_ref, cw_ref, cb_ref,
              dtb_ref, alog_ref, dsk_ref, ng_ref, wout_ref, exp_ref, w1_ref, w3_ref, w2_ref,
              o_ref, w1b_ref, w3b_ref, w2b_ref,
              act_scr, dts_scr, yf_scr, st_scr, ext_scr, *, n_steps):
    d = pl.program_id(1)
    c = pl.program_id(2)
    lane1 = lax.broadcasted_iota(jnp.int32, (1, LANES), 1)
    a_row = jnp.where(lane1 < 2 * SSD_HEADS, -jnp.exp(alog_ref[...]), 0.0)
    chunk_rows = [slice(j * CHUNK, (j + 1) * CHUNK) for j in range(SSD_STEP_ROWS // CHUNK)]

    @pl.when(c == 0)
    def _():
        st_scr[...] = jnp.zeros_like(st_scr)

    def scan(act, dtv, expand, rev):
        state = st_scr[...]
        ys = [None] * len(chunk_rows)
        for j in (reversed(range(len(chunk_rows))) if rev else range(len(chunk_rows))):
            ys[j], state = _ssd_chunk(act[chunk_rows[j], :], dtv[chunk_rows[j], :], a_row, expand, state, rev)
        st_scr[...] = state
        return jnp.concatenate(ys, axis=0)

    @pl.when(d == 0)
    def _forward():
        w1b_ref[...] = w1_ref[...].astype(BF16)
        w3b_ref[...] = w3_ref[...].astype(BF16)
        w2b_ref[...] = w2_ref[...].astype(BF16)
        row0 = pl.multiple_of(c * SSD_STEP_ROWS, SSD_STEP_ROWS)
        ext_scr[0:SUBLANES, :] = jnp.where(c > 0, xprev_ref[...], 0.0)
        ext_scr[SUBLANES:SUBLANES + SSD_STEP_ROWS, :] = xbc_ref[...]
        ext_scr[SUBLANES + SSD_STEP_ROWS:, :] = jnp.where(c < n_steps - 1, xnext_ref[...], 0.0)
        ext = ext_scr[...]
        n_ext = SSD_STEP_ROWS + 2 * SUBLANES
        body = slice(SUBLANES, SUBLANES + SSD_STEP_ROWS)
        down1 = pltpu.roll(ext, 1, 0)
        down2 = pltpu.roll(down1, 1, 0)
        up1 = pltpu.roll(ext, n_ext - 1, 0)
        up2 = pltpu.roll(up1, n_ext - 1, 0)
        acc = cb_ref[...]
        for k, tap in enumerate((down2, down1, ext, up1, up2)):
            acc = acc + cw_ref[k:k + 1, :] * tap[body, :]
        act = _silu(acc).astype(BF16)
        dtv = _softplus(dt_ref[...] + dtb_ref[...])
        act_scr[pl.ds(row0, SSD_STEP_ROWS), :] = act
        dts_scr[pl.ds(row0, SSD_STEP_ROWS), :] = dtv
        yf_scr[pl.ds(row0, SSD_STEP_ROWS), :] = scan(act, dtv, exp_ref[0], rev=False).astype(BF16)

    @pl.when(d == 1)
    def _backward():
        row0 = pl.multiple_of((n_steps - 1 - c) * SSD_STEP_ROWS, SSD_STEP_ROWS)
        act = act_scr[pl.ds(row0, SSD_STEP_ROWS), :]
        dtv = dts_scr[pl.ds(row0, SSD_STEP_ROWS), :]
        yb = scan(act, dtv, exp_ref[1], rev=True)
        xs = act[:, :SSD_WIDTH].astype(F32)
        y = yf_scr[pl.ds(row0, SSD_STEP_ROWS), :].astype(F32) + yb + dsk_ref[...] * xs
        y = y * _silu(z_ref[...].astype(F32))
        y = jnp.concatenate(
            [_rms(y[:, g * GROUP_WIDTH:(g + 1) * GROUP_WIDTH]) for g in range(SSD_GROUPS)], axis=1)
        y = y * ng_ref[...]
        mix = jnp.concatenate([ya_ref[...], y.astype(BF16)], axis=1)
        o = jnp.dot(mix, wout_ref[...], preferred_element_type=F32)
        o_ref[...] = x_ref[...] + gate_ref[...] * o


def _ssd_outproj(xbc, dt, z, ya, x2, mod3, conv_w8, conv_b, dtb_row, alog_row, dsk_row, norm_g, w_out_b,
                 expand, w1, w3, w2, bn, seq):
    t = x2.shape[0]
    nc = seq // SSD_STEP_ROWS
    blocks8 = SSD_STEP_ROWS // SUBLANES
    last8 = t // SUBLANES - 1

    def fwd_chunk(b, d, c):
        return b * nc + c + d * (nc - 1 - c)

    def bwd_chunk(b, d, c):
        return b * nc + nc - 1 - d * c

    const2 = lambda b, d, c: (0, 0)
    n_grid = bn * nc
    up_rows = N_EXPERTS * D_MODEL // n_grid
    down_rows = N_EXPERTS * D_EXPERT // n_grid
    assert up_rows * n_grid == N_EXPERTS * D_MODEL and up_rows % SEG_ALIGN == 0
    assert down_rows * n_grid == N_EXPERTS * D_EXPERT and down_rows % SEG_ALIGN == 0
    share = lambda b, d, c: (fwd_chunk(b, d, c), 0)
    outs = pl.pallas_call(
        functools.partial(_ssd_body, n_steps=nc),
        grid=(bn, 2, nc),
        in_specs=[
            pl.BlockSpec((SSD_STEP_ROWS, CONV_CH), lambda b, d, c: (fwd_chunk(b, d, c), 0)),
            pl.BlockSpec((SUBLANES, CONV_CH),
                         lambda b, d, c: (jnp.maximum(fwd_chunk(b, d, c) * blocks8 - 1, 0), 0)),
            pl.BlockSpec((SUBLANES, CONV_CH),
                         lambda b, d, c: (jnp.minimum((fwd_chunk(b, d, c) + 1) * blocks8, last8), 0)),
            pl.BlockSpec((SSD_STEP_ROWS, LANES), lambda b, d, c: (fwd_chunk(b, d, c), 0)),
            pl.BlockSpec((SSD_STEP_ROWS, SSD_WIDTH), lambda b, d, c: (bwd_chunk(b, d, c), 0)),
            pl.BlockSpec((SSD_STEP_ROWS, GMLP_WIDTH), lambda b, d, c: (bwd_chunk(b, d, c), 0)),
            pl.BlockSpec((SSD_STEP_ROWS, D_MODEL), lambda b, d, c: (bwd_chunk(b, d, c), 0)),
            pl.BlockSpec((None, 1, D_MODEL), lambda b, d, c: (b, 0, 2)),
            pl.BlockSpec((SUBLANES, CONV_CH), const2),
            pl.BlockSpec((1, CONV_CH), const2),
            pl.BlockSpec((1, LANES), const2),
            pl.BlockSpec((1, LANES), const2),
            pl.BlockSpec((1, SSD_WIDTH), const2),
            pl.BlockSpec((1, SSD_WIDTH), const2),
            pl.BlockSpec((GMLP_WIDTH + SSD_WIDTH, D_MODEL), const2, pipeline_mode=pl.Buffered(1)),
            pl.BlockSpec((2, LANES, SSD_WIDTH), lambda b, d, c: (0, 0, 0), pipeline_mode=pl.Buffered(1)),
            pl.BlockSpec((up_rows, D_EXPERT), share),
            pl.BlockSpec((up_rows, D_EXPERT), share),
            pl.BlockSpec((down_rows, D_MODEL), share),
        ],
        out_specs=[
            pl.BlockSpec((SSD_STEP_ROWS, D_MODEL), lambda b, d, c: (bwd_chunk(b, d, c), 0)),
            pl.BlockSpec((up_rows, D_EXPERT), share),
            pl.BlockSpec((up_rows, D_EXPERT), share),
            pl.BlockSpec((down_rows, D_MODEL), share),
        ],
        out_shape=[
            jax.ShapeDtypeStruct((t, D_MODEL), F32),
            jax.ShapeDtypeStruct((N_EXPERTS * D_MODEL, D_EXPERT), BF16),
            jax.ShapeDtypeStruct((N_EXPERTS * D_MODEL, D_EXPERT), BF16),
            jax.ShapeDtypeStruct((N_EXPERTS * D_EXPERT, D_MODEL), BF16),
        ],
        scratch_shapes=[
            pltpu.VMEM((seq, CONV_CH), BF16),
            pltpu.VMEM((seq, LANES), F32),
            pltpu.VMEM((seq, SSD_WIDTH), BF16),
            pltpu.VMEM((SSD_STATE, SSD_WIDTH), F32),
            pltpu.VMEM((SSD_STEP_ROWS + 2 * SUBLANES, CONV_CH), F32),
        ],
        compiler_params=pltpu.CompilerParams(
            dimension_semantics=("arbitrary", "arbitrary", "arbitrary"), vmem_limit_bytes=VMEM_LIMIT),
        name="ssd_outproj",
    )(xbc, xbc, xbc, dt, z, ya, x2, mod3, conv_w8, conv_b, dtb_row, alog_row, dsk_row, norm_g, w_out_b,
      expand, w1.reshape(-1, D_EXPERT), w3.reshape(-1, D_EXPERT), w2.reshape(-1, D_MODEL))
    x1, w1b, w3b, w2b = outs
    return (x1, w1b.reshape(N_EXPERTS, D_MODEL, D_EXPERT), w3b.reshape(N_EXPERTS, D_MODEL, D_EXPERT),
            w2b.reshape(N_EXPERTS, D_EXPERT, D_MODEL))


def _route(logits):
    lane = lax.broadcasted_iota(jnp.int32, logits.shape, 1)
    big = jnp.int32(LANES)
    neg = -jnp.inf
    gmask = (lane >= N_EXPERTS) & (lane < N_EXPERTS + N_EGROUPS)
    gl = jnp.where(gmask, logits, neg)
    gmax = jnp.max(gl, axis=-1, keepdims=True)
    gidx = jnp.min(jnp.where(gl == gmax, lane, big), axis=-1, keepdims=True) - N_EXPERTS
    p_g = 1.0 / jnp.sum(jnp.where(gmask, jnp.exp(gl - gmax), 0.0), axis=-1, keepdims=True)
    lo = gidx * EXPERTS_PER_GROUP
    emask = (lane >= lo) & (lane < lo + EXPERTS_PER_GROUP)
    el = jnp.where(emask, logits, neg)
    v1 = jnp.max(el, axis=-1, keepdims=True)
    i1 = jnp.min(jnp.where(el == v1, lane, big), axis=-1, keepdims=True)
    el2 = jnp.where(lane == i1, neg, el)
    v2 = jnp.max(el2, axis=-1, keepdims=True)
    i2 = jnp.min(jnp.where(el2 == v2, lane, big), axis=-1, keepdims=True)
    e2 = jnp.exp(v2 - v1)
    den = 1.0 + e2
    w1 = p_g / den
    w2 = p_g * e2 / den
    return i1, i2, w1, w2


def _router_body(x_ref, shift_ref, scale_ref, g2_ref, wr_ref, br_ref, h_ref, rt_ref, cnt_ref):
    counts = jnp.zeros((1, LANES), F32)
    gain = g2_ref[...] * (1.0 + scale_ref[...])
    groups_per_count = TM_MOE // ROUTER_ROWS
    for part in range(TM_ROUTER // ROUTER_ROWS):
        rows = slice(part * ROUTER_ROWS, (part + 1) * ROUTER_ROWS)
        h = _rms(x_ref[rows, :]) * gain + shift_ref[...]
        hb = h.astype(BF16)
        h_ref[rows, :] = hb
        h_lo = (h - hb.astype(F32)).astype(BF16)
        both = jnp.dot(hb, wr_ref[...], preferred_element_type=F32)
        logits = (both[:, :LANES] + both[:, LANES:]
                  + jnp.dot(h_lo, wr_ref[:, :LANES], preferred_element_type=F32)) + br_ref[...]
        i1, i2, w1, w2 = _route(logits)
        lane = lax.broadcasted_iota(jnp.int32, (ROUTER_ROWS, LANES), 1)
        rt_ref[rows, :] = jnp.where(lane == 0, i1.astype(F32), jnp.where(lane == 1, i2.astype(F32),
                                    jnp.where(lane == 2, w1, jnp.where(lane == 3, w2, 0.0))))
        mask = jnp.where(lane == i1, 1.0, jnp.where(lane == i2, 1.0, 0.0))
        counts = counts + jnp.sum(mask, axis=0, keepdims=True)
        if (part + 1) % groups_per_count == 0:
            tile = part // groups_per_count
            cnt_ref[tile * SUBLANES:(tile + 1) * SUBLANES, :] = jnp.broadcast_to(counts, (SUBLANES, LANES))
            counts = jnp.zeros((1, LANES), F32)


def _router(x1, mod3, norm2_g, w_router, b_router, seq):
    t = x1.shape[0]
    tiles_per_seq = seq // TM_ROUTER
    n_tiles = t // TM_MOE
    count_rows = TM_ROUTER // TM_MOE * SUBLANES
    row = lambda i: (i, 0)
    const2 = lambda i: (0, 0)
    return pl.pallas_call(
        _router_body,
        grid=(t // TM_ROUTER,),
        in_specs=[
            pl.BlockSpec((TM_ROUTER, D_MODEL), row),
            pl.BlockSpec((None, 1, D_MODEL), lambda i: (i // tiles_per_seq, 0, 3)),
            pl.BlockSpec((None, 1, D_MODEL), lambda i: (i // tiles_per_seq, 0, 4)),
            pl.BlockSpec((1, D_MODEL), const2),
            pl.BlockSpec((D_MODEL, 2 * LANES), const2),
            pl.BlockSpec((1, LANES), const2),
        ],
        out_specs=[
            pl.BlockSpec((TM_ROUTER, D_MODEL), row),
            pl.BlockSpec((TM_ROUTER, LANES), row),
            pl.BlockSpec((count_rows, LANES), row),
        ],
        out_shape=[
            jax.ShapeDtypeStruct((t, D_MODEL), BF16),
            jax.ShapeDtypeStruct((t, LANES), F32),
            jax.ShapeDtypeStruct((n_tiles * SUBLANES, LANES), F32),
        ],
        compiler_params=pltpu.CompilerParams(
            dimension_semantics=("arbitrary",), vmem_limit_bytes=VMEM_LIMIT),
        name="moe_router",
    )(x1, mod3, mod3, norm2_g, w_router, b_router)


def _row_copy(vmem_buf, v_start, hbm_ref, g_start, size, sem, to_hbm):
    v = vmem_buf.at[pl.ds(v_start, size), :]
    g = hbm_ref.at[pl.ds(g_start, size), :]
    return pltpu.make_async_copy(v, g, sem) if to_hbm else pltpu.make_async_copy(g, v, sem)


def _copy_rows(n, vmem_buf, v0, hbm_ref, g0, sem, *, to_hbm, wait):
    def run(v_off, g_off, size):
        v_start = 0 if v0 is None else pl.multiple_of(v0 + v_off, SEG_ALIGN)
        cp = _row_copy(vmem_buf, v_start, hbm_ref, pl.multiple_of(g0 + g_off, SEG_ALIGN), size, sem, to_hbm)
        if wait:
            cp.wait()
        else:
            cp.start()

    n_big = n // PIECE_ROWS

    def big_piece(k, carry):
        run(k * PIECE_ROWS, k * PIECE_ROWS, PIECE_ROWS)
        return carry

    lax.fori_loop(0, n_big, big_piece, 0)
    base = n_big * PIECE_ROWS
    rem = n - base
    for size in REM_SIZES:
        done = base + (rem // (2 * size)) * (2 * size)

        @pl.when((rem & size) != 0)
        def _():
            run(done, done, size)


def _segment_copies(seg_ref, loc_ref, goff_ref, tile, vmem_buf, hbm_ref, sem, *, to_hbm):
    base = tile * N_EXPERTS

    def per_expert(e, carry):
        _copy_rows(seg_ref[base + e], vmem_buf, loc_ref[base + e], hbm_ref, goff_ref[base + e], sem,
                   to_hbm=to_hbm, wait=False)
        return carry

    lax.fori_loop(0, N_EXPERTS, per_expert, 0)


def _tile_rows(seg_ref, loc_ref, tile):
    last = tile * N_EXPERTS + N_EXPERTS - 1
    return loc_ref[last] + seg_ref[last]


def _segment_wait(seg_ref, loc_ref, tile, vmem_buf, hbm_ref, sem, *, to_hbm):
    total = _tile_rows(seg_ref, loc_ref, tile)
    for size in TOTAL_SIZES:
        @pl.when((total & size) != 0)
        def _():
            _row_copy(vmem_buf, 0, hbm_ref, 0, size, sem, to_hbm).wait()


def _dispatch_body(seg_ref, loc_ref, goff_ref, fill_ref, h_ref, rt_ref, locrow_ref, pos_ref, xs_hbm, buf, sem,
                   *, n_tiles, n_row_tiles):
    i = pl.program_id(0)
    slot = i % 2

    start = functools.partial(_segment_copies, seg_ref, loc_ref, goff_ref, hbm_ref=xs_hbm, to_hbm=True)
    wait = functools.partial(_segment_wait, seg_ref, loc_ref, hbm_ref=xs_hbm, to_hbm=True)

    @pl.when(i >= 2)
    def _():
        wait(i - 2, buf.at[slot], sem=sem.at[slot])

    rt = rt_ref[...]
    lane = lax.broadcasted_iota(jnp.int32, (TM_MOE, LANES), 1)
    lanef = lane.astype(F32)
    e1 = rt[:, 0:1]
    e2 = rt[:, 1:2]
    mask = jnp.where(lanef == e1, 1.0, jnp.where(lanef == e2, 1.0, 0.0))
    r = lax.broadcasted_iota(jnp.int32, (TM_MOE, TM_MOE), 0)
    cc = lax.broadcasted_iota(jnp.int32, (TM_MOE, TM_MOE), 1)
    strict = jnp.where(r > cc, 1.0, 0.0).astype(BF16)
    rank = jnp.dot(strict, mask.astype(BF16), preferred_element_type=F32)
    posall = rank + locrow_ref[...]
    pos1 = jnp.sum(jnp.where(lanef == e1, posall, 0.0), axis=-1, keepdims=True)
    pos2 = jnp.sum(jnp.where(lanef == e2, posall, 0.0), axis=-1, keepdims=True)
    pp = jnp.where(lane == 0, pos1, jnp.where(lane == 1, pos2, rt))
    pos_ref[...] = pp

    slots_t = [pp[k * LANES:(k + 1) * LANES, :].T for k in range(TM_MOE // LANES)]

    rowp = lax.broadcasted_iota(jnp.int32, (R_LOC, LANES), 0).astype(F32)
    perm = jnp.concatenate(
        [jnp.where(rowp == st[0:1, :], 1.0, jnp.where(rowp == st[1:2, :], 1.0, 0.0)).astype(BF16)
         for st in slots_t], axis=1)
    buf[slot] = jnp.dot(perm, h_ref[...], preferred_element_type=F32).astype(BF16)
    start(i, buf.at[slot], sem=sem.at[slot])

    @pl.when(i == n_tiles - 1)
    def _():
        if n_tiles > 1:
            wait(i - 1, buf.at[1 - slot], sem=sem.at[1 - slot])
        wait(i, buf.at[slot], sem=sem.at[slot])
        zsrc = buf.at[1 - slot]
        zsem = sem.at[1 - slot]
        zsrc[0:TR_EXPERT, :] = jnp.zeros((TR_EXPERT, D_MODEL), BF16)
        for waiting in (False, True):
            def per_expert(e, carry):
                _copy_rows(fill_ref[N_EXPERTS + e], zsrc, None, xs_hbm, fill_ref[e], zsem,
                           to_hbm=True, wait=waiting)
                return carry

            def per_row_tile(k, carry):
                cp = _row_copy(zsrc, 0, xs_hbm, pl.multiple_of(k * TR_EXPERT, TR_EXPERT), TR_EXPERT, zsem, True)
                if waiting:
                    cp.wait()
                else:
                    cp.start()
                return carry

            lax.fori_loop(0, N_EXPERTS, per_expert, 0)
            lax.fori_loop(fill_ref[2 * N_EXPERTS], n_row_tiles, per_row_tile, 0)


def _dispatch(h2, rt, seg, loc, goff, fill, locrow, n_row_tiles):
    t = h2.shape[0]
    n_tiles = t // TM_MOE
    row = lambda i, *_: (i, 0)
    grid_spec = pltpu.PrefetchScalarGridSpec(
        num_scalar_prefetch=4,
        grid=(n_tiles,),
        in_specs=[
            pl.BlockSpec((TM_MOE, D_MODEL), row),
            pl.BlockSpec((TM_MOE, LANES), row),
            pl.BlockSpec((None, 1, LANES), lambda i, *_: (i, 0, 0)),
        ],
        out_specs=[
            pl.BlockSpec((TM_MOE, LANES), row),
            pl.BlockSpec(memory_space=pl.ANY),
        ],
        scratch_shapes=[
            pltpu.VMEM((2, R_LOC, D_MODEL), BF16),
            pltpu.SemaphoreType.DMA((2,)),
        ],
    )
    return pl.pallas_call(
        functools.partial(_dispatch_body, n_tiles=n_tiles, n_row_tiles=n_row_tiles),
        grid_spec=grid_spec,
        out_shape=[
            jax.ShapeDtypeStruct((t, LANES), F32),
            jax.ShapeDtypeStruct((n_row_tiles * TR_EXPERT, D_MODEL), BF16),
        ],
        compiler_params=pltpu.CompilerParams(
            dimension_semantics=("arbitrary",), vmem_limit_bytes=VMEM_LIMIT),
        name="moe_dispatch",
    )(seg, loc, goff, fill, h2, rt, locrow)


def _expert_body(te_ref, tw_ref, na_ref, xs_ref, w1a_ref, w3a_ref, w2a_ref, w1b_ref, w3b_ref, w2b_ref, y_ref):
    del tw_ref
    r = pl.program_id(0)
    n_active = na_ref[0]
    weights = ((w1a_ref, w3a_ref, w2a_ref), (w1b_ref, w3b_ref, w2b_ref))

    def swiglu(half, rows=None):
        w1_ref, w3_ref, w2_ref = weights[half]
        rows = slice(half * TR_EXPERT, (half + 1) * TR_EXPERT) if rows is None else rows
        x = xs_ref[rows, :]
        a = _silu(jnp.dot(x, w1_ref[...], preferred_element_type=F32))
        a = a * jnp.dot(x, w3_ref[...], preferred_element_type=F32)
        y_ref[rows, :] = jnp.dot(a.astype(BF16), w2_ref[...], preferred_element_type=F32).astype(BF16)

    first = EXPERT_TILES_PER_STEP * r
    both = first + 1 < n_active
    same = te_ref[first] == te_ref[first + 1]

    @pl.when(both & same)
    def _():
        swiglu(0, slice(0, EXPERT_TILES_PER_STEP * TR_EXPERT))

    @pl.when(both & jnp.logical_not(same))
    def _():
        swiglu(0)
        swiglu(1)

    @pl.when(first + 1 == n_active)
    def _():
        swiglu(0)
        y_ref[TR_EXPERT:, :] = jnp.zeros((TR_EXPERT, D_MODEL), BF16)

    @pl.when(first >= n_active)
    def _():
        y_ref[...] = jnp.zeros_like(y_ref)


def _experts(xs, tile_expert, tile_weights, n_active, w1b, w3b, w2b):
    n_rows = xs.shape[0]
    step_rows = EXPERT_TILES_PER_STEP * TR_EXPERT
    x_block = lambda r, te, tw, na: (jnp.minimum(r, (na[0] - 1) // EXPERT_TILES_PER_STEP), 0)
    w_specs = []
    for half in range(EXPERT_TILES_PER_STEP):
        w_block = lambda r, te, tw, na, half=half: (tw[EXPERT_TILES_PER_STEP * r + half], 0, 0)
        w_specs += [pl.BlockSpec((None, D_MODEL, D_EXPERT), w_block),
                    pl.BlockSpec((None, D_MODEL, D_EXPERT), w_block),
                    pl.BlockSpec((None, D_EXPERT, D_MODEL), w_block)]
    grid_spec = pltpu.PrefetchScalarGridSpec(
        num_scalar_prefetch=3,
        grid=(n_rows // step_rows,),
        in_specs=[pl.BlockSpec((step_rows, D_MODEL), x_block)] + w_specs,
        out_specs=pl.BlockSpec((step_rows, D_MODEL), lambda r, te, tw, na: (r, 0)),
    )
    return pl.pallas_call(
        _expert_body,
        grid_spec=grid_spec,
        out_shape=jax.ShapeDtypeStruct((n_rows, D_MODEL), BF16),
        compiler_params=pltpu.CompilerParams(
            dimension_semantics=("arbitrary",), vmem_limit_bytes=VMEM_LIMIT),
        name="moe_experts",
    )(tile_expert, tile_weights, n_active, xs, w1b, w3b, w2b, w1b, w3b, w2b)


def _combine_body(seg_ref, loc_ref, goff_ref, y_hbm, pos_ref, x_ref, gate_ref, fg_ref, o_ref, buf, sem, *,
                  n_tiles, final_norm):
    i = pl.program_id(0)
    slot = i % 2
    start = functools.partial(_segment_copies, seg_ref, loc_ref, goff_ref, hbm_ref=y_hbm, to_hbm=False)

    @pl.when(i == 0)
    def _():
        buf[...] = jnp.zeros_like(buf)
        start(0, buf.at[0], sem=sem.at[0])

    @pl.when(i + 1 < n_tiles)
    def _():
        start(i + 1, buf.at[1 - slot], sem=sem.at[1 - slot])

    _segment_wait(seg_ref, loc_ref, i, buf.at[slot], y_hbm, sem.at[slot], to_hbm=False)
    for part in range(TM_MOE // COMBINE_ROWS):
        rows = slice(part * COMBINE_ROWS, (part + 1) * COMBINE_ROWS)
        pp = pos_ref[rows, :]
        colp = lax.broadcasted_iota(jnp.int32, (COMBINE_ROWS, R_LOC), 1).astype(F32)
        wc = jnp.where(colp == pp[:, 0:1], pp[:, 2:3], jnp.where(colp == pp[:, 1:2], pp[:, 3:4], 0.0))
        moe = jnp.dot(wc.astype(BF16), buf[slot], preferred_element_type=F32)
        x2 = x_ref[rows, :] + gate_ref[...] * moe
        o_ref[rows, :] = _rms(x2) * fg_ref[...] if final_norm else x2


def _combine(y, pos, x1, mod3, final_g, seg, loc, goff, seq, final_norm):
    t = x1.shape[0]
    n_tiles = t // TM_MOE
    tiles_per_seq = seq // TM_MOE
    row = lambda i, *_: (i, 0)
    grid_spec = pltpu.PrefetchScalarGridSpec(
        num_scalar_prefetch=3,
        grid=(n_tiles,),
        in_specs=[
            pl.BlockSpec(memory_space=pl.ANY),
            pl.BlockSpec((TM_MOE, LANES), row),
            pl.BlockSpec((TM_MOE, D_MODEL), row),
            pl.BlockSpec((None, 1, D_MODEL), lambda i, *_: (i // tiles_per_seq, 0, 5)),
            pl.BlockSpec((1, D_MODEL), lambda i, *_: (0, 0)),
        ],
        out_specs=pl.BlockSpec((TM_MOE, D_MODEL), row),
        scratch_shapes=[
            pltpu.VMEM((2, R_LOC, D_MODEL), BF16),
            pltpu.SemaphoreType.DMA((2,)),
        ],
    )
    return pl.pallas_call(
        functools.partial(_combine_body, n_tiles=n_tiles, final_norm=final_norm),
        grid_spec=grid_spec,
        out_shape=jax.ShapeDtypeStruct((t, D_MODEL), F32),
        compiler_params=pltpu.CompilerParams(
            dimension_semantics=("arbitrary",), vmem_limit_bytes=VMEM_LIMIT),
        name="moe_combine",
    )(seg, loc, goff, y, pos, x1, mod3, final_g)


def _moe_plan(counts, n_tiles, n_row_tiles):
    cnt = counts.reshape(n_tiles, SUBLANES, LANES)[:, 0, :N_EXPERTS].astype(jnp.int32)
    seg = (cnt + SEG_ALIGN - 1) // SEG_ALIGN * SEG_ALIGN
    before_e = jnp.arange(N_EXPERTS)[:, None] < jnp.arange(N_EXPERTS)[None, :]
    before_t = jnp.arange(n_tiles)[:, None] > jnp.arange(n_tiles)[None, :]
    loc = jnp.sum(jnp.where(before_e[None], seg[:, :, None], 0), axis=1)
    tot = jnp.sum(seg, axis=0)
    region = (tot + TR_EXPERT - 1) // TR_EXPERT * TR_EXPERT
    gstart = jnp.sum(jnp.where(before_e, region[:, None], 0), axis=0)
    gend = gstart + region
    goff = gstart[None, :] + jnp.sum(jnp.where(before_t[:, :, None], seg[None], 0), axis=1)
    n_active = gend[-1] // TR_EXPERT
    tile_row0 = jnp.arange(n_row_tiles, dtype=jnp.int32) * TR_EXPERT
    last_row0 = (n_active - 1) * TR_EXPERT
    te = jnp.sum(gend[None, :] <= jnp.minimum(tile_row0, last_row0)[:, None], axis=1).astype(jnp.int32)
    te = jnp.minimum(te, N_EXPERTS - 1)
    te2 = te.reshape(-1, EXPERT_TILES_PER_STEP)
    cand = jnp.where(te2[:, 1] != te2[:, 0], te2[:, 1], 0)
    steps = jnp.arange(te2.shape[0])
    held = jnp.max(jnp.where(steps[None, :] <= steps[:, None], cand[None, :], 0), axis=1)
    tw = jnp.stack([te2[:, 0], held], axis=1).reshape(-1).astype(jnp.int32)
    locrow = _pad_cols(loc.astype(F32), LANES).reshape(n_tiles, 1, LANES)
    n_active = n_active.reshape(1).astype(jnp.int32)
    fill = jnp.concatenate([gstart + tot, region - tot, n_active]).astype(jnp.int32)
    return (seg.reshape(-1), loc.reshape(-1).astype(jnp.int32), goff.reshape(-1).astype(jnp.int32), fill,
            locrow, te, tw, n_active)


def _moe_final(x1, mod3, norm2_g, w_router, b_router, w1b, w3b, w2b, final_g, seq, final_norm):
    t = x1.shape[0]
    n_tiles = t // TM_MOE
    max_rows = N_ASSIGN * t + (SEG_ALIGN - 1) * N_EXPERTS * n_tiles + N_EXPERTS * (TR_EXPERT - SEG_ALIGN)
    step_rows = EXPERT_TILES_PER_STEP * TR_EXPERT
    n_row_tiles = -(-max_rows // step_rows) * EXPERT_TILES_PER_STEP
    h2, rt, counts = _router(x1, mod3, norm2_g, w_router, b_router, seq)
    seg, loc, goff, fill, locrow, te, tw, n_active = _moe_plan(counts, n_tiles, n_row_tiles)
    pos, xs = _dispatch(h2, rt, seg, loc, goff, fill, locrow, n_row_tiles)
    y = _experts(xs, te, tw, n_active, w1b, w3b, w2b)
    return _combine(y, pos, x1, mod3, final_g, seg, loc, goff, seq, final_norm)


def _pad_cols(a, width):
    return jnp.pad(a, ((0, 0), (0, width - a.shape[1])))


def kernel(x, c, w_ada, b_ada, norm1_g, w_in, b_in, gmlp_ln_g, gmlp_ln_b, gmlp_w_s, gmlp_b_s, gmlp_out_g, conv_w, conv_b, a_log_f, a_log_b, dt_bias_f, dt_bias_b, d_skip, ssd_norm_g, w_out, norm2_g, w_router_g, b_router_g, w_router_e, b_router_e, w1, w3, w2, final_g):
    bn, seq, _ = x.shape
    depth = w_ada.shape[0]
    x2 = x.reshape(bn * seq, D_MODEL)

    head_rows = jnp.arange(LANES)[:, None]
    head_cols = jnp.arange(SSD_WIDTH)[None, :] // SSD_HEAD_DIM
    expand = jnp.stack([head_rows == head_cols, head_rows == head_cols + SSD_HEADS]).astype(BF16)

    for l in range(depth):
        mod3 = _modulation(c, w_ada, b_ada[l], l).reshape(bn, 1, N_MOD * D_MODEL)

        w_in_p = _pad_cols(w_in[l].astype(BF16), IN_COLS_PAD)
        b_in_p = _pad_cols(b_in[l][None, :], IN_COLS_PAD)
        bs = jnp.repeat(gmlp_b_s[l].T, LANES, axis=1)
        bs_tile = jnp.tile(bs, (TM_PROJ // CHUNK, 1))
        ya, z, xbc, dt = _inproj_gmlp(
            x2, mod3, norm1_g[l][None, :], w_in_p, b_in_p, gmlp_ln_g[l][None, :], gmlp_ln_b[l][None, :],
            gmlp_w_s[l].astype(BF16), bs_tile, gmlp_out_g[l][None, :], seq)

        conv_w8 = jnp.pad(conv_w[l], ((0, SUBLANES - CONV_WIDTH), (0, 0)))
        dtb_row = _pad_cols(jnp.concatenate([dt_bias_f[l], dt_bias_b[l]])[None, :], LANES)
        alog_row = _pad_cols(jnp.concatenate([a_log_f[l], a_log_b[l]])[None, :], LANES)
        dsk_row = jnp.repeat(d_skip[l], SSD_HEAD_DIM)[None, :]
        x2, w1b, w3b, w2b = _ssd_outproj(
            xbc, dt, z, ya, x2, mod3, conv_w8, conv_b[l][None, :], dtb_row, alog_row, dsk_row,
            ssd_norm_g[l][None, :], w_out[l].astype(BF16), expand, w1[l], w3[l], w2[l], bn, seq)

        w_re = jnp.transpose(w_router_e[l], (1, 0, 2)).reshape(D_MODEL, N_EXPERTS)
        w_router = _pad_cols(jnp.concatenate([w_re, w_router_g[l]], axis=1), LANES)
        w_router_hi = w_router.astype(BF16)
        w_router = jnp.concatenate([w_router_hi, (w_router - w_router_hi.astype(F32)).astype(BF16)], axis=1)
        b_router = _pad_cols(jnp.concatenate([b_router_e[l].reshape(-1), b_router_g[l]])[None, :], LANES)
        x2 = _moe_final(x2, mod3, norm2_g[l][None, :], w_router, b_router, w1b, w3b, w2b,
                        final_g[None, :], seq, final_norm=(l == depth - 1))
    return x2.reshape(bn, seq, D_MODEL)
```

```python
import functools
import math

import jax
import jax.numpy as jnp
from jax import lax
from jax.experimental import pallas as pl
from jax.experimental.pallas import tpu as pltpu

F32 = jnp.float32
BF16 = jnp.bfloat16
HIGHEST = lax.Precision.HIGHEST

D_MODEL = 1024
N_MOD = 6
GMLP_WIDTH = 1024
GMLP_HEADS = 8
CHUNK = 128
SSD_STEP_ROWS = 4 * CHUNK
SSD_WIDTH = 1024
SSD_HEADS = 16
SSD_HEAD_DIM = 64
SSD_GROUPS = 2
SSD_STATE = 128
GROUP_WIDTH = SSD_WIDTH // SSD_GROUPS
CONV_WIDTH = 5
CONV_CH = SSD_WIDTH + 2 * SSD_GROUPS * SSD_STATE
N_EGROUPS = 4
EXPERTS_PER_GROUP = 8
N_EXPERTS = 32
D_EXPERT = 256
EPS = 1e-6
LOG2_E = 1.4426950408889634

LANES = 128
SUBLANES = 8
COL_U, COL_V, COL_Z, COL_XBC, COL_DT = 0, 1024, 2048, 3072, 4608
IN_COLS = 4640
IN_COLS_PAD = COL_DT + LANES
TM_PROJ = 1024
PROJ_ROWS = 512
COMBINE_ROWS = 256
ROUTER_ROWS = 512
TM_ROUTER = 2048
TM_MOE = 512
TR_EXPERT = 512
EXPERT_TILES_PER_STEP = 2
N_ASSIGN = 2
SEG_ALIGN = 2 * SUBLANES
R_LOC = N_ASSIGN * TM_MOE + N_EXPERTS * SEG_ALIGN
PIECE_ROWS = 128
REM_SIZES = tuple(PIECE_ROWS >> s for s in range(1, PIECE_ROWS.bit_length()) if PIECE_ROWS >> s >= SEG_ALIGN)
TOTAL_SIZES = tuple(1 << s for s in range(R_LOC.bit_length() - 1, -1, -1) if 1 << s >= SEG_ALIGN)
VMEM_LIMIT = 56 * 1024 * 1024


def _silu(v):
    return v * jax.nn.sigmoid(v)


def _gelu(v):
    return 0.5 * v * (1.0 + lax.erf(v * math.sqrt(0.5)))


def _softplus(v):
    return jnp.maximum(v, 0.0) + jnp.log1p(jnp.exp(-jnp.abs(v)))


def _rms(v):
    return v * lax.rsqrt(jnp.mean(v * v, axis=-1, keepdims=True) + EPS)


def _mod_body(c_ref, w_ref, b_ref, o_ref):
    ca = _silu(c_ref[...])
    o_ref[...] = jnp.dot(ca, w_ref[...], precision=HIGHEST, preferred_element_type=F32) + b_ref[...]


def _modulation(c, w_ada, b_ada, layer):
    bn = c.shape[0]
    return pl.pallas_call(
        _mod_body,
        grid=(N_MOD,),
        in_specs=[
            pl.BlockSpec((bn, D_MODEL), lambda j: (0, 0)),
            pl.BlockSpec((None, D_MODEL, D_MODEL), lambda j: (layer, 0, j)),
            pl.BlockSpec((1, D_MODEL), lambda j: (0, j)),
        ],
        out_specs=pl.BlockSpec((bn, D_MODEL), lambda j: (0, j)),
        out_shape=jax.ShapeDtypeStruct((bn, N_MOD * D_MODEL), F32),
        name="adaln_mod",
    )(c, w_ada, b_ada.reshape(1, -1))


def _inproj_body(x_ref, shift_ref, scale_ref, g_ref, w_ref, b_ref, lng_ref, lnb_ref, ws_ref, bs_ref,
                 og_ref, ya_ref, z_ref, xbc_ref, dt_ref, mix_scr):
    gain = g_ref[...] * (1.0 + scale_ref[...])
    for part in range(TM_PROJ // PROJ_ROWS):
        rows = slice(part * PROJ_ROWS, (part + 1) * PROJ_ROWS)
        hb = (_rms(x_ref[rows, :]) * gain + shift_ref[...]).astype(BF16)

        def proj(lo, hi, hb=hb):
            return jnp.dot(hb, w_ref[:, lo:hi], preferred_element_type=F32) + b_ref[:, lo:hi]

        z_ref[rows, :] = proj(COL_Z, COL_XBC).astype(BF16)
        xbc_ref[rows, :] = proj(COL_XBC, COL_DT)
        dt_ref[rows, :] = proj(COL_DT, IN_COLS_PAD)

        v = _gelu(proj(COL_V, COL_Z))
        mu = jnp.mean(v, axis=-1, keepdims=True)
        vc = v - mu
        var = jnp.mean(vc * vc, axis=-1, keepdims=True)
        vn = (vc * lax.rsqrt(var + EPS) * lng_ref[...] + lnb_ref[...]).astype(BF16)
        n_chunks = PROJ_ROWS // CHUNK
        for hd in range(GMLP_HEADS):
            cols = slice(hd * LANES, (hd + 1) * LANES)
            rhs = jnp.concatenate([vn[c * CHUNK:(c + 1) * CHUNK, cols] for c in range(n_chunks)], axis=1)
            res = jnp.dot(ws_ref[hd], rhs, preferred_element_type=F32)
            for c in range(n_chunks):
                mix_scr[part * PROJ_ROWS + c * CHUNK:part * PROJ_ROWS + (c + 1) * CHUNK, cols] = (
                    res[:, c * LANES:(c + 1) * LANES])
        u = _gelu(proj(COL_U, COL_V))
        out = u * (mix_scr[rows, :] + bs_ref[rows, :])
        ya_ref[rows, :] = (_rms(out) * og_ref[...]).astype(BF16)


def _inproj_gmlp(x2, mod3, norm1_g, w_in_p, b_in_p, ln_g, ln_b, w_s, bs_tile, out_g, seq):
    t = x2.shape[0]
    tiles_per_seq = seq // TM_PROJ
    row = lambda i: (i, 0)
    const2 = lambda i: (0, 0)
    return pl.pallas_call(
        _inproj_body,
        grid=(t // TM_PROJ,),
        in_specs=[
            pl.BlockSpec((TM_PROJ, D_MODEL), row),
            pl.BlockSpec((None, 1, D_MODEL), lambda i: (i // tiles_per_seq, 0, 0)),
            pl.BlockSpec((None, 1, D_MODEL), lambda i: (i // tiles_per_seq, 0, 1)),
            pl.BlockSpec((1, D_MODEL), const2),
            pl.BlockSpec((D_MODEL, IN_COLS_PAD), const2, pipeline_mode=pl.Buffered(1)),
            pl.BlockSpec((1, IN_COLS_PAD), const2),
            pl.BlockSpec((1, GMLP_WIDTH), const2),
            pl.BlockSpec((1, GMLP_WIDTH), const2),
            pl.BlockSpec((GMLP_HEADS, CHUNK, CHUNK), lambda i: (0, 0, 0)),
            pl.BlockSpec((TM_PROJ, GMLP_WIDTH), const2, pipeline_mode=pl.Buffered(1)),
            pl.BlockSpec((1, GMLP_WIDTH), const2),
        ],
        out_specs=[
            pl.BlockSpec((TM_PROJ, GMLP_WIDTH), row),
            pl.BlockSpec((TM_PROJ, SSD_WIDTH), row),
            pl.BlockSpec((TM_PROJ, CONV_CH), row),
            pl.BlockSpec((TM_PROJ, LANES), row),
        ],
        out_shape=[
            jax.ShapeDtypeStruct((t, GMLP_WIDTH), BF16),
            jax.ShapeDtypeStruct((t, SSD_WIDTH), BF16),
            jax.ShapeDtypeStruct((t, CONV_CH), F32),
            jax.ShapeDtypeStruct((t, LANES), F32),
        ],
        scratch_shapes=[pltpu.VMEM((TM_PROJ, GMLP_WIDTH), F32)],
        compiler_params=pltpu.CompilerParams(
            dimension_semantics=("arbitrary",), vmem_limit_bytes=VMEM_LIMIT),
        name="inproj_gmlp",
    )(x2, mod3, mod3, norm1_g, w_in_p, b_in_p, ln_g, ln_b, w_s, bs_tile, out_g)


def _ssd_chunk(act, dtv, a_row, expand, state, rev):
    off = SSD_HEADS if rev else 0
    row = lax.broadcasted_iota(jnp.int32, (CHUNK, CHUNK), 0)
    col = lax.broadcasted_iota(jnp.int32, (CHUNK, CHUNK), 1)
    lower = row >= col
    upper = row <= col
    keep = upper if rev else lower
    da = dtv * a_row
    cs = jnp.dot(keep.astype(F32), da, precision=HIGHEST, preferred_element_type=F32)
    cs2 = cs * LOG2_E
    col_t = cs2.T - jnp.log2(dtv.T)
    tot = cs[0:1, :] if rev else cs[CHUNK - 1:CHUNK, :]

    xs = act[:, :SSD_WIDTH]
    lane = lax.broadcasted_iota(jnp.int32, (CHUNK, LANES), 1)
    first_half = lane < SSD_HEAD_DIM
    zero = jnp.zeros((), BF16)

    stack = jnp.concatenate(
        [jnp.exp(cs), dtv * jnp.exp(tot - cs), jnp.broadcast_to(jnp.exp(tot), (SUBLANES, LANES))], axis=0)
    stack_x = jnp.dot(stack.astype(BF16), expand, preferred_element_type=F32)
    into_x = stack_x[:CHUNK]
    w_x = stack_x[CHUNK:2 * CHUNK]
    cd_x = stack_x[2 * CHUNK:2 * CHUNK + 1]
    xw = xs * w_x.astype(BF16)

    pieces = []
    for g in range(SSD_GROUPS):
        bg = act[:, SSD_WIDTH + g * SSD_STATE:SSD_WIDTH + (g + 1) * SSD_STATE]
        cg = act[:, SSD_WIDTH + SSD_GROUPS * SSD_STATE + g * SSD_STATE:
                 SSD_WIDTH + SSD_GROUPS * SSD_STATE + (g + 1) * SSD_STATE]
        scores = lax.dot_general(cg, bg, (((1,), (1,)), ((), ())), preferred_element_type=F32)
        heads_per_group = SSD_HEADS // SSD_GROUPS
        for pair in range(heads_per_group // 2):
            h0 = g * heads_per_group + 2 * pair
            xs_pair = xs[:, h0 * SSD_HEAD_DIM:(h0 + 2) * SSD_HEAD_DIM]
            y_pair = None
            for k in range(2):
                hh = off + h0 + k
                seg = cs2[:, hh:hh + 1] - col_t[hh:hh + 1, :]
                dec = jnp.exp2(jnp.where(keep, seg, -jnp.inf))
                m = (scores * dec).astype(BF16)
                rhs = jnp.where(first_half if k == 0 else jnp.logical_not(first_half), xs_pair, zero)
                part = jnp.dot(m, rhs, preferred_element_type=F32)
                y_pair = part if y_pair is None else y_pair + part
            pieces.append(y_pair)
    y_diag = jnp.concatenate(pieces, axis=1)

    y_off = []
    new_state = []
    for g in range(SSD_GROUPS):
        gcols = slice(g * GROUP_WIDTH, (g + 1) * GROUP_WIDTH)
        bg = act[:, SSD_WIDTH + g * SSD_STATE:SSD_WIDTH + (g + 1) * SSD_STATE]
        cg = act[:, SSD_WIDTH + SSD_GROUPS * SSD_STATE + g * SSD_STATE:
                 SSD_WIDTH + SSD_GROUPS * SSD_STATE + (g + 1) * SSD_STATE]
        prev = state[:, gcols]
        y_off.append(jnp.dot(cg, prev.astype(BF16), preferred_element_type=F32))
        bg_t = bg.astype(F32).T.astype(BF16)
        new = jnp.dot(bg_t, xw[:, gcols], preferred_element_type=F32)
        new_state.append(prev * cd_x[:, gcols] + new)
    y = y_diag + jnp.concatenate(y_off, axis=1) * into_x
    return y, jnp.concatenate(new_state, axis=1)


def _ssd_body(xbc_ref, xprev_ref, xnext_ref, dt_ref, z_ref, ya_ref, x_ref, gate_ref, cw_ref, cb_ref,
              dtb_ref, alog_ref, dsk_ref, ng_ref, wout_ref, exp_ref, w1_ref, w3_ref, w2_ref,
              o_ref, w1b_ref, w3b_ref, w2b_ref,
              act_scr, dts_scr, yf_scr, st_scr, ext_scr, *, n_steps):
    d = pl.program_id(1)
    c = pl.program_id(2)
    lane1 = lax.broadcasted_iota(jnp.int32, (1, LANES), 1)
    a_row = jnp.where(lane1 < 2 * SSD_HEADS, -jnp.exp(alog_ref[...]), 0.0)
    chunk_rows = [slice(j * CHUNK, (j + 1) * CHUNK) for j in range(SSD_STEP_ROWS // CHUNK)]

    @pl.when(c == 0)
    def _():
        st_scr[...] = jnp.zeros_like(st_scr)

    def scan(act, dtv, expand, rev):
        state = st_scr[...]
        ys = [None] * len(chunk_rows)
        for j in (reversed(range(len(chunk_rows))) if rev else range(len(chunk_rows))):
            ys[j], state = _ssd_chunk(act[chunk_rows[j], :], dtv[chunk_rows[j], :], a_row, expand, state, rev)
        st_scr[...] = state
        return jnp.concatenate(ys, axis=0)

    @pl.when(d == 0)
    def _forward():
        w1b_ref[...] = w1_ref[...].astype(BF16)
        w3b_ref[...] = w3_ref[...].astype(BF16)
        w2b_ref[...] = w2_ref[...].astype(BF16)
        row0 = pl.multiple_of(c * SSD_STEP_ROWS, SSD_STEP_ROWS)
        ext_scr[0:SUBLANES, :] = jnp.where(c > 0, xprev_ref[...], 0.0)
        ext_scr[SUBLANES:SUBLANES + SSD_STEP_ROWS, :] = xbc_ref[...]
        ext_scr[SUBLANES + SSD_STEP_ROWS:, :] = jnp.where(c < n_steps - 1, xnext_ref[...], 0.0)
        ext = ext_scr[...]
        n_ext = SSD_STEP_ROWS + 2 * SUBLANES
        body = slice(SUBLANES, SUBLANES + SSD_STEP_ROWS)
        down1 = pltpu.roll(ext, 1, 0)
        down2 = pltpu.roll(down1, 1, 0)
        up1 = pltpu.roll(ext, n_ext - 1, 0)
        up2 = pltpu.roll(up1, n_ext - 1, 0)
        acc = cb_ref[...]
        for k, tap in enumerate((down2, down1, ext, up1, up2)):
            acc = acc + cw_ref[k:k + 1, :] * tap[body, :]
        act = _silu(acc).astype(BF16)
        dtv = _softplus(dt_ref[...] + dtb_ref[...])
        act_scr[pl.ds(row0, SSD_STEP_ROWS), :] = act
        dts_scr[pl.ds(row0, SSD_STEP_ROWS), :] = dtv
        yf_scr[pl.ds(row0, SSD_STEP_ROWS), :] = scan(act, dtv, exp_ref[0], rev=False).astype(BF16)

    @pl.when(d == 1)
    def _backward():
        row0 = pl.multiple_of((n_steps - 1 - c) * SSD_STEP_ROWS, SSD_STEP_ROWS)
        act = act_scr[pl.ds(row0, SSD_STEP_ROWS), :]
        dtv = dts_scr[pl.ds(row0, SSD_STEP_ROWS), :]
        yb = scan(act, dtv, exp_ref[1], rev=True)
        xs = act[:, :SSD_WIDTH].astype(F32)
        y = yf_scr[pl.ds(row0, SSD_STEP_ROWS), :].astype(F32) + yb + dsk_ref[...] * xs
        y = y * _silu(z_ref[...].astype(F32))
        y = jnp.concatenate(
            [_rms(y[:, g * GROUP_WIDTH:(g + 1) * GROUP_WIDTH]) for g in range(SSD_GROUPS)], axis=1)
        y = y * ng_ref[...]
        mix = jnp.concatenate([ya_ref[...], y.astype(BF16)], axis=1)
        o = jnp.dot(mix, wout_ref[...], preferred_element_type=F32)
        o_ref[...] = x_ref[...] + gate_ref[...] * o


def _ssd_outproj(xbc, dt, z, ya, x2, mod3, conv_w8, conv_b, dtb_row, alog_row, dsk_row, norm_g, w_out_b,
                 expand, w1, w3, w2, bn, seq):
    t = x2.shape[0]
    nc = seq // SSD_STEP_ROWS
    blocks8 = SSD_STEP_ROWS // SUBLANES
    last8 = t // SUBLANES - 1

    def fwd_chunk(b, d, c):
        return b * nc + c + d * (nc - 1 - c)

    def bwd_chunk(b, d, c):
        return b * nc + nc - 1 - d * c

    const2 = lambda b, d, c: (0, 0)
    n_grid = bn * nc
    up_rows = N_EXPERTS * D_MODEL // n_grid
    down_rows = N_EXPERTS * D_EXPERT // n_grid
    assert up_rows * n_grid == N_EXPERTS * D_MODEL and up_rows % SEG_ALIGN == 0
    assert down_rows * n_grid == N_EXPERTS * D_EXPERT and down_rows % SEG_ALIGN == 0
    share = lambda b, d, c: (fwd_chunk(b, d, c), 0)
    outs = pl.pallas_call(
        functools.partial(_ssd_body, n_steps=nc),
        grid=(bn, 2, nc),
        in_specs=[
            pl.BlockSpec((SSD_STEP_ROWS, CONV_CH), lambda b, d, c: (fwd_chunk(b, d, c), 0)),
            pl.BlockSpec((SUBLANES, CONV_CH),
                         lambda b, d, c: (jnp.maximum(fwd_chunk(b, d, c) * blocks8 - 1, 0), 0)),
            pl.BlockSpec((SUBLANES, CONV_CH),
                         lambda b, d, c: (jnp.minimum((fwd_chunk(b, d, c) + 1) * blocks8, last8), 0)),
            pl.BlockSpec((SSD_STEP_ROWS, LANES), lambda b, d, c: (fwd_chunk(b, d, c), 0)),
            pl.BlockSpec((SSD_STEP_ROWS, SSD_WIDTH), lambda b, d, c: (bwd_chunk(b, d, c), 0)),
            pl.BlockSpec((SSD_STEP_ROWS, GMLP_WIDTH), lambda b, d, c: (bwd_chunk(b, d, c), 0)),
            pl.BlockSpec((SSD_STEP_ROWS, D_MODEL), lambda b, d, c: (bwd_chunk(b, d, c), 0)),
            pl.BlockSpec((None, 1, D_MODEL), lambda b, d, c: (b, 0, 2)),
            pl.BlockSpec((SUBLANES, CONV_CH), const2),
            pl.BlockSpec((1, CONV_CH), const2),
            pl.BlockSpec((1, LANES), const2),
            pl.BlockSpec((1, LANES), const2),
            pl.BlockSpec((1, SSD_WIDTH), const2),
            pl.BlockSpec((1, SSD_WIDTH), const2),
            pl.BlockSpec((GMLP_WIDTH + SSD_WIDTH, D_MODEL), const2, pipeline_mode=pl.Buffered(1)),
            pl.BlockSpec((2, LANES, SSD_WIDTH), lambda b, d, c: (0, 0, 0), pipeline_mode=pl.Buffered(1)),
            pl.BlockSpec((up_rows, D_EXPERT), share),
            pl.BlockSpec((up_rows, D_EXPERT), share),
            pl.BlockSpec((down_rows, D_MODEL), share),
        ],
        out_specs=[
            pl.BlockSpec((SSD_STEP_ROWS, D_MODEL), lambda b, d, c: (bwd_chunk(b, d, c), 0)),
            pl.BlockSpec((up_rows, D_EXPERT), share),
            pl.BlockSpec((up_rows, D_EXPERT), share),
            pl.BlockSpec((down_rows, D_MODEL), share),
        ],
        out_shape=[
            jax.ShapeDtypeStruct((t, D_MODEL), F32),
            jax.ShapeDtypeStruct((N_EXPERTS * D_MODEL, D_EXPERT), BF16),
            jax.ShapeDtypeStruct((N_EXPERTS * D_MODEL, D_EXPERT), BF16),
            jax.ShapeDtypeStruct((N_EXPERTS * D_EXPERT, D_MODEL), BF16),
        ],
        scratch_shapes=[
            pltpu.VMEM((seq, CONV_CH), BF16),
            pltpu.VMEM((seq, LANES), F32),
            pltpu.VMEM((seq, SSD_WIDTH), BF16),
            pltpu.VMEM((SSD_STATE, SSD_WIDTH), F32),
            pltpu.VMEM((SSD_STEP_ROWS + 2 * SUBLANES, CONV_CH), F32),
        ],
        compiler_params=pltpu.CompilerParams(
            dimension_semantics=("arbitrary", "arbitrary", "arbitrary"), vmem_limit_bytes=VMEM_LIMIT),
        name="ssd_outproj",
    )(xbc, xbc, xbc, dt, z, ya, x2, mod3, conv_w8, conv_b, dtb_row, alog_row, dsk_row, norm_g, w_out_b,
      expand, w1.reshape(-1, D_EXPERT), w3.reshape(-1, D_EXPERT), w2.reshape(-1, D_MODEL))
    x1, w1b, w3b, w2b = outs
    return (x1, w1b.reshape(N_EXPERTS, D_MODEL, D_EXPERT), w3b.reshape(N_EXPERTS, D_MODEL, D_EXPERT),
            w2b.reshape(N_EXPERTS, D_EXPERT, D_MODEL))


def _route(logits):
    lane = lax.broadcasted_iota(jnp.int32, logits.shape, 1)
    big = jnp.int32(LANES)
    neg = -jnp.inf
    gmask = (lane >= N_EXPERTS) & (lane < N_EXPERTS + N_EGROUPS)
    gl = jnp.where(gmask, logits, neg)
    gmax = jnp.max(gl, axis=-1, keepdims=True)
    gidx = jnp.min(jnp.where(gl == gmax, lane, big), axis=-1, keepdims=True) - N_EXPERTS
    p_g = 1.0 / jnp.sum(jnp.where(gmask, jnp.exp(gl - gmax), 0.0), axis=-1, keepdims=True)
    lo = gidx * EXPERTS_PER_GROUP
    emask = (lane >= lo) & (lane < lo + EXPERTS_PER_GROUP)
    el = jnp.where(emask, logits, neg)
    v1 = jnp.max(el, axis=-1, keepdims=True)
    i1 = jnp.min(jnp.where(el == v1, lane, big), axis=-1, keepdims=True)
    el2 = jnp.where(lane == i1, neg, el)
    v2 = jnp.max(el2, axis=-1, keepdims=True)
    i2 = jnp.min(jnp.where(el2 == v2, lane, big), axis=-1, keepdims=True)
    e2 = jnp.exp(v2 - v1)
    den = 1.0 + e2
    w1 = p_g / den
    w2 = p_g * e2 / den
    return i1, i2, w1, w2


def _router_body(x_ref, shift_ref, scale_ref, g2_ref, wr_ref, br_ref, h_ref, rt_ref, cnt_ref):
    counts = jnp.zeros((1, LANES), F32)
    gain = g2_ref[...] * (1.0 + scale_ref[...])
    groups_per_count = TM_MOE // ROUTER_ROWS
    for part in range(TM_ROUTER // ROUTER_ROWS):
        rows = slice(part * ROUTER_ROWS, (part + 1) * ROUTER_ROWS)
        h = _rms(x_ref[rows, :]) * gain + shift_ref[...]
        hb = h.astype(BF16)
        h_ref[rows, :] = hb
        h_lo = (h - hb.astype(F32)).astype(BF16)
        both = jnp.dot(hb, wr_ref[...], preferred_element_type=F32)
        logits = (both[:, :LANES] + both[:, LANES:]
                  + jnp.dot(h_lo, wr_ref[:, :LANES], preferred_element_type=F32)) + br_ref[...]
        i1, i2, w1, w2 = _route(logits)
        lane = lax.broadcasted_iota(jnp.int32, (ROUTER_ROWS, LANES), 1)
        rt_ref[rows, :] = jnp.where(lane == 0, i1.astype(F32), jnp.where(lane == 1, i2.astype(F32),
                                    jnp.where(lane == 2, w1, jnp.where(lane == 3, w2, 0.0))))
        mask = jnp.where(lane == i1, 1.0, jnp.where(lane == i2, 1.0, 0.0))
        counts = counts + jnp.sum(mask, axis=0, keepdims=True)
        if (part + 1) % groups_per_count == 0:
            tile = part // groups_per_count
            cnt_ref[tile * SUBLANES:(tile + 1) * SUBLANES, :] = jnp.broadcast_to(counts, (SUBLANES, LANES))
            counts = jnp.zeros((1, LANES), F32)


def _router(x1, mod3, norm2_g, w_router, b_router, seq):
    t = x1.shape[0]
    tiles_per_seq = seq // TM_ROUTER
    n_tiles = t // TM_MOE
    count_rows = TM_ROUTER // TM_MOE * SUBLANES
    row = lambda i: (i, 0)
    const2 = lambda i: (0, 0)
    return pl.pallas_call(
        _router_body,
        grid=(t // TM_ROUTER,),
        in_specs=[
            pl.BlockSpec((TM_ROUTER, D_MODEL), row),
            pl.BlockSpec((None, 1, D_MODEL), lambda i: (i // tiles_per_seq, 0, 3)),
            pl.BlockSpec((None, 1, D_MODEL), lambda i: (i // tiles_per_seq, 0, 4)),
            pl.BlockSpec((1, D_MODEL), const2),
            pl.BlockSpec((D_MODEL, 2 * LANES), const2),
            pl.BlockSpec((1, LANES), const2),
        ],
        out_specs=[
            pl.BlockSpec((TM_ROUTER, D_MODEL), row),
            pl.BlockSpec((TM_ROUTER, LANES), row),
            pl.BlockSpec((count_rows, LANES), row),
        ],
        out_shape=[
            jax.ShapeDtypeStruct((t, D_MODEL), BF16),
            jax.ShapeDtypeStruct((t, LANES), F32),
            jax.ShapeDtypeStruct((n_tiles * SUBLANES, LANES), F32),
        ],
        compiler_params=pltpu.CompilerParams(
            dimension_semantics=("arbitrary",), vmem_limit_bytes=VMEM_LIMIT),
        name="moe_router",
    )(x1, mod3, mod3, norm2_g, w_router, b_router)


def _row_copy(vmem_buf, v_start, hbm_ref, g_start, size, sem, to_hbm):
    v = vmem_buf.at[pl.ds(v_start, size), :]
    g = hbm_ref.at[pl.ds(g_start, size), :]
    return pltpu.make_async_copy(v, g, sem) if to_hbm else pltpu.make_async_copy(g, v, sem)


def _copy_rows(n, vmem_buf, v0, hbm_ref, g0, sem, *, to_hbm, wait):
    def run(v_off, g_off, size):
        v_start = 0 if v0 is None else pl.multiple_of(v0 + v_off, SEG_ALIGN)
        cp = _row_copy(vmem_buf, v_start, hbm_ref, pl.multiple_of(g0 + g_off, SEG_ALIGN), size, sem, to_hbm)
        if wait:
            cp.wait()
        else:
            cp.start()

    n_big = n // PIECE_ROWS

    def big_piece(k, carry):
        run(k * PIECE_ROWS, k * PIECE_ROWS, PIECE_ROWS)
        return carry

    lax.fori_loop(0, n_big, big_piece, 0)
    base = n_big * PIECE_ROWS
    rem = n - base
    for size in REM_SIZES:
        done = base + (rem // (2 * size)) * (2 * size)

        @pl.when((rem & size) != 0)
        def _():
            run(done, done, size)


def _segment_copies(seg_ref, loc_ref, goff_ref, tile, vmem_buf, hbm_ref, sem, *, to_hbm):
    base = tile * N_EXPERTS

    def per_expert(e, carry):
        _copy_rows(seg_ref[base + e], vmem_buf, loc_ref[base + e], hbm_ref, goff_ref[base + e], sem,
                   to_hbm=to_hbm, wait=False)
        return carry

    lax.fori_loop(0, N_EXPERTS, per_expert, 0)


def _tile_rows(seg_ref, loc_ref, tile):
    last = tile * N_EXPERTS + N_EXPERTS - 1
    return loc_ref[last] + seg_ref[last]


def _segment_wait(seg_ref, loc_ref, tile, vmem_buf, hbm_ref, sem, *, to_hbm):
    total = _tile_rows(seg_ref, loc_ref, tile)
    for size in TOTAL_SIZES:
        @pl.when((total & size) != 0)
        def _():
            _row_copy(vmem_buf, 0, hbm_ref, 0, size, sem, to_hbm).wait()


def _dispatch_body(seg_ref, loc_ref, goff_ref, fill_ref, h_ref, rt_ref, locrow_ref, pos_ref, xs_hbm, buf, sem,
                   *, n_tiles, n_row_tiles):
    i = pl.program_id(0)
    slot = i % 2

    start = functools.partial(_segment_copies, seg_ref, loc_ref, goff_ref, hbm_ref=xs_hbm, to_hbm=True)
    wait = functools.partial(_segment_wait, seg_ref, loc_ref, hbm_ref=xs_hbm, to_hbm=True)

    @pl.when(i >= 2)
    def _():
        wait(i - 2, buf.at[slot], sem=sem.at[slot])

    rt = rt_ref[...]
    lane = lax.broadcasted_iota(jnp.int32, (TM_MOE, LANES), 1)
    lanef = lane.astype(F32)
    e1 = rt[:, 0:1]
    e2 = rt[:, 1:2]
    mask = jnp.where(lanef == e1, 1.0, jnp.where(lanef == e2, 1.0, 0.0))
    r = lax.broadcasted_iota(jnp.int32, (TM_MOE, TM_MOE), 0)
    cc = lax.broadcasted_iota(jnp.int32, (TM_MOE, TM_MOE), 1)
    strict = jnp.where(r > cc, 1.0, 0.0).astype(BF16)
    rank = jnp.dot(strict, mask.astype(BF16), preferred_element_type=F32)
    posall = rank + locrow_ref[...]
    pos1 = jnp.sum(jnp.where(lanef == e1, posall, 0.0), axis=-1, keepdims=True)
    pos2 = jnp.sum(jnp.where(lanef == e2, posall, 0.0), axis=-1, keepdims=True)
    pp = jnp.where(lane == 0, pos1, jnp.where(lane == 1, pos2, rt))
    pos_ref[...] = pp

    slots_t = [pp[k * LANES:(k + 1) * LANES, :].T for k in range(TM_MOE // LANES)]

    rowp = lax.broadcasted_iota(jnp.int32, (R_LOC, LANES), 0).astype(F32)
    perm = jnp.concatenate(
        [jnp.where(rowp == st[0:1, :], 1.0, jnp.where(rowp == st[1:2, :], 1.0, 0.0)).astype(BF16)
         for st in slots_t], axis=1)
    buf[slot] = jnp.dot(perm, h_ref[...], preferred_element_type=F32).astype(BF16)
    start(i, buf.at[slot], sem=sem.at[slot])

    @pl.when(i == n_tiles - 1)
    def _():
        if n_tiles > 1:
            wait(i - 1, buf.at[1 - slot], sem=sem.at[1 - slot])
        wait(i, buf.at[slot], sem=sem.at[slot])
        zsrc = buf.at[1 - slot]
        zsem = sem.at[1 - slot]
        zsrc[0:TR_EXPERT, :] = jnp.zeros((TR_EXPERT, D_MODEL), BF16)
        for waiting in (False, True):
            def per_expert(e, carry):
                _copy_rows(fill_ref[N_EXPERTS + e], zsrc, None, xs_hbm, fill_ref[e], zsem,
                           to_hbm=True, wait=waiting)
                return carry

            def per_row_tile(k, carry):
                cp = _row_copy(zsrc, 0, xs_hbm, pl.multiple_of(k * TR_EXPERT, TR_EXPERT), TR_EXPERT, zsem, True)
                if waiting:
                    cp.wait()
                else:
                    cp.start()
                return carry

            lax.fori_loop(0, N_EXPERTS, per_expert, 0)
            lax.fori_loop(fill_ref[2 * N_EXPERTS], n_row_tiles, per_row_tile, 0)


def _dispatch(h2, rt, seg, loc, goff, fill, locrow, n_row_tiles):
    t = h2.shape[0]
    n_tiles = t // TM_MOE
    row = lambda i, *_: (i, 0)
    grid_spec = pltpu.PrefetchScalarGridSpec(
        num_scalar_prefetch=4,
        grid=(n_tiles,),
        in_specs=[
            pl.BlockSpec((TM_MOE, D_MODEL), row),
            pl.BlockSpec((TM_MOE, LANES), row),
            pl.BlockSpec((None, 1, LANES), lambda i, *_: (i, 0, 0)),
        ],
        out_specs=[
            pl.BlockSpec((TM_MOE, LANES), row),
            pl.BlockSpec(memory_space=pl.ANY),
        ],
        scratch_shapes=[
            pltpu.VMEM((2, R_LOC, D_MODEL), BF16),
            pltpu.SemaphoreType.DMA((2,)),
        ],
    )
    return pl.pallas_call(
        functools.partial(_dispatch_body, n_tiles=n_tiles, n_row_tiles=n_row_tiles),
        grid_spec=grid_spec,
        out_shape=[
            jax.ShapeDtypeStruct((t, LANES), F32),
            jax.ShapeDtypeStruct((n_row_tiles * TR_EXPERT, D_MODEL), BF16),
        ],
        compiler_params=pltpu.CompilerParams(
            dimension_semantics=("arbitrary",), vmem_limit_bytes=VMEM_LIMIT),
        name="moe_dispatch",
    )(seg, loc, goff, fill, h2, rt, locrow)


def _expert_body(te_ref, tw_ref, na_ref, xs_ref, w1a_ref, w3a_ref, w2a_ref, w1b_ref, w3b_ref, w2b_ref, y_ref):
    del tw_ref
    r = pl.program_id(0)
    n_active = na_ref[0]
    weights = ((w1a_ref, w3a_ref, w2a_ref), (w1b_ref, w3b_ref, w2b_ref))

    def swiglu(half, rows=None):
        w1_ref, w3_ref, w2_ref = weights[half]
        rows = slice(half * TR_EXPERT, (half + 1) * TR_EXPERT) if rows is None else rows
        x = xs_ref[rows, :]
        a = _silu(jnp.dot(x, w1_ref[...], preferred_element_type=F32))
        a = a * jnp.dot(x, w3_ref[...], preferred_element_type=F32)
        y_ref[rows, :] = jnp.dot(a.astype(BF16), w2_ref[...], preferred_element_type=F32).astype(BF16)

    first = EXPERT_TILES_PER_STEP * r
    both = first + 1 < n_active
    same = te_ref[first] == te_ref[first + 1]

    @pl.when(both & same)
    def _():
        swiglu(0, slice(0, EXPERT_TILES_PER_STEP * TR_EXPERT))

    @pl.when(both & jnp.logical_not(same))
    def _():
        swiglu(0)
        swiglu(1)

    @pl.when(first + 1 == n_active)
    def _():
        swiglu(0)
        y_ref[TR_EXPERT:, :] = jnp.zeros((TR_EXPERT, D_MODEL), BF16)

    @pl.when(first >= n_active)
    def _():
        y_ref[...] = jnp.zeros_like(y_ref)


def _experts(xs, tile_expert, tile_weights, n_active, w1b, w3b, w2b):
    n_rows = xs.shape[0]
    step_rows = EXPERT_TILES_PER_STEP * TR_EXPERT
    x_block = lambda r, te, tw, na: (jnp.minimum(r, (na[0] - 1) // EXPERT_TILES_PER_STEP), 0)
    w_specs = []
    for half in range(EXPERT_TILES_PER_STEP):
        w_block = lambda r, te, tw, na, half=half: (tw[EXPERT_TILES_PER_STEP * r + half], 0, 0)
        w_specs += [pl.BlockSpec((None, D_MODEL, D_EXPERT), w_block),
                    pl.BlockSpec((None, D_MODEL, D_EXPERT), w_block),
                    pl.BlockSpec((None, D_EXPERT, D_MODEL), w_block)]
    grid_spec = pltpu.PrefetchScalarGridSpec(
        num_scalar_prefetch=3,
        grid=(n_rows // step_rows,),
        in_specs=[pl.BlockSpec((step_rows, D_MODEL), x_block)] + w_specs,
        out_specs=pl.BlockSpec((step_rows, D_MODEL), lambda r, te, tw, na: (r, 0)),
    )
    return pl.pallas_call(
        _expert_body,
        grid_spec=grid_spec,
        out_shape=jax.ShapeDtypeStruct((n_rows, D_MODEL), BF16),
        compiler_params=pltpu.CompilerParams(
            dimension_semantics=("arbitrary",), vmem_limit_bytes=VMEM_LIMIT),
        name="moe_experts",
    )(tile_expert, tile_weights, n_active, xs, w1b, w3b, w2b, w1b, w3b, w2b)


def _combine_body(seg_ref, loc_ref, goff_ref, y_hbm, pos_ref, x_ref, gate_ref, fg_ref, o_ref, buf, sem, *,
                  n_tiles, final_norm):
    i = pl.program_id(0)
    slot = i % 2
    start = functools.partial(_segment_copies, seg_ref, loc_ref, goff_ref, hbm_ref=y_hbm, to_hbm=False)

    @pl.when(i == 0)
    def _():
        buf[...] = jnp.zeros_like(buf)
        start(0, buf.at[0], sem=sem.at[0])

    @pl.when(i + 1 < n_tiles)
    def _():
        start(i + 1, buf.at[1 - slot], sem=sem.at[1 - slot])

    _segment_wait(seg_ref, loc_ref, i, buf.at[slot], y_hbm, sem.at[slot], to_hbm=False)
    for part in range(TM_MOE // COMBINE_ROWS):
        rows = slice(part * COMBINE_ROWS, (part + 1) * COMBINE_ROWS)
        pp = pos_ref[rows, :]
        colp = lax.broadcasted_iota(jnp.int32, (COMBINE_ROWS, R_LOC), 1).astype(F32)
        wc = jnp.where(colp == pp[:, 0:1], pp[:, 2:3], jnp.where(colp == pp[:, 1:2], pp[:, 3:4], 0.0))
        moe = jnp.dot(wc.astype(BF16), buf[slot], preferred_element_type=F32)
        x2 = x_ref[rows, :] + gate_ref[...] * moe
        o_ref[rows, :] = _rms(x2) * fg_ref[...] if final_norm else x2


def _combine(y, pos, x1, mod3, final_g, seg, loc, goff, seq, final_norm):
    t = x1.shape[0]
    n_tiles = t // TM_MOE
    tiles_per_seq = seq // TM_MOE
    row = lambda i, *_: (i, 0)
    grid_spec = pltpu.PrefetchScalarGridSpec(
        num_scalar_prefetch=3,
        grid=(n_tiles,),
        in_specs=[
            pl.BlockSpec(memory_space=pl.ANY),
            pl.BlockSpec((TM_MOE, LANES), row),
            pl.BlockSpec((TM_MOE, D_MODEL), row),
            pl.BlockSpec((None, 1, D_MODEL), lambda i, *_: (i // tiles_per_seq, 0, 5)),
            pl.BlockSpec((1, D_MODEL), lambda i, *_: (0, 0)),
        ],
        out_specs=pl.BlockSpec((TM_MOE, D_MODEL), row),
        scratch_shapes=[
            pltpu.VMEM((2, R_LOC, D_MODEL), BF16),
            pltpu.SemaphoreType.DMA((2,)),
        ],
    )
    return pl.pallas_call(
        functools.partial(_combine_body, n_tiles=n_tiles, final_norm=final_norm),
        grid_spec=grid_spec,
        out_shape=jax.ShapeDtypeStruct((t, D_MODEL), F32),
        compiler_params=pltpu.CompilerParams(
            dimension_semantics=("arbitrary",), vmem_limit_bytes=VMEM_LIMIT),
        name="moe_combine",
    )(seg, loc, goff, y, pos, x1, mod3, final_g)


def _moe_plan(counts, n_tiles, n_row_tiles):
    cnt = counts.reshape(n_tiles, SUBLANES, LANES)[:, 0, :N_EXPERTS].astype(jnp.int32)
    seg = (cnt + SEG_ALIGN - 1) // SEG_ALIGN * SEG_ALIGN
    before_e = jnp.arange(N_EXPERTS)[:, None] < jnp.arange(N_EXPERTS)[None, :]
    before_t = jnp.arange(n_tiles)[:, None] > jnp.arange(n_tiles)[None, :]
    loc = jnp.sum(jnp.where(before_e[None], seg[:, :, None], 0), axis=1)
    tot = jnp.sum(seg, axis=0)
    region = (tot + TR_EXPERT - 1) // TR_EXPERT * TR_EXPERT
    gstart = jnp.sum(jnp.where(before_e, region[:, None], 0), axis=0)
    gend = gstart + region
    goff = gstart[None, :] + jnp.sum(jnp.where(before_t[:, :, None], seg[None], 0), axis=1)
    n_active = gend[-1] // TR_EXPERT
    tile_row0 = jnp.arange(n_row_tiles, dtype=jnp.int32) * TR_EXPERT
    last_row0 = (n_active - 1) * TR_EXPERT
    te = jnp.sum(gend[None, :] <= jnp.minimum(tile_row0, last_row0)[:, None], axis=1).astype(jnp.int32)
    te = jnp.minimum(te, N_EXPERTS - 1)
    te2 = te.reshape(-1, EXPERT_TILES_PER_STEP)
    cand = jnp.where(te2[:, 1] != te2[:, 0], te2[:, 1], 0)
    steps = jnp.arange(te2.shape[0])
    held = jnp.max(jnp.where(steps[None, :] <= steps[:, None], cand[None, :], 0), axis=1)
    tw = jnp.stack([te2[:, 0], held], axis=1).reshape(-1).astype(jnp.int32)
    locrow = _pad_cols(loc.astype(F32), LANES).reshape(n_tiles, 1, LANES)
    n_active = n_active.reshape(1).astype(jnp.int32)
    fill = jnp.concatenate([gstart + tot, region - tot, n_active]).astype(jnp.int32)
    return (seg.reshape(-1), loc.reshape(-1).astype(jnp.int32), goff.reshape(-1).astype(jnp.int32), fill,
            locrow, te, tw, n_active)


def _moe_final(x1, mod3, norm2_g, w_router, b_router, w1b, w3b, w2b, final_g, seq, final_norm):
    t = x1.shape[0]
    n_tiles = t // TM_MOE
    max_rows = N_ASSIGN * t + (SEG_ALIGN - 1) * N_EXPERTS * n_tiles + N_EXPERTS * (TR_EXPERT - SEG_ALIGN)
    step_rows = EXPERT_TILES_PER_STEP * TR_EXPERT
    n_row_tiles = -(-max_rows // step_rows) * EXPERT_TILES_PER_STEP
    h2, rt, counts = _router(x1, mod3, norm2_g, w_router, b_router, seq)
    seg, loc, goff, fill, locrow, te, tw, n_active = _moe_plan(counts, n_tiles, n_row_tiles)
    pos, xs = _dispatch(h2, rt, seg, loc, goff, fill, locrow, n_row_tiles)
    y = _experts(xs, te, tw, n_active, w1b, w3b, w2b)
    return _combine(y, pos, x1, mod3, final_g, seg, loc, goff, seq, final_norm)


def _pad_cols(a, width):
    return jnp.pad(a, ((0, 0), (0, width - a.shape[1])))


def kernel(x, c, w_ada, b_ada, norm1_g, w_in, b_in, gmlp_ln_g, gmlp_ln_b, gmlp_w_s, gmlp_b_s, gmlp_out_g, conv_w, conv_b, a_log_f, a_log_b, dt_bias_f, dt_bias_b, d_skip, ssd_norm_g, w_out, norm2_g, w_router_g, b_router_g, w_router_e, b_router_e, w1, w3, w2, final_g):
    bn, seq, _ = x.shape
    depth = w_ada.shape[0]
    x2 = x.reshape(bn * seq, D_MODEL)

    head_rows = jnp.arange(LANES)[:, None]
    head_cols = jnp.arange(SSD_WIDTH)[None, :] // SSD_HEAD_DIM
    expand = jnp.stack([head_rows == head_cols, head_rows == head_cols + SSD_HEADS]).astype(BF16)

    for l in range(depth):
        mod3 = _modulation(c, w_ada, b_ada[l], l).reshape(bn, 1, N_MOD * D_MODEL)

        w_in_p = _pad_cols(w_in[l].astype(BF16), IN_COLS_PAD)
        b_in_p = _pad_cols(b_in[l][None, :], IN_COLS_PAD)
        bs = jnp.repeat(gmlp_b_s[l].T, LANES, axis=1)
        bs_tile = jnp.tile(bs, (TM_PROJ // CHUNK, 1))
        ya, z, xbc, dt = _inproj_gmlp(
            x2, mod3, norm1_g[l][None, :], w_in_p, b_in_p, gmlp_ln_g[l][None, :], gmlp_ln_b[l][None, :],
            gmlp_w_s[l].astype(BF16), bs_tile, gmlp_out_g[l][None, :], seq)

        conv_w8 = jnp.pad(conv_w[l], ((0, SUBLANES - CONV_WIDTH), (0, 0)))
        dtb_row = _pad_cols(jnp.concatenate([dt_bias_f[l], dt_bias_b[l]])[None, :], LANES)
        alog_row = _pad_cols(jnp.concatenate([a_log_f[l], a_log_b[l]])[None, :], LANES)
        dsk_row = jnp.repeat(d_skip[l], SSD_HEAD_DIM)[None, :]
        x2, w1b, w3b, w2b = _ssd_outproj(
            xbc, dt, z, ya, x2, mod3, conv_w8, conv_b[l][None, :], dtb_row, alog_row, dsk_row,
            ssd_norm_g[l][None, :], w_out[l].astype(BF16), expand, w1[l], w3[l], w2[l], bn, seq)

        w_re = jnp.transpose(w_router_e[l], (1, 0, 2)).reshape(D_MODEL, N_EXPERTS)
        w_router = _pad_cols(jnp.concatenate([w_re, w_router_g[l]], axis=1), LANES)
        w_router_hi = w_router.astype(BF16)
        w_router = jnp.concatenate([w_router_hi, (w_router - w_router_hi.astype(F32)).astype(BF16)], axis=1)
        b_router = _pad_cols(jnp.concatenate([b_router_e[l].reshape(-1), b_router_g[l]])[None, :], LANES)
        x2 = _moe_final(x2, mod3, norm2_g[l][None, :], w_router, b_router, w1b, w3b, w2b,
                        final_g[None, :], seq, final_norm=(l == depth - 1))
    return x2.reshape(bn, seq, D_MODEL)
```

```python
import functools
import math

import jax
import jax.numpy as jnp
from jax import lax
from jax.experimental import pallas as pl
from jax.experimental.pallas import tpu as pltpu

F32 = jnp.float32
BF16 = jnp.bfloat16
HIGHEST = lax.Precision.HIGHEST

D_MODEL = 1024
N_MOD = 6
GMLP_WIDTH = 1024
GMLP_HEADS = 8
CHUNK = 128
SSD_STEP_ROWS = 4 * CHUNK
SSD_WIDTH = 1024
SSD_HEADS = 16
SSD_HEAD_DIM = 64
SSD_GROUPS = 2
SSD_STATE = 128
GROUP_WIDTH = SSD_WIDTH // SSD_GROUPS
CONV_WIDTH = 5
CONV_CH = SSD_WIDTH + 2 * SSD_GROUPS * SSD_STATE
N_EGROUPS = 4
EXPERTS_PER_GROUP = 8
N_EXPERTS = 32
D_EXPERT = 256
EPS = 1e-6
LOG2_E = 1.4426950408889634

LANES = 128
SUBLANES = 8
COL_U, COL_V, COL_Z, COL_XBC, COL_DT = 0, 1024, 2048, 3072, 4608
IN_COLS = 4640
IN_COLS_PAD = COL_DT + LANES
TM_PROJ = 1024
PROJ_ROWS = 512
COMBINE_ROWS = 256
ROUTER_ROWS = 512
TM_ROUTER = 2048
TM_MOE = 512
TR_EXPERT = 512
EXPERT_TILES_PER_STEP = 2
N_ASSIGN = 2
SEG_ALIGN = 2 * SUBLANES
R_LOC = N_ASSIGN * TM_MOE + N_EXPERTS * SEG_ALIGN
H_BUFS = 3
PIECE_ROWS = 128
REM_SIZES = tuple(PIECE_ROWS >> s for s in range(1, PIECE_ROWS.bit_length()) if PIECE_ROWS >> s >= SEG_ALIGN)
TOTAL_SIZES = tuple(1 << s for s in range(R_LOC.bit_length() - 1, -1, -1) if 1 << s >= SEG_ALIGN)
VMEM_LIMIT = 56 * 1024 * 1024


def _silu(v):
    return v * jax.nn.sigmoid(v)


def _gelu(v):
    return 0.5 * v * (1.0 + lax.erf(v * math.sqrt(0.5)))


def _softplus(v):
    return jnp.maximum(v, 0.0) + jnp.log1p(jnp.exp(-jnp.abs(v)))


def _rms(v):
    return v * lax.rsqrt(jnp.mean(v * v, axis=-1, keepdims=True) + EPS)


def _mod_body(c_ref, w_ref, b_ref, o_ref):
    ca = _silu(c_ref[...])
    o_ref[...] = jnp.dot(ca, w_ref[...], precision=HIGHEST, preferred_element_type=F32) + b_ref[...]


def _modulation(c, w_ada, b_ada, layer):
    bn = c.shape[0]
    return pl.pallas_call(
        _mod_body,
        grid=(N_MOD,),
        in_specs=[
            pl.BlockSpec((bn, D_MODEL), lambda j: (0, 0)),
            pl.BlockSpec((None, D_MODEL, D_MODEL), lambda j: (layer, 0, j)),
            pl.BlockSpec((1, D_MODEL), lambda j: (0, j)),
        ],
        out_specs=pl.BlockSpec((bn, D_MODEL), lambda j: (0, j)),
        out_shape=jax.ShapeDtypeStruct((bn, N_MOD * D_MODEL), F32),
        name="adaln_mod",
    )(c, w_ada, b_ada.reshape(1, -1))


def _inproj_body(x_ref, shift_ref, scale_ref, g_ref, w_ref, b_ref, lng_ref, lnb_ref, ws_ref, bs_ref,
                 og_ref, ya_ref, z_ref, xbc_ref, dt_ref, mix_scr):
    gain = g_ref[...] * (1.0 + scale_ref[...])
    for part in range(TM_PROJ // PROJ_ROWS):
        rows = slice(part * PROJ_ROWS, (part + 1) * PROJ_ROWS)
        hb = (_rms(x_ref[rows, :]) * gain + shift_ref[...]).astype(BF16)

        def proj(lo, hi, hb=hb):
            return jnp.dot(hb, w_ref[:, lo:hi], preferred_element_type=F32) + b_ref[:, lo:hi]

        z_ref[rows, :] = proj(COL_Z, COL_XBC).astype(BF16)
        xbc_ref[rows, :] = proj(COL_XBC, COL_DT)
        dt_ref[rows, :] = proj(COL_DT, IN_COLS_PAD)

        v = _gelu(proj(COL_V, COL_Z))
        mu = jnp.mean(v, axis=-1, keepdims=True)
        vc = v - mu
        var = jnp.mean(vc * vc, axis=-1, keepdims=True)
        vn = (vc * lax.rsqrt(var + EPS) * lng_ref[...] + lnb_ref[...]).astype(BF16)
        n_chunks = PROJ_ROWS // CHUNK
        for hd in range(GMLP_HEADS):
            cols = slice(hd * LANES, (hd + 1) * LANES)
            rhs = jnp.concatenate([vn[c * CHUNK:(c + 1) * CHUNK, cols] for c in range(n_chunks)], axis=1)
            res = jnp.dot(ws_ref[hd], rhs, preferred_element_type=F32)
            for c in range(n_chunks):
                mix_scr[part * PROJ_ROWS + c * CHUNK:part * PROJ_ROWS + (c + 1) * CHUNK, cols] = (
                    res[:, c * LANES:(c + 1) * LANES])
        u = _gelu(proj(COL_U, COL_V))
        out = u * (mix_scr[rows, :] + bs_ref[rows, :])
        ya_ref[rows, :] = (_rms(out) * og_ref[...]).astype(BF16)


def _inproj_gmlp(x2, mod3, norm1_g, w_in_p, b_in_p, ln_g, ln_b, w_s, bs_tile, out_g, seq):
    t = x2.shape[0]
    tiles_per_seq = seq // TM_PROJ
    row = lambda i: (i, 0)
    const2 = lambda i: (0, 0)
    return pl.pallas_call(
        _inproj_body,
        grid=(t // TM_PROJ,),
        in_specs=[
            pl.BlockSpec((TM_PROJ, D_MODEL), row),
            pl.BlockSpec((None, 1, D_MODEL), lambda i: (i // tiles_per_seq, 0, 0)),
            pl.BlockSpec((None, 1, D_MODEL), lambda i: (i // tiles_per_seq, 0, 1)),
            pl.BlockSpec((1, D_MODEL), const2),
            pl.BlockSpec((D_MODEL, IN_COLS_PAD), const2, pipeline_mode=pl.Buffered(1)),
            pl.BlockSpec((1, IN_COLS_PAD), const2),
            pl.BlockSpec((1, GMLP_WIDTH), const2),
            pl.BlockSpec((1, GMLP_WIDTH), const2),
            pl.BlockSpec((GMLP_HEADS, CHUNK, CHUNK), lambda i: (0, 0, 0)),
            pl.BlockSpec((TM_PROJ, GMLP_WIDTH), const2, pipeline_mode=pl.Buffered(1)),
            pl.BlockSpec((1, GMLP_WIDTH), const2),
        ],
        out_specs=[
            pl.BlockSpec((TM_PROJ, GMLP_WIDTH), row),
            pl.BlockSpec((TM_PROJ, SSD_WIDTH), row),
            pl.BlockSpec((TM_PROJ, CONV_CH), row),
            pl.BlockSpec((TM_PROJ, LANES), row),
        ],
        out_shape=[
            jax.ShapeDtypeStruct((t, GMLP_WIDTH), BF16),
            jax.ShapeDtypeStruct((t, SSD_WIDTH), BF16),
            jax.ShapeDtypeStruct((t, CONV_CH), F32),
            jax.ShapeDtypeStruct((t, LANES), F32),
        ],
        scratch_shapes=[pltpu.VMEM((TM_PROJ, GMLP_WIDTH), F32)],
        compiler_params=pltpu.CompilerParams(
            dimension_semantics=("arbitrary",), vmem_limit_bytes=VMEM_LIMIT),
        name="inproj_gmlp",
    )(x2, mod3, mod3, norm1_g, w_in_p, b_in_p, ln_g, ln_b, w_s, bs_tile, out_g)


def _ssd_chunk(act, dtv, a_row, expand, state, rev):
    off = SSD_HEADS if rev else 0
    row = lax.broadcasted_iota(jnp.int32, (CHUNK, CHUNK), 0)
    col = lax.broadcasted_iota(jnp.int32, (CHUNK, CHUNK), 1)
    lower = row >= col
    upper = row <= col
    keep = upper if rev else lower
    da = dtv * a_row
    cs = jnp.dot(keep.astype(F32), da, precision=HIGHEST, preferred_element_type=F32)
    cs2 = cs * LOG2_E
    col_t = cs2.T - jnp.log2(dtv.T)
    tot = cs[0:1, :] if rev else cs[CHUNK - 1:CHUNK, :]

    xs = act[:, :SSD_WIDTH]
    lane = lax.broadcasted_iota(jnp.int32, (CHUNK, LANES), 1)
    first_half = lane < SSD_HEAD_DIM
    zero = jnp.zeros((), BF16)

    stack = jnp.concatenate(
        [jnp.exp(cs), dtv * jnp.exp(tot - cs), jnp.broadcast_to(jnp.exp(tot), (SUBLANES, LANES))], axis=0)
    stack_x = jnp.dot(stack.astype(BF16), expand, preferred_element_type=F32)
    into_x = stack_x[:CHUNK]
    w_x = stack_x[CHUNK:2 * CHUNK]
    cd_x = stack_x[2 * CHUNK:2 * CHUNK + 1]
    xw = xs * w_x.astype(BF16)

    pieces = []
    for g in range(SSD_GROUPS):
        bg = act[:, SSD_WIDTH + g * SSD_STATE:SSD_WIDTH + (g + 1) * SSD_STATE]
        cg = act[:, SSD_WIDTH + SSD_GROUPS * SSD_STATE + g * SSD_STATE:
                 SSD_WIDTH + SSD_GROUPS * SSD_STATE + (g + 1) * SSD_STATE]
        scores = lax.dot_general(cg, bg, (((1,), (1,)), ((), ())), preferred_element_type=F32)
        heads_per_group = SSD_HEADS // SSD_GROUPS
        for pair in range(heads_per_group // 2):
            h0 = g * heads_per_group + 2 * pair
            xs_pair = xs[:, h0 * SSD_HEAD_DIM:(h0 + 2) * SSD_HEAD_DIM]
            y_pair = None
            for k in range(2):
                hh = off + h0 + k
                seg = cs2[:, hh:hh + 1] - col_t[hh:hh + 1, :]
                dec = jnp.exp2(jnp.where(keep, seg, -jnp.inf))
                m = (scores * dec).astype(BF16)
                rhs = jnp.where(first_half if k == 0 else jnp.logical_not(first_half), xs_pair, zero)
                part = jnp.dot(m, rhs, preferred_element_type=F32)
                y_pair = part if y_pair is None else y_pair + part
            pieces.append(y_pair)
    y_diag = jnp.concatenate(pieces, axis=1)

    y_off = []
    new_state = []
    for g in range(SSD_GROUPS):
        gcols = slice(g * GROUP_WIDTH, (g + 1) * GROUP_WIDTH)
        bg = act[:, SSD_WIDTH + g * SSD_STATE:SSD_WIDTH + (g + 1) * SSD_STATE]
        cg = act[:, SSD_WIDTH + SSD_GROUPS * SSD_STATE + g * SSD_STATE:
                 SSD_WIDTH + SSD_GROUPS * SSD_STATE + (g + 1) * SSD_STATE]
        prev = state[:, gcols]
        y_off.append(jnp.dot(cg, prev.astype(BF16), preferred_element_type=F32))
        bg_t = bg.astype(F32).T.astype(BF16)
        new = jnp.dot(bg_t, xw[:, gcols], preferred_element_type=F32)
        new_state.append(prev * cd_x[:, gcols] + new)
    y = y_diag + jnp.concatenate(y_off, axis=1) * into_x
    return y, jnp.concatenate(new_state, axis=1)


def _ssd_body(xbc_ref, xprev_ref, xnext_ref, dt_ref, z_ref, ya_ref, x_ref, gate_ref, cw_ref, cb_ref,
              dtb_ref, alog_ref, dsk_ref, ng_ref, wout_ref, exp_ref, w1_ref, w3_ref, w2_ref,
              o_ref, w1b_ref, w3b_ref, w2b_ref,
              act_scr, dts_scr, yf_scr, st_scr, ext_scr, *, n_steps):
    d = pl.program_id(1)
    c = pl.program_id(2)
    lane1 = lax.broadcasted_iota(jnp.int32, (1, LANES), 1)
    a_row = jnp.where(lane1 < 2 * SSD_HEADS, -jnp.exp(alog_ref[...]), 0.0)
    chunk_rows = [slice(j * CHUNK, (j + 1) * CHUNK) for j in range(SSD_STEP_ROWS // CHUNK)]

    @pl.when(c == 0)
    def _():
        st_scr[...] = jnp.zeros_like(st_scr)

    def scan(act, dtv, expand, rev):
        state = st_scr[...]
        ys = [None] * len(chunk_rows)
        for j in (reversed(range(len(chunk_rows))) if rev else range(len(chunk_rows))):
            ys[j], state = _ssd_chunk(act[chunk_rows[j], :], dtv[chunk_rows[j], :], a_row, expand, state, rev)
        st_scr[...] = state
        return jnp.concatenate(ys, axis=0)

    @pl.when(d == 0)
    def _forward():
        w1b_ref[...] = w1_ref[...].astype(BF16)
        w3b_ref[...] = w3_ref[...].astype(BF16)
        w2b_ref[...] = w2_ref[...].astype(BF16)
        row0 = pl.multiple_of(c * SSD_STEP_ROWS, SSD_STEP_ROWS)
        ext_scr[0:SUBLANES, :] = jnp.where(c > 0, xprev_ref[...], 0.0)
        ext_scr[SUBLANES:SUBLANES + SSD_STEP_ROWS, :] = xbc_ref[...]
        ext_scr[SUBLANES + SSD_STEP_ROWS:, :] = jnp.where(c < n_steps - 1, xnext_ref[...], 0.0)
        ext = ext_scr[...]
        n_ext = SSD_STEP_ROWS + 2 * SUBLANES
        body = slice(SUBLANES, SUBLANES + SSD_STEP_ROWS)
        down1 = pltpu.roll(ext, 1, 0)
        down2 = pltpu.roll(down1, 1, 0)
        up1 = pltpu.roll(ext, n_ext - 1, 0)
        up2 = pltpu.roll(up1, n_ext - 1, 0)
        acc = cb_ref[...]
        for k, tap in enumerate((down2, down1, ext, up1, up2)):
            acc = acc + cw_ref[k:k + 1, :] * tap[body, :]
        act = _silu(acc).astype(BF16)
        dtv = _softplus(dt_ref[...] + dtb_ref[...])
        act_scr[pl.ds(row0, SSD_STEP_ROWS), :] = act
        dts_scr[pl.ds(row0, SSD_STEP_ROWS), :] = dtv
        yf_scr[pl.ds(row0, SSD_STEP_ROWS), :] = scan(act, dtv, exp_ref[0], rev=False).astype(BF16)

    @pl.when(d == 1)
    def _backward():
        row0 = pl.multiple_of((n_steps - 1 - c) * SSD_STEP_ROWS, SSD_STEP_ROWS)
        act = act_scr[pl.ds(row0, SSD_STEP_ROWS), :]
        dtv = dts_scr[pl.ds(row0, SSD_STEP_ROWS), :]
        yb = scan(act, dtv, exp_ref[1], rev=True)
        xs = act[:, :SSD_WIDTH].astype(F32)
        y = yf_scr[pl.ds(row0, SSD_STEP_ROWS), :].astype(F32) + yb + dsk_ref[...] * xs
        y = y * _silu(z_ref[...].astype(F32))
        y = jnp.concatenate(
            [_rms(y[:, g * GROUP_WIDTH:(g + 1) * GROUP_WIDTH]) for g in range(SSD_GROUPS)], axis=1)
        y = y * ng_ref[...]
        mix = jnp.concatenate([ya_ref[...], y.astype(BF16)], axis=1)
        o = jnp.dot(mix, wout_ref[...], preferred_element_type=F32)
        o_ref[...] = x_ref[...] + gate_ref[...] * o


def _ssd_outproj(xbc, dt, z, ya, x2, mod3, conv_w8, conv_b, dtb_row, alog_row, dsk_row, norm_g, w_out_b,
                 expand, w1, w3, w2, bn, seq):
    t = x2.shape[0]
    nc = seq // SSD_STEP_ROWS
    blocks8 = SSD_STEP_ROWS // SUBLANES
    last8 = t // SUBLANES - 1

    def fwd_chunk(b, d, c):
        return b * nc + c + d * (nc - 1 - c)

    def bwd_chunk(b, d, c):
        return b * nc + nc - 1 - d * c

    const2 = lambda b, d, c: (0, 0)
    n_grid = bn * nc
    up_rows = N_EXPERTS * D_MODEL // n_grid
    down_rows = N_EXPERTS * D_EXPERT // n_grid
    assert up_rows * n_grid == N_EXPERTS * D_MODEL and up_rows % SEG_ALIGN == 0
    assert down_rows * n_grid == N_EXPERTS * D_EXPERT and down_rows % SEG_ALIGN == 0
    share = lambda b, d, c: (fwd_chunk(b, d, c), 0)
    outs = pl.pallas_call(
        functools.partial(_ssd_body, n_steps=nc),
        grid=(bn, 2, nc),
        in_specs=[
            pl.BlockSpec((SSD_STEP_ROWS, CONV_CH), lambda b, d, c: (fwd_chunk(b, d, c), 0)),
            pl.BlockSpec((SUBLANES, CONV_CH),
                         lambda b, d, c: (jnp.maximum(fwd_chunk(b, d, c) * blocks8 - 1, 0), 0)),
            pl.BlockSpec((SUBLANES, CONV_CH),
                         lambda b, d, c: (jnp.minimum((fwd_chunk(b, d, c) + 1) * blocks8, last8), 0)),
            pl.BlockSpec((SSD_STEP_ROWS, LANES), lambda b, d, c: (fwd_chunk(b, d, c), 0)),
            pl.BlockSpec((SSD_STEP_ROWS, SSD_WIDTH), lambda b, d, c: (bwd_chunk(b, d, c), 0)),
            pl.BlockSpec((SSD_STEP_ROWS, GMLP_WIDTH), lambda b, d, c: (bwd_chunk(b, d, c), 0)),
            pl.BlockSpec((SSD_STEP_ROWS, D_MODEL), lambda b, d, c: (bwd_chunk(b, d, c), 0)),
            pl.BlockSpec((None, 1, D_MODEL), lambda b, d, c: (b, 0, 2)),
            pl.BlockSpec((SUBLANES, CONV_CH), const2),
            pl.BlockSpec((1, CONV_CH), const2),
            pl.BlockSpec((1, LANES), const2),
            pl.BlockSpec((1, LANES), const2),
            pl.BlockSpec((1, SSD_WIDTH), const2),
            pl.BlockSpec((1, SSD_WIDTH), const2),
            pl.BlockSpec((GMLP_WIDTH + SSD_WIDTH, D_MODEL), const2, pipeline_mode=pl.Buffered(1)),
            pl.BlockSpec((2, LANES, SSD_WIDTH), lambda b, d, c: (0, 0, 0), pipeline_mode=pl.Buffered(1)),
            pl.BlockSpec((up_rows, D_EXPERT), share),
            pl.BlockSpec((up_rows, D_EXPERT), share),
            pl.BlockSpec((down_rows, D_MODEL), share),
        ],
        out_specs=[
            pl.BlockSpec((SSD_STEP_ROWS, D_MODEL), lambda b, d, c: (bwd_chunk(b, d, c), 0)),
            pl.BlockSpec((up_rows, D_EXPERT), share),
            pl.BlockSpec((up_rows, D_EXPERT), share),
            pl.BlockSpec((down_rows, D_MODEL), share),
        ],
        out_shape=[
            jax.ShapeDtypeStruct((t, D_MODEL), F32),
            jax.ShapeDtypeStruct((N_EXPERTS * D_MODEL, D_EXPERT), BF16),
            jax.ShapeDtypeStruct((N_EXPERTS * D_MODEL, D_EXPERT), BF16),
            jax.ShapeDtypeStruct((N_EXPERTS * D_EXPERT, D_MODEL), BF16),
        ],
        scratch_shapes=[
            pltpu.VMEM((seq, CONV_CH), BF16),
            pltpu.VMEM((seq, LANES), F32),
            pltpu.VMEM((seq, SSD_WIDTH), BF16),
            pltpu.VMEM((SSD_STATE, SSD_WIDTH), F32),
            pltpu.VMEM((SSD_STEP_ROWS + 2 * SUBLANES, CONV_CH), F32),
        ],
        compiler_params=pltpu.CompilerParams(
            dimension_semantics=("arbitrary", "arbitrary", "arbitrary"), vmem_limit_bytes=VMEM_LIMIT),
        name="ssd_outproj",
    )(xbc, xbc, xbc, dt, z, ya, x2, mod3, conv_w8, conv_b, dtb_row, alog_row, dsk_row, norm_g, w_out_b,
      expand, w1.reshape(-1, D_EXPERT), w3.reshape(-1, D_EXPERT), w2.reshape(-1, D_MODEL))
    x1, w1b, w3b, w2b = outs
    return (x1, w1b.reshape(N_EXPERTS, D_MODEL, D_EXPERT), w3b.reshape(N_EXPERTS, D_MODEL, D_EXPERT),
            w2b.reshape(N_EXPERTS, D_EXPERT, D_MODEL))


def _route(logits):
    lane = lax.broadcasted_iota(jnp.int32, logits.shape, 1)
    big = jnp.int32(LANES)
    neg = -jnp.inf
    gmask = (lane >= N_EXPERTS) & (lane < N_EXPERTS + N_EGROUPS)
    gl = jnp.where(gmask, logits, neg)
    gmax = jnp.max(gl, axis=-1, keepdims=True)
    gidx = jnp.min(jnp.where(gl == gmax, lane, big), axis=-1, keepdims=True) - N_EXPERTS
    p_g = 1.0 / jnp.sum(jnp.where(gmask, jnp.exp(gl - gmax), 0.0), axis=-1, keepdims=True)
    lo = gidx * EXPERTS_PER_GROUP
    emask = (lane >= lo) & (lane < lo + EXPERTS_PER_GROUP)
    el = jnp.where(emask, logits, neg)
    v1 = jnp.max(el, axis=-1, keepdims=True)
    i1 = jnp.min(jnp.where(el == v1, lane, big), axis=-1, keepdims=True)
    el2 = jnp.where(lane == i1, neg, el)
    v2 = jnp.max(el2, axis=-1, keepdims=True)
    i2 = jnp.min(jnp.where(el2 == v2, lane, big), axis=-1, keepdims=True)
    e2 = jnp.exp(v2 - v1)
    den = 1.0 + e2
    w1 = p_g / den
    w2 = p_g * e2 / den
    return i1, i2, w1, w2


def _router_body(x_ref, shift_ref, scale_ref, g2_ref, wr_ref, br_ref, h_ref, rt_ref, cnt_ref):
    counts = jnp.zeros((1, LANES), F32)
    gain = g2_ref[...] * (1.0 + scale_ref[...])
    groups_per_count = TM_MOE // ROUTER_ROWS
    for part in range(TM_ROUTER // ROUTER_ROWS):
        rows = slice(part * ROUTER_ROWS, (part + 1) * ROUTER_ROWS)
        h = _rms(x_ref[rows, :]) * gain + shift_ref[...]
        hb = h.astype(BF16)
        h_ref[rows, :] = hb
        h_lo = (h - hb.astype(F32)).astype(BF16)
        both = jnp.dot(hb, wr_ref[...], preferred_element_type=F32)
        logits = (both[:, :LANES] + both[:, LANES:]
                  + jnp.dot(h_lo, wr_ref[:, :LANES], preferred_element_type=F32)) + br_ref[...]
        i1, i2, w1, w2 = _route(logits)
        lane = lax.broadcasted_iota(jnp.int32, (ROUTER_ROWS, LANES), 1)
        rt_ref[rows, :] = jnp.where(lane == 0, i1.astype(F32), jnp.where(lane == 1, i2.astype(F32),
                                    jnp.where(lane == 2, w1, jnp.where(lane == 3, w2, 0.0))))
        mask = jnp.where(lane == i1, 1.0, jnp.where(lane == i2, 1.0, 0.0))
        counts = counts + jnp.sum(mask, axis=0, keepdims=True)
        if (part + 1) % groups_per_count == 0:
            tile = part // groups_per_count
            cnt_ref[tile * SUBLANES:(tile + 1) * SUBLANES, :] = jnp.broadcast_to(counts, (SUBLANES, LANES))
            counts = jnp.zeros((1, LANES), F32)


def _router(x1, mod3, norm2_g, w_router, b_router, seq):
    t = x1.shape[0]
    tiles_per_seq = seq // TM_ROUTER
    n_tiles = t // TM_MOE
    count_rows = TM_ROUTER // TM_MOE * SUBLANES
    row = lambda i: (i, 0)
    const2 = lambda i: (0, 0)
    return pl.pallas_call(
        _router_body,
        grid=(t // TM_ROUTER,),
        in_specs=[
            pl.BlockSpec((TM_ROUTER, D_MODEL), row),
            pl.BlockSpec((None, 1, D_MODEL), lambda i: (i // tiles_per_seq, 0, 3)),
            pl.BlockSpec((None, 1, D_MODEL), lambda i: (i // tiles_per_seq, 0, 4)),
            pl.BlockSpec((1, D_MODEL), const2),
            pl.BlockSpec((D_MODEL, 2 * LANES), const2),
            pl.BlockSpec((1, LANES), const2),
        ],
        out_specs=[
            pl.BlockSpec((TM_ROUTER, D_MODEL), row),
            pl.BlockSpec((TM_ROUTER, LANES), row),
            pl.BlockSpec((count_rows, LANES), row),
        ],
        out_shape=[
            jax.ShapeDtypeStruct((t, D_MODEL), BF16),
            jax.ShapeDtypeStruct((t, LANES), F32),
            jax.ShapeDtypeStruct((n_tiles * SUBLANES, LANES), F32),
        ],
        compiler_params=pltpu.CompilerParams(
            dimension_semantics=("arbitrary",), vmem_limit_bytes=VMEM_LIMIT),
        name="moe_router",
    )(x1, mod3, mod3, norm2_g, w_router, b_router)


def _row_copy(vmem_buf, v_start, hbm_ref, g_start, size, sem, to_hbm):
    v = vmem_buf.at[pl.ds(v_start, size), :]
    g = hbm_ref.at[pl.ds(g_start, size), :]
    return pltpu.make_async_copy(v, g, sem) if to_hbm else pltpu.make_async_copy(g, v, sem)


def _copy_rows(n, vmem_buf, v0, hbm_ref, g0, sem, *, to_hbm, wait):
    def run(v_off, g_off, size):
        v_start = 0 if v0 is None else pl.multiple_of(v0 + v_off, SEG_ALIGN)
        cp = _row_copy(vmem_buf, v_start, hbm_ref, pl.multiple_of(g0 + g_off, SEG_ALIGN), size, sem, to_hbm)
        if wait:
            cp.wait()
        else:
            cp.start()

    n_big = n // PIECE_ROWS

    def big_piece(k, carry):
        run(k * PIECE_ROWS, k * PIECE_ROWS, PIECE_ROWS)
        return carry

    lax.fori_loop(0, n_big, big_piece, 0)
    base = n_big * PIECE_ROWS
    rem = n - base
    for size in REM_SIZES:
        done = base + (rem // (2 * size)) * (2 * size)

        @pl.when((rem & size) != 0)
        def _():
            run(done, done, size)


def _segment_copies(seg_ref, loc_ref, goff_ref, tile, vmem_buf, hbm_ref, sem, *, to_hbm):
    base = tile * N_EXPERTS

    def per_expert(e, carry):
        _copy_rows(seg_ref[base + e], vmem_buf, loc_ref[base + e], hbm_ref, goff_ref[base + e], sem,
                   to_hbm=to_hbm, wait=False)
        return carry

    lax.fori_loop(0, N_EXPERTS, per_expert, 0)


def _tile_rows(seg_ref, loc_ref, tile):
    last = tile * N_EXPERTS + N_EXPERTS - 1
    return loc_ref[last] + seg_ref[last]


def _segment_wait(seg_ref, loc_ref, tile, vmem_buf, hbm_ref, sem, *, to_hbm):
    total = _tile_rows(seg_ref, loc_ref, tile)
    for size in TOTAL_SIZES:
        @pl.when((total & size) != 0)
        def _():
            _row_copy(vmem_buf, 0, hbm_ref, 0, size, sem, to_hbm).wait()


def _dispatch_body(seg_ref, loc_ref, goff_ref, fill_ref, h_hbm, rt_ref, locrow_ref, pos_ref, xs_hbm, buf, sem,
                   hbuf, hsem, *, n_tiles, n_row_tiles):
    i = pl.program_id(0)
    slot = i % 2

    def token_fetch(tile):
        k = tile % H_BUFS
        rows = pl.ds(pl.multiple_of(tile * TM_MOE, TM_MOE), TM_MOE)
        return pltpu.make_async_copy(h_hbm.at[rows, :], hbuf.at[k], hsem.at[k])

    @pl.when(i == 0)
    def _():
        for tile in range(min(H_BUFS - 1, n_tiles)):
            token_fetch(tile).start()

    @pl.when(i + H_BUFS - 1 < n_tiles)
    def _():
        token_fetch(i + H_BUFS - 1).start()

    start = functools.partial(_segment_copies, seg_ref, loc_ref, goff_ref, hbm_ref=xs_hbm, to_hbm=True)
    wait = functools.partial(_segment_wait, seg_ref, loc_ref, hbm_ref=xs_hbm, to_hbm=True)

    @pl.when(i >= 2)
    def _():
        wait(i - 2, buf.at[slot], sem=sem.at[slot])

    rt = rt_ref[...]
    lane = lax.broadcasted_iota(jnp.int32, (TM_MOE, LANES), 1)
    lanef = lane.astype(F32)
    e1 = rt[:, 0:1]
    e2 = rt[:, 1:2]
    mask = jnp.where(lanef == e1, 1.0, jnp.where(lanef == e2, 1.0, 0.0))
    r = lax.broadcasted_iota(jnp.int32, (TM_MOE, TM_MOE), 0)
    cc = lax.broadcasted_iota(jnp.int32, (TM_MOE, TM_MOE), 1)
    strict = jnp.where(r > cc, 1.0, 0.0).astype(BF16)
    rank = jnp.dot(strict, mask.astype(BF16), preferred_element_type=F32)
    posall = rank + locrow_ref[...]
    pos1 = jnp.sum(jnp.where(lanef == e1, posall, 0.0), axis=-1, keepdims=True)
    pos2 = jnp.sum(jnp.where(lanef == e2, posall, 0.0), axis=-1, keepdims=True)
    pp = jnp.where(lane == 0, pos1, jnp.where(lane == 1, pos2, rt))
    pos_ref[...] = pp

    slots_t = [pp[k * LANES:(k + 1) * LANES, :].T for k in range(TM_MOE // LANES)]

    rowp = lax.broadcasted_iota(jnp.int32, (R_LOC, LANES), 0).astype(F32)
    perm = jnp.concatenate(
        [jnp.where(rowp == st[0:1, :], 1.0, jnp.where(rowp == st[1:2, :], 1.0, 0.0)).astype(BF16)
         for st in slots_t], axis=1)
    token_fetch(i).wait()
    buf[slot] = jnp.dot(perm, hbuf[i % H_BUFS], preferred_element_type=F32).astype(BF16)
    start(i, buf.at[slot], sem=sem.at[slot])

    @pl.when(i == n_tiles - 1)
    def _():
        if n_tiles > 1:
            wait(i - 1, buf.at[1 - slot], sem=sem.at[1 - slot])
        wait(i, buf.at[slot], sem=sem.at[slot])
        zsrc = buf.at[1 - slot]
        zsem = sem.at[1 - slot]
        zsrc[0:TR_EXPERT, :] = jnp.zeros((TR_EXPERT, D_MODEL), BF16)
        for waiting in (False, True):
            def per_expert(e, carry):
                _copy_rows(fill_ref[N_EXPERTS + e], zsrc, None, xs_hbm, fill_ref[e], zsem,
                           to_hbm=True, wait=waiting)
                return carry

            def per_row_tile(k, carry):
                cp = _row_copy(zsrc, 0, xs_hbm, pl.multiple_of(k * TR_EXPERT, TR_EXPERT), TR_EXPERT, zsem, True)
                if waiting:
                    cp.wait()
                else:
                    cp.start()
                return carry

            lax.fori_loop(0, N_EXPERTS, per_expert, 0)
            lax.fori_loop(fill_ref[2 * N_EXPERTS], n_row_tiles, per_row_tile, 0)


def _dispatch(h2, rt, seg, loc, goff, fill, locrow, n_row_tiles):
    t = h2.shape[0]
    n_tiles = t // TM_MOE
    row = lambda i, *_: (i, 0)
    grid_spec = pltpu.PrefetchScalarGridSpec(
        num_scalar_prefetch=4,
        grid=(n_tiles,),
        in_specs=[
            pl.BlockSpec(memory_space=pl.ANY),
            pl.BlockSpec((TM_MOE, LANES), row),
            pl.BlockSpec((None, 1, LANES), lambda i, *_: (i, 0, 0)),
        ],
        out_specs=[
            pl.BlockSpec((TM_MOE, LANES), row),
            pl.BlockSpec(memory_space=pl.ANY),
        ],
        scratch_shapes=[
            pltpu.VMEM((2, R_LOC, D_MODEL), BF16),
            pltpu.SemaphoreType.DMA((2,)),
            pltpu.VMEM((H_BUFS, TM_MOE, D_MODEL), BF16),
            pltpu.SemaphoreType.DMA((H_BUFS,)),
        ],
    )
    return pl.pallas_call(
        functools.partial(_dispatch_body, n_tiles=n_tiles, n_row_tiles=n_row_tiles),
        grid_spec=grid_spec,
        out_shape=[
            jax.ShapeDtypeStruct((t, LANES), F32),
            jax.ShapeDtypeStruct((n_row_tiles * TR_EXPERT, D_MODEL), BF16),
        ],
        compiler_params=pltpu.CompilerParams(
            dimension_semantics=("arbitrary",), vmem_limit_bytes=VMEM_LIMIT),
        name="moe_dispatch",
    )(seg, loc, goff, fill, h2, rt, locrow)


def _expert_body(te_ref, tw_ref, na_ref, xs_ref, w1a_ref, w3a_ref, w2a_ref, w1b_ref, w3b_ref, w2b_ref, y_ref):
    del tw_ref
    r = pl.program_id(0)
    n_active = na_ref[0]
    weights = ((w1a_ref, w3a_ref, w2a_ref), (w1b_ref, w3b_ref, w2b_ref))

    def swiglu(half, rows=None):
        w1_ref, w3_ref, w2_ref = weights[half]
        rows = slice(half * TR_EXPERT, (half + 1) * TR_EXPERT) if rows is None else rows
        x = xs_ref[rows, :]
        a = _silu(jnp.dot(x, w1_ref[...], preferred_element_type=F32))
        a = a * jnp.dot(x, w3_ref[...], preferred_element_type=F32)
        y_ref[rows, :] = jnp.dot(a.astype(BF16), w2_ref[...], preferred_element_type=F32).astype(BF16)

    first = EXPERT_TILES_PER_STEP * r
    both = first + 1 < n_active
    same = te_ref[first] == te_ref[first + 1]

    @pl.when(both & same)
    def _():
        swiglu(0, slice(0, EXPERT_TILES_PER_STEP * TR_EXPERT))

    @pl.when(both & jnp.logical_not(same))
    def _():
        swiglu(0)
        swiglu(1)

    @pl.when(first + 1 == n_active)
    def _():
        swiglu(0)
        y_ref[TR_EXPERT:, :] = jnp.zeros((TR_EXPERT, D_MODEL), BF16)

    @pl.when(first >= n_active)
    def _():
        y_ref[...] = jnp.zeros_like(y_ref)


def _experts(xs, tile_expert, tile_weights, n_active, w1b, w3b, w2b):
    n_rows = xs.shape[0]
    step_rows = EXPERT_TILES_PER_STEP * TR_EXPERT
    x_block = lambda r, te, tw, na: (jnp.minimum(r, (na[0] - 1) // EXPERT_TILES_PER_STEP), 0)
    w_specs = []
    for half in range(EXPERT_TILES_PER_STEP):
        w_block = lambda r, te, tw, na, half=half: (tw[EXPERT_TILES_PER_STEP * r + half], 0, 0)
        w_specs += [pl.BlockSpec((None, D_MODEL, D_EXPERT), w_block),
                    pl.BlockSpec((None, D_MODEL, D_EXPERT), w_block),
                    pl.BlockSpec((None, D_EXPERT, D_MODEL), w_block)]
    grid_spec = pltpu.PrefetchScalarGridSpec(
        num_scalar_prefetch=3,
        grid=(n_rows // step_rows,),
        in_specs=[pl.BlockSpec((step_rows, D_MODEL), x_block)] + w_specs,
        out_specs=pl.BlockSpec((step_rows, D_MODEL), lambda r, te, tw, na: (r, 0)),
    )
    return pl.pallas_call(
        _expert_body,
        grid_spec=grid_spec,
        out_shape=jax.ShapeDtypeStruct((n_rows, D_MODEL), BF16),
        compiler_params=pltpu.CompilerParams(
            dimension_semantics=("arbitrary",), vmem_limit_bytes=VMEM_LIMIT),
        name="moe_experts",
    )(tile_expert, tile_weights, n_active, xs, w1b, w3b, w2b, w1b, w3b, w2b)


def _combine_body(seg_ref, loc_ref, goff_ref, y_hbm, pos_ref, x_ref, gate_ref, fg_ref, o_ref, buf, sem, *,
                  n_tiles, final_norm):
    i = pl.program_id(0)
    slot = i % 2
    start = functools.partial(_segment_copies, seg_ref, loc_ref, goff_ref, hbm_ref=y_hbm, to_hbm=False)

    @pl.when(i == 0)
    def _():
        buf[...] = jnp.zeros_like(buf)
        start(0, buf.at[0], sem=sem.at[0])

    @pl.when(i + 1 < n_tiles)
    def _():
        start(i + 1, buf.at[1 - slot], sem=sem.at[1 - slot])

    _segment_wait(seg_ref, loc_ref, i, buf.at[slot], y_hbm, sem.at[slot], to_hbm=False)
    for part in range(TM_MOE // COMBINE_ROWS):
        rows = slice(part * COMBINE_ROWS, (part + 1) * COMBINE_ROWS)
        pp = pos_ref[rows, :]
        colp = lax.broadcasted_iota(jnp.int32, (COMBINE_ROWS, R_LOC), 1).astype(F32)
        wc = jnp.where(colp == pp[:, 0:1], pp[:, 2:3], jnp.where(colp == pp[:, 1:2], pp[:, 3:4], 0.0))
        moe = jnp.dot(wc.astype(BF16), buf[slot], preferred_element_type=F32)
        x2 = x_ref[rows, :] + gate_ref[...] * moe
        o_ref[rows, :] = _rms(x2) * fg_ref[...] if final_norm else x2


def _combine(y, pos, x1, mod3, final_g, seg, loc, goff, seq, final_norm):
    t = x1.shape[0]
    n_tiles = t // TM_MOE
    tiles_per_seq = seq // TM_MOE
    row = lambda i, *_: (i, 0)
    grid_spec = pltpu.PrefetchScalarGridSpec(
        num_scalar_prefetch=3,
        grid=(n_tiles,),
        in_specs=[
            pl.BlockSpec(memory_space=pl.ANY),
            pl.BlockSpec((TM_MOE, LANES), row),
            pl.BlockSpec((TM_MOE, D_MODEL), row),
            pl.BlockSpec((None, 1, D_MODEL), lambda i, *_: (i // tiles_per_seq, 0, 5)),
            pl.BlockSpec((1, D_MODEL), lambda i, *_: (0, 0)),
        ],
        out_specs=pl.BlockSpec((TM_MOE, D_MODEL), row),
        scratch_shapes=[
            pltpu.VMEM((2, R_LOC, D_MODEL), BF16),
            pltpu.SemaphoreType.DMA((2,)),
        ],
    )
    return pl.pallas_call(
        functools.partial(_combine_body, n_tiles=n_tiles, final_norm=final_norm),
        grid_spec=grid_spec,
        out_shape=jax.ShapeDtypeStruct((t, D_MODEL), F32),
        compiler_params=pltpu.CompilerParams(
            dimension_semantics=("arbitrary",), vmem_limit_bytes=VMEM_LIMIT),
        name="moe_combine",
    )(seg, loc, goff, y, pos, x1, mod3, final_g)


def _moe_plan(counts, n_tiles, n_row_tiles):
    cnt = counts.reshape(n_tiles, SUBLANES, LANES)[:, 0, :N_EXPERTS].astype(jnp.int32)
    seg = (cnt + SEG_ALIGN - 1) // SEG_ALIGN * SEG_ALIGN
    before_e = jnp.arange(N_EXPERTS)[:, None] < jnp.arange(N_EXPERTS)[None, :]
    before_t = jnp.arange(n_tiles)[:, None] > jnp.arange(n_tiles)[None, :]
    loc = jnp.sum(jnp.where(before_e[None], seg[:, :, None], 0), axis=1)
    tot = jnp.sum(seg, axis=0)
    region = (tot + TR_EXPERT - 1) // TR_EXPERT * TR_EXPERT
    gstart = jnp.sum(jnp.where(before_e, region[:, None], 0), axis=0)
    gend = gstart + region
    goff = gstart[None, :] + jnp.sum(jnp.where(before_t[:, :, None], seg[None], 0), axis=1)
    n_active = gend[-1] // TR_EXPERT
    tile_row0 = jnp.arange(n_row_tiles, dtype=jnp.int32) * TR_EXPERT
    last_row0 = (n_active - 1) * TR_EXPERT
    te = jnp.sum(gend[None, :] <= jnp.minimum(tile_row0, last_row0)[:, None], axis=1).astype(jnp.int32)
    te = jnp.minimum(te, N_EXPERTS - 1)
    te2 = te.reshape(-1, EXPERT_TILES_PER_STEP)
    cand = jnp.where(te2[:, 1] != te2[:, 0], te2[:, 1], 0)
    steps = jnp.arange(te2.shape[0])
    held = jnp.max(jnp.where(steps[None, :] <= steps[:, None], cand[None, :], 0), axis=1)
    tw = jnp.stack([te2[:, 0], held], axis=1).reshape(-1).astype(jnp.int32)
    locrow = _pad_cols(loc.astype(F32), LANES).reshape(n_tiles, 1, LANES)
    n_active = n_active.reshape(1).astype(jnp.int32)
    fill = jnp.concatenate([gstart + tot, region - tot, n_active]).astype(jnp.int32)
    return (seg.reshape(-1), loc.reshape(-1).astype(jnp.int32), goff.reshape(-1).astype(jnp.int32), fill,
            locrow, te, tw, n_active)


def _moe_final(x1, mod3, norm2_g, w_router, b_router, w1b, w3b, w2b, final_g, seq, final_norm):
    t = x1.shape[0]
    n_tiles = t // TM_MOE
    max_rows = N_ASSIGN * t + (SEG_ALIGN - 1) * N_EXPERTS * n_tiles + N_EXPERTS * (TR_EXPERT - SEG_ALIGN)
    step_rows = EXPERT_TILES_PER_STEP * TR_EXPERT
    n_row_tiles = -(-max_rows // step_rows) * EXPERT_TILES_PER_STEP
    h2, rt, counts = _router(x1, mod3, norm2_g, w_router, b_router, seq)
    seg, loc, goff, fill, locrow, te, tw, n_active = _moe_plan(counts, n_tiles, n_row_tiles)
    pos, xs = _dispatch(h2, rt, seg, loc, goff, fill, locrow, n_row_tiles)
    y = _experts(xs, te, tw, n_active, w1b, w3b, w2b)
    return _combine(y, pos, x1, mod3, final_g, seg, loc, goff, seq, final_norm)


def _pad_cols(a, width):
    return jnp.pad(a, ((0, 0), (0, width - a.shape[1])))


def kernel(x, c, w_ada, b_ada, norm1_g, w_in, b_in, gmlp_ln_g, gmlp_ln_b, gmlp_w_s, gmlp_b_s, gmlp_out_g, conv_w, conv_b, a_log_f, a_log_b, dt_bias_f, dt_bias_b, d_skip, ssd_norm_g, w_out, norm2_g, w_router_g, b_router_g, w_router_e, b_router_e, w1, w3, w2, final_g):
    bn, seq, _ = x.shape
    depth = w_ada.shape[0]
    x2 = x.reshape(bn * seq, D_MODEL)

    head_rows = jnp.arange(LANES)[:, None]
    head_cols = jnp.arange(SSD_WIDTH)[None, :] // SSD_HEAD_DIM
    expand = jnp.stack([head_rows == head_cols, head_rows == head_cols + SSD_HEADS]).astype(BF16)

    for l in range(depth):
        mod3 = _modulation(c, w_ada, b_ada[l], l).reshape(bn, 1, N_MOD * D_MODEL)

        w_in_p = _pad_cols(w_in[l].astype(BF16), IN_COLS_PAD)
        b_in_p = _pad_cols(b_in[l][None, :], IN_COLS_PAD)
        bs = jnp.repeat(gmlp_b_s[l].T, LANES, axis=1)
        bs_tile = jnp.tile(bs, (TM_PROJ // CHUNK, 1))
        ya, z, xbc, dt = _inproj_gmlp(
            x2, mod3, norm1_g[l][None, :], w_in_p, b_in_p, gmlp_ln_g[l][None, :], gmlp_ln_b[l][None, :],
            gmlp_w_s[l].astype(BF16), bs_tile, gmlp_out_g[l][None, :], seq)

        conv_w8 = jnp.pad(conv_w[l], ((0, SUBLANES - CONV_WIDTH), (0, 0)))
        dtb_row = _pad_cols(jnp.concatenate([dt_bias_f[l], dt_bias_b[l]])[None, :], LANES)
        alog_row = _pad_cols(jnp.concatenate([a_log_f[l], a_log_b[l]])[None, :], LANES)
        dsk_row = jnp.repeat(d_skip[l], SSD_HEAD_DIM)[None, :]
        x2, w1b, w3b, w2b = _ssd_outproj(
            xbc, dt, z, ya, x2, mod3, conv_w8, conv_b[l][None, :], dtb_row, alog_row, dsk_row,
            ssd_norm_g[l][None, :], w_out[l].astype(BF16), expand, w1[l], w3[l], w2[l], bn, seq)

        w_re = jnp.transpose(w_router_e[l], (1, 0, 2)).reshape(D_MODEL, N_EXPERTS)
        w_router = _pad_cols(jnp.concatenate([w_re, w_router_g[l]], axis=1), LANES)
        w_router_hi = w_router.astype(BF16)
        w_router = jnp.concatenate([w_router_hi, (w_router - w_router_hi.astype(F32)).astype(BF16)], axis=1)
        b_router = _pad_cols(jnp.concatenate([b_router_e[l].reshape(-1), b_router_g[l]])[None, :], LANES)
        x2 = _moe_final(x2, mod3, norm2_g[l][None, :], w_router, b_router, w1b, w3b, w2b,
                        final_g[None, :], seq, final_norm=(l == depth - 1))
    return x2.reshape(bn, seq, D_MODEL)
```
